```python
import jax, jax.numpy as jnp
from jax import lax
import numpy as np

D_MODEL = 1024
BATCH = 16
SEQ = 256
DEPTH = 2
DEC_BATCH = 2
DEC_SEQ = 2048
PAST_LEN = 256

GRID_W = 64
MLA_HEADS = 6
MLA_Q_RANK = 384
MLA_KV_RANK = 256
MLA_NOPE_DIM = 64
MLA_ROPE_DIM = 32
MLA_V_DIM = 64
MLA_QK_DIM = MLA_NOPE_DIM + MLA_ROPE_DIM
MLA_WIDTH = MLA_HEADS * MLA_V_DIM
FNET_GROUPS = 4
FNET_GROUP_DIM = 64
FNET_WIDTH = FNET_GROUPS * FNET_GROUP_DIM
GQA_HEADS = 6
GQA_KV_HEADS = 2
GQA_HEAD_DIM = 64
GQA_GROUP = GQA_HEADS // GQA_KV_HEADS
GQA_WIDTH = GQA_HEADS * GQA_HEAD_DIM
MIX_WIDTH = MLA_WIDTH + FNET_WIDTH + GQA_WIDTH
IN_SIZES = (MLA_Q_RANK, MLA_KV_RANK, MLA_ROPE_DIM, FNET_WIDTH,
            GQA_WIDTH, GQA_KV_HEADS * GQA_HEAD_DIM, GQA_KV_HEADS * GQA_HEAD_DIM)
IN_WIDTH = sum(IN_SIZES)
D_FF = -(-8 * D_MODEL // (3 * 256)) * 256
ROPE_THETA = 10000.0
EPS = 1e-6
Q_BLOCK = 128

kernel_name = "hybrid_mla_fnet_gqa_diffusion_step"


def rms_norm(x, g):
    xf = x.astype(jnp.float32)
    y = xf * lax.rsqrt(jnp.mean(xf * xf, axis=-1, keepdims=True) + EPS)
    return (y * g.astype(jnp.float32)).astype(x.dtype)


def grid_angles(n_tokens, rot_dim):
    rows = n_tokens // GRID_W
    row = jnp.repeat(jnp.arange(rows, dtype=jnp.float32), GRID_W)
    col = jnp.tile(jnp.arange(GRID_W, dtype=jnp.float32), rows)
    n_axis = rot_dim // 4
    inv = ROPE_THETA ** (-jnp.arange(n_axis, dtype=jnp.float32) / n_axis)
    return jnp.concatenate([row[:, None] * inv, col[:, None] * inv], axis=-1)


def apply_rope(x, ang):
    half = x.shape[-1] // 2
    cos = jnp.cos(ang)[None, :, None, :]
    sin = jnp.sin(ang)[None, :, None, :]
    x1 = x[..., :half].astype(jnp.float32)
    x2 = x[..., half:].astype(jnp.float32)
    return jnp.concatenate([x1 * cos - x2 * sin, x2 * cos + x1 * sin], axis=-1).astype(x.dtype)


def block_attention(q, k, v, scale):
    b, sq = q.shape[:2]
    nb = sq // Q_BLOCK
    qb = q.reshape((b, nb, Q_BLOCK) + q.shape[2:]).swapaxes(0, 1)

    def one_block(qi):
        s = jnp.einsum('bqhgd,bkhd->bhgqk', qi, k, preferred_element_type=jnp.float32) * scale
        p = jax.nn.softmax(s, axis=-1)
        return jnp.einsum('bhgqk,bkhe->bqhge', p.astype(v.dtype), v)

    out = lax.map(one_block, qb)
    return out.swapaxes(0, 1).reshape((b, sq) + out.shape[3:])


def token_mixers(h, w_in, g_q_a, w_q_up, g_kv_a, w_kv_up, g_q_head, g_k_head, w_out, ctx):
    b, s, _ = h.shape
    proj = h @ w_in
    offs = [int(o) for o in np.cumsum(IN_SIZES)[:-1]]
    cq, ckv, krope, u, gq, gk, gv = jnp.split(proj, offs, axis=-1)

    q_mla = (rms_norm(cq, g_q_a) @ w_q_up).reshape(b, s, MLA_HEADS, MLA_QK_DIM)
    q_nope, q_rope = q_mla[..., :MLA_NOPE_DIM], q_mla[..., MLA_NOPE_DIM:]
    ckv = rms_norm(ckv, g_kv_a)
    krope = krope[:, :, None, :]
    gq = rms_norm(gq.reshape(b, s, GQA_HEADS, GQA_HEAD_DIM), g_q_head)
    gk = rms_norm(gk.reshape(b, s, GQA_KV_HEADS, GQA_HEAD_DIM), g_k_head)
    gv = gv.reshape(b, s, GQA_KV_HEADS, GQA_HEAD_DIM)

    if ctx is None:
        new_ctx = (ckv, krope[:, :, 0, :], gk, gv)
        ckv_all, krope_all, gk_all, gv_all = ckv, krope, gk, gv
    else:
        new_ctx = None
        ang_mla = grid_angles(s, MLA_ROPE_DIM)
        ang_gqa = grid_angles(s, GQA_HEAD_DIM)
        q_rope = apply_rope(q_rope, ang_mla)
        krope_l = apply_rope(krope, ang_mla)
        gq = apply_rope(gq, ang_gqa)
        gk_l = apply_rope(gk, ang_gqa)
        c_ckv, c_krope, c_gk, c_gv = ctx
        ckv_all = jnp.concatenate([c_ckv.astype(ckv.dtype), ckv], axis=1)
        krope_all = jnp.concatenate([c_krope[:, :, None, :].astype(krope.dtype), krope_l], axis=1)
        gk_all = jnp.concatenate([c_gk.astype(gk.dtype), gk_l], axis=1)
        gv_all = jnp.concatenate([c_gv.astype(gv.dtype), gv], axis=1)

    sk = ckv_all.shape[1]
    kv = (ckv_all @ w_kv_up).reshape(b, sk, MLA_HEADS, MLA_NOPE_DIM + MLA_V_DIM)
    k_nope, v_mla = kv[..., :MLA_NOPE_DIM], kv[..., MLA_NOPE_DIM:]
    k_mla = jnp.concatenate(
        [k_nope, jnp.broadcast_to(krope_all, (b, sk, MLA_HEADS, MLA_ROPE_DIM))], axis=-1)
    q_full = jnp.concatenate([q_nope, q_rope], axis=-1)[:, :, :, None, :]
    mla_out = block_attention(q_full, k_mla, v_mla, MLA_QK_DIM ** -0.5).reshape(b, s, MLA_WIDTH)

    uf = u.reshape(b, s, FNET_GROUPS, FNET_GROUP_DIM).astype(jnp.float32)
    fnet_out = jnp.fft.fft2(uf, axes=(1, 3), norm='ortho').real.astype(h.dtype).reshape(b, s, FNET_WIDTH)

    gq5 = gq.reshape(b, s, GQA_KV_HEADS, GQA_GROUP, GQA_HEAD_DIM)
    gqa_out = block_attention(gq5, gk_all, gv_all, GQA_HEAD_DIM ** -0.5).reshape(b, s, GQA_WIDTH)

    out = jnp.concatenate([mla_out, fnet_out, gqa_out], axis=-1) @ w_out
    return out, new_ctx


def trunk_layer(x, mod, g1, g2, w_in, g_q_a, w_q_up, g_kv_a, w_kv_up, g_q_head, g_k_head,
                w_out, w_gate, w_up, w_down, ctx):
    shift1, scale1, gate1, shift2, scale2, gate2 = jnp.split(mod.astype(x.dtype), 6, axis=-1)
    h = rms_norm(x, g1) * (1 + scale1) + shift1
    mix, new_ctx = token_mixers(h, w_in, g_q_a, w_q_up, g_kv_a, w_kv_up, g_q_head, g_k_head,
                                w_out, ctx)
    x = x + gate1 * mix
    h = rms_norm(x, g2) * (1 + scale2) + shift2
    x = x + gate2 * ((jax.nn.silu(h @ w_gate) * (h @ w_up)) @ w_down)
    return x, new_ctx


def setup_inputs(seed: int = 0) -> dict:
    key = jax.random.key(seed)
    ks = jax.random.split(key, 26)

    def nrm(k, shape, scale):
        return jax.random.normal(k, shape, jnp.float32) * scale

    def gain(k, shape):
        return 1.0 + 0.05 * jax.random.normal(k, shape, jnp.float32)

    return {
        "x_prompt": nrm(ks[0], (BATCH, SEQ, D_MODEL), 1.0),
        "x_sample": nrm(ks[1], (DEC_BATCH, DEC_SEQ, D_MODEL), 1.0),
        "cache_mla_ckv": nrm(ks[2], (DEC_BATCH, DEPTH, PAST_LEN, MLA_KV_RANK), 1.0),
        "cache_mla_krope": nrm(ks[3], (DEC_BATCH, DEPTH, PAST_LEN, MLA_ROPE_DIM), 1.0),
        "cache_gqa_k": nrm(ks[4], (DEC_BATCH, DEPTH, PAST_LEN, GQA_KV_HEADS, GQA_HEAD_DIM), 1.0),
        "cache_gqa_v": nrm(ks[5], (DEC_BATCH, DEPTH, PAST_LEN, GQA_KV_HEADS, GQA_HEAD_DIM), 1.0),
        "c": nrm(ks[6], (DEC_BATCH, D_MODEL), 1.0),
        "c_ctx": nrm(ks[7], (D_MODEL,), 1.0),
        "w_ada": nrm(ks[8], (DEPTH, D_MODEL, 6 * D_MODEL), 0.5 * D_MODEL ** -0.5),
        "b_ada": nrm(ks[9], (DEPTH, 6 * D_MODEL), 0.01),
        "g_norm1": gain(ks[10], (DEPTH, D_MODEL)),
        "g_norm2": gain(ks[11], (DEPTH, D_MODEL)),
        "w_in": nrm(ks[12], (DEPTH, D_MODEL, IN_WIDTH), D_MODEL ** -0.5),
        "g_q_a": gain(ks[13], (DEPTH, MLA_Q_RANK)),
        "w_q_up": nrm(ks[14], (DEPTH, MLA_Q_RANK, MLA_HEADS * MLA_QK_DIM), MLA_Q_RANK ** -0.5),
        "g_kv_a": gain(ks[15], (DEPTH, MLA_KV_RANK)),
        "w_kv_up": nrm(ks[16], (DEPTH, MLA_KV_RANK, MLA_HEADS * (MLA_NOPE_DIM + MLA_V_DIM)),
                       MLA_KV_RANK ** -0.5),
        "g_q_head": gain(ks[17], (DEPTH, GQA_HEAD_DIM)),
        "g_k_head": gain(ks[18], (DEPTH, GQA_HEAD_DIM)),
        "w_out": nrm(ks[19], (DEPTH, MIX_WIDTH, D_MODEL), MIX_WIDTH ** -0.5),
        "w_ffn_gate": nrm(ks[20], (DEPTH, D_MODEL, D_FF), D_MODEL ** -0.5),
        "w_ffn_up": nrm(ks[21], (DEPTH, D_MODEL, D_FF), D_MODEL ** -0.5),
        "w_ffn_down": nrm(ks[22], (DEPTH, D_FF, D_MODEL), D_FF ** -0.5),
        "g_final": gain(ks[23], (D_MODEL,)),
    }


def reference(x_prompt, x_sample, cache_mla_ckv, cache_mla_krope, cache_gqa_k, cache_gqa_v,
              c, c_ctx, w_ada, b_ada, g_norm1, g_norm2, w_in, g_q_a, w_q_up, g_kv_a, w_kv_up,
              g_q_head, g_k_head, w_out, w_ffn_gate, w_ffn_up, w_ffn_down, g_final):
    xp, xs = x_prompt, x_sample
    silu_ctx = jax.nn.silu(c_ctx)
    silu_c = jax.nn.silu(c)
    ckv_list, krope_list, gk_list, gv_list = [], [], [], []
    for l in range(DEPTH):
        w_l = (g_norm1[l], g_norm2[l], w_in[l], g_q_a[l], w_q_up[l], g_kv_a[l], w_kv_up[l],
               g_q_head[l], g_k_head[l], w_out[l], w_ffn_gate[l], w_ffn_up[l], w_ffn_down[l])
        mod_ctx = (silu_ctx @ w_ada[l] + b_ada[l])[None, None, :]
        xp, (ckv, krope, gk, gv) = trunk_layer(xp, mod_ctx, *w_l, None)
        ckv_list.append(ckv)
        krope_list.append(krope)
        gk_list.append(gk)
        gv_list.append(gv)
        mod_lat = (silu_c @ w_ada[l] + b_ada[l])[:, None, :]
        ctx_l = (cache_mla_ckv[:, l], cache_mla_krope[:, l], cache_gqa_k[:, l], cache_gqa_v[:, l])
        xs, _ = trunk_layer(xs, mod_lat, *w_l, ctx_l)
    y_prompt = rms_norm(xp, g_final)
    y_sample = rms_norm(xs, g_final)
    new_mla_ckv = jnp.stack(ckv_list, axis=1)
    new_mla_krope = jnp.stack(krope_list, axis=1)
    new_gqa_k = jnp.stack(gk_list, axis=1)
    new_gqa_v = jnp.stack(gv_list, axis=1)
    return (y_prompt, y_sample, new_mla_ckv, new_mla_krope, new_gqa_k, new_gqa_v)
```

```python
import functools

import numpy as np
import jax
import jax.numpy as jnp
from jax import lax
from jax.experimental import pallas as pl
from jax.experimental.pallas import tpu as pltpu

D_MODEL = 1024
DEPTH = 2
GRID_W = 64
MLA_HEADS = 6
MLA_Q_RANK = 384
MLA_KV_RANK = 256
MLA_NOPE_DIM = 64
MLA_ROPE_DIM = 32
MLA_V_DIM = 64
MLA_QK_DIM = MLA_NOPE_DIM + MLA_ROPE_DIM
FNET_GROUPS = 4
FNET_GROUP_DIM = 64
FNET_WIDTH = FNET_GROUPS * FNET_GROUP_DIM
GQA_HEADS = 6
GQA_KV_HEADS = 2
GQA_HEAD_DIM = 64
GQA_GROUP = GQA_HEADS // GQA_KV_HEADS
GQA_WIDTH = GQA_HEADS * GQA_HEAD_DIM
GQA_KV_WIDTH = GQA_KV_HEADS * GQA_HEAD_DIM
D_FF = 2816
ROPE_THETA = 10000.0
EPS = 1e-6

LANES = 128
HEAD_PAIRS = MLA_HEADS // 2
MLA_PAD_WIDTH = MLA_HEADS * LANES
VMEM_LIMIT = 52 * 1024 * 1024

_OFF_CQ = 0
_OFF_CKV = _OFF_CQ + MLA_Q_RANK
_OFF_U = _OFF_CKV + MLA_KV_RANK
_OFF_GQ = _OFF_U + FNET_WIDTH
_OFF_GK = _OFF_GQ + GQA_WIDTH
_OFF_GV = _OFF_GK + GQA_KV_WIDTH
_OFF_KR = _OFF_GV + GQA_KV_WIDTH
IN_PAD_WIDTH = _OFF_KR + LANES

BF16 = jnp.bfloat16
F32 = jnp.float32


def _rope_tables(n_tokens):
    t = np.arange(n_tokens)
    row = (t // GRID_W).astype(np.float64)
    col = (t % GRID_W).astype(np.float64)

    def angles(rot_dim):
        n_axis = rot_dim // 4
        inv = ROPE_THETA ** (-np.arange(n_axis, dtype=np.float64) / n_axis)
        return np.concatenate([row[:, None] * inv, col[:, None] * inv], axis=-1)

    def tables(ang, lane_to_pair, is_first, is_second):
        cos = np.where((is_first | is_second)[None, :], np.cos(ang)[:, lane_to_pair], 1.0)
        sin = np.sin(ang)[:, lane_to_pair]
        sin_a = np.where(is_second[None, :], sin, 0.0)
        sin_b = np.where(is_first[None, :], -sin, 0.0)
        return [cos, sin_a, sin_b]

    lane = np.arange(LANES)
    half_m = MLA_ROPE_DIM // 2
    in_rope = (lane >= MLA_NOPE_DIM) & (lane < MLA_QK_DIM)
    first_m = in_rope & (lane < MLA_NOPE_DIM + half_m)
    second_m = in_rope & ~first_m
    pair_m = np.where(in_rope, (lane - MLA_NOPE_DIM) % half_m, 0)
    half_g = GQA_HEAD_DIM // 2
    first_g = (lane % GQA_HEAD_DIM) < half_g
    pair_g = lane % half_g
    tabs = (tables(angles(MLA_ROPE_DIM), pair_m, first_m, second_m)
            + tables(angles(GQA_HEAD_DIM), pair_g, first_g, ~first_g))
    return jnp.asarray(np.concatenate(tabs, axis=-1), dtype=F32)


def _channel_dft():
    c = np.arange(FNET_GROUP_DIM)
    ang = 2.0 * np.pi * np.outer(c, c) / FNET_GROUP_DIM
    eye = np.eye(FNET_GROUPS)
    table = np.concatenate([np.kron(eye, np.cos(ang)), np.kron(eye, np.sin(ang))], axis=1)
    return jnp.asarray(table, dtype=F32).astype(BF16)


def _group_mean_matrix(n_heads):
    return jnp.asarray(np.kron(np.eye(n_heads), np.full((GQA_HEAD_DIM, GQA_HEAD_DIM), 1.0 / GQA_HEAD_DIM)), dtype=BF16)


def _kv_expand_matrix():
    e = np.kron(np.eye(GQA_KV_HEADS), np.tile(np.eye(GQA_HEAD_DIM), (1, GQA_GROUP)))
    return jnp.asarray(e, dtype=BF16)


def _rope_place_matrix():
    p = np.zeros((MLA_ROPE_DIM, LANES))
    p[np.arange(MLA_ROPE_DIM), MLA_NOPE_DIM + np.arange(MLA_ROPE_DIM)] = 1.0
    return jnp.asarray(p, dtype=BF16)


def _seq_dft_tables(seq):
    s = jnp.arange(seq, dtype=jnp.int32)
    idx = (s[:, None] * s[None, :]) % seq
    ang = idx.astype(F32) * np.float32(2.0 * np.pi / seq)
    return jnp.cos(ang).astype(BF16), (-jnp.sin(ang)).astype(BF16)


def _rms(x):
    return x * lax.rsqrt(jnp.mean(x * x, axis=-1, keepdims=True) + EPS)


def _dot(a, b):
    return jnp.dot(a, b, preferred_element_type=F32)


def _group_rms(x, mean_mat):
    sq = x * x
    hi = sq.astype(BF16)
    lo = (sq - hi.astype(F32)).astype(BF16)
    ms = _dot(hi, mean_mat) + _dot(lo, mean_mat)
    return x * lax.rsqrt(ms + EPS)


def _rotate(x, cos, sin_a, sin_b, half):
    return x * cos + pltpu.roll(x, half, 1) * sin_a + pltpu.roll(x, LANES - half, 1) * sin_b


def _ada_kernel(ct_ref, w_ref, b_ref, o_ref):
    s = ct_ref[...]
    s = s * jax.nn.sigmoid(s)
    w = w_ref[...]
    for m in range(o_ref.shape[0]):
        o_ref[m:m + 1, :] = jnp.sum(w * s[:, m:m + 1], axis=0, keepdims=True) + b_ref[...]


def _ada(cond_t, w_ada, b_ada, tn=512):
    n_cond = cond_t.shape[1]
    width = w_ada.shape[2]
    return pl.pallas_call(
        _ada_kernel,
        grid=(DEPTH, width // tn),
        in_specs=[
            pl.BlockSpec((D_MODEL, n_cond), lambda l, j: (0, 0)),
            pl.BlockSpec((None, D_MODEL, tn), lambda l, j: (l, 0, j)),
            pl.BlockSpec((None, 1, tn), lambda l, j: (l, 0, j)),
        ],
        out_specs=pl.BlockSpec((None, n_cond, tn), lambda l, j: (l, 0, j)),
        out_shape=jax.ShapeDtypeStruct((DEPTH, n_cond, width), F32),
        compiler_params=pltpu.CompilerParams(dimension_semantics=("arbitrary", "arbitrary"),
                                             vmem_limit_bytes=VMEM_LIMIT),
        name="ada",
    )(cond_t, w_ada, b_ada.reshape(DEPTH, 1, width))


def _proj_kernel(*refs, rope, emit_cache):
    (x_ref, mod_ref, g1_ref, w_in_ref, gqa_ref, wq_ref, gkva_ref, wk_ref, wv_ref, wdft_ref,
     gqh_ref, gkh_ref, mq_ref, mk_ref, exp_ref) = refs[:15]
    refs = refs[15:]
    if rope:
        tab_ref, refs = refs[0], refs[1:]
    qm_ref, km_ref, vm_ref, uc_ref, us_ref, gq_ref, gk_ref, gv_ref = refs[:8]
    refs = refs[8:]

    x = x_ref[...]
    shift = mod_ref[:, 0:D_MODEL]
    scale = mod_ref[:, D_MODEL:2 * D_MODEL]
    h = _rms(x) * g1_ref[...] * (1.0 + scale) + shift
    proj = _dot(h.astype(BF16), w_in_ref[...])

    if rope:
        cos_m, sa_m, sb_m = (tab_ref[:, i * LANES:(i + 1) * LANES] for i in range(3))
        cos_g, sa_g, sb_g = (tab_ref[:, i * LANES:(i + 1) * LANES] for i in range(3, 6))

    cqn = _rms(proj[:, _OFF_CQ:_OFF_CQ + MLA_Q_RANK]) * gqa_ref[...]
    q = _dot(cqn.astype(BF16), wq_ref[...])
    for hd in range(MLA_HEADS):
        qh = q[:, hd * LANES:(hd + 1) * LANES]
        if rope:
            qh = _rotate(qh, cos_m, sa_m, sb_m, MLA_ROPE_DIM // 2)
        qm_ref[:, hd * LANES:(hd + 1) * LANES] = (qh * (MLA_QK_DIM ** -0.5)).astype(BF16)

    ckvn = _rms(proj[:, _OFF_CKV:_OFF_CKV + MLA_KV_RANK]) * gkva_ref[...]
    cb = ckvn.astype(BF16)
    kn = _dot(cb, wk_ref[...])
    kr = proj[:, _OFF_KR:_OFF_KR + LANES]
    krr = _rotate(kr, cos_m, sa_m, sb_m, MLA_ROPE_DIM // 2) if rope else kr
    for hd in range(MLA_HEADS):
        km_ref[:, hd * LANES:(hd + 1) * LANES] = (kn[:, hd * LANES:(hd + 1) * LANES] + krr).astype(BF16)
    vm_ref[...] = _dot(cb, wv_ref[...]).astype(BF16)

    ucs = _dot(proj[:, _OFF_U:_OFF_U + FNET_WIDTH].astype(BF16), wdft_ref[...])
    uc_ref[...] = ucs[:, :FNET_WIDTH].astype(BF16)
    us_ref[...] = ucs[:, FNET_WIDTH:].astype(BF16)

    gqn = _group_rms(proj[:, _OFF_GQ:_OFF_GQ + GQA_WIDTH], mq_ref[...]) * gqh_ref[...]
    for p in range(HEAD_PAIRS):
        gp = gqn[:, p * LANES:(p + 1) * LANES]
        if rope:
            gp = _rotate(gp, cos_g, sa_g, sb_g, GQA_HEAD_DIM // 2)
        gq_ref[:, p * LANES:(p + 1) * LANES] = (gp * (GQA_HEAD_DIM ** -0.5)).astype(BF16)
    gkn = _group_rms(proj[:, _OFF_GK:_OFF_GK + GQA_KV_WIDTH], mk_ref[...]) * gkh_ref[...]
    gkr = _rotate(gkn, cos_g, sa_g, sb_g, GQA_HEAD_DIM // 2) if rope else gkn
    gv = proj[:, _OFF_GV:_OFF_GV + GQA_KV_WIDTH]
    gk_ref[...] = _dot(gkr.astype(BF16), exp_ref[...]).astype(BF16)
    gv_ref[...] = _dot(gv.astype(BF16), exp_ref[...]).astype(BF16)

    if emit_cache:
        ckv_out, kr_out, gk_out, gv_out = refs
        ckv_out[...] = ckvn
        kr_out[...] = kr
        gk_out[...] = gkn
        gv_out[...] = gv


def _proj(x, mod, lw, consts, *, seq, rope, emit_cache, tm=256):
    n_tok = x.shape[0]
    n_tiles = n_tok // tm
    tps = seq // tm
    n_seq = n_tok // seq
    tiles_per_mod = n_tiles // mod.shape[0]
    const = lambda shape: pl.BlockSpec(shape, lambda i: (0,) * len(shape))
    tok = lambda w: pl.BlockSpec((tm, w), lambda i: (i, 0))
    in_specs = [
        tok(D_MODEL),
        pl.BlockSpec((None, 1, 6 * D_MODEL), lambda i: (i // tiles_per_mod, 0, 0)),
        const((1, D_MODEL)),
        const((D_MODEL, IN_PAD_WIDTH)),
        const((1, MLA_Q_RANK)),
        const((MLA_Q_RANK, MLA_PAD_WIDTH)),
        const((1, MLA_KV_RANK)),
        const((MLA_KV_RANK, MLA_PAD_WIDTH)),
        const((MLA_KV_RANK, MLA_HEADS * MLA_V_DIM)),
        const((FNET_WIDTH, 2 * FNET_WIDTH)),
        const((1, GQA_WIDTH)),
        const((1, GQA_KV_WIDTH)),
        const((GQA_WIDTH, GQA_WIDTH)),
        const((GQA_KV_WIDTH, GQA_KV_WIDTH)),
        const((GQA_KV_WIDTH, GQA_WIDTH)),
    ]
    args = [x, mod, lw["g1"], lw["w_in"], lw["g_q_a"], lw["wq"], lw["g_kv_a"], lw["wk"], lw["wv"],
            consts["wdft"], lw["g_q_head"], lw["g_k_head"], consts["mean_q"], consts["mean_k"], consts["expand"]]
    if rope:
        in_specs.append(pl.BlockSpec((tm, 6 * LANES), lambda i: (i % tps, 0)))
        args.append(consts["rope"])
    seq_major = pl.BlockSpec((tm, FNET_WIDTH), lambda i: (i % tps, i // tps))
    out_specs = [tok(MLA_PAD_WIDTH), tok(MLA_PAD_WIDTH), tok(MLA_HEADS * MLA_V_DIM), seq_major, seq_major,
                 tok(GQA_WIDTH), tok(GQA_WIDTH), tok(GQA_WIDTH)]
    out_shape = [jax.ShapeDtypeStruct((n_tok, MLA_PAD_WIDTH), BF16),
                 jax.ShapeDtypeStruct((n_tok, MLA_PAD_WIDTH), BF16),
                 jax.ShapeDtypeStruct((n_tok, MLA_HEADS * MLA_V_DIM), BF16),
                 jax.ShapeDtypeStruct((seq, n_seq * FNET_WIDTH), BF16),
                 jax.ShapeDtypeStruct((seq, n_seq * FNET_WIDTH), BF16),
                 jax.ShapeDtypeStruct((n_tok, GQA_WIDTH), BF16),
                 jax.ShapeDtypeStruct((n_tok, GQA_WIDTH), BF16),
                 jax.ShapeDtypeStruct((n_tok, GQA_WIDTH), BF16)]
    if emit_cache:
        out_specs += [tok(MLA_KV_RANK), tok(LANES), tok(GQA_KV_WIDTH), tok(GQA_KV_WIDTH)]
        out_shape += [jax.ShapeDtypeStruct((n_tok, MLA_KV_RANK), F32),
                      jax.ShapeDtypeStruct((n_tok, LANES), F32),
                      jax.ShapeDtypeStruct((n_tok, GQA_KV_WIDTH), F32),
                      jax.ShapeDtypeStruct((n_tok, GQA_KV_WIDTH), F32)]
    return pl.pallas_call(
        functools.partial(_proj_kernel, rope=rope, emit_cache=emit_cache),
        grid=(n_tiles,),
        in_specs=in_specs,
        out_specs=out_specs,
        out_shape=out_shape,
        compiler_params=pltpu.CompilerParams(dimension_semantics=("arbitrary",), vmem_limit_bytes=VMEM_LIMIT),
        name="proj_rope" if rope else "proj_ctx",
    )(*args)


def _ctx_kernel(ckv_ref, kr_ref, gk_ref, gv_ref, wk_ref, wv_ref, place_ref, exp_ref,
                km_ref, vm_ref, gko_ref, gvo_ref):
    cb = ckv_ref[...].astype(BF16)
    kn = _dot(cb, wk_ref[...])
    kr = _dot(kr_ref[...].astype(BF16), place_ref[...])
    for hd in range(MLA_HEADS):
        km_ref[:, hd * LANES:(hd + 1) * LANES] = (kn[:, hd * LANES:(hd + 1) * LANES] + kr).astype(BF16)
    vm_ref[...] = _dot(cb, wv_ref[...]).astype(BF16)
    gko_ref[...] = _dot(gk_ref[...].astype(BF16), exp_ref[...]).astype(BF16)
    gvo_ref[...] = _dot(gv_ref[...].astype(BF16), exp_ref[...]).astype(BF16)


def _ctx_prep(layer, cache_ckv, cache_krope, cache_gk, cache_gv, lw, consts):
    nb, _, past, _ = cache_ckv.shape
    cache = lambda w: pl.BlockSpec((None, None, past, w), lambda b: (b, layer, 0, 0))
    const = lambda shape: pl.BlockSpec(shape, lambda b: (0,) * len(shape))
    out = lambda w: pl.BlockSpec((None, past, w), lambda b: (b, 0, 0))
    return pl.pallas_call(
        _ctx_kernel,
        grid=(nb,),
        in_specs=[cache(MLA_KV_RANK), cache(MLA_ROPE_DIM), cache(GQA_KV_WIDTH), cache(GQA_KV_WIDTH),
                  const((MLA_KV_RANK, MLA_PAD_WIDTH)), const((MLA_KV_RANK, MLA_HEADS * MLA_V_DIM)),
                  const((MLA_ROPE_DIM, LANES)), const((GQA_KV_WIDTH, GQA_WIDTH))],
        out_specs=[out(MLA_PAD_WIDTH), out(MLA_HEADS * MLA_V_DIM), out(GQA_WIDTH), out(GQA_WIDTH)],
        out_shape=[jax.ShapeDtypeStruct((nb, past, MLA_PAD_WIDTH), BF16),
                   jax.ShapeDtypeStruct((nb, past, MLA_HEADS * MLA_V_DIM), BF16),
                   jax.ShapeDtypeStruct((nb, past, GQA_WIDTH), BF16),
                   jax.ShapeDtypeStruct((nb, past, GQA_WIDTH), BF16)],
        compiler_params=pltpu.CompilerParams(dimension_semantics=("arbitrary",), vmem_limit_bytes=VMEM_LIMIT),
        name="ctx_prep",
    )(cache_ckv, cache_krope, cache_gk.reshape(nb, DEPTH, past, GQA_KV_WIDTH),
      cache_gv.reshape(nb, DEPTH, past, GQA_KV_WIDTH), lw["wk"], lw["wv"], consts["place"], consts["expand"])


def _attn_kernel(*refs, n_seg, head_tiles):
    q_ref, o_ref = refs[0], refs[-1]
    k_refs = refs[1:1 + n_seg]
    v_refs = refs[1 + n_seg:1 + 2 * n_seg]
    q = q_ref[...]
    lane = lax.broadcasted_iota(jnp.int32, (q.shape[0], LANES), 1)
    first_half = lane < (LANES // 2)
    outs = []
    for hd in range(2):
        if head_tiles == 2:
            qh = q[:, hd * LANES:(hd + 1) * LANES]
        else:
            qh = jnp.where(first_half if hd == 0 else ~first_half, q, jnp.zeros_like(q))
        scores = []
        for k_ref in k_refs:
            kh = k_ref[:, hd * LANES:(hd + 1) * LANES] if head_tiles == 2 else k_ref[...]
            scores.append(lax.dot_general(qh, kh, (((1,), (1,)), ((), ())), preferred_element_type=F32))
        m = functools.reduce(jnp.maximum, [jnp.max(s, axis=-1, keepdims=True) for s in scores])
        probs = [jnp.exp(s - m) for s in scores]
        denom = functools.reduce(jnp.add, [jnp.sum(p, axis=-1, keepdims=True) for p in probs])
        acc = functools.reduce(jnp.add, [_dot(p.astype(BF16), v_ref[...]) for p, v_ref in zip(probs, v_refs)])
        outs.append(acc * (1.0 / denom))
    o_ref[...] = jnp.where(first_half, outs[0], outs[1]).astype(o_ref.dtype)


def _attention(q, kv_segments, *, n_batch, head_tiles, tq=256):
    sq = q.shape[1]
    qw = head_tiles * LANES
    n_seg = len(kv_segments)
    in_specs = [pl.BlockSpec((None, tq, qw), lambda b, p, i: (b, i, p))]
    in_specs += [pl.BlockSpec((None, k.shape[1], qw), lambda b, p, i: (b, 0, p)) for k, _ in kv_segments]
    in_specs += [pl.BlockSpec((None, v.shape[1], LANES), lambda b, p, i: (b, 0, p)) for _, v in kv_segments]
    return pl.pallas_call(
        functools.partial(_attn_kernel, n_seg=n_seg, head_tiles=head_tiles),
        grid=(n_batch, HEAD_PAIRS, sq // tq),
        in_specs=in_specs,
        out_specs=pl.BlockSpec((None, tq, LANES), lambda b, p, i: (b, i, p)),
        out_shape=jax.ShapeDtypeStruct((n_batch, sq, HEAD_PAIRS * LANES), BF16),
        compiler_params=pltpu.CompilerParams(dimension_semantics=("arbitrary",) * 3, vmem_limit_bytes=VMEM_LIMIT),
        name=f"attn_t{head_tiles}_s{n_seg}",
    )(q, *[k for k, _ in kv_segments], *[v for _, v in kv_segments])


def _fnet_kernel(tc_ref, ts_ref, uc_ref, us_ref, o_ref, *, scale):
    acc = _dot(tc_ref[...], uc_ref[...]) + _dot(ts_ref[...], us_ref[...])
    o_ref[...] = (acc * scale).astype(o_ref.dtype)


def _fnet(t_cos, t_nsin, uc, us, tm, tn):
    seq, width = uc.shape
    scale = float((seq * FNET_GROUP_DIM) ** -0.5)
    return pl.pallas_call(
        functools.partial(_fnet_kernel, scale=scale),
        grid=(width // tn, seq // tm),
        in_specs=[pl.BlockSpec((tm, seq), lambda j, i: (i, 0)),
                  pl.BlockSpec((tm, seq), lambda j, i: (i, 0)),
                  pl.BlockSpec((seq, tn), lambda j, i: (0, j)),
                  pl.BlockSpec((seq, tn), lambda j, i: (0, j))],
        out_specs=pl.BlockSpec((tm, tn), lambda j, i: (i, j)),
        out_shape=jax.ShapeDtypeStruct((seq, width), BF16),
        compiler_params=pltpu.CompilerParams(dimension_semantics=("arbitrary", "arbitrary"),
                                             vmem_limit_bytes=VMEM_LIMIT),
        name="fnet",
    )(t_cos, t_nsin, uc, us)


def _post_kernel(x_ref, mla_ref, fn_ref, gqa_ref, mod_ref, g2_ref, wo_ref, wg_ref, wu_ref, wd_ref, gf_ref,
                 o_ref, *, final):
    mix = jnp.concatenate([mla_ref[...], fn_ref[...], gqa_ref[...]], axis=-1)
    gate1 = mod_ref[:, 2 * D_MODEL:3 * D_MODEL]
    shift2 = mod_ref[:, 3 * D_MODEL:4 * D_MODEL]
    scale2 = mod_ref[:, 4 * D_MODEL:5 * D_MODEL]
    gate2 = mod_ref[:, 5 * D_MODEL:6 * D_MODEL]
    x = x_ref[...] + gate1 * _dot(mix, wo_ref[...])
    h = (_rms(x) * g2_ref[...] * (1.0 + scale2) + shift2).astype(BF16)
    g = _dot(h, wg_ref[...])
    u = _dot(h, wu_ref[...])
    a = (g * jax.nn.sigmoid(g) * u).astype(BF16)
    x = x + gate2 * _dot(a, wd_ref[...])
    if final:
        x = _rms(x) * gf_ref[...]
    o_ref[...] = x


def _post(x, mla_o, fnet_o, gqa_o, mod, lw, g_final, *, seq, final, tm=256):
    n_tok = x.shape[0]
    n_tiles = n_tok // tm
    tps = seq // tm
    tiles_per_mod = n_tiles // mod.shape[0]
    const = lambda shape: pl.BlockSpec(shape, lambda i: (0,) * len(shape), pipeline_mode=pl.Buffered(1))
    tok = lambda w: pl.BlockSpec((tm, w), lambda i: (i, 0))
    return pl.pallas_call(
        functools.partial(_post_kernel, final=final),
        grid=(n_tiles,),
        in_specs=[tok(D_MODEL), tok(HEAD_PAIRS * LANES),
                  pl.BlockSpec((tm, FNET_WIDTH), lambda i: (i % tps, i // tps)),
                  tok(HEAD_PAIRS * LANES),
                  pl.BlockSpec((None, 1, 6 * D_MODEL), lambda i: (i // tiles_per_mod, 0, 0)),
                  const((1, D_MODEL)), const((D_MODEL, D_MODEL)), const((D_MODEL, D_FF)),
                  const((D_MODEL, D_FF)), const((D_FF, D_MODEL)), const((1, D_MODEL))],
        out_specs=tok(D_MODEL),
        out_shape=jax.ShapeDtypeStruct((n_tok, D_MODEL), F32),
        compiler_params=pltpu.CompilerParams(dimension_semantics=("arbitrary",), vmem_limit_bytes=VMEM_LIMIT),
        name="post_final" if final else "post",
    )(x, mla_o, fnet_o, gqa_o, mod, lw["g2"], lw["w_out"], lw["w_gate"], lw["w_up"], lw["w_down"], g_final)


def _layer_weights(l, g_norm1, g_norm2, w_in, g_q_a, w_q_up, g_kv_a, w_kv_up, g_q_head, g_k_head,
                   w_out, w_ffn_gate, w_ffn_up, w_ffn_down):
    w = w_in[l]
    o = np.cumsum([0, MLA_Q_RANK, MLA_KV_RANK, MLA_ROPE_DIM, FNET_WIDTH, GQA_WIDTH, GQA_KV_WIDTH, GQA_KV_WIDTH])
    cq, ckv, kr, u, gq, gk, gv = (w[:, o[i]:o[i + 1]] for i in range(7))
    kr_tile = jnp.pad(kr, ((0, 0), (MLA_NOPE_DIM, LANES - MLA_QK_DIM)))
    w_in_p = jnp.concatenate([cq, ckv, u, gq, gk, gv, kr_tile], axis=1).astype(BF16)
    wq = jnp.pad(w_q_up[l].reshape(MLA_Q_RANK, MLA_HEADS, MLA_QK_DIM), ((0, 0), (0, 0), (0, LANES - MLA_QK_DIM)))
    wkv = w_kv_up[l].reshape(MLA_KV_RANK, MLA_HEADS, MLA_NOPE_DIM + MLA_V_DIM)
    wk = jnp.pad(wkv[:, :, :MLA_NOPE_DIM], ((0, 0), (0, 0), (0, LANES - MLA_NOPE_DIM)))
    wv = wkv[:, :, MLA_NOPE_DIM:]
    return {
        "g1": g_norm1[l].reshape(1, D_MODEL), "g2": g_norm2[l].reshape(1, D_MODEL),
        "w_in": w_in_p,
        "g_q_a": g_q_a[l].reshape(1, MLA_Q_RANK), "g_kv_a": g_kv_a[l].reshape(1, MLA_KV_RANK),
        "wq": wq.reshape(MLA_Q_RANK, MLA_PAD_WIDTH).astype(BF16),
        "wk": wk.reshape(MLA_KV_RANK, MLA_PAD_WIDTH).astype(BF16),
        "wv": wv.reshape(MLA_KV_RANK, MLA_HEADS * MLA_V_DIM).astype(BF16),
        "g_q_head": jnp.tile(g_q_head[l], GQA_HEADS).reshape(1, GQA_WIDTH),
        "g_k_head": jnp.tile(g_k_head[l], GQA_KV_HEADS).reshape(1, GQA_KV_WIDTH),
        "w_out": w_out[l].astype(BF16), "w_gate": w_ffn_gate[l].astype(BF16),
        "w_up": w_ffn_up[l].astype(BF16), "w_down": w_ffn_down[l].astype(BF16),
    }


def kernel(x_prompt, x_sample, cache_mla_ckv, cache_mla_krope, cache_gqa_k, cache_gqa_v, c, c_ctx, w_ada, b_ada,
           g_norm1, g_norm2, w_in, g_q_a, w_q_up, g_kv_a, w_kv_up, g_q_head, g_k_head, w_out,
           w_ffn_gate, w_ffn_up, w_ffn_down, g_final):
    n_pb, p_seq, _ = x_prompt.shape
    n_sb, s_seq, _ = x_sample.shape
    consts = {"wdft": _channel_dft(), "mean_q": _group_mean_matrix(GQA_HEADS),
              "mean_k": _group_mean_matrix(GQA_KV_HEADS), "expand": _kv_expand_matrix(),
              "place": _rope_place_matrix(), "rope": _rope_tables(s_seq)}
    dft_p = _seq_dft_tables(p_seq)
    dft_s = _seq_dft_tables(s_seq)
    gf = g_final.reshape(1, D_MODEL)

    cond_t = jnp.concatenate([c_ctx[None, :], c], axis=0).T
    mod = _ada(cond_t, w_ada, b_ada)

    xp = x_prompt.reshape(n_pb * p_seq, D_MODEL)
    xs = x_sample.reshape(n_sb * s_seq, D_MODEL)
    caches = []
    for l in range(DEPTH):
        lw = _layer_weights(l, g_norm1, g_norm2, w_in, g_q_a, w_q_up, g_kv_a, w_kv_up, g_q_head, g_k_head,
                            w_out, w_ffn_gate, w_ffn_up, w_ffn_down)
        final = l == DEPTH - 1
        mod_p = mod[l, 0:1].reshape(1, 1, 6 * D_MODEL)
        mod_s = mod[l, 1:].reshape(n_sb, 1, 6 * D_MODEL)

        qm, km, vm, uc, us, gq, gk, gv, ckv_c, kr_c, gk_c, gv_c = _proj(
            xp, mod_p, lw, consts, seq=p_seq, rope=False, emit_cache=True)
        caches.append((ckv_c, kr_c[:, MLA_NOPE_DIM:MLA_QK_DIM], gk_c, gv_c))
        b3 = lambda a, nb: a.reshape(nb, a.shape[0] // nb, a.shape[1])
        mla_o = _attention(b3(qm, n_pb), [(b3(km, n_pb), b3(vm, n_pb))], n_batch=n_pb, head_tiles=2)
        gqa_o = _attention(b3(gq, n_pb), [(b3(gk, n_pb), b3(gv, n_pb))], n_batch=n_pb, head_tiles=1)
        fn_o = _fnet(dft_p[0], dft_p[1], uc, us, tm=p_seq, tn=1024)
        xp = _post(xp, mla_o.reshape(-1, GQA_WIDTH), fn_o, gqa_o.reshape(-1, GQA_WIDTH), mod_p, lw, gf,
                   seq=p_seq, final=final)

        km_c, vm_c, gk_cx, gv_cx = _ctx_prep(l, cache_mla_ckv, cache_mla_krope, cache_gqa_k, cache_gqa_v, lw, consts)
        qm, km, vm, uc, us, gq, gk, gv = _proj(xs, mod_s, lw, consts, seq=s_seq, rope=True, emit_cache=False)
        mla_o = _attention(b3(qm, n_sb), [(km_c, vm_c), (b3(km, n_sb), b3(vm, n_sb))], n_batch=n_sb, head_tiles=2)
        gqa_o = _attention(b3(gq, n_sb), [(gk_cx, gv_cx), (b3(gk, n_sb), b3(gv, n_sb))], n_batch=n_sb, head_tiles=1)
        fn_o = _fnet(dft_s[0], dft_s[1], uc, us, tm=512, tn=n_sb * FNET_WIDTH)
        xs = _post(xs, mla_o.reshape(-1, GQA_WIDTH), fn_o, gqa_o.reshape(-1, GQA_WIDTH), mod_s, lw, gf,
                   seq=s_seq, final=final)

    stack = lambda i, tail: jnp.stack([cl[i].reshape((n_pb, p_seq) + tail) for cl in caches], axis=1)
    return (xp.reshape(n_pb, p_seq, D_MODEL), xs.reshape(n_sb, s_seq, D_MODEL),
            stack(0, (MLA_KV_RANK,)), stack(1, (MLA_ROPE_DIM,)),
            stack(2, (GQA_KV_HEADS, GQA_HEAD_DIM)), stack(3, (GQA_KV_HEADS, GQA_HEAD_DIM)))
```

```python
import functools

import numpy as np
import jax
import jax.numpy as jnp
from jax import lax
from jax.experimental import pallas as pl
from jax.experimental.pallas import tpu as pltpu

D_MODEL = 1024
DEPTH = 2
GRID_W = 64
MLA_HEADS = 6
MLA_Q_RANK = 384
MLA_KV_RANK = 256
MLA_NOPE_DIM = 64
MLA_ROPE_DIM = 32
MLA_V_DIM = 64
MLA_QK_DIM = MLA_NOPE_DIM + MLA_ROPE_DIM
FNET_GROUPS = 4
FNET_GROUP_DIM = 64
FNET_WIDTH = FNET_GROUPS * FNET_GROUP_DIM
GQA_HEADS = 6
GQA_KV_HEADS = 2
GQA_HEAD_DIM = 64
GQA_GROUP = GQA_HEADS // GQA_KV_HEADS
GQA_WIDTH = GQA_HEADS * GQA_HEAD_DIM
GQA_KV_WIDTH = GQA_KV_HEADS * GQA_HEAD_DIM
D_FF = 2816
ROPE_THETA = 10000.0
EPS = 1e-6
LOG2_E = 1.4426950408889634

LANES = 128
HEAD_PAIRS = MLA_HEADS // 2
MLA_PAD_WIDTH = MLA_HEADS * LANES
VMEM_LIMIT = 52 * 1024 * 1024

_OFF_CQ = 0
_OFF_CKV = _OFF_CQ + MLA_Q_RANK
_OFF_U = _OFF_CKV + MLA_KV_RANK
_OFF_GQ = _OFF_U + FNET_WIDTH
_OFF_GK = _OFF_GQ + GQA_WIDTH
_OFF_GV = _OFF_GK + GQA_KV_WIDTH
_OFF_KR = _OFF_GV + GQA_KV_WIDTH
IN_PAD_WIDTH = _OFF_KR + LANES

BF16 = jnp.bfloat16
F32 = jnp.float32


def _rope_tables(n_tokens):
    t = np.arange(n_tokens)
    row = (t // GRID_W).astype(np.float64)
    col = (t % GRID_W).astype(np.float64)

    def angles(rot_dim):
        n_axis = rot_dim // 4
        inv = ROPE_THETA ** (-np.arange(n_axis, dtype=np.float64) / n_axis)
        return np.concatenate([row[:, None] * inv, col[:, None] * inv], axis=-1)

    def tables(ang, lane_to_pair, is_first, is_second):
        cos = np.where((is_first | is_second)[None, :], np.cos(ang)[:, lane_to_pair], 1.0)
        sin = np.sin(ang)[:, lane_to_pair]
        sin_a = np.where(is_second[None, :], sin, 0.0)
        sin_b = np.where(is_first[None, :], -sin, 0.0)
        return [cos, sin_a, sin_b]

    lane = np.arange(LANES)
    half_m = MLA_ROPE_DIM // 2
    in_rope = lane < MLA_ROPE_DIM
    first_m = lane < half_m
    second_m = in_rope & ~first_m
    pair_m = lane % half_m
    half_g = GQA_HEAD_DIM // 2
    first_g = (lane % GQA_HEAD_DIM) < half_g
    pair_g = lane % half_g
    tabs = (tables(angles(MLA_ROPE_DIM), pair_m, first_m, second_m)
            + tables(angles(GQA_HEAD_DIM), pair_g, first_g, ~first_g))
    return jnp.asarray(np.concatenate(tabs, axis=-1), dtype=F32)


def _channel_dft():
    c = np.arange(FNET_GROUP_DIM)
    ang = 2.0 * np.pi * np.outer(c, c) / FNET_GROUP_DIM
    eye = np.eye(FNET_GROUPS)
    table = np.concatenate([np.kron(eye, np.cos(ang)), np.kron(eye, np.sin(ang))], axis=1)
    return jnp.asarray(table, dtype=F32).astype(BF16)


def _group_mean_matrix(n_heads):
    return jnp.asarray(np.kron(np.eye(n_heads), np.full((GQA_HEAD_DIM, GQA_HEAD_DIM), 1.0 / GQA_HEAD_DIM)), dtype=BF16)


def _kv_expand_matrix():
    e = np.kron(np.eye(GQA_KV_HEADS), np.tile(np.eye(GQA_HEAD_DIM), (1, GQA_GROUP)))
    return jnp.asarray(e, dtype=BF16)


def _rope_place_matrix():
    return jnp.asarray(np.eye(MLA_ROPE_DIM, LANES), dtype=BF16)


DFT_SPLIT = 32


def _seq_dft_tables(seq):
    s = np.arange(seq)
    n_a = seq // DFT_SPLIT
    ang_a = 2.0 * np.pi * ((np.arange(n_a)[:, None] * s[None, :]) % n_a) / n_a
    ang_b = 2.0 * np.pi * ((np.arange(DFT_SPLIT)[:, None] * s[None, :]) % seq) / seq
    return tuple(jnp.asarray(t, dtype=F32) for t in (np.cos(ang_a), np.sin(ang_a), np.cos(ang_b), np.sin(ang_b)))


def _rms(x):
    return x * lax.rsqrt(jnp.mean(x * x, axis=-1, keepdims=True) + EPS)


def _dot(a, b):
    return jnp.dot(a, b, preferred_element_type=F32)


def _group_rms(x, mean_mat):
    sq = x * x
    hi = sq.astype(BF16)
    lo = (sq - hi.astype(F32)).astype(BF16)
    ms = _dot(hi, mean_mat) + _dot(lo, mean_mat)
    return x * lax.rsqrt(ms + EPS)


def _rotate(x, cos, sin_a, sin_b, half):
    return x * cos + pltpu.roll(x, half, 1) * sin_a + pltpu.roll(x, LANES - half, 1) * sin_b


def _ada_kernel(ct_ref, w_ref, b_ref, o_ref):
    s = ct_ref[...]
    s = s * jax.nn.sigmoid(s)
    w = w_ref[...]
    for m in range(o_ref.shape[0]):
        o_ref[m:m + 1, :] = jnp.sum(w * s[:, m:m + 1], axis=0, keepdims=True) + b_ref[...]


def _ada(cond_t, w_ada, b_ada, tn=512):
    n_cond = cond_t.shape[1]
    width = w_ada.shape[2]
    return pl.pallas_call(
        _ada_kernel,
        grid=(DEPTH, width // tn),
        in_specs=[
            pl.BlockSpec((D_MODEL, n_cond), lambda l, j: (0, 0)),
            pl.BlockSpec((None, D_MODEL, tn), lambda l, j: (l, 0, j)),
            pl.BlockSpec((None, 1, tn), lambda l, j: (l, 0, j)),
        ],
        out_specs=pl.BlockSpec((None, n_cond, tn), lambda l, j: (l, 0, j)),
        out_shape=jax.ShapeDtypeStruct((DEPTH, n_cond, width), F32),
        compiler_params=pltpu.CompilerParams(dimension_semantics=("arbitrary", "arbitrary"),
                                             vmem_limit_bytes=VMEM_LIMIT),
        name="ada",
    )(cond_t, w_ada, b_ada.reshape(DEPTH, 1, width))


def _proj_kernel(*refs, rope, emit_cache, n_prev):
    (x_ref, mod_ref, g1_ref, w_in_ref, gqa_ref, wq_ref, gkva_ref, wk_ref, wv_ref, wdft_ref,
     gqh_ref, gkh_ref, mq_ref, mk_ref, exp_ref) = refs[:15]
    refs = refs[15:]
    if rope:
        tab_ref, refs = refs[0], refs[1:]
    if n_prev:
        prev_refs, refs = refs[:4], refs[4:]
    qm_ref, km_ref, vm_ref, uc_ref, us_ref, gq_ref, gk_ref, gv_ref = refs[:8]
    refs = refs[8:]

    x = x_ref[...]
    shift = mod_ref[:, 0:D_MODEL]
    scale = mod_ref[:, D_MODEL:2 * D_MODEL]
    h = _rms(x) * g1_ref[...] * (1.0 + scale) + shift
    proj = _dot(h.astype(BF16), w_in_ref[...])

    if rope:
        cos_m, sa_m, sb_m = (tab_ref[:, i * LANES:(i + 1) * LANES] for i in range(3))
        cos_g, sa_g, sb_g = (tab_ref[:, i * LANES:(i + 1) * LANES] for i in range(3, 6))

    cqn = _rms(proj[:, _OFF_CQ:_OFF_CQ + MLA_Q_RANK]) * gqa_ref[...]
    q = _dot(cqn.astype(BF16), wq_ref[...])
    for hd in range(MLA_HEADS):
        qh = q[:, hd * LANES:(hd + 1) * LANES]
        if rope:
            qh = _rotate(qh, cos_m, sa_m, sb_m, MLA_ROPE_DIM // 2)
        qm_ref[:, hd * LANES:(hd + 1) * LANES] = (qh * (LOG2_E * MLA_QK_DIM ** -0.5)).astype(BF16)

    ckvn = _rms(proj[:, _OFF_CKV:_OFF_CKV + MLA_KV_RANK]) * gkva_ref[...]
    cb = ckvn.astype(BF16)
    kn = _dot(cb, wk_ref[...])
    kr = proj[:, _OFF_KR:_OFF_KR + LANES]
    krr = _rotate(kr, cos_m, sa_m, sb_m, MLA_ROPE_DIM // 2) if rope else kr
    for hd in range(MLA_HEADS):
        km_ref[:, hd * LANES:(hd + 1) * LANES] = (kn[:, hd * LANES:(hd + 1) * LANES] + krr).astype(BF16)
    vm_ref[...] = _dot(cb, wv_ref[...]).astype(BF16)

    ucs = _dot(proj[:, _OFF_U:_OFF_U + FNET_WIDTH].astype(BF16), wdft_ref[...])
    uc_ref[...] = ucs[:, :FNET_WIDTH].astype(BF16)
    us_ref[...] = ucs[:, FNET_WIDTH:].astype(BF16)

    gqn = _group_rms(proj[:, _OFF_GQ:_OFF_GQ + GQA_WIDTH], mq_ref[...]) * gqh_ref[...]
    for p in range(HEAD_PAIRS):
        gp = gqn[:, p * LANES:(p + 1) * LANES]
        if rope:
            gp = _rotate(gp, cos_g, sa_g, sb_g, GQA_HEAD_DIM // 2)
        gq_ref[:, p * LANES:(p + 1) * LANES] = (gp * (LOG2_E * GQA_HEAD_DIM ** -0.5)).astype(BF16)
    gkn = _group_rms(proj[:, _OFF_GK:_OFF_GK + GQA_KV_WIDTH], mk_ref[...]) * gkh_ref[...]
    gkr = _rotate(gkn, cos_g, sa_g, sb_g, GQA_HEAD_DIM // 2) if rope else gkn
    gv = proj[:, _OFF_GV:_OFF_GV + GQA_KV_WIDTH]
    gk_ref[...] = _dot(gkr.astype(BF16), exp_ref[...]).astype(BF16)
    gv_ref[...] = _dot(gv.astype(BF16), exp_ref[...]).astype(BF16)

    if emit_cache:
        new = (ckvn, kr[:, :MLA_ROPE_DIM], gkn, gv)
        for i, out_ref in enumerate(refs):
            if n_prev:
                out_ref[:n_prev] = prev_refs[i][...]
            out_ref[n_prev] = new[i]


def _proj(x, mod, lw, consts, *, seq, rope, prev_cache=None, emit_cache=False, tm=256):
    n_tok = x.shape[0]
    n_tiles = n_tok // tm
    tps = seq // tm
    n_seq = n_tok // seq
    tiles_per_mod = n_tiles // mod.shape[0]
    const = lambda shape: pl.BlockSpec(shape, lambda i: (0,) * len(shape))
    tok = lambda w: pl.BlockSpec((tm, w), lambda i: (i, 0))
    in_specs = [
        tok(D_MODEL),
        pl.BlockSpec((None, 1, 6 * D_MODEL), lambda i: (i // tiles_per_mod, 0, 0)),
        const((1, D_MODEL)),
        const((D_MODEL, IN_PAD_WIDTH)),
        const((1, MLA_Q_RANK)),
        const((MLA_Q_RANK, MLA_PAD_WIDTH)),
        const((1, MLA_KV_RANK)),
        const((MLA_KV_RANK, MLA_PAD_WIDTH)),
        const((MLA_KV_RANK, MLA_HEADS * MLA_V_DIM)),
        const((FNET_WIDTH, 2 * FNET_WIDTH)),
        const((1, GQA_WIDTH)),
        const((1, GQA_KV_WIDTH)),
        const((GQA_WIDTH, GQA_WIDTH)),
        const((GQA_KV_WIDTH, GQA_KV_WIDTH)),
        const((GQA_KV_WIDTH, GQA_WIDTH)),
    ]
    args = [x, mod, lw["g1"], lw["w_in"], lw["g_q_a"], lw["wq"], lw["g_kv_a"], lw["wk"], lw["wv"],
            consts["wdft"], lw["g_q_head"], lw["g_k_head"], consts["mean_q"], consts["mean_k"], consts["expand"]]
    if rope:
        in_specs.append(pl.BlockSpec((tm, 6 * LANES), lambda i: (i % tps, 0)))
        args.append(consts["rope"])
    seq_major = pl.BlockSpec((tm, FNET_WIDTH), lambda i: (i % tps, i // tps))
    out_specs = [tok(MLA_PAD_WIDTH), tok(MLA_PAD_WIDTH), tok(MLA_HEADS * MLA_V_DIM), seq_major, seq_major,
                 tok(GQA_WIDTH), tok(GQA_WIDTH), tok(GQA_WIDTH)]
    out_shape = [jax.ShapeDtypeStruct((n_tok, MLA_PAD_WIDTH), BF16),
                 jax.ShapeDtypeStruct((n_tok, MLA_PAD_WIDTH), BF16),
                 jax.ShapeDtypeStruct((n_tok, MLA_HEADS * MLA_V_DIM), BF16),
                 jax.ShapeDtypeStruct((seq, n_seq * FNET_WIDTH), BF16),
                 jax.ShapeDtypeStruct((seq, n_seq * FNET_WIDTH), BF16),
                 jax.ShapeDtypeStruct((n_tok, GQA_WIDTH), BF16),
                 jax.ShapeDtypeStruct((n_tok, GQA_WIDTH), BF16),
                 jax.ShapeDtypeStruct((n_tok, GQA_WIDTH), BF16)]
    n_prev = 0
    if emit_cache:
        assert tps == 1, "cache outputs are written one sequence per token tile"
        n_prev = prev_cache[0].shape[1] if prev_cache is not None else 0
        layers = lambda n, w: pl.BlockSpec((None, n, seq, w), lambda i: (i, 0, 0, 0))
        widths = (MLA_KV_RANK, MLA_ROPE_DIM, GQA_KV_WIDTH, GQA_KV_WIDTH)
        if n_prev:
            in_specs += [layers(n_prev, w) for w in widths]
            args += list(prev_cache)
        out_specs += [layers(n_prev + 1, w) for w in widths]
        out_shape += [jax.ShapeDtypeStruct((n_seq, n_prev + 1, seq, w), F32) for w in widths]
    return pl.pallas_call(
        functools.partial(_proj_kernel, rope=rope, emit_cache=emit_cache, n_prev=n_prev),
        grid=(n_tiles,),
        in_specs=in_specs,
        out_specs=out_specs,
        out_shape=out_shape,
        compiler_params=pltpu.CompilerParams(dimension_semantics=("arbitrary",), vmem_limit_bytes=VMEM_LIMIT),
        name="proj_rope" if rope else "proj_ctx",
    )(*args)


def _ctx_kernel(ckv_ref, kr_ref, gk_ref, gv_ref, wk_ref, wv_ref, place_ref, exp_ref,
                km_ref, vm_ref, gko_ref, gvo_ref):
    cb = ckv_ref[...].astype(BF16)
    kn = _dot(cb, wk_ref[...])
    kr = _dot(kr_ref[...].astype(BF16), place_ref[...])
    for hd in range(MLA_HEADS):
        km_ref[:, hd * LANES:(hd + 1) * LANES] = (kn[:, hd * LANES:(hd + 1) * LANES] + kr).astype(BF16)
    vm_ref[...] = _dot(cb, wv_ref[...]).astype(BF16)
    gko_ref[...] = _dot(gk_ref[...].astype(BF16), exp_ref[...]).astype(BF16)
    gvo_ref[...] = _dot(gv_ref[...].astype(BF16), exp_ref[...]).astype(BF16)


def _ctx_prep(layer, cache_ckv, cache_krope, cache_gk, cache_gv, lw, consts):
    nb, _, past, _ = cache_ckv.shape
    cache = lambda w: pl.BlockSpec((None, None, past, w), lambda b: (b, layer, 0, 0))
    const = lambda shape: pl.BlockSpec(shape, lambda b: (0,) * len(shape))
    out = lambda w: pl.BlockSpec((None, past, w), lambda b: (b, 0, 0))
    return pl.pallas_call(
        _ctx_kernel,
        grid=(nb,),
        in_specs=[cache(MLA_KV_RANK), cache(MLA_ROPE_DIM), cache(GQA_KV_WIDTH), cache(GQA_KV_WIDTH),
                  const((MLA_KV_RANK, MLA_PAD_WIDTH)), const((MLA_KV_RANK, MLA_HEADS * MLA_V_DIM)),
                  const((MLA_ROPE_DIM, LANES)), const((GQA_KV_WIDTH, GQA_WIDTH))],
        out_specs=[out(MLA_PAD_WIDTH), out(MLA_HEADS * MLA_V_DIM), out(GQA_WIDTH), out(GQA_WIDTH)],
        out_shape=[jax.ShapeDtypeStruct((nb, past, MLA_PAD_WIDTH), BF16),
                   jax.ShapeDtypeStruct((nb, past, MLA_HEADS * MLA_V_DIM), BF16),
                   jax.ShapeDtypeStruct((nb, past, GQA_WIDTH), BF16),
                   jax.ShapeDtypeStruct((nb, past, GQA_WIDTH), BF16)],
        compiler_params=pltpu.CompilerParams(dimension_semantics=("arbitrary",), vmem_limit_bytes=VMEM_LIMIT),
        name="ctx_prep",
    )(cache_ckv, cache_krope, cache_gk.reshape(nb, DEPTH, past, GQA_KV_WIDTH),
      cache_gv.reshape(nb, DEPTH, past, GQA_KV_WIDTH), lw["wk"], lw["wv"], consts["place"], consts["expand"])


def _attn_kernel(*refs, n_seg, head_tiles, pairs):
    q_ref, o_ref = refs[0], refs[-1]
    k_refs = refs[1:1 + n_seg]
    v_refs = refs[1 + n_seg:1 + 2 * n_seg]
    lane = lax.broadcasted_iota(jnp.int32, (q_ref.shape[0], LANES), 1)
    first_half = lane < (LANES // 2)
    for p in range(pairs):
        outs = []
        for hd in range(2):
            if head_tiles == 2:
                cols = slice((2 * p + hd) * LANES, (2 * p + hd + 1) * LANES)
                qh = q_ref[:, cols]
            else:
                cols = slice(p * LANES, (p + 1) * LANES)
                q = q_ref[:, cols]
                qh = jnp.where(first_half if hd == 0 else ~first_half, q, jnp.zeros_like(q))
            scores = [lax.dot_general(qh, k_ref[:, cols], (((1,), (1,)), ((), ())), preferred_element_type=F32)
                      for k_ref in k_refs]
            m = functools.reduce(jnp.maximum, [jnp.max(s, axis=-1, keepdims=True) for s in scores])
            probs = [jnp.exp2(s - m) for s in scores]
            denom = functools.reduce(jnp.add, [jnp.sum(e, axis=-1, keepdims=True) for e in probs])
            acc = functools.reduce(jnp.add, [_dot(e.astype(BF16), v_ref[:, p * LANES:(p + 1) * LANES])
                                             for e, v_ref in zip(probs, v_refs)])
            outs.append(acc * (1.0 / denom))
        o_ref[:, p * LANES:(p + 1) * LANES] = jnp.where(first_half, outs[0], outs[1]).astype(o_ref.dtype)


def _attention(q, kv_segments, *, n_batch, head_tiles, pairs=HEAD_PAIRS, tq=256):
    sq = q.shape[1]
    qw = pairs * head_tiles * LANES
    n_seg = len(kv_segments)
    in_specs = [pl.BlockSpec((None, tq, qw), lambda b, p, i: (b, i, p))]
    in_specs += [pl.BlockSpec((None, k.shape[1], qw), lambda b, p, i: (b, 0, p)) for k, _ in kv_segments]
    in_specs += [pl.BlockSpec((None, v.shape[1], pairs * LANES), lambda b, p, i: (b, 0, p)) for _, v in kv_segments]
    return pl.pallas_call(
        functools.partial(_attn_kernel, n_seg=n_seg, head_tiles=head_tiles, pairs=pairs),
        grid=(n_batch, HEAD_PAIRS // pairs, sq // tq),
        in_specs=in_specs,
        out_specs=pl.BlockSpec((None, tq, pairs * LANES), lambda b, p, i: (b, i, p)),
        out_shape=jax.ShapeDtypeStruct((n_batch, sq, HEAD_PAIRS * LANES), BF16),
        compiler_params=pltpu.CompilerParams(dimension_semantics=("arbitrary",) * 3, vmem_limit_bytes=VMEM_LIMIT),
        name=f"attn_t{head_tiles}_s{n_seg}",
    )(q, *[k for k, _ in kv_segments], *[v for _, v in kv_segments])


def _fnet_kernel(ca_ref, sa_ref, cb_ref, sb_ref, uc_ref, us_ref, o_ref, tc_ref, ts_ref, *, scale):
    @pl.when(pl.program_id(1) == 0)
    def _build_twiddles():
        cb, sb = cb_ref[...], sb_ref[...]
        for a in range(ca_ref.shape[0]):
            ca, sa = ca_ref[a:a + 1, :], sa_ref[a:a + 1, :]
            rows = slice(a * DFT_SPLIT, (a + 1) * DFT_SPLIT)
            tc_ref[rows, :] = (ca * cb - sa * sb).astype(BF16)
            ts_ref[rows, :] = (sa * cb + ca * sb).astype(BF16)

    acc = _dot(tc_ref[...], uc_ref[...]) - _dot(ts_ref[...], us_ref[...])
    o_ref[...] = (acc * scale).astype(o_ref.dtype)


def _fnet(tables, uc, us, tm, tn):
    seq, width = uc.shape
    scale = float((seq * FNET_GROUP_DIM) ** -0.5)
    n_a = tm // DFT_SPLIT
    part_a = pl.BlockSpec((n_a, seq), lambda i, j: (i, 0))
    part_b = pl.BlockSpec((DFT_SPLIT, seq), lambda i, j: (0, 0))
    data = pl.BlockSpec((seq, tn), lambda i, j: (0, j))
    return pl.pallas_call(
        functools.partial(_fnet_kernel, scale=scale),
        grid=(seq // tm, width // tn),
        in_specs=[part_a, part_a, part_b, part_b, data, data],
        out_specs=pl.BlockSpec((tm, tn), lambda i, j: (i, j)),
        out_shape=jax.ShapeDtypeStruct((seq, width), BF16),
        scratch_shapes=[pltpu.VMEM((tm, seq), BF16), pltpu.VMEM((tm, seq), BF16)],
        compiler_params=pltpu.CompilerParams(dimension_semantics=("arbitrary", "arbitrary"),
                                             vmem_limit_bytes=VMEM_LIMIT),
        name="fnet",
    )(*tables, uc, us)


def _post_kernel(x_ref, mla_ref, fn_ref, gqa_ref, mod_ref, g2_ref, wo_ref, wg_ref, wu_ref, wd_ref, gf_ref,
                 o_ref, *, final):
    mix = jnp.concatenate([mla_ref[...], fn_ref[...], gqa_ref[...]], axis=-1)
    gate1 = mod_ref[:, 2 * D_MODEL:3 * D_MODEL]
    shift2 = mod_ref[:, 3 * D_MODEL:4 * D_MODEL]
    scale2 = mod_ref[:, 4 * D_MODEL:5 * D_MODEL]
    gate2 = mod_ref[:, 5 * D_MODEL:6 * D_MODEL]
    x = x_ref[...] + gate1 * _dot(mix, wo_ref[...])
    h = (_rms(x) * g2_ref[...] * (1.0 + scale2) + shift2).astype(BF16)
    g = _dot(h, wg_ref[...])
    u = _dot(h, wu_ref[...])
    a = (g * jax.nn.sigmoid(g) * u).astype(BF16)
    x = x + gate2 * _dot(a, wd_ref[...])
    if final:
        x = _rms(x) * gf_ref[...]
    o_ref[...] = x


def _post(x, mla_o, fnet_o, gqa_o, mod, lw, g_final, *, seq, final, tm=256):
    n_tok = x.shape[0]
    n_tiles = n_tok // tm
    tps = seq // tm
    tiles_per_mod = n_tiles // mod.shape[0]
    const = lambda shape: pl.BlockSpec(shape, lambda i: (0,) * len(shape), pipeline_mode=pl.Buffered(1))
    tok = lambda w: pl.BlockSpec((tm, w), lambda i: (i, 0))
    return pl.pallas_call(
        functools.partial(_post_kernel, final=final),
        grid=(n_tiles,),
        in_specs=[tok(D_MODEL), tok(HEAD_PAIRS * LANES),
                  pl.BlockSpec((tm, FNET_WIDTH), lambda i: (i % tps, i // tps)),
                  tok(HEAD_PAIRS * LANES),
                  pl.BlockSpec((None, 1, 6 * D_MODEL), lambda i: (i // tiles_per_mod, 0, 0)),
                  const((1, D_MODEL)), const((D_MODEL, D_MODEL)), const((D_MODEL, D_FF)),
                  const((D_MODEL, D_FF)), const((D_FF, D_MODEL)), const((1, D_MODEL))],
        out_specs=tok(D_MODEL),
        out_shape=jax.ShapeDtypeStruct((n_tok, D_MODEL), F32),
        compiler_params=pltpu.CompilerParams(dimension_semantics=("arbitrary",), vmem_limit_bytes=VMEM_LIMIT),
        name="post_final" if final else "post",
    )(x, mla_o, fnet_o, gqa_o, mod, lw["g2"], lw["w_out"], lw["w_gate"], lw["w_up"], lw["w_down"], g_final)


def _layer_weights(l, g_norm1, g_norm2, w_in, g_q_a, w_q_up, g_kv_a, w_kv_up, g_q_head, g_k_head,
                   w_out, w_ffn_gate, w_ffn_up, w_ffn_down):
    w = w_in[l]
    o = np.cumsum([0, MLA_Q_RANK, MLA_KV_RANK, MLA_ROPE_DIM, FNET_WIDTH, GQA_WIDTH, GQA_KV_WIDTH, GQA_KV_WIDTH])
    cq, ckv, kr, u, gq, gk, gv = (w[:, o[i]:o[i + 1]] for i in range(7))
    kr_tile = jnp.pad(kr, ((0, 0), (0, LANES - MLA_ROPE_DIM)))
    w_in_p = jnp.concatenate([cq, ckv, u, gq, gk, gv, kr_tile], axis=1).astype(BF16)
    wq = w_q_up[l].reshape(MLA_Q_RANK, MLA_HEADS, MLA_QK_DIM)
    wq = jnp.concatenate([wq[:, :, MLA_NOPE_DIM:], wq[:, :, :MLA_NOPE_DIM]], axis=-1)
    wq = jnp.pad(wq, ((0, 0), (0, 0), (0, LANES - MLA_QK_DIM)))
    wkv = w_kv_up[l].reshape(MLA_KV_RANK, MLA_HEADS, MLA_NOPE_DIM + MLA_V_DIM)
    wk = jnp.pad(wkv[:, :, :MLA_NOPE_DIM], ((0, 0), (0, 0), (MLA_ROPE_DIM, LANES - MLA_QK_DIM)))
    wv = wkv[:, :, MLA_NOPE_DIM:]
    return {
        "g1": g_norm1[l].reshape(1, D_MODEL), "g2": g_norm2[l].reshape(1, D_MODEL),
        "w_in": w_in_p,
        "g_q_a": g_q_a[l].reshape(1, MLA_Q_RANK), "g_kv_a": g_kv_a[l].reshape(1, MLA_KV_RANK),
        "wq": wq.reshape(MLA_Q_RANK, MLA_PAD_WIDTH).astype(BF16),
        "wk": wk.reshape(MLA_KV_RANK, MLA_PAD_WIDTH).astype(BF16),
        "wv": wv.reshape(MLA_KV_RANK, MLA_HEADS * MLA_V_DIM).astype(BF16),
        "g_q_head": jnp.tile(g_q_head[l], GQA_HEADS).reshape(1, GQA_WIDTH),
        "g_k_head": jnp.tile(g_k_head[l], GQA_KV_HEADS).reshape(1, GQA_KV_WIDTH),
        "w_out": w_out[l].astype(BF16), "w_gate": w_ffn_gate[l].astype(BF16),
        "w_up": w_ffn_up[l].astype(BF16), "w_down": w_ffn_down[l].astype(BF16),
    }


def kernel(x_prompt, x_sample, cache_mla_ckv, cache_mla_krope, cache_gqa_k, cache_gqa_v, c, c_ctx, w_ada, b_ada,
           g_norm1, g_norm2, w_in, g_q_a, w_q_up, g_kv_a, w_kv_up, g_q_head, g_k_head, w_out,
           w_ffn_gate, w_ffn_up, w_ffn_down, g_final):
    n_pb, p_seq, _ = x_prompt.shape
    n_sb, s_seq, _ = x_sample.shape
    consts = {"wdft": _channel_dft(), "mean_q": _group_mean_matrix(GQA_HEADS),
              "mean_k": _group_mean_matrix(GQA_KV_HEADS), "expand": _kv_expand_matrix(),
              "place": _rope_place_matrix(), "rope": _rope_tables(s_seq)}
    dft_p = _seq_dft_tables(p_seq)
    dft_s = _seq_dft_tables(s_seq)
    gf = g_final.reshape(1, D_MODEL)

    cond_t = jnp.concatenate([c_ctx[None, :], c], axis=0).T
    mod = _ada(cond_t, w_ada, b_ada)

    xp = x_prompt.reshape(n_pb * p_seq, D_MODEL)
    xs = x_sample.reshape(n_sb * s_seq, D_MODEL)
    caches = None
    for l in range(DEPTH):
        lw = _layer_weights(l, g_norm1, g_norm2, w_in, g_q_a, w_q_up, g_kv_a, w_kv_up, g_q_head, g_k_head,
                            w_out, w_ffn_gate, w_ffn_up, w_ffn_down)
        final = l == DEPTH - 1
        mod_p = mod[l, 0:1].reshape(1, 1, 6 * D_MODEL)
        mod_s = mod[l, 1:].reshape(n_sb, 1, 6 * D_MODEL)

        qm, km, vm, uc, us, gq, gk, gv, *caches = _proj(
            xp, mod_p, lw, consts, seq=p_seq, rope=False, emit_cache=True, prev_cache=caches)
        b3 = lambda a, nb: a.reshape(nb, a.shape[0] // nb, a.shape[1])
        mla_o = _attention(b3(qm, n_pb), [(b3(km, n_pb), b3(vm, n_pb))], n_batch=n_pb, head_tiles=2)
        gqa_o = _attention(b3(gq, n_pb), [(b3(gk, n_pb), b3(gv, n_pb))], n_batch=n_pb, head_tiles=1)
        fn_o = _fnet(dft_p, uc, us, tm=p_seq, tn=1024)
        xp = _post(xp, mla_o.reshape(-1, GQA_WIDTH), fn_o, gqa_o.reshape(-1, GQA_WIDTH), mod_p, lw, gf,
                   seq=p_seq, final=final)

        km_c, vm_c, gk_cx, gv_cx = _ctx_prep(l, cache_mla_ckv, cache_mla_krope, cache_gqa_k, cache_gqa_v, lw, consts)
        qm, km, vm, uc, us, gq, gk, gv = _proj(xs, mod_s, lw, consts, seq=s_seq, rope=True)
        mla_o = _attention(b3(qm, n_sb), [(km_c, vm_c), (b3(km, n_sb), b3(vm, n_sb))], n_batch=n_sb, head_tiles=2)
        gqa_o = _attention(b3(gq, n_sb), [(gk_cx, gv_cx), (b3(gk, n_sb), b3(gv, n_sb))], n_batch=n_sb, head_tiles=1)
        fn_o = _fnet(dft_s, uc, us, tm=512, tn=n_sb * FNET_WIDTH)
        xs = _post(xs, mla_o.reshape(-1, GQA_WIDTH), fn_o, gqa_o.reshape(-1, GQA_WIDTH), mod_s, lw, gf,
                   seq=s_seq, final=final)

    ckv_new, krope_new, gk_new, gv_new = caches
    heads = (n_pb, DEPTH, p_seq, GQA_KV_HEADS, GQA_HEAD_DIM)
    return (xp.reshape(n_pb, p_seq, D_MODEL), xs.reshape(n_sb, s_seq, D_MODEL),
            ckv_new, krope_new, gk_new.reshape(heads), gv_new.reshape(heads))
```

```python
import functools

import numpy as np
import jax
import jax.numpy as jnp
from jax import lax
from jax.experimental import pallas as pl
from jax.experimental.pallas import tpu as pltpu

D_MODEL = 1024
DEPTH = 2
GRID_W = 64
MLA_HEADS = 6
MLA_Q_RANK = 384
MLA_KV_RANK = 256
MLA_NOPE_DIM = 64
MLA_ROPE_DIM = 32
MLA_V_DIM = 64
MLA_QK_DIM = MLA_NOPE_DIM + MLA_ROPE_DIM
FNET_GROUPS = 4
FNET_GROUP_DIM = 64
FNET_WIDTH = FNET_GROUPS * FNET_GROUP_DIM
GQA_HEADS = 6
GQA_KV_HEADS = 2
GQA_HEAD_DIM = 64
GQA_GROUP = GQA_HEADS // GQA_KV_HEADS
GQA_WIDTH = GQA_HEADS * GQA_HEAD_DIM
GQA_KV_WIDTH = GQA_KV_HEADS * GQA_HEAD_DIM
D_FF = 2816
ROPE_THETA = 10000.0
EPS = 1e-6
LOG2_E = 1.4426950408889634

LANES = 128
HALF = LANES // 2
HEAD_PAIRS = MLA_HEADS // 2
MLA_PAD_WIDTH = MLA_HEADS * LANES
MIX_TILE = HEAD_PAIRS * LANES
VMEM_LIMIT = 52 * 1024 * 1024

_IN_A = MLA_Q_RANK + MLA_KV_RANK
_IN_C0 = _IN_A + MLA_ROPE_DIM
_IN_C = FNET_WIDTH + GQA_WIDTH + 2 * GQA_KV_WIDTH
_C_U, _C_GQ, _C_GK, _C_GV = 0, FNET_WIDTH, FNET_WIDTH + GQA_WIDTH, FNET_WIDTH + GQA_WIDTH + GQA_KV_WIDTH

BF16 = jnp.bfloat16
F32 = jnp.float32


def _rope_tables(n_tokens):
    t = np.arange(n_tokens)
    row = (t // GRID_W).astype(np.float64)
    col = (t % GRID_W).astype(np.float64)

    def angles(rot_dim):
        n_axis = rot_dim // 4
        inv = ROPE_THETA ** (-np.arange(n_axis, dtype=np.float64) / n_axis)
        return np.concatenate([row[:, None] * inv, col[:, None] * inv], axis=-1)

    def tables(ang, lane_to_pair, is_first, is_second):
        cos = np.where((is_first | is_second)[None, :], np.cos(ang)[:, lane_to_pair], 1.0)
        sin = np.sin(ang)[:, lane_to_pair]
        sin_a = np.where(is_second[None, :], sin, 0.0)
        sin_b = np.where(is_first[None, :], -sin, 0.0)
        return [cos, sin_a, sin_b]

    lane = np.arange(LANES)
    half_m = MLA_ROPE_DIM // 2
    in_rope = lane < MLA_ROPE_DIM
    first_m = lane < half_m
    second_m = in_rope & ~first_m
    pair_m = lane % half_m
    half_g = GQA_HEAD_DIM // 2
    first_g = (lane % GQA_HEAD_DIM) < half_g
    pair_g = lane % half_g
    tabs = (tables(angles(MLA_ROPE_DIM), pair_m, first_m, second_m)
            + tables(angles(GQA_HEAD_DIM), pair_g, first_g, ~first_g))
    return jnp.asarray(np.concatenate(tabs, axis=-1), dtype=F32)


def _channel_dft():
    c = np.arange(FNET_GROUP_DIM)
    ang = 2.0 * np.pi * np.outer(c, c) / FNET_GROUP_DIM
    eye = np.eye(FNET_GROUPS)
    table = np.concatenate([np.kron(eye, np.cos(ang)), np.kron(eye, np.sin(ang))], axis=1)
    return jnp.asarray(table, dtype=F32).astype(BF16)


def _group_mean_matrix(n_heads):
    return jnp.asarray(np.kron(np.eye(n_heads), np.full((GQA_HEAD_DIM, GQA_HEAD_DIM), 1.0 / GQA_HEAD_DIM)), dtype=BF16)


def _rope_place_matrix():
    return jnp.asarray(np.eye(MLA_ROPE_DIM, LANES), dtype=BF16)


DFT_SPLIT = 32


def _seq_dft_tables(seq):
    s = np.arange(seq)
    n_a = seq // DFT_SPLIT
    ang_a = 2.0 * np.pi * ((np.arange(n_a)[:, None] * s[None, :]) % n_a) / n_a
    ang_b = 2.0 * np.pi * ((np.arange(DFT_SPLIT)[:, None] * s[None, :]) % seq) / seq
    return tuple(jnp.asarray(t, dtype=F32) for t in (np.cos(ang_a), np.sin(ang_a), np.cos(ang_b), np.sin(ang_b)))


def _rms(x):
    return x * lax.rsqrt(jnp.mean(x * x, axis=-1, keepdims=True) + EPS)


def _dot(a, b):
    return jnp.dot(a, b, preferred_element_type=F32)


def _group_rms(x, mean_mat):
    ms = _dot((x * x).astype(BF16), mean_mat)
    return x * lax.rsqrt(ms + EPS)


def _rotate(x, cos, sin_a, sin_b, half):
    return x * cos + pltpu.roll(x, half, 1) * sin_a + pltpu.roll(x, LANES - half, 1) * sin_b


def _lower_half(rows):
    return lax.broadcasted_iota(jnp.int32, (rows, LANES), 1) < HALF


def _layer_spec(arr, layer):
    return pl.BlockSpec((None,) + arr.shape[1:], lambda i: (layer, 0, 0))


def _ada_kernel(ct_ref, w_ref, b_ref, o_ref):
    s = ct_ref[...]
    s = s * jax.nn.sigmoid(s)
    w = w_ref[...]
    for m in range(o_ref.shape[0]):
        o_ref[m:m + 1, :] = jnp.sum(w * s[:, m:m + 1], axis=0, keepdims=True) + b_ref[...]


def _ada(cond_t, w_ada, b_ada, tn=512):
    n_cond = cond_t.shape[1]
    width = w_ada.shape[2]
    return pl.pallas_call(
        _ada_kernel,
        grid=(DEPTH, width // tn),
        in_specs=[
            pl.BlockSpec((D_MODEL, n_cond), lambda l, j: (0, 0)),
            pl.BlockSpec((None, D_MODEL, tn), lambda l, j: (l, 0, j)),
            pl.BlockSpec((None, 1, tn), lambda l, j: (l, 0, j)),
        ],
        out_specs=pl.BlockSpec((None, n_cond, tn), lambda l, j: (l, 0, j)),
        out_shape=jax.ShapeDtypeStruct((DEPTH, n_cond, width), F32),
        compiler_params=pltpu.CompilerParams(dimension_semantics=("arbitrary", "arbitrary"),
                                             vmem_limit_bytes=VMEM_LIMIT),
        name="ada",
    )(cond_t, w_ada, b_ada.reshape(DEPTH, 1, width))


_PROJ_WEIGHTS = ("g1", "w_a", "w_c", "w_kr", "g_q_a", "wq", "g_kv_a", "wk", "wv", "g_q_head", "g_k_head")


def _proj_kernel(*refs, rope, emit_cache, n_prev):
    (x_ref, mod_ref, g1_ref, wa_ref, wc_ref, wkr_ref, gqa_ref, wq_ref, gkva_ref, wk_ref, wv_ref,
     gqh_ref, gkh_ref, wdft_ref, mq_ref, mk_ref) = refs[:16]
    refs = refs[16:]
    if rope:
        tab_ref, refs = refs[0], refs[1:]
    if n_prev:
        prev_refs, refs = refs[:4], refs[4:]
    qm_ref, km_ref, vm_ref, uc_ref, us_ref, gq_ref, gk_ref, gv_ref = refs[:8]
    refs = refs[8:]

    x = x_ref[...]
    shift = mod_ref[:, 0:D_MODEL]
    scale = mod_ref[:, D_MODEL:2 * D_MODEL]
    hb = (_rms(x) * g1_ref[...] * (1.0 + scale) + shift).astype(BF16)
    pa = _dot(hb, wa_ref[...])
    pc = _dot(hb, wc_ref[...])
    kr = _dot(hb, wkr_ref[...])

    if rope:
        cos_m, sa_m, sb_m = (tab_ref[:, i * LANES:(i + 1) * LANES] for i in range(3))
        cos_g, sa_g, sb_g = (tab_ref[:, i * LANES:(i + 1) * LANES] for i in range(3, 6))

    cqn = _rms(pa[:, :MLA_Q_RANK]) * gqa_ref[...]
    q = _dot(cqn.astype(BF16), wq_ref[...])
    for hd in range(MLA_HEADS):
        qh = q[:, hd * LANES:(hd + 1) * LANES]
        if rope:
            qh = _rotate(qh, cos_m, sa_m, sb_m, MLA_ROPE_DIM // 2)
        qm_ref[:, hd * LANES:(hd + 1) * LANES] = (qh * (LOG2_E * MLA_QK_DIM ** -0.5)).astype(BF16)

    ckvn = _rms(pa[:, MLA_Q_RANK:]) * gkva_ref[...]
    cb = ckvn.astype(BF16)
    kn = _dot(cb, wk_ref[...])
    krr = _rotate(kr, cos_m, sa_m, sb_m, MLA_ROPE_DIM // 2) if rope else kr
    for hd in range(MLA_HEADS):
        km_ref[:, hd * LANES:(hd + 1) * LANES] = (kn[:, hd * LANES:(hd + 1) * LANES] + krr).astype(BF16)
    vm_ref[...] = _dot(cb, wv_ref[...]).astype(BF16)

    ucs = _dot(pc[:, _C_U:_C_U + FNET_WIDTH].astype(BF16), wdft_ref[...])
    uc_ref[...] = ucs[:, :FNET_WIDTH].astype(BF16)
    us_ref[...] = ucs[:, FNET_WIDTH:].astype(BF16)

    gqn = _group_rms(pc[:, _C_GQ:_C_GQ + GQA_WIDTH], mq_ref[...]) * gqh_ref[...]
    tiles = []
    for p in range(HEAD_PAIRS):
        gp = gqn[:, p * LANES:(p + 1) * LANES]
        if rope:
            gp = _rotate(gp, cos_g, sa_g, sb_g, GQA_HEAD_DIM // 2)
        tiles.append(gp * (LOG2_E * GQA_HEAD_DIM ** -0.5))
    lower = _lower_half(x.shape[0])
    gq_ref[:, 0:LANES] = jnp.where(lower, tiles[0], tiles[1]).astype(BF16)
    gq_ref[:, LANES:2 * LANES] = jnp.where(lower, pltpu.roll(tiles[0], HALF, 1), pltpu.roll(tiles[2], HALF, 1)).astype(BF16)
    gq_ref[:, 2 * LANES:] = jnp.where(lower, tiles[1], tiles[2]).astype(BF16)
    gkn = _group_rms(pc[:, _C_GK:_C_GK + GQA_KV_WIDTH], mk_ref[...]) * gkh_ref[...]
    gkr = _rotate(gkn, cos_g, sa_g, sb_g, GQA_HEAD_DIM // 2) if rope else gkn
    gv = pc[:, _C_GV:_C_GV + GQA_KV_WIDTH]
    gk_ref[...] = gkr.astype(BF16)
    gv_ref[...] = gv.astype(BF16)

    if emit_cache:
        new = (ckvn, kr[:, :MLA_ROPE_DIM], gkn, gv)
        for i, out_ref in enumerate(refs):
            if n_prev:
                out_ref[:n_prev] = prev_refs[i][...]
            out_ref[n_prev] = new[i]


def _proj(x, mod, wts, layer, consts, *, seq, rope, prev_cache=None, emit_cache=False, tm=256):
    n_tok = x.shape[0]
    n_tiles = n_tok // tm
    tps = seq // tm
    n_seq = n_tok // seq
    tiles_per_mod = n_tiles // mod.shape[0]
    const = lambda arr: pl.BlockSpec(arr.shape, lambda i: (0,) * arr.ndim)
    tok = lambda w: pl.BlockSpec((tm, w), lambda i: (i, 0))
    in_specs = [tok(D_MODEL), pl.BlockSpec((None, 1, 6 * D_MODEL), lambda i: (i // tiles_per_mod, 0, 0))]
    args = [x, mod]
    for name in _PROJ_WEIGHTS:
        in_specs.append(_layer_spec(wts[name], layer))
        args.append(wts[name])
    for name in ("wdft", "mean_q", "mean_k"):
        in_specs.append(const(consts[name]))
        args.append(consts[name])
    if rope:
        in_specs.append(pl.BlockSpec((tm, 6 * LANES), lambda i: (i % tps, 0)))
        args.append(consts["rope"])
    seq_major = pl.BlockSpec((tm, FNET_WIDTH), lambda i: (i % tps, i // tps))
    out_specs = [tok(MLA_PAD_WIDTH), tok(MLA_PAD_WIDTH), tok(MIX_TILE), seq_major, seq_major,
                 tok(GQA_WIDTH), tok(GQA_KV_WIDTH), tok(GQA_KV_WIDTH)]
    out_shape = [jax.ShapeDtypeStruct((n_tok, MLA_PAD_WIDTH), BF16),
                 jax.ShapeDtypeStruct((n_tok, MLA_PAD_WIDTH), BF16),
                 jax.ShapeDtypeStruct((n_tok, MIX_TILE), BF16),
                 jax.ShapeDtypeStruct((seq, n_seq * FNET_WIDTH), BF16),
                 jax.ShapeDtypeStruct((seq, n_seq * FNET_WIDTH), BF16),
                 jax.ShapeDtypeStruct((n_tok, GQA_WIDTH), BF16),
                 jax.ShapeDtypeStruct((n_tok, GQA_KV_WIDTH), BF16),
                 jax.ShapeDtypeStruct((n_tok, GQA_KV_WIDTH), BF16)]
    n_prev = 0
    if emit_cache:
        assert tps == 1, "cache outputs are written one sequence per token tile"
        n_prev = prev_cache[0].shape[1] if prev_cache is not None else 0
        layers = lambda n, w: pl.BlockSpec((None, n, seq, w), lambda i: (i, 0, 0, 0))
        widths = (MLA_KV_RANK, MLA_ROPE_DIM, GQA_KV_WIDTH, GQA_KV_WIDTH)
        if n_prev:
            in_specs += [layers(n_prev, w) for w in widths]
            args += list(prev_cache)
        out_specs += [layers(n_prev + 1, w) for w in widths]
        out_shape += [jax.ShapeDtypeStruct((n_seq, n_prev + 1, seq, w), F32) for w in widths]
    return pl.pallas_call(
        functools.partial(_proj_kernel, rope=rope, emit_cache=emit_cache, n_prev=n_prev),
        grid=(n_tiles,),
        in_specs=in_specs,
        out_specs=out_specs,
        out_shape=out_shape,
        compiler_params=pltpu.CompilerParams(dimension_semantics=("arbitrary",), vmem_limit_bytes=VMEM_LIMIT),
        name="proj_rope" if rope else "proj_ctx",
    )(*args)


def _ctx_kernel(ckv_ref, kr_ref, gk_ref, gv_ref, wk_ref, wv_ref, place_ref, km_ref, vm_ref, gko_ref, gvo_ref):
    cb = ckv_ref[...].astype(BF16)
    kn = _dot(cb, wk_ref[...])
    kr = _dot(kr_ref[...].astype(BF16), place_ref[...])
    for hd in range(MLA_HEADS):
        km_ref[:, hd * LANES:(hd + 1) * LANES] = (kn[:, hd * LANES:(hd + 1) * LANES] + kr).astype(BF16)
    vm_ref[...] = _dot(cb, wv_ref[...]).astype(BF16)
    gko_ref[...] = gk_ref[...].astype(BF16)
    gvo_ref[...] = gv_ref[...].astype(BF16)


def _ctx_prep(layer, cache_ckv, cache_krope, cache_gk, cache_gv, wts, consts):
    nb, _, past, _ = cache_ckv.shape
    cache = lambda w: pl.BlockSpec((None, None, past, w), lambda b: (b, layer, 0, 0))
    out = lambda w: pl.BlockSpec((None, past, w), lambda b: (b, 0, 0))
    return pl.pallas_call(
        _ctx_kernel,
        grid=(nb,),
        in_specs=[cache(MLA_KV_RANK), cache(MLA_ROPE_DIM), cache(GQA_KV_WIDTH), cache(GQA_KV_WIDTH),
                  _layer_spec(wts["wk"], layer), _layer_spec(wts["wv"], layer),
                  pl.BlockSpec((MLA_ROPE_DIM, LANES), lambda b: (0, 0))],
        out_specs=[out(MLA_PAD_WIDTH), out(MIX_TILE), out(GQA_KV_WIDTH), out(GQA_KV_WIDTH)],
        out_shape=[jax.ShapeDtypeStruct((nb, past, MLA_PAD_WIDTH), BF16),
                   jax.ShapeDtypeStruct((nb, past, MIX_TILE), BF16),
                   jax.ShapeDtypeStruct((nb, past, GQA_KV_WIDTH), BF16),
                   jax.ShapeDtypeStruct((nb, past, GQA_KV_WIDTH), BF16)],
        compiler_params=pltpu.CompilerParams(dimension_semantics=("arbitrary",), vmem_limit_bytes=VMEM_LIMIT),
        name="ctx_prep",
    )(cache_ckv, cache_krope, cache_gk.reshape(nb, DEPTH, past, GQA_KV_WIDTH),
      cache_gv.reshape(nb, DEPTH, past, GQA_KV_WIDTH), wts["wk"], wts["wv"], consts["place"])


def _attn_kernel(*refs, n_seg, mla):
    q_ref, o_ref = refs[0], refs[-1]
    k_refs = refs[1:1 + n_seg]
    v_refs = refs[1 + n_seg:1 + 2 * n_seg]
    lower = _lower_half(q_ref.shape[0])
    if not mla:
        v_wide = [jnp.concatenate([v_ref[...], v_ref[...]], axis=-1) for v_ref in v_refs]

    def scores(p, hd):
        if mla:
            cols = slice((2 * p + hd) * LANES, (2 * p + hd + 1) * LANES)
            qh = q_ref[:, cols]
            keys = [k_ref[:, cols] for k_ref in k_refs]
        else:
            q = q_ref[:, p * LANES:(p + 1) * LANES]
            qh = jnp.where(lower if hd == 0 else ~lower, q, jnp.zeros_like(q))
            keys = [k_ref[...] for k_ref in k_refs]
        return [lax.dot_general(qh, k, (((1,), (1,)), ((), ())), preferred_element_type=F32) for k in keys]

    def softmax(ss):
        m = functools.reduce(jnp.maximum, [jnp.max(s, axis=-1, keepdims=True) for s in ss])
        es = [jnp.exp2(s - m) for s in ss]
        denom = functools.reduce(jnp.add, [jnp.sum(e, axis=-1, keepdims=True) for e in es])
        return [e.astype(BF16) for e in es], denom

    def weighted_values(p, es, denom):
        if mla:
            win = min(p, HEAD_PAIRS - 2)
            vals = [v_ref[:, win * LANES:(win + 2) * LANES] for v_ref in v_refs]
        else:
            win, vals = p, v_wide
        acc = functools.reduce(jnp.add, [_dot(e, v) for e, v in zip(es, vals)])
        return acc[:, (p - win) * LANES:(p - win + 1) * LANES] * (1.0 / denom)

    tiles = []
    for p in range(HEAD_PAIRS):
        both = [scores(p, 0), scores(p, 1)]
        both = [softmax(ss) for ss in both]
        outs = [weighted_values(p, es, denom) for es, denom in both]
        tiles.append(jnp.where(lower, outs[0], outs[1]))
    if not mla:
        t1 = pltpu.roll(tiles[1], HALF, 1)
        tiles = [jnp.where(lower, tiles[0], t1), jnp.where(lower, tiles[2], tiles[0]), jnp.where(lower, t1, tiles[2])]
    for p in range(HEAD_PAIRS):
        o_ref[:, p * LANES:(p + 1) * LANES] = tiles[p].astype(o_ref.dtype)


def _attention(q, kv_segments, *, n_batch, mla, tq=256):
    sq = q.shape[1]
    n_seg = len(kv_segments)
    whole = lambda a: pl.BlockSpec((None,) + a.shape[1:], lambda b, i: (b, 0, 0))
    in_specs = [pl.BlockSpec((None, tq, q.shape[2]), lambda b, i: (b, i, 0))]
    in_specs += [whole(k) for k, _ in kv_segments] + [whole(v) for _, v in kv_segments]
    return pl.pallas_call(
        functools.partial(_attn_kernel, n_seg=n_seg, mla=mla),
        grid=(n_batch, sq // tq),
        in_specs=in_specs,
        out_specs=pl.BlockSpec((None, tq, MIX_TILE), lambda b, i: (b, i, 0)),
        out_shape=jax.ShapeDtypeStruct((n_batch, sq, MIX_TILE), BF16),
        compiler_params=pltpu.CompilerParams(dimension_semantics=("arbitrary",) * 2, vmem_limit_bytes=VMEM_LIMIT),
        name=f"attn_{'mla' if mla else 'gqa'}_s{n_seg}",
    )(q, *[k for k, _ in kv_segments], *[v for _, v in kv_segments])


def _fnet_kernel(ca_ref, sa_ref, cb_ref, sb_ref, uc_ref, us_ref, o_ref, tc_ref, ts_ref, *, scale):
    @pl.when(pl.program_id(1) == 0)
    def _build_twiddles():
        cb, sb = cb_ref[...], sb_ref[...]
        for a in range(ca_ref.shape[0]):
            ca, sa = ca_ref[a:a + 1, :], sa_ref[a:a + 1, :]
            rows = slice(a * DFT_SPLIT, (a + 1) * DFT_SPLIT)
            tc_ref[rows, :] = (ca * cb - sa * sb).astype(BF16)
            ts_ref[rows, :] = (sa * cb + ca * sb).astype(BF16)

    acc = _dot(tc_ref[...], uc_ref[...]) - _dot(ts_ref[...], us_ref[...])
    o_ref[...] = (acc * scale).astype(o_ref.dtype)


def _fnet(tables, uc, us, tm, tn):
    seq, width = uc.shape
    scale = float((seq * FNET_GROUP_DIM) ** -0.5)
    n_a = tm // DFT_SPLIT
    part_a = pl.BlockSpec((n_a, seq), lambda i, j: (i, 0))
    part_b = pl.BlockSpec((DFT_SPLIT, seq), lambda i, j: (0, 0))
    data = pl.BlockSpec((seq, tn), lambda i, j: (0, j))
    return pl.pallas_call(
        functools.partial(_fnet_kernel, scale=scale),
        grid=(seq // tm, width // tn),
        in_specs=[part_a, part_a, part_b, part_b, data, data],
        out_specs=pl.BlockSpec((tm, tn), lambda i, j: (i, j)),
        out_shape=jax.ShapeDtypeStruct((seq, width), BF16),
        scratch_shapes=[pltpu.VMEM((tm, seq), BF16), pltpu.VMEM((tm, seq), BF16)],
        compiler_params=pltpu.CompilerParams(dimension_semantics=("arbitrary", "arbitrary"),
                                             vmem_limit_bytes=VMEM_LIMIT),
        name="fnet",
    )(*tables, uc, us)


def _post_kernel(x_ref, mla_ref, fn_ref, gqa_ref, mod_ref, g2_ref, wo_ref, wg_ref, wu_ref, wd_ref, gf_ref,
                 o_ref, *, final):
    mix = jnp.concatenate([mla_ref[...], fn_ref[...], gqa_ref[...]], axis=-1)
    gate1 = mod_ref[:, 2 * D_MODEL:3 * D_MODEL]
    shift2 = mod_ref[:, 3 * D_MODEL:4 * D_MODEL]
    scale2 = mod_ref[:, 4 * D_MODEL:5 * D_MODEL]
    gate2 = mod_ref[:, 5 * D_MODEL:6 * D_MODEL]
    x = x_ref[...] + gate1 * _dot(mix, wo_ref[...])
    h = (_rms(x) * g2_ref[...] * (1.0 + scale2) + shift2).astype(BF16)
    g = _dot(h, wg_ref[...])
    u = _dot(h, wu_ref[...])
    a = (g * jax.nn.sigmoid(g) * u).astype(BF16)
    x = x + gate2 * _dot(a, wd_ref[...])
    if final:
        x = _rms(x) * gf_ref[...]
    o_ref[...] = x


def _post(x, mla_o, fnet_o, gqa_o, mod, wts, layer, g_final, *, seq, final, tm=256):
    n_tok = x.shape[0]
    n_tiles = n_tok // tm
    tps = seq // tm
    tiles_per_mod = n_tiles // mod.shape[0]
    resident = lambda arr: pl.BlockSpec((None,) + arr.shape[1:], lambda i: (layer, 0, 0), pipeline_mode=pl.Buffered(1))
    tok = lambda w: pl.BlockSpec((tm, w), lambda i: (i, 0))
    names = ("g2", "w_out", "w_gate", "w_up", "w_down")
    return pl.pallas_call(
        functools.partial(_post_kernel, final=final),
        grid=(n_tiles,),
        in_specs=[tok(D_MODEL), tok(MIX_TILE),
                  pl.BlockSpec((tm, FNET_WIDTH), lambda i: (i % tps, i // tps)),
                  tok(MIX_TILE),
                  pl.BlockSpec((None, 1, 6 * D_MODEL), lambda i: (i // tiles_per_mod, 0, 0))]
                 + [resident(wts[n]) for n in names]
                 + [pl.BlockSpec((1, D_MODEL), lambda i: (0, 0))],
        out_specs=tok(D_MODEL),
        out_shape=jax.ShapeDtypeStruct((n_tok, D_MODEL), F32),
        compiler_params=pltpu.CompilerParams(dimension_semantics=("arbitrary",), vmem_limit_bytes=VMEM_LIMIT),
        name="post_final" if final else "post",
    )(x, mla_o, fnet_o, gqa_o, mod, *[wts[n] for n in names], g_final)


def _prep_weights(g_norm1, g_norm2, w_in, g_q_a, w_q_up, g_kv_a, w_kv_up, g_q_head, g_k_head,
                  w_out, w_ffn_gate, w_ffn_up, w_ffn_down):
    row = lambda g: g.reshape(DEPTH, 1, -1)
    w_kr = jnp.pad(w_in[:, :, _IN_A:_IN_C0], ((0, 0), (0, 0), (0, LANES - MLA_ROPE_DIM)))
    wq = w_q_up.reshape(DEPTH, MLA_Q_RANK, MLA_HEADS, MLA_QK_DIM)
    wq = jnp.concatenate([wq[..., MLA_NOPE_DIM:], wq[..., :MLA_NOPE_DIM]], axis=-1)
    wq = jnp.pad(wq, ((0, 0), (0, 0), (0, 0), (0, LANES - MLA_QK_DIM)))
    wkv = w_kv_up.reshape(DEPTH, MLA_KV_RANK, MLA_HEADS, MLA_NOPE_DIM + MLA_V_DIM)
    wk = jnp.pad(wkv[..., :MLA_NOPE_DIM], ((0, 0), (0, 0), (0, 0), (MLA_ROPE_DIM, LANES - MLA_QK_DIM)))
    wv = wkv[..., MLA_NOPE_DIM:]
    return {
        "g1": row(g_norm1), "g2": row(g_norm2), "g_q_a": row(g_q_a), "g_kv_a": row(g_kv_a),
        "g_q_head": row(jnp.tile(g_q_head, (1, GQA_HEADS))), "g_k_head": row(jnp.tile(g_k_head, (1, GQA_KV_HEADS))),
        "w_a": w_in[:, :, :_IN_A].astype(BF16), "w_c": w_in[:, :, _IN_C0:].astype(BF16), "w_kr": w_kr.astype(BF16),
        "wq": wq.reshape(DEPTH, MLA_Q_RANK, MLA_PAD_WIDTH).astype(BF16),
        "wk": wk.reshape(DEPTH, MLA_KV_RANK, MLA_PAD_WIDTH).astype(BF16),
        "wv": wv.reshape(DEPTH, MLA_KV_RANK, MIX_TILE).astype(BF16),
        "w_out": w_out.astype(BF16), "w_gate": w_ffn_gate.astype(BF16),
        "w_up": w_ffn_up.astype(BF16), "w_down": w_ffn_down.astype(BF16),
    }


def kernel(x_prompt, x_sample, cache_mla_ckv, cache_mla_krope, cache_gqa_k, cache_gqa_v, c, c_ctx, w_ada, b_ada,
           g_norm1, g_norm2, w_in, g_q_a, w_q_up, g_kv_a, w_kv_up, g_q_head, g_k_head, w_out,
           w_ffn_gate, w_ffn_up, w_ffn_down, g_final):
    n_pb, p_seq, _ = x_prompt.shape
    n_sb, s_seq, _ = x_sample.shape
    consts = {"wdft": _channel_dft(), "mean_q": _group_mean_matrix(GQA_HEADS),
              "mean_k": _group_mean_matrix(GQA_KV_HEADS), "place": _rope_place_matrix(),
              "rope": _rope_tables(s_seq)}
    dft_p = _seq_dft_tables(p_seq)
    dft_s = _seq_dft_tables(s_seq)
    gf = g_final.reshape(1, D_MODEL)
    wts = _prep_weights(g_norm1, g_norm2, w_in, g_q_a, w_q_up, g_kv_a, w_kv_up, g_q_head, g_k_head,
                        w_out, w_ffn_gate, w_ffn_up, w_ffn_down)

    cond_t = jnp.concatenate([c_ctx[None, :], c], axis=0).T
    mod = _ada(cond_t, w_ada, b_ada)

    xp = x_prompt.reshape(n_pb * p_seq, D_MODEL)
    xs = x_sample.reshape(n_sb * s_seq, D_MODEL)
    b3 = lambda a, nb: a.reshape(nb, a.shape[0] // nb, a.shape[1])
    caches = None
    for l in range(DEPTH):
        final = l == DEPTH - 1
        mod_p = mod[l, 0:1].reshape(1, 1, 6 * D_MODEL)
        mod_s = mod[l, 1:].reshape(n_sb, 1, 6 * D_MODEL)

        qm, km, vm, uc, us, gq, gk, gv, *caches = _proj(
            xp, mod_p, wts, l, consts, seq=p_seq, rope=False, emit_cache=True, prev_cache=caches)
        mla_o = _attention(b3(qm, n_pb), [(b3(km, n_pb), b3(vm, n_pb))], n_batch=n_pb, mla=True)
        gqa_o = _attention(b3(gq, n_pb), [(b3(gk, n_pb), b3(gv, n_pb))], n_batch=n_pb, mla=False)
        fn_o = _fnet(dft_p, uc, us, tm=p_seq, tn=1024)
        xp = _post(xp, mla_o.reshape(-1, MIX_TILE), fn_o, gqa_o.reshape(-1, MIX_TILE), mod_p, wts, l, gf,
                   seq=p_seq, final=final)

        km_c, vm_c, gk_c, gv_c = _ctx_prep(l, cache_mla_ckv, cache_mla_krope, cache_gqa_k, cache_gqa_v, wts, consts)
        qm, km, vm, uc, us, gq, gk, gv = _proj(xs, mod_s, wts, l, consts, seq=s_seq, rope=True)
        mla_o = _attention(b3(qm, n_sb), [(km_c, vm_c), (b3(km, n_sb), b3(vm, n_sb))], n_batch=n_sb, mla=True)
        gqa_o = _attention(b3(gq, n_sb), [(gk_c, gv_c), (b3(gk, n_sb), b3(gv, n_sb))], n_batch=n_sb, mla=False)
        fn_o = _fnet(dft_s, uc, us, tm=512, tn=n_sb * FNET_WIDTH)
        xs = _post(xs, mla_o.reshape(-1, MIX_TILE), fn_o, gqa_o.reshape(-1, MIX_TILE), mod_s, wts, l, gf,
                   seq=s_seq, final=final)

    ckv_new, krope_new, gk_new, gv_new = caches
    heads = (n_pb, DEPTH, p_seq, GQA_KV_HEADS, GQA_HEAD_DIM)
    return (xp.reshape(n_pb, p_seq, D_MODEL), xs.reshape(n_sb, s_seq, D_MODEL),
            ckv_new, krope_new, gk_new.reshape(heads), gv_new.reshape(heads))
```

```python
import functools

import numpy as np
import jax
import jax.numpy as jnp
from jax import lax
from jax.experimental import pallas as pl
from jax.experimental.pallas import tpu as pltpu

D_MODEL = 1024
DEPTH = 2
GRID_W = 64
MLA_HEADS = 6
MLA_Q_RANK = 384
MLA_KV_RANK = 256
MLA_NOPE_DIM = 64
MLA_ROPE_DIM = 32
MLA_V_DIM = 64
MLA_QK_DIM = MLA_NOPE_DIM + MLA_ROPE_DIM
FNET_GROUPS = 4
FNET_GROUP_DIM = 64
FNET_WIDTH = FNET_GROUPS * FNET_GROUP_DIM
GQA_HEADS = 6
GQA_KV_HEADS = 2
GQA_HEAD_DIM = 64
GQA_GROUP = GQA_HEADS // GQA_KV_HEADS
GQA_WIDTH = GQA_HEADS * GQA_HEAD_DIM
GQA_KV_WIDTH = GQA_KV_HEADS * GQA_HEAD_DIM
D_FF = 2816
ROPE_THETA = 10000.0
EPS = 1e-6
LOG2_E = 1.4426950408889634

LANES = 128
HALF = LANES // 2
HEAD_PAIRS = MLA_HEADS // 2
MLA_PAD_WIDTH = MLA_HEADS * LANES
MIX_TILE = HEAD_PAIRS * LANES
VMEM_LIMIT = 52 * 1024 * 1024
POST_TILE = 512

_IN_A = MLA_Q_RANK + MLA_KV_RANK
_IN_C0 = _IN_A + MLA_ROPE_DIM
_IN_C = FNET_WIDTH + GQA_WIDTH + 2 * GQA_KV_WIDTH
_C_U, _C_GQ, _C_GK, _C_GV = 0, FNET_WIDTH, FNET_WIDTH + GQA_WIDTH, FNET_WIDTH + GQA_WIDTH + GQA_KV_WIDTH

BF16 = jnp.bfloat16
F32 = jnp.float32


def _rope_tables(n_tokens):
    t = np.arange(n_tokens)
    row = (t // GRID_W).astype(np.float64)
    col = (t % GRID_W).astype(np.float64)

    def angles(rot_dim):
        n_axis = rot_dim // 4
        inv = ROPE_THETA ** (-np.arange(n_axis, dtype=np.float64) / n_axis)
        return np.concatenate([row[:, None] * inv, col[:, None] * inv], axis=-1)

    def tables(ang, lane_to_pair, is_first, is_second):
        cos = np.where((is_first | is_second)[None, :], np.cos(ang)[:, lane_to_pair], 1.0)
        sin = np.sin(ang)[:, lane_to_pair]
        sin_a = np.where(is_second[None, :], sin, 0.0)
        sin_b = np.where(is_first[None, :], -sin, 0.0)
        return [cos, sin_a, sin_b]

    lane = np.arange(LANES)
    half_m = MLA_ROPE_DIM // 2
    in_rope = lane < MLA_ROPE_DIM
    first_m = lane < half_m
    second_m = in_rope & ~first_m
    pair_m = lane % half_m
    half_g = GQA_HEAD_DIM // 2
    first_g = (lane % GQA_HEAD_DIM) < half_g
    pair_g = lane % half_g
    tabs = (tables(angles(MLA_ROPE_DIM), pair_m, first_m, second_m)
            + tables(angles(GQA_HEAD_DIM), pair_g, first_g, ~first_g))
    return jnp.asarray(np.concatenate(tabs, axis=-1), dtype=F32)


def _channel_dft():
    c = np.arange(FNET_GROUP_DIM)
    ang = 2.0 * np.pi * np.outer(c, c) / FNET_GROUP_DIM
    eye = np.eye(FNET_GROUPS)
    table = np.concatenate([np.kron(eye, np.cos(ang)), np.kron(eye, np.sin(ang))], axis=1)
    return jnp.asarray(table, dtype=F32).astype(BF16)


def _group_mean_matrix(n_heads):
    return jnp.asarray(np.kron(np.eye(n_heads), np.full((GQA_HEAD_DIM, GQA_HEAD_DIM), 1.0 / GQA_HEAD_DIM)), dtype=BF16)


def _rope_place_matrix():
    return jnp.asarray(np.eye(MLA_ROPE_DIM, LANES), dtype=BF16)


DFT_SPLIT = 32


def _seq_dft_tables(seq):
    s = np.arange(seq)
    n_a = seq // DFT_SPLIT
    ang_a = 2.0 * np.pi * ((np.arange(n_a)[:, None] * s[None, :]) % n_a) / n_a
    ang_b = 2.0 * np.pi * ((np.arange(DFT_SPLIT)[:, None] * s[None, :]) % seq) / seq
    return tuple(jnp.asarray(t, dtype=F32) for t in (np.cos(ang_a), np.sin(ang_a), np.cos(ang_b), np.sin(ang_b)))


def _rms(x):
    return x * lax.rsqrt(jnp.mean(x * x, axis=-1, keepdims=True) + EPS)


def _dot(a, b):
    return jnp.dot(a, b, preferred_element_type=F32)


def _group_rms(x, mean_mat):
    ms = _dot((x * x).astype(BF16), mean_mat)
    return x * lax.rsqrt(ms + EPS)


def _rotate(x, cos, sin_a, sin_b, half):
    return x * cos + pltpu.roll(x, half, 1) * sin_a + pltpu.roll(x, LANES - half, 1) * sin_b


def _lower_half(rows):
    return lax.broadcasted_iota(jnp.int32, (rows, LANES), 1) < HALF


def _layer_spec(arr, layer):
    return pl.BlockSpec((None,) + arr.shape[1:], lambda i: (layer, 0, 0))


def _ada_kernel(ct_ref, w_ref, b_ref, o_ref):
    s = ct_ref[...]
    s = s * jax.nn.sigmoid(s)
    w = w_ref[...]
    for m in range(o_ref.shape[0]):
        o_ref[m:m + 1, :] = jnp.sum(w * s[:, m:m + 1], axis=0, keepdims=True) + b_ref[...]


def _ada(cond_t, w_ada, b_ada, tn=2048):
    n_cond = cond_t.shape[1]
    width = w_ada.shape[2]
    return pl.pallas_call(
        _ada_kernel,
        grid=(DEPTH, width // tn),
        in_specs=[
            pl.BlockSpec((D_MODEL, n_cond), lambda l, j: (0, 0)),
            pl.BlockSpec((None, D_MODEL, tn), lambda l, j: (l, 0, j)),
            pl.BlockSpec((None, 1, tn), lambda l, j: (l, 0, j)),
        ],
        out_specs=pl.BlockSpec((None, n_cond, tn), lambda l, j: (l, 0, j)),
        out_shape=jax.ShapeDtypeStruct((DEPTH, n_cond, width), F32),
        compiler_params=pltpu.CompilerParams(dimension_semantics=("arbitrary", "arbitrary"),
                                             vmem_limit_bytes=VMEM_LIMIT),
        name="ada",
    )(cond_t, w_ada, b_ada.reshape(DEPTH, 1, width))


_PROJ_WEIGHTS = ("g1", "w_a", "w_c", "w_kr", "g_q_a", "wq", "g_kv_a", "wk", "wv", "g_q_head", "g_k_head")


def _proj_kernel(*refs, rope, emit_cache, n_prev):
    (x_ref, mod_ref, g1_ref, wa_ref, wc_ref, wkr_ref, gqa_ref, wq_ref, gkva_ref, wk_ref, wv_ref,
     gqh_ref, gkh_ref, wdft_ref, mq_ref, mk_ref) = refs[:16]
    refs = refs[16:]
    if rope:
        tab_ref, refs = refs[0], refs[1:]
    if n_prev:
        prev_refs, refs = refs[:4], refs[4:]
    qm_ref, km_ref, vm_ref, uc_ref, us_ref, gq_ref, gk_ref, gv_ref = refs[:8]
    refs = refs[8:]

    x = x_ref[...]
    shift = mod_ref[:, 0:D_MODEL]
    scale = mod_ref[:, D_MODEL:2 * D_MODEL]
    hb = (_rms(x) * g1_ref[...] * (1.0 + scale) + shift).astype(BF16)
    pa = _dot(hb, wa_ref[...])
    pc = _dot(hb, wc_ref[...])
    kr = _dot(hb, wkr_ref[...])

    if rope:
        cos_m, sa_m, sb_m = (tab_ref[:, i * LANES:(i + 1) * LANES] for i in range(3))
        cos_g, sa_g, sb_g = (tab_ref[:, i * LANES:(i + 1) * LANES] for i in range(3, 6))

    cqn = _rms(pa[:, :MLA_Q_RANK]) * gqa_ref[...]
    q = _dot(cqn.astype(BF16), wq_ref[...])
    for hd in range(MLA_HEADS):
        qh = q[:, hd * LANES:(hd + 1) * LANES]
        if rope:
            qh = _rotate(qh, cos_m, sa_m, sb_m, MLA_ROPE_DIM // 2)
        qm_ref[:, hd * LANES:(hd + 1) * LANES] = (qh * (LOG2_E * MLA_QK_DIM ** -0.5)).astype(BF16)

    ckvn = _rms(pa[:, MLA_Q_RANK:]) * gkva_ref[...]
    cb = ckvn.astype(BF16)
    kn = _dot(cb, wk_ref[...])
    krr = _rotate(kr, cos_m, sa_m, sb_m, MLA_ROPE_DIM // 2) if rope else kr
    for hd in range(MLA_HEADS):
        km_ref[:, hd * LANES:(hd + 1) * LANES] = (kn[:, hd * LANES:(hd + 1) * LANES] + krr).astype(BF16)
    vm_ref[...] = _dot(cb, wv_ref[...]).astype(BF16)

    ucs = _dot(pc[:, _C_U:_C_U + FNET_WIDTH].astype(BF16), wdft_ref[...])
    uc_ref[...] = ucs[:, :FNET_WIDTH].astype(BF16)
    us_ref[...] = ucs[:, FNET_WIDTH:].astype(BF16)

    gqn = _group_rms(pc[:, _C_GQ:_C_GQ + GQA_WIDTH], mq_ref[...]) * gqh_ref[...]
    tiles = []
    for p in range(HEAD_PAIRS):
        gp = gqn[:, p * LANES:(p + 1) * LANES]
        if rope:
            gp = _rotate(gp, cos_g, sa_g, sb_g, GQA_HEAD_DIM // 2)
        tiles.append(gp * (LOG2_E * GQA_HEAD_DIM ** -0.5))
    lower = _lower_half(x.shape[0])
    gq_ref[:, 0:LANES] = jnp.where(lower, tiles[0], tiles[1]).astype(BF16)
    gq_ref[:, LANES:2 * LANES] = jnp.where(lower, pltpu.roll(tiles[0], HALF, 1), pltpu.roll(tiles[2], HALF, 1)).astype(BF16)
    gq_ref[:, 2 * LANES:] = jnp.where(lower, tiles[1], tiles[2]).astype(BF16)
    gkn = _group_rms(pc[:, _C_GK:_C_GK + GQA_KV_WIDTH], mk_ref[...]) * gkh_ref[...]
    gkr = _rotate(gkn, cos_g, sa_g, sb_g, GQA_HEAD_DIM // 2) if rope else gkn
    gv = pc[:, _C_GV:_C_GV + GQA_KV_WIDTH]
    gk_ref[...] = gkr.astype(BF16)
    gv_ref[...] = gv.astype(BF16)

    if emit_cache:
        new = (ckvn, kr[:, :MLA_ROPE_DIM], gkn, gv)
        for i, out_ref in enumerate(refs):
            if n_prev:
                out_ref[:n_prev] = prev_refs[i][...]
            out_ref[n_prev] = new[i]


def _proj(x, mod, wts, layer, consts, *, seq, rope, prev_cache=None, emit_cache=False, tm=256):
    n_tok = x.shape[0]
    n_tiles = n_tok // tm
    tps = seq // tm
    n_seq = n_tok // seq
    tiles_per_mod = n_tiles // mod.shape[0]
    const = lambda arr: pl.BlockSpec(arr.shape, lambda i: (0,) * arr.ndim)
    tok = lambda w: pl.BlockSpec((tm, w), lambda i: (i, 0))
    in_specs = [tok(D_MODEL), pl.BlockSpec((None, 1, 6 * D_MODEL), lambda i: (i // tiles_per_mod, 0, 0))]
    args = [x, mod]
    for name in _PROJ_WEIGHTS:
        in_specs.append(_layer_spec(wts[name], layer))
        args.append(wts[name])
    for name in ("wdft", "mean_q", "mean_k"):
        in_specs.append(const(consts[name]))
        args.append(consts[name])
    if rope:
        in_specs.append(pl.BlockSpec((tm, 6 * LANES), lambda i: (i % tps, 0)))
        args.append(consts["rope"])
    seq_major = pl.BlockSpec((tm, FNET_WIDTH), lambda i: (i % tps, i // tps))
    out_specs = [tok(MLA_PAD_WIDTH), tok(MLA_PAD_WIDTH), tok(MIX_TILE), seq_major, seq_major,
                 tok(GQA_WIDTH), tok(GQA_KV_WIDTH), tok(GQA_KV_WIDTH)]
    out_shape = [jax.ShapeDtypeStruct((n_tok, MLA_PAD_WIDTH), BF16),
                 jax.ShapeDtypeStruct((n_tok, MLA_PAD_WIDTH), BF16),
                 jax.ShapeDtypeStruct((n_tok, MIX_TILE), BF16),
                 jax.ShapeDtypeStruct((seq, n_seq * FNET_WIDTH), BF16),
                 jax.ShapeDtypeStruct((seq, n_seq * FNET_WIDTH), BF16),
                 jax.ShapeDtypeStruct((n_tok, GQA_WIDTH), BF16),
                 jax.ShapeDtypeStruct((n_tok, GQA_KV_WIDTH), BF16),
                 jax.ShapeDtypeStruct((n_tok, GQA_KV_WIDTH), BF16)]
    n_prev = 0
    if emit_cache:
        assert tps == 1, "cache outputs are written one sequence per token tile"
        n_prev = prev_cache[0].shape[1] if prev_cache is not None else 0
        layers = lambda n, w: pl.BlockSpec((None, n, seq, w), lambda i: (i, 0, 0, 0))
        widths = (MLA_KV_RANK, MLA_ROPE_DIM, GQA_KV_WIDTH, GQA_KV_WIDTH)
        if n_prev:
            in_specs += [layers(n_prev, w) for w in widths]
            args += list(prev_cache)
        out_specs += [layers(n_prev + 1, w) for w in widths]
        out_shape += [jax.ShapeDtypeStruct((n_seq, n_prev + 1, seq, w), F32) for w in widths]
    return pl.pallas_call(
        functools.partial(_proj_kernel, rope=rope, emit_cache=emit_cache, n_prev=n_prev),
        grid=(n_tiles,),
        in_specs=in_specs,
        out_specs=out_specs,
        out_shape=out_shape,
        compiler_params=pltpu.CompilerParams(dimension_semantics=("arbitrary",), vmem_limit_bytes=VMEM_LIMIT),
        name="proj_rope" if rope else "proj_ctx",
    )(*args)


def _ctx_kernel(ckv_ref, kr_ref, gk_ref, gv_ref, wk_ref, wv_ref, place_ref, km_ref, vm_ref, gko_ref, gvo_ref):
    cb = ckv_ref[...].astype(BF16)
    kn = _dot(cb, wk_ref[...])
    kr = _dot(kr_ref[...].astype(BF16), place_ref[...])
    for hd in range(MLA_HEADS):
        km_ref[:, hd * LANES:(hd + 1) * LANES] = (kn[:, hd * LANES:(hd + 1) * LANES] + kr).astype(BF16)
    vm_ref[...] = _dot(cb, wv_ref[...]).astype(BF16)
    gko_ref[...] = gk_ref[...].astype(BF16)
    gvo_ref[...] = gv_ref[...].astype(BF16)


def _ctx_prep(layer, cache_ckv, cache_krope, cache_gk, cache_gv, wts, consts):
    nb, _, past, _ = cache_ckv.shape
    cache = lambda w: pl.BlockSpec((None, None, past, w), lambda b: (b, layer, 0, 0))
    out = lambda w: pl.BlockSpec((None, past, w), lambda b: (b, 0, 0))
    return pl.pallas_call(
        _ctx_kernel,
        grid=(nb,),
        in_specs=[cache(MLA_KV_RANK), cache(MLA_ROPE_DIM), cache(GQA_KV_WIDTH), cache(GQA_KV_WIDTH),
                  _layer_spec(wts["wk"], layer), _layer_spec(wts["wv"], layer),
                  pl.BlockSpec((MLA_ROPE_DIM, LANES), lambda b: (0, 0))],
        out_specs=[out(MLA_PAD_WIDTH), out(MIX_TILE), out(GQA_KV_WIDTH), out(GQA_KV_WIDTH)],
        out_shape=[jax.ShapeDtypeStruct((nb, past, MLA_PAD_WIDTH), BF16),
                   jax.ShapeDtypeStruct((nb, past, MIX_TILE), BF16),
                   jax.ShapeDtypeStruct((nb, past, GQA_KV_WIDTH), BF16),
                   jax.ShapeDtypeStruct((nb, past, GQA_KV_WIDTH), BF16)],
        compiler_params=pltpu.CompilerParams(dimension_semantics=("arbitrary",), vmem_limit_bytes=VMEM_LIMIT),
        name="ctx_prep",
    )(cache_ckv, cache_krope, cache_gk.reshape(nb, DEPTH, past, GQA_KV_WIDTH),
      cache_gv.reshape(nb, DEPTH, past, GQA_KV_WIDTH), wts["wk"], wts["wv"], consts["place"])


def _attn_kernel(*refs, n_seg, mla):
    q_ref, o_ref = refs[0], refs[-1]
    k_refs = refs[1:1 + n_seg]
    v_refs = refs[1 + n_seg:1 + 2 * n_seg]
    lower = _lower_half(q_ref.shape[1])

    def scores(b, p, hd):
        if mla:
            cols = slice((2 * p + hd) * LANES, (2 * p + hd + 1) * LANES)
            qh = q_ref[b, :, cols]
            keys = [k_ref[b, :, cols] for k_ref in k_refs]
        else:
            q = q_ref[b, :, p * LANES:(p + 1) * LANES]
            qh = jnp.where(lower if hd == 0 else ~lower, q, jnp.zeros_like(q))
            keys = [k_ref[b] for k_ref in k_refs]
        return [lax.dot_general(qh, k, (((1,), (1,)), ((), ())), preferred_element_type=F32) for k in keys]

    def softmax(ss):
        m = functools.reduce(jnp.maximum, [jnp.max(s, axis=-1, keepdims=True) for s in ss])
        es = [jnp.exp2(s - m) for s in ss]
        denom = functools.reduce(jnp.add, [jnp.sum(e, axis=-1, keepdims=True) for e in es])
        return [e.astype(BF16) for e in es], denom

    def weighted_values(b, p, es, denom, v_wide):
        if mla:
            win = min(p, HEAD_PAIRS - 2)
            vals = [v_ref[b, :, win * LANES:(win + 2) * LANES] for v_ref in v_refs]
        else:
            win, vals = p, v_wide
        acc = functools.reduce(jnp.add, [_dot(e, v) for e, v in zip(es, vals)])
        return acc[:, (p - win) * LANES:(p - win + 1) * LANES] * (1.0 / denom)

    for b in range(q_ref.shape[0]):
        v_wide = None if mla else [jnp.concatenate([v_ref[b], v_ref[b]], axis=-1) for v_ref in v_refs]
        tiles = []
        for p in range(HEAD_PAIRS):
            both = [scores(b, p, 0), scores(b, p, 1)]
            both = [softmax(ss) for ss in both]
            outs = [weighted_values(b, p, es, denom, v_wide) for es, denom in both]
            tiles.append(jnp.where(lower, outs[0], outs[1]))
        if not mla:
            t1 = pltpu.roll(tiles[1], HALF, 1)
            tiles = [jnp.where(lower, tiles[0], t1), jnp.where(lower, tiles[2], tiles[0]),
                     jnp.where(lower, t1, tiles[2])]
        for p in range(HEAD_PAIRS):
            o_ref[b, :, p * LANES:(p + 1) * LANES] = tiles[p].astype(o_ref.dtype)


def _attention(q, kv_segments, *, mla, tq, bb=1):
    n_batch, sq = q.shape[:2]
    n_seg = len(kv_segments)
    whole = lambda a: pl.BlockSpec((bb,) + a.shape[1:], lambda b, i: (b, 0, 0))
    in_specs = [pl.BlockSpec((bb, tq, q.shape[2]), lambda b, i: (b, i, 0))]
    in_specs += [whole(k) for k, _ in kv_segments] + [whole(v) for _, v in kv_segments]
    return pl.pallas_call(
        functools.partial(_attn_kernel, n_seg=n_seg, mla=mla),
        grid=(n_batch // bb, sq // tq),
        in_specs=in_specs,
        out_specs=pl.BlockSpec((bb, tq, MIX_TILE), lambda b, i: (b, i, 0)),
        out_shape=jax.ShapeDtypeStruct((n_batch, sq, MIX_TILE), BF16),
        compiler_params=pltpu.CompilerParams(dimension_semantics=("arbitrary",) * 2, vmem_limit_bytes=VMEM_LIMIT),
        name=f"attn_{'mla' if mla else 'gqa'}_s{n_seg}",
    )(q, *[k for k, _ in kv_segments], *[v for _, v in kv_segments])


def _fnet_kernel(ca_ref, sa_ref, cb_ref, sb_ref, uc_ref, us_ref, o_ref, tc_ref, ts_ref, *, scale):
    @pl.when(pl.program_id(1) == 0)
    def _build_twiddles():
        cb, sb = cb_ref[...], sb_ref[...]
        for a in range(ca_ref.shape[0]):
            ca, sa = ca_ref[a:a + 1, :], sa_ref[a:a + 1, :]
            rows = slice(a * DFT_SPLIT, (a + 1) * DFT_SPLIT)
            tc_ref[rows, :] = (ca * cb - sa * sb).astype(BF16)
            ts_ref[rows, :] = (sa * cb + ca * sb).astype(BF16)

    acc = _dot(tc_ref[...], uc_ref[...]) - _dot(ts_ref[...], us_ref[...])
    o_ref[...] = (acc * scale).astype(o_ref.dtype)


def _fnet(tables, uc, us, tm, tn):
    seq, width = uc.shape
    scale = float((seq * FNET_GROUP_DIM) ** -0.5)
    n_a = tm // DFT_SPLIT
    part_a = pl.BlockSpec((n_a, seq), lambda i, j: (i, 0))
    part_b = pl.BlockSpec((DFT_SPLIT, seq), lambda i, j: (0, 0))
    data = pl.BlockSpec((seq, tn), lambda i, j: (0, j))
    return pl.pallas_call(
        functools.partial(_fnet_kernel, scale=scale),
        grid=(seq // tm, width // tn),
        in_specs=[part_a, part_a, part_b, part_b, data, data],
        out_specs=pl.BlockSpec((tm, tn), lambda i, j: (i, j)),
        out_shape=jax.ShapeDtypeStruct((seq, width), BF16),
        scratch_shapes=[pltpu.VMEM((tm, seq), BF16), pltpu.VMEM((tm, seq), BF16)],
        compiler_params=pltpu.CompilerParams(dimension_semantics=("arbitrary", "arbitrary"),
                                             vmem_limit_bytes=VMEM_LIMIT),
        name="fnet",
    )(*tables, uc, us)


def _post_kernel(x_ref, mla_ref, fn_ref, gqa_ref, mod_ref, g2_ref, wo_ref, wg_ref, wu_ref, wd_ref, gf_ref,
                 o_ref, *, final):
    fn = fn_ref[...]
    if fn.shape[1] > FNET_WIDTH:
        fn = jnp.concatenate([fn[:, j:j + FNET_WIDTH] for j in range(0, fn.shape[1], FNET_WIDTH)], axis=0)
    mix = jnp.concatenate([mla_ref[...], fn, gqa_ref[...]], axis=-1)
    gate1 = mod_ref[:, 2 * D_MODEL:3 * D_MODEL]
    shift2 = mod_ref[:, 3 * D_MODEL:4 * D_MODEL]
    scale2 = mod_ref[:, 4 * D_MODEL:5 * D_MODEL]
    gate2 = mod_ref[:, 5 * D_MODEL:6 * D_MODEL]
    x = x_ref[...] + gate1 * _dot(mix, wo_ref[...])
    h = (_rms(x) * g2_ref[...] * (1.0 + scale2) + shift2).astype(BF16)
    g = _dot(h, wg_ref[...])
    u = _dot(h, wu_ref[...])
    a = (g * jax.nn.sigmoid(g) * u).astype(BF16)
    x = x + gate2 * _dot(a, wd_ref[...])
    if final:
        x = _rms(x) * gf_ref[...]
    o_ref[...] = x


def _post(x, mla_o, fnet_o, gqa_o, mod, wts, layer, g_final, *, seq, final, tm):
    n_tok = x.shape[0]
    n_tiles = n_tok // tm
    tiles_per_mod = n_tiles // mod.shape[0]
    resident = lambda arr: pl.BlockSpec((None,) + arr.shape[1:], lambda i: (layer, 0, 0), pipeline_mode=pl.Buffered(1))
    tok = lambda w: pl.BlockSpec((tm, w), lambda i: (i, 0))
    if tm <= seq:
        tps = seq // tm
        fnet_spec = pl.BlockSpec((tm, FNET_WIDTH), lambda i: (i % tps, i // tps))
    else:
        fnet_spec = pl.BlockSpec((seq, (tm // seq) * FNET_WIDTH), lambda i: (0, i))
    names = ("g2", "w_out", "w_gate", "w_up", "w_down")
    return pl.pallas_call(
        functools.partial(_post_kernel, final=final),
        grid=(n_tiles,),
        in_specs=[tok(D_MODEL), tok(MIX_TILE), fnet_spec, tok(MIX_TILE),
                  pl.BlockSpec((None, 1, 6 * D_MODEL), lambda i: (i // tiles_per_mod, 0, 0))]
                 + [resident(wts[n]) for n in names]
                 + [pl.BlockSpec((1, D_MODEL), lambda i: (0, 0))],
        out_specs=tok(D_MODEL),
        out_shape=jax.ShapeDtypeStruct((n_tok, D_MODEL), F32),
        compiler_params=pltpu.CompilerParams(dimension_semantics=("arbitrary",), vmem_limit_bytes=VMEM_LIMIT),
        name="post_final" if final else "post",
    )(x, mla_o, fnet_o, gqa_o, mod, *[wts[n] for n in names], g_final)


def _prep_weights(g_norm1, g_norm2, w_in, g_q_a, w_q_up, g_kv_a, w_kv_up, g_q_head, g_k_head,
                  w_out, w_ffn_gate, w_ffn_up, w_ffn_down):
    row = lambda g: g.reshape(DEPTH, 1, -1)
    w_kr = jnp.pad(w_in[:, :, _IN_A:_IN_C0], ((0, 0), (0, 0), (0, LANES - MLA_ROPE_DIM)))
    wq = w_q_up.reshape(DEPTH, MLA_Q_RANK, MLA_HEADS, MLA_QK_DIM)
    wq = jnp.concatenate([wq[..., MLA_NOPE_DIM:], wq[..., :MLA_NOPE_DIM]], axis=-1)
    wq = jnp.pad(wq, ((0, 0), (0, 0), (0, 0), (0, LANES - MLA_QK_DIM)))
    wkv = w_kv_up.reshape(DEPTH, MLA_KV_RANK, MLA_HEADS, MLA_NOPE_DIM + MLA_V_DIM)
    wk = jnp.pad(wkv[..., :MLA_NOPE_DIM], ((0, 0), (0, 0), (0, 0), (MLA_ROPE_DIM, LANES - MLA_QK_DIM)))
    wv = wkv[..., MLA_NOPE_DIM:]
    return {
        "g1": row(g_norm1), "g2": row(g_norm2), "g_q_a": row(g_q_a), "g_kv_a": row(g_kv_a),
        "g_q_head": row(jnp.tile(g_q_head, (1, GQA_HEADS))), "g_k_head": row(jnp.tile(g_k_head, (1, GQA_KV_HEADS))),
        "w_a": w_in[:, :, :_IN_A].astype(BF16), "w_c": w_in[:, :, _IN_C0:].astype(BF16), "w_kr": w_kr.astype(BF16),
        "wq": wq.reshape(DEPTH, MLA_Q_RANK, MLA_PAD_WIDTH).astype(BF16),
        "wk": wk.reshape(DEPTH, MLA_KV_RANK, MLA_PAD_WIDTH).astype(BF16),
        "wv": wv.reshape(DEPTH, MLA_KV_RANK, MIX_TILE).astype(BF16),
        "w_out": w_out.astype(BF16), "w_gate": w_ffn_gate.astype(BF16),
        "w_up": w_ffn_up.astype(BF16), "w_down": w_ffn_down.astype(BF16),
    }


def kernel(x_prompt, x_sample, cache_mla_ckv, cache_mla_krope, cache_gqa_k, cache_gqa_v, c, c_ctx, w_ada, b_ada,
           g_norm1, g_norm2, w_in, g_q_a, w_q_up, g_kv_a, w_kv_up, g_q_head, g_k_head, w_out,
           w_ffn_gate, w_ffn_up, w_ffn_down, g_final):
    n_pb, p_seq, _ = x_prompt.shape
    n_sb, s_seq, _ = x_sample.shape
    consts = {"wdft": _channel_dft(), "mean_q": _group_mean_matrix(GQA_HEADS),
              "mean_k": _group_mean_matrix(GQA_KV_HEADS), "place": _rope_place_matrix(),
              "rope": _rope_tables(s_seq)}
    dft_p = _seq_dft_tables(p_seq)
    dft_s = _seq_dft_tables(s_seq)
    gf = g_final.reshape(1, D_MODEL)
    wts = _prep_weights(g_norm1, g_norm2, w_in, g_q_a, w_q_up, g_kv_a, w_kv_up, g_q_head, g_k_head,
                        w_out, w_ffn_gate, w_ffn_up, w_ffn_down)

    cond_t = jnp.concatenate([c_ctx[None, :], c], axis=0).T
    mod = _ada(cond_t, w_ada, b_ada)

    xp = x_prompt.reshape(n_pb * p_seq, D_MODEL)
    xs = x_sample.reshape(n_sb * s_seq, D_MODEL)
    b3 = lambda a, nb: a.reshape(nb, a.shape[0] // nb, a.shape[1])
    caches = None
    for l in range(DEPTH):
        final = l == DEPTH - 1
        mod_p = mod[l, 0:1].reshape(1, 1, 6 * D_MODEL)
        mod_s = mod[l, 1:].reshape(n_sb, 1, 6 * D_MODEL)

        qm, km, vm, uc, us, gq, gk, gv, *caches = _proj(
            xp, mod_p, wts, l, consts, seq=p_seq, rope=False, emit_cache=True, prev_cache=caches)
        mla_o = _attention(b3(qm, n_pb), [(b3(km, n_pb), b3(vm, n_pb))], mla=True, tq=p_seq, bb=2)
        gqa_o = _attention(b3(gq, n_pb), [(b3(gk, n_pb), b3(gv, n_pb))], mla=False, tq=p_seq, bb=2)
        fn_o = _fnet(dft_p, uc, us, tm=p_seq, tn=1024)
        xp = _post(xp, mla_o.reshape(-1, MIX_TILE), fn_o, gqa_o.reshape(-1, MIX_TILE), mod_p, wts, l, gf,
                   seq=p_seq, final=final, tm=POST_TILE)

        km_c, vm_c, gk_c, gv_c = _ctx_prep(l, cache_mla_ckv, cache_mla_krope, cache_gqa_k, cache_gqa_v, wts, consts)
        qm, km, vm, uc, us, gq, gk, gv = _proj(xs, mod_s, wts, l, consts, seq=s_seq, rope=True)
        mla_o = _attention(b3(qm, n_sb), [(km_c, vm_c), (b3(km, n_sb), b3(vm, n_sb))], mla=True, tq=512)
        gqa_o = _attention(b3(gq, n_sb), [(gk_c, gv_c), (b3(gk, n_sb), b3(gv, n_sb))], mla=False, tq=512)
        fn_o = _fnet(dft_s, uc, us, tm=512, tn=n_sb * FNET_WIDTH)
        xs = _post(xs, mla_o.reshape(-1, MIX_TILE), fn_o, gqa_o.reshape(-1, MIX_TILE), mod_s, wts, l, gf,
                   seq=s_seq, final=final, tm=POST_TILE)

    ckv_new, krope_new, gk_new, gv_new = caches
    heads = (n_pb, DEPTH, p_seq, GQA_KV_HEADS, GQA_HEAD_DIM)
    return (xp.reshape(n_pb, p_seq, D_MODEL), xs.reshape(n_sb, s_seq, D_MODEL),
            ckv_new, krope_new, gk_new.reshape(heads), gv_new.reshape(heads))
```

```python
import functools

import numpy as np
import jax
import jax.numpy as jnp
from jax import lax
from jax.experimental import pallas as pl
from jax.experimental.pallas import tpu as pltpu

D_MODEL = 1024
DEPTH = 2
GRID_W = 64
MLA_HEADS = 6
MLA_Q_RANK = 384
MLA_KV_RANK = 256
MLA_NOPE_DIM = 64
MLA_ROPE_DIM = 32
MLA_V_DIM = 64
MLA_QK_DIM = MLA_NOPE_DIM + MLA_ROPE_DIM
FNET_GROUPS = 4
FNET_GROUP_DIM = 64
FNET_WIDTH = FNET_GROUPS * FNET_GROUP_DIM
GQA_HEADS = 6
GQA_KV_HEADS = 2
GQA_HEAD_DIM = 64
GQA_GROUP = GQA_HEADS // GQA_KV_HEADS
GQA_WIDTH = GQA_HEADS * GQA_HEAD_DIM
GQA_KV_WIDTH = GQA_KV_HEADS * GQA_HEAD_DIM
D_FF = 2816
ROPE_THETA = 10000.0
EPS = 1e-6
LOG2_E = 1.4426950408889634

LANES = 128
HALF = LANES // 2
HEAD_PAIRS = MLA_HEADS // 2
MLA_PAD_WIDTH = MLA_HEADS * LANES
MIX_TILE = HEAD_PAIRS * LANES
VMEM_LIMIT = 52 * 1024 * 1024
POST_TILE = 512

_IN_A = MLA_Q_RANK + MLA_KV_RANK
_IN_C0 = _IN_A + MLA_ROPE_DIM
_IN_C = FNET_WIDTH + GQA_WIDTH + 2 * GQA_KV_WIDTH
_C_U, _C_GQ, _C_GK, _C_GV = 0, FNET_WIDTH, FNET_WIDTH + GQA_WIDTH, FNET_WIDTH + GQA_WIDTH + GQA_KV_WIDTH

BF16 = jnp.bfloat16
F32 = jnp.float32


def _rope_tables(n_tokens):
    t = np.arange(n_tokens)
    row = (t // GRID_W).astype(np.float64)
    col = (t % GRID_W).astype(np.float64)

    def angles(rot_dim):
        n_axis = rot_dim // 4
        inv = ROPE_THETA ** (-np.arange(n_axis, dtype=np.float64) / n_axis)
        return np.concatenate([row[:, None] * inv, col[:, None] * inv], axis=-1)

    def tables(ang, lane_to_pair, is_first, is_second):
        cos = np.where((is_first | is_second)[None, :], np.cos(ang)[:, lane_to_pair], 1.0)
        sin = np.sin(ang)[:, lane_to_pair]
        sin_a = np.where(is_second[None, :], sin, 0.0)
        sin_b = np.where(is_first[None, :], -sin, 0.0)
        return [cos, sin_a, sin_b]

    lane = np.arange(LANES)
    half_m = MLA_ROPE_DIM // 2
    in_rope = lane < MLA_ROPE_DIM
    first_m = lane < half_m
    second_m = in_rope & ~first_m
    pair_m = lane % half_m
    half_g = GQA_HEAD_DIM // 2
    first_g = (lane % GQA_HEAD_DIM) < half_g
    pair_g = lane % half_g
    tabs = (tables(angles(MLA_ROPE_DIM), pair_m, first_m, second_m)
            + tables(angles(GQA_HEAD_DIM), pair_g, first_g, ~first_g))
    return jnp.asarray(np.concatenate(tabs, axis=-1), dtype=F32)


def _channel_dft():
    c = np.arange(FNET_GROUP_DIM)
    ang = 2.0 * np.pi * np.outer(c, c) / FNET_GROUP_DIM
    eye = np.eye(FNET_GROUPS)
    table = np.concatenate([np.kron(eye, np.cos(ang)), np.kron(eye, np.sin(ang))], axis=1)
    return jnp.asarray(table, dtype=F32).astype(BF16)


def _group_mean_matrix(n_heads):
    return jnp.asarray(np.kron(np.eye(n_heads), np.full((GQA_HEAD_DIM, GQA_HEAD_DIM), 1.0 / GQA_HEAD_DIM)), dtype=BF16)


def _rope_place_matrix():
    return jnp.asarray(np.eye(MLA_ROPE_DIM, LANES), dtype=BF16)


DFT_SPLIT = 32


def _seq_dft_tables(seq):
    s = np.arange(seq)
    n_a = seq // DFT_SPLIT
    ang_a = 2.0 * np.pi * ((np.arange(n_a)[:, None] * s[None, :]) % n_a) / n_a
    ang_b = 2.0 * np.pi * ((np.arange(DFT_SPLIT)[:, None] * s[None, :]) % seq) / seq
    return tuple(jnp.asarray(t, dtype=F32) for t in (np.cos(ang_a), np.sin(ang_a), np.cos(ang_b), np.sin(ang_b)))


def _rms(x):
    return x * lax.rsqrt(jnp.mean(x * x, axis=-1, keepdims=True) + EPS)


def _dot(a, b):
    return jnp.dot(a, b, preferred_element_type=F32)


def _group_rms(x, mean_mat):
    ms = _dot((x * x).astype(BF16), mean_mat)
    return x * lax.rsqrt(ms + EPS)


def _rotate(x, cos, sin_a, sin_b, half):
    return x * cos + pltpu.roll(x, half, 1) * sin_a + pltpu.roll(x, LANES - half, 1) * sin_b


def _lower_half(rows):
    return lax.broadcasted_iota(jnp.int32, (rows, LANES), 1) < HALF


def _layer_spec(arr, layer):
    return pl.BlockSpec((None,) + arr.shape[1:], lambda i: (layer, 0, 0))


def _ada_kernel(ct_ref, w_ref, b_ref, o_ref):
    s = ct_ref[...]
    s = s * jax.nn.sigmoid(s)
    w = w_ref[...]
    for m in range(o_ref.shape[0]):
        o_ref[m:m + 1, :] = jnp.sum(w * s[:, m:m + 1], axis=0, keepdims=True) + b_ref[...]


def _ada(cond_t, w_ada, b_ada, tn=2048):
    n_cond = cond_t.shape[1]
    width = w_ada.shape[2]
    return pl.pallas_call(
        _ada_kernel,
        grid=(DEPTH, width // tn),
        in_specs=[
            pl.BlockSpec((D_MODEL, n_cond), lambda l, j: (0, 0)),
            pl.BlockSpec((None, D_MODEL, tn), lambda l, j: (l, 0, j)),
            pl.BlockSpec((None, 1, tn), lambda l, j: (l, 0, j)),
        ],
        out_specs=pl.BlockSpec((None, n_cond, tn), lambda l, j: (l, 0, j)),
        out_shape=jax.ShapeDtypeStruct((DEPTH, n_cond, width), F32),
        compiler_params=pltpu.CompilerParams(dimension_semantics=("arbitrary", "arbitrary"),
                                             vmem_limit_bytes=VMEM_LIMIT),
        name="ada",
    )(cond_t, w_ada, b_ada.reshape(DEPTH, 1, width))


_PROJ_WEIGHTS = ("g1", "w_a", "w_c", "w_kr", "g_q_a", "wq", "g_kv_a", "wk", "wv", "g_q_head", "g_k_head")


def _proj_kernel(*refs, rope, emit_cache, n_prev):
    (x_ref, mod_ref, g1_ref, wa_ref, wc_ref, wkr_ref, gqa_ref, wq_ref, gkva_ref, wk_ref, wv_ref,
     gqh_ref, gkh_ref, wdft_ref, mq_ref, mk_ref) = refs[:16]
    refs = refs[16:]
    if rope:
        tab_ref, refs = refs[0], refs[1:]
    if n_prev:
        prev_refs, refs = refs[:4], refs[4:]
    qm_ref, km_ref, vm_ref, uc_ref, us_ref, gq_ref, gk_ref, gv_ref = refs[:8]
    refs = refs[8:]

    x = x_ref[...]
    shift = mod_ref[:, 0:D_MODEL]
    scale = mod_ref[:, D_MODEL:2 * D_MODEL]
    hb = (_rms(x) * g1_ref[...] * (1.0 + scale) + shift).astype(BF16)
    pa = _dot(hb, wa_ref[...])
    pc = _dot(hb, wc_ref[...])
    kr = _dot(hb, wkr_ref[...])

    if rope:
        cos_m, sa_m, sb_m = (tab_ref[:, i * LANES:(i + 1) * LANES] for i in range(3))
        cos_g, sa_g, sb_g = (tab_ref[:, i * LANES:(i + 1) * LANES] for i in range(3, 6))

    cqn = _rms(pa[:, :MLA_Q_RANK]) * gqa_ref[...]
    q = _dot(cqn.astype(BF16), wq_ref[...])
    for hd in range(MLA_HEADS):
        qh = q[:, hd * LANES:(hd + 1) * LANES]
        if rope:
            qh = _rotate(qh, cos_m, sa_m, sb_m, MLA_ROPE_DIM // 2)
        qm_ref[:, hd * LANES:(hd + 1) * LANES] = (qh * (LOG2_E * MLA_QK_DIM ** -0.5)).astype(BF16)

    ckvn = _rms(pa[:, MLA_Q_RANK:]) * gkva_ref[...]
    cb = ckvn.astype(BF16)
    kn = _dot(cb, wk_ref[...])
    krr = _rotate(kr, cos_m, sa_m, sb_m, MLA_ROPE_DIM // 2) if rope else kr
    for hd in range(MLA_HEADS):
        km_ref[:, hd * LANES:(hd + 1) * LANES] = (kn[:, hd * LANES:(hd + 1) * LANES] + krr).astype(BF16)
    vm_ref[...] = _dot(cb, wv_ref[...]).T.astype(BF16)

    ucs = _dot(pc[:, _C_U:_C_U + FNET_WIDTH].astype(BF16), wdft_ref[...])
    uc_ref[...] = ucs[:, :FNET_WIDTH].astype(BF16)
    us_ref[...] = ucs[:, FNET_WIDTH:].astype(BF16)

    gqn = _group_rms(pc[:, _C_GQ:_C_GQ + GQA_WIDTH], mq_ref[...]) * gqh_ref[...]
    tiles = []
    for p in range(HEAD_PAIRS):
        gp = gqn[:, p * LANES:(p + 1) * LANES]
        if rope:
            gp = _rotate(gp, cos_g, sa_g, sb_g, GQA_HEAD_DIM // 2)
        tiles.append(gp * (LOG2_E * GQA_HEAD_DIM ** -0.5))
    lower = _lower_half(x.shape[0])
    gq_ref[:, 0:LANES] = jnp.where(lower, tiles[0], tiles[1]).astype(BF16)
    gq_ref[:, LANES:2 * LANES] = jnp.where(lower, pltpu.roll(tiles[0], HALF, 1), pltpu.roll(tiles[2], HALF, 1)).astype(BF16)
    gq_ref[:, 2 * LANES:] = jnp.where(lower, tiles[1], tiles[2]).astype(BF16)
    gkn = _group_rms(pc[:, _C_GK:_C_GK + GQA_KV_WIDTH], mk_ref[...]) * gkh_ref[...]
    gkr = _rotate(gkn, cos_g, sa_g, sb_g, GQA_HEAD_DIM // 2) if rope else gkn
    gv = pc[:, _C_GV:_C_GV + GQA_KV_WIDTH]
    gk_ref[...] = gkr.astype(BF16)
    gv_ref[...] = gv.T.astype(BF16)

    if emit_cache:
        new = (ckvn, kr[:, :MLA_ROPE_DIM], gkn, gv)
        for i, out_ref in enumerate(refs):
            if n_prev:
                out_ref[:n_prev] = prev_refs[i][...]
            out_ref[n_prev] = new[i]


def _proj(x, mod, wts, layer, consts, *, seq, rope, prev_cache=None, emit_cache=False, tm=256):
    n_tok = x.shape[0]
    n_tiles = n_tok // tm
    tps = seq // tm
    n_seq = n_tok // seq
    tiles_per_mod = n_tiles // mod.shape[0]
    const = lambda arr: pl.BlockSpec(arr.shape, lambda i: (0,) * arr.ndim)
    tok = lambda w: pl.BlockSpec((tm, w), lambda i: (i, 0))
    in_specs = [tok(D_MODEL), pl.BlockSpec((None, 1, 6 * D_MODEL), lambda i: (i // tiles_per_mod, 0, 0))]
    args = [x, mod]
    for name in _PROJ_WEIGHTS:
        in_specs.append(_layer_spec(wts[name], layer))
        args.append(wts[name])
    for name in ("wdft", "mean_q", "mean_k"):
        in_specs.append(const(consts[name]))
        args.append(consts[name])
    if rope:
        in_specs.append(pl.BlockSpec((tm, 6 * LANES), lambda i: (i % tps, 0)))
        args.append(consts["rope"])
    seq_major = pl.BlockSpec((tm, FNET_WIDTH), lambda i: (i % tps, i // tps))
    transposed = lambda w: pl.BlockSpec((None, w, tm), lambda i: (i // tps, 0, i % tps))
    out_specs = [tok(MLA_PAD_WIDTH), tok(MLA_PAD_WIDTH), transposed(MIX_TILE), seq_major, seq_major,
                 tok(GQA_WIDTH), tok(GQA_KV_WIDTH), transposed(GQA_KV_WIDTH)]
    out_shape = [jax.ShapeDtypeStruct((n_tok, MLA_PAD_WIDTH), BF16),
                 jax.ShapeDtypeStruct((n_tok, MLA_PAD_WIDTH), BF16),
                 jax.ShapeDtypeStruct((n_seq, MIX_TILE, seq), BF16),
                 jax.ShapeDtypeStruct((seq, n_seq * FNET_WIDTH), BF16),
                 jax.ShapeDtypeStruct((seq, n_seq * FNET_WIDTH), BF16),
                 jax.ShapeDtypeStruct((n_tok, GQA_WIDTH), BF16),
                 jax.ShapeDtypeStruct((n_tok, GQA_KV_WIDTH), BF16),
                 jax.ShapeDtypeStruct((n_seq, GQA_KV_WIDTH, seq), BF16)]
    n_prev = 0
    if emit_cache:
        assert tps == 1, "cache outputs are written one sequence per token tile"
        n_prev = prev_cache[0].shape[1] if prev_cache is not None else 0
        layers = lambda n, w: pl.BlockSpec((None, n, seq, w), lambda i: (i, 0, 0, 0))
        widths = (MLA_KV_RANK, MLA_ROPE_DIM, GQA_KV_WIDTH, GQA_KV_WIDTH)
        if n_prev:
            in_specs += [layers(n_prev, w) for w in widths]
            args += list(prev_cache)
        out_specs += [layers(n_prev + 1, w) for w in widths]
        out_shape += [jax.ShapeDtypeStruct((n_seq, n_prev + 1, seq, w), F32) for w in widths]
    return pl.pallas_call(
        functools.partial(_proj_kernel, rope=rope, emit_cache=emit_cache, n_prev=n_prev),
        grid=(n_tiles,),
        in_specs=in_specs,
        out_specs=out_specs,
        out_shape=out_shape,
        compiler_params=pltpu.CompilerParams(dimension_semantics=("arbitrary",), vmem_limit_bytes=VMEM_LIMIT),
        name="proj_rope" if rope else "proj_ctx",
    )(*args)


def _ctx_kernel(ckv_ref, kr_ref, gk_ref, gv_ref, wk_ref, wv_ref, place_ref, km_ref, vm_ref, gko_ref, gvo_ref):
    cb = ckv_ref[...].astype(BF16)
    kn = _dot(cb, wk_ref[...])
    kr = _dot(kr_ref[...].astype(BF16), place_ref[...])
    for hd in range(MLA_HEADS):
        km_ref[:, hd * LANES:(hd + 1) * LANES] = (kn[:, hd * LANES:(hd + 1) * LANES] + kr).astype(BF16)
    vm_ref[...] = _dot(cb, wv_ref[...]).T.astype(BF16)
    gko_ref[...] = gk_ref[...].astype(BF16)
    gvo_ref[...] = gv_ref[...].T.astype(BF16)


def _ctx_prep(layer, cache_ckv, cache_krope, cache_gk, cache_gv, wts, consts):
    nb, _, past, _ = cache_ckv.shape
    cache = lambda w: pl.BlockSpec((None, None, past, w), lambda b: (b, layer, 0, 0))
    out = lambda w: pl.BlockSpec((None, past, w), lambda b: (b, 0, 0))
    out_t = lambda w: pl.BlockSpec((None, w, past), lambda b: (b, 0, 0))
    return pl.pallas_call(
        _ctx_kernel,
        grid=(nb,),
        in_specs=[cache(MLA_KV_RANK), cache(MLA_ROPE_DIM), cache(GQA_KV_WIDTH), cache(GQA_KV_WIDTH),
                  _layer_spec(wts["wk"], layer), _layer_spec(wts["wv"], layer),
                  pl.BlockSpec((MLA_ROPE_DIM, LANES), lambda b: (0, 0))],
        out_specs=[out(MLA_PAD_WIDTH), out_t(MIX_TILE), out(GQA_KV_WIDTH), out_t(GQA_KV_WIDTH)],
        out_shape=[jax.ShapeDtypeStruct((nb, past, MLA_PAD_WIDTH), BF16),
                   jax.ShapeDtypeStruct((nb, MIX_TILE, past), BF16),
                   jax.ShapeDtypeStruct((nb, past, GQA_KV_WIDTH), BF16),
                   jax.ShapeDtypeStruct((nb, GQA_KV_WIDTH, past), BF16)],
        compiler_params=pltpu.CompilerParams(dimension_semantics=("arbitrary",), vmem_limit_bytes=VMEM_LIMIT),
        name="ctx_prep",
    )(cache_ckv, cache_krope, cache_gk.reshape(nb, DEPTH, past, GQA_KV_WIDTH),
      cache_gv.reshape(nb, DEPTH, past, GQA_KV_WIDTH), wts["wk"], wts["wv"], consts["place"])


def _attn_kernel(*refs, n_seg, mla):
    q_ref, o_ref = refs[0], refs[-1]
    k_refs = refs[1:1 + n_seg]
    vt_refs = refs[1 + n_seg:1 + 2 * n_seg]
    tq = q_ref.shape[1]
    lower = _lower_half(tq)
    top = lax.broadcasted_iota(jnp.int32, (LANES, tq), 0) < HALF

    def scores_t(b, p, hd):
        if mla:
            cols = slice((2 * p + hd) * LANES, (2 * p + hd + 1) * LANES)
            qh = q_ref[b, :, cols]
            keys = [k_ref[b, :, cols] for k_ref in k_refs]
        else:
            q = q_ref[b, :, p * LANES:(p + 1) * LANES]
            qh = jnp.where(lower if hd == 0 else ~lower, q, jnp.zeros_like(q))
            keys = [k_ref[b] for k_ref in k_refs]
        return [lax.dot_general(k, qh, (((1,), (1,)), ((), ())), preferred_element_type=F32) for k in keys]

    def softmax_t(ss):
        m = functools.reduce(jnp.maximum, [jnp.max(s, axis=0, keepdims=True) for s in ss])
        es = [jnp.exp2(s - m) for s in ss]
        denom = functools.reduce(jnp.add, [jnp.sum(e, axis=0, keepdims=True) for e in es])
        return [e.astype(BF16) for e in es], denom

    def weighted_values_t(b, p, es, denom):
        rows = slice(p * LANES, (p + 1) * LANES) if mla else slice(None)
        acc = functools.reduce(jnp.add, [_dot(vt_ref[b, rows, :], e) for e, vt_ref in zip(es, vt_refs)])
        return acc * (1.0 / denom)

    ahead = 2
    work = [(b, p, hd) for b in range(q_ref.shape[0]) for p in range(HEAD_PAIRS) for hd in range(2)]
    pending = [scores_t(*w) for w in work[:ahead]]
    outs = {}
    for i, (b, p, hd) in enumerate(work):
        es, denom = softmax_t(pending.pop(0))
        if i + ahead < len(work):
            pending.append(scores_t(*work[i + ahead]))
        outs[b, p, hd] = weighted_values_t(b, p, es, denom)
    for b in range(q_ref.shape[0]):
        tiles = [jnp.where(top, outs[b, p, 0], outs[b, p, 1]).T for p in range(HEAD_PAIRS)]
        if not mla:
            t1 = pltpu.roll(tiles[1], HALF, 1)
            tiles = [jnp.where(lower, tiles[0], t1), jnp.where(lower, tiles[2], tiles[0]),
                     jnp.where(lower, t1, tiles[2])]
        for p in range(HEAD_PAIRS):
            o_ref[b, :, p * LANES:(p + 1) * LANES] = tiles[p].astype(o_ref.dtype)


def _attention(q, kv_segments, *, mla, tq, bb=1):
    n_batch, sq = q.shape[:2]
    n_seg = len(kv_segments)
    whole = lambda a: pl.BlockSpec((bb,) + a.shape[1:], lambda b, i: (b, 0, 0))
    in_specs = [pl.BlockSpec((bb, tq, q.shape[2]), lambda b, i: (b, i, 0))]
    in_specs += [whole(k) for k, _ in kv_segments] + [whole(v) for _, v in kv_segments]
    return pl.pallas_call(
        functools.partial(_attn_kernel, n_seg=n_seg, mla=mla),
        grid=(n_batch // bb, sq // tq),
        in_specs=in_specs,
        out_specs=pl.BlockSpec((bb, tq, MIX_TILE), lambda b, i: (b, i, 0)),
        out_shape=jax.ShapeDtypeStruct((n_batch, sq, MIX_TILE), BF16),
        compiler_params=pltpu.CompilerParams(dimension_semantics=("arbitrary",) * 2, vmem_limit_bytes=VMEM_LIMIT),
        name=f"attn_{'mla' if mla else 'gqa'}_s{n_seg}",
    )(q, *[k for k, _ in kv_segments], *[v for _, v in kv_segments])


def _fnet_kernel(ca_ref, sa_ref, cb_ref, sb_ref, uc_ref, us_ref, o_ref, tc_ref, ts_ref, *, scale):
    @pl.when(pl.program_id(1) == 0)
    def _build_twiddles():
        cb, sb = cb_ref[...], sb_ref[...]
        for a in range(ca_ref.shape[0]):
            ca, sa = ca_ref[a:a + 1, :], sa_ref[a:a + 1, :]
            rows = slice(a * DFT_SPLIT, (a + 1) * DFT_SPLIT)
            tc_ref[rows, :] = (ca * cb - sa * sb).astype(BF16)
            ts_ref[rows, :] = (sa * cb + ca * sb).astype(BF16)

    acc = _dot(tc_ref[...], uc_ref[...]) - _dot(ts_ref[...], us_ref[...])
    o_ref[...] = (acc * scale).astype(o_ref.dtype)


def _fnet(tables, uc, us, tm, tn):
    seq, width = uc.shape
    scale = float((seq * FNET_GROUP_DIM) ** -0.5)
    n_a = tm // DFT_SPLIT
    part_a = pl.BlockSpec((n_a, seq), lambda i, j: (i, 0))
    part_b = pl.BlockSpec((DFT_SPLIT, seq), lambda i, j: (0, 0))
    data = pl.BlockSpec((seq, tn), lambda i, j: (0, j))
    return pl.pallas_call(
        functools.partial(_fnet_kernel, scale=scale),
        grid=(seq // tm, width // tn),
        in_specs=[part_a, part_a, part_b, part_b, data, data],
        out_specs=pl.BlockSpec((tm, tn), lambda i, j: (i, j)),
        out_shape=jax.ShapeDtypeStruct((seq, width), BF16),
        scratch_shapes=[pltpu.VMEM((tm, seq), BF16), pltpu.VMEM((tm, seq), BF16)],
        compiler_params=pltpu.CompilerParams(dimension_semantics=("arbitrary", "arbitrary"),
                                             vmem_limit_bytes=VMEM_LIMIT),
        name="fnet",
    )(*tables, uc, us)


def _post_kernel(x_ref, mla_ref, fn_ref, gqa_ref, mod_ref, g2_ref, wo_ref, wg_ref, wu_ref, wd_ref, gf_ref,
                 o_ref, *, final):
    fn = fn_ref[...]
    if fn.shape[1] > FNET_WIDTH:
        fn = jnp.concatenate([fn[:, j:j + FNET_WIDTH] for j in range(0, fn.shape[1], FNET_WIDTH)], axis=0)
    mix = jnp.concatenate([mla_ref[...], fn, gqa_ref[...]], axis=-1)
    gate1 = mod_ref[:, 2 * D_MODEL:3 * D_MODEL]
    shift2 = mod_ref[:, 3 * D_MODEL:4 * D_MODEL]
    scale2 = mod_ref[:, 4 * D_MODEL:5 * D_MODEL]
    gate2 = mod_ref[:, 5 * D_MODEL:6 * D_MODEL]
    x = x_ref[...] + gate1 * _dot(mix, wo_ref[...])
    h = (_rms(x) * g2_ref[...] * (1.0 + scale2) + shift2).astype(BF16)
    g = _dot(h, wg_ref[...])
    u = _dot(h, wu_ref[...])
    a = (g * jax.nn.sigmoid(g) * u).astype(BF16)
    x = x + gate2 * _dot(a, wd_ref[...])
    if final:
        x = _rms(x) * gf_ref[...]
    o_ref[...] = x


def _post(x, mla_o, fnet_o, gqa_o, mod, wts, layer, g_final, *, seq, final, tm):
    n_tok = x.shape[0]
    n_tiles = n_tok // tm
    tiles_per_mod = n_tiles // mod.shape[0]
    resident = lambda arr: pl.BlockSpec((None,) + arr.shape[1:], lambda i: (layer, 0, 0), pipeline_mode=pl.Buffered(1))
    tok = lambda w: pl.BlockSpec((tm, w), lambda i: (i, 0))
    if tm <= seq:
        tps = seq // tm
        fnet_spec = pl.BlockSpec((tm, FNET_WIDTH), lambda i: (i % tps, i // tps))
    else:
        fnet_spec = pl.BlockSpec((seq, (tm // seq) * FNET_WIDTH), lambda i: (0, i))
    names = ("g2", "w_out", "w_gate", "w_up", "w_down")
    return pl.pallas_call(
        functools.partial(_post_kernel, final=final),
        grid=(n_tiles,),
        in_specs=[tok(D_MODEL), tok(MIX_TILE), fnet_spec, tok(MIX_TILE),
                  pl.BlockSpec((None, 1, 6 * D_MODEL), lambda i: (i // tiles_per_mod, 0, 0))]
                 + [resident(wts[n]) for n in names]
                 + [pl.BlockSpec((1, D_MODEL), lambda i: (0, 0))],
        out_specs=tok(D_MODEL),
        out_shape=jax.ShapeDtypeStruct((n_tok, D_MODEL), F32),
        compiler_params=pltpu.CompilerParams(dimension_semantics=("arbitrary",), vmem_limit_bytes=VMEM_LIMIT),
        name="post_final" if final else "post",
    )(x, mla_o, fnet_o, gqa_o, mod, *[wts[n] for n in names], g_final)


def _prep_weights(g_norm1, g_norm2, w_in, g_q_a, w_q_up, g_kv_a, w_kv_up, g_q_head, g_k_head,
                  w_out, w_ffn_gate, w_ffn_up, w_ffn_down):
    row = lambda g: g.reshape(DEPTH, 1, -1)
    w_kr = jnp.pad(w_in[:, :, _IN_A:_IN_C0], ((0, 0), (0, 0), (0, LANES - MLA_ROPE_DIM)))
    wq = w_q_up.reshape(DEPTH, MLA_Q_RANK, MLA_HEADS, MLA_QK_DIM)
    wq = jnp.concatenate([wq[..., MLA_NOPE_DIM:], wq[..., :MLA_NOPE_DIM]], axis=-1)
    wq = jnp.pad(wq, ((0, 0), (0, 0), (0, 0), (0, LANES - MLA_QK_DIM)))
    wkv = w_kv_up.reshape(DEPTH, MLA_KV_RANK, MLA_HEADS, MLA_NOPE_DIM + MLA_V_DIM)
    wk = jnp.pad(wkv[..., :MLA_NOPE_DIM], ((0, 0), (0, 0), (0, 0), (MLA_ROPE_DIM, LANES - MLA_QK_DIM)))
    wv = wkv[..., MLA_NOPE_DIM:]
    return {
        "g1": row(g_norm1), "g2": row(g_norm2), "g_q_a": row(g_q_a), "g_kv_a": row(g_kv_a),
        "g_q_head": row(jnp.tile(g_q_head, (1, GQA_HEADS))), "g_k_head": row(jnp.tile(g_k_head, (1, GQA_KV_HEADS))),
        "w_a": w_in[:, :, :_IN_A].astype(BF16), "w_c": w_in[:, :, _IN_C0:].astype(BF16), "w_kr": w_kr.astype(BF16),
        "wq": wq.reshape(DEPTH, MLA_Q_RANK, MLA_PAD_WIDTH).astype(BF16),
        "wk": wk.reshape(DEPTH, MLA_KV_RANK, MLA_PAD_WIDTH).astype(BF16),
        "wv": wv.reshape(DEPTH, MLA_KV_RANK, MIX_TILE).astype(BF16),
        "w_out": w_out.astype(BF16), "w_gate": w_ffn_gate.astype(BF16),
        "w_up": w_ffn_up.astype(BF16), "w_down": w_ffn_down.astype(BF16),
    }


def kernel(x_prompt, x_sample, cache_mla_ckv, cache_mla_krope, cache_gqa_k, cache_gqa_v, c, c_ctx, w_ada, b_ada,
           g_norm1, g_norm2, w_in, g_q_a, w_q_up, g_kv_a, w_kv_up, g_q_head, g_k_head, w_out,
           w_ffn_gate, w_ffn_up, w_ffn_down, g_final):
    n_pb, p_seq, _ = x_prompt.shape
    n_sb, s_seq, _ = x_sample.shape
    consts = {"wdft": _channel_dft(), "mean_q": _group_mean_matrix(GQA_HEADS),
              "mean_k": _group_mean_matrix(GQA_KV_HEADS), "place": _rope_place_matrix(),
              "rope": _rope_tables(s_seq)}
    dft_p = _seq_dft_tables(p_seq)
    dft_s = _seq_dft_tables(s_seq)
    gf = g_final.reshape(1, D_MODEL)
    wts = _prep_weights(g_norm1, g_norm2, w_in, g_q_a, w_q_up, g_kv_a, w_kv_up, g_q_head, g_k_head,
                        w_out, w_ffn_gate, w_ffn_up, w_ffn_down)

    cond_t = jnp.concatenate([c_ctx[None, :], c], axis=0).T
    mod = _ada(cond_t, w_ada, b_ada)

    xp = x_prompt.reshape(n_pb * p_seq, D_MODEL)
    xs = x_sample.reshape(n_sb * s_seq, D_MODEL)
    b3 = lambda a, nb: a.reshape(nb, a.shape[0] // nb, a.shape[1])
    caches = None
    for l in range(DEPTH):
        final = l == DEPTH - 1
        mod_p = mod[l, 0:1].reshape(1, 1, 6 * D_MODEL)
        mod_s = mod[l, 1:].reshape(n_sb, 1, 6 * D_MODEL)

        qm, km, vm, uc, us, gq, gk, gv, *caches = _proj(
            xp, mod_p, wts, l, consts, seq=p_seq, rope=False, emit_cache=True, prev_cache=caches)
        mla_o = _attention(b3(qm, n_pb), [(b3(km, n_pb), vm)], mla=True, tq=p_seq, bb=2)
        gqa_o = _attention(b3(gq, n_pb), [(b3(gk, n_pb), gv)], mla=False, tq=p_seq, bb=2)
        fn_o = _fnet(dft_p, uc, us, tm=p_seq, tn=1024)
        xp = _post(xp, mla_o.reshape(-1, MIX_TILE), fn_o, gqa_o.reshape(-1, MIX_TILE), mod_p, wts, l, gf,
                   seq=p_seq, final=final, tm=POST_TILE)

        km_c, vm_c, gk_c, gv_c = _ctx_prep(l, cache_mla_ckv, cache_mla_krope, cache_gqa_k, cache_gqa_v, wts, consts)
        qm, km, vm, uc, us, gq, gk, gv = _proj(xs, mod_s, wts, l, consts, seq=s_seq, rope=True)
        mla_o = _attention(b3(qm, n_sb), [(km_c, vm_c), (b3(km, n_sb), vm)], mla=True, tq=512)
        gqa_o = _attention(b3(gq, n_sb), [(gk_c, gv_c), (b3(gk, n_sb), gv)], mla=False, tq=512)
        fn_o = _fnet(dft_s, uc, us, tm=512, tn=n_sb * FNET_WIDTH)
        xs = _post(xs, mla_o.reshape(-1, MIX_TILE), fn_o, gqa_o.reshape(-1, MIX_TILE), mod_s, wts, l, gf,
                   seq=s_seq, final=final, tm=POST_TILE)

    ckv_new, krope_new, gk_new, gv_new = caches
    heads = (n_pb, DEPTH, p_seq, GQA_KV_HEADS, GQA_HEAD_DIM)
    return (xp.reshape(n_pb, p_seq, D_MODEL), xs.reshape(n_sb, s_seq, D_MODEL),
            ckv_new, krope_new, gk_new.reshape(heads), gv_new.reshape(heads))
```

```python
import functools

import numpy as np
import jax
import jax.numpy as jnp
from jax import lax
from jax.experimental import pallas as pl
from jax.experimental.pallas import tpu as pltpu

D_MODEL = 1024
DEPTH = 2
GRID_W = 64
MLA_HEADS = 6
MLA_Q_RANK = 384
MLA_KV_RANK = 256
MLA_NOPE_DIM = 64
MLA_ROPE_DIM = 32
MLA_V_DIM = 64
MLA_QK_DIM = MLA_NOPE_DIM + MLA_ROPE_DIM
FNET_GROUPS = 4
FNET_GROUP_DIM = 64
FNET_WIDTH = FNET_GROUPS * FNET_GROUP_DIM
GQA_HEADS = 6
GQA_KV_HEADS = 2
GQA_HEAD_DIM = 64
GQA_GROUP = GQA_HEADS // GQA_KV_HEADS
GQA_WIDTH = GQA_HEADS * GQA_HEAD_DIM
GQA_KV_WIDTH = GQA_KV_HEADS * GQA_HEAD_DIM
D_FF = 2816
ROPE_THETA = 10000.0
EPS = 1e-6
LOG2_E = 1.4426950408889634

LANES = 128
HALF = LANES // 2
HEAD_PAIRS = MLA_HEADS // 2
MLA_PAD_WIDTH = MLA_HEADS * LANES
MIX_TILE = HEAD_PAIRS * LANES
VMEM_LIMIT = 52 * 1024 * 1024
POST_TILE = 512
PROJ_SUB = 256

_IN_A = MLA_Q_RANK + MLA_KV_RANK
_IN_C0 = _IN_A + MLA_ROPE_DIM
_IN_C = FNET_WIDTH + GQA_WIDTH + 2 * GQA_KV_WIDTH
_C_U, _C_GQ, _C_GK, _C_GV = 0, FNET_WIDTH, FNET_WIDTH + GQA_WIDTH, FNET_WIDTH + GQA_WIDTH + GQA_KV_WIDTH

BF16 = jnp.bfloat16
F32 = jnp.float32


def _rope_tables(n_tokens):
    t = np.arange(n_tokens)
    row = (t // GRID_W).astype(np.float64)
    col = (t % GRID_W).astype(np.float64)

    def angles(rot_dim):
        n_axis = rot_dim // 4
        inv = ROPE_THETA ** (-np.arange(n_axis, dtype=np.float64) / n_axis)
        return np.concatenate([row[:, None] * inv, col[:, None] * inv], axis=-1)

    def tables(ang, lane_to_pair, is_first, is_second):
        cos = np.where((is_first | is_second)[None, :], np.cos(ang)[:, lane_to_pair], 1.0)
        sin = np.sin(ang)[:, lane_to_pair]
        sin_a = np.where(is_second[None, :], sin, 0.0)
        sin_b = np.where(is_first[None, :], -sin, 0.0)
        return [cos, sin_a, sin_b]

    lane = np.arange(LANES)
    half_m = MLA_ROPE_DIM // 2
    in_rope = lane < MLA_ROPE_DIM
    first_m = lane < half_m
    second_m = in_rope & ~first_m
    pair_m = lane % half_m
    half_g = GQA_HEAD_DIM // 2
    first_g = (lane % GQA_HEAD_DIM) < half_g
    pair_g = lane % half_g
    tabs = (tables(angles(MLA_ROPE_DIM), pair_m, first_m, second_m)
            + tables(angles(GQA_HEAD_DIM), pair_g, first_g, ~first_g))
    return jnp.asarray(np.concatenate(tabs, axis=-1), dtype=F32)


def _channel_dft():
    c = np.arange(FNET_GROUP_DIM)
    ang = 2.0 * np.pi * np.outer(c, c) / FNET_GROUP_DIM
    eye = np.eye(FNET_GROUPS)
    table = np.concatenate([np.kron(eye, np.cos(ang)), np.kron(eye, np.sin(ang))], axis=1)
    return jnp.asarray(table, dtype=F32).astype(BF16)


def _group_mean_matrix(n_heads):
    return jnp.asarray(np.kron(np.eye(n_heads), np.full((GQA_HEAD_DIM, GQA_HEAD_DIM), 1.0 / GQA_HEAD_DIM)), dtype=BF16)


def _rope_place_matrix():
    return jnp.asarray(np.eye(MLA_ROPE_DIM, LANES), dtype=BF16)


DFT_SPLIT = 32


def _seq_dft_tables(seq):
    s = np.arange(seq)
    n_a = seq // DFT_SPLIT
    ang_a = 2.0 * np.pi * ((np.arange(n_a)[:, None] * s[None, :]) % n_a) / n_a
    ang_b = 2.0 * np.pi * ((np.arange(DFT_SPLIT)[:, None] * s[None, :]) % seq) / seq
    return tuple(jnp.asarray(t, dtype=F32) for t in (np.cos(ang_a), np.sin(ang_a), np.cos(ang_b), np.sin(ang_b)))


def _rms(x):
    return x * lax.rsqrt(jnp.mean(x * x, axis=-1, keepdims=True) + EPS)


def _dot(a, b):
    return jnp.dot(a, b, preferred_element_type=F32)


def _group_rms(x, mean_mat):
    ms = _dot((x * x).astype(BF16), mean_mat)
    return x * lax.rsqrt(ms + EPS)


def _rotate(x, cos, sin_a, sin_b, half):
    return x * cos + pltpu.roll(x, half, 1) * sin_a + pltpu.roll(x, LANES - half, 1) * sin_b


def _lower_half(rows):
    return lax.broadcasted_iota(jnp.int32, (rows, LANES), 1) < HALF


def _layer_spec(arr, layer):
    return pl.BlockSpec((None,) + arr.shape[1:], lambda i: (layer, 0, 0))


def _ada_kernel(ct_ref, w_ref, b_ref, o_ref):
    s = ct_ref[...]
    s = s * jax.nn.sigmoid(s)
    w = w_ref[...]
    for m in range(o_ref.shape[0]):
        o_ref[m:m + 1, :] = jnp.sum(w * s[:, m:m + 1], axis=0, keepdims=True) + b_ref[...]


def _ada(cond_t, w_ada, b_ada, tn=2048):
    n_cond = cond_t.shape[1]
    width = w_ada.shape[2]
    return pl.pallas_call(
        _ada_kernel,
        grid=(DEPTH, width // tn),
        in_specs=[
            pl.BlockSpec((D_MODEL, n_cond), lambda l, j: (0, 0)),
            pl.BlockSpec((None, D_MODEL, tn), lambda l, j: (l, 0, j)),
            pl.BlockSpec((None, 1, tn), lambda l, j: (l, 0, j)),
        ],
        out_specs=pl.BlockSpec((None, n_cond, tn), lambda l, j: (l, 0, j)),
        out_shape=jax.ShapeDtypeStruct((DEPTH, n_cond, width), F32),
        compiler_params=pltpu.CompilerParams(dimension_semantics=("arbitrary", "arbitrary"),
                                             vmem_limit_bytes=VMEM_LIMIT),
        name="ada",
    )(cond_t, w_ada, b_ada.reshape(DEPTH, 1, width))


_PROJ_WEIGHTS = ("g1", "w_a", "w_c", "w_kr", "g_q_a", "wq", "g_kv_a", "wk", "wv", "g_q_head", "g_k_head")


def _proj_kernel(*refs, rope, emit_cache, n_prev, multi_seq):
    (x_ref, mod_ref, g1_ref, wa_ref, wc_ref, wkr_ref, gqa_ref, wq_ref, gkva_ref, wk_ref, wv_ref,
     gqh_ref, gkh_ref, wdft_ref, mq_ref, mk_ref) = refs[:16]
    refs = refs[16:]
    if rope:
        tab_ref, refs = refs[0], refs[1:]
    if n_prev:
        prev_refs, refs = refs[:4], refs[4:]
    qm_ref, km_ref, vm_ref, uc_ref, us_ref, gq_ref, gk_ref, gv_ref = refs[:8]
    cache_refs = refs[8:]

    shift = mod_ref[:, 0:D_MODEL]
    scale = mod_ref[:, D_MODEL:2 * D_MODEL]
    lower = _lower_half(PROJ_SUB)

    for r in range(x_ref.shape[0] // PROJ_SUB):
        rows = slice(r * PROJ_SUB, (r + 1) * PROJ_SUB)
        seq_cols = (slice(None), slice(r * FNET_WIDTH, (r + 1) * FNET_WIDTH)) if multi_seq else (rows, slice(None))
        t_idx = (r,) if multi_seq else (slice(None), rows)

        hb = (_rms(x_ref[rows, :]) * g1_ref[...] * (1.0 + scale) + shift).astype(BF16)
        pa = _dot(hb, wa_ref[...])
        pc = _dot(hb, wc_ref[...])
        kr = _dot(hb, wkr_ref[...])

        if rope:
            cos_m, sa_m, sb_m = (tab_ref[rows, i * LANES:(i + 1) * LANES] for i in range(3))
            cos_g, sa_g, sb_g = (tab_ref[rows, i * LANES:(i + 1) * LANES] for i in range(3, 6))

        gqn = _group_rms(pc[:, _C_GQ:_C_GQ + GQA_WIDTH], mq_ref[...]) * gqh_ref[...]
        tiles = []
        for p in range(HEAD_PAIRS):
            gp = gqn[:, p * LANES:(p + 1) * LANES]
            if rope:
                gp = _rotate(gp, cos_g, sa_g, sb_g, GQA_HEAD_DIM // 2)
            tiles.append(gp * (LOG2_E * GQA_HEAD_DIM ** -0.5))
        gq_ref[rows, 0:LANES] = jnp.where(lower, tiles[0], tiles[1]).astype(BF16)
        gq_ref[rows, LANES:2 * LANES] = jnp.where(
            lower, pltpu.roll(tiles[0], HALF, 1), pltpu.roll(tiles[2], HALF, 1)).astype(BF16)
        gq_ref[rows, 2 * LANES:] = jnp.where(lower, tiles[1], tiles[2]).astype(BF16)
        gkn = _group_rms(pc[:, _C_GK:_C_GK + GQA_KV_WIDTH], mk_ref[...]) * gkh_ref[...]
        gkr = _rotate(gkn, cos_g, sa_g, sb_g, GQA_HEAD_DIM // 2) if rope else gkn
        gv = pc[:, _C_GV:_C_GV + GQA_KV_WIDTH]
        gk_ref[rows, :] = gkr.astype(BF16)
        gv_ref[t_idx] = gv.T.astype(BF16)

        cqn = _rms(pa[:, :MLA_Q_RANK]) * gqa_ref[...]
        q = _dot(cqn.astype(BF16), wq_ref[...])
        for hd in range(MLA_HEADS):
            qh = q[:, hd * LANES:(hd + 1) * LANES]
            if rope:
                qh = _rotate(qh, cos_m, sa_m, sb_m, MLA_ROPE_DIM // 2)
            qm_ref[rows, hd * LANES:(hd + 1) * LANES] = (qh * (LOG2_E * MLA_QK_DIM ** -0.5)).astype(BF16)

        ckvn = _rms(pa[:, MLA_Q_RANK:]) * gkva_ref[...]
        cb = ckvn.astype(BF16)
        kn = _dot(cb, wk_ref[...])
        krr = _rotate(kr, cos_m, sa_m, sb_m, MLA_ROPE_DIM // 2) if rope else kr
        for hd in range(MLA_HEADS):
            km_ref[rows, hd * LANES:(hd + 1) * LANES] = (kn[:, hd * LANES:(hd + 1) * LANES] + krr).astype(BF16)
        vm_ref[t_idx] = _dot(cb, wv_ref[...]).T.astype(BF16)

        ucs = _dot(pc[:, _C_U:_C_U + FNET_WIDTH].astype(BF16), wdft_ref[...])
        uc_ref[seq_cols] = ucs[:, :FNET_WIDTH].astype(BF16)
        us_ref[seq_cols] = ucs[:, FNET_WIDTH:].astype(BF16)

        if emit_cache:
            new = (ckvn, kr[:, :MLA_ROPE_DIM], gkn, gv)
            for i, out_ref in enumerate(cache_refs):
                if n_prev:
                    out_ref[r, :n_prev] = prev_refs[i][r]
                out_ref[r, n_prev] = new[i]


def _proj(x, mod, wts, layer, consts, *, seq, rope, prev_cache=None, emit_cache=False, tm=512):
    n_tok = x.shape[0]
    n_tiles = n_tok // tm
    n_seq = n_tok // seq
    tiles_per_mod = n_tiles // mod.shape[0]
    multi_seq = tm > seq
    if multi_seq:
        assert seq == PROJ_SUB and not rope
        spt = tm // seq
        seq_major = pl.BlockSpec((seq, spt * FNET_WIDTH), lambda i: (0, i))
        transposed = lambda w: pl.BlockSpec((spt, w, seq), lambda i: (i, 0, 0))
    else:
        assert not emit_cache
        tps = seq // tm
        seq_major = pl.BlockSpec((tm, FNET_WIDTH), lambda i: (i % tps, i // tps))
        transposed = lambda w: pl.BlockSpec((None, w, tm), lambda i: (i // tps, 0, i % tps))
    const = lambda arr: pl.BlockSpec(arr.shape, lambda i: (0,) * arr.ndim)
    tok = lambda w: pl.BlockSpec((tm, w), lambda i: (i, 0))
    in_specs = [tok(D_MODEL), pl.BlockSpec((None, 1, 6 * D_MODEL), lambda i: (i // tiles_per_mod, 0, 0))]
    args = [x, mod]
    for name in _PROJ_WEIGHTS:
        in_specs.append(_layer_spec(wts[name], layer))
        args.append(wts[name])
    for name in ("wdft", "mean_q", "mean_k"):
        in_specs.append(const(consts[name]))
        args.append(consts[name])
    if rope:
        in_specs.append(pl.BlockSpec((tm, 6 * LANES), lambda i: (i % tps, 0)))
        args.append(consts["rope"])
    out_specs = [tok(MLA_PAD_WIDTH), tok(MLA_PAD_WIDTH), transposed(MIX_TILE), seq_major, seq_major,
                 tok(GQA_WIDTH), tok(GQA_KV_WIDTH), transposed(GQA_KV_WIDTH)]
    out_shape = [jax.ShapeDtypeStruct((n_tok, MLA_PAD_WIDTH), BF16),
                 jax.ShapeDtypeStruct((n_tok, MLA_PAD_WIDTH), BF16),
                 jax.ShapeDtypeStruct((n_seq, MIX_TILE, seq), BF16),
                 jax.ShapeDtypeStruct((seq, n_seq * FNET_WIDTH), BF16),
                 jax.ShapeDtypeStruct((seq, n_seq * FNET_WIDTH), BF16),
                 jax.ShapeDtypeStruct((n_tok, GQA_WIDTH), BF16),
                 jax.ShapeDtypeStruct((n_tok, GQA_KV_WIDTH), BF16),
                 jax.ShapeDtypeStruct((n_seq, GQA_KV_WIDTH, seq), BF16)]
    n_prev = 0
    if emit_cache:
        n_prev = prev_cache[0].shape[1] if prev_cache is not None else 0
        layers = lambda n, w: pl.BlockSpec((spt, n, seq, w), lambda i: (i, 0, 0, 0))
        widths = (MLA_KV_RANK, MLA_ROPE_DIM, GQA_KV_WIDTH, GQA_KV_WIDTH)
        if n_prev:
            in_specs += [layers(n_prev, w) for w in widths]
            args += list(prev_cache)
        out_specs += [layers(n_prev + 1, w) for w in widths]
        out_shape += [jax.ShapeDtypeStruct((n_seq, n_prev + 1, seq, w), F32) for w in widths]
    return pl.pallas_call(
        functools.partial(_proj_kernel, rope=rope, emit_cache=emit_cache, n_prev=n_prev, multi_seq=multi_seq),
        grid=(n_tiles,),
        in_specs=in_specs,
        out_specs=out_specs,
        out_shape=out_shape,
        compiler_params=pltpu.CompilerParams(dimension_semantics=("arbitrary",), vmem_limit_bytes=VMEM_LIMIT),
        name="proj_rope" if rope else "proj_ctx",
    )(*args)


def _ctx_kernel(ckv_ref, kr_ref, gk_ref, gv_ref, wk_ref, wv_ref, place_ref, km_ref, vm_ref, gko_ref, gvo_ref):
    cb = ckv_ref[...].astype(BF16)
    kn = _dot(cb, wk_ref[...])
    kr = _dot(kr_ref[...].astype(BF16), place_ref[...])
    for hd in range(MLA_HEADS):
        km_ref[:, hd * LANES:(hd + 1) * LANES] = (kn[:, hd * LANES:(hd + 1) * LANES] + kr).astype(BF16)
    vm_ref[...] = _dot(cb, wv_ref[...]).T.astype(BF16)
    gko_ref[...] = gk_ref[...].astype(BF16)
    gvo_ref[...] = gv_ref[...].T.astype(BF16)


def _ctx_prep(layer, cache_ckv, cache_krope, cache_gk, cache_gv, wts, consts):
    nb, _, past, _ = cache_ckv.shape
    cache = lambda w: pl.BlockSpec((None, None, past, w), lambda b: (b, layer, 0, 0))
    out = lambda w: pl.BlockSpec((None, past, w), lambda b: (b, 0, 0))
    out_t = lambda w: pl.BlockSpec((None, w, past), lambda b: (b, 0, 0))
    return pl.pallas_call(
        _ctx_kernel,
        grid=(nb,),
        in_specs=[cache(MLA_KV_RANK), cache(MLA_ROPE_DIM), cache(GQA_KV_WIDTH), cache(GQA_KV_WIDTH),
                  _layer_spec(wts["wk"], layer), _layer_spec(wts["wv"], layer),
                  pl.BlockSpec((MLA_ROPE_DIM, LANES), lambda b: (0, 0))],
        out_specs=[out(MLA_PAD_WIDTH), out_t(MIX_TILE), out(GQA_KV_WIDTH), out_t(GQA_KV_WIDTH)],
        out_shape=[jax.ShapeDtypeStruct((nb, past, MLA_PAD_WIDTH), BF16),
                   jax.ShapeDtypeStruct((nb, MIX_TILE, past), BF16),
                   jax.ShapeDtypeStruct((nb, past, GQA_KV_WIDTH), BF16),
                   jax.ShapeDtypeStruct((nb, GQA_KV_WIDTH, past), BF16)],
        compiler_params=pltpu.CompilerParams(dimension_semantics=("arbitrary",), vmem_limit_bytes=VMEM_LIMIT),
        name="ctx_prep",
    )(cache_ckv, cache_krope, cache_gk.reshape(nb, DEPTH, past, GQA_KV_WIDTH),
      cache_gv.reshape(nb, DEPTH, past, GQA_KV_WIDTH), wts["wk"], wts["wv"], consts["place"])


def _attn_kernel(*refs, n_seg, mla):
    q_ref, o_ref = refs[0], refs[-1]
    k_refs = refs[1:1 + n_seg]
    vt_refs = refs[1 + n_seg:1 + 2 * n_seg]
    tq = q_ref.shape[1]
    lower = _lower_half(tq)
    top = lax.broadcasted_iota(jnp.int32, (LANES, tq), 0) < HALF

    def scores_t(b, p, hd):
        if mla:
            cols = slice((2 * p + hd) * LANES, (2 * p + hd + 1) * LANES)
            qh = q_ref[b, :, cols]
            keys = [k_ref[b, :, cols] for k_ref in k_refs]
        else:
            q = q_ref[b, :, p * LANES:(p + 1) * LANES]
            qh = jnp.where(lower if hd == 0 else ~lower, q, jnp.zeros_like(q))
            keys = [k_ref[b] for k_ref in k_refs]
        return [lax.dot_general(k, qh, (((1,), (1,)), ((), ())), preferred_element_type=F32) for k in keys]

    def softmax_t(ss):
        m = functools.reduce(jnp.maximum, [jnp.max(s, axis=0, keepdims=True) for s in ss])
        es = [jnp.exp2(s - m) for s in ss]
        denom = functools.reduce(jnp.add, [jnp.sum(e, axis=0, keepdims=True) for e in es])
        return [e.astype(BF16) for e in es], denom

    def weighted_values_t(b, p, es, denom):
        rows = slice(p * LANES, (p + 1) * LANES) if mla else slice(None)
        acc = functools.reduce(jnp.add, [_dot(vt_ref[b, rows, :], e) for e, vt_ref in zip(es, vt_refs)])
        return acc * (1.0 / denom)

    ahead = 2
    work = [(b, p, hd) for b in range(q_ref.shape[0]) for p in range(HEAD_PAIRS) for hd in range(2)]
    pending = [scores_t(*w) for w in work[:ahead]]
    outs = {}
    for i, (b, p, hd) in enumerate(work):
        es, denom = softmax_t(pending.pop(0))
        if i + ahead < len(work):
            pending.append(scores_t(*work[i + ahead]))
        outs[b, p, hd] = weighted_values_t(b, p, es, denom)
    for b in range(q_ref.shape[0]):
        tiles = [jnp.where(top, outs[b, p, 0], outs[b, p, 1]).T for p in range(HEAD_PAIRS)]
        if not mla:
            t1 = pltpu.roll(tiles[1], HALF, 1)
            tiles = [jnp.where(lower, tiles[0], t1), jnp.where(lower, tiles[2], tiles[0]),
                     jnp.where(lower, t1, tiles[2])]
        for p in range(HEAD_PAIRS):
            o_ref[b, :, p * LANES:(p + 1) * LANES] = tiles[p].astype(o_ref.dtype)


def _attention(q, kv_segments, *, mla, tq, bb=1):
    n_batch, sq = q.shape[:2]
    n_seg = len(kv_segments)
    whole = lambda a: pl.BlockSpec((bb,) + a.shape[1:], lambda b, i: (b, 0, 0))
    in_specs = [pl.BlockSpec((bb, tq, q.shape[2]), lambda b, i: (b, i, 0))]
    in_specs += [whole(k) for k, _ in kv_segments] + [whole(v) for _, v in kv_segments]
    return pl.pallas_call(
        functools.partial(_attn_kernel, n_seg=n_seg, mla=mla),
        grid=(n_batch // bb, sq // tq),
        in_specs=in_specs,
        out_specs=pl.BlockSpec((bb, tq, MIX_TILE), lambda b, i: (b, i, 0)),
        out_shape=jax.ShapeDtypeStruct((n_batch, sq, MIX_TILE), BF16),
        compiler_params=pltpu.CompilerParams(dimension_semantics=("arbitrary",) * 2, vmem_limit_bytes=VMEM_LIMIT),
        name=f"attn_{'mla' if mla else 'gqa'}_s{n_seg}",
    )(q, *[k for k, _ in kv_segments], *[v for _, v in kv_segments])


def _fnet_kernel(ca_ref, sa_ref, cb_ref, sb_ref, uc_ref, us_ref, o_ref, tc_ref, ts_ref, *, scale):
    @pl.when(pl.program_id(1) == 0)
    def _build_twiddles():
        cb, sb = cb_ref[...], sb_ref[...]
        for a in range(ca_ref.shape[0]):
            ca, sa = ca_ref[a:a + 1, :], sa_ref[a:a + 1, :]
            rows = slice(a * DFT_SPLIT, (a + 1) * DFT_SPLIT)
            tc_ref[rows, :] = (ca * cb - sa * sb).astype(BF16)
            ts_ref[rows, :] = (sa * cb + ca * sb).astype(BF16)

    acc = _dot(tc_ref[...], uc_ref[...]) - _dot(ts_ref[...], us_ref[...])
    o_ref[...] = (acc * scale).astype(o_ref.dtype)


def _fnet(tables, uc, us, tm, tn):
    seq, width = uc.shape
    scale = float((seq * FNET_GROUP_DIM) ** -0.5)
    n_a = tm // DFT_SPLIT
    part_a = pl.BlockSpec((n_a, seq), lambda i, j: (i, 0))
    part_b = pl.BlockSpec((DFT_SPLIT, seq), lambda i, j: (0, 0))
    data = pl.BlockSpec((seq, tn), lambda i, j: (0, j))
    return pl.pallas_call(
        functools.partial(_fnet_kernel, scale=scale),
        grid=(seq // tm, width // tn),
        in_specs=[part_a, part_a, part_b, part_b, data, data],
        out_specs=pl.BlockSpec((tm, tn), lambda i, j: (i, j)),
        out_shape=jax.ShapeDtypeStruct((seq, width), BF16),
        scratch_shapes=[pltpu.VMEM((tm, seq), BF16), pltpu.VMEM((tm, seq), BF16)],
        compiler_params=pltpu.CompilerParams(dimension_semantics=("arbitrary", "arbitrary"),
                                             vmem_limit_bytes=VMEM_LIMIT),
        name="fnet",
    )(*tables, uc, us)


def _post_kernel(x_ref, mla_ref, fn_ref, gqa_ref, mod_ref, g2_ref, wo_ref, wg_ref, wu_ref, wd_ref, gf_ref,
                 o_ref, *, final):
    fn = fn_ref[...]
    if fn.shape[1] > FNET_WIDTH:
        fn = jnp.concatenate([fn[:, j:j + FNET_WIDTH] for j in range(0, fn.shape[1], FNET_WIDTH)], axis=0)
    mix = jnp.concatenate([mla_ref[...], fn, gqa_ref[...]], axis=-1)
    gate1 = mod_ref[:, 2 * D_MODEL:3 * D_MODEL]
    shift2 = mod_ref[:, 3 * D_MODEL:4 * D_MODEL]
    scale2 = mod_ref[:, 4 * D_MODEL:5 * D_MODEL]
    gate2 = mod_ref[:, 5 * D_MODEL:6 * D_MODEL]
    x = x_ref[...] + gate1 * _dot(mix, wo_ref[...])
    h = (_rms(x) * g2_ref[...] * (1.0 + scale2) + shift2).astype(BF16)
    g = _dot(h, wg_ref[...])
    u = _dot(h, wu_ref[...])
    a = (g * jax.nn.sigmoid(g) * u).astype(BF16)
    x = x + gate2 * _dot(a, wd_ref[...])
    if final:
        x = _rms(x) * gf_ref[...]
    o_ref[...] = x


def _post(x, mla_o, fnet_o, gqa_o, mod, wts, layer, g_final, *, seq, final, tm):
    n_tok = x.shape[0]
    n_tiles = n_tok // tm
    tiles_per_mod = n_tiles // mod.shape[0]
    resident = lambda arr: pl.BlockSpec((None,) + arr.shape[1:], lambda i: (layer, 0, 0), pipeline_mode=pl.Buffered(1))
    tok = lambda w: pl.BlockSpec((tm, w), lambda i: (i, 0))
    if tm <= seq:
        tps = seq // tm
        fnet_spec = pl.BlockSpec((tm, FNET_WIDTH), lambda i: (i % tps, i // tps))
    else:
        fnet_spec = pl.BlockSpec((seq, (tm // seq) * FNET_WIDTH), lambda i: (0, i))
    names = ("g2", "w_out", "w_gate", "w_up", "w_down")
    return pl.pallas_call(
        functools.partial(_post_kernel, final=final),
        grid=(n_tiles,),
        in_specs=[tok(D_MODEL), tok(MIX_TILE), fnet_spec, tok(MIX_TILE),
                  pl.BlockSpec((None, 1, 6 * D_MODEL), lambda i: (i // tiles_per_mod, 0, 0))]
                 + [resident(wts[n]) for n in names]
                 + [pl.BlockSpec((1, D_MODEL), lambda i: (0, 0))],
        out_specs=tok(D_MODEL),
        out_shape=jax.ShapeDtypeStruct((n_tok, D_MODEL), F32),
        compiler_params=pltpu.CompilerParams(dimension_semantics=("arbitrary",), vmem_limit_bytes=VMEM_LIMIT),
        name="post_final" if final else "post",
    )(x, mla_o, fnet_o, gqa_o, mod, *[wts[n] for n in names], g_final)


def _prep_weights(g_norm1, g_norm2, w_in, g_q_a, w_q_up, g_kv_a, w_kv_up, g_q_head, g_k_head,
                  w_out, w_ffn_gate, w_ffn_up, w_ffn_down):
    row = lambda g: g.reshape(DEPTH, 1, -1)
    w_kr = jnp.pad(w_in[:, :, _IN_A:_IN_C0], ((0, 0), (0, 0), (0, LANES - MLA_ROPE_DIM)))
    wq = w_q_up.reshape(DEPTH, MLA_Q_RANK, MLA_HEADS, MLA_QK_DIM)
    wq = jnp.concatenate([wq[..., MLA_NOPE_DIM:], wq[..., :MLA_NOPE_DIM]], axis=-1)
    wq = jnp.pad(wq, ((0, 0), (0, 0), (0, 0), (0, LANES - MLA_QK_DIM)))
    wkv = w_kv_up.reshape(DEPTH, MLA_KV_RANK, MLA_HEADS, MLA_NOPE_DIM + MLA_V_DIM)
    wk = jnp.pad(wkv[..., :MLA_NOPE_DIM], ((0, 0), (0, 0), (0, 0), (MLA_ROPE_DIM, LANES - MLA_QK_DIM)))
    wv = wkv[..., MLA_NOPE_DIM:]
    return {
        "g1": row(g_norm1), "g2": row(g_norm2), "g_q_a": row(g_q_a), "g_kv_a": row(g_kv_a),
        "g_q_head": row(jnp.tile(g_q_head, (1, GQA_HEADS))), "g_k_head": row(jnp.tile(g_k_head, (1, GQA_KV_HEADS))),
        "w_a": w_in[:, :, :_IN_A].astype(BF16), "w_c": w_in[:, :, _IN_C0:].astype(BF16), "w_kr": w_kr.astype(BF16),
        "wq": wq.reshape(DEPTH, MLA_Q_RANK, MLA_PAD_WIDTH).astype(BF16),
        "wk": wk.reshape(DEPTH, MLA_KV_RANK, MLA_PAD_WIDTH).astype(BF16),
        "wv": wv.reshape(DEPTH, MLA_KV_RANK, MIX_TILE).astype(BF16),
        "w_out": w_out.astype(BF16), "w_gate": w_ffn_gate.astype(BF16),
        "w_up": w_ffn_up.astype(BF16), "w_down": w_ffn_down.astype(BF16),
    }


def kernel(x_prompt, x_sample, cache_mla_ckv, cache_mla_krope, cache_gqa_k, cache_gqa_v, c, c_ctx, w_ada, b_ada,
           g_norm1, g_norm2, w_in, g_q_a, w_q_up, g_kv_a, w_kv_up, g_q_head, g_k_head, w_out,
           w_ffn_gate, w_ffn_up, w_ffn_down, g_final):
    n_pb, p_seq, _ = x_prompt.shape
    n_sb, s_seq, _ = x_sample.shape
    consts = {"wdft": _channel_dft(), "mean_q": _group_mean_matrix(GQA_HEADS),
              "mean_k": _group_mean_matrix(GQA_KV_HEADS), "place": _rope_place_matrix(),
              "rope": _rope_tables(s_seq)}
    dft_p = _seq_dft_tables(p_seq)
    dft_s = _seq_dft_tables(s_seq)
    gf = g_final.reshape(1, D_MODEL)
    wts = _prep_weights(g_norm1, g_norm2, w_in, g_q_a, w_q_up, g_kv_a, w_kv_up, g_q_head, g_k_head,
                        w_out, w_ffn_gate, w_ffn_up, w_ffn_down)

    cond_t = jnp.concatenate([c_ctx[None, :], c], axis=0).T
    mod = _ada(cond_t, w_ada, b_ada)

    xp = x_prompt.reshape(n_pb * p_seq, D_MODEL)
    xs = x_sample.reshape(n_sb * s_seq, D_MODEL)
    b3 = lambda a, nb: a.reshape(nb, a.shape[0] // nb, a.shape[1])
    caches = None
    for l in range(DEPTH):
        final = l == DEPTH - 1
        mod_p = mod[l, 0:1].reshape(1, 1, 6 * D_MODEL)
        mod_s = mod[l, 1:].reshape(n_sb, 1, 6 * D_MODEL)

        qm, km, vm, uc, us, gq, gk, gv, *caches = _proj(
            xp, mod_p, wts, l, consts, seq=p_seq, rope=False, emit_cache=True, prev_cache=caches)
        mla_o = _attention(b3(qm, n_pb), [(b3(km, n_pb), vm)], mla=True, tq=p_seq, bb=2)
        gqa_o = _attention(b3(gq, n_pb), [(b3(gk, n_pb), gv)], mla=False, tq=p_seq, bb=2)
        fn_o = _fnet(dft_p, uc, us, tm=p_seq, tn=1024)
        xp = _post(xp, mla_o.reshape(-1, MIX_TILE), fn_o, gqa_o.reshape(-1, MIX_TILE), mod_p, wts, l, gf,
                   seq=p_seq, final=final, tm=POST_TILE)

        km_c, vm_c, gk_c, gv_c = _ctx_prep(l, cache_mla_ckv, cache_mla_krope, cache_gqa_k, cache_gqa_v, wts, consts)
        qm, km, vm, uc, us, gq, gk, gv = _proj(xs, mod_s, wts, l, consts, seq=s_seq, rope=True)
        mla_o = _attention(b3(qm, n_sb), [(km_c, vm_c), (b3(km, n_sb), vm)], mla=True, tq=512)
        gqa_o = _attention(b3(gq, n_sb), [(gk_c, gv_c), (b3(gk, n_sb), gv)], mla=False, tq=512)
        fn_o = _fnet(dft_s, uc, us, tm=512, tn=n_sb * FNET_WIDTH)
        xs = _post(xs, mla_o.reshape(-1, MIX_TILE), fn_o, gqa_o.reshape(-1, MIX_TILE), mod_s, wts, l, gf,
                   seq=s_seq, final=final, tm=POST_TILE)

    ckv_new, krope_new, gk_new, gv_new = caches
    heads = (n_pb, DEPTH, p_seq, GQA_KV_HEADS, GQA_HEAD_DIM)
    return (xp.reshape(n_pb, p_seq, D_MODEL), xs.reshape(n_sb, s_seq, D_MODEL),
            ckv_new, krope_new, gk_new.reshape(heads), gv_new.reshape(heads))
```

```python
import functools

import numpy as np
import jax
import jax.numpy as jnp
from jax import lax
from jax.experimental import pallas as pl
from jax.experimental.pallas import tpu as pltpu

D_MODEL = 1024
DEPTH = 2
GRID_W = 64
MLA_HEADS = 6
MLA_Q_RANK = 384
MLA_KV_RANK = 256
MLA_NOPE_DIM = 64
MLA_ROPE_DIM = 32
MLA_V_DIM = 64
MLA_QK_DIM = MLA_NOPE_DIM + MLA_ROPE_DIM
FNET_GROUPS = 4
FNET_GROUP_DIM = 64
FNET_WIDTH = FNET_GROUPS * FNET_GROUP_DIM
GQA_HEADS = 6
GQA_KV_HEADS = 2
GQA_HEAD_DIM = 64
GQA_GROUP = GQA_HEADS // GQA_KV_HEADS
GQA_WIDTH = GQA_HEADS * GQA_HEAD_DIM
GQA_KV_WIDTH = GQA_KV_HEADS * GQA_HEAD_DIM
D_FF = 2816
ROPE_THETA = 10000.0
EPS = 1e-6
LOG2_E = 1.4426950408889634

LANES = 128
HALF = LANES // 2
HEAD_PAIRS = MLA_HEADS // 2
MLA_PAD_WIDTH = MLA_HEADS * LANES
MIX_TILE = HEAD_PAIRS * LANES
VMEM_LIMIT = 52 * 1024 * 1024
POST_TILE = 512
PROJ_SUB = 256

_IN_A = MLA_Q_RANK + MLA_KV_RANK
_IN_C0 = _IN_A + MLA_ROPE_DIM
_IN_C = FNET_WIDTH + GQA_WIDTH + 2 * GQA_KV_WIDTH
_C_U, _C_GQ, _C_GK, _C_GV = 0, FNET_WIDTH, FNET_WIDTH + GQA_WIDTH, FNET_WIDTH + GQA_WIDTH + GQA_KV_WIDTH

BF16 = jnp.bfloat16
F32 = jnp.float32


def _rope_tables(n_tokens):
    t = np.arange(n_tokens)
    row = (t // GRID_W).astype(np.float64)
    col = (t % GRID_W).astype(np.float64)

    def angles(rot_dim):
        n_axis = rot_dim // 4
        inv = ROPE_THETA ** (-np.arange(n_axis, dtype=np.float64) / n_axis)
        return np.concatenate([row[:, None] * inv, col[:, None] * inv], axis=-1)

    def tables(ang, lane_to_pair, is_first, is_second):
        cos = np.where((is_first | is_second)[None, :], np.cos(ang)[:, lane_to_pair], 1.0)
        sin = np.sin(ang)[:, lane_to_pair]
        sin_a = np.where(is_second[None, :], sin, 0.0)
        sin_b = np.where(is_first[None, :], -sin, 0.0)
        return [cos, sin_a, sin_b]

    lane = np.arange(LANES)
    half_m = MLA_ROPE_DIM // 2
    in_rope = lane < MLA_ROPE_DIM
    first_m = lane < half_m
    second_m = in_rope & ~first_m
    pair_m = lane % half_m
    half_g = GQA_HEAD_DIM // 2
    first_g = (lane % GQA_HEAD_DIM) < half_g
    pair_g = lane % half_g
    tabs = (tables(angles(MLA_ROPE_DIM), pair_m, first_m, second_m)
            + tables(angles(GQA_HEAD_DIM), pair_g, first_g, ~first_g))
    return jnp.asarray(np.concatenate(tabs, axis=-1), dtype=F32)


def _channel_dft():
    c = np.arange(FNET_GROUP_DIM)
    ang = 2.0 * np.pi * np.outer(c, c) / FNET_GROUP_DIM
    eye = np.eye(FNET_GROUPS)
    table = np.concatenate([np.kron(eye, np.cos(ang)), np.kron(eye, np.sin(ang))], axis=1)
    return jnp.asarray(table, dtype=F32).astype(BF16)


def _group_mean_matrix(n_heads):
    return jnp.asarray(np.kron(np.eye(n_heads), np.full((GQA_HEAD_DIM, GQA_HEAD_DIM), 1.0 / GQA_HEAD_DIM)), dtype=BF16)


def _rope_place_matrix():
    return jnp.asarray(np.eye(MLA_ROPE_DIM, LANES), dtype=BF16)


DFT_SPLIT = 32


def _seq_dft_tables(seq):
    s = np.arange(seq)
    n_a = seq // DFT_SPLIT
    ang_a = 2.0 * np.pi * ((np.arange(n_a)[:, None] * s[None, :]) % n_a) / n_a
    ang_b = 2.0 * np.pi * ((np.arange(DFT_SPLIT)[:, None] * s[None, :]) % seq) / seq
    return tuple(jnp.asarray(t, dtype=F32) for t in (np.cos(ang_a), np.sin(ang_a), np.cos(ang_b), np.sin(ang_b)))


def _rms(x):
    return x * lax.rsqrt(jnp.mean(x * x, axis=-1, keepdims=True) + EPS)


def _dot(a, b):
    return jnp.dot(a, b, preferred_element_type=F32)


def _group_rms(x, mean_mat):
    ms = _dot((x * x).astype(BF16), mean_mat)
    return x * lax.rsqrt(ms + EPS)


def _rotate(x, cos, sin_a, sin_b, half):
    return x * cos + pltpu.roll(x, half, 1) * sin_a + pltpu.roll(x, LANES - half, 1) * sin_b


def _lower_half(rows):
    return lax.broadcasted_iota(jnp.int32, (rows, LANES), 1) < HALF


def _layer_spec(arr, layer):
    return pl.BlockSpec((None,) + arr.shape[1:], lambda i: (layer, 0, 0))


def _ada_kernel(ct_ref, w_ref, b_ref, o_ref):
    s = ct_ref[...]
    s = s * jax.nn.sigmoid(s)
    w = w_ref[...]
    for m in range(o_ref.shape[0]):
        o_ref[m:m + 1, :] = jnp.sum(w * s[:, m:m + 1], axis=0, keepdims=True) + b_ref[...]


def _ada(cond_t, w_ada, b_ada, tn=2048):
    n_cond = cond_t.shape[1]
    width = w_ada.shape[2]
    return pl.pallas_call(
        _ada_kernel,
        grid=(DEPTH, width // tn),
        in_specs=[
            pl.BlockSpec((D_MODEL, n_cond), lambda l, j: (0, 0)),
            pl.BlockSpec((None, D_MODEL, tn), lambda l, j: (l, 0, j)),
            pl.BlockSpec((None, 1, tn), lambda l, j: (l, 0, j)),
        ],
        out_specs=pl.BlockSpec((None, n_cond, tn), lambda l, j: (l, 0, j)),
        out_shape=jax.ShapeDtypeStruct((DEPTH, n_cond, width), F32),
        compiler_params=pltpu.CompilerParams(dimension_semantics=("arbitrary", "arbitrary"),
                                             vmem_limit_bytes=VMEM_LIMIT),
        name="ada",
    )(cond_t, w_ada, b_ada.reshape(DEPTH, 1, width))


_PROJ_WEIGHTS = ("g1", "w_a", "w_c", "w_kr", "g_q_a", "wq", "g_kv_a", "wk", "wv", "g_q_head", "g_k_head")


def _proj_kernel(*refs, rope, emit_cache, n_prev, multi_seq):
    (x_ref, mod_ref, g1_ref, wa_ref, wc_ref, wkr_ref, gqa_ref, wq_ref, gkva_ref, wk_ref, wv_ref,
     gqh_ref, gkh_ref, wdft_ref, mq_ref, mk_ref) = refs[:16]
    refs = refs[16:]
    if rope:
        tab_ref, refs = refs[0], refs[1:]
    if n_prev:
        prev_refs, refs = refs[:4], refs[4:]
    qm_ref, km_ref, vm_ref, uc_ref, us_ref, gq_ref, gk_ref, gv_ref = refs[:8]
    cache_refs = refs[8:]

    shift = mod_ref[:, 0:D_MODEL]
    scale = mod_ref[:, D_MODEL:2 * D_MODEL]
    lower = _lower_half(PROJ_SUB)

    for r in range(x_ref.shape[0] // PROJ_SUB):
        rows = slice(r * PROJ_SUB, (r + 1) * PROJ_SUB)
        seq_cols = (slice(None), slice(r * FNET_WIDTH, (r + 1) * FNET_WIDTH)) if multi_seq else (rows, slice(None))
        t_idx = (r,) if multi_seq else (slice(None), rows)

        hb = (_rms(x_ref[rows, :]) * g1_ref[...] * (1.0 + scale) + shift).astype(BF16)
        pa = _dot(hb, wa_ref[...])
        pc = _dot(hb, wc_ref[...])
        kr = _dot(hb, wkr_ref[...])

        if rope:
            cos_m, sa_m, sb_m = (tab_ref[rows, i * LANES:(i + 1) * LANES] for i in range(3))
            cos_g, sa_g, sb_g = (tab_ref[rows, i * LANES:(i + 1) * LANES] for i in range(3, 6))

        gqn = _group_rms(pc[:, _C_GQ:_C_GQ + GQA_WIDTH], mq_ref[...]) * gqh_ref[...]
        tiles = []
        for p in range(HEAD_PAIRS):
            gp = gqn[:, p * LANES:(p + 1) * LANES]
            if rope:
                gp = _rotate(gp, cos_g, sa_g, sb_g, GQA_HEAD_DIM // 2)
            tiles.append(gp * (LOG2_E * GQA_HEAD_DIM ** -0.5))
        gq_ref[rows, 0:LANES] = jnp.where(lower, tiles[0], tiles[1]).astype(BF16)
        gq_ref[rows, LANES:2 * LANES] = jnp.where(
            lower, pltpu.roll(tiles[0], HALF, 1), pltpu.roll(tiles[2], HALF, 1)).astype(BF16)
        gq_ref[rows, 2 * LANES:] = jnp.where(lower, tiles[1], tiles[2]).astype(BF16)
        gkn = _group_rms(pc[:, _C_GK:_C_GK + GQA_KV_WIDTH], mk_ref[...]) * gkh_ref[...]
        gkr = _rotate(gkn, cos_g, sa_g, sb_g, GQA_HEAD_DIM // 2) if rope else gkn
        gv_t = pc[:, _C_GV:_C_GV + GQA_KV_WIDTH].T
        gk_ref[rows, :] = gkr.astype(BF16)
        gv_ref[t_idx] = gv_t.astype(BF16)

        cqn = _rms(pa[:, :MLA_Q_RANK]) * gqa_ref[...]
        q = _dot(cqn.astype(BF16), wq_ref[...])
        for hd in range(MLA_HEADS):
            qh = q[:, hd * LANES:(hd + 1) * LANES]
            if rope:
                qh = _rotate(qh, cos_m, sa_m, sb_m, MLA_ROPE_DIM // 2)
            qm_ref[rows, hd * LANES:(hd + 1) * LANES] = (qh * (LOG2_E * MLA_QK_DIM ** -0.5)).astype(BF16)

        ckvn = _rms(pa[:, MLA_Q_RANK:]) * gkva_ref[...]
        cb = ckvn.astype(BF16)
        kn = _dot(cb, wk_ref[...])
        krr = _rotate(kr, cos_m, sa_m, sb_m, MLA_ROPE_DIM // 2) if rope else kr
        for hd in range(MLA_HEADS):
            km_ref[rows, hd * LANES:(hd + 1) * LANES] = (kn[:, hd * LANES:(hd + 1) * LANES] + krr).astype(BF16)
        vm_ref[t_idx] = _dot(cb, wv_ref[...]).T.astype(BF16)

        ucs = _dot(pc[:, _C_U:_C_U + FNET_WIDTH].astype(BF16), wdft_ref[...])
        uc_ref[seq_cols] = ucs[:, :FNET_WIDTH].astype(BF16)
        us_ref[seq_cols] = ucs[:, FNET_WIDTH:].astype(BF16)

        if emit_cache:
            new = (ckvn, kr.T[:MLA_ROPE_DIM, :], gkn.T, gv_t)
            for i, out_ref in enumerate(cache_refs):
                if n_prev:
                    out_ref[r, :n_prev] = prev_refs[i][r]
                out_ref[r, n_prev] = new[i]


def _proj(x, mod, wts, layer, consts, *, seq, rope, prev_cache=None, emit_cache=False, tm=512):
    n_tok = x.shape[0]
    n_tiles = n_tok // tm
    n_seq = n_tok // seq
    tiles_per_mod = n_tiles // mod.shape[0]
    multi_seq = tm > seq
    if multi_seq:
        assert seq == PROJ_SUB and not rope
        spt = tm // seq
        seq_major = pl.BlockSpec((seq, spt * FNET_WIDTH), lambda i: (0, i))
        transposed = lambda w: pl.BlockSpec((spt, w, seq), lambda i: (i, 0, 0))
    else:
        assert not emit_cache
        tps = seq // tm
        seq_major = pl.BlockSpec((tm, FNET_WIDTH), lambda i: (i % tps, i // tps))
        transposed = lambda w: pl.BlockSpec((None, w, tm), lambda i: (i // tps, 0, i % tps))
    const = lambda arr: pl.BlockSpec(arr.shape, lambda i: (0,) * arr.ndim)
    tok = lambda w: pl.BlockSpec((tm, w), lambda i: (i, 0))
    in_specs = [tok(D_MODEL), pl.BlockSpec((None, 1, 6 * D_MODEL), lambda i: (i // tiles_per_mod, 0, 0))]
    args = [x, mod]
    for name in _PROJ_WEIGHTS:
        in_specs.append(_layer_spec(wts[name], layer))
        args.append(wts[name])
    for name in ("wdft", "mean_q", "mean_k"):
        in_specs.append(const(consts[name]))
        args.append(consts[name])
    if rope:
        in_specs.append(pl.BlockSpec((tm, 6 * LANES), lambda i: (i % tps, 0)))
        args.append(consts["rope"])
    out_specs = [tok(MLA_PAD_WIDTH), tok(MLA_PAD_WIDTH), transposed(MIX_TILE), seq_major, seq_major,
                 tok(GQA_WIDTH), tok(GQA_KV_WIDTH), transposed(GQA_KV_WIDTH)]
    out_shape = [jax.ShapeDtypeStruct((n_tok, MLA_PAD_WIDTH), BF16),
                 jax.ShapeDtypeStruct((n_tok, MLA_PAD_WIDTH), BF16),
                 jax.ShapeDtypeStruct((n_seq, MIX_TILE, seq), BF16),
                 jax.ShapeDtypeStruct((seq, n_seq * FNET_WIDTH), BF16),
                 jax.ShapeDtypeStruct((seq, n_seq * FNET_WIDTH), BF16),
                 jax.ShapeDtypeStruct((n_tok, GQA_WIDTH), BF16),
                 jax.ShapeDtypeStruct((n_tok, GQA_KV_WIDTH), BF16),
                 jax.ShapeDtypeStruct((n_seq, GQA_KV_WIDTH, seq), BF16)]
    n_prev = 0
    if emit_cache:
        n_prev = prev_cache[0].shape[1] if prev_cache is not None else 0
        layers = lambda n, tail: pl.BlockSpec((spt, n) + tail, lambda i: (i, 0, 0, 0))
        tails = ((seq, MLA_KV_RANK), (MLA_ROPE_DIM, seq), (GQA_KV_WIDTH, seq), (GQA_KV_WIDTH, seq))
        if n_prev:
            in_specs += [layers(n_prev, t) for t in tails]
            args += list(prev_cache)
        out_specs += [layers(n_prev + 1, t) for t in tails]
        out_shape += [jax.ShapeDtypeStruct((n_seq, n_prev + 1) + t, F32) for t in tails]
    return pl.pallas_call(
        functools.partial(_proj_kernel, rope=rope, emit_cache=emit_cache, n_prev=n_prev, multi_seq=multi_seq),
        grid=(n_tiles,),
        in_specs=in_specs,
        out_specs=out_specs,
        out_shape=out_shape,
        compiler_params=pltpu.CompilerParams(dimension_semantics=("arbitrary",), vmem_limit_bytes=VMEM_LIMIT),
        name="proj_rope" if rope else "proj_ctx",
    )(*args)


def _ctx_kernel(ckv_ref, kr_ref, gk_ref, gv_ref, wk_ref, wv_ref, place_ref, km_ref, vm_ref, gko_ref, gvo_ref):
    cb = ckv_ref[...].astype(BF16)
    kn = _dot(cb, wk_ref[...])
    kr = _dot(kr_ref[...].astype(BF16), place_ref[...])
    for hd in range(MLA_HEADS):
        km_ref[:, hd * LANES:(hd + 1) * LANES] = (kn[:, hd * LANES:(hd + 1) * LANES] + kr).astype(BF16)
    vm_ref[...] = _dot(cb, wv_ref[...]).T.astype(BF16)
    gko_ref[...] = gk_ref[...].astype(BF16)
    gvo_ref[...] = gv_ref[...].T.astype(BF16)


def _ctx_prep(layer, cache_ckv, cache_krope, cache_gk, cache_gv, wts, consts):
    nb, _, past, _ = cache_ckv.shape
    cache = lambda w: pl.BlockSpec((None, None, past, w), lambda b: (b, layer, 0, 0))
    out = lambda w: pl.BlockSpec((None, past, w), lambda b: (b, 0, 0))
    out_t = lambda w: pl.BlockSpec((None, w, past), lambda b: (b, 0, 0))
    return pl.pallas_call(
        _ctx_kernel,
        grid=(nb,),
        in_specs=[cache(MLA_KV_RANK), cache(MLA_ROPE_DIM), cache(GQA_KV_WIDTH), cache(GQA_KV_WIDTH),
                  _layer_spec(wts["wk"], layer), _layer_spec(wts["wv"], layer),
                  pl.BlockSpec((MLA_ROPE_DIM, LANES), lambda b: (0, 0))],
        out_specs=[out(MLA_PAD_WIDTH), out_t(MIX_TILE), out(GQA_KV_WIDTH), out_t(GQA_KV_WIDTH)],
        out_shape=[jax.ShapeDtypeStruct((nb, past, MLA_PAD_WIDTH), BF16),
                   jax.ShapeDtypeStruct((nb, MIX_TILE, past), BF16),
                   jax.ShapeDtypeStruct((nb, past, GQA_KV_WIDTH), BF16),
                   jax.ShapeDtypeStruct((nb, GQA_KV_WIDTH, past), BF16)],
        compiler_params=pltpu.CompilerParams(dimension_semantics=("arbitrary",), vmem_limit_bytes=VMEM_LIMIT),
        name="ctx_prep",
    )(cache_ckv, cache_krope, cache_gk.reshape(nb, DEPTH, past, GQA_KV_WIDTH),
      cache_gv.reshape(nb, DEPTH, past, GQA_KV_WIDTH), wts["wk"], wts["wv"], consts["place"])


def _attn_kernel(*refs, n_seg, mla, ahead):
    q_ref, o_ref = refs[0], refs[-1]
    k_refs = refs[1:1 + n_seg]
    vt_refs = refs[1 + n_seg:1 + 2 * n_seg]
    tq = q_ref.shape[1]
    lower = _lower_half(tq)
    top = lax.broadcasted_iota(jnp.int32, (LANES, tq), 0) < HALF

    def scores_t(b, p, hd):
        if mla:
            cols = slice((2 * p + hd) * LANES, (2 * p + hd + 1) * LANES)
            qh = q_ref[b, :, cols]
            keys = [k_ref[b, :, cols] for k_ref in k_refs]
        else:
            q = q_ref[b, :, p * LANES:(p + 1) * LANES]
            qh = jnp.where(lower if hd == 0 else ~lower, q, jnp.zeros_like(q))
            keys = [k_ref[b] for k_ref in k_refs]
        return [lax.dot_general(k, qh, (((1,), (1,)), ((), ())), preferred_element_type=F32) for k in keys]

    def softmax_t(ss):
        m = functools.reduce(jnp.maximum, [jnp.max(s, axis=0, keepdims=True) for s in ss])
        es = [jnp.exp2(s - m) for s in ss]
        denom = functools.reduce(jnp.add, [jnp.sum(e, axis=0, keepdims=True) for e in es])
        return [e.astype(BF16) for e in es], denom

    def weighted_values_t(b, p, es, denom):
        rows = slice(p * LANES, (p + 1) * LANES) if mla else slice(None)
        acc = functools.reduce(jnp.add, [_dot(vt_ref[b, rows, :], e) for e, vt_ref in zip(es, vt_refs)])
        return acc * (1.0 / denom)

    work =[(b, p, hd) for b in range(q_ref.shape[0]) for p in range(HEAD_PAIRS) for hd in range(2)]
    pending = [scores_t(*w) for w in work[:ahead]]
    outs = {}
    for i, (b, p, hd) in enumerate(work):
        es, denom = softmax_t(pending.pop(0))
        if i + ahead < len(work):
            pending.append(scores_t(*work[i + ahead]))
        outs[b, p, hd] = weighted_values_t(b, p, es, denom)
    for b in range(q_ref.shape[0]):
        tiles = [jnp.where(top, outs[b, p, 0], outs[b, p, 1]).T for p in range(HEAD_PAIRS)]
        if not mla:
            t1 = pltpu.roll(tiles[1], HALF, 1)
            tiles = [jnp.where(lower, tiles[0], t1), jnp.where(lower, tiles[2], tiles[0]),
                     jnp.where(lower, t1, tiles[2])]
        for p in range(HEAD_PAIRS):
            o_ref[b, :, p * LANES:(p + 1) * LANES] = tiles[p].astype(o_ref.dtype)


def _attention(q, kv_segments, *, mla, tq, bb=1, ahead=2):
    n_batch, sq = q.shape[:2]
    n_seg = len(kv_segments)
    whole = lambda a: pl.BlockSpec((bb,) + a.shape[1:], lambda b, i: (b, 0, 0))
    in_specs = [pl.BlockSpec((bb, tq, q.shape[2]), lambda b, i: (b, i, 0))]
    in_specs += [whole(k) for k, _ in kv_segments] + [whole(v) for _, v in kv_segments]
    return pl.pallas_call(
        functools.partial(_attn_kernel, n_seg=n_seg, mla=mla, ahead=ahead),
        grid=(n_batch // bb, sq // tq),
        in_specs=in_specs,
        out_specs=pl.BlockSpec((bb, tq, MIX_TILE), lambda b, i: (b, i, 0)),
        out_shape=jax.ShapeDtypeStruct((n_batch, sq, MIX_TILE), BF16),
        compiler_params=pltpu.CompilerParams(dimension_semantics=("arbitrary",) * 2, vmem_limit_bytes=VMEM_LIMIT),
        name=f"attn_{'mla' if mla else 'gqa'}_s{n_seg}",
    )(q, *[k for k, _ in kv_segments], *[v for _, v in kv_segments])


def _fnet_kernel(ca_ref, sa_ref, cb_ref, sb_ref, uc_ref, us_ref, o_ref, tc_ref, ts_ref, *, scale):
    @pl.when(pl.program_id(1) == 0)
    def _build_twiddles():
        cb, sb = cb_ref[...], sb_ref[...]
        for a in range(ca_ref.shape[0]):
            ca, sa = ca_ref[a:a + 1, :], sa_ref[a:a + 1, :]
            rows = slice(a * DFT_SPLIT, (a + 1) * DFT_SPLIT)
            tc_ref[rows, :] = (ca * cb - sa * sb).astype(BF16)
            ts_ref[rows, :] = (sa * cb + ca * sb).astype(BF16)

    acc = _dot(tc_ref[...], uc_ref[...]) - _dot(ts_ref[...], us_ref[...])
    o_ref[...] = (acc * scale).astype(o_ref.dtype)


def _fnet(tables, uc, us, tm, tn):
    seq, width = uc.shape
    scale = float((seq * FNET_GROUP_DIM) ** -0.5)
    n_a = tm // DFT_SPLIT
    part_a = pl.BlockSpec((n_a, seq), lambda i, j: (i, 0))
    part_b = pl.BlockSpec((DFT_SPLIT, seq), lambda i, j: (0, 0))
    data = pl.BlockSpec((seq, tn), lambda i, j: (0, j))
    return pl.pallas_call(
        functools.partial(_fnet_kernel, scale=scale),
        grid=(seq // tm, width // tn),
        in_specs=[part_a, part_a, part_b, part_b, data, data],
        out_specs=pl.BlockSpec((tm, tn), lambda i, j: (i, j)),
        out_shape=jax.ShapeDtypeStruct((seq, width), BF16),
        scratch_shapes=[pltpu.VMEM((tm, seq), BF16), pltpu.VMEM((tm, seq), BF16)],
        compiler_params=pltpu.CompilerParams(dimension_semantics=("arbitrary", "arbitrary"),
                                             vmem_limit_bytes=VMEM_LIMIT),
        name="fnet",
    )(*tables, uc, us)


def _post_kernel(x_ref, mla_ref, fn_ref, gqa_ref, mod_ref, g2_ref, wo_ref, wg_ref, wu_ref, wd_ref, gf_ref,
                 o_ref, *, final):
    fn = fn_ref[...]
    if fn.shape[1] > FNET_WIDTH:
        fn = jnp.concatenate([fn[:, j:j + FNET_WIDTH] for j in range(0, fn.shape[1], FNET_WIDTH)], axis=0)
    mix = jnp.concatenate([mla_ref[...], fn, gqa_ref[...]], axis=-1)
    gate1 = mod_ref[:, 2 * D_MODEL:3 * D_MODEL]
    shift2 = mod_ref[:, 3 * D_MODEL:4 * D_MODEL]
    scale2 = mod_ref[:, 4 * D_MODEL:5 * D_MODEL]
    gate2 = mod_ref[:, 5 * D_MODEL:6 * D_MODEL]
    x = x_ref[...] + gate1 * _dot(mix, wo_ref[...])
    h = (_rms(x) * g2_ref[...] * (1.0 + scale2) + shift2).astype(BF16)
    g = _dot(h, wg_ref[...])
    u = _dot(h, wu_ref[...])
    a = (g * jax.nn.sigmoid(g) * u).astype(BF16)
    x = x + gate2 * _dot(a, wd_ref[...])
    if final:
        x = _rms(x) * gf_ref[...]
    o_ref[...] = x


def _post(x, mla_o, fnet_o, gqa_o, mod, wts, layer, g_final, *, seq, final, tm):
    n_tok = x.shape[0]
    n_tiles = n_tok // tm
    tiles_per_mod = n_tiles // mod.shape[0]
    resident = lambda arr: pl.BlockSpec((None,) + arr.shape[1:], lambda i: (layer, 0, 0), pipeline_mode=pl.Buffered(1))
    tok = lambda w: pl.BlockSpec((tm, w), lambda i: (i, 0))
    if tm <= seq:
        tps = seq // tm
        fnet_spec = pl.BlockSpec((tm, FNET_WIDTH), lambda i: (i % tps, i // tps))
    else:
        fnet_spec = pl.BlockSpec((seq, (tm // seq) * FNET_WIDTH), lambda i: (0, i))
    names = ("g2", "w_out", "w_gate", "w_up", "w_down")
    return pl.pallas_call(
        functools.partial(_post_kernel, final=final),
        grid=(n_tiles,),
        in_specs=[tok(D_MODEL), tok(MIX_TILE), fnet_spec, tok(MIX_TILE),
                  pl.BlockSpec((None, 1, 6 * D_MODEL), lambda i: (i // tiles_per_mod, 0, 0))]
                 + [resident(wts[n]) for n in names]
                 + [pl.BlockSpec((1, D_MODEL), lambda i: (0, 0))],
        out_specs=tok(D_MODEL),
        out_shape=jax.ShapeDtypeStruct((n_tok, D_MODEL), F32),
        compiler_params=pltpu.CompilerParams(dimension_semantics=("arbitrary",), vmem_limit_bytes=VMEM_LIMIT),
        name="post_final" if final else "post",
    )(x, mla_o, fnet_o, gqa_o, mod, *[wts[n] for n in names], g_final)


def _prep_weights(g_norm1, g_norm2, w_in, g_q_a, w_q_up, g_kv_a, w_kv_up, g_q_head, g_k_head,
                  w_out, w_ffn_gate, w_ffn_up, w_ffn_down):
    row = lambda g: g.reshape(DEPTH, 1, -1)
    w_kr = jnp.pad(w_in[:, :, _IN_A:_IN_C0], ((0, 0), (0, 0), (0, LANES - MLA_ROPE_DIM)))
    wq = w_q_up.reshape(DEPTH, MLA_Q_RANK, MLA_HEADS, MLA_QK_DIM)
    wq = jnp.concatenate([wq[..., MLA_NOPE_DIM:], wq[..., :MLA_NOPE_DIM]], axis=-1)
    wq = jnp.pad(wq, ((0, 0), (0, 0), (0, 0), (0, LANES - MLA_QK_DIM)))
    wkv = w_kv_up.reshape(DEPTH, MLA_KV_RANK, MLA_HEADS, MLA_NOPE_DIM + MLA_V_DIM)
    wk = jnp.pad(wkv[..., :MLA_NOPE_DIM], ((0, 0), (0, 0), (0, 0), (MLA_ROPE_DIM, LANES - MLA_QK_DIM)))
    wv = wkv[..., MLA_NOPE_DIM:]
    return {
        "g1": row(g_norm1), "g2": row(g_norm2), "g_q_a": row(g_q_a), "g_kv_a": row(g_kv_a),
        "g_q_head": row(jnp.tile(g_q_head, (1, GQA_HEADS))), "g_k_head": row(jnp.tile(g_k_head, (1, GQA_KV_HEADS))),
        "w_a": w_in[:, :, :_IN_A].astype(BF16), "w_c": w_in[:, :, _IN_C0:].astype(BF16), "w_kr": w_kr.astype(BF16),
        "wq": wq.reshape(DEPTH, MLA_Q_RANK, MLA_PAD_WIDTH).astype(BF16),
        "wk": wk.reshape(DEPTH, MLA_KV_RANK, MLA_PAD_WIDTH).astype(BF16),
        "wv": wv.reshape(DEPTH, MLA_KV_RANK, MIX_TILE).astype(BF16),
        "w_out": w_out.astype(BF16), "w_gate": w_ffn_gate.astype(BF16),
        "w_up": w_ffn_up.astype(BF16), "w_down": w_ffn_down.astype(BF16),
    }


def kernel(x_prompt, x_sample, cache_mla_ckv, cache_mla_krope, cache_gqa_k, cache_gqa_v, c, c_ctx, w_ada, b_ada,
           g_norm1, g_norm2, w_in, g_q_a, w_q_up, g_kv_a, w_kv_up, g_q_head, g_k_head, w_out,
           w_ffn_gate, w_ffn_up, w_ffn_down, g_final):
    n_pb, p_seq, _ = x_prompt.shape
    n_sb, s_seq, _ = x_sample.shape
    consts = {"wdft": _channel_dft(), "mean_q": _group_mean_matrix(GQA_HEADS),
              "mean_k": _group_mean_matrix(GQA_KV_HEADS), "place": _rope_place_matrix(),
              "rope": _rope_tables(s_seq)}
    dft_p = _seq_dft_tables(p_seq)
    dft_s = _seq_dft_tables(s_seq)
    gf = g_final.reshape(1, D_MODEL)
    wts = _prep_weights(g_norm1, g_norm2, w_in, g_q_a, w_q_up, g_kv_a, w_kv_up, g_q_head, g_k_head,
                        w_out, w_ffn_gate, w_ffn_up, w_ffn_down)

    cond_t = jnp.concatenate([c_ctx[None, :], c], axis=0).T
    mod = _ada(cond_t, w_ada, b_ada)

    xp = x_prompt.reshape(n_pb * p_seq, D_MODEL)
    xs = x_sample.reshape(n_sb * s_seq, D_MODEL)
    b3 = lambda a, nb: a.reshape(nb, a.shape[0] // nb, a.shape[1])
    caches = None
    for l in range(DEPTH):
        final = l == DEPTH - 1
        mod_p = mod[l, 0:1].reshape(1, 1, 6 * D_MODEL)
        mod_s = mod[l, 1:].reshape(n_sb, 1, 6 * D_MODEL)

        qm, km, vm, uc, us, gq, gk, gv, *caches = _proj(
            xp, mod_p, wts, l, consts, seq=p_seq, rope=False, emit_cache=True, prev_cache=caches)
        mla_o = _attention(b3(qm, n_pb), [(b3(km, n_pb), vm)], mla=True, tq=p_seq, bb=4, ahead=8)
        gqa_o = _attention(b3(gq, n_pb), [(b3(gk, n_pb), gv)], mla=False, tq=p_seq, bb=4, ahead=8)
        fn_o = _fnet(dft_p, uc, us, tm=p_seq, tn=1024)
        xp = _post(xp, mla_o.reshape(-1, MIX_TILE), fn_o, gqa_o.reshape(-1, MIX_TILE), mod_p, wts, l, gf,
                   seq=p_seq, final=final, tm=POST_TILE)

        km_c, vm_c, gk_c, gv_c = _ctx_prep(l, cache_mla_ckv, cache_mla_krope, cache_gqa_k, cache_gqa_v, wts, consts)
        qm, km, vm, uc, us, gq, gk, gv = _proj(xs, mod_s, wts, l, consts, seq=s_seq, rope=True)
        mla_o = _attention(b3(qm, n_sb), [(km_c, vm_c), (b3(km, n_sb), vm)], mla=True, tq=512)
        gqa_o = _attention(b3(gq, n_sb), [(gk_c, gv_c), (b3(gk, n_sb), gv)], mla=False, tq=512)
        fn_o = _fnet(dft_s, uc, us, tm=512, tn=n_sb * FNET_WIDTH)
        xs = _post(xs, mla_o.reshape(-1, MIX_TILE), fn_o, gqa_o.reshape(-1, MIX_TILE), mod_s, wts, l, gf,
                   seq=s_seq, final=final, tm=POST_TILE)

    ckv_new, krope_t, gk_t, gv_t = caches
    heads = lambda a: jnp.swapaxes(a, 2, 3).reshape(n_pb, DEPTH, p_seq, GQA_KV_HEADS, GQA_HEAD_DIM)
    return (xp.reshape(n_pb, p_seq, D_MODEL), xs.reshape(n_sb, s_seq, D_MODEL),
            ckv_new, jnp.swapaxes(krope_t, 2, 3), heads(gk_t), heads(gv_t))
```

```python
import functools

import numpy as np
import jax
import jax.numpy as jnp
from jax import lax
from jax.experimental import pallas as pl
from jax.experimental.pallas import tpu as pltpu

D_MODEL = 1024
DEPTH = 2
GRID_W = 64
MLA_HEADS = 6
MLA_Q_RANK = 384
MLA_KV_RANK = 256
MLA_NOPE_DIM = 64
MLA_ROPE_DIM = 32
MLA_V_DIM = 64
MLA_QK_DIM = MLA_NOPE_DIM + MLA_ROPE_DIM
FNET_GROUPS = 4
FNET_GROUP_DIM = 64
FNET_WIDTH = FNET_GROUPS * FNET_GROUP_DIM
GQA_HEADS = 6
GQA_KV_HEADS = 2
GQA_HEAD_DIM = 64
GQA_GROUP = GQA_HEADS // GQA_KV_HEADS
GQA_WIDTH = GQA_HEADS * GQA_HEAD_DIM
GQA_KV_WIDTH = GQA_KV_HEADS * GQA_HEAD_DIM
D_FF = 2816
ROPE_THETA = 10000.0
EPS = 1e-6
LOG2_E = 1.4426950408889634

LANES = 128
HALF = LANES // 2
HEAD_PAIRS = MLA_HEADS // 2
MLA_PAD_WIDTH = MLA_HEADS * LANES
MIX_TILE = HEAD_PAIRS * LANES
VMEM_LIMIT = 52 * 1024 * 1024
POST_TILE = 512
PROJ_SUB = 256

_IN_A = MLA_Q_RANK + MLA_KV_RANK
_IN_C0 = _IN_A + MLA_ROPE_DIM
_IN_C = FNET_WIDTH + GQA_WIDTH + 2 * GQA_KV_WIDTH
_C_U, _C_GQ, _C_GK, _C_GV = 0, FNET_WIDTH, FNET_WIDTH + GQA_WIDTH, FNET_WIDTH + GQA_WIDTH + GQA_KV_WIDTH

BF16 = jnp.bfloat16
F32 = jnp.float32


def _rope_tables(n_tokens):
    t = np.arange(n_tokens)
    row = (t // GRID_W).astype(np.float64)
    col = (t % GRID_W).astype(np.float64)

    def angles(rot_dim):
        n_axis = rot_dim // 4
        inv = ROPE_THETA ** (-np.arange(n_axis, dtype=np.float64) / n_axis)
        return np.concatenate([row[:, None] * inv, col[:, None] * inv], axis=-1)

    def tables(ang, lane_to_pair, is_first, is_second):
        cos = np.where((is_first | is_second)[None, :], np.cos(ang)[:, lane_to_pair], 1.0)
        sin = np.sin(ang)[:, lane_to_pair]
        sin_a = np.where(is_second[None, :], sin, 0.0)
        sin_b = np.where(is_first[None, :], -sin, 0.0)
        return [cos, sin_a, sin_b]

    lane = np.arange(LANES)
    half_m = MLA_ROPE_DIM // 2
    in_rope = lane < MLA_ROPE_DIM
    first_m = lane < half_m
    second_m = in_rope & ~first_m
    pair_m = lane % half_m
    half_g = GQA_HEAD_DIM // 2
    first_g = (lane % GQA_HEAD_DIM) < half_g
    pair_g = lane % half_g
    tabs = (tables(angles(MLA_ROPE_DIM), pair_m, first_m, second_m)
            + tables(angles(GQA_HEAD_DIM), pair_g, first_g, ~first_g))
    return jnp.asarray(np.concatenate(tabs, axis=-1), dtype=F32)


def _channel_dft():
    c = np.arange(FNET_GROUP_DIM)
    ang = 2.0 * np.pi * np.outer(c, c) / FNET_GROUP_DIM
    eye = np.eye(FNET_GROUPS)
    table = np.concatenate([np.kron(eye, np.cos(ang)), np.kron(eye, np.sin(ang))], axis=1)
    return jnp.asarray(table, dtype=F32).astype(BF16)


def _group_mean_matrix(n_heads):
    return jnp.asarray(np.kron(np.eye(n_heads), np.full((GQA_HEAD_DIM, GQA_HEAD_DIM), 1.0 / GQA_HEAD_DIM)), dtype=BF16)


def _rope_place_matrix():
    return jnp.asarray(np.eye(MLA_ROPE_DIM, LANES), dtype=BF16)


DFT_SPLIT = 32


def _seq_dft_tables(seq):
    s = np.arange(seq)
    n_a = seq // DFT_SPLIT
    ang_a = 2.0 * np.pi * ((np.arange(n_a)[:, None] * s[None, :]) % n_a) / n_a
    ang_b = 2.0 * np.pi * ((np.arange(DFT_SPLIT)[:, None] * s[None, :]) % seq) / seq
    return tuple(jnp.asarray(t, dtype=F32) for t in (np.cos(ang_a), np.sin(ang_a), np.cos(ang_b), np.sin(ang_b)))


def _rms(x):
    return x * lax.rsqrt(jnp.mean(x * x, axis=-1, keepdims=True) + EPS)


def _dot(a, b):
    return jnp.dot(a, b, preferred_element_type=F32)


def _group_rms(x, mean_mat):
    ms = _dot((x * x).astype(BF16), mean_mat)
    return x * lax.rsqrt(ms + EPS)


def _rotate(x, cos, sin_a, sin_b, half):
    return x * cos + pltpu.roll(x, half, 1) * sin_a + pltpu.roll(x, LANES - half, 1) * sin_b


def _lower_half(rows):
    return lax.broadcasted_iota(jnp.int32, (rows, LANES), 1) < HALF


def _layer_spec(arr, layer):
    return pl.BlockSpec((None,) + arr.shape[1:], lambda i: (layer, 0, 0))


def _ada_kernel(ct_ref, w_ref, b_ref, o_ref):
    s = ct_ref[...]
    s = s * jax.nn.sigmoid(s)
    w = w_ref[...]
    for m in range(o_ref.shape[0]):
        o_ref[m:m + 1, :] = jnp.sum(w * s[:, m:m + 1], axis=0, keepdims=True) + b_ref[...]


def _ada(cond_t, w_ada, b_ada, tn=2048):
    n_cond = cond_t.shape[1]
    width = w_ada.shape[2]
    return pl.pallas_call(
        _ada_kernel,
        grid=(DEPTH, width // tn),
        in_specs=[
            pl.BlockSpec((D_MODEL, n_cond), lambda l, j: (0, 0)),
            pl.BlockSpec((None, D_MODEL, tn), lambda l, j: (l, 0, j)),
            pl.BlockSpec((None, 1, tn), lambda l, j: (l, 0, j)),
        ],
        out_specs=pl.BlockSpec((None, n_cond, tn), lambda l, j: (l, 0, j)),
        out_shape=jax.ShapeDtypeStruct((DEPTH, n_cond, width), F32),
        compiler_params=pltpu.CompilerParams(dimension_semantics=("arbitrary", "arbitrary"),
                                             vmem_limit_bytes=VMEM_LIMIT),
        name="ada",
    )(cond_t, w_ada, b_ada.reshape(DEPTH, 1, width))


_PROJ_WEIGHTS = ("g1", "w_a", "w_c", "w_kr", "g_q_a", "wq", "g_kv_a", "wk", "wv", "g_q_head", "g_k_head")


def _proj_kernel(*refs, rope, emit_cache, n_prev, multi_seq):
    (x_ref, mod_ref, g1_ref, wa_ref, wc_ref, wkr_ref, gqa_ref, wq_ref, gkva_ref, wk_ref, wv_ref,
     gqh_ref, gkh_ref, wdft_ref, mq_ref, mk_ref) = refs[:16]
    refs = refs[16:]
    if rope:
        tab_ref, refs = refs[0], refs[1:]
    if n_prev:
        prev_refs, refs = refs[:4], refs[4:]
    qm_ref, km_ref, vm_ref, uc_ref, us_ref, gq_ref, gk_ref, gv_ref = refs[:8]
    cache_refs = refs[8:]

    shift = mod_ref[:, 0:D_MODEL]
    scale = mod_ref[:, D_MODEL:2 * D_MODEL]
    lower = _lower_half(PROJ_SUB)

    for r in range(x_ref.shape[0] // PROJ_SUB):
        rows = slice(r * PROJ_SUB, (r + 1) * PROJ_SUB)
        seq_cols = (slice(None), slice(r * FNET_WIDTH, (r + 1) * FNET_WIDTH)) if multi_seq else (rows, slice(None))
        t_idx = (r,) if multi_seq else (slice(None), rows)

        hb = (_rms(x_ref[rows, :]) * g1_ref[...] * (1.0 + scale) + shift).astype(BF16)
        pa = _dot(hb, wa_ref[...])
        pc = _dot(hb, wc_ref[...])
        kr = _dot(hb, wkr_ref[...])

        if rope:
            cos_m, sa_m, sb_m = (tab_ref[rows, i * LANES:(i + 1) * LANES] for i in range(3))
            cos_g, sa_g, sb_g = (tab_ref[rows, i * LANES:(i + 1) * LANES] for i in range(3, 6))

        gqn = _group_rms(pc[:, _C_GQ:_C_GQ + GQA_WIDTH], mq_ref[...]) * gqh_ref[...]
        tiles = []
        for p in range(HEAD_PAIRS):
            gp = gqn[:, p * LANES:(p + 1) * LANES]
            if rope:
                gp = _rotate(gp, cos_g, sa_g, sb_g, GQA_HEAD_DIM // 2)
            tiles.append(gp * (LOG2_E * GQA_HEAD_DIM ** -0.5))
        gq_ref[rows, 0:LANES] = jnp.where(lower, tiles[0], tiles[1]).astype(BF16)
        gq_ref[rows, LANES:2 * LANES] = jnp.where(
            lower, pltpu.roll(tiles[0], HALF, 1), pltpu.roll(tiles[2], HALF, 1)).astype(BF16)
        gq_ref[rows, 2 * LANES:] = jnp.where(lower, tiles[1], tiles[2]).astype(BF16)
        gkn = _group_rms(pc[:, _C_GK:_C_GK + GQA_KV_WIDTH], mk_ref[...]) * gkh_ref[...]
        gkr = _rotate(gkn, cos_g, sa_g, sb_g, GQA_HEAD_DIM // 2) if rope else gkn
        gv_t = pc[:, _C_GV:_C_GV + GQA_KV_WIDTH].T
        gk_ref[rows, :] = gkr.astype(BF16)
        gv_ref[t_idx] = gv_t.astype(BF16)

        cqn = _rms(pa[:, :MLA_Q_RANK]) * gqa_ref[...]
        q = _dot(cqn.astype(BF16), wq_ref[...])
        for hd in range(MLA_HEADS):
            qh = q[:, hd * LANES:(hd + 1) * LANES]
            if rope:
                qh = _rotate(qh, cos_m, sa_m, sb_m, MLA_ROPE_DIM // 2)
            qm_ref[rows, hd * LANES:(hd + 1) * LANES] = (qh * (LOG2_E * MLA_QK_DIM ** -0.5)).astype(BF16)

        ckvn = _rms(pa[:, MLA_Q_RANK:]) * gkva_ref[...]
        cb = ckvn.astype(BF16)
        kn = _dot(cb, wk_ref[...])
        krr = _rotate(kr, cos_m, sa_m, sb_m, MLA_ROPE_DIM // 2) if rope else kr
        for hd in range(MLA_HEADS):
            km_ref[rows, hd * LANES:(hd + 1) * LANES] = (kn[:, hd * LANES:(hd + 1) * LANES] + krr).astype(BF16)
        vm_ref[t_idx] = _dot(cb, wv_ref[...]).T.astype(BF16)

        ucs = _dot(pc[:, _C_U:_C_U + FNET_WIDTH].astype(BF16), wdft_ref[...])
        uc_ref[seq_cols] = ucs[:, :FNET_WIDTH].astype(BF16)
        us_ref[seq_cols] = ucs[:, FNET_WIDTH:].astype(BF16)

        if emit_cache:
            new = (ckvn, kr.T[:MLA_ROPE_DIM, :], gkn.T, gv_t)
            for i, out_ref in enumerate(cache_refs):
                if n_prev:
                    out_ref[r, :n_prev] = prev_refs[i][r]
                out_ref[r, n_prev] = new[i]


def _proj(x, mod, wts, layer, consts, *, seq, rope, prev_cache=None, emit_cache=False, tm=512):
    n_tok = x.shape[0]
    n_tiles = n_tok // tm
    n_seq = n_tok // seq
    tiles_per_mod = n_tiles // mod.shape[0]
    multi_seq = tm > seq
    if multi_seq:
        assert seq == PROJ_SUB and not rope
        spt = tm // seq
        seq_major = pl.BlockSpec((seq, spt * FNET_WIDTH), lambda i: (0, i))
        transposed = lambda w: pl.BlockSpec((spt, w, seq), lambda i: (i, 0, 0))
    else:
        assert not emit_cache
        tps = seq // tm
        seq_major = pl.BlockSpec((tm, FNET_WIDTH), lambda i: (i % tps, i // tps))
        transposed = lambda w: pl.BlockSpec((None, w, tm), lambda i: (i // tps, 0, i % tps))
    const = lambda arr: pl.BlockSpec(arr.shape, lambda i: (0,) * arr.ndim)
    tok = lambda w: pl.BlockSpec((tm, w), lambda i: (i, 0))
    in_specs = [tok(D_MODEL), pl.BlockSpec((None, 1, 6 * D_MODEL), lambda i: (i // tiles_per_mod, 0, 0))]
    args = [x, mod]
    for name in _PROJ_WEIGHTS:
        in_specs.append(_layer_spec(wts[name], layer))
        args.append(wts[name])
    for name in ("wdft", "mean_q", "mean_k"):
        in_specs.append(const(consts[name]))
        args.append(consts[name])
    if rope:
        in_specs.append(pl.BlockSpec((tm, 6 * LANES), lambda i: (i % tps, 0)))
        args.append(consts["rope"])
    out_specs = [tok(MLA_PAD_WIDTH), tok(MLA_PAD_WIDTH), transposed(MIX_TILE), seq_major, seq_major,
                 tok(GQA_WIDTH), tok(GQA_KV_WIDTH), transposed(GQA_KV_WIDTH)]
    out_shape = [jax.ShapeDtypeStruct((n_tok, MLA_PAD_WIDTH), BF16),
                 jax.ShapeDtypeStruct((n_tok, MLA_PAD_WIDTH), BF16),
                 jax.ShapeDtypeStruct((n_seq, MIX_TILE, seq), BF16),
                 jax.ShapeDtypeStruct((seq, n_seq * FNET_WIDTH), BF16),
                 jax.ShapeDtypeStruct((seq, n_seq * FNET_WIDTH), BF16),
                 jax.ShapeDtypeStruct((n_tok, GQA_WIDTH), BF16),
                 jax.ShapeDtypeStruct((n_tok, GQA_KV_WIDTH), BF16),
                 jax.ShapeDtypeStruct((n_seq, GQA_KV_WIDTH, seq), BF16)]
    n_prev = 0
    if emit_cache:
        n_prev = prev_cache[0].shape[1] if prev_cache is not None else 0
        layers = lambda n, tail: pl.BlockSpec((spt, n) + tail, lambda i: (i, 0, 0, 0))
        tails = ((seq, MLA_KV_RANK), (MLA_ROPE_DIM, seq), (GQA_KV_WIDTH, seq), (GQA_KV_WIDTH, seq))
        if n_prev:
            in_specs += [layers(n_prev, t) for t in tails]
            args += list(prev_cache)
        out_specs += [layers(n_prev + 1, t) for t in tails]
        out_shape += [jax.ShapeDtypeStruct((n_seq, n_prev + 1) + t, F32) for t in tails]
    return pl.pallas_call(
        functools.partial(_proj_kernel, rope=rope, emit_cache=emit_cache, n_prev=n_prev, multi_seq=multi_seq),
        grid=(n_tiles,),
        in_specs=in_specs,
        out_specs=out_specs,
        out_shape=out_shape,
        compiler_params=pltpu.CompilerParams(dimension_semantics=("arbitrary",), vmem_limit_bytes=VMEM_LIMIT),
        name="proj_rope" if rope else "proj_ctx",
    )(*args)


def _ctx_kernel(ckv_ref, kr_ref, gk_ref, gv_ref, wk_ref, wv_ref, place_ref, km_ref, vm_ref, gko_ref, gvo_ref):
    cb = ckv_ref[...].astype(BF16)
    kn = _dot(cb, wk_ref[...])
    kr = _dot(kr_ref[...].astype(BF16), place_ref[...])
    for hd in range(MLA_HEADS):
        km_ref[:, hd * LANES:(hd + 1) * LANES] = (kn[:, hd * LANES:(hd + 1) * LANES] + kr).astype(BF16)
    vm_ref[...] = _dot(cb, wv_ref[...]).T.astype(BF16)
    gko_ref[...] = gk_ref[...].astype(BF16)
    gvo_ref[...] = gv_ref[...].T.astype(BF16)


def _ctx_prep(layer, cache_ckv, cache_krope, cache_gk, cache_gv, wts, consts):
    nb, _, past, _ = cache_ckv.shape
    cache = lambda w: pl.BlockSpec((None, None, past, w), lambda b: (b, layer, 0, 0))
    out = lambda w: pl.BlockSpec((None, past, w), lambda b: (b, 0, 0))
    out_t = lambda w: pl.BlockSpec((None, w, past), lambda b: (b, 0, 0))
    return pl.pallas_call(
        _ctx_kernel,
        grid=(nb,),
        in_specs=[cache(MLA_KV_RANK), cache(MLA_ROPE_DIM), cache(GQA_KV_WIDTH), cache(GQA_KV_WIDTH),
                  _layer_spec(wts["wk"], layer), _layer_spec(wts["wv"], layer),
                  pl.BlockSpec((MLA_ROPE_DIM, LANES), lambda b: (0, 0))],
        out_specs=[out(MLA_PAD_WIDTH), out_t(MIX_TILE), out(GQA_KV_WIDTH), out_t(GQA_KV_WIDTH)],
        out_shape=[jax.ShapeDtypeStruct((nb, past, MLA_PAD_WIDTH), BF16),
                   jax.ShapeDtypeStruct((nb, MIX_TILE, past), BF16),
                   jax.ShapeDtypeStruct((nb, past, GQA_KV_WIDTH), BF16),
                   jax.ShapeDtypeStruct((nb, GQA_KV_WIDTH, past), BF16)],
        compiler_params=pltpu.CompilerParams(dimension_semantics=("arbitrary",), vmem_limit_bytes=VMEM_LIMIT),
        name="ctx_prep",
    )(cache_ckv, cache_krope, cache_gk.reshape(nb, DEPTH, past, GQA_KV_WIDTH),
      cache_gv.reshape(nb, DEPTH, past, GQA_KV_WIDTH), wts["wk"], wts["wv"], consts["place"])


def _attn_kernel(*refs, n_seg, ahead):
    n_in = 1 + 2 * n_seg
    families = []
    for f, mla in enumerate((True, False)):
        ins = refs[f * n_in:(f + 1) * n_in]
        families.append((mla, ins[0], ins[1:1 + n_seg], ins[1 + n_seg:], refs[2 * n_in + f]))
    n_seqs, tq = refs[0].shape[:2]
    lower = _lower_half(tq)
    top = lax.broadcasted_iota(jnp.int32, (LANES, tq), 0) < HALF

    def scores_t(f, b, p, hd):
        mla, q_ref, k_refs, _, _ = families[f]
        if mla:
            cols = slice((2 * p + hd) * LANES, (2 * p + hd + 1) * LANES)
            qh = q_ref[b, :, cols]
            keys = [k_ref[b, :, cols] for k_ref in k_refs]
        else:
            q = q_ref[b, :, p * LANES:(p + 1) * LANES]
            qh = jnp.where(lower if hd == 0 else ~lower, q, jnp.zeros_like(q))
            keys = [k_ref[b] for k_ref in k_refs]
        return [lax.dot_general(k, qh, (((1,), (1,)), ((), ())), preferred_element_type=F32) for k in keys]

    def softmax_t(ss):
        m = functools.reduce(jnp.maximum, [jnp.max(s, axis=0, keepdims=True) for s in ss])
        es = [jnp.exp2(s - m) for s in ss]
        denom = functools.reduce(jnp.add, [jnp.sum(e, axis=0, keepdims=True) for e in es])
        return [e.astype(BF16) for e in es], denom

    def weighted_values_t(f, b, p, es, denom):
        mla, _, _, vt_refs, _ = families[f]
        rows = slice(p * LANES, (p + 1) * LANES) if mla else slice(None)
        acc = functools.reduce(jnp.add, [_dot(vt_ref[b, rows, :], e) for e, vt_ref in zip(es, vt_refs)])
        return acc * (1.0 / denom)

    work = [(f, b, p, hd) for f in range(2) for b in range(n_seqs) for p in range(HEAD_PAIRS) for hd in range(2)]
    pending = [scores_t(*w) for w in work[:ahead]]
    outs = {}
    for i, (f, b, p, hd) in enumerate(work):
        es, denom = softmax_t(pending.pop(0))
        if i + ahead < len(work):
            pending.append(scores_t(*work[i + ahead]))
        outs[f, b, p, hd] = weighted_values_t(f, b, p, es, denom)
        if (p, hd) != (HEAD_PAIRS - 1, 1):
            continue
        mla, o_ref = families[f][0], families[f][4]
        tiles = [jnp.where(top, outs[f, b, t, 0], outs[f, b, t, 1]).T for t in range(HEAD_PAIRS)]
        if not mla:
            t1 = pltpu.roll(tiles[1], HALF, 1)
            tiles = [jnp.where(lower, tiles[0], t1), jnp.where(lower, tiles[2], tiles[0]),
                     jnp.where(lower, t1, tiles[2])]
        for t in range(HEAD_PAIRS):
            o_ref[b, :, t * LANES:(t + 1) * LANES] = tiles[t].astype(o_ref.dtype)


def _attention(mla_q, mla_kv, gqa_q, gqa_kv, *, tq, bb=1, ahead=2):
    n_batch, sq = mla_q.shape[:2]
    n_seg = len(mla_kv)
    whole = lambda a: pl.BlockSpec((bb,) + a.shape[1:], lambda b, i: (b, 0, 0))
    in_specs, args = [], []
    for q, kv in ((mla_q, mla_kv), (gqa_q, gqa_kv)):
        in_specs += [pl.BlockSpec((bb, tq, q.shape[2]), lambda b, i: (b, i, 0))]
        in_specs += [whole(k) for k, _ in kv] + [whole(v) for _, v in kv]
        args += [q] + [k for k, _ in kv] + [v for _, v in kv]
    out_spec = pl.BlockSpec((bb, tq, MIX_TILE), lambda b, i: (b, i, 0))
    out_shape = jax.ShapeDtypeStruct((n_batch, sq, MIX_TILE), BF16)
    return pl.pallas_call(
        functools.partial(_attn_kernel, n_seg=n_seg, ahead=ahead),
        grid=(n_batch // bb, sq // tq),
        in_specs=in_specs,
        out_specs=[out_spec, out_spec],
        out_shape=[out_shape, out_shape],
        compiler_params=pltpu.CompilerParams(dimension_semantics=("arbitrary",) * 2, vmem_limit_bytes=VMEM_LIMIT),
        name=f"attn_s{n_seg}",
    )(*args)


def _fnet_kernel(ca_ref, sa_ref, cb_ref, sb_ref, uc_ref, us_ref, o_ref, tc_ref, ts_ref, *, scale):
    @pl.when(pl.program_id(1) == 0)
    def _build_twiddles():
        cb, sb = cb_ref[...], sb_ref[...]
        for a in range(ca_ref.shape[0]):
            ca, sa = ca_ref[a:a + 1, :], sa_ref[a:a + 1, :]
            rows = slice(a * DFT_SPLIT, (a + 1) * DFT_SPLIT)
            tc_ref[rows, :] = (ca * cb - sa * sb).astype(BF16)
            ts_ref[rows, :] = (sa * cb + ca * sb).astype(BF16)

    acc = _dot(tc_ref[...], uc_ref[...]) - _dot(ts_ref[...], us_ref[...])
    o_ref[...] = (acc * scale).astype(o_ref.dtype)


def _fnet(tables, uc, us, tm, tn):
    seq, width = uc.shape
    scale = float((seq * FNET_GROUP_DIM) ** -0.5)
    n_a = tm // DFT_SPLIT
    part_a = pl.BlockSpec((n_a, seq), lambda i, j: (i, 0))
    part_b = pl.BlockSpec((DFT_SPLIT, seq), lambda i, j: (0, 0))
    data = pl.BlockSpec((seq, tn), lambda i, j: (0, j))
    return pl.pallas_call(
        functools.partial(_fnet_kernel, scale=scale),
        grid=(seq // tm, width // tn),
        in_specs=[part_a, part_a, part_b, part_b, data, data],
        out_specs=pl.BlockSpec((tm, tn), lambda i, j: (i, j)),
        out_shape=jax.ShapeDtypeStruct((seq, width), BF16),
        scratch_shapes=[pltpu.VMEM((tm, seq), BF16), pltpu.VMEM((tm, seq), BF16)],
        compiler_params=pltpu.CompilerParams(dimension_semantics=("arbitrary", "arbitrary"),
                                             vmem_limit_bytes=VMEM_LIMIT),
        name="fnet",
    )(*tables, uc, us)


def _post_kernel(x_ref, mla_ref, fn_ref, gqa_ref, mod_ref, g2_ref, wo_ref, wg_ref, wu_ref, wd_ref, gf_ref,
                 o_ref, *, final):
    fn = fn_ref[...]
    if fn.shape[1] > FNET_WIDTH:
        fn = jnp.concatenate([fn[:, j:j + FNET_WIDTH] for j in range(0, fn.shape[1], FNET_WIDTH)], axis=0)
    mix = jnp.concatenate([mla_ref[...], fn, gqa_ref[...]], axis=-1)
    gate1 = mod_ref[:, 2 * D_MODEL:3 * D_MODEL]
    shift2 = mod_ref[:, 3 * D_MODEL:4 * D_MODEL]
    scale2 = mod_ref[:, 4 * D_MODEL:5 * D_MODEL]
    gate2 = mod_ref[:, 5 * D_MODEL:6 * D_MODEL]
    x = x_ref[...] + gate1 * _dot(mix, wo_ref[...])
    h = (_rms(x) * g2_ref[...] * (1.0 + scale2) + shift2).astype(BF16)
    g = _dot(h, wg_ref[...])
    u = _dot(h, wu_ref[...])
    a = (g * jax.nn.sigmoid(g) * u).astype(BF16)
    x = x + gate2 * _dot(a, wd_ref[...])
    if final:
        x = _rms(x) * gf_ref[...]
    o_ref[...] = x


def _post(x, mla_o, fnet_o, gqa_o, mod, wts, layer, g_final, *, seq, final, tm):
    n_tok = x.shape[0]
    n_tiles = n_tok // tm
    tiles_per_mod = n_tiles // mod.shape[0]
    resident = lambda arr: pl.BlockSpec((None,) + arr.shape[1:], lambda i: (layer, 0, 0), pipeline_mode=pl.Buffered(1))
    tok = lambda w: pl.BlockSpec((tm, w), lambda i: (i, 0))
    if tm <= seq:
        tps = seq // tm
        fnet_spec = pl.BlockSpec((tm, FNET_WIDTH), lambda i: (i % tps, i // tps))
    else:
        fnet_spec = pl.BlockSpec((seq, (tm // seq) * FNET_WIDTH), lambda i: (0, i))
    names = ("g2", "w_out", "w_gate", "w_up", "w_down")
    return pl.pallas_call(
        functools.partial(_post_kernel, final=final),
        grid=(n_tiles,),
        in_specs=[tok(D_MODEL), tok(MIX_TILE), fnet_spec, tok(MIX_TILE),
                  pl.BlockSpec((None, 1, 6 * D_MODEL), lambda i: (i // tiles_per_mod, 0, 0))]
                 + [resident(wts[n]) for n in names]
                 + [pl.BlockSpec((1, D_MODEL), lambda i: (0, 0))],
        out_specs=tok(D_MODEL),
        out_shape=jax.ShapeDtypeStruct((n_tok, D_MODEL), F32),
        compiler_params=pltpu.CompilerParams(dimension_semantics=("arbitrary",), vmem_limit_bytes=VMEM_LIMIT),
        name="post_final" if final else "post",
    )(x, mla_o, fnet_o, gqa_o, mod, *[wts[n] for n in names], g_final)


def _prep_weights(g_norm1, g_norm2, w_in, g_q_a, w_q_up, g_kv_a, w_kv_up, g_q_head, g_k_head,
                  w_out, w_ffn_gate, w_ffn_up, w_ffn_down):
    row = lambda g: g.reshape(DEPTH, 1, -1)
    w_kr = jnp.pad(w_in[:, :, _IN_A:_IN_C0], ((0, 0), (0, 0), (0, LANES - MLA_ROPE_DIM)))
    wq = w_q_up.reshape(DEPTH, MLA_Q_RANK, MLA_HEADS, MLA_QK_DIM)
    wq = jnp.concatenate([wq[..., MLA_NOPE_DIM:], wq[..., :MLA_NOPE_DIM]], axis=-1)
    wq = jnp.pad(wq, ((0, 0), (0, 0), (0, 0), (0, LANES - MLA_QK_DIM)))
    wkv = w_kv_up.reshape(DEPTH, MLA_KV_RANK, MLA_HEADS, MLA_NOPE_DIM + MLA_V_DIM)
    wk = jnp.pad(wkv[..., :MLA_NOPE_DIM], ((0, 0), (0, 0), (0, 0), (MLA_ROPE_DIM, LANES - MLA_QK_DIM)))
    wv = wkv[..., MLA_NOPE_DIM:]
    return {
        "g1": row(g_norm1), "g2": row(g_norm2), "g_q_a": row(g_q_a), "g_kv_a": row(g_kv_a),
        "g_q_head": row(jnp.tile(g_q_head, (1, GQA_HEADS))), "g_k_head": row(jnp.tile(g_k_head, (1, GQA_KV_HEADS))),
        "w_a": w_in[:, :, :_IN_A].astype(BF16), "w_c": w_in[:, :, _IN_C0:].astype(BF16), "w_kr": w_kr.astype(BF16),
        "wq": wq.reshape(DEPTH, MLA_Q_RANK, MLA_PAD_WIDTH).astype(BF16),
        "wk": wk.reshape(DEPTH, MLA_KV_RANK, MLA_PAD_WIDTH).astype(BF16),
        "wv": wv.reshape(DEPTH, MLA_KV_RANK, MIX_TILE).astype(BF16),
        "w_out": w_out.astype(BF16), "w_gate": w_ffn_gate.astype(BF16),
        "w_up": w_ffn_up.astype(BF16), "w_down": w_ffn_down.astype(BF16),
    }


def kernel(x_prompt, x_sample, cache_mla_ckv, cache_mla_krope, cache_gqa_k, cache_gqa_v, c, c_ctx, w_ada, b_ada,
           g_norm1, g_norm2, w_in, g_q_a, w_q_up, g_kv_a, w_kv_up, g_q_head, g_k_head, w_out,
           w_ffn_gate, w_ffn_up, w_ffn_down, g_final):
    n_pb, p_seq, _ = x_prompt.shape
    n_sb, s_seq, _ = x_sample.shape
    consts = {"wdft": _channel_dft(), "mean_q": _group_mean_matrix(GQA_HEADS),
              "mean_k": _group_mean_matrix(GQA_KV_HEADS), "place": _rope_place_matrix(),
              "rope": _rope_tables(s_seq)}
    dft_p = _seq_dft_tables(p_seq)
    dft_s = _seq_dft_tables(s_seq)
    gf = g_final.reshape(1, D_MODEL)
    wts = _prep_weights(g_norm1, g_norm2, w_in, g_q_a, w_q_up, g_kv_a, w_kv_up, g_q_head, g_k_head,
                        w_out, w_ffn_gate, w_ffn_up, w_ffn_down)

    cond_t = jnp.concatenate([c_ctx[None, :], c], axis=0).T
    mod = _ada(cond_t, w_ada, b_ada)

    xp = x_prompt.reshape(n_pb * p_seq, D_MODEL)
    xs = x_sample.reshape(n_sb * s_seq, D_MODEL)
    b3 = lambda a, nb: a.reshape(nb, a.shape[0] // nb, a.shape[1])
    caches = None
    for l in range(DEPTH):
        final = l == DEPTH - 1
        mod_p = mod[l, 0:1].reshape(1, 1, 6 * D_MODEL)
        mod_s = mod[l, 1:].reshape(n_sb, 1, 6 * D_MODEL)

        qm, km, vm, uc, us, gq, gk, gv, *caches = _proj(
            xp, mod_p, wts, l, consts, seq=p_seq, rope=False, emit_cache=True, prev_cache=caches)
        mla_o, gqa_o = _attention(b3(qm, n_pb), [(b3(km, n_pb), vm)], b3(gq, n_pb), [(b3(gk, n_pb), gv)],
                                  tq=p_seq, bb=4, ahead=8)
        fn_o = _fnet(dft_p, uc, us, tm=p_seq, tn=1024)
        xp = _post(xp, mla_o.reshape(-1, MIX_TILE), fn_o, gqa_o.reshape(-1, MIX_TILE), mod_p, wts, l, gf,
                   seq=p_seq, final=final, tm=POST_TILE)

        km_c, vm_c, gk_c, gv_c = _ctx_prep(l, cache_mla_ckv, cache_mla_krope, cache_gqa_k, cache_gqa_v, wts, consts)
        qm, km, vm, uc, us, gq, gk, gv = _proj(xs, mod_s, wts, l, consts, seq=s_seq, rope=True)
        mla_o, gqa_o = _attention(b3(qm, n_sb), [(km_c, vm_c), (b3(km, n_sb), vm)],
                                  b3(gq, n_sb), [(gk_c, gv_c), (b3(gk, n_sb), gv)], tq=512)
        fn_o = _fnet(dft_s, uc, us, tm=512, tn=n_sb * FNET_WIDTH)
        xs = _post(xs, mla_o.reshape(-1, MIX_TILE), fn_o, gqa_o.reshape(-1, MIX_TILE), mod_s, wts, l, gf,
                   seq=s_seq, final=final, tm=POST_TILE)

    ckv_new, krope_t, gk_t, gv_t = caches
    heads = lambda a: jnp.swapaxes(a, 2, 3).reshape(n_pb, DEPTH, p_seq, GQA_KV_HEADS, GQA_HEAD_DIM)
    return (xp.reshape(n_pb, p_seq, D_MODEL), xs.reshape(n_sb, s_seq, D_MODEL),
            ckv_new, jnp.swapaxes(krope_t, 2, 3), heads(gk_t), heads(gv_t))
```

```python
import functools

import numpy as np
import jax
import jax.numpy as jnp
from jax import lax
from jax.experimental import pallas as pl
from jax.experimental.pallas import tpu as pltpu

D_MODEL = 1024
DEPTH = 2
GRID_W = 64
MLA_HEADS = 6
MLA_Q_RANK = 384
MLA_KV_RANK = 256
MLA_NOPE_DIM = 64
MLA_ROPE_DIM = 32
MLA_V_DIM = 64
MLA_QK_DIM = MLA_NOPE_DIM + MLA_ROPE_DIM
FNET_GROUPS = 4
FNET_GROUP_DIM = 64
FNET_WIDTH = FNET_GROUPS * FNET_GROUP_DIM
GQA_HEADS = 6
GQA_KV_HEADS = 2
GQA_HEAD_DIM = 64
GQA_GROUP = GQA_HEADS // GQA_KV_HEADS
GQA_WIDTH = GQA_HEADS * GQA_HEAD_DIM
GQA_KV_WIDTH = GQA_KV_HEADS * GQA_HEAD_DIM
D_FF = 2816
ROPE_THETA = 10000.0
EPS = 1e-6
LOG2_E = 1.4426950408889634

LANES = 128
HALF = LANES // 2
HEAD_PAIRS = MLA_HEADS // 2
MLA_PAD_WIDTH = MLA_HEADS * LANES
MIX_TILE = HEAD_PAIRS * LANES
VMEM_LIMIT = 52 * 1024 * 1024
POST_TILE = 512
PROJ_SUB = 256

_IN_A = MLA_Q_RANK + MLA_KV_RANK
_IN_C0 = _IN_A + MLA_ROPE_DIM
_IN_C = FNET_WIDTH + GQA_WIDTH + 2 * GQA_KV_WIDTH
_C_U, _C_GQ, _C_GK, _C_GV = 0, FNET_WIDTH, FNET_WIDTH + GQA_WIDTH, FNET_WIDTH + GQA_WIDTH + GQA_KV_WIDTH

BF16 = jnp.bfloat16
F32 = jnp.float32


def _rope_tables(n_tokens):
    t = np.arange(n_tokens)
    row = (t // GRID_W).astype(np.float64)
    col = (t % GRID_W).astype(np.float64)

    def angles(rot_dim):
        n_axis = rot_dim // 4
        inv = ROPE_THETA ** (-np.arange(n_axis, dtype=np.float64) / n_axis)
        return np.concatenate([row[:, None] * inv, col[:, None] * inv], axis=-1)

    def tables(ang, lane_to_pair, is_first, is_second):
        cos = np.where((is_first | is_second)[None, :], np.cos(ang)[:, lane_to_pair], 1.0)
        sin = np.sin(ang)[:, lane_to_pair]
        sin_a = np.where(is_second[None, :], sin, 0.0)
        sin_b = np.where(is_first[None, :], -sin, 0.0)
        return [cos, sin_a, sin_b]

    lane = np.arange(LANES)
    half_m = MLA_ROPE_DIM // 2
    in_rope = lane < MLA_ROPE_DIM
    first_m = lane < half_m
    second_m = in_rope & ~first_m
    pair_m = lane % half_m
    half_g = GQA_HEAD_DIM // 2
    first_g = (lane % GQA_HEAD_DIM) < half_g
    pair_g = lane % half_g
    tabs = (tables(angles(MLA_ROPE_DIM), pair_m, first_m, second_m)
            + tables(angles(GQA_HEAD_DIM), pair_g, first_g, ~first_g))
    return jnp.asarray(np.concatenate(tabs, axis=-1), dtype=F32)


def _channel_dft():
    c = np.arange(FNET_GROUP_DIM)
    ang = 2.0 * np.pi * np.outer(c, c) / FNET_GROUP_DIM
    eye = np.eye(FNET_GROUPS)
    table = np.concatenate([np.kron(eye, np.cos(ang)), np.kron(eye, np.sin(ang))], axis=1)
    return jnp.asarray(table, dtype=F32).astype(BF16)


def _group_mean_matrix(n_heads):
    return jnp.asarray(np.kron(np.eye(n_heads), np.full((GQA_HEAD_DIM, GQA_HEAD_DIM), 1.0 / GQA_HEAD_DIM)), dtype=BF16)


def _rope_place_matrix():
    return jnp.asarray(np.eye(MLA_ROPE_DIM, LANES), dtype=BF16)


DFT_SPLIT = 32


def _seq_dft_tables(seq):
    s = np.arange(seq)
    n_a = seq // DFT_SPLIT
    ang_a = 2.0 * np.pi * ((np.arange(n_a)[:, None] * s[None, :]) % n_a) / n_a
    ang_b = 2.0 * np.pi * ((np.arange(DFT_SPLIT)[:, None] * s[None, :]) % seq) / seq
    return tuple(jnp.asarray(t, dtype=F32) for t in (np.cos(ang_a), np.sin(ang_a), np.cos(ang_b), np.sin(ang_b)))


def _rms(x):
    return x * lax.rsqrt(jnp.mean(x * x, axis=-1, keepdims=True) + EPS)


def _dot(a, b):
    return jnp.dot(a, b, preferred_element_type=F32)


def _group_rms(x, mean_mat):
    ms = _dot((x * x).astype(BF16), mean_mat)
    return x * lax.rsqrt(ms + EPS)


def _rotate(x, cos, sin_a, sin_b, half):
    return x * cos + pltpu.roll(x, half, 1) * sin_a + pltpu.roll(x, LANES - half, 1) * sin_b


def _lower_half(rows):
    return lax.broadcasted_iota(jnp.int32, (rows, LANES), 1) < HALF


def _layer_spec(arr, layer):
    return pl.BlockSpec((None,) + arr.shape[1:], lambda i: (layer, 0, 0))


def _ada_kernel(ct_ref, w_ref, b_ref, o_ref):
    s = ct_ref[...]
    s = s * jax.nn.sigmoid(s)
    w = w_ref[...]
    for m in range(o_ref.shape[0]):
        o_ref[m:m + 1, :] = jnp.sum(w * s[:, m:m + 1], axis=0, keepdims=True) + b_ref[...]


def _ada(cond_t, w_ada, b_ada, tn=2048):
    n_cond = cond_t.shape[1]
    width = w_ada.shape[2]
    return pl.pallas_call(
        _ada_kernel,
        grid=(DEPTH, width // tn),
        in_specs=[
            pl.BlockSpec((D_MODEL, n_cond), lambda l, j: (0, 0)),
            pl.BlockSpec((None, D_MODEL, tn), lambda l, j: (l, 0, j)),
            pl.BlockSpec((None, 1, tn), lambda l, j: (l, 0, j)),
        ],
        out_specs=pl.BlockSpec((None, n_cond, tn), lambda l, j: (l, 0, j)),
        out_shape=jax.ShapeDtypeStruct((DEPTH, n_cond, width), F32),
        compiler_params=pltpu.CompilerParams(dimension_semantics=("arbitrary", "arbitrary"),
                                             vmem_limit_bytes=VMEM_LIMIT),
        name="ada",
    )(cond_t, w_ada, b_ada.reshape(DEPTH, 1, width))


_PROJ_WEIGHTS = ("g1", "w_a", "w_c", "w_kr", "g_q_a", "wq", "g_kv_a", "wk", "wv", "g_q_head", "g_k_head")


def _proj_kernel(*refs, rope, emit_cache, n_prev, multi_seq):
    (x_ref, mod_ref, g1_ref, wa_ref, wc_ref, wkr_ref, gqa_ref, wq_ref, gkva_ref, wk_ref, wv_ref,
     gqh_ref, gkh_ref, wdft_ref, mq_ref, mk_ref) = refs[:16]
    refs = refs[16:]
    if rope:
        tab_ref, refs = refs[0], refs[1:]
    if n_prev:
        prev_refs, refs = refs[:4], refs[4:]
    qm_ref, km_ref, vm_ref, uc_ref, us_ref, gq_ref, gk_ref, gv_ref = refs[:8]
    cache_refs = refs[8:]

    shift = mod_ref[:, 0:D_MODEL]
    scale = mod_ref[:, D_MODEL:2 * D_MODEL]
    lower = _lower_half(PROJ_SUB)

    for r in range(x_ref.shape[0] // PROJ_SUB):
        rows = slice(r * PROJ_SUB, (r + 1) * PROJ_SUB)
        seq_cols = (slice(None), slice(r * FNET_WIDTH, (r + 1) * FNET_WIDTH)) if multi_seq else (rows, slice(None))
        t_idx = (r,) if multi_seq else (slice(None), rows)

        hb = (_rms(x_ref[rows, :]) * g1_ref[...] * (1.0 + scale) + shift).astype(BF16)
        pa = _dot(hb, wa_ref[...])
        pc = _dot(hb, wc_ref[...])
        kr = _dot(hb, wkr_ref[...])

        if rope:
            cos_m, sa_m, sb_m = (tab_ref[rows, i * LANES:(i + 1) * LANES] for i in range(3))
            cos_g, sa_g, sb_g = (tab_ref[rows, i * LANES:(i + 1) * LANES] for i in range(3, 6))

        gqn = _group_rms(pc[:, _C_GQ:_C_GQ + GQA_WIDTH], mq_ref[...]) * gqh_ref[...]
        tiles = []
        for p in range(HEAD_PAIRS):
            gp = gqn[:, p * LANES:(p + 1) * LANES]
            if rope:
                gp = _rotate(gp, cos_g, sa_g, sb_g, GQA_HEAD_DIM // 2)
            tiles.append(gp * (LOG2_E * GQA_HEAD_DIM ** -0.5))
        gq_ref[rows, 0:LANES] = jnp.where(lower, tiles[0], tiles[1]).astype(BF16)
        gq_ref[rows, LANES:2 * LANES] = jnp.where(
            lower, pltpu.roll(tiles[0], HALF, 1), pltpu.roll(tiles[2], HALF, 1)).astype(BF16)
        gq_ref[rows, 2 * LANES:] = jnp.where(lower, tiles[1], tiles[2]).astype(BF16)
        gkn = _group_rms(pc[:, _C_GK:_C_GK + GQA_KV_WIDTH], mk_ref[...]) * gkh_ref[...]
        gkr = _rotate(gkn, cos_g, sa_g, sb_g, GQA_HEAD_DIM // 2) if rope else gkn
        gv_t = pc[:, _C_GV:_C_GV + GQA_KV_WIDTH].T
        gk_ref[rows, :] = gkr.astype(BF16)
        gv_ref[t_idx] = gv_t.astype(BF16)

        cqn = _rms(pa[:, :MLA_Q_RANK]) * gqa_ref[...]
        q = _dot(cqn.astype(BF16), wq_ref[...])
        for hd in range(MLA_HEADS):
            qh = q[:, hd * LANES:(hd + 1) * LANES]
            if rope:
                qh = _rotate(qh, cos_m, sa_m, sb_m, MLA_ROPE_DIM // 2)
            qm_ref[rows, hd * LANES:(hd + 1) * LANES] = (qh * (LOG2_E * MLA_QK_DIM ** -0.5)).astype(BF16)

        ckvn = _rms(pa[:, MLA_Q_RANK:]) * gkva_ref[...]
        cb = ckvn.astype(BF16)
        kn = _dot(cb, wk_ref[...])
        krr = _rotate(kr, cos_m, sa_m, sb_m, MLA_ROPE_DIM // 2) if rope else kr
        for hd in range(MLA_HEADS):
            km_ref[rows, hd * LANES:(hd + 1) * LANES] = (kn[:, hd * LANES:(hd + 1) * LANES] + krr).astype(BF16)
        vm_ref[t_idx] = _dot(cb, wv_ref[...]).T.astype(BF16)

        ucs = _dot(pc[:, _C_U:_C_U + FNET_WIDTH].astype(BF16), wdft_ref[...])
        uc_ref[seq_cols] = ucs[:, :FNET_WIDTH].astype(BF16)
        us_ref[seq_cols] = ucs[:, FNET_WIDTH:].astype(BF16)

        if emit_cache:
            new = (ckvn, kr.T[:MLA_ROPE_DIM, :], gkn.T, gv_t)
            for i, out_ref in enumerate(cache_refs):
                if n_prev:
                    out_ref[r, :n_prev] = prev_refs[i][r]
                out_ref[r, n_prev] = new[i]


def _proj(x, mod, wts, layer, consts, *, seq, rope, prev_cache=None, emit_cache=False, tm=512):
    n_tok = x.shape[0]
    n_tiles = n_tok // tm
    n_seq = n_tok // seq
    tiles_per_mod = n_tiles // mod.shape[0]
    multi_seq = tm > seq
    if multi_seq:
        assert seq == PROJ_SUB and not rope
        spt = tm // seq
        seq_major = pl.BlockSpec((seq, spt * FNET_WIDTH), lambda i: (0, i))
        transposed = lambda w: pl.BlockSpec((spt, w, seq), lambda i: (i, 0, 0))
    else:
        assert not emit_cache
        tps = seq // tm
        seq_major = pl.BlockSpec((tm, FNET_WIDTH), lambda i: (i % tps, i // tps))
        transposed = lambda w: pl.BlockSpec((None, w, tm), lambda i: (i // tps, 0, i % tps))
    const = lambda arr: pl.BlockSpec(arr.shape, lambda i: (0,) * arr.ndim)
    tok = lambda w: pl.BlockSpec((tm, w), lambda i: (i, 0))
    in_specs = [tok(D_MODEL), pl.BlockSpec((None, 1, 6 * D_MODEL), lambda i: (i // tiles_per_mod, 0, 0))]
    args = [x, mod]
    for name in _PROJ_WEIGHTS:
        in_specs.append(_layer_spec(wts[name], layer))
        args.append(wts[name])
    for name in ("wdft", "mean_q", "mean_k"):
        in_specs.append(const(consts[name]))
        args.append(consts[name])
    if rope:
        in_specs.append(pl.BlockSpec((tm, 6 * LANES), lambda i: (i % tps, 0)))
        args.append(consts["rope"])
    out_specs = [tok(MLA_PAD_WIDTH), tok(MLA_PAD_WIDTH), transposed(MIX_TILE), seq_major, seq_major,
                 tok(GQA_WIDTH), tok(GQA_KV_WIDTH), transposed(GQA_KV_WIDTH)]
    out_shape = [jax.ShapeDtypeStruct((n_tok, MLA_PAD_WIDTH), BF16),
                 jax.ShapeDtypeStruct((n_tok, MLA_PAD_WIDTH), BF16),
                 jax.ShapeDtypeStruct((n_seq, MIX_TILE, seq), BF16),
                 jax.ShapeDtypeStruct((seq, n_seq * FNET_WIDTH), BF16),
                 jax.ShapeDtypeStruct((seq, n_seq * FNET_WIDTH), BF16),
                 jax.ShapeDtypeStruct((n_tok, GQA_WIDTH), BF16),
                 jax.ShapeDtypeStruct((n_tok, GQA_KV_WIDTH), BF16),
                 jax.ShapeDtypeStruct((n_seq, GQA_KV_WIDTH, seq), BF16)]
    n_prev = 0
    if emit_cache:
        n_prev = prev_cache[0].shape[1] if prev_cache is not None else 0
        layers = lambda n, tail: pl.BlockSpec((spt, n) + tail, lambda i: (i, 0, 0, 0))
        tails = ((seq, MLA_KV_RANK), (MLA_ROPE_DIM, seq), (GQA_KV_WIDTH, seq), (GQA_KV_WIDTH, seq))
        if n_prev:
            in_specs += [layers(n_prev, t) for t in tails]
            args += list(prev_cache)
        out_specs += [layers(n_prev + 1, t) for t in tails]
        out_shape += [jax.ShapeDtypeStruct((n_seq, n_prev + 1) + t, F32) for t in tails]
    return pl.pallas_call(
        functools.partial(_proj_kernel, rope=rope, emit_cache=emit_cache, n_prev=n_prev, multi_seq=multi_seq),
        grid=(n_tiles,),
        in_specs=in_specs,
        out_specs=out_specs,
        out_shape=out_shape,
        compiler_params=pltpu.CompilerParams(dimension_semantics=("arbitrary",), vmem_limit_bytes=VMEM_LIMIT),
        name="proj_rope" if rope else "proj_ctx",
    )(*args)


def _ctx_kernel(ckv_ref, kr_ref, gk_ref, gv_ref, wk_ref, wv_ref, place_ref, km_ref, vm_ref, gko_ref, gvo_ref):
    cb = ckv_ref[...].astype(BF16)
    kn = _dot(cb, wk_ref[...])
    kr = _dot(kr_ref[...].astype(BF16), place_ref[...])
    for hd in range(MLA_HEADS):
        km_ref[:, hd * LANES:(hd + 1) * LANES] = (kn[:, hd * LANES:(hd + 1) * LANES] + kr).astype(BF16)
    vm_ref[...] = _dot(cb, wv_ref[...]).T.astype(BF16)
    gko_ref[...] = gk_ref[...].astype(BF16)
    gvo_ref[...] = gv_ref[...].T.astype(BF16)


def _ctx_prep(layer, cache_ckv, cache_krope, cache_gk, cache_gv, wts, consts):
    nb, _, past, _ = cache_ckv.shape
    cache = lambda w: pl.BlockSpec((None, None, past, w), lambda b: (b, layer, 0, 0))
    out = lambda w: pl.BlockSpec((None, past, w), lambda b: (b, 0, 0))
    out_t = lambda w: pl.BlockSpec((None, w, past), lambda b: (b, 0, 0))
    return pl.pallas_call(
        _ctx_kernel,
        grid=(nb,),
        in_specs=[cache(MLA_KV_RANK), cache(MLA_ROPE_DIM), cache(GQA_KV_WIDTH), cache(GQA_KV_WIDTH),
                  _layer_spec(wts["wk"], layer), _layer_spec(wts["wv"], layer),
                  pl.BlockSpec((MLA_ROPE_DIM, LANES), lambda b: (0, 0))],
        out_specs=[out(MLA_PAD_WIDTH), out_t(MIX_TILE), out(GQA_KV_WIDTH), out_t(GQA_KV_WIDTH)],
        out_shape=[jax.ShapeDtypeStruct((nb, past, MLA_PAD_WIDTH), BF16),
                   jax.ShapeDtypeStruct((nb, MIX_TILE, past), BF16),
                   jax.ShapeDtypeStruct((nb, past, GQA_KV_WIDTH), BF16),
                   jax.ShapeDtypeStruct((nb, GQA_KV_WIDTH, past), BF16)],
        compiler_params=pltpu.CompilerParams(dimension_semantics=("arbitrary",), vmem_limit_bytes=VMEM_LIMIT),
        name="ctx_prep",
    )(cache_ckv, cache_krope, cache_gk.reshape(nb, DEPTH, past, GQA_KV_WIDTH),
      cache_gv.reshape(nb, DEPTH, past, GQA_KV_WIDTH), wts["wk"], wts["wv"], consts["place"])


def _attn_kernel(*refs, n_seg, ahead, n_cast):
    n_in = 1 + 2 * n_seg
    outs_at = 2 * n_in + n_cast
    families = []
    for f, mla in enumerate((True, False)):
        ins = refs[f * n_in:(f + 1) * n_in]
        families.append((mla, ins[0], ins[1:1 + n_seg], ins[1 + n_seg:], refs[outs_at + f]))
    for c in range(n_cast):
        refs[outs_at + 2 + c][...] = refs[2 * n_in + c][...].astype(BF16)
    n_seqs, tq = refs[0].shape[:2]
    lower = _lower_half(tq)
    top = lax.broadcasted_iota(jnp.int32, (LANES, tq), 0) < HALF

    def scores_t(f, b, p, hd):
        mla, q_ref, k_refs, _, _ = families[f]
        if mla:
            cols = slice((2 * p + hd) * LANES, (2 * p + hd + 1) * LANES)
            qh = q_ref[b, :, cols]
            keys = [k_ref[b, :, cols] for k_ref in k_refs]
        else:
            q = q_ref[b, :, p * LANES:(p + 1) * LANES]
            qh = jnp.where(lower if hd == 0 else ~lower, q, jnp.zeros_like(q))
            keys = [k_ref[b] for k_ref in k_refs]
        return [lax.dot_general(k, qh, (((1,), (1,)), ((), ())), preferred_element_type=F32) for k in keys]

    def softmax_t(ss):
        m = functools.reduce(jnp.maximum, [jnp.max(s, axis=0, keepdims=True) for s in ss])
        es = [jnp.exp2(s - m) for s in ss]
        denom = functools.reduce(jnp.add, [jnp.sum(e, axis=0, keepdims=True) for e in es])
        return [e.astype(BF16) for e in es], denom

    def weighted_values_t(f, b, p, es, denom):
        mla, _, _, vt_refs, _ = families[f]
        rows = slice(p * LANES, (p + 1) * LANES) if mla else slice(None)
        acc = functools.reduce(jnp.add, [_dot(vt_ref[b, rows, :], e) for e, vt_ref in zip(es, vt_refs)])
        return acc * (1.0 / denom)

    work = [(f, b, p, hd) for f in range(2) for b in range(n_seqs) for p in range(HEAD_PAIRS) for hd in range(2)]
    pending = [scores_t(*w) for w in work[:ahead]]
    outs = {}
    for i, (f, b, p, hd) in enumerate(work):
        es, denom = softmax_t(pending.pop(0))
        if i + ahead < len(work):
            pending.append(scores_t(*work[i + ahead]))
        outs[f, b, p, hd] = weighted_values_t(f, b, p, es, denom)
        if (p, hd) != (HEAD_PAIRS - 1, 1):
            continue
        mla, o_ref = families[f][0], families[f][4]
        tiles = [jnp.where(top, outs[f, b, t, 0], outs[f, b, t, 1]).T for t in range(HEAD_PAIRS)]
        if not mla:
            t1 = pltpu.roll(tiles[1], HALF, 1)
            tiles = [jnp.where(lower, tiles[0], t1), jnp.where(lower, tiles[2], tiles[0]),
                     jnp.where(lower, t1, tiles[2])]
        for t in range(HEAD_PAIRS):
            o_ref[b, :, t * LANES:(t + 1) * LANES] = tiles[t].astype(o_ref.dtype)


def _attention(mla_q, mla_kv, gqa_q, gqa_kv, *, tq, bb=1, ahead=2, cast=(), layer=0):
    n_batch, sq = mla_q.shape[:2]
    n_seg = len(mla_kv)
    n_q = sq // tq
    steps = (n_batch // bb) * n_q
    whole = lambda a: pl.BlockSpec((bb,) + a.shape[1:], lambda b, i: (b, 0, 0))
    in_specs, args = [], []
    for q, kv in ((mla_q, mla_kv), (gqa_q, gqa_kv)):
        in_specs += [pl.BlockSpec((bb, tq, q.shape[2]), lambda b, i: (b, i, 0))]
        in_specs += [whole(k) for k, _ in kv] + [whole(v) for _, v in kv]
        args += [q] + [k for k, _ in kv] + [v for _, v in kv]
    out_spec = pl.BlockSpec((bb, tq, MIX_TILE), lambda b, i: (b, i, 0))
    out_specs = [out_spec, out_spec]
    out_shape = [jax.ShapeDtypeStruct((n_batch, sq, MIX_TILE), BF16)] * 2
    for w in cast:
        rows, cols = w.shape[1] // steps, w.shape[2]
        in_specs.append(pl.BlockSpec((None, rows, cols), lambda b, i: (layer, b * n_q + i, 0)))
        out_specs.append(pl.BlockSpec((rows, cols), lambda b, i: (b * n_q + i, 0)))
        out_shape.append(jax.ShapeDtypeStruct(w.shape[1:], BF16))
        args.append(w)
    return pl.pallas_call(
        functools.partial(_attn_kernel, n_seg=n_seg, ahead=ahead, n_cast=len(cast)),
        grid=(n_batch // bb, n_q),
        in_specs=in_specs,
        out_specs=out_specs,
        out_shape=out_shape,
        compiler_params=pltpu.CompilerParams(dimension_semantics=("arbitrary",) * 2, vmem_limit_bytes=VMEM_LIMIT),
        name=f"attn_s{n_seg}",
    )(*args)


def _fnet_kernel(ca_ref, sa_ref, cb_ref, sb_ref, uc_ref, us_ref, o_ref, tc_ref, ts_ref, *, scale):
    @pl.when(pl.program_id(1) == 0)
    def _build_twiddles():
        cb, sb = cb_ref[...], sb_ref[...]
        for a in range(ca_ref.shape[0]):
            ca, sa = ca_ref[a:a + 1, :], sa_ref[a:a + 1, :]
            rows = slice(a * DFT_SPLIT, (a + 1) * DFT_SPLIT)
            tc_ref[rows, :] = (ca * cb - sa * sb).astype(BF16)
            ts_ref[rows, :] = (sa * cb + ca * sb).astype(BF16)

    acc = _dot(tc_ref[...], uc_ref[...]) - _dot(ts_ref[...], us_ref[...])
    o_ref[...] = (acc * scale).astype(o_ref.dtype)


def _fnet(tables, uc, us, tm, tn):
    seq, width = uc.shape
    scale = float((seq * FNET_GROUP_DIM) ** -0.5)
    n_a = tm // DFT_SPLIT
    part_a = pl.BlockSpec((n_a, seq), lambda i, j: (i, 0))
    part_b = pl.BlockSpec((DFT_SPLIT, seq), lambda i, j: (0, 0))
    data = pl.BlockSpec((seq, tn), lambda i, j: (0, j))
    return pl.pallas_call(
        functools.partial(_fnet_kernel, scale=scale),
        grid=(seq // tm, width // tn),
        in_specs=[part_a, part_a, part_b, part_b, data, data],
        out_specs=pl.BlockSpec((tm, tn), lambda i, j: (i, j)),
        out_shape=jax.ShapeDtypeStruct((seq, width), BF16),
        scratch_shapes=[pltpu.VMEM((tm, seq), BF16), pltpu.VMEM((tm, seq), BF16)],
        compiler_params=pltpu.CompilerParams(dimension_semantics=("arbitrary", "arbitrary"),
                                             vmem_limit_bytes=VMEM_LIMIT),
        name="fnet",
    )(*tables, uc, us)


def _post_kernel(x_ref, mla_ref, fn_ref, gqa_ref, mod_ref, g2_ref, wo_ref, wg_ref, wu_ref, wd_ref, gf_ref,
                 o_ref, *, final):
    fn = fn_ref[...]
    if fn.shape[1] > FNET_WIDTH:
        fn = jnp.concatenate([fn[:, j:j + FNET_WIDTH] for j in range(0, fn.shape[1], FNET_WIDTH)], axis=0)
    mix = jnp.concatenate([mla_ref[...], fn, gqa_ref[...]], axis=-1)
    gate1 = mod_ref[:, 2 * D_MODEL:3 * D_MODEL]
    shift2 = mod_ref[:, 3 * D_MODEL:4 * D_MODEL]
    scale2 = mod_ref[:, 4 * D_MODEL:5 * D_MODEL]
    gate2 = mod_ref[:, 5 * D_MODEL:6 * D_MODEL]
    x = x_ref[...] + gate1 * _dot(mix, wo_ref[...])
    h = (_rms(x) * g2_ref[...] * (1.0 + scale2) + shift2).astype(BF16)
    g = _dot(h, wg_ref[...])
    u = _dot(h, wu_ref[...])
    a = (g * jax.nn.sigmoid(g) * u).astype(BF16)
    x = x + gate2 * _dot(a, wd_ref[...])
    if final:
        x = _rms(x) * gf_ref[...]
    o_ref[...] = x


def _post(x, mla_o, fnet_o, gqa_o, mod, g2, ffn, layer, g_final, *, seq, final, tm):
    n_tok = x.shape[0]
    n_tiles = n_tok // tm
    tiles_per_mod = n_tiles // mod.shape[0]
    resident = lambda arr: pl.BlockSpec(arr.shape, lambda i: (0, 0), pipeline_mode=pl.Buffered(1))
    tok = lambda w: pl.BlockSpec((tm, w), lambda i: (i, 0))
    if tm <= seq:
        tps = seq // tm
        fnet_spec = pl.BlockSpec((tm, FNET_WIDTH), lambda i: (i % tps, i // tps))
    else:
        fnet_spec = pl.BlockSpec((seq, (tm // seq) * FNET_WIDTH), lambda i: (0, i))
    return pl.pallas_call(
        functools.partial(_post_kernel, final=final),
        grid=(n_tiles,),
        in_specs=[tok(D_MODEL), tok(MIX_TILE), fnet_spec, tok(MIX_TILE),
                  pl.BlockSpec((None, 1, 6 * D_MODEL), lambda i: (i // tiles_per_mod, 0, 0)),
                  _layer_spec(g2, layer)]
                 + [resident(w) for w in ffn]
                 + [pl.BlockSpec((1, D_MODEL), lambda i: (0, 0))],
        out_specs=tok(D_MODEL),
        out_shape=jax.ShapeDtypeStruct((n_tok, D_MODEL), F32),
        compiler_params=pltpu.CompilerParams(dimension_semantics=("arbitrary",), vmem_limit_bytes=VMEM_LIMIT),
        name="post_final" if final else "post",
    )(x, mla_o, fnet_o, gqa_o, mod, g2, *ffn, g_final)


def _prep_weights(g_norm1, g_norm2, w_in, g_q_a, w_q_up, g_kv_a, w_kv_up, g_q_head, g_k_head):
    row = lambda g: g.reshape(DEPTH, 1, -1)
    w_kr = jnp.pad(w_in[:, :, _IN_A:_IN_C0], ((0, 0), (0, 0), (0, LANES - MLA_ROPE_DIM)))
    wq = w_q_up.reshape(DEPTH, MLA_Q_RANK, MLA_HEADS, MLA_QK_DIM)
    wq = jnp.concatenate([wq[..., MLA_NOPE_DIM:], wq[..., :MLA_NOPE_DIM]], axis=-1)
    wq = jnp.pad(wq, ((0, 0), (0, 0), (0, 0), (0, LANES - MLA_QK_DIM)))
    wkv = w_kv_up.reshape(DEPTH, MLA_KV_RANK, MLA_HEADS, MLA_NOPE_DIM + MLA_V_DIM)
    wk = jnp.pad(wkv[..., :MLA_NOPE_DIM], ((0, 0), (0, 0), (0, 0), (MLA_ROPE_DIM, LANES - MLA_QK_DIM)))
    wv = wkv[..., MLA_NOPE_DIM:]
    return {
        "g1": row(g_norm1), "g2": row(g_norm2), "g_q_a": row(g_q_a), "g_kv_a": row(g_kv_a),
        "g_q_head": row(jnp.tile(g_q_head, (1, GQA_HEADS))), "g_k_head": row(jnp.tile(g_k_head, (1, GQA_KV_HEADS))),
        "w_a": w_in[:, :, :_IN_A].astype(BF16), "w_c": w_in[:, :, _IN_C0:].astype(BF16), "w_kr": w_kr.astype(BF16),
        "wq": wq.reshape(DEPTH, MLA_Q_RANK, MLA_PAD_WIDTH).astype(BF16),
        "wk": wk.reshape(DEPTH, MLA_KV_RANK, MLA_PAD_WIDTH).astype(BF16),
        "wv": wv.reshape(DEPTH, MLA_KV_RANK, MIX_TILE).astype(BF16),
    }


def kernel(x_prompt, x_sample, cache_mla_ckv, cache_mla_krope, cache_gqa_k, cache_gqa_v, c, c_ctx, w_ada, b_ada,
           g_norm1, g_norm2, w_in, g_q_a, w_q_up, g_kv_a, w_kv_up, g_q_head, g_k_head, w_out,
           w_ffn_gate, w_ffn_up, w_ffn_down, g_final):
    n_pb, p_seq, _ = x_prompt.shape
    n_sb, s_seq, _ = x_sample.shape
    consts = {"wdft": _channel_dft(), "mean_q": _group_mean_matrix(GQA_HEADS),
              "mean_k": _group_mean_matrix(GQA_KV_HEADS), "place": _rope_place_matrix(),
              "rope": _rope_tables(s_seq)}
    dft_p = _seq_dft_tables(p_seq)
    dft_s = _seq_dft_tables(s_seq)
    gf = g_final.reshape(1, D_MODEL)
    wts = _prep_weights(g_norm1, g_norm2, w_in, g_q_a, w_q_up, g_kv_a, w_kv_up, g_q_head, g_k_head)

    cond_t = jnp.concatenate([c_ctx[None, :], c], axis=0).T
    mod = _ada(cond_t, w_ada, b_ada)

    xp = x_prompt.reshape(n_pb * p_seq, D_MODEL)
    xs = x_sample.reshape(n_sb * s_seq, D_MODEL)
    b3 = lambda a, nb: a.reshape(nb, a.shape[0] // nb, a.shape[1])
    caches = None
    for l in range(DEPTH):
        final = l == DEPTH - 1
        mod_p = mod[l, 0:1].reshape(1, 1, 6 * D_MODEL)
        mod_s = mod[l, 1:].reshape(n_sb, 1, 6 * D_MODEL)

        km_c, vm_c, gk_c, gv_c = _ctx_prep(l, cache_mla_ckv, cache_mla_krope, cache_gqa_k, cache_gqa_v, wts, consts)
        qm, km, vm, uc_s, us_s, gq, gk, gv = _proj(xs, mod_s, wts, l, consts, seq=s_seq, rope=True)
        mla_s, gqa_s, *ffn = _attention(b3(qm, n_sb), [(km_c, vm_c), (b3(km, n_sb), vm)],
                                        b3(gq, n_sb), [(gk_c, gv_c), (b3(gk, n_sb), gv)], tq=512,
                                        cast=(w_out, w_ffn_gate, w_ffn_up, w_ffn_down), layer=l)

        qm, km, vm, uc, us, gq, gk, gv, *caches = _proj(
            xp, mod_p, wts, l, consts, seq=p_seq, rope=False, emit_cache=True, prev_cache=caches)
        mla_o, gqa_o = _attention(b3(qm, n_pb), [(b3(km, n_pb), vm)], b3(gq, n_pb), [(b3(gk, n_pb), gv)],
                                  tq=p_seq, bb=4, ahead=8)
        fn_o = _fnet(dft_p, uc, us, tm=p_seq, tn=1024)
        xp = _post(xp, mla_o.reshape(-1, MIX_TILE), fn_o, gqa_o.reshape(-1, MIX_TILE), mod_p, wts["g2"], ffn, l, gf,
                   seq=p_seq, final=final, tm=POST_TILE)

        fn_o = _fnet(dft_s, uc_s, us_s, tm=512, tn=n_sb * FNET_WIDTH)
        xs = _post(xs, mla_s.reshape(-1, MIX_TILE), fn_o, gqa_s.reshape(-1, MIX_TILE), mod_s, wts["g2"], ffn, l, gf,
                   seq=s_seq, final=final, tm=POST_TILE)

    ckv_new, krope_t, gk_t, gv_t = caches
    heads = lambda a: jnp.swapaxes(a, 2, 3).reshape(n_pb, DEPTH, p_seq, GQA_KV_HEADS, GQA_HEAD_DIM)
    return (xp.reshape(n_pb, p_seq, D_MODEL), xs.reshape(n_sb, s_seq, D_MODEL),
            ckv_new, jnp.swapaxes(krope_t, 2, 3), heads(gk_t), heads(gv_t))
```

```python
import functools

import numpy as np
import jax
import jax.numpy as jnp
from jax import lax
from jax.experimental import pallas as pl
from jax.experimental.pallas import tpu as pltpu

D_MODEL = 1024
DEPTH = 2
GRID_W = 64
MLA_HEADS = 6
MLA_Q_RANK = 384
MLA_KV_RANK = 256
MLA_NOPE_DIM = 64
MLA_ROPE_DIM = 32
MLA_V_DIM = 64
MLA_QK_DIM = MLA_NOPE_DIM + MLA_ROPE_DIM
FNET_GROUPS = 4
FNET_GROUP_DIM = 64
FNET_WIDTH = FNET_GROUPS * FNET_GROUP_DIM
GQA_HEADS = 6
GQA_KV_HEADS = 2
GQA_HEAD_DIM = 64
GQA_GROUP = GQA_HEADS // GQA_KV_HEADS
GQA_WIDTH = GQA_HEADS * GQA_HEAD_DIM
GQA_KV_WIDTH = GQA_KV_HEADS * GQA_HEAD_DIM
D_FF = 2816
ROPE_THETA = 10000.0
EPS = 1e-6
LOG2_E = 1.4426950408889634

LANES = 128
HALF = LANES // 2
HEAD_PAIRS = MLA_HEADS // 2
MLA_PAD_WIDTH = MLA_HEADS * LANES
MIX_TILE = HEAD_PAIRS * LANES
VMEM_LIMIT = 52 * 1024 * 1024
POST_TILE = 512
PROJ_SUB = 256

_IN_A = MLA_Q_RANK + MLA_KV_RANK
_IN_C0 = _IN_A + MLA_ROPE_DIM
_IN_C = FNET_WIDTH + GQA_WIDTH + 2 * GQA_KV_WIDTH
_C_U, _C_GQ, _C_GK, _C_GV = 0, FNET_WIDTH, FNET_WIDTH + GQA_WIDTH, FNET_WIDTH + GQA_WIDTH + GQA_KV_WIDTH

BF16 = jnp.bfloat16
F32 = jnp.float32


def _rope_tables(n_tokens):
    t = np.arange(n_tokens)
    row = (t // GRID_W).astype(np.float64)
    col = (t % GRID_W).astype(np.float64)

    def angles(rot_dim):
        n_axis = rot_dim // 4
        inv = ROPE_THETA ** (-np.arange(n_axis, dtype=np.float64) / n_axis)
        return np.concatenate([row[:, None] * inv, col[:, None] * inv], axis=-1)

    def tables(ang, lane_to_pair, is_first, is_second):
        cos = np.where((is_first | is_second)[None, :], np.cos(ang)[:, lane_to_pair], 1.0)
        sin = np.sin(ang)[:, lane_to_pair]
        sin_a = np.where(is_second[None, :], sin, 0.0)
        sin_b = np.where(is_first[None, :], -sin, 0.0)
        return [cos, sin_a, sin_b]

    lane = np.arange(LANES)
    half_m = MLA_ROPE_DIM // 2
    in_rope = lane < MLA_ROPE_DIM
    first_m = lane < half_m
    second_m = in_rope & ~first_m
    pair_m = lane % half_m
    half_g = GQA_HEAD_DIM // 2
    first_g = (lane % GQA_HEAD_DIM) < half_g
    pair_g = lane % half_g
    tabs = (tables(angles(MLA_ROPE_DIM), pair_m, first_m, second_m)
            + tables(angles(GQA_HEAD_DIM), pair_g, first_g, ~first_g))
    return jnp.asarray(np.concatenate(tabs, axis=-1), dtype=F32)


def _channel_dft():
    c = np.arange(FNET_GROUP_DIM)
    ang = 2.0 * np.pi * np.outer(c, c) / FNET_GROUP_DIM
    eye = np.eye(FNET_GROUPS)
    table = np.concatenate([np.kron(eye, np.cos(ang)), np.kron(eye, np.sin(ang))], axis=1)
    return jnp.asarray(table, dtype=F32).astype(BF16)


def _group_mean_matrix(n_heads):
    return jnp.asarray(np.kron(np.eye(n_heads), np.full((GQA_HEAD_DIM, GQA_HEAD_DIM), 1.0 / GQA_HEAD_DIM)), dtype=BF16)


def _rope_place_matrix():
    return jnp.asarray(np.eye(MLA_ROPE_DIM, LANES), dtype=BF16)


DFT_SPLIT = 32


def _seq_dft_tables(seq):
    s = np.arange(seq)
    n_a = seq // DFT_SPLIT
    ang_a = 2.0 * np.pi * ((np.arange(n_a)[:, None] * s[None, :]) % n_a) / n_a
    ang_b = 2.0 * np.pi * ((np.arange(DFT_SPLIT)[:, None] * s[None, :]) % seq) / seq
    return tuple(jnp.asarray(t, dtype=F32) for t in (np.cos(ang_a), np.sin(ang_a), np.cos(ang_b), np.sin(ang_b)))


def _rms(x):
    return x * lax.rsqrt(jnp.mean(x * x, axis=-1, keepdims=True) + EPS)


def _dot(a, b):
    return jnp.dot(a, b, preferred_element_type=F32)


def _group_rms(x, mean_mat):
    ms = _dot((x * x).astype(BF16), mean_mat)
    return x * lax.rsqrt(ms + EPS)


def _rotate(x, cos, sin_a, sin_b, half):
    return x * cos + pltpu.roll(x, half, 1) * sin_a + pltpu.roll(x, LANES - half, 1) * sin_b


def _lower_half(rows):
    return lax.broadcasted_iota(jnp.int32, (rows, LANES), 1) < HALF


def _layer_spec(arr, layer):
    return pl.BlockSpec((None,) + arr.shape[1:], lambda i: (layer, 0, 0))


def _ada_kernel(ct_ref, w_ref, b_ref, o_ref):
    s = ct_ref[...]
    s = s * jax.nn.sigmoid(s)
    w = w_ref[...]
    for m in range(o_ref.shape[0]):
        o_ref[m:m + 1, :] = jnp.sum(w * s[:, m:m + 1], axis=0, keepdims=True) + b_ref[...]


def _ada(cond_t, w_ada, b_ada, tn=2048):
    n_cond = cond_t.shape[1]
    width = w_ada.shape[2]
    return pl.pallas_call(
        _ada_kernel,
        grid=(DEPTH, width // tn),
        in_specs=[
            pl.BlockSpec((D_MODEL, n_cond), lambda l, j: (0, 0)),
            pl.BlockSpec((None, D_MODEL, tn), lambda l, j: (l, 0, j)),
            pl.BlockSpec((None, 1, tn), lambda l, j: (l, 0, j)),
        ],
        out_specs=pl.BlockSpec((None, n_cond, tn), lambda l, j: (l, 0, j)),
        out_shape=jax.ShapeDtypeStruct((DEPTH, n_cond, width), F32),
        compiler_params=pltpu.CompilerParams(dimension_semantics=("arbitrary", "arbitrary"),
                                             vmem_limit_bytes=VMEM_LIMIT),
        name="ada",
    )(cond_t, w_ada, b_ada.reshape(DEPTH, 1, width))


_PROJ_WEIGHTS = ("g1", "w_a", "w_c", "w_kr", "g_q_a", "wq", "g_kv_a", "wk", "wv", "g_q_head", "g_k_head")


def _proj_kernel(*refs, rope, emit_cache, n_prev, multi_seq):
    (x_ref, mod_ref, g1_ref, wa_ref, wc_ref, wkr_ref, gqa_ref, wq_ref, gkva_ref, wk_ref, wv_ref,
     gqh_ref, gkh_ref, wdft_ref, mq_ref, mk_ref) = refs[:16]
    refs = refs[16:]
    if rope:
        tab_ref, refs = refs[0], refs[1:]
    if n_prev:
        prev_refs, refs = refs[:4], refs[4:]
    qm_ref, km_ref, vm_ref, uc_ref, us_ref, gq_ref, gk_ref, gv_ref = refs[:8]
    cache_refs = refs[8:]

    shift = mod_ref[:, 0:D_MODEL]
    scale = mod_ref[:, D_MODEL:2 * D_MODEL]
    lower = _lower_half(PROJ_SUB)

    for r in range(x_ref.shape[0] // PROJ_SUB):
        rows = slice(r * PROJ_SUB, (r + 1) * PROJ_SUB)
        seq_cols = (slice(None), slice(r * FNET_WIDTH, (r + 1) * FNET_WIDTH)) if multi_seq else (rows, slice(None))
        t_idx = (r,) if multi_seq else (slice(None), rows)

        hb = (_rms(x_ref[rows, :]) * g1_ref[...] * (1.0 + scale) + shift).astype(BF16)
        pa = _dot(hb, wa_ref[...])
        pc = _dot(hb, wc_ref[...])
        kr = _dot(hb, wkr_ref[...])

        if rope:
            cos_m, sa_m, sb_m = (tab_ref[rows, i * LANES:(i + 1) * LANES] for i in range(3))
            cos_g, sa_g, sb_g = (tab_ref[rows, i * LANES:(i + 1) * LANES] for i in range(3, 6))

        gqn = _group_rms(pc[:, _C_GQ:_C_GQ + GQA_WIDTH], mq_ref[...]) * gqh_ref[...]
        tiles = []
        for p in range(HEAD_PAIRS):
            gp = gqn[:, p * LANES:(p + 1) * LANES]
            if rope:
                gp = _rotate(gp, cos_g, sa_g, sb_g, GQA_HEAD_DIM // 2)
            tiles.append(gp * (LOG2_E * GQA_HEAD_DIM ** -0.5))
        gq_ref[rows, 0:LANES] = jnp.where(lower, tiles[0], tiles[1]).astype(BF16)
        gq_ref[rows, LANES:2 * LANES] = jnp.where(
            lower, pltpu.roll(tiles[0], HALF, 1), pltpu.roll(tiles[2], HALF, 1)).astype(BF16)
        gq_ref[rows, 2 * LANES:] = jnp.where(lower, tiles[1], tiles[2]).astype(BF16)
        gkn = _group_rms(pc[:, _C_GK:_C_GK + GQA_KV_WIDTH], mk_ref[...]) * gkh_ref[...]
        gkr = _rotate(gkn, cos_g, sa_g, sb_g, GQA_HEAD_DIM // 2) if rope else gkn
        gv_t = pc[:, _C_GV:_C_GV + GQA_KV_WIDTH].T
        gk_ref[rows, :] = gkr.astype(BF16)
        gv_ref[t_idx] = gv_t.astype(BF16)

        cqn = _rms(pa[:, :MLA_Q_RANK]) * gqa_ref[...]
        q = _dot(cqn.astype(BF16), wq_ref[...])
        for hd in range(MLA_HEADS):
            qh = q[:, hd * LANES:(hd + 1) * LANES]
            if rope:
                qh = _rotate(qh, cos_m, sa_m, sb_m, MLA_ROPE_DIM // 2)
            qm_ref[rows, hd * LANES:(hd + 1) * LANES] = (qh * (LOG2_E * MLA_QK_DIM ** -0.5)).astype(BF16)

        ckvn = _rms(pa[:, MLA_Q_RANK:]) * gkva_ref[...]
        cb = ckvn.astype(BF16)
        kn = _dot(cb, wk_ref[...])
        krr = _rotate(kr, cos_m, sa_m, sb_m, MLA_ROPE_DIM // 2) if rope else kr
        for hd in range(MLA_HEADS):
            km_ref[rows, hd * LANES:(hd + 1) * LANES] = (kn[:, hd * LANES:(hd + 1) * LANES] + krr).astype(BF16)
        vm_ref[t_idx] = _dot(cb, wv_ref[...]).T.astype(BF16)

        ucs = _dot(pc[:, _C_U:_C_U + FNET_WIDTH].astype(BF16), wdft_ref[...])
        uc_ref[seq_cols] = ucs[:, :FNET_WIDTH].astype(BF16)
        us_ref[seq_cols] = ucs[:, FNET_WIDTH:].astype(BF16)

        if emit_cache:
            new = (ckvn, kr.T[:MLA_ROPE_DIM, :], gkn.T, gv_t)
            for i, out_ref in enumerate(cache_refs):
                if n_prev:
                    out_ref[r, :n_prev] = prev_refs[i][r]
                out_ref[r, n_prev] = new[i]


def _proj(x, mod, wts, win_parts, layer, consts, *, seq, rope, prev_cache=None, emit_cache=False, tm=1024):
    n_tok = x.shape[0]
    n_tiles = n_tok // tm
    n_seq = n_tok // seq
    tiles_per_mod = n_tiles // mod.shape[0]
    multi_seq = tm > seq
    if multi_seq:
        assert seq == PROJ_SUB and not rope
        spt = tm // seq
        seq_major = pl.BlockSpec((seq, spt * FNET_WIDTH), lambda i: (0, i))
        transposed = lambda w: pl.BlockSpec((spt, w, seq), lambda i: (i, 0, 0))
    else:
        assert not emit_cache
        tps = seq // tm
        seq_major = pl.BlockSpec((tm, FNET_WIDTH), lambda i: (i % tps, i // tps))
        transposed = lambda w: pl.BlockSpec((None, w, tm), lambda i: (i // tps, 0, i % tps))
    const = lambda arr: pl.BlockSpec(arr.shape, lambda i: (0,) * arr.ndim)
    tok = lambda w: pl.BlockSpec((tm, w), lambda i: (i, 0))
    in_specs = [tok(D_MODEL), pl.BlockSpec((None, 1, 6 * D_MODEL), lambda i: (i // tiles_per_mod, 0, 0))]
    args = [x, mod]
    for name in _PROJ_WEIGHTS:
        arr = win_parts.get(name)
        in_specs.append(const(arr) if arr is not None else _layer_spec(wts[name], layer))
        args.append(arr if arr is not None else wts[name])
    for name in ("wdft", "mean_q", "mean_k"):
        in_specs.append(const(consts[name]))
        args.append(consts[name])
    if rope:
        in_specs.append(pl.BlockSpec((tm, 6 * LANES), lambda i: (i % tps, 0)))
        args.append(consts["rope"])
    out_specs = [tok(MLA_PAD_WIDTH), tok(MLA_PAD_WIDTH), transposed(MIX_TILE), seq_major, seq_major,
                 tok(GQA_WIDTH), tok(GQA_KV_WIDTH), transposed(GQA_KV_WIDTH)]
    out_shape = [jax.ShapeDtypeStruct((n_tok, MLA_PAD_WIDTH), BF16),
                 jax.ShapeDtypeStruct((n_tok, MLA_PAD_WIDTH), BF16),
                 jax.ShapeDtypeStruct((n_seq, MIX_TILE, seq), BF16),
                 jax.ShapeDtypeStruct((seq, n_seq * FNET_WIDTH), BF16),
                 jax.ShapeDtypeStruct((seq, n_seq * FNET_WIDTH), BF16),
                 jax.ShapeDtypeStruct((n_tok, GQA_WIDTH), BF16),
                 jax.ShapeDtypeStruct((n_tok, GQA_KV_WIDTH), BF16),
                 jax.ShapeDtypeStruct((n_seq, GQA_KV_WIDTH, seq), BF16)]
    n_prev = 0
    if emit_cache:
        n_prev = prev_cache[0].shape[1] if prev_cache is not None else 0
        layers = lambda n, tail: pl.BlockSpec((spt, n) + tail, lambda i: (i, 0, 0, 0))
        tails = ((seq, MLA_KV_RANK), (MLA_ROPE_DIM, seq), (GQA_KV_WIDTH, seq), (GQA_KV_WIDTH, seq))
        if n_prev:
            in_specs += [layers(n_prev, t) for t in tails]
            args += list(prev_cache)
        out_specs += [layers(n_prev + 1, t) for t in tails]
        out_shape += [jax.ShapeDtypeStruct((n_seq, n_prev + 1) + t, F32) for t in tails]
    return pl.pallas_call(
        functools.partial(_proj_kernel, rope=rope, emit_cache=emit_cache, n_prev=n_prev, multi_seq=multi_seq),
        grid=(n_tiles,),
        in_specs=in_specs,
        out_specs=out_specs,
        out_shape=out_shape,
        compiler_params=pltpu.CompilerParams(dimension_semantics=("arbitrary",), vmem_limit_bytes=VMEM_LIMIT),
        name="proj_rope" if rope else "proj_ctx",
    )(*args)


def _ctx_kernel(ckv_ref, kr_ref, gk_ref, gv_ref, wk_ref, wv_ref, place_ref, win_ref,
                km_ref, vm_ref, gko_ref, gvo_ref, wa_ref, wc_ref, wkr_ref):
    w = win_ref[...]
    wa_ref[...] = w[:, :_IN_A].astype(BF16)
    wc_ref[...] = w[:, _IN_C0:].astype(BF16)
    kr_tile = w[:, _IN_A:_IN_A + LANES]
    rope_lanes = lax.broadcasted_iota(jnp.int32, kr_tile.shape, 1) < MLA_ROPE_DIM
    wkr_ref[...] = jnp.where(rope_lanes, kr_tile, 0.0).astype(BF16)

    cb = ckv_ref[...].astype(BF16)
    kn = _dot(cb, wk_ref[...])
    kr = _dot(kr_ref[...].astype(BF16), place_ref[...])
    for hd in range(MLA_HEADS):
        km_ref[:, hd * LANES:(hd + 1) * LANES] = (kn[:, hd * LANES:(hd + 1) * LANES] + kr).astype(BF16)
    vm_ref[...] = _dot(cb, wv_ref[...]).T.astype(BF16)
    gko_ref[...] = gk_ref[...].astype(BF16)
    gvo_ref[...] = gv_ref[...].T.astype(BF16)


def _ctx_prep(layer, cache_ckv, cache_krope, cache_gk, cache_gv, w_in, wts, consts):
    nb, _, past, _ = cache_ckv.shape
    rows = D_MODEL // nb
    cache = lambda w: pl.BlockSpec((None, None, past, w), lambda b: (b, layer, 0, 0))
    out = lambda w: pl.BlockSpec((None, past, w), lambda b: (b, 0, 0))
    out_t = lambda w: pl.BlockSpec((None, w, past), lambda b: (b, 0, 0))
    part = lambda w: pl.BlockSpec((rows, w), lambda b: (b, 0))
    part_widths = (_IN_A, _IN_C, LANES)
    return pl.pallas_call(
        _ctx_kernel,
        grid=(nb,),
        in_specs=[cache(MLA_KV_RANK), cache(MLA_ROPE_DIM), cache(GQA_KV_WIDTH), cache(GQA_KV_WIDTH),
                  _layer_spec(wts["wk"], layer), _layer_spec(wts["wv"], layer),
                  pl.BlockSpec((MLA_ROPE_DIM, LANES), lambda b: (0, 0)),
                  pl.BlockSpec((None, rows, w_in.shape[2]), lambda b: (layer, b, 0))],
        out_specs=[out(MLA_PAD_WIDTH), out_t(MIX_TILE), out(GQA_KV_WIDTH), out_t(GQA_KV_WIDTH)]
                  + [part(w) for w in part_widths],
        out_shape=[jax.ShapeDtypeStruct((nb, past, MLA_PAD_WIDTH), BF16),
                   jax.ShapeDtypeStruct((nb, MIX_TILE, past), BF16),
                   jax.ShapeDtypeStruct((nb, past, GQA_KV_WIDTH), BF16),
                   jax.ShapeDtypeStruct((nb, GQA_KV_WIDTH, past), BF16)]
                  + [jax.ShapeDtypeStruct((D_MODEL, w), BF16) for w in part_widths],
        compiler_params=pltpu.CompilerParams(dimension_semantics=("arbitrary",), vmem_limit_bytes=VMEM_LIMIT),
        name="ctx_prep",
    )(cache_ckv, cache_krope, cache_gk.reshape(nb, DEPTH, past, GQA_KV_WIDTH),
      cache_gv.reshape(nb, DEPTH, past, GQA_KV_WIDTH), wts["wk"], wts["wv"], consts["place"], w_in)


def _attn_kernel(*refs, n_seg, ahead, n_cast):
    n_in = 1 + 2 * n_seg
    outs_at = 2 * n_in + n_cast
    families = []
    for f, mla in enumerate((True, False)):
        ins = refs[f * n_in:(f + 1) * n_in]
        families.append((mla, ins[0], ins[1:1 + n_seg], ins[1 + n_seg:], refs[outs_at + f]))
    for c in range(n_cast):
        refs[outs_at + 2 + c][...] = refs[2 * n_in + c][...].astype(BF16)
    n_seqs, tq = refs[0].shape[:2]
    lower = _lower_half(tq)
    top = lax.broadcasted_iota(jnp.int32, (LANES, tq), 0) < HALF

    def scores_t(f, b, p, hd):
        mla, q_ref, k_refs, _, _ = families[f]
        if mla:
            cols = slice((2 * p + hd) * LANES, (2 * p + hd + 1) * LANES)
            qh = q_ref[b, :, cols]
            keys = [k_ref[b, :, cols] for k_ref in k_refs]
        else:
            q = q_ref[b, :, p * LANES:(p + 1) * LANES]
            qh = jnp.where(lower if hd == 0 else ~lower, q, jnp.zeros_like(q))
            keys = [k_ref[b] for k_ref in k_refs]
        return [lax.dot_general(k, qh, (((1,), (1,)), ((), ())), preferred_element_type=F32) for k in keys]

    def softmax_t(ss):
        m = functools.reduce(jnp.maximum, [jnp.max(s, axis=0, keepdims=True) for s in ss])
        es = [jnp.exp2(s - m) for s in ss]
        denom = functools.reduce(jnp.add, [jnp.sum(e, axis=0, keepdims=True) for e in es])
        return [e.astype(BF16) for e in es], denom

    def weighted_values_t(f, b, p, es, denom):
        mla, _, _, vt_refs, _ = families[f]
        rows = slice(p * LANES, (p + 1) * LANES) if mla else slice(None)
        acc = functools.reduce(jnp.add, [_dot(vt_ref[b, rows, :], e) for e, vt_ref in zip(es, vt_refs)])
        return acc * (1.0 / denom)

    work = [(f, b, p, hd) for f in range(2) for b in range(n_seqs) for p in range(HEAD_PAIRS) for hd in range(2)]
    pending = [scores_t(*w) for w in work[:ahead]]
    outs = {}
    for i, (f, b, p, hd) in enumerate(work):
        es, denom = softmax_t(pending.pop(0))
        if i + ahead < len(work):
            pending.append(scores_t(*work[i + ahead]))
        outs[f, b, p, hd] = weighted_values_t(f, b, p, es, denom)
        if (p, hd) != (HEAD_PAIRS - 1, 1):
            continue
        mla, o_ref = families[f][0], families[f][4]
        tiles = [jnp.where(top, outs[f, b, t, 0], outs[f, b, t, 1]).T for t in range(HEAD_PAIRS)]
        if not mla:
            t1 = pltpu.roll(tiles[1], HALF, 1)
            tiles = [jnp.where(lower, tiles[0], t1), jnp.where(lower, tiles[2], tiles[0]),
                     jnp.where(lower, t1, tiles[2])]
        for t in range(HEAD_PAIRS):
            o_ref[b, :, t * LANES:(t + 1) * LANES] = tiles[t].astype(o_ref.dtype)


def _attention(mla_q, mla_kv, gqa_q, gqa_kv, *, tq, bb=1, ahead=2, cast=(), layer=0):
    n_batch, sq = mla_q.shape[:2]
    n_seg = len(mla_kv)
    n_q = sq // tq
    steps = (n_batch // bb) * n_q
    whole = lambda a: pl.BlockSpec((bb,) + a.shape[1:], lambda b, i: (b, 0, 0))
    in_specs, args = [], []
    for q, kv in ((mla_q, mla_kv), (gqa_q, gqa_kv)):
        in_specs += [pl.BlockSpec((bb, tq, q.shape[2]), lambda b, i: (b, i, 0))]
        in_specs += [whole(k) for k, _ in kv] + [whole(v) for _, v in kv]
        args += [q] + [k for k, _ in kv] + [v for _, v in kv]
    out_spec = pl.BlockSpec((bb, tq, MIX_TILE), lambda b, i: (b, i, 0))
    out_specs = [out_spec, out_spec]
    out_shape = [jax.ShapeDtypeStruct((n_batch, sq, MIX_TILE), BF16)] * 2
    for w in cast:
        rows, cols = w.shape[1] // steps, w.shape[2]
        in_specs.append(pl.BlockSpec((None, rows, cols), lambda b, i: (layer, b * n_q + i, 0)))
        out_specs.append(pl.BlockSpec((rows, cols), lambda b, i: (b * n_q + i, 0)))
        out_shape.append(jax.ShapeDtypeStruct(w.shape[1:], BF16))
        args.append(w)
    return pl.pallas_call(
        functools.partial(_attn_kernel, n_seg=n_seg, ahead=ahead, n_cast=len(cast)),
        grid=(n_batch // bb, n_q),
        in_specs=in_specs,
        out_specs=out_specs,
        out_shape=out_shape,
        compiler_params=pltpu.CompilerParams(dimension_semantics=("arbitrary",) * 2, vmem_limit_bytes=VMEM_LIMIT),
        name=f"attn_s{n_seg}",
    )(*args)


def _fnet_kernel(ca_ref, sa_ref, cb_ref, sb_ref, uc_ref, us_ref, o_ref, tc_ref, ts_ref, *, scale):
    @pl.when(pl.program_id(1) == 0)
    def _build_twiddles():
        cb, sb = cb_ref[...], sb_ref[...]
        for a in range(ca_ref.shape[0]):
            ca, sa = ca_ref[a:a + 1, :], sa_ref[a:a + 1, :]
            rows = slice(a * DFT_SPLIT, (a + 1) * DFT_SPLIT)
            tc_ref[rows, :] = (ca * cb - sa * sb).astype(BF16)
            ts_ref[rows, :] = (sa * cb + ca * sb).astype(BF16)

    acc = _dot(tc_ref[...], uc_ref[...]) - _dot(ts_ref[...], us_ref[...])
    o_ref[...] = (acc * scale).astype(o_ref.dtype)


def _fnet(tables, uc, us, tm, tn):
    seq, width = uc.shape
    scale = float((seq * FNET_GROUP_DIM) ** -0.5)
    n_a = tm // DFT_SPLIT
    part_a = pl.BlockSpec((n_a, seq), lambda i, j: (i, 0))
    part_b = pl.BlockSpec((DFT_SPLIT, seq), lambda i, j: (0, 0))
    data = pl.BlockSpec((seq, tn), lambda i, j: (0, j))
    return pl.pallas_call(
        functools.partial(_fnet_kernel, scale=scale),
        grid=(seq // tm, width // tn),
        in_specs=[part_a, part_a, part_b, part_b, data, data],
        out_specs=pl.BlockSpec((tm, tn), lambda i, j: (i, j)),
        out_shape=jax.ShapeDtypeStruct((seq, width), BF16),
        scratch_shapes=[pltpu.VMEM((tm, seq), BF16), pltpu.VMEM((tm, seq), BF16)],
        compiler_params=pltpu.CompilerParams(dimension_semantics=("arbitrary", "arbitrary"),
                                             vmem_limit_bytes=VMEM_LIMIT),
        name="fnet",
    )(*tables, uc, us)


def _post_kernel(x_ref, mla_ref, fn_ref, gqa_ref, mod_ref, g2_ref, wo_ref, wg_ref, wu_ref, wd_ref, gf_ref,
                 o_ref, *, final):
    fn = fn_ref[...]
    if fn.shape[1] > FNET_WIDTH:
        fn = jnp.concatenate([fn[:, j:j + FNET_WIDTH] for j in range(0, fn.shape[1], FNET_WIDTH)], axis=0)
    mix = jnp.concatenate([mla_ref[...], fn, gqa_ref[...]], axis=-1)
    gate1 = mod_ref[:, 2 * D_MODEL:3 * D_MODEL]
    shift2 = mod_ref[:, 3 * D_MODEL:4 * D_MODEL]
    scale2 = mod_ref[:, 4 * D_MODEL:5 * D_MODEL]
    gate2 = mod_ref[:, 5 * D_MODEL:6 * D_MODEL]
    x = x_ref[...] + gate1 * _dot(mix, wo_ref[...])
    h = (_rms(x) * g2_ref[...] * (1.0 + scale2) + shift2).astype(BF16)
    g = _dot(h, wg_ref[...])
    u = _dot(h, wu_ref[...])
    a = (g * jax.nn.sigmoid(g) * u).astype(BF16)
    x = x + gate2 * _dot(a, wd_ref[...])
    if final:
        x = _rms(x) * gf_ref[...]
    o_ref[...] = x


def _post(x, mla_o, fnet_o, gqa_o, mod, g2, ffn, layer, g_final, *, seq, final, tm):
    n_tok = x.shape[0]
    n_tiles = n_tok // tm
    tiles_per_mod = n_tiles // mod.shape[0]
    resident = lambda arr: pl.BlockSpec(arr.shape, lambda i: (0, 0), pipeline_mode=pl.Buffered(1))
    tok = lambda w: pl.BlockSpec((tm, w), lambda i: (i, 0))
    if tm <= seq:
        tps = seq // tm
        fnet_spec = pl.BlockSpec((tm, FNET_WIDTH), lambda i: (i % tps, i // tps))
    else:
        fnet_spec = pl.BlockSpec((seq, (tm // seq) * FNET_WIDTH), lambda i: (0, i))
    return pl.pallas_call(
        functools.partial(_post_kernel, final=final),
        grid=(n_tiles,),
        in_specs=[tok(D_MODEL), tok(MIX_TILE), fnet_spec, tok(MIX_TILE),
                  pl.BlockSpec((None, 1, 6 * D_MODEL), lambda i: (i // tiles_per_mod, 0, 0)),
                  _layer_spec(g2, layer)]
                 + [resident(w) for w in ffn]
                 + [pl.BlockSpec((1, D_MODEL), lambda i: (0, 0))],
        out_specs=tok(D_MODEL),
        out_shape=jax.ShapeDtypeStruct((n_tok, D_MODEL), F32),
        compiler_params=pltpu.CompilerParams(dimension_semantics=("arbitrary",), vmem_limit_bytes=VMEM_LIMIT),
        name="post_final" if final else "post",
    )(x, mla_o, fnet_o, gqa_o, mod, g2, *ffn, g_final)


def _prep_weights(g_norm1, g_norm2, g_q_a, w_q_up, g_kv_a, w_kv_up, g_q_head, g_k_head):
    row = lambda g: g.reshape(DEPTH, 1, -1)
    wq = w_q_up.reshape(DEPTH, MLA_Q_RANK, MLA_HEADS, MLA_QK_DIM)
    wq = jnp.concatenate([wq[..., MLA_NOPE_DIM:], wq[..., :MLA_NOPE_DIM]], axis=-1)
    wq = jnp.pad(wq, ((0, 0), (0, 0), (0, 0), (0, LANES - MLA_QK_DIM)))
    wkv = w_kv_up.reshape(DEPTH, MLA_KV_RANK, MLA_HEADS, MLA_NOPE_DIM + MLA_V_DIM)
    wk = jnp.pad(wkv[..., :MLA_NOPE_DIM], ((0, 0), (0, 0), (0, 0), (MLA_ROPE_DIM, LANES - MLA_QK_DIM)))
    wv = wkv[..., MLA_NOPE_DIM:]
    return {
        "g1": row(g_norm1), "g2": row(g_norm2), "g_q_a": row(g_q_a), "g_kv_a": row(g_kv_a),
        "g_q_head": row(jnp.tile(g_q_head, (1, GQA_HEADS))), "g_k_head": row(jnp.tile(g_k_head, (1, GQA_KV_HEADS))),
        "wq": wq.reshape(DEPTH, MLA_Q_RANK, MLA_PAD_WIDTH).astype(BF16),
        "wk": wk.reshape(DEPTH, MLA_KV_RANK, MLA_PAD_WIDTH).astype(BF16),
        "wv": wv.reshape(DEPTH, MLA_KV_RANK, MIX_TILE).astype(BF16),
    }


def kernel(x_prompt, x_sample, cache_mla_ckv, cache_mla_krope, cache_gqa_k, cache_gqa_v, c, c_ctx, w_ada, b_ada,
           g_norm1, g_norm2, w_in, g_q_a, w_q_up, g_kv_a, w_kv_up, g_q_head, g_k_head, w_out,
           w_ffn_gate, w_ffn_up, w_ffn_down, g_final):
    n_pb, p_seq, _ = x_prompt.shape
    n_sb, s_seq, _ = x_sample.shape
    consts = {"wdft": _channel_dft(), "mean_q": _group_mean_matrix(GQA_HEADS),
              "mean_k": _group_mean_matrix(GQA_KV_HEADS), "place": _rope_place_matrix(),
              "rope": _rope_tables(s_seq)}
    dft_p = _seq_dft_tables(p_seq)
    dft_s = _seq_dft_tables(s_seq)
    gf = g_final.reshape(1, D_MODEL)
    wts = _prep_weights(g_norm1, g_norm2, g_q_a, w_q_up, g_kv_a, w_kv_up, g_q_head, g_k_head)

    cond_t = jnp.concatenate([c_ctx[None, :], c], axis=0).T
    mod = _ada(cond_t, w_ada, b_ada)

    xp = x_prompt.reshape(n_pb * p_seq, D_MODEL)
    xs = x_sample.reshape(n_sb * s_seq, D_MODEL)
    b3 = lambda a, nb: a.reshape(nb, a.shape[0] // nb, a.shape[1])
    caches = None
    for l in range(DEPTH):
        final = l == DEPTH - 1
        mod_p = mod[l, 0:1].reshape(1, 1, 6 * D_MODEL)
        mod_s = mod[l, 1:].reshape(n_sb, 1, 6 * D_MODEL)

        km_c, vm_c, gk_c, gv_c, w_a, w_c, w_kr = _ctx_prep(
            l, cache_mla_ckv, cache_mla_krope, cache_gqa_k, cache_gqa_v, w_in, wts, consts)
        win_parts = {"w_a": w_a, "w_c": w_c, "w_kr": w_kr}
        qm, km, vm, uc_s, us_s, gq, gk, gv = _proj(xs, mod_s, wts, win_parts, l, consts, seq=s_seq, rope=True)
        mla_s, gqa_s, *ffn = _attention(b3(qm, n_sb), [(km_c, vm_c), (b3(km, n_sb), vm)],
                                        b3(gq, n_sb), [(gk_c, gv_c), (b3(gk, n_sb), gv)], tq=512,
                                        cast=(w_out, w_ffn_gate, w_ffn_up, w_ffn_down), layer=l)

        qm, km, vm, uc, us, gq, gk, gv, *caches = _proj(
            xp, mod_p, wts, win_parts, l, consts, seq=p_seq, rope=False, emit_cache=True, prev_cache=caches)
        mla_o, gqa_o = _attention(b3(qm, n_pb), [(b3(km, n_pb), vm)], b3(gq, n_pb), [(b3(gk, n_pb), gv)],
                                  tq=p_seq, bb=4, ahead=8)
        fn_o = _fnet(dft_p, uc, us, tm=p_seq, tn=1024)
        xp = _post(xp, mla_o.reshape(-1, MIX_TILE), fn_o, gqa_o.reshape(-1, MIX_TILE), mod_p, wts["g2"], ffn, l, gf,
                   seq=p_seq, final=final, tm=POST_TILE)

        fn_o = _fnet(dft_s, uc_s, us_s, tm=512, tn=n_sb * FNET_WIDTH)
        xs = _post(xs, mla_s.reshape(-1, MIX_TILE), fn_o, gqa_s.reshape(-1, MIX_TILE), mod_s, wts["g2"], ffn, l, gf,
                   seq=s_seq, final=final, tm=POST_TILE)

    ckv_new, krope_t, gk_t, gv_t = caches
    heads = lambda a: jnp.swapaxes(a, 2, 3).reshape(n_pb, DEPTH, p_seq, GQA_KV_HEADS, GQA_HEAD_DIM)
    return (xp.reshape(n_pb, p_seq, D_MODEL), xs.reshape(n_sb, s_seq, D_MODEL),
            ckv_new, jnp.swapaxes(krope_t, 2, 3), heads(gk_t), heads(gv_t))
```

```python
import functools

import numpy as np
import jax
import jax.numpy as jnp
from jax import lax
from jax.experimental import pallas as pl
from jax.experimental.pallas import tpu as pltpu

D_MODEL = 1024
DEPTH = 2
GRID_W = 64
MLA_HEADS = 6
MLA_Q_RANK = 384
MLA_KV_RANK = 256
MLA_NOPE_DIM = 64
MLA_ROPE_DIM = 32
MLA_V_DIM = 64
MLA_QK_DIM = MLA_NOPE_DIM + MLA_ROPE_DIM
FNET_GROUPS = 4
FNET_GROUP_DIM = 64
FNET_WIDTH = FNET_GROUPS * FNET_GROUP_DIM
GQA_HEADS = 6
GQA_KV_HEADS = 2
GQA_HEAD_DIM = 64
GQA_GROUP = GQA_HEADS // GQA_KV_HEADS
GQA_WIDTH = GQA_HEADS * GQA_HEAD_DIM
GQA_KV_WIDTH = GQA_KV_HEADS * GQA_HEAD_DIM
D_FF = 2816
ROPE_THETA = 10000.0
EPS = 1e-6
LOG2_E = 1.4426950408889634

LANES = 128
HALF = LANES // 2
HEAD_PAIRS = MLA_HEADS // 2
MLA_PAD_WIDTH = MLA_HEADS * LANES
MIX_TILE = HEAD_PAIRS * LANES
VMEM_LIMIT = 52 * 1024 * 1024
POST_TILE = 512
PROJ_SUB = 256

_IN_A = MLA_Q_RANK + MLA_KV_RANK
_IN_C0 = _IN_A + MLA_ROPE_DIM
_IN_C = FNET_WIDTH + GQA_WIDTH + 2 * GQA_KV_WIDTH
_C_U, _C_GQ, _C_GK, _C_GV = 0, FNET_WIDTH, FNET_WIDTH + GQA_WIDTH, FNET_WIDTH + GQA_WIDTH + GQA_KV_WIDTH

BF16 = jnp.bfloat16
F32 = jnp.float32


def _rope_tables(n_tokens):
    t = np.arange(n_tokens)
    row = (t // GRID_W).astype(np.float64)
    col = (t % GRID_W).astype(np.float64)

    def angles(rot_dim):
        n_axis = rot_dim // 4
        inv = ROPE_THETA ** (-np.arange(n_axis, dtype=np.float64) / n_axis)
        return np.concatenate([row[:, None] * inv, col[:, None] * inv], axis=-1)

    def tables(ang, lane_to_pair, is_first, is_second):
        cos = np.where((is_first | is_second)[None, :], np.cos(ang)[:, lane_to_pair], 1.0)
        sin = np.sin(ang)[:, lane_to_pair]
        sin_a = np.where(is_second[None, :], sin, 0.0)
        sin_b = np.where(is_first[None, :], -sin, 0.0)
        return [cos, sin_a, sin_b]

    lane = np.arange(LANES)
    half_m = MLA_ROPE_DIM // 2
    in_rope = lane < MLA_ROPE_DIM
    first_m = lane < half_m
    second_m = in_rope & ~first_m
    pair_m = lane % half_m
    half_g = GQA_HEAD_DIM // 2
    first_g = (lane % GQA_HEAD_DIM) < half_g
    pair_g = lane % half_g
    tabs = (tables(angles(MLA_ROPE_DIM), pair_m, first_m, second_m)
            + tables(angles(GQA_HEAD_DIM), pair_g, first_g, ~first_g))
    return jnp.asarray(np.concatenate(tabs, axis=-1), dtype=F32)


def _channel_dft():
    c = np.arange(FNET_GROUP_DIM)
    ang = 2.0 * np.pi * np.outer(c, c) / FNET_GROUP_DIM
    eye = np.eye(FNET_GROUPS)
    table = np.concatenate([np.kron(eye, np.cos(ang)), np.kron(eye, np.sin(ang))], axis=1)
    return jnp.asarray(table, dtype=F32).astype(BF16)


def _group_mean_matrix(n_heads):
    return jnp.asarray(np.kron(np.eye(n_heads), np.full((GQA_HEAD_DIM, GQA_HEAD_DIM), 1.0 / GQA_HEAD_DIM)), dtype=BF16)


def _rope_place_matrix():
    return jnp.asarray(np.eye(MLA_ROPE_DIM, LANES), dtype=BF16)


DFT_SPLIT = 32


def _seq_dft_tables(seq):
    s = np.arange(seq)
    n_a = seq // DFT_SPLIT
    ang_a = 2.0 * np.pi * ((np.arange(n_a)[:, None] * s[None, :]) % n_a) / n_a
    ang_b = 2.0 * np.pi * ((np.arange(DFT_SPLIT)[:, None] * s[None, :]) % seq) / seq
    return tuple(jnp.asarray(t, dtype=F32) for t in (np.cos(ang_a), np.sin(ang_a), np.cos(ang_b), np.sin(ang_b)))


def _rms(x):
    return x * lax.rsqrt(jnp.mean(x * x, axis=-1, keepdims=True) + EPS)


def _dot(a, b):
    return jnp.dot(a, b, preferred_element_type=F32)


def _group_rms(x, mean_mat):
    ms = _dot((x * x).astype(BF16), mean_mat)
    return x * lax.rsqrt(ms + EPS)


def _rotate(x, cos, sin_a, sin_b, half):
    return x * cos + pltpu.roll(x, half, 1) * sin_a + pltpu.roll(x, LANES - half, 1) * sin_b


def _lower_half(rows):
    return lax.broadcasted_iota(jnp.int32, (rows, LANES), 1) < HALF


def _layer_spec(arr, layer):
    return pl.BlockSpec((None,) + arr.shape[1:], lambda i: (layer, 0, 0))


def _ada_kernel(ct_ref, w_ref, b_ref, o_ref):
    s = ct_ref[...]
    s = s * jax.nn.sigmoid(s)
    w = w_ref[...]
    for m in range(o_ref.shape[0]):
        o_ref[m:m + 1, :] = jnp.sum(w * s[:, m:m + 1], axis=0, keepdims=True) + b_ref[...]


def _ada(cond_t, w_ada, b_ada, tn=2048):
    n_cond = cond_t.shape[1]
    width = w_ada.shape[2]
    return pl.pallas_call(
        _ada_kernel,
        grid=(DEPTH, width // tn),
        in_specs=[
            pl.BlockSpec((D_MODEL, n_cond), lambda l, j: (0, 0)),
            pl.BlockSpec((None, D_MODEL, tn), lambda l, j: (l, 0, j)),
            pl.BlockSpec((None, 1, tn), lambda l, j: (l, 0, j)),
        ],
        out_specs=pl.BlockSpec((None, n_cond, tn), lambda l, j: (l, 0, j)),
        out_shape=jax.ShapeDtypeStruct((DEPTH, n_cond, width), F32),
        compiler_params=pltpu.CompilerParams(dimension_semantics=("arbitrary", "arbitrary"),
                                             vmem_limit_bytes=VMEM_LIMIT),
        name="ada",
    )(cond_t, w_ada, b_ada.reshape(DEPTH, 1, width))


_PROJ_WEIGHTS = ("g1", "w_a", "w_c", "w_kr", "g_q_a", "wq", "g_kv_a", "wk", "wv", "g_q_head", "g_k_head")


def _proj_kernel(*refs, rope, emit_cache, n_prev, multi_seq):
    (x_ref, mod_ref, g1_ref, wa_ref, wc_ref, wkr_ref, gqa_ref, wq_ref, gkva_ref, wk_ref, wv_ref,
     gqh_ref, gkh_ref, wdft_ref, mq_ref, mk_ref) = refs[:16]
    refs = refs[16:]
    if rope:
        tab_ref, refs = refs[0], refs[1:]
    if n_prev:
        prev_refs, refs = refs[:4], refs[4:]
    qm_ref, km_ref, vm_ref, uc_ref, us_ref, gq_ref, gk_ref, gv_ref = refs[:8]
    cache_refs = refs[8:]

    shift = mod_ref[:, 0:D_MODEL]
    scale = mod_ref[:, D_MODEL:2 * D_MODEL]
    lower = _lower_half(PROJ_SUB)

    for r in range(x_ref.shape[0] // PROJ_SUB):
        rows = slice(r * PROJ_SUB, (r + 1) * PROJ_SUB)
        seq_cols = (slice(None), slice(r * FNET_WIDTH, (r + 1) * FNET_WIDTH)) if multi_seq else (rows, slice(None))
        t_idx = (r,) if multi_seq else (slice(None), rows)

        hb = (_rms(x_ref[rows, :]) * g1_ref[...] * (1.0 + scale) + shift).astype(BF16)
        pa = _dot(hb, wa_ref[...])
        pc = _dot(hb, wc_ref[...])
        kr = _dot(hb, wkr_ref[...])

        if rope:
            cos_m, sa_m, sb_m = (tab_ref[rows, i * LANES:(i + 1) * LANES] for i in range(3))
            cos_g, sa_g, sb_g = (tab_ref[rows, i * LANES:(i + 1) * LANES] for i in range(3, 6))

        gqn = _group_rms(pc[:, _C_GQ:_C_GQ + GQA_WIDTH], mq_ref[...]) * gqh_ref[...]
        tiles = []
        for p in range(HEAD_PAIRS):
            gp = gqn[:, p * LANES:(p + 1) * LANES]
            if rope:
                gp = _rotate(gp, cos_g, sa_g, sb_g, GQA_HEAD_DIM // 2)
            tiles.append(gp * (LOG2_E * GQA_HEAD_DIM ** -0.5))
        gq_ref[rows, 0:LANES] = jnp.where(lower, tiles[0], tiles[1]).astype(BF16)
        gq_ref[rows, LANES:2 * LANES] = jnp.where(
            lower, pltpu.roll(tiles[0], HALF, 1), pltpu.roll(tiles[2], HALF, 1)).astype(BF16)
        gq_ref[rows, 2 * LANES:] = jnp.where(lower, tiles[1], tiles[2]).astype(BF16)
        gkn = _group_rms(pc[:, _C_GK:_C_GK + GQA_KV_WIDTH], mk_ref[...]) * gkh_ref[...]
        gkr = _rotate(gkn, cos_g, sa_g, sb_g, GQA_HEAD_DIM // 2) if rope else gkn
        gv_t = pc[:, _C_GV:_C_GV + GQA_KV_WIDTH].T
        gk_ref[rows, :] = gkr.astype(BF16)
        gv_ref[t_idx] = gv_t.astype(BF16)

        cqn = _rms(pa[:, :MLA_Q_RANK]) * gqa_ref[...]
        q = _dot(cqn.astype(BF16), wq_ref[...])
        for hd in range(MLA_HEADS):
            qh = q[:, hd * LANES:(hd + 1) * LANES]
            if rope:
                qh = _rotate(qh, cos_m, sa_m, sb_m, MLA_ROPE_DIM // 2)
            qm_ref[rows, hd * LANES:(hd + 1) * LANES] = (qh * (LOG2_E * MLA_QK_DIM ** -0.5)).astype(BF16)

        ckvn = _rms(pa[:, MLA_Q_RANK:]) * gkva_ref[...]
        cb = ckvn.astype(BF16)
        kn = _dot(cb, wk_ref[...])
        krr = _rotate(kr, cos_m, sa_m, sb_m, MLA_ROPE_DIM // 2) if rope else kr
        for hd in range(MLA_HEADS):
            km_ref[rows, hd * LANES:(hd + 1) * LANES] = (kn[:, hd * LANES:(hd + 1) * LANES] + krr).astype(BF16)
        vm_ref[t_idx] = _dot(cb, wv_ref[...]).T.astype(BF16)

        ucs = _dot(pc[:, _C_U:_C_U + FNET_WIDTH].astype(BF16), wdft_ref[...])
        uc_ref[seq_cols] = ucs[:, :FNET_WIDTH].astype(BF16)
        us_ref[seq_cols] = ucs[:, FNET_WIDTH:].astype(BF16)

        if emit_cache:
            new = (ckvn, kr.T[:MLA_ROPE_DIM, :], gkn.T, gv_t)
            for i, out_ref in enumerate(cache_refs):
                if n_prev:
                    out_ref[r, :n_prev] = prev_refs[i][r]
                out_ref[r, n_prev] = new[i]


def _proj(x, mod, wts, win_parts, layer, consts, *, seq, rope, prev_cache=None, emit_cache=False, tm=512):
    n_tok = x.shape[0]
    n_tiles = n_tok // tm
    n_seq = n_tok // seq
    tiles_per_mod = n_tiles // mod.shape[0]
    multi_seq = tm > seq
    if multi_seq:
        assert seq == PROJ_SUB and not rope
        spt = tm // seq
        seq_major = pl.BlockSpec((seq, spt * FNET_WIDTH), lambda i: (0, i))
        transposed = lambda w: pl.BlockSpec((spt, w, seq), lambda i: (i, 0, 0))
    else:
        assert not emit_cache
        tps = seq // tm
        seq_major = pl.BlockSpec((tm, FNET_WIDTH), lambda i: (i % tps, i // tps))
        transposed = lambda w: pl.BlockSpec((None, w, tm), lambda i: (i // tps, 0, i % tps))
    const = lambda arr: pl.BlockSpec(arr.shape, lambda i: (0,) * arr.ndim)
    tok = lambda w: pl.BlockSpec((tm, w), lambda i: (i, 0))
    in_specs = [tok(D_MODEL), pl.BlockSpec((None, 1, 6 * D_MODEL), lambda i: (i // tiles_per_mod, 0, 0))]
    args = [x, mod]
    for name in _PROJ_WEIGHTS:
        arr = win_parts.get(name)
        in_specs.append(const(arr) if arr is not None else _layer_spec(wts[name], layer))
        args.append(arr if arr is not None else wts[name])
    for name in ("wdft", "mean_q", "mean_k"):
        in_specs.append(const(consts[name]))
        args.append(consts[name])
    if rope:
        in_specs.append(pl.BlockSpec((tm, 6 * LANES), lambda i: (i % tps, 0)))
        args.append(consts["rope"])
    out_specs = [tok(MLA_PAD_WIDTH), tok(MLA_PAD_WIDTH), transposed(MIX_TILE), seq_major, seq_major,
                 tok(GQA_WIDTH), tok(GQA_KV_WIDTH), transposed(GQA_KV_WIDTH)]
    out_shape = [jax.ShapeDtypeStruct((n_tok, MLA_PAD_WIDTH), BF16),
                 jax.ShapeDtypeStruct((n_tok, MLA_PAD_WIDTH), BF16),
                 jax.ShapeDtypeStruct((n_seq, MIX_TILE, seq), BF16),
                 jax.ShapeDtypeStruct((seq, n_seq * FNET_WIDTH), BF16),
                 jax.ShapeDtypeStruct((seq, n_seq * FNET_WIDTH), BF16),
                 jax.ShapeDtypeStruct((n_tok, GQA_WIDTH), BF16),
                 jax.ShapeDtypeStruct((n_tok, GQA_KV_WIDTH), BF16),
                 jax.ShapeDtypeStruct((n_seq, GQA_KV_WIDTH, seq), BF16)]
    n_prev = 0
    if emit_cache:
        n_prev = prev_cache[0].shape[1] if prev_cache is not None else 0
        layers = lambda n, tail: pl.BlockSpec((spt, n) + tail, lambda i: (i, 0, 0, 0))
        tails = ((seq, MLA_KV_RANK), (MLA_ROPE_DIM, seq), (GQA_KV_WIDTH, seq), (GQA_KV_WIDTH, seq))
        if n_prev:
            in_specs += [layers(n_prev, t) for t in tails]
            args += list(prev_cache)
        out_specs += [layers(n_prev + 1, t) for t in tails]
        out_shape += [jax.ShapeDtypeStruct((n_seq, n_prev + 1) + t, F32) for t in tails]
    return pl.pallas_call(
        functools.partial(_proj_kernel, rope=rope, emit_cache=emit_cache, n_prev=n_prev, multi_seq=multi_seq),
        grid=(n_tiles,),
        in_specs=in_specs,
        out_specs=out_specs,
        out_shape=out_shape,
        compiler_params=pltpu.CompilerParams(dimension_semantics=("arbitrary",), vmem_limit_bytes=VMEM_LIMIT),
        name="proj_rope" if rope else "proj_ctx",
    )(*args)


def _ctx_kernel(ckv_ref, kr_ref, gk_ref, gv_ref, wk_ref, wv_ref, place_ref, win_ref,
                km_ref, vm_ref, gko_ref, gvo_ref, wa_ref, wc_ref, wkr_ref):
    wt = win_ref[...]
    wa_ref[...] = wt[:_IN_A, :].T.astype(BF16)
    wc_ref[...] = wt[_IN_C0:, :].T.astype(BF16)
    kr_rows = jnp.concatenate([wt[_IN_A:_IN_C0, :], jnp.zeros((LANES - MLA_ROPE_DIM, wt.shape[1]), F32)], axis=0)
    wkr_ref[...] = kr_rows.T.astype(BF16)

    cb = ckv_ref[...].astype(BF16)
    kn = _dot(cb, wk_ref[...])
    kr = _dot(kr_ref[...].astype(BF16), place_ref[...])
    for hd in range(MLA_HEADS):
        km_ref[:, hd * LANES:(hd + 1) * LANES] = (kn[:, hd * LANES:(hd + 1) * LANES] + kr).astype(BF16)
    vm_ref[...] = _dot(cb, wv_ref[...]).T.astype(BF16)
    gko_ref[...] = gk_ref[...].astype(BF16)
    gvo_ref[...] = gv_ref[...].T.astype(BF16)


def _ctx_prep(layer, cache_ckv, cache_krope, cache_gk, cache_gv, w_in_t, wts, consts):
    nb, _, past, _ = cache_ckv.shape
    rows = D_MODEL // nb
    cache = lambda w: pl.BlockSpec((None, None, past, w), lambda b: (b, layer, 0, 0))
    out = lambda w: pl.BlockSpec((None, past, w), lambda b: (b, 0, 0))
    out_t = lambda w: pl.BlockSpec((None, w, past), lambda b: (b, 0, 0))
    part = lambda w: pl.BlockSpec((rows, w), lambda b: (b, 0))
    part_widths = (_IN_A, _IN_C, LANES)
    return pl.pallas_call(
        _ctx_kernel,
        grid=(nb,),
        in_specs=[cache(MLA_KV_RANK), cache(MLA_ROPE_DIM), cache(GQA_KV_WIDTH), cache(GQA_KV_WIDTH),
                  _layer_spec(wts["wk"], layer), _layer_spec(wts["wv"], layer),
                  pl.BlockSpec((MLA_ROPE_DIM, LANES), lambda b: (0, 0)),
                  pl.BlockSpec((None, w_in_t.shape[1], rows), lambda b: (layer, 0, b))],
        out_specs=[out(MLA_PAD_WIDTH), out_t(MIX_TILE), out(GQA_KV_WIDTH), out_t(GQA_KV_WIDTH)]
                  + [part(w) for w in part_widths],
        out_shape=[jax.ShapeDtypeStruct((nb, past, MLA_PAD_WIDTH), BF16),
                   jax.ShapeDtypeStruct((nb, MIX_TILE, past), BF16),
                   jax.ShapeDtypeStruct((nb, past, GQA_KV_WIDTH), BF16),
                   jax.ShapeDtypeStruct((nb, GQA_KV_WIDTH, past), BF16)]
                  + [jax.ShapeDtypeStruct((D_MODEL, w), BF16) for w in part_widths],
        compiler_params=pltpu.CompilerParams(dimension_semantics=("arbitrary",), vmem_limit_bytes=VMEM_LIMIT),
        name="ctx_prep",
    )(cache_ckv, cache_krope, cache_gk.reshape(nb, DEPTH, past, GQA_KV_WIDTH),
      cache_gv.reshape(nb, DEPTH, past, GQA_KV_WIDTH), wts["wk"], wts["wv"], consts["place"], w_in_t)


def _attn_kernel(*refs, n_seg, ahead, n_cast):
    n_in = 1 + 2 * n_seg
    outs_at = 2 * n_in + n_cast
    families = []
    for f, mla in enumerate((True, False)):
        ins = refs[f * n_in:(f + 1) * n_in]
        families.append((mla, ins[0], ins[1:1 + n_seg], ins[1 + n_seg:], refs[outs_at + f]))
    for c in range(n_cast):
        refs[outs_at + 2 + c][...] = refs[2 * n_in + c][...].astype(BF16)
    n_seqs, tq = refs[0].shape[:2]
    lower = _lower_half(tq)
    top = lax.broadcasted_iota(jnp.int32, (LANES, tq), 0) < HALF

    def scores_t(f, b, p, hd):
        mla, q_ref, k_refs, _, _ = families[f]
        if mla:
            cols = slice((2 * p + hd) * LANES, (2 * p + hd + 1) * LANES)
            qh = q_ref[b, :, cols]
            keys = [k_ref[b, :, cols] for k_ref in k_refs]
        else:
            q = q_ref[b, :, p * LANES:(p + 1) * LANES]
            qh = jnp.where(lower if hd == 0 else ~lower, q, jnp.zeros_like(q))
            keys = [k_ref[b] for k_ref in k_refs]
        return [lax.dot_general(k, qh, (((1,), (1,)), ((), ())), preferred_element_type=F32) for k in keys]

    def softmax_t(ss):
        m = functools.reduce(jnp.maximum, [jnp.max(s, axis=0, keepdims=True) for s in ss])
        es = [jnp.exp2(s - m) for s in ss]
        denom = functools.reduce(jnp.add, [jnp.sum(e, axis=0, keepdims=True) for e in es])
        return [e.astype(BF16) for e in es], denom

    def weighted_values_t(f, b, p, es, denom):
        mla, _, _, vt_refs, _ = families[f]
        rows = slice(p * LANES, (p + 1) * LANES) if mla else slice(None)
        acc = functools.reduce(jnp.add, [_dot(vt_ref[b, rows, :], e) for e, vt_ref in zip(es, vt_refs)])
        return acc * (1.0 / denom)

    work = [(f, b, p, hd) for f in range(2) for b in range(n_seqs) for p in range(HEAD_PAIRS) for hd in range(2)]
    pending = [scores_t(*w) for w in work[:ahead]]
    outs = {}
    for i, (f, b, p, hd) in enumerate(work):
        es, denom = softmax_t(pending.pop(0))
        if i + ahead < len(work):
            pending.append(scores_t(*work[i + ahead]))
        outs[f, b, p, hd] = weighted_values_t(f, b, p, es, denom)
        if (p, hd) != (HEAD_PAIRS - 1, 1):
            continue
        mla, o_ref = families[f][0], families[f][4]
        tiles = [jnp.where(top, outs[f, b, t, 0], outs[f, b, t, 1]).T for t in range(HEAD_PAIRS)]
        if not mla:
            t1 = pltpu.roll(tiles[1], HALF, 1)
            tiles = [jnp.where(lower, tiles[0], t1), jnp.where(lower, tiles[2], tiles[0]),
                     jnp.where(lower, t1, tiles[2])]
        for t in range(HEAD_PAIRS):
            o_ref[b, :, t * LANES:(t + 1) * LANES] = tiles[t].astype(o_ref.dtype)


def _attention(mla_q, mla_kv, gqa_q, gqa_kv, *, tq, bb=1, ahead=2, cast=(), layer=0):
    n_batch, sq = mla_q.shape[:2]
    n_seg = len(mla_kv)
    n_q = sq // tq
    steps = (n_batch // bb) * n_q
    whole = lambda a: pl.BlockSpec((bb,) + a.shape[1:], lambda b, i: (b, 0, 0))
    in_specs, args = [], []
    for q, kv in ((mla_q, mla_kv), (gqa_q, gqa_kv)):
        in_specs += [pl.BlockSpec((bb, tq, q.shape[2]), lambda b, i: (b, i, 0))]
        in_specs += [whole(k) for k, _ in kv] + [whole(v) for _, v in kv]
        args += [q] + [k for k, _ in kv] + [v for _, v in kv]
    out_spec = pl.BlockSpec((bb, tq, MIX_TILE), lambda b, i: (b, i, 0))
    out_specs = [out_spec, out_spec]
    out_shape = [jax.ShapeDtypeStruct((n_batch, sq, MIX_TILE), BF16)] * 2
    for w in cast:
        rows, cols = w.shape[1] // steps, w.shape[2]
        in_specs.append(pl.BlockSpec((None, rows, cols), lambda b, i: (layer, b * n_q + i, 0)))
        out_specs.append(pl.BlockSpec((rows, cols), lambda b, i: (b * n_q + i, 0)))
        out_shape.append(jax.ShapeDtypeStruct(w.shape[1:], BF16))
        args.append(w)
    return pl.pallas_call(
        functools.partial(_attn_kernel, n_seg=n_seg, ahead=ahead, n_cast=len(cast)),
        grid=(n_batch // bb, n_q),
        in_specs=in_specs,
        out_specs=out_specs,
        out_shape=out_shape,
        compiler_params=pltpu.CompilerParams(dimension_semantics=("arbitrary",) * 2, vmem_limit_bytes=VMEM_LIMIT),
        name=f"attn_s{n_seg}",
    )(*args)


def _fnet_kernel(ca_ref, sa_ref, cb_ref, sb_ref, uc_ref, us_ref, o_ref, tc_ref, ts_ref, *, scale):
    @pl.when(pl.program_id(1) == 0)
    def _build_twiddles():
        cb, sb = cb_ref[...], sb_ref[...]
        for a in range(ca_ref.shape[0]):
            ca, sa = ca_ref[a:a + 1, :], sa_ref[a:a + 1, :]
            rows = slice(a * DFT_SPLIT, (a + 1) * DFT_SPLIT)
            tc_ref[rows, :] = (ca * cb - sa * sb).astype(BF16)
            ts_ref[rows, :] = (sa * cb + ca * sb).astype(BF16)

    acc = _dot(tc_ref[...], uc_ref[...]) - _dot(ts_ref[...], us_ref[...])
    o_ref[...] = (acc * scale).astype(o_ref.dtype)


def _fnet(tables, uc, us, tm, tn):
    seq, width = uc.shape
    scale = float((seq * FNET_GROUP_DIM) ** -0.5)
    n_a = tm // DFT_SPLIT
    part_a = pl.BlockSpec((n_a, seq), lambda i, j: (i, 0))
    part_b = pl.BlockSpec((DFT_SPLIT, seq), lambda i, j: (0, 0))
    data = pl.BlockSpec((seq, tn), lambda i, j: (0, j))
    return pl.pallas_call(
        functools.partial(_fnet_kernel, scale=scale),
        grid=(seq // tm, width // tn),
        in_specs=[part_a, part_a, part_b, part_b, data, data],
        out_specs=pl.BlockSpec((tm, tn), lambda i, j: (i, j)),
        out_shape=jax.ShapeDtypeStruct((seq, width), BF16),
        scratch_shapes=[pltpu.VMEM((tm, seq), BF16), pltpu.VMEM((tm, seq), BF16)],
        compiler_params=pltpu.CompilerParams(dimension_semantics=("arbitrary", "arbitrary"),
                                             vmem_limit_bytes=VMEM_LIMIT),
        name="fnet",
    )(*tables, uc, us)


def _post_kernel(x_ref, mla_ref, fn_ref, gqa_ref, mod_ref, g2_ref, wo_ref, wg_ref, wu_ref, wd_ref, gf_ref,
                 o_ref, *, final):
    fn = fn_ref[...]
    if fn.shape[1] > FNET_WIDTH:
        fn = jnp.concatenate([fn[:, j:j + FNET_WIDTH] for j in range(0, fn.shape[1], FNET_WIDTH)], axis=0)
    mix = jnp.concatenate([mla_ref[...], fn, gqa_ref[...]], axis=-1)
    gate1 = mod_ref[:, 2 * D_MODEL:3 * D_MODEL]
    shift2 = mod_ref[:, 3 * D_MODEL:4 * D_MODEL]
    scale2 = mod_ref[:, 4 * D_MODEL:5 * D_MODEL]
    gate2 = mod_ref[:, 5 * D_MODEL:6 * D_MODEL]
    x = x_ref[...] + gate1 * _dot(mix, wo_ref[...])
    h = (_rms(x) * g2_ref[...] * (1.0 + scale2) + shift2).astype(BF16)
    g = _dot(h, wg_ref[...])
    u = _dot(h, wu_ref[...])
    a = (g * jax.nn.sigmoid(g) * u).astype(BF16)
    x = x + gate2 * _dot(a, wd_ref[...])
    if final:
        x = _rms(x) * gf_ref[...]
    o_ref[...] = x


def _post(x, mla_o, fnet_o, gqa_o, mod, g2, ffn, layer, g_final, *, seq, final, tm):
    n_tok = x.shape[0]
    n_tiles = n_tok // tm
    tiles_per_mod = n_tiles // mod.shape[0]
    resident = lambda arr: pl.BlockSpec(arr.shape, lambda i: (0, 0), pipeline_mode=pl.Buffered(1))
    tok = lambda w: pl.BlockSpec((tm, w), lambda i: (i, 0))
    if tm <= seq:
        tps = seq // tm
        fnet_spec = pl.BlockSpec((tm, FNET_WIDTH), lambda i: (i % tps, i // tps))
    else:
        fnet_spec = pl.BlockSpec((seq, (tm // seq) * FNET_WIDTH), lambda i: (0, i))
    return pl.pallas_call(
        functools.partial(_post_kernel, final=final),
        grid=(n_tiles,),
        in_specs=[tok(D_MODEL), tok(MIX_TILE), fnet_spec, tok(MIX_TILE),
                  pl.BlockSpec((None, 1, 6 * D_MODEL), lambda i: (i // tiles_per_mod, 0, 0)),
                  _layer_spec(g2, layer)]
                 + [resident(w) for w in ffn]
                 + [pl.BlockSpec((1, D_MODEL), lambda i: (0, 0))],
        out_specs=tok(D_MODEL),
        out_shape=jax.ShapeDtypeStruct((n_tok, D_MODEL), F32),
        compiler_params=pltpu.CompilerParams(dimension_semantics=("arbitrary",), vmem_limit_bytes=VMEM_LIMIT),
        name="post_final" if final else "post",
    )(x, mla_o, fnet_o, gqa_o, mod, g2, *ffn, g_final)


def _prep_weights(g_norm1, g_norm2, g_q_a, w_q_up, g_kv_a, w_kv_up, g_q_head, g_k_head):
    row = lambda g: g.reshape(DEPTH, 1, -1)
    wq = w_q_up.reshape(DEPTH, MLA_Q_RANK, MLA_HEADS, MLA_QK_DIM)
    wq = jnp.concatenate([wq[..., MLA_NOPE_DIM:], wq[..., :MLA_NOPE_DIM]], axis=-1)
    wq = jnp.pad(wq, ((0, 0), (0, 0), (0, 0), (0, LANES - MLA_QK_DIM)))
    wkv = w_kv_up.reshape(DEPTH, MLA_KV_RANK, MLA_HEADS, MLA_NOPE_DIM + MLA_V_DIM)
    wk = jnp.pad(wkv[..., :MLA_NOPE_DIM], ((0, 0), (0, 0), (0, 0), (MLA_ROPE_DIM, LANES - MLA_QK_DIM)))
    wv = wkv[..., MLA_NOPE_DIM:]
    return {
        "g1": row(g_norm1), "g2": row(g_norm2), "g_q_a": row(g_q_a), "g_kv_a": row(g_kv_a),
        "g_q_head": row(jnp.tile(g_q_head, (1, GQA_HEADS))), "g_k_head": row(jnp.tile(g_k_head, (1, GQA_KV_HEADS))),
        "wq": wq.reshape(DEPTH, MLA_Q_RANK, MLA_PAD_WIDTH).astype(BF16),
        "wk": wk.reshape(DEPTH, MLA_KV_RANK, MLA_PAD_WIDTH).astype(BF16),
        "wv": wv.reshape(DEPTH, MLA_KV_RANK, MIX_TILE).astype(BF16),
    }


def kernel(x_prompt, x_sample, cache_mla_ckv, cache_mla_krope, cache_gqa_k, cache_gqa_v, c, c_ctx, w_ada, b_ada,
           g_norm1, g_norm2, w_in, g_q_a, w_q_up, g_kv_a, w_kv_up, g_q_head, g_k_head, w_out,
           w_ffn_gate, w_ffn_up, w_ffn_down, g_final):
    n_pb, p_seq, _ = x_prompt.shape
    n_sb, s_seq, _ = x_sample.shape
    consts = {"wdft": _channel_dft(), "mean_q": _group_mean_matrix(GQA_HEADS),
              "mean_k": _group_mean_matrix(GQA_KV_HEADS), "place": _rope_place_matrix(),
              "rope": _rope_tables(s_seq)}
    dft_p = _seq_dft_tables(p_seq)
    dft_s = _seq_dft_tables(s_seq)
    gf = g_final.reshape(1, D_MODEL)
    wts = _prep_weights(g_norm1, g_norm2, g_q_a, w_q_up, g_kv_a, w_kv_up, g_q_head, g_k_head)
    w_in_t = jnp.swapaxes(w_in, 1, 2)

    cond_t = jnp.concatenate([c_ctx[None, :], c], axis=0).T
    mod = _ada(cond_t, w_ada, b_ada)

    xp = x_prompt.reshape(n_pb * p_seq, D_MODEL)
    xs = x_sample.reshape(n_sb * s_seq, D_MODEL)
    b3 = lambda a, nb: a.reshape(nb, a.shape[0] // nb, a.shape[1])
    caches = None
    for l in range(DEPTH):
        final = l == DEPTH - 1
        mod_p = mod[l, 0:1].reshape(1, 1, 6 * D_MODEL)
        mod_s = mod[l, 1:].reshape(n_sb, 1, 6 * D_MODEL)

        km_c, vm_c, gk_c, gv_c, w_a, w_c, w_kr = _ctx_prep(
            l, cache_mla_ckv, cache_mla_krope, cache_gqa_k, cache_gqa_v, w_in_t, wts, consts)
        win_parts = {"w_a": w_a, "w_c": w_c, "w_kr": w_kr}
        qm, km, vm, uc_s, us_s, gq, gk, gv = _proj(xs, mod_s, wts, win_parts, l, consts, seq=s_seq, rope=True)
        mla_s, gqa_s, *ffn = _attention(b3(qm, n_sb), [(km_c, vm_c), (b3(km, n_sb), vm)],
                                        b3(gq, n_sb), [(gk_c, gv_c), (b3(gk, n_sb), gv)], tq=512,
                                        cast=(w_out, w_ffn_gate, w_ffn_up, w_ffn_down), layer=l)

        qm, km, vm, uc, us, gq, gk, gv, *caches = _proj(
            xp, mod_p, wts, win_parts, l, consts, seq=p_seq, rope=False, emit_cache=True, prev_cache=caches)
        mla_o, gqa_o = _attention(b3(qm, n_pb), [(b3(km, n_pb), vm)], b3(gq, n_pb), [(b3(gk, n_pb), gv)],
                                  tq=p_seq, bb=4, ahead=8)
        fn_o = _fnet(dft_p, uc, us, tm=p_seq, tn=1024)
        xp = _post(xp, mla_o.reshape(-1, MIX_TILE), fn_o, gqa_o.reshape(-1, MIX_TILE), mod_p, wts["g2"], ffn, l, gf,
                   seq=p_seq, final=final, tm=POST_TILE)

        fn_o = _fnet(dft_s, uc_s, us_s, tm=512, tn=n_sb * FNET_WIDTH)
        xs = _post(xs, mla_s.reshape(-1, MIX_TILE), fn_o, gqa_s.reshape(-1, MIX_TILE), mod_s, wts["g2"], ffn, l, gf,
                   seq=s_seq, final=final, tm=POST_TILE)

    ckv_new, krope_t, gk_t, gv_t = caches
    heads = lambda a: jnp.swapaxes(a, 2, 3).reshape(n_pb, DEPTH, p_seq, GQA_KV_HEADS, GQA_HEAD_DIM)
    return (xp.reshape(n_pb, p_seq, D_MODEL), xs.reshape(n_sb, s_seq, D_MODEL),
            ckv_new, jnp.swapaxes(krope_t, 2, 3), heads(gk_t), heads(gv_t))
```

```python
import functools

import numpy as np
import jax
import jax.numpy as jnp
from jax import lax
from jax.experimental import pallas as pl
from jax.experimental.pallas import tpu as pltpu

D_MODEL = 1024
DEPTH = 2
GRID_W = 64
MLA_HEADS = 6
MLA_Q_RANK = 384
MLA_KV_RANK = 256
MLA_NOPE_DIM = 64
MLA_ROPE_DIM = 32
MLA_V_DIM = 64
MLA_QK_DIM = MLA_NOPE_DIM + MLA_ROPE_DIM
FNET_GROUPS = 4
FNET_GROUP_DIM = 64
FNET_WIDTH = FNET_GROUPS * FNET_GROUP_DIM
GQA_HEADS = 6
GQA_KV_HEADS = 2
GQA_HEAD_DIM = 64
GQA_GROUP = GQA_HEADS // GQA_KV_HEADS
GQA_WIDTH = GQA_HEADS * GQA_HEAD_DIM
GQA_KV_WIDTH = GQA_KV_HEADS * GQA_HEAD_DIM
D_FF = 2816
ROPE_THETA = 10000.0
EPS = 1e-6
LOG2_E = 1.4426950408889634

LANES = 128
HALF = LANES // 2
HEAD_PAIRS = MLA_HEADS // 2
MLA_PAD_WIDTH = MLA_HEADS * LANES
MIX_TILE = HEAD_PAIRS * LANES
VMEM_LIMIT = 52 * 1024 * 1024
POST_TILE = 512
PROJ_SUB = 256

_IN_A = MLA_Q_RANK + MLA_KV_RANK
_IN_C0 = _IN_A + MLA_ROPE_DIM
_IN_C = FNET_WIDTH + GQA_WIDTH + 2 * GQA_KV_WIDTH
_C_U, _C_GQ, _C_GK, _C_GV = 0, FNET_WIDTH, FNET_WIDTH + GQA_WIDTH, FNET_WIDTH + GQA_WIDTH + GQA_KV_WIDTH

BF16 = jnp.bfloat16
F32 = jnp.float32


def _rope_tables(n_tokens):
    t = np.arange(n_tokens)
    row = (t // GRID_W).astype(np.float64)
    col = (t % GRID_W).astype(np.float64)

    def angles(rot_dim):
        n_axis = rot_dim // 4
        inv = ROPE_THETA ** (-np.arange(n_axis, dtype=np.float64) / n_axis)
        return np.concatenate([row[:, None] * inv, col[:, None] * inv], axis=-1)

    def tables(ang, lane_to_pair, is_first, is_second):
        cos = np.where((is_first | is_second)[None, :], np.cos(ang)[:, lane_to_pair], 1.0)
        sin = np.sin(ang)[:, lane_to_pair]
        sin_a = np.where(is_second[None, :], sin, 0.0)
        sin_b = np.where(is_first[None, :], -sin, 0.0)
        return [cos, sin_a, sin_b]

    lane = np.arange(LANES)
    half_m = MLA_ROPE_DIM // 2
    in_rope = lane < MLA_ROPE_DIM
    first_m = lane < half_m
    second_m = in_rope & ~first_m
    pair_m = lane % half_m
    half_g = GQA_HEAD_DIM // 2
    first_g = (lane % GQA_HEAD_DIM) < half_g
    pair_g = lane % half_g
    tabs = (tables(angles(MLA_ROPE_DIM), pair_m, first_m, second_m)
            + tables(angles(GQA_HEAD_DIM), pair_g, first_g, ~first_g))
    return jnp.asarray(np.concatenate(tabs, axis=-1), dtype=F32)


def _channel_dft():
    c = np.arange(FNET_GROUP_DIM)
    ang = 2.0 * np.pi * np.outer(c, c) / FNET_GROUP_DIM
    eye = np.eye(FNET_GROUPS)
    table = np.concatenate([np.kron(eye, np.cos(ang)), np.kron(eye, np.sin(ang))], axis=1)
    return jnp.asarray(table, dtype=F32).astype(BF16)


def _group_mean_matrix(n_heads):
    return jnp.asarray(np.kron(np.eye(n_heads), np.full((GQA_HEAD_DIM, GQA_HEAD_DIM), 1.0 / GQA_HEAD_DIM)), dtype=BF16)


def _rope_place_matrix():
    return jnp.asarray(np.eye(MLA_ROPE_DIM, LANES), dtype=BF16)


DFT_SPLIT = 32


def _seq_dft_tables(seq):
    s = np.arange(seq)
    n_a = seq // DFT_SPLIT
    ang_a = 2.0 * np.pi * ((np.arange(n_a)[:, None] * s[None, :]) % n_a) / n_a
    ang_b = 2.0 * np.pi * ((np.arange(DFT_SPLIT)[:, None] * s[None, :]) % seq) / seq
    return tuple(jnp.asarray(t, dtype=F32) for t in (np.cos(ang_a), np.sin(ang_a), np.cos(ang_b), np.sin(ang_b)))


def _rms(x):
    return x * lax.rsqrt(jnp.mean(x * x, axis=-1, keepdims=True) + EPS)


def _dot(a, b):
    return jnp.dot(a, b, preferred_element_type=F32)


def _group_rms(x, mean_mat):
    ms = _dot((x * x).astype(BF16), mean_mat)
    return x * lax.rsqrt(ms + EPS)


def _rotate(x, cos, sin_a, sin_b, half):
    return x * cos + pltpu.roll(x, half, 1) * sin_a + pltpu.roll(x, LANES - half, 1) * sin_b


def _lower_half(rows):
    return lax.broadcasted_iota(jnp.int32, (rows, LANES), 1) < HALF


def _layer_spec(arr, layer):
    return pl.BlockSpec((None,) + arr.shape[1:], lambda i: (layer, 0, 0))


def _ada_kernel(ct_ref, w_ref, b_ref, o_ref):
    s = ct_ref[...]
    s = s * jax.nn.sigmoid(s)
    w = w_ref[...]
    for m in range(o_ref.shape[0]):
        o_ref[m:m + 1, :] = jnp.sum(w * s[:, m:m + 1], axis=0, keepdims=True) + b_ref[...]


def _ada(cond_t, w_ada, b_ada, tn=2048):
    n_cond = cond_t.shape[1]
    width = w_ada.shape[2]
    return pl.pallas_call(
        _ada_kernel,
        grid=(DEPTH, width // tn),
        in_specs=[
            pl.BlockSpec((D_MODEL, n_cond), lambda l, j: (0, 0)),
            pl.BlockSpec((None, D_MODEL, tn), lambda l, j: (l, 0, j)),
            pl.BlockSpec((None, 1, tn), lambda l, j: (l, 0, j)),
        ],
        out_specs=pl.BlockSpec((None, n_cond, tn), lambda l, j: (l, 0, j)),
        out_shape=jax.ShapeDtypeStruct((DEPTH, n_cond, width), F32),
        compiler_params=pltpu.CompilerParams(dimension_semantics=("arbitrary", "arbitrary"),
                                             vmem_limit_bytes=VMEM_LIMIT),
        name="ada",
    )(cond_t, w_ada, b_ada.reshape(DEPTH, 1, width))


_PROJ_WEIGHTS = ("g1", "w_a", "w_c", "w_kr", "g_q_a", "wq", "g_kv_a", "wk", "wv", "g_q_head", "g_k_head")


def _proj_kernel(*refs, rope, emit_cache, n_prev, multi_seq):
    (x_ref, mod_ref, g1_ref, wa_ref, wc_ref, wkr_ref, gqa_ref, wq_ref, gkva_ref, wk_ref, wv_ref,
     gqh_ref, gkh_ref, wdft_ref, mq_ref, mk_ref) = refs[:16]
    refs = refs[16:]
    if rope:
        tab_ref, refs = refs[0], refs[1:]
    if n_prev:
        prev_refs, refs = refs[:4], refs[4:]
    qm_ref, km_ref, vm_ref, uc_ref, us_ref, gq_ref, gk_ref, gv_ref = refs[:8]
    cache_refs = refs[8:]

    shift = mod_ref[:, 0:D_MODEL]
    scale = mod_ref[:, D_MODEL:2 * D_MODEL]
    lower = _lower_half(PROJ_SUB)

    for r in range(x_ref.shape[0] // PROJ_SUB):
        rows = slice(r * PROJ_SUB, (r + 1) * PROJ_SUB)
        seq_cols = (slice(None), slice(r * FNET_WIDTH, (r + 1) * FNET_WIDTH)) if multi_seq else (rows, slice(None))
        t_idx = (r,) if multi_seq else (slice(None), rows)

        hb = (_rms(x_ref[rows, :]) * g1_ref[...] * (1.0 + scale) + shift).astype(BF16)
        pa = _dot(hb, wa_ref[...])
        pc = _dot(hb, wc_ref[...])
        kr = _dot(hb, wkr_ref[...])

        if rope:
            cos_m, sa_m, sb_m = (tab_ref[rows, i * LANES:(i + 1) * LANES] for i in range(3))
            cos_g, sa_g, sb_g = (tab_ref[rows, i * LANES:(i + 1) * LANES] for i in range(3, 6))

        gqn = _group_rms(pc[:, _C_GQ:_C_GQ + GQA_WIDTH], mq_ref[...]) * gqh_ref[...]
        tiles = []
        for p in range(HEAD_PAIRS):
            gp = gqn[:, p * LANES:(p + 1) * LANES]
            if rope:
                gp = _rotate(gp, cos_g, sa_g, sb_g, GQA_HEAD_DIM // 2)
            tiles.append(gp * (LOG2_E * GQA_HEAD_DIM ** -0.5))
        gq_ref[rows, 0:LANES] = jnp.where(lower, tiles[0], tiles[1]).astype(BF16)
        gq_ref[rows, LANES:2 * LANES] = jnp.where(
            lower, pltpu.roll(tiles[0], HALF, 1), pltpu.roll(tiles[2], HALF, 1)).astype(BF16)
        gq_ref[rows, 2 * LANES:] = jnp.where(lower, tiles[1], tiles[2]).astype(BF16)
        gkn = _group_rms(pc[:, _C_GK:_C_GK + GQA_KV_WIDTH], mk_ref[...]) * gkh_ref[...]
        gkr = _rotate(gkn, cos_g, sa_g, sb_g, GQA_HEAD_DIM // 2) if rope else gkn
        gv_t = pc[:, _C_GV:_C_GV + GQA_KV_WIDTH].T
        gk_ref[rows, :] = gkr.astype(BF16)
        gv_ref[t_idx] = gv_t.astype(BF16)

        cqn = _rms(pa[:, :MLA_Q_RANK]) * gqa_ref[...]
        q = _dot(cqn.astype(BF16), wq_ref[...])
        for hd in range(MLA_HEADS):
            qh = q[:, hd * LANES:(hd + 1) * LANES]
            if rope:
                qh = _rotate(qh, cos_m, sa_m, sb_m, MLA_ROPE_DIM // 2)
            qm_ref[rows, hd * LANES:(hd + 1) * LANES] = (qh * (LOG2_E * MLA_QK_DIM ** -0.5)).astype(BF16)

        ckvn = _rms(pa[:, MLA_Q_RANK:]) * gkva_ref[...]
        cb = ckvn.astype(BF16)
        kn = _dot(cb, wk_ref[...])
        krr = _rotate(kr, cos_m, sa_m, sb_m, MLA_ROPE_DIM // 2) if rope else kr
        for hd in range(MLA_HEADS):
            km_ref[rows, hd * LANES:(hd + 1) * LANES] = (kn[:, hd * LANES:(hd + 1) * LANES] + krr).astype(BF16)
        vm_ref[t_idx] = _dot(cb, wv_ref[...]).T.astype(BF16)

        ucs = _dot(pc[:, _C_U:_C_U + FNET_WIDTH].astype(BF16), wdft_ref[...])
        uc_ref[seq_cols] = ucs[:, :FNET_WIDTH].astype(BF16)
        us_ref[seq_cols] = ucs[:, FNET_WIDTH:].astype(BF16)

        if emit_cache:
            new = (ckvn, kr.T[:MLA_ROPE_DIM, :], gkn.T, gv_t)
            for i, out_ref in enumerate(cache_refs):
                if n_prev:
                    out_ref[r, :n_prev] = prev_refs[i][r]
                out_ref[r, n_prev] = new[i]


def _proj(x, mod, wts, win_parts, layer, consts, *, seq, rope, prev_cache=None, emit_cache=False, tm=512):
    n_tok = x.shape[0]
    n_tiles = n_tok // tm
    n_seq = n_tok // seq
    tiles_per_mod = n_tiles // mod.shape[0]
    multi_seq = tm > seq
    if multi_seq:
        assert seq == PROJ_SUB and not rope
        spt = tm // seq
        seq_major = pl.BlockSpec((seq, spt * FNET_WIDTH), lambda i: (0, i))
        transposed = lambda w: pl.BlockSpec((spt, w, seq), lambda i: (i, 0, 0))
    else:
        assert not emit_cache
        tps = seq // tm
        seq_major = pl.BlockSpec((tm, FNET_WIDTH), lambda i: (i % tps, i // tps))
        transposed = lambda w: pl.BlockSpec((None, w, tm), lambda i: (i // tps, 0, i % tps))
    const = lambda arr: pl.BlockSpec(arr.shape, lambda i: (0,) * arr.ndim)
    tok = lambda w: pl.BlockSpec((tm, w), lambda i: (i, 0))
    in_specs = [tok(D_MODEL), pl.BlockSpec((None, 1, 6 * D_MODEL), lambda i: (i // tiles_per_mod, 0, 0))]
    args = [x, mod]
    for name in _PROJ_WEIGHTS:
        arr = win_parts.get(name)
        in_specs.append(const(arr) if arr is not None else _layer_spec(wts[name], layer))
        args.append(arr if arr is not None else wts[name])
    for name in ("wdft", "mean_q", "mean_k"):
        in_specs.append(const(consts[name]))
        args.append(consts[name])
    if rope:
        in_specs.append(pl.BlockSpec((tm, 6 * LANES), lambda i: (i % tps, 0)))
        args.append(consts["rope"])
    out_specs = [tok(MLA_PAD_WIDTH), tok(MLA_PAD_WIDTH), transposed(MIX_TILE), seq_major, seq_major,
                 tok(GQA_WIDTH), tok(GQA_KV_WIDTH), transposed(GQA_KV_WIDTH)]
    out_shape = [jax.ShapeDtypeStruct((n_tok, MLA_PAD_WIDTH), BF16),
                 jax.ShapeDtypeStruct((n_tok, MLA_PAD_WIDTH), BF16),
                 jax.ShapeDtypeStruct((n_seq, MIX_TILE, seq), BF16),
                 jax.ShapeDtypeStruct((seq, n_seq * FNET_WIDTH), BF16),
                 jax.ShapeDtypeStruct((seq, n_seq * FNET_WIDTH), BF16),
                 jax.ShapeDtypeStruct((n_tok, GQA_WIDTH), BF16),
                 jax.ShapeDtypeStruct((n_tok, GQA_KV_WIDTH), BF16),
                 jax.ShapeDtypeStruct((n_seq, GQA_KV_WIDTH, seq), BF16)]
    n_prev = 0
    if emit_cache:
        n_prev = prev_cache[0].shape[1] if prev_cache is not None else 0
        layers = lambda n, tail: pl.BlockSpec((spt, n) + tail, lambda i: (i, 0, 0, 0))
        tails = ((seq, MLA_KV_RANK), (MLA_ROPE_DIM, seq), (GQA_KV_WIDTH, seq), (GQA_KV_WIDTH, seq))
        if n_prev:
            in_specs += [layers(n_prev, t) for t in tails]
            args += list(prev_cache)
        out_specs += [layers(n_prev + 1, t) for t in tails]
        out_shape += [jax.ShapeDtypeStruct((n_seq, n_prev + 1) + t, F32) for t in tails]
    return pl.pallas_call(
        functools.partial(_proj_kernel, rope=rope, emit_cache=emit_cache, n_prev=n_prev, multi_seq=multi_seq),
        grid=(n_tiles,),
        in_specs=in_specs,
        out_specs=out_specs,
        out_shape=out_shape,
        compiler_params=pltpu.CompilerParams(dimension_semantics=("arbitrary",), vmem_limit_bytes=VMEM_LIMIT),
        name="proj_rope" if rope else "proj_ctx",
    )(*args)


def _ctx_kernel(ckv_ref, kr_ref, gk_ref, gv_ref, wk_ref, wv_ref, place_ref, win_ref,
                km_ref, vm_ref, gko_ref, gvo_ref, wa_ref, wc_ref, wkr_ref):
    wt = win_ref[...]
    wa_ref[...] = wt[:_IN_A, :].T.astype(BF16)
    wc_ref[...] = wt[_IN_C0:, :].T.astype(BF16)
    kr_rows = jnp.concatenate([wt[_IN_A:_IN_C0, :], jnp.zeros((LANES - MLA_ROPE_DIM, wt.shape[1]), F32)], axis=0)
    wkr_ref[...] = kr_rows.T.astype(BF16)

    cb = ckv_ref[...].astype(BF16)
    kn = _dot(cb, wk_ref[...])
    kr = _dot(kr_ref[...].astype(BF16), place_ref[...])
    for hd in range(MLA_HEADS):
        km_ref[:, hd * LANES:(hd + 1) * LANES] = (kn[:, hd * LANES:(hd + 1) * LANES] + kr).astype(BF16)
    vm_ref[...] = _dot(cb, wv_ref[...]).T.astype(BF16)
    gko_ref[...] = gk_ref[...].astype(BF16)
    gvo_ref[...] = gv_ref[...].T.astype(BF16)


def _ctx_prep(layer, cache_ckv, cache_krope, cache_gk, cache_gv, w_in_t, wts, consts):
    nb, _, past, _ = cache_ckv.shape
    rows = D_MODEL // nb
    cache = lambda w: pl.BlockSpec((None, None, past, w), lambda b: (b, layer, 0, 0))
    out = lambda w: pl.BlockSpec((None, past, w), lambda b: (b, 0, 0))
    out_t = lambda w: pl.BlockSpec((None, w, past), lambda b: (b, 0, 0))
    part = lambda w: pl.BlockSpec((rows, w), lambda b: (b, 0))
    part_widths = (_IN_A, _IN_C, LANES)
    return pl.pallas_call(
        _ctx_kernel,
        grid=(nb,),
        in_specs=[cache(MLA_KV_RANK), cache(MLA_ROPE_DIM), cache(GQA_KV_WIDTH), cache(GQA_KV_WIDTH),
                  _layer_spec(wts["wk"], layer), _layer_spec(wts["wv"], layer),
                  pl.BlockSpec((MLA_ROPE_DIM, LANES), lambda b: (0, 0)),
                  pl.BlockSpec((None, w_in_t.shape[1], rows), lambda b: (layer, 0, b))],
        out_specs=[out(MLA_PAD_WIDTH), out_t(MIX_TILE), out(GQA_KV_WIDTH), out_t(GQA_KV_WIDTH)]
                  + [part(w) for w in part_widths],
        out_shape=[jax.ShapeDtypeStruct((nb, past, MLA_PAD_WIDTH), BF16),
                   jax.ShapeDtypeStruct((nb, MIX_TILE, past), BF16),
                   jax.ShapeDtypeStruct((nb, past, GQA_KV_WIDTH), BF16),
                   jax.ShapeDtypeStruct((nb, GQA_KV_WIDTH, past), BF16)]
                  + [jax.ShapeDtypeStruct((D_MODEL, w), BF16) for w in part_widths],
        compiler_params=pltpu.CompilerParams(dimension_semantics=("arbitrary",), vmem_limit_bytes=VMEM_LIMIT),
        name="ctx_prep",
    )(cache_ckv, cache_krope, cache_gk.reshape(nb, DEPTH, past, GQA_KV_WIDTH),
      cache_gv.reshape(nb, DEPTH, past, GQA_KV_WIDTH), wts["wk"], wts["wv"], consts["place"], w_in_t)


def _attn_kernel(*refs, n_seg, ahead, n_cast):
    n_in = 1 + 2 * n_seg
    outs_at = 2 * n_in + n_cast
    families = []
    for f, mla in enumerate((True, False)):
        ins = refs[f * n_in:(f + 1) * n_in]
        families.append((mla, ins[0], ins[1:1 + n_seg], ins[1 + n_seg:], refs[outs_at + f]))
    for c in range(n_cast):
        refs[outs_at + 2 + c][...] = refs[2 * n_in + c][...].astype(BF16)
    n_seqs, tq = refs[0].shape[:2]
    lower = _lower_half(tq)
    top = lax.broadcasted_iota(jnp.int32, (LANES, tq), 0) < HALF

    def scores_t(f, b, p, hd):
        mla, q_ref, k_refs, _, _ = families[f]
        if mla:
            cols = slice((2 * p + hd) * LANES, (2 * p + hd + 1) * LANES)
            qh = q_ref[b, :, cols]
            keys = [k_ref[b, :, cols] for k_ref in k_refs]
        else:
            q = q_ref[b, :, p * LANES:(p + 1) * LANES]
            qh = jnp.where(lower if hd == 0 else ~lower, q, jnp.zeros_like(q))
            keys = [k_ref[b] for k_ref in k_refs]
        return [lax.dot_general(k, qh, (((1,), (1,)), ((), ())), preferred_element_type=F32) for k in keys]

    def softmax_t(ss):
        m = functools.reduce(jnp.maximum, [jnp.max(s, axis=0, keepdims=True) for s in ss])
        es = [jnp.exp2(s - m) for s in ss]
        denom = functools.reduce(jnp.add, [jnp.sum(e, axis=0, keepdims=True) for e in es])
        return [e.astype(BF16) for e in es], denom

    def weighted_values_t(f, b, p, es, denom):
        mla, _, _, vt_refs, _ = families[f]
        rows = slice(p * LANES, (p + 1) * LANES) if mla else slice(None)
        acc = functools.reduce(jnp.add, [_dot(vt_ref[b, rows, :], e) for e, vt_ref in zip(es, vt_refs)])
        return acc * (1.0 / denom)

    work = [(f, b, p, hd) for f in range(2) for b in range(n_seqs) for p in range(HEAD_PAIRS) for hd in range(2)]
    pending = [scores_t(*w) for w in work[:ahead]]
    outs = {}
    for i, (f, b, p, hd) in enumerate(work):
        es, denom = softmax_t(pending.pop(0))
        if i + ahead < len(work):
            pending.append(scores_t(*work[i + ahead]))
        outs[f, b, p, hd] = weighted_values_t(f, b, p, es, denom)
        if (p, hd) != (HEAD_PAIRS - 1, 1):
            continue
        mla, o_ref = families[f][0], families[f][4]
        tiles = [jnp.where(top, outs[f, b, t, 0], outs[f, b, t, 1]).T for t in range(HEAD_PAIRS)]
        if not mla:
            t1 = pltpu.roll(tiles[1], HALF, 1)
            tiles = [jnp.where(lower, tiles[0], t1), jnp.where(lower, tiles[2], tiles[0]),
                     jnp.where(lower, t1, tiles[2])]
        for t in range(HEAD_PAIRS):
            o_ref[b, :, t * LANES:(t + 1) * LANES] = tiles[t].astype(o_ref.dtype)


def _attention(mla_q, mla_kv, gqa_q, gqa_kv, *, tq, bb=1, ahead=2, cast=(), layer=0):
    n_batch, sq = mla_q.shape[:2]
    n_seg = len(mla_kv)
    n_q = sq // tq
    steps = (n_batch // bb) * n_q
    whole = lambda a: pl.BlockSpec((bb,) + a.shape[1:], lambda b, i: (b, 0, 0))
    in_specs, args = [], []
    for q, kv in ((mla_q, mla_kv), (gqa_q, gqa_kv)):
        in_specs += [pl.BlockSpec((bb, tq, q.shape[2]), lambda b, i: (b, i, 0))]
        in_specs += [whole(k) for k, _ in kv] + [whole(v) for _, v in kv]
        args += [q] + [k for k, _ in kv] + [v for _, v in kv]
    out_spec = pl.BlockSpec((bb, tq, MIX_TILE), lambda b, i: (b, i, 0))
    out_specs = [out_spec, out_spec]
    out_shape = [jax.ShapeDtypeStruct((n_batch, sq, MIX_TILE), BF16)] * 2
    for w in cast:
        rows, cols = w.shape[1] // steps, w.shape[2]
        in_specs.append(pl.BlockSpec((None, rows, cols), lambda b, i: (layer, b * n_q + i, 0)))
        out_specs.append(pl.BlockSpec((rows, cols), lambda b, i: (b * n_q + i, 0)))
        out_shape.append(jax.ShapeDtypeStruct(w.shape[1:], BF16))
        args.append(w)
    return pl.pallas_call(
        functools.partial(_attn_kernel, n_seg=n_seg, ahead=ahead, n_cast=len(cast)),
        grid=(n_batch // bb, n_q),
        in_specs=in_specs,
        out_specs=out_specs,
        out_shape=out_shape,
        compiler_params=pltpu.CompilerParams(dimension_semantics=("arbitrary",) * 2, vmem_limit_bytes=VMEM_LIMIT),
        name=f"attn_s{n_seg}",
    )(*args)


def _fnet_kernel(ca_ref, sa_ref, cb_ref, sb_ref, uc_ref, us_ref, o_ref, tc_ref, ts_ref, *, scale):
    @pl.when(pl.program_id(1) == 0)
    def _build_twiddles():
        cb, sb = cb_ref[...], sb_ref[...]
        for a in range(ca_ref.shape[0]):
            ca, sa = ca_ref[a:a + 1, :], sa_ref[a:a + 1, :]
            rows = slice(a * DFT_SPLIT, (a + 1) * DFT_SPLIT)
            tc_ref[rows, :] = (ca * cb - sa * sb).astype(BF16)
            ts_ref[rows, :] = (sa * cb + ca * sb).astype(BF16)

    acc = _dot(tc_ref[...], uc_ref[...]) - _dot(ts_ref[...], us_ref[...])
    o_ref[...] = (acc * scale).astype(o_ref.dtype)


def _fnet(tables, uc, us, tm, tn):
    seq, width = uc.shape
    scale = float((seq * FNET_GROUP_DIM) ** -0.5)
    n_a = tm // DFT_SPLIT
    part_a = pl.BlockSpec((n_a, seq), lambda i, j: (i, 0))
    part_b = pl.BlockSpec((DFT_SPLIT, seq), lambda i, j: (0, 0))
    data = pl.BlockSpec((seq, tn), lambda i, j: (0, j))
    return pl.pallas_call(
        functools.partial(_fnet_kernel, scale=scale),
        grid=(seq // tm, width // tn),
        in_specs=[part_a, part_a, part_b, part_b, data, data],
        out_specs=pl.BlockSpec((tm, tn), lambda i, j: (i, j)),
        out_shape=jax.ShapeDtypeStruct((seq, width), BF16),
        scratch_shapes=[pltpu.VMEM((tm, seq), BF16), pltpu.VMEM((tm, seq), BF16)],
        compiler_params=pltpu.CompilerParams(dimension_semantics=("arbitrary", "arbitrary"),
                                             vmem_limit_bytes=VMEM_LIMIT),
        name="fnet",
    )(*tables, uc, us)


def _post_body(x_ref, mla_ref, fn_ref, gqa_ref, o_ref, mod_ref, g2_ref, wo_ref, wg_ref, wu_ref, wd_ref, gf_ref, final):
    fn = fn_ref[...]
    if fn.shape[1] > FNET_WIDTH:
        fn = jnp.concatenate([fn[:, j:j + FNET_WIDTH] for j in range(0, fn.shape[1], FNET_WIDTH)], axis=0)
    mix = jnp.concatenate([mla_ref[...], fn, gqa_ref[...]], axis=-1)
    gate1 = mod_ref[:, 2 * D_MODEL:3 * D_MODEL]
    shift2 = mod_ref[:, 3 * D_MODEL:4 * D_MODEL]
    scale2 = mod_ref[:, 4 * D_MODEL:5 * D_MODEL]
    gate2 = mod_ref[:, 5 * D_MODEL:6 * D_MODEL]
    x = x_ref[...] + gate1 * _dot(mix, wo_ref[...])
    h = (_rms(x) * g2_ref[...] * (1.0 + scale2) + shift2).astype(BF16)
    g = _dot(h, wg_ref[...])
    u = _dot(h, wu_ref[...])
    a = (g * jax.nn.sigmoid(g) * u).astype(BF16)
    x = x + gate2 * _dot(a, wd_ref[...])
    if final:
        x = _rms(x) * gf_ref[...]
    o_ref[...] = x


def _post_kernel(*refs, final, n_first):
    first, second, shared = refs[0:4], refs[4:8], refs[8:15]
    o_first, o_second = refs[15:17]
    step = pl.program_id(0)

    @pl.when(step < n_first)
    def _first_group():
        _post_body(*first, o_first, *shared, final)

    @pl.when(step >= n_first)
    def _second_group():
        _post_body(*second, o_second, *shared, final)


def _post(groups, mod, g2, ffn, layer, g_final, *, final, tm):
    (x1, _, _, _, seq1), (x2, _, _, _, seq2) = groups
    n1, n2 = x1.shape[0] // tm, x2.shape[0] // tm
    tiles_per_mod = n2 // (mod.shape[0] - 1)
    tile1 = lambda i: jnp.minimum(i, n1 - 1)
    tile2 = lambda i: jnp.maximum(i - n1, 0)
    resident = lambda arr: pl.BlockSpec(arr.shape, lambda i: (0, 0), pipeline_mode=pl.Buffered(1))

    def group_specs(seq, tile):
        tok = lambda w: pl.BlockSpec((tm, w), lambda i: (tile(i), 0))
        if tm <= seq:
            tps = seq // tm
            fnet_spec = pl.BlockSpec((tm, FNET_WIDTH), lambda i: (tile(i) % tps, tile(i) // tps))
        else:
            fnet_spec = pl.BlockSpec((seq, (tm // seq) * FNET_WIDTH), lambda i: (0, tile(i)))
        return [tok(D_MODEL), tok(MIX_TILE), fnet_spec, tok(MIX_TILE)], tok(D_MODEL)

    in1, out1 = group_specs(seq1, tile1)
    in2, out2 = group_specs(seq2, tile2)
    mod_spec = pl.BlockSpec((None, 1, 6 * D_MODEL),
                            lambda i: (jnp.where(i < n1, 0, 1 + tile2(i) // tiles_per_mod), 0, 0))
    return pl.pallas_call(
        functools.partial(_post_kernel, final=final, n_first=n1),
        grid=(n1 + n2,),
        in_specs=in1 + in2 + [mod_spec, _layer_spec(g2, layer)] + [resident(w) for w in ffn]
                 + [pl.BlockSpec((1, D_MODEL), lambda i: (0, 0))],
        out_specs=[out1, out2],
        out_shape=[jax.ShapeDtypeStruct(x1.shape, F32), jax.ShapeDtypeStruct(x2.shape, F32)],
        compiler_params=pltpu.CompilerParams(dimension_semantics=("arbitrary",), vmem_limit_bytes=VMEM_LIMIT),
        name="post_final" if final else "post",
    )(*groups[0][:4], *groups[1][:4], mod, g2, *ffn, g_final)


def _prep_weights(g_norm1, g_norm2, g_q_a, w_q_up, g_kv_a, w_kv_up, g_q_head, g_k_head):
    row = lambda g: g.reshape(DEPTH, 1, -1)
    wq = w_q_up.reshape(DEPTH, MLA_Q_RANK, MLA_HEADS, MLA_QK_DIM)
    wq = jnp.concatenate([wq[..., MLA_NOPE_DIM:], wq[..., :MLA_NOPE_DIM]], axis=-1)
    wq = jnp.pad(wq, ((0, 0), (0, 0), (0, 0), (0, LANES - MLA_QK_DIM)))
    wkv = w_kv_up.reshape(DEPTH, MLA_KV_RANK, MLA_HEADS, MLA_NOPE_DIM + MLA_V_DIM)
    wk = jnp.pad(wkv[..., :MLA_NOPE_DIM], ((0, 0), (0, 0), (0, 0), (MLA_ROPE_DIM, LANES - MLA_QK_DIM)))
    wv = wkv[..., MLA_NOPE_DIM:]
    return {
        "g1": row(g_norm1), "g2": row(g_norm2), "g_q_a": row(g_q_a), "g_kv_a": row(g_kv_a),
        "g_q_head": row(jnp.tile(g_q_head, (1, GQA_HEADS))), "g_k_head": row(jnp.tile(g_k_head, (1, GQA_KV_HEADS))),
        "wq": wq.reshape(DEPTH, MLA_Q_RANK, MLA_PAD_WIDTH).astype(BF16),
        "wk": wk.reshape(DEPTH, MLA_KV_RANK, MLA_PAD_WIDTH).astype(BF16),
        "wv": wv.reshape(DEPTH, MLA_KV_RANK, MIX_TILE).astype(BF16),
    }


def kernel(x_prompt, x_sample, cache_mla_ckv, cache_mla_krope, cache_gqa_k, cache_gqa_v, c, c_ctx, w_ada, b_ada,
           g_norm1, g_norm2, w_in, g_q_a, w_q_up, g_kv_a, w_kv_up, g_q_head, g_k_head, w_out,
           w_ffn_gate, w_ffn_up, w_ffn_down, g_final):
    n_pb, p_seq, _ = x_prompt.shape
    n_sb, s_seq, _ = x_sample.shape
    consts = {"wdft": _channel_dft(), "mean_q": _group_mean_matrix(GQA_HEADS),
              "mean_k": _group_mean_matrix(GQA_KV_HEADS), "place": _rope_place_matrix(),
              "rope": _rope_tables(s_seq)}
    dft_p = _seq_dft_tables(p_seq)
    dft_s = _seq_dft_tables(s_seq)
    gf = g_final.reshape(1, D_MODEL)
    wts = _prep_weights(g_norm1, g_norm2, g_q_a, w_q_up, g_kv_a, w_kv_up, g_q_head, g_k_head)
    w_in_t = jnp.swapaxes(w_in, 1, 2)

    cond_t = jnp.concatenate([c_ctx[None, :], c], axis=0).T
    mod = _ada(cond_t, w_ada, b_ada)

    xp = x_prompt.reshape(n_pb * p_seq, D_MODEL)
    xs = x_sample.reshape(n_sb * s_seq, D_MODEL)
    b3 = lambda a, nb: a.reshape(nb, a.shape[0] // nb, a.shape[1])
    caches = None
    for l in range(DEPTH):
        final = l == DEPTH - 1
        mod_p = mod[l, 0:1].reshape(1, 1, 6 * D_MODEL)
        mod_s = mod[l, 1:].reshape(n_sb, 1, 6 * D_MODEL)

        km_c, vm_c, gk_c, gv_c, w_a, w_c, w_kr = _ctx_prep(
            l, cache_mla_ckv, cache_mla_krope, cache_gqa_k, cache_gqa_v, w_in_t, wts, consts)
        win_parts = {"w_a": w_a, "w_c": w_c, "w_kr": w_kr}
        qm, km, vm, uc_s, us_s, gq, gk, gv = _proj(xs, mod_s, wts, win_parts, l, consts, seq=s_seq, rope=True)
        mla_s, gqa_s, *ffn = _attention(b3(qm, n_sb), [(km_c, vm_c), (b3(km, n_sb), vm)],
                                        b3(gq, n_sb), [(gk_c, gv_c), (b3(gk, n_sb), gv)], tq=512,
                                        cast=(w_out, w_ffn_gate, w_ffn_up, w_ffn_down), layer=l)

        qm, km, vm, uc, us, gq, gk, gv, *caches = _proj(
            xp, mod_p, wts, win_parts, l, consts, seq=p_seq, rope=False, emit_cache=True, prev_cache=caches)
        mla_o, gqa_o = _attention(b3(qm, n_pb), [(b3(km, n_pb), vm)], b3(gq, n_pb), [(b3(gk, n_pb), gv)],
                                  tq=p_seq, bb=4, ahead=8)
        fn_p = _fnet(dft_p, uc, us, tm=p_seq, tn=1024)
        fn_s = _fnet(dft_s, uc_s, us_s, tm=512, tn=n_sb * FNET_WIDTH)
        flat = lambda a: a.reshape(-1, MIX_TILE)
        xp, xs = _post([(xp, flat(mla_o), fn_p, flat(gqa_o), p_seq), (xs, flat(mla_s), fn_s, flat(gqa_s), s_seq)],
                       mod[l].reshape(1 + n_sb, 1, 6 * D_MODEL), wts["g2"], ffn, l, gf, final=final, tm=POST_TILE)

    ckv_new, krope_t, gk_t, gv_t = caches
    heads = lambda a: jnp.swapaxes(a, 2, 3).reshape(n_pb, DEPTH, p_seq, GQA_KV_HEADS, GQA_HEAD_DIM)
    return (xp.reshape(n_pb, p_seq, D_MODEL), xs.reshape(n_sb, s_seq, D_MODEL),
            ckv_new, jnp.swapaxes(krope_t, 2, 3), heads(gk_t), heads(gv_t))
```

```python
import functools

import numpy as np
import jax
import jax.numpy as jnp
from jax import lax
from jax.experimental import pallas as pl
from jax.experimental.pallas import tpu as pltpu

D_MODEL = 1024
DEPTH = 2
GRID_W = 64
MLA_HEADS = 6
MLA_Q_RANK = 384
MLA_KV_RANK = 256
MLA_NOPE_DIM = 64
MLA_ROPE_DIM = 32
MLA_V_DIM = 64
MLA_QK_DIM = MLA_NOPE_DIM + MLA_ROPE_DIM
FNET_GROUPS = 4
FNET_GROUP_DIM = 64
FNET_WIDTH = FNET_GROUPS * FNET_GROUP_DIM
GQA_HEADS = 6
GQA_KV_HEADS = 2
GQA_HEAD_DIM = 64
GQA_GROUP = GQA_HEADS // GQA_KV_HEADS
GQA_WIDTH = GQA_HEADS * GQA_HEAD_DIM
GQA_KV_WIDTH = GQA_KV_HEADS * GQA_HEAD_DIM
D_FF = 2816
ROPE_THETA = 10000.0
EPS = 1e-6
LOG2_E = 1.4426950408889634

LANES = 128
HALF = LANES // 2
HEAD_PAIRS = MLA_HEADS // 2
MLA_PAD_WIDTH = MLA_HEADS * LANES
MIX_TILE = HEAD_PAIRS * LANES
VMEM_LIMIT = 56 * 1024 * 1024
POST_TILE = 512
PROJ_SUB = 256

_IN_A = MLA_Q_RANK + MLA_KV_RANK
_IN_C0 = _IN_A + MLA_ROPE_DIM
_IN_C = FNET_WIDTH + GQA_WIDTH + 2 * GQA_KV_WIDTH
_C_U, _C_GQ, _C_GK, _C_GV = 0, FNET_WIDTH, FNET_WIDTH + GQA_WIDTH, FNET_WIDTH + GQA_WIDTH + GQA_KV_WIDTH

BF16 = jnp.bfloat16
F32 = jnp.float32


def _rope_tables(n_tokens):
    t = np.arange(n_tokens)
    row = (t // GRID_W).astype(np.float64)
    col = (t % GRID_W).astype(np.float64)

    def angles(rot_dim):
        n_axis = rot_dim // 4
        inv = ROPE_THETA ** (-np.arange(n_axis, dtype=np.float64) / n_axis)
        return np.concatenate([row[:, None] * inv, col[:, None] * inv], axis=-1)

    def tables(ang, lane_to_pair, is_first, is_second):
        cos = np.where((is_first | is_second)[None, :], np.cos(ang)[:, lane_to_pair], 1.0)
        sin = np.sin(ang)[:, lane_to_pair]
        sin_a = np.where(is_second[None, :], sin, 0.0)
        sin_b = np.where(is_first[None, :], -sin, 0.0)
        return [cos, sin_a, sin_b]

    lane = np.arange(LANES)
    half_m = MLA_ROPE_DIM // 2
    in_rope = lane < MLA_ROPE_DIM
    first_m = lane < half_m
    second_m = in_rope & ~first_m
    pair_m = lane % half_m
    half_g = GQA_HEAD_DIM // 2
    first_g = (lane % GQA_HEAD_DIM) < half_g
    pair_g = lane % half_g
    tabs = (tables(angles(MLA_ROPE_DIM), pair_m, first_m, second_m)
            + tables(angles(GQA_HEAD_DIM), pair_g, first_g, ~first_g))
    return jnp.asarray(np.concatenate(tabs, axis=-1), dtype=F32)


def _channel_dft():
    c = np.arange(FNET_GROUP_DIM)
    ang = 2.0 * np.pi * np.outer(c, c) / FNET_GROUP_DIM
    eye = np.eye(FNET_GROUPS)
    table = np.concatenate([np.kron(eye, np.cos(ang)), np.kron(eye, np.sin(ang))], axis=1)
    return jnp.asarray(table, dtype=F32).astype(BF16)


def _group_mean_matrix(n_heads):
    return jnp.asarray(np.kron(np.eye(n_heads), np.full((GQA_HEAD_DIM, GQA_HEAD_DIM), 1.0 / GQA_HEAD_DIM)), dtype=BF16)


def _rope_place_matrix():
    return jnp.asarray(np.eye(MLA_ROPE_DIM, LANES), dtype=BF16)


DFT_SPLIT = 32


def _seq_dft_tables(seq):
    s = np.arange(seq)
    n_a = seq // DFT_SPLIT
    ang_a = 2.0 * np.pi * ((np.arange(n_a)[:, None] * s[None, :]) % n_a) / n_a
    ang_b = 2.0 * np.pi * ((np.arange(DFT_SPLIT)[:, None] * s[None, :]) % seq) / seq
    return tuple(jnp.asarray(t, dtype=F32) for t in (np.cos(ang_a), np.sin(ang_a), np.cos(ang_b), np.sin(ang_b)))


def _rms(x):
    return x * lax.rsqrt(jnp.mean(x * x, axis=-1, keepdims=True) + EPS)


def _dot(a, b):
    return jnp.dot(a, b, preferred_element_type=F32)


def _group_rms(x, mean_mat):
    ms = _dot((x * x).astype(BF16), mean_mat)
    return x * lax.rsqrt(ms + EPS)


def _rotate(x, cos, sin_a, sin_b, half):
    return x * cos + pltpu.roll(x, half, 1) * sin_a + pltpu.roll(x, LANES - half, 1) * sin_b


def _lower_half(rows):
    return lax.broadcasted_iota(jnp.int32, (rows, LANES), 1) < HALF


def _layer_spec(arr, layer):
    return pl.BlockSpec((None,) + arr.shape[1:], lambda i: (layer, 0, 0))


def _ada_kernel(ct_ref, w_ref, b_ref, o_ref):
    s = ct_ref[...]
    s = s * jax.nn.sigmoid(s)
    w = w_ref[...]
    for m in range(o_ref.shape[0]):
        o_ref[m:m + 1, :] = jnp.sum(w * s[:, m:m + 1], axis=0, keepdims=True) + b_ref[...]


def _ada_specs(cond_t, w_ada, layer, tn):
    n_cond = cond_t.shape[1]
    in_specs = [pl.BlockSpec((D_MODEL, n_cond), lambda j: (0, 0)),
                pl.BlockSpec((None, D_MODEL, tn), lambda j: (layer, 0, j)),
                pl.BlockSpec((None, 1, tn), lambda j: (layer, 0, j))]
    out_shape = jax.ShapeDtypeStruct((n_cond, w_ada.shape[2]), F32)
    return in_specs, pl.BlockSpec((n_cond, tn), lambda j: (0, j)), out_shape


def _ada(cond_t, w_ada, b_ada, layer, tn=2048):
    in_specs, out_spec, out_shape = _ada_specs(cond_t, w_ada, layer, tn)
    return pl.pallas_call(
        _ada_kernel,
        grid=(w_ada.shape[2] // tn,),
        in_specs=in_specs,
        out_specs=out_spec,
        out_shape=out_shape,
        compiler_params=pltpu.CompilerParams(dimension_semantics=("arbitrary",), vmem_limit_bytes=VMEM_LIMIT),
        name="ada",
    )(cond_t, w_ada, b_ada)


_PROJ_WEIGHTS = ("g1", "w_a", "w_c", "w_kr", "g_q_a", "wq", "g_kv_a", "wk", "wv", "g_q_head", "g_k_head")


def _proj_kernel(*refs, rope, emit_cache, n_prev, multi_seq):
    (x_ref, mod_ref, g1_ref, wa_ref, wc_ref, wkr_ref, gqa_ref, wq_ref, gkva_ref, wk_ref, wv_ref,
     gqh_ref, gkh_ref, wdft_ref, mq_ref, mk_ref) = refs[:16]
    refs = refs[16:]
    if rope:
        tab_ref, refs = refs[0], refs[1:]
    if n_prev:
        prev_refs, refs = refs[:4], refs[4:]
    qm_ref, km_ref, vm_ref, uc_ref, us_ref, gq_ref, gk_ref, gv_ref = refs[:8]
    cache_refs = refs[8:]

    shift = mod_ref[:, 0:D_MODEL]
    scale = mod_ref[:, D_MODEL:2 * D_MODEL]
    lower = _lower_half(PROJ_SUB)

    for r in range(x_ref.shape[0] // PROJ_SUB):
        rows = slice(r * PROJ_SUB, (r + 1) * PROJ_SUB)
        seq_cols = (slice(None), slice(r * FNET_WIDTH, (r + 1) * FNET_WIDTH)) if multi_seq else (rows, slice(None))
        t_idx = (r,) if multi_seq else (slice(None), rows)

        hb = (_rms(x_ref[rows, :]) * g1_ref[...] * (1.0 + scale) + shift).astype(BF16)
        pa = _dot(hb, wa_ref[...])
        pc = _dot(hb, wc_ref[...])
        kr = _dot(hb, wkr_ref[...])

        if rope:
            cos_m, sa_m, sb_m = (tab_ref[rows, i * LANES:(i + 1) * LANES] for i in range(3))
            cos_g, sa_g, sb_g = (tab_ref[rows, i * LANES:(i + 1) * LANES] for i in range(3, 6))

        gqn = _group_rms(pc[:, _C_GQ:_C_GQ + GQA_WIDTH], mq_ref[...]) * gqh_ref[...]
        tiles = []
        for p in range(HEAD_PAIRS):
            gp = gqn[:, p * LANES:(p + 1) * LANES]
            if rope:
                gp = _rotate(gp, cos_g, sa_g, sb_g, GQA_HEAD_DIM // 2)
            tiles.append(gp * (LOG2_E * GQA_HEAD_DIM ** -0.5))
        gq_ref[rows, 0:LANES] = jnp.where(lower, tiles[0], tiles[1]).astype(BF16)
        gq_ref[rows, LANES:2 * LANES] = jnp.where(
            lower, pltpu.roll(tiles[0], HALF, 1), pltpu.roll(tiles[2], HALF, 1)).astype(BF16)
        gq_ref[rows, 2 * LANES:] = jnp.where(lower, tiles[1], tiles[2]).astype(BF16)
        gkn = _group_rms(pc[:, _C_GK:_C_GK + GQA_KV_WIDTH], mk_ref[...]) * gkh_ref[...]
        gkr = _rotate(gkn, cos_g, sa_g, sb_g, GQA_HEAD_DIM // 2) if rope else gkn
        gv_t = pc[:, _C_GV:_C_GV + GQA_KV_WIDTH].T
        gk_ref[rows, :] = gkr.astype(BF16)
        gv_ref[t_idx] = gv_t.astype(BF16)

        cqn = _rms(pa[:, :MLA_Q_RANK]) * gqa_ref[...]
        q = _dot(cqn.astype(BF16), wq_ref[...])
        for hd in range(MLA_HEADS):
            qh = q[:, hd * LANES:(hd + 1) * LANES]
            if rope:
                qh = _rotate(qh, cos_m, sa_m, sb_m, MLA_ROPE_DIM // 2)
            qm_ref[rows, hd * LANES:(hd + 1) * LANES] = (qh * (LOG2_E * MLA_QK_DIM ** -0.5)).astype(BF16)

        ckvn = _rms(pa[:, MLA_Q_RANK:]) * gkva_ref[...]
        cb = ckvn.astype(BF16)
        kn = _dot(cb, wk_ref[...])
        krr = _rotate(kr, cos_m, sa_m, sb_m, MLA_ROPE_DIM // 2) if rope else kr
        for hd in range(MLA_HEADS):
            km_ref[rows, hd * LANES:(hd + 1) * LANES] = (kn[:, hd * LANES:(hd + 1) * LANES] + krr).astype(BF16)
        vm_ref[t_idx] = _dot(cb, wv_ref[...]).T.astype(BF16)

        ucs = _dot(pc[:, _C_U:_C_U + FNET_WIDTH].astype(BF16), wdft_ref[...])
        uc_ref[seq_cols] = ucs[:, :FNET_WIDTH].astype(BF16)
        us_ref[seq_cols] = ucs[:, FNET_WIDTH:].astype(BF16)

        if emit_cache:
            new = (ckvn, kr.T[:MLA_ROPE_DIM, :], gkn.T, gv_t)
            for i, out_ref in enumerate(cache_refs):
                if n_prev:
                    out_ref[r, :n_prev] = prev_refs[i][r]
                out_ref[r, n_prev] = new[i]


def _proj(x, mod, wts, win_parts, layer, consts, *, seq, rope, prev_cache=None, emit_cache=False, tm=512):
    n_tok = x.shape[0]
    n_tiles = n_tok // tm
    n_seq = n_tok // seq
    tiles_per_mod = n_tiles // mod.shape[0]
    multi_seq = tm > seq
    if multi_seq:
        assert seq == PROJ_SUB and not rope
        spt = tm // seq
        seq_major = pl.BlockSpec((seq, spt * FNET_WIDTH), lambda i: (0, i))
        transposed = lambda w: pl.BlockSpec((spt, w, seq), lambda i: (i, 0, 0))
    else:
        assert not emit_cache
        tps = seq // tm
        seq_major = pl.BlockSpec((tm, FNET_WIDTH), lambda i: (i % tps, i // tps))
        transposed = lambda w: pl.BlockSpec((None, w, tm), lambda i: (i // tps, 0, i % tps))
    const = lambda arr: pl.BlockSpec(arr.shape, lambda i: (0,) * arr.ndim)
    tok = lambda w: pl.BlockSpec((tm, w), lambda i: (i, 0))
    in_specs = [tok(D_MODEL), pl.BlockSpec((None, 1, 6 * D_MODEL), lambda i: (i // tiles_per_mod, 0, 0))]
    args = [x, mod]
    for name in _PROJ_WEIGHTS:
        arr = win_parts.get(name)
        in_specs.append(const(arr) if arr is not None else _layer_spec(wts[name], layer))
        args.append(arr if arr is not None else wts[name])
    for name in ("wdft", "mean_q", "mean_k"):
        in_specs.append(const(consts[name]))
        args.append(consts[name])
    if rope:
        in_specs.append(pl.BlockSpec((tm, 6 * LANES), lambda i: (i % tps, 0)))
        args.append(consts["rope"])
    out_specs = [tok(MLA_PAD_WIDTH), tok(MLA_PAD_WIDTH), transposed(MIX_TILE), seq_major, seq_major,
                 tok(GQA_WIDTH), tok(GQA_KV_WIDTH), transposed(GQA_KV_WIDTH)]
    out_shape = [jax.ShapeDtypeStruct((n_tok, MLA_PAD_WIDTH), BF16),
                 jax.ShapeDtypeStruct((n_tok, MLA_PAD_WIDTH), BF16),
                 jax.ShapeDtypeStruct((n_seq, MIX_TILE, seq), BF16),
                 jax.ShapeDtypeStruct((seq, n_seq * FNET_WIDTH), BF16),
                 jax.ShapeDtypeStruct((seq, n_seq * FNET_WIDTH), BF16),
                 jax.ShapeDtypeStruct((n_tok, GQA_WIDTH), BF16),
                 jax.ShapeDtypeStruct((n_tok, GQA_KV_WIDTH), BF16),
                 jax.ShapeDtypeStruct((n_seq, GQA_KV_WIDTH, seq), BF16)]
    n_prev = 0
    if emit_cache:
        n_prev = prev_cache[0].shape[1] if prev_cache is not None else 0
        layers = lambda n, tail: pl.BlockSpec((spt, n) + tail, lambda i: (i, 0, 0, 0))
        tails = ((seq, MLA_KV_RANK), (MLA_ROPE_DIM, seq), (GQA_KV_WIDTH, seq), (GQA_KV_WIDTH, seq))
        if n_prev:
            in_specs += [layers(n_prev, t) for t in tails]
            args += list(prev_cache)
        out_specs += [layers(n_prev + 1, t) for t in tails]
        out_shape += [jax.ShapeDtypeStruct((n_seq, n_prev + 1) + t, F32) for t in tails]
    return pl.pallas_call(
        functools.partial(_proj_kernel, rope=rope, emit_cache=emit_cache, n_prev=n_prev, multi_seq=multi_seq),
        grid=(n_tiles,),
        in_specs=in_specs,
        out_specs=out_specs,
        out_shape=out_shape,
        compiler_params=pltpu.CompilerParams(dimension_semantics=("arbitrary",), vmem_limit_bytes=VMEM_LIMIT),
        name="proj_rope" if rope else "proj_ctx",
    )(*args)


def _ctx_kernel(ckv_ref, kr_ref, gk_ref, gv_ref, wk_ref, wv_ref, place_ref, win_ref,
                km_ref, vm_ref, gko_ref, gvo_ref, wa_ref, wc_ref, wkr_ref):
    wt = win_ref[...]
    wa_ref[...] = wt[:_IN_A, :].T.astype(BF16)
    wc_ref[...] = wt[_IN_C0:, :].T.astype(BF16)
    kr_rows = jnp.concatenate([wt[_IN_A:_IN_C0, :], jnp.zeros((LANES - MLA_ROPE_DIM, wt.shape[1]), F32)], axis=0)
    wkr_ref[...] = kr_rows.T.astype(BF16)

    cb = ckv_ref[...].astype(BF16)
    kn = _dot(cb, wk_ref[...])
    kr = _dot(kr_ref[...].astype(BF16), place_ref[...])
    for hd in range(MLA_HEADS):
        km_ref[:, hd * LANES:(hd + 1) * LANES] = (kn[:, hd * LANES:(hd + 1) * LANES] + kr).astype(BF16)
    vm_ref[...] = _dot(cb, wv_ref[...]).T.astype(BF16)
    gko_ref[...] = gk_ref[...].astype(BF16)
    gvo_ref[...] = gv_ref[...].T.astype(BF16)


def _ctx_prep(layer, cache_ckv, cache_krope, cache_gk, cache_gv, w_in_t, wts, consts):
    nb, _, past, _ = cache_ckv.shape
    rows = D_MODEL // nb
    cache = lambda w: pl.BlockSpec((None, None, past, w), lambda b: (b, layer, 0, 0))
    out = lambda w: pl.BlockSpec((None, past, w), lambda b: (b, 0, 0))
    out_t = lambda w: pl.BlockSpec((None, w, past), lambda b: (b, 0, 0))
    part = lambda w: pl.BlockSpec((rows, w), lambda b: (b, 0))
    part_widths = (_IN_A, _IN_C, LANES)
    return pl.pallas_call(
        _ctx_kernel,
        grid=(nb,),
        in_specs=[cache(MLA_KV_RANK), cache(MLA_ROPE_DIM), cache(GQA_KV_WIDTH), cache(GQA_KV_WIDTH),
                  _layer_spec(wts["wk"], layer), _layer_spec(wts["wv"], layer),
                  pl.BlockSpec((MLA_ROPE_DIM, LANES), lambda b: (0, 0)),
                  pl.BlockSpec((None, w_in_t.shape[1], rows), lambda b: (layer, 0, b))],
        out_specs=[out(MLA_PAD_WIDTH), out_t(MIX_TILE), out(GQA_KV_WIDTH), out_t(GQA_KV_WIDTH)]
                  + [part(w) for w in part_widths],
        out_shape=[jax.ShapeDtypeStruct((nb, past, MLA_PAD_WIDTH), BF16),
                   jax.ShapeDtypeStruct((nb, MIX_TILE, past), BF16),
                   jax.ShapeDtypeStruct((nb, past, GQA_KV_WIDTH), BF16),
                   jax.ShapeDtypeStruct((nb, GQA_KV_WIDTH, past), BF16)]
                  + [jax.ShapeDtypeStruct((D_MODEL, w), BF16) for w in part_widths],
        compiler_params=pltpu.CompilerParams(dimension_semantics=("arbitrary",), vmem_limit_bytes=VMEM_LIMIT),
        name="ctx_prep",
    )(cache_ckv, cache_krope, cache_gk.reshape(nb, DEPTH, past, GQA_KV_WIDTH),
      cache_gv.reshape(nb, DEPTH, past, GQA_KV_WIDTH), wts["wk"], wts["wv"], consts["place"], w_in_t)


def _attn_kernel(*refs, n_seg, ahead, n_cast):
    n_in = 1 + 2 * n_seg
    outs_at = 2 * n_in + n_cast
    families = []
    for f, mla in enumerate((True, False)):
        ins = refs[f * n_in:(f + 1) * n_in]
        families.append((mla, ins[0], ins[1:1 + n_seg], ins[1 + n_seg:], refs[outs_at + f]))
    for c in range(n_cast):
        refs[outs_at + 2 + c][...] = refs[2 * n_in + c][...].astype(BF16)
    n_seqs, tq = refs[0].shape[:2]
    lower = _lower_half(tq)
    top = lax.broadcasted_iota(jnp.int32, (LANES, tq), 0) < HALF

    def scores_t(f, b, p, hd):
        mla, q_ref, k_refs, _, _ = families[f]
        if mla:
            cols = slice((2 * p + hd) * LANES, (2 * p + hd + 1) * LANES)
            qh = q_ref[b, :, cols]
            keys = [k_ref[b, :, cols] for k_ref in k_refs]
        else:
            q = q_ref[b, :, p * LANES:(p + 1) * LANES]
            qh = jnp.where(lower if hd == 0 else ~lower, q, jnp.zeros_like(q))
            keys = [k_ref[b] for k_ref in k_refs]
        return [lax.dot_general(k, qh, (((1,), (1,)), ((), ())), preferred_element_type=F32) for k in keys]

    def softmax_t(ss):
        m = functools.reduce(jnp.maximum, [jnp.max(s, axis=0, keepdims=True) for s in ss])
        es = [jnp.exp2(s - m) for s in ss]
        denom = functools.reduce(jnp.add, [jnp.sum(e, axis=0, keepdims=True) for e in es])
        return [e.astype(BF16) for e in es], denom

    def weighted_values_t(f, b, p, es, denom):
        mla, _, _, vt_refs, _ = families[f]
        rows = slice(p * LANES, (p + 1) * LANES) if mla else slice(None)
        acc = functools.reduce(jnp.add, [_dot(vt_ref[b, rows, :], e) for e, vt_ref in zip(es, vt_refs)])
        return acc * (1.0 / denom)

    work = [(f, b, p, hd) for f in range(2) for b in range(n_seqs) for p in range(HEAD_PAIRS) for hd in range(2)]
    pending = [scores_t(*w) for w in work[:ahead]]
    outs = {}
    for i, (f, b, p, hd) in enumerate(work):
        es, denom = softmax_t(pending.pop(0))
        if i + ahead < len(work):
            pending.append(scores_t(*work[i + ahead]))
        outs[f, b, p, hd] = weighted_values_t(f, b, p, es, denom)
        if (p, hd) != (HEAD_PAIRS - 1, 1):
            continue
        mla, o_ref = families[f][0], families[f][4]
        tiles = [jnp.where(top, outs[f, b, t, 0], outs[f, b, t, 1]).T for t in range(HEAD_PAIRS)]
        if not mla:
            t1 = pltpu.roll(tiles[1], HALF, 1)
            tiles = [jnp.where(lower, tiles[0], t1), jnp.where(lower, tiles[2], tiles[0]),
                     jnp.where(lower, t1, tiles[2])]
        for t in range(HEAD_PAIRS):
            o_ref[b, :, t * LANES:(t + 1) * LANES] = tiles[t].astype(o_ref.dtype)


def _attention(mla_q, mla_kv, gqa_q, gqa_kv, *, tq, bb=1, ahead=2, cast=(), layer=0):
    n_batch, sq = mla_q.shape[:2]
    n_seg = len(mla_kv)
    n_q = sq // tq
    steps = (n_batch // bb) * n_q
    whole = lambda a: pl.BlockSpec((bb,) + a.shape[1:], lambda b, i: (b, 0, 0))
    in_specs, args = [], []
    for q, kv in ((mla_q, mla_kv), (gqa_q, gqa_kv)):
        in_specs += [pl.BlockSpec((bb, tq, q.shape[2]), lambda b, i: (b, i, 0))]
        in_specs += [whole(k) for k, _ in kv] + [whole(v) for _, v in kv]
        args += [q] + [k for k, _ in kv] + [v for _, v in kv]
    out_spec = pl.BlockSpec((bb, tq, MIX_TILE), lambda b, i: (b, i, 0))
    out_specs = [out_spec, out_spec]
    out_shape = [jax.ShapeDtypeStruct((n_batch, sq, MIX_TILE), BF16)] * 2
    for w in cast:
        rows, cols = w.shape[1] // steps, w.shape[2]
        in_specs.append(pl.BlockSpec((None, rows, cols), lambda b, i: (layer, b * n_q + i, 0)))
        out_specs.append(pl.BlockSpec((rows, cols), lambda b, i: (b * n_q + i, 0)))
        out_shape.append(jax.ShapeDtypeStruct(w.shape[1:], BF16))
        args.append(w)
    return pl.pallas_call(
        functools.partial(_attn_kernel, n_seg=n_seg, ahead=ahead, n_cast=len(cast)),
        grid=(n_batch // bb, n_q),
        in_specs=in_specs,
        out_specs=out_specs,
        out_shape=out_shape,
        compiler_params=pltpu.CompilerParams(dimension_semantics=("arbitrary",) * 2, vmem_limit_bytes=VMEM_LIMIT),
        name=f"attn_s{n_seg}",
    )(*args)


def _fnet_kernel(ca_ref, sa_ref, cb_ref, sb_ref, uc_ref, us_ref, o_ref, tc_ref, ts_ref, *, scale):
    @pl.when(pl.program_id(1) == 0)
    def _build_twiddles():
        cb, sb = cb_ref[...], sb_ref[...]
        for a in range(ca_ref.shape[0]):
            ca, sa = ca_ref[a:a + 1, :], sa_ref[a:a + 1, :]
            rows = slice(a * DFT_SPLIT, (a + 1) * DFT_SPLIT)
            tc_ref[rows, :] = (ca * cb - sa * sb).astype(BF16)
            ts_ref[rows, :] = (sa * cb + ca * sb).astype(BF16)

    acc = _dot(tc_ref[...], uc_ref[...]) - _dot(ts_ref[...], us_ref[...])
    o_ref[...] = (acc * scale).astype(o_ref.dtype)


def _fnet(tables, uc, us, tm, tn):
    seq, width = uc.shape
    scale = float((seq * FNET_GROUP_DIM) ** -0.5)
    n_a = tm // DFT_SPLIT
    part_a = pl.BlockSpec((n_a, seq), lambda i, j: (i, 0))
    part_b = pl.BlockSpec((DFT_SPLIT, seq), lambda i, j: (0, 0))
    data = pl.BlockSpec((seq, tn), lambda i, j: (0, j))
    return pl.pallas_call(
        functools.partial(_fnet_kernel, scale=scale),
        grid=(seq // tm, width // tn),
        in_specs=[part_a, part_a, part_b, part_b, data, data],
        out_specs=pl.BlockSpec((tm, tn), lambda i, j: (i, j)),
        out_shape=jax.ShapeDtypeStruct((seq, width), BF16),
        scratch_shapes=[pltpu.VMEM((tm, seq), BF16), pltpu.VMEM((tm, seq), BF16)],
        compiler_params=pltpu.CompilerParams(dimension_semantics=("arbitrary", "arbitrary"),
                                             vmem_limit_bytes=VMEM_LIMIT),
        name="fnet",
    )(*tables, uc, us)


def _post_body(x_ref, mla_ref, fn_ref, gqa_ref, o_ref, mod_ref, g2_ref, wo_ref, wg_ref, wu_ref, wd_ref, gf_ref, final):
    fn = fn_ref[...]
    if fn.shape[1] > FNET_WIDTH:
        fn = jnp.concatenate([fn[:, j:j + FNET_WIDTH] for j in range(0, fn.shape[1], FNET_WIDTH)], axis=0)
    mix = jnp.concatenate([mla_ref[...], fn, gqa_ref[...]], axis=-1)
    gate1 = mod_ref[:, 2 * D_MODEL:3 * D_MODEL]
    shift2 = mod_ref[:, 3 * D_MODEL:4 * D_MODEL]
    scale2 = mod_ref[:, 4 * D_MODEL:5 * D_MODEL]
    gate2 = mod_ref[:, 5 * D_MODEL:6 * D_MODEL]
    x = x_ref[...] + gate1 * _dot(mix, wo_ref[...])
    h = (_rms(x) * g2_ref[...] * (1.0 + scale2) + shift2).astype(BF16)
    g = _dot(h, wg_ref[...])
    u = _dot(h, wu_ref[...])
    a = (g * jax.nn.sigmoid(g) * u).astype(BF16)
    x = x + gate2 * _dot(a, wd_ref[...])
    if final:
        x = _rms(x) * gf_ref[...]
    o_ref[...] = x


def _post_kernel(*refs, final, n_first, with_next_mod):
    first, second, shared = refs[0:4], refs[4:8], refs[8:15]
    n_in = 18 if with_next_mod else 15
    o_first, o_second = refs[n_in:n_in + 2]
    step = pl.program_id(0)

    def run(group, o_ref):
        if with_next_mod:
            _ada_kernel(*refs[15:18], refs[n_in + 2])
        _post_body(*group, o_ref, *shared, final)

    @pl.when(step < n_first)
    def _first_group():
        run(first, o_first)

    @pl.when(step >= n_first)
    def _second_group():
        run(second, o_second)


def _post(groups, mod, g2, ffn, layer, g_final, *, final, tm, next_mod=None):
    (x1, _, _, _, seq1), (x2, _, _, _, seq2) = groups
    n1, n2 = x1.shape[0] // tm, x2.shape[0] // tm
    tiles_per_mod = n2 // (mod.shape[0] - 1)
    tile1 = lambda i: jnp.minimum(i, n1 - 1)
    tile2 = lambda i: jnp.maximum(i - n1, 0)
    resident = lambda arr: pl.BlockSpec(arr.shape, lambda i: (0, 0), pipeline_mode=pl.Buffered(1))

    def group_specs(seq, tile):
        tok = lambda w: pl.BlockSpec((tm, w), lambda i: (tile(i), 0))
        if tm <= seq:
            tps = seq // tm
            fnet_spec = pl.BlockSpec((tm, FNET_WIDTH), lambda i: (tile(i) % tps, tile(i) // tps))
        else:
            fnet_spec = pl.BlockSpec((seq, (tm // seq) * FNET_WIDTH), lambda i: (0, tile(i)))
        return [tok(D_MODEL), tok(MIX_TILE), fnet_spec, tok(MIX_TILE)], tok(D_MODEL)

    in1, out1 = group_specs(seq1, tile1)
    in2, out2 = group_specs(seq2, tile2)
    mod_spec = pl.BlockSpec((None, 1, 6 * D_MODEL),
                            lambda i: (jnp.where(i < n1, 0, 1 + tile2(i) // tiles_per_mod), 0, 0))
    in_specs = (in1 + in2 + [mod_spec, _layer_spec(g2, layer)] + [resident(w) for w in ffn]
                + [pl.BlockSpec((1, D_MODEL), lambda i: (0, 0))])
    args = [*groups[0][:4], *groups[1][:4], mod, g2, *ffn, g_final]
    out_specs = [out1, out2]
    out_shape = [jax.ShapeDtypeStruct(x1.shape, F32), jax.ShapeDtypeStruct(x2.shape, F32)]
    if next_mod is not None:
        cond_t, w_ada, b_ada = next_mod
        ada_in, ada_out, ada_shape = _ada_specs(cond_t, w_ada, layer + 1, w_ada.shape[2] // (n1 + n2))
        in_specs += ada_in
        args += [cond_t, w_ada, b_ada]
        out_specs.append(ada_out)
        out_shape.append(ada_shape)
    return pl.pallas_call(
        functools.partial(_post_kernel, final=final, n_first=n1, with_next_mod=next_mod is not None),
        grid=(n1 + n2,),
        in_specs=in_specs,
        out_specs=out_specs,
        out_shape=out_shape,
        compiler_params=pltpu.CompilerParams(dimension_semantics=("arbitrary",), vmem_limit_bytes=VMEM_LIMIT),
        name="post_final" if final else "post",
    )(*args)


def _prep_weights(g_norm1, g_norm2, g_q_a, w_q_up, g_kv_a, w_kv_up, g_q_head, g_k_head):
    row = lambda g: g.reshape(DEPTH, 1, -1)
    wq = w_q_up.reshape(DEPTH, MLA_Q_RANK, MLA_HEADS, MLA_QK_DIM)
    wq = jnp.concatenate([wq[..., MLA_NOPE_DIM:], wq[..., :MLA_NOPE_DIM]], axis=-1)
    wq = jnp.pad(wq, ((0, 0), (0, 0), (0, 0), (0, LANES - MLA_QK_DIM)))
    wkv = w_kv_up.reshape(DEPTH, MLA_KV_RANK, MLA_HEADS, MLA_NOPE_DIM + MLA_V_DIM)
    wk = jnp.pad(wkv[..., :MLA_NOPE_DIM], ((0, 0), (0, 0), (0, 0), (MLA_ROPE_DIM, LANES - MLA_QK_DIM)))
    wv = wkv[..., MLA_NOPE_DIM:]
    return {
        "g1": row(g_norm1), "g2": row(g_norm2), "g_q_a": row(g_q_a), "g_kv_a": row(g_kv_a),
        "g_q_head": row(jnp.tile(g_q_head, (1, GQA_HEADS))), "g_k_head": row(jnp.tile(g_k_head, (1, GQA_KV_HEADS))),
        "wq": wq.reshape(DEPTH, MLA_Q_RANK, MLA_PAD_WIDTH).astype(BF16),
        "wk": wk.reshape(DEPTH, MLA_KV_RANK, MLA_PAD_WIDTH).astype(BF16),
        "wv": wv.reshape(DEPTH, MLA_KV_RANK, MIX_TILE).astype(BF16),
    }


def kernel(x_prompt, x_sample, cache_mla_ckv, cache_mla_krope, cache_gqa_k, cache_gqa_v, c, c_ctx, w_ada, b_ada,
           g_norm1, g_norm2, w_in, g_q_a, w_q_up, g_kv_a, w_kv_up, g_q_head, g_k_head, w_out,
           w_ffn_gate, w_ffn_up, w_ffn_down, g_final):
    n_pb, p_seq, _ = x_prompt.shape
    n_sb, s_seq, _ = x_sample.shape
    consts = {"wdft": _channel_dft(), "mean_q": _group_mean_matrix(GQA_HEADS),
              "mean_k": _group_mean_matrix(GQA_KV_HEADS), "place": _rope_place_matrix(),
              "rope": _rope_tables(s_seq)}
    dft_p = _seq_dft_tables(p_seq)
    dft_s = _seq_dft_tables(s_seq)
    gf = g_final.reshape(1, D_MODEL)
    wts = _prep_weights(g_norm1, g_norm2, g_q_a, w_q_up, g_kv_a, w_kv_up, g_q_head, g_k_head)
    w_in_t = jnp.swapaxes(w_in, 1, 2)

    cond_t = jnp.concatenate([c_ctx[None, :], c], axis=0).T
    b_ada3 = b_ada.reshape(DEPTH, 1, 6 * D_MODEL)
    mod = _ada(cond_t, w_ada, b_ada3, 0)

    xp = x_prompt.reshape(n_pb * p_seq, D_MODEL)
    xs = x_sample.reshape(n_sb * s_seq, D_MODEL)
    b3 = lambda a, nb: a.reshape(nb, a.shape[0] // nb, a.shape[1])
    caches = None
    for l in range(DEPTH):
        final = l == DEPTH - 1
        mod_p = mod[0:1].reshape(1, 1, 6 * D_MODEL)
        mod_s = mod[1:].reshape(n_sb, 1, 6 * D_MODEL)

        km_c, vm_c, gk_c, gv_c, w_a, w_c, w_kr = _ctx_prep(
            l, cache_mla_ckv, cache_mla_krope, cache_gqa_k, cache_gqa_v, w_in_t, wts, consts)
        win_parts = {"w_a": w_a, "w_c": w_c, "w_kr": w_kr}
        qm, km, vm, uc_s, us_s, gq, gk, gv = _proj(xs, mod_s, wts, win_parts, l, consts, seq=s_seq, rope=True)
        mla_s, gqa_s, *ffn = _attention(b3(qm, n_sb), [(km_c, vm_c), (b3(km, n_sb), vm)],
                                        b3(gq, n_sb), [(gk_c, gv_c), (b3(gk, n_sb), gv)], tq=512,
                                        cast=(w_out, w_ffn_gate, w_ffn_up, w_ffn_down), layer=l)

        qm, km, vm, uc, us, gq, gk, gv, *caches = _proj(
            xp, mod_p, wts, win_parts, l, consts, seq=p_seq, rope=False, emit_cache=True, prev_cache=caches)
        mla_o, gqa_o = _attention(b3(qm, n_pb), [(b3(km, n_pb), vm)], b3(gq, n_pb), [(b3(gk, n_pb), gv)],
                                  tq=p_seq, bb=4, ahead=8)
        fn_p = _fnet(dft_p, uc, us, tm=p_seq, tn=1024)
        fn_s = _fnet(dft_s, uc_s, us_s, tm=512, tn=n_sb * FNET_WIDTH)
        flat = lambda a: a.reshape(-1, MIX_TILE)
        xp, xs, *mod_next = _post(
            [(xp, flat(mla_o), fn_p, flat(gqa_o), p_seq), (xs, flat(mla_s), fn_s, flat(gqa_s), s_seq)],
            mod.reshape(1 + n_sb, 1, 6 * D_MODEL), wts["g2"], ffn, l, gf, final=final, tm=POST_TILE,
            next_mod=None if final else (cond_t, w_ada, b_ada3))
        if mod_next:
            mod = mod_next[0]

    ckv_new, krope_t, gk_t, gv_t = caches
    heads = lambda a: jnp.swapaxes(a, 2, 3).reshape(n_pb, DEPTH, p_seq, GQA_KV_HEADS, GQA_HEAD_DIM)
    return (xp.reshape(n_pb, p_seq, D_MODEL), xs.reshape(n_sb, s_seq, D_MODEL),
            ckv_new, jnp.swapaxes(krope_t, 2, 3), heads(gk_t), heads(gv_t))
```

```python
import functools

import numpy as np
import jax
import jax.numpy as jnp
from jax import lax
from jax.experimental import pallas as pl
from jax.experimental.pallas import tpu as pltpu

D_MODEL = 1024
DEPTH = 2
GRID_W = 64
MLA_HEADS = 6
MLA_Q_RANK = 384
MLA_KV_RANK = 256
MLA_NOPE_DIM = 64
MLA_ROPE_DIM = 32
MLA_V_DIM = 64
MLA_QK_DIM = MLA_NOPE_DIM + MLA_ROPE_DIM
FNET_GROUPS = 4
FNET_GROUP_DIM = 64
FNET_WIDTH = FNET_GROUPS * FNET_GROUP_DIM
GQA_HEADS = 6
GQA_KV_HEADS = 2
GQA_HEAD_DIM = 64
GQA_GROUP = GQA_HEADS // GQA_KV_HEADS
GQA_WIDTH = GQA_HEADS * GQA_HEAD_DIM
GQA_KV_WIDTH = GQA_KV_HEADS * GQA_HEAD_DIM
D_FF = 2816
ROPE_THETA = 10000.0
EPS = 1e-6
LOG2_E = 1.4426950408889634

LANES = 128
HALF = LANES // 2
HEAD_PAIRS = MLA_HEADS // 2
MLA_PAD_WIDTH = MLA_HEADS * LANES
MIX_TILE = HEAD_PAIRS * LANES
VMEM_LIMIT = 52 * 1024 * 1024
POST_TILE = 512
PROJ_SUB = 256

_IN_A = MLA_Q_RANK + MLA_KV_RANK
_IN_C0 = _IN_A + MLA_ROPE_DIM
_IN_C = FNET_WIDTH + GQA_WIDTH + 2 * GQA_KV_WIDTH
_C_U, _C_GQ, _C_GK, _C_GV = 0, FNET_WIDTH, FNET_WIDTH + GQA_WIDTH, FNET_WIDTH + GQA_WIDTH + GQA_KV_WIDTH

BF16 = jnp.bfloat16
F32 = jnp.float32


def _rope_tables(n_tokens):
    t = np.arange(n_tokens)
    row = (t // GRID_W).astype(np.float64)
    col = (t % GRID_W).astype(np.float64)

    def angles(rot_dim):
        n_axis = rot_dim // 4
        inv = ROPE_THETA ** (-np.arange(n_axis, dtype=np.float64) / n_axis)
        return np.concatenate([row[:, None] * inv, col[:, None] * inv], axis=-1)

    def tables(ang, lane_to_pair, is_first, is_second):
        cos = np.where((is_first | is_second)[None, :], np.cos(ang)[:, lane_to_pair], 1.0)
        sin = np.sin(ang)[:, lane_to_pair]
        sin_a = np.where(is_second[None, :], sin, 0.0)
        sin_b = np.where(is_first[None, :], -sin, 0.0)
        return [cos, sin_a, sin_b]

    lane = np.arange(LANES)
    half_m = MLA_ROPE_DIM // 2
    in_rope = lane < MLA_ROPE_DIM
    first_m = lane < half_m
    second_m = in_rope & ~first_m
    pair_m = lane % half_m
    half_g = GQA_HEAD_DIM // 2
    first_g = (lane % GQA_HEAD_DIM) < half_g
    pair_g = lane % half_g
    tabs = (tables(angles(MLA_ROPE_DIM), pair_m, first_m, second_m)
            + tables(angles(GQA_HEAD_DIM), pair_g, first_g, ~first_g))
    return jnp.asarray(np.concatenate(tabs, axis=-1), dtype=F32)


def _channel_dft():
    c = np.arange(FNET_GROUP_DIM)
    ang = 2.0 * np.pi * np.outer(c, c) / FNET_GROUP_DIM
    eye = np.eye(FNET_GROUPS)
    table = np.concatenate([np.kron(eye, np.cos(ang)), np.kron(eye, np.sin(ang))], axis=1)
    return jnp.asarray(table, dtype=F32).astype(BF16)


def _group_mean_matrix(n_heads):
    return jnp.asarray(np.kron(np.eye(n_heads), np.full((GQA_HEAD_DIM, GQA_HEAD_DIM), 1.0 / GQA_HEAD_DIM)), dtype=BF16)


def _rope_place_matrix():
    return jnp.asarray(np.eye(MLA_ROPE_DIM, LANES), dtype=BF16)


DFT_SPLIT = 32


def _seq_dft_tables(seq):
    s = np.arange(seq)
    n_a = seq // DFT_SPLIT
    ang_a = 2.0 * np.pi * ((np.arange(n_a)[:, None] * s[None, :]) % n_a) / n_a
    ang_b = 2.0 * np.pi * ((np.arange(DFT_SPLIT)[:, None] * s[None, :]) % seq) / seq
    return tuple(jnp.asarray(t, dtype=F32) for t in (np.cos(ang_a), np.sin(ang_a), np.cos(ang_b), np.sin(ang_b)))


def _rms(x):
    return x * lax.rsqrt(jnp.mean(x * x, axis=-1, keepdims=True) + EPS)


def _dot(a, b):
    return jnp.dot(a, b, preferred_element_type=F32)


def _group_rms(x, mean_mat):
    ms = _dot((x * x).astype(BF16), mean_mat)
    return x * lax.rsqrt(ms + EPS)


def _rotate(x, cos, sin_a, sin_b, half):
    return x * cos + pltpu.roll(x, half, 1) * sin_a + pltpu.roll(x, LANES - half, 1) * sin_b


def _lower_half(rows):
    return lax.broadcasted_iota(jnp.int32, (rows, LANES), 1) < HALF


def _layer_spec(arr, layer):
    return pl.BlockSpec((None,) + arr.shape[1:], lambda i: (layer, 0, 0))


def _ada_kernel(ct_ref, w_ref, b_ref, o_ref):
    s = ct_ref[...]
    s = s * jax.nn.sigmoid(s)
    w = w_ref[...]
    for m in range(o_ref.shape[0]):
        o_ref[m:m + 1, :] = jnp.sum(w * s[:, m:m + 1], axis=0, keepdims=True) + b_ref[...]


def _ada_specs(cond_t, w_ada, layer, tn):
    n_cond = cond_t.shape[1]
    in_specs = [pl.BlockSpec((D_MODEL, n_cond), lambda j: (0, 0)),
                pl.BlockSpec((None, D_MODEL, tn), lambda j: (layer, 0, j)),
                pl.BlockSpec((None, 1, tn), lambda j: (layer, 0, j))]
    out_shape = jax.ShapeDtypeStruct((n_cond, w_ada.shape[2]), F32)
    return in_specs, pl.BlockSpec((n_cond, tn), lambda j: (0, j)), out_shape


def _ada(cond_t, w_ada, b_ada, layer, tn=1024):
    in_specs, out_spec, out_shape = _ada_specs(cond_t, w_ada, layer, tn)
    return pl.pallas_call(
        _ada_kernel,
        grid=(w_ada.shape[2] // tn,),
        in_specs=in_specs,
        out_specs=out_spec,
        out_shape=out_shape,
        compiler_params=pltpu.CompilerParams(dimension_semantics=("arbitrary",), vmem_limit_bytes=VMEM_LIMIT),
        name="ada",
    )(cond_t, w_ada, b_ada)


_PROJ_WEIGHTS = ("g1", "w_a", "w_c", "w_kr", "g_q_a", "wq", "g_kv_a", "wk", "wv", "g_q_head", "g_k_head")


def _proj_kernel(*refs, rope, emit_cache, n_prev, multi_seq):
    (x_ref, mod_ref, g1_ref, wa_ref, wc_ref, wkr_ref, gqa_ref, wq_ref, gkva_ref, wk_ref, wv_ref,
     gqh_ref, gkh_ref, wdft_ref, mq_ref, mk_ref) = refs[:16]
    refs = refs[16:]
    if rope:
        tab_ref, refs = refs[0], refs[1:]
    if n_prev:
        prev_refs, refs = refs[:4], refs[4:]
    qm_ref, km_ref, vm_ref, uc_ref, us_ref, gq_ref, gk_ref, gv_ref = refs[:8]
    cache_refs = refs[8:]

    shift = mod_ref[:, 0:D_MODEL]
    scale = mod_ref[:, D_MODEL:2 * D_MODEL]
    lower = _lower_half(PROJ_SUB)

    for r in range(x_ref.shape[0] // PROJ_SUB):
        rows = slice(r * PROJ_SUB, (r + 1) * PROJ_SUB)
        seq_cols = (slice(None), slice(r * FNET_WIDTH, (r + 1) * FNET_WIDTH)) if multi_seq else (rows, slice(None))
        t_idx = (r,) if multi_seq else (slice(None), rows)

        hb = (_rms(x_ref[rows, :]) * g1_ref[...] * (1.0 + scale) + shift).astype(BF16)
        pa = _dot(hb, wa_ref[...])
        pc = _dot(hb, wc_ref[...])
        kr = _dot(hb, wkr_ref[...])

        if rope:
            cos_m, sa_m, sb_m = (tab_ref[rows, i * LANES:(i + 1) * LANES] for i in range(3))
            cos_g, sa_g, sb_g = (tab_ref[rows, i * LANES:(i + 1) * LANES] for i in range(3, 6))

        gqn = _group_rms(pc[:, _C_GQ:_C_GQ + GQA_WIDTH], mq_ref[...]) * gqh_ref[...]
        tiles = []
        for p in range(HEAD_PAIRS):
            gp = gqn[:, p * LANES:(p + 1) * LANES]
            if rope:
                gp = _rotate(gp, cos_g, sa_g, sb_g, GQA_HEAD_DIM // 2)
            tiles.append(gp * (LOG2_E * GQA_HEAD_DIM ** -0.5))
        gq_ref[rows, 0:LANES] = jnp.where(lower, tiles[0], tiles[1]).astype(BF16)
        gq_ref[rows, LANES:2 * LANES] = jnp.where(
            lower, pltpu.roll(tiles[0], HALF, 1), pltpu.roll(tiles[2], HALF, 1)).astype(BF16)
        gq_ref[rows, 2 * LANES:] = jnp.where(lower, tiles[1], tiles[2]).astype(BF16)
        gkn = _group_rms(pc[:, _C_GK:_C_GK + GQA_KV_WIDTH], mk_ref[...]) * gkh_ref[...]
        gkr = _rotate(gkn, cos_g, sa_g, sb_g, GQA_HEAD_DIM // 2) if rope else gkn
        gv_t = pc[:, _C_GV:_C_GV + GQA_KV_WIDTH].T
        gk_ref[rows, :] = gkr.astype(BF16)
        gv_ref[t_idx] = gv_t.astype(BF16)

        cqn = _rms(pa[:, :MLA_Q_RANK]) * gqa_ref[...]
        q = _dot(cqn.astype(BF16), wq_ref[...])
        for hd in range(MLA_HEADS):
            qh = q[:, hd * LANES:(hd + 1) * LANES]
            if rope:
                qh = _rotate(qh, cos_m, sa_m, sb_m, MLA_ROPE_DIM // 2)
            qm_ref[rows, hd * LANES:(hd + 1) * LANES] = (qh * (LOG2_E * MLA_QK_DIM ** -0.5)).astype(BF16)

        ckvn = _rms(pa[:, MLA_Q_RANK:]) * gkva_ref[...]
        cb = ckvn.astype(BF16)
        kn = _dot(cb, wk_ref[...])
        krr = _rotate(kr, cos_m, sa_m, sb_m, MLA_ROPE_DIM // 2) if rope else kr
        for hd in range(MLA_HEADS):
            km_ref[rows, hd * LANES:(hd + 1) * LANES] = (kn[:, hd * LANES:(hd + 1) * LANES] + krr).astype(BF16)
        vm_ref[t_idx] = _dot(cb, wv_ref[...]).T.astype(BF16)

        ucs = _dot(pc[:, _C_U:_C_U + FNET_WIDTH].astype(BF16), wdft_ref[...])
        uc_ref[seq_cols] = ucs[:, :FNET_WIDTH].astype(BF16)
        us_ref[seq_cols] = ucs[:, FNET_WIDTH:].astype(BF16)

        if emit_cache:
            new = (ckvn, kr.T[:MLA_ROPE_DIM, :], gkn.T, gv_t)
            for i, out_ref in enumerate(cache_refs):
                if n_prev:
                    out_ref[r, :n_prev] = prev_refs[i][r]
                out_ref[r, n_prev] = new[i]


def _proj(x, mod, wts, win_parts, layer, consts, *, seq, rope, prev_cache=None, emit_cache=False, tm=512):
    n_tok = x.shape[0]
    n_tiles = n_tok // tm
    n_seq = n_tok // seq
    tiles_per_mod = n_tiles // mod.shape[0]
    multi_seq = tm > seq
    if multi_seq:
        assert seq == PROJ_SUB and not rope
        spt = tm // seq
        seq_major = pl.BlockSpec((seq, spt * FNET_WIDTH), lambda i: (0, i))
        transposed = lambda w: pl.BlockSpec((spt, w, seq), lambda i: (i, 0, 0))
    else:
        assert not emit_cache
        tps = seq // tm
        seq_major = pl.BlockSpec((tm, FNET_WIDTH), lambda i: (i % tps, i // tps))
        transposed = lambda w: pl.BlockSpec((None, w, tm), lambda i: (i // tps, 0, i % tps))
    const = lambda arr: pl.BlockSpec(arr.shape, lambda i: (0,) * arr.ndim)
    tok = lambda w: pl.BlockSpec((tm, w), lambda i: (i, 0))
    in_specs = [tok(D_MODEL), pl.BlockSpec((None, 1, 6 * D_MODEL), lambda i: (i // tiles_per_mod, 0, 0))]
    args = [x, mod]
    for name in _PROJ_WEIGHTS:
        arr = win_parts.get(name)
        in_specs.append(const(arr) if arr is not None else _layer_spec(wts[name], layer))
        args.append(arr if arr is not None else wts[name])
    for name in ("wdft", "mean_q", "mean_k"):
        in_specs.append(const(consts[name]))
        args.append(consts[name])
    if rope:
        in_specs.append(pl.BlockSpec((tm, 6 * LANES), lambda i: (i % tps, 0)))
        args.append(consts["rope"])
    out_specs = [tok(MLA_PAD_WIDTH), tok(MLA_PAD_WIDTH), transposed(MIX_TILE), seq_major, seq_major,
                 tok(GQA_WIDTH), tok(GQA_KV_WIDTH), transposed(GQA_KV_WIDTH)]
    out_shape = [jax.ShapeDtypeStruct((n_tok, MLA_PAD_WIDTH), BF16),
                 jax.ShapeDtypeStruct((n_tok, MLA_PAD_WIDTH), BF16),
                 jax.ShapeDtypeStruct((n_seq, MIX_TILE, seq), BF16),
                 jax.ShapeDtypeStruct((seq, n_seq * FNET_WIDTH), BF16),
                 jax.ShapeDtypeStruct((seq, n_seq * FNET_WIDTH), BF16),
                 jax.ShapeDtypeStruct((n_tok, GQA_WIDTH), BF16),
                 jax.ShapeDtypeStruct((n_tok, GQA_KV_WIDTH), BF16),
                 jax.ShapeDtypeStruct((n_seq, GQA_KV_WIDTH, seq), BF16)]
    n_prev = 0
    if emit_cache:
        n_prev = prev_cache[0].shape[1] if prev_cache is not None else 0
        layers = lambda n, tail: pl.BlockSpec((spt, n) + tail, lambda i: (i, 0, 0, 0))
        tails = ((seq, MLA_KV_RANK), (MLA_ROPE_DIM, seq), (GQA_KV_WIDTH, seq), (GQA_KV_WIDTH, seq))
        if n_prev:
            in_specs += [layers(n_prev, t) for t in tails]
            args += list(prev_cache)
        out_specs += [layers(n_prev + 1, t) for t in tails]
        out_shape += [jax.ShapeDtypeStruct((n_seq, n_prev + 1) + t, F32) for t in tails]
    return pl.pallas_call(
        functools.partial(_proj_kernel, rope=rope, emit_cache=emit_cache, n_prev=n_prev, multi_seq=multi_seq),
        grid=(n_tiles,),
        in_specs=in_specs,
        out_specs=out_specs,
        out_shape=out_shape,
        compiler_params=pltpu.CompilerParams(dimension_semantics=("arbitrary",), vmem_limit_bytes=VMEM_LIMIT),
        name="proj_rope" if rope else "proj_ctx",
    )(*args)


def _ctx_kernel(ckv_ref, kr_ref, gk_ref, gv_ref, wk_ref, wv_ref, place_ref, win_ref,
                km_ref, vm_ref, gko_ref, gvo_ref, wa_ref, wc_ref, wkr_ref):
    wt = win_ref[...]
    wa_ref[...] = wt[:_IN_A, :].T.astype(BF16)
    wc_ref[...] = wt[_IN_C0:, :].T.astype(BF16)
    kr_rows = jnp.concatenate([wt[_IN_A:_IN_C0, :], jnp.zeros((LANES - MLA_ROPE_DIM, wt.shape[1]), F32)], axis=0)
    wkr_ref[...] = kr_rows.T.astype(BF16)

    cb = ckv_ref[...].astype(BF16)
    kn = _dot(cb, wk_ref[...])
    kr = _dot(kr_ref[...].astype(BF16), place_ref[...])
    for hd in range(MLA_HEADS):
        km_ref[:, hd * LANES:(hd + 1) * LANES] = (kn[:, hd * LANES:(hd + 1) * LANES] + kr).astype(BF16)
    vm_ref[...] = _dot(cb, wv_ref[...]).T.astype(BF16)
    gko_ref[...] = gk_ref[...].astype(BF16)
    gvo_ref[...] = gv_ref[...].T.astype(BF16)


def _ctx_prep(layer, cache_ckv, cache_krope, cache_gk, cache_gv, w_in_t, wts, consts):
    nb, _, past, _ = cache_ckv.shape
    rows = D_MODEL // nb
    cache = lambda w: pl.BlockSpec((None, None, past, w), lambda b: (b, layer, 0, 0))
    out = lambda w: pl.BlockSpec((None, past, w), lambda b: (b, 0, 0))
    out_t = lambda w: pl.BlockSpec((None, w, past), lambda b: (b, 0, 0))
    part = lambda w: pl.BlockSpec((rows, w), lambda b: (b, 0))
    part_widths = (_IN_A, _IN_C, LANES)
    return pl.pallas_call(
        _ctx_kernel,
        grid=(nb,),
        in_specs=[cache(MLA_KV_RANK), cache(MLA_ROPE_DIM), cache(GQA_KV_WIDTH), cache(GQA_KV_WIDTH),
                  _layer_spec(wts["wk"], layer), _layer_spec(wts["wv"], layer),
                  pl.BlockSpec((MLA_ROPE_DIM, LANES), lambda b: (0, 0)),
                  pl.BlockSpec((None, w_in_t.shape[1], rows), lambda b: (layer, 0, b))],
        out_specs=[out(MLA_PAD_WIDTH), out_t(MIX_TILE), out(GQA_KV_WIDTH), out_t(GQA_KV_WIDTH)]
                  + [part(w) for w in part_widths],
        out_shape=[jax.ShapeDtypeStruct((nb, past, MLA_PAD_WIDTH), BF16),
                   jax.ShapeDtypeStruct((nb, MIX_TILE, past), BF16),
                   jax.ShapeDtypeStruct((nb, past, GQA_KV_WIDTH), BF16),
                   jax.ShapeDtypeStruct((nb, GQA_KV_WIDTH, past), BF16)]
                  + [jax.ShapeDtypeStruct((D_MODEL, w), BF16) for w in part_widths],
        compiler_params=pltpu.CompilerParams(dimension_semantics=("arbitrary",), vmem_limit_bytes=VMEM_LIMIT),
        name="ctx_prep",
    )(cache_ckv, cache_krope, cache_gk.reshape(nb, DEPTH, past, GQA_KV_WIDTH),
      cache_gv.reshape(nb, DEPTH, past, GQA_KV_WIDTH), wts["wk"], wts["wv"], consts["place"], w_in_t)


def _attn_kernel(*refs, n_seg, ahead, n_cast):
    n_in = 1 + 2 * n_seg
    outs_at = 2 * n_in + n_cast
    families = []
    for f, mla in enumerate((True, False)):
        ins = refs[f * n_in:(f + 1) * n_in]
        families.append((mla, ins[0], ins[1:1 + n_seg], ins[1 + n_seg:], refs[outs_at + f]))
    for c in range(n_cast):
        refs[outs_at + 2 + c][...] = refs[2 * n_in + c][...].astype(BF16)
    n_seqs, tq = refs[0].shape[:2]
    lower = _lower_half(tq)
    top = lax.broadcasted_iota(jnp.int32, (LANES, tq), 0) < HALF

    def scores_t(f, b, p, hd):
        mla, q_ref, k_refs, _, _ = families[f]
        if mla:
            cols = slice((2 * p + hd) * LANES, (2 * p + hd + 1) * LANES)
            qh = q_ref[b, :, cols]
            keys = [k_ref[b, :, cols] for k_ref in k_refs]
        else:
            q = q_ref[b, :, p * LANES:(p + 1) * LANES]
            qh = jnp.where(lower if hd == 0 else ~lower, q, jnp.zeros_like(q))
            keys = [k_ref[b] for k_ref in k_refs]
        return [lax.dot_general(k, qh, (((1,), (1,)), ((), ())), preferred_element_type=F32) for k in keys]

    def softmax_t(ss):
        m = functools.reduce(jnp.maximum, [jnp.max(s, axis=0, keepdims=True) for s in ss])
        es = [jnp.exp2(s - m) for s in ss]
        denom = functools.reduce(jnp.add, [jnp.sum(e, axis=0, keepdims=True) for e in es])
        return [e.astype(BF16) for e in es], denom

    def weighted_values_t(f, b, p, es, denom):
        mla, _, _, vt_refs, _ = families[f]
        rows = slice(p * LANES, (p + 1) * LANES) if mla else slice(None)
        acc = functools.reduce(jnp.add, [_dot(vt_ref[b, rows, :], e) for e, vt_ref in zip(es, vt_refs)])
        return acc * (1.0 / denom)

    work = [(f, b, p, hd) for f in range(2) for b in range(n_seqs) for p in range(HEAD_PAIRS) for hd in range(2)]
    pending = [scores_t(*w) for w in work[:ahead]]
    outs = {}
    for i, (f, b, p, hd) in enumerate(work):
        es, denom = softmax_t(pending.pop(0))
        if i + ahead < len(work):
            pending.append(scores_t(*work[i + ahead]))
        outs[f, b, p, hd] = weighted_values_t(f, b, p, es, denom)
        if (p, hd) != (HEAD_PAIRS - 1, 1):
            continue
        mla, o_ref = families[f][0], families[f][4]
        tiles = [jnp.where(top, outs[f, b, t, 0], outs[f, b, t, 1]).T for t in range(HEAD_PAIRS)]
        if not mla:
            t1 = pltpu.roll(tiles[1], HALF, 1)
            tiles = [jnp.where(lower, tiles[0], t1), jnp.where(lower, tiles[2], tiles[0]),
                     jnp.where(lower, t1, tiles[2])]
        for t in range(HEAD_PAIRS):
            o_ref[b, :, t * LANES:(t + 1) * LANES] = tiles[t].astype(o_ref.dtype)


def _attention(mla_q, mla_kv, gqa_q, gqa_kv, *, tq, bb=1, ahead=2, cast=(), layer=0):
    n_batch, sq = mla_q.shape[:2]
    n_seg = len(mla_kv)
    n_q = sq // tq
    steps = (n_batch // bb) * n_q
    whole = lambda a: pl.BlockSpec((bb,) + a.shape[1:], lambda b, i: (b, 0, 0))
    in_specs, args = [], []
    for q, kv in ((mla_q, mla_kv), (gqa_q, gqa_kv)):
        in_specs += [pl.BlockSpec((bb, tq, q.shape[2]), lambda b, i: (b, i, 0))]
        in_specs += [whole(k) for k, _ in kv] + [whole(v) for _, v in kv]
        args += [q] + [k for k, _ in kv] + [v for _, v in kv]
    out_spec = pl.BlockSpec((bb, tq, MIX_TILE), lambda b, i: (b, i, 0))
    out_specs = [out_spec, out_spec]
    out_shape = [jax.ShapeDtypeStruct((n_batch, sq, MIX_TILE), BF16)] * 2
    for w in cast:
        rows, cols = w.shape[1] // steps, w.shape[2]
        in_specs.append(pl.BlockSpec((None, rows, cols), lambda b, i: (layer, b * n_q + i, 0)))
        out_specs.append(pl.BlockSpec((rows, cols), lambda b, i: (b * n_q + i, 0)))
        out_shape.append(jax.ShapeDtypeStruct(w.shape[1:], BF16))
        args.append(w)
    return pl.pallas_call(
        functools.partial(_attn_kernel, n_seg=n_seg, ahead=ahead, n_cast=len(cast)),
        grid=(n_batch // bb, n_q),
        in_specs=in_specs,
        out_specs=out_specs,
        out_shape=out_shape,
        compiler_params=pltpu.CompilerParams(dimension_semantics=("arbitrary",) * 2, vmem_limit_bytes=VMEM_LIMIT),
        name=f"attn_s{n_seg}",
    )(*args)


def _fnet_kernel(ca_ref, sa_ref, cb_ref, sb_ref, uc_ref, us_ref, o_ref, tc_ref, ts_ref, *, scale):
    @pl.when(pl.program_id(1) == 0)
    def _build_twiddles():
        cb, sb = cb_ref[...], sb_ref[...]
        for a in range(ca_ref.shape[0]):
            ca, sa = ca_ref[a:a + 1, :], sa_ref[a:a + 1, :]
            rows = slice(a * DFT_SPLIT, (a + 1) * DFT_SPLIT)
            tc_ref[rows, :] = (ca * cb - sa * sb).astype(BF16)
            ts_ref[rows, :] = (sa * cb + ca * sb).astype(BF16)

    acc = _dot(tc_ref[...], uc_ref[...]) - _dot(ts_ref[...], us_ref[...])
    o_ref[...] = (acc * scale).astype(o_ref.dtype)


def _fnet(tables, uc, us, tm, tn):
    seq, width = uc.shape
    scale = float((seq * FNET_GROUP_DIM) ** -0.5)
    n_a = tm // DFT_SPLIT
    part_a = pl.BlockSpec((n_a, seq), lambda i, j: (i, 0))
    part_b = pl.BlockSpec((DFT_SPLIT, seq), lambda i, j: (0, 0))
    data = pl.BlockSpec((seq, tn), lambda i, j: (0, j))
    return pl.pallas_call(
        functools.partial(_fnet_kernel, scale=scale),
        grid=(seq // tm, width // tn),
        in_specs=[part_a, part_a, part_b, part_b, data, data],
        out_specs=pl.BlockSpec((tm, tn), lambda i, j: (i, j)),
        out_shape=jax.ShapeDtypeStruct((seq, width), BF16),
        scratch_shapes=[pltpu.VMEM((tm, seq), BF16), pltpu.VMEM((tm, seq), BF16)],
        compiler_params=pltpu.CompilerParams(dimension_semantics=("arbitrary", "arbitrary"),
                                             vmem_limit_bytes=VMEM_LIMIT),
        name="fnet",
    )(*tables, uc, us)


def _post_body(x_ref, mla_ref, fn_ref, gqa_ref, o_ref, mod_ref, g2_ref, wo_ref, wg_ref, wu_ref, wd_ref, gf_ref, final):
    fn = fn_ref[...]
    if fn.shape[1] > FNET_WIDTH:
        fn = jnp.concatenate([fn[:, j:j + FNET_WIDTH] for j in range(0, fn.shape[1], FNET_WIDTH)], axis=0)
    mix = jnp.concatenate([mla_ref[...], fn, gqa_ref[...]], axis=-1)
    gate1 = mod_ref[:, 2 * D_MODEL:3 * D_MODEL]
    shift2 = mod_ref[:, 3 * D_MODEL:4 * D_MODEL]
    scale2 = mod_ref[:, 4 * D_MODEL:5 * D_MODEL]
    gate2 = mod_ref[:, 5 * D_MODEL:6 * D_MODEL]
    x = x_ref[...] + gate1 * _dot(mix, wo_ref[...])
    h = (_rms(x) * g2_ref[...] * (1.0 + scale2) + shift2).astype(BF16)
    g = _dot(h, wg_ref[...])
    u = _dot(h, wu_ref[...])
    a = (g * jax.nn.sigmoid(g) * u).astype(BF16)
    x = x + gate2 * _dot(a, wd_ref[...])
    if final:
        x = _rms(x) * gf_ref[...]
    o_ref[...] = x


def _post_kernel(*refs, final, n_first, with_next_mod):
    first, second, shared = refs[0:4], refs[4:8], refs[8:15]
    n_in = 18 if with_next_mod else 15
    o_first, o_second = refs[n_in:n_in + 2]
    step = pl.program_id(0)

    def run(group, o_ref):
        if with_next_mod:
            _ada_kernel(*refs[15:18], refs[n_in + 2])
        _post_body(*group, o_ref, *shared, final)

    @pl.when(step < n_first)
    def _first_group():
        run(first, o_first)

    @pl.when(step >= n_first)
    def _second_group():
        run(second, o_second)


def _post(groups, mod, g2, ffn, layer, g_final, *, final, tm, next_mod=None):
    (x1, _, _, _, seq1), (x2, _, _, _, seq2) = groups
    n1, n2 = x1.shape[0] // tm, x2.shape[0] // tm
    tiles_per_mod = n2 // (mod.shape[0] - 1)
    tile1 = lambda i: jnp.minimum(i, n1 - 1)
    tile2 = lambda i: jnp.maximum(i - n1, 0)
    resident = lambda arr: pl.BlockSpec(arr.shape, lambda i: (0, 0), pipeline_mode=pl.Buffered(1))

    def group_specs(seq, tile):
        tok = lambda w: pl.BlockSpec((tm, w), lambda i: (tile(i), 0))
        if tm <= seq:
            tps = seq // tm
            fnet_spec = pl.BlockSpec((tm, FNET_WIDTH), lambda i: (tile(i) % tps, tile(i) // tps))
        else:
            fnet_spec = pl.BlockSpec((seq, (tm // seq) * FNET_WIDTH), lambda i: (0, tile(i)))
        return [tok(D_MODEL), tok(MIX_TILE), fnet_spec, tok(MIX_TILE)], tok(D_MODEL)

    in1, out1 = group_specs(seq1, tile1)
    in2, out2 = group_specs(seq2, tile2)
    mod_spec = pl.BlockSpec((None, 1, 6 * D_MODEL),
                            lambda i: (jnp.where(i < n1, 0, 1 + tile2(i) // tiles_per_mod), 0, 0))
    in_specs = (in1 + in2 + [mod_spec, _layer_spec(g2, layer)] + [resident(w) for w in ffn]
                + [pl.BlockSpec((1, D_MODEL), lambda i: (0, 0))])
    args = [*groups[0][:4], *groups[1][:4], mod, g2, *ffn, g_final]
    out_specs = [out1, out2]
    out_shape = [jax.ShapeDtypeStruct(x1.shape, F32), jax.ShapeDtypeStruct(x2.shape, F32)]
    if next_mod is not None:
        cond_t, w_ada, b_ada = next_mod
        ada_in, ada_out, ada_shape = _ada_specs(cond_t, w_ada, layer + 1, w_ada.shape[2] // (n1 + n2))
        in_specs += ada_in
        args += [cond_t, w_ada, b_ada]
        out_specs.append(ada_out)
        out_shape.append(ada_shape)
    return pl.pallas_call(
        functools.partial(_post_kernel, final=final, n_first=n1, with_next_mod=next_mod is not None),
        grid=(n1 + n2,),
        in_specs=in_specs,
        out_specs=out_specs,
        out_shape=out_shape,
        compiler_params=pltpu.CompilerParams(dimension_semantics=("arbitrary",), vmem_limit_bytes=VMEM_LIMIT),
        name="post_final" if final else "post",
    )(*args)


def _prep_weights(g_norm1, g_norm2, g_q_a, w_q_up, g_kv_a, w_kv_up, g_q_head, g_k_head):
    row = lambda g: g.reshape(DEPTH, 1, -1)
    wq = w_q_up.reshape(DEPTH, MLA_Q_RANK, MLA_HEADS, MLA_QK_DIM)
    wq = jnp.concatenate([wq[..., MLA_NOPE_DIM:], wq[..., :MLA_NOPE_DIM]], axis=-1)
    wq = jnp.pad(wq, ((0, 0), (0, 0), (0, 0), (0, LANES - MLA_QK_DIM)))
    wkv = w_kv_up.reshape(DEPTH, MLA_KV_RANK, MLA_HEADS, MLA_NOPE_DIM + MLA_V_DIM)
    wk = jnp.pad(wkv[..., :MLA_NOPE_DIM], ((0, 0), (0, 0), (0, 0), (MLA_ROPE_DIM, LANES - MLA_QK_DIM)))
    wv = wkv[..., MLA_NOPE_DIM:]
    return {
        "g1": row(g_norm1), "g2": row(g_norm2), "g_q_a": row(g_q_a), "g_kv_a": row(g_kv_a),
        "g_q_head": row(jnp.tile(g_q_head, (1, GQA_HEADS))), "g_k_head": row(jnp.tile(g_k_head, (1, GQA_KV_HEADS))),
        "wq": wq.reshape(DEPTH, MLA_Q_RANK, MLA_PAD_WIDTH).astype(BF16),
        "wk": wk.reshape(DEPTH, MLA_KV_RANK, MLA_PAD_WIDTH).astype(BF16),
        "wv": wv.reshape(DEPTH, MLA_KV_RANK, MIX_TILE).astype(BF16),
    }


def kernel(x_prompt, x_sample, cache_mla_ckv, cache_mla_krope, cache_gqa_k, cache_gqa_v, c, c_ctx, w_ada, b_ada,
           g_norm1, g_norm2, w_in, g_q_a, w_q_up, g_kv_a, w_kv_up, g_q_head, g_k_head, w_out,
           w_ffn_gate, w_ffn_up, w_ffn_down, g_final):
    n_pb, p_seq, _ = x_prompt.shape
    n_sb, s_seq, _ = x_sample.shape
    consts = {"wdft": _channel_dft(), "mean_q": _group_mean_matrix(GQA_HEADS),
              "mean_k": _group_mean_matrix(GQA_KV_HEADS), "place": _rope_place_matrix(),
              "rope": _rope_tables(s_seq)}
    dft_p = _seq_dft_tables(p_seq)
    dft_s = _seq_dft_tables(s_seq)
    gf = g_final.reshape(1, D_MODEL)
    wts = _prep_weights(g_norm1, g_norm2, g_q_a, w_q_up, g_kv_a, w_kv_up, g_q_head, g_k_head)
    w_in_t = jnp.swapaxes(w_in, 1, 2)

    cond_t = jnp.concatenate([c_ctx[None, :], c], axis=0).T
    b_ada3 = b_ada.reshape(DEPTH, 1, 6 * D_MODEL)
    mod = _ada(cond_t, w_ada, b_ada3, 0)

    xp = x_prompt.reshape(n_pb * p_seq, D_MODEL)
    xs = x_sample.reshape(n_sb * s_seq, D_MODEL)
    b3 = lambda a, nb: a.reshape(nb, a.shape[0] // nb, a.shape[1])
    caches = None
    for l in range(DEPTH):
        final = l == DEPTH - 1
        mod_p = mod[0:1].reshape(1, 1, 6 * D_MODEL)
        mod_s = mod[1:].reshape(n_sb, 1, 6 * D_MODEL)

        km_c, vm_c, gk_c, gv_c, w_a, w_c, w_kr = _ctx_prep(
            l, cache_mla_ckv, cache_mla_krope, cache_gqa_k, cache_gqa_v, w_in_t, wts, consts)
        win_parts = {"w_a": w_a, "w_c": w_c, "w_kr": w_kr}
        qm, km, vm, uc_s, us_s, gq, gk, gv = _proj(xs, mod_s, wts, win_parts, l, consts, seq=s_seq, rope=True)
        mla_s, gqa_s, *ffn = _attention(b3(qm, n_sb), [(km_c, vm_c), (b3(km, n_sb), vm)],
                                        b3(gq, n_sb), [(gk_c, gv_c), (b3(gk, n_sb), gv)], tq=512,
                                        cast=(w_out, w_ffn_gate, w_ffn_up, w_ffn_down), layer=l)

        qm, km, vm, uc, us, gq, gk, gv, *caches = _proj(
            xp, mod_p, wts, win_parts, l, consts, seq=p_seq, rope=False, emit_cache=True, prev_cache=caches)
        mla_o, gqa_o = _attention(b3(qm, n_pb), [(b3(km, n_pb), vm)], b3(gq, n_pb), [(b3(gk, n_pb), gv)],
                                  tq=p_seq, bb=4, ahead=8)
        fn_p = _fnet(dft_p, uc, us, tm=p_seq, tn=1024)
        fn_s = _fnet(dft_s, uc_s, us_s, tm=512, tn=n_sb * FNET_WIDTH)
        flat = lambda a: a.reshape(-1, MIX_TILE)
        xp, xs, *mod_next = _post(
            [(xp, flat(mla_o), fn_p, flat(gqa_o), p_seq), (xs, flat(mla_s), fn_s, flat(gqa_s), s_seq)],
            mod.reshape(1 + n_sb, 1, 6 * D_MODEL), wts["g2"], ffn, l, gf, final=final, tm=POST_TILE,
            next_mod=None if final else (cond_t, w_ada, b_ada3))
        if mod_next:
            mod = mod_next[0]

    ckv_new, krope_t, gk_t, gv_t = caches
    heads = lambda a: jnp.swapaxes(a, 2, 3).reshape(n_pb, DEPTH, p_seq, GQA_KV_HEADS, GQA_HEAD_DIM)
    return (xp.reshape(n_pb, p_seq, D_MODEL), xs.reshape(n_sb, s_seq, D_MODEL),
            ckv_new, jnp.swapaxes(krope_t, 2, 3), heads(gk_t), heads(gv_t))
```

```python
import functools

import numpy as np
import jax
import jax.numpy as jnp
from jax import lax
from jax.experimental import pallas as pl
from jax.experimental.pallas import tpu as pltpu

D_MODEL = 1024
DEPTH = 2
GRID_W = 64
MLA_HEADS = 6
MLA_Q_RANK = 384
MLA_KV_RANK = 256
MLA_NOPE_DIM = 64
MLA_ROPE_DIM = 32
MLA_V_DIM = 64
MLA_QK_DIM = MLA_NOPE_DIM + MLA_ROPE_DIM
FNET_GROUPS = 4
FNET_GROUP_DIM = 64
FNET_WIDTH = FNET_GROUPS * FNET_GROUP_DIM
GQA_HEADS = 6
GQA_KV_HEADS = 2
GQA_HEAD_DIM = 64
GQA_GROUP = GQA_HEADS // GQA_KV_HEADS
GQA_WIDTH = GQA_HEADS * GQA_HEAD_DIM
GQA_KV_WIDTH = GQA_KV_HEADS * GQA_HEAD_DIM
D_FF = 2816
ROPE_THETA = 10000.0
EPS = 1e-6
LOG2_E = 1.4426950408889634

LANES = 128
HALF = LANES // 2
HEAD_PAIRS = MLA_HEADS // 2
MLA_PAD_WIDTH = MLA_HEADS * LANES
MIX_TILE = HEAD_PAIRS * LANES
MIB = 1024 * 1024
VMEM_MIB = {"ada": 12, "ctx_prep": 16, "proj": 26, "attn_prompt": 18, "attn_latent": 46, "fnet": 12, "post": 52}


def _params(n_grid_dims, kernel_name):
    return pltpu.CompilerParams(dimension_semantics=("arbitrary",) * n_grid_dims,
                                vmem_limit_bytes=VMEM_MIB[kernel_name] * MIB)
POST_TILE = 512
PROJ_SUB = 256

_IN_A = MLA_Q_RANK + MLA_KV_RANK
_IN_C0 = _IN_A + MLA_ROPE_DIM
_IN_C = FNET_WIDTH + GQA_WIDTH + 2 * GQA_KV_WIDTH
_C_U, _C_GQ, _C_GK, _C_GV = 0, FNET_WIDTH, FNET_WIDTH + GQA_WIDTH, FNET_WIDTH + GQA_WIDTH + GQA_KV_WIDTH

BF16 = jnp.bfloat16
F32 = jnp.float32


def _rope_tables(n_tokens):
    t = np.arange(n_tokens)
    row = (t // GRID_W).astype(np.float64)
    col = (t % GRID_W).astype(np.float64)

    def angles(rot_dim):
        n_axis = rot_dim // 4
        inv = ROPE_THETA ** (-np.arange(n_axis, dtype=np.float64) / n_axis)
        return np.concatenate([row[:, None] * inv, col[:, None] * inv], axis=-1)

    def tables(ang, lane_to_pair, is_first, is_second):
        cos = np.where((is_first | is_second)[None, :], np.cos(ang)[:, lane_to_pair], 1.0)
        sin = np.sin(ang)[:, lane_to_pair]
        sin_a = np.where(is_second[None, :], sin, 0.0)
        sin_b = np.where(is_first[None, :], -sin, 0.0)
        return [cos, sin_a, sin_b]

    lane = np.arange(LANES)
    half_m = MLA_ROPE_DIM // 2
    in_rope = lane < MLA_ROPE_DIM
    first_m = lane < half_m
    second_m = in_rope & ~first_m
    pair_m = lane % half_m
    half_g = GQA_HEAD_DIM // 2
    first_g = (lane % GQA_HEAD_DIM) < half_g
    pair_g = lane % half_g
    tabs = (tables(angles(MLA_ROPE_DIM), pair_m, first_m, second_m)
            + tables(angles(GQA_HEAD_DIM), pair_g, first_g, ~first_g))
    return jnp.asarray(np.concatenate(tabs, axis=-1), dtype=F32)


def _channel_dft():
    c = np.arange(FNET_GROUP_DIM)
    ang = 2.0 * np.pi * np.outer(c, c) / FNET_GROUP_DIM
    eye = np.eye(FNET_GROUPS)
    table = np.concatenate([np.kron(eye, np.cos(ang)), np.kron(eye, np.sin(ang))], axis=1)
    return jnp.asarray(table, dtype=F32).astype(BF16)


def _group_mean_matrix(n_heads):
    return jnp.asarray(np.kron(np.eye(n_heads), np.full((GQA_HEAD_DIM, GQA_HEAD_DIM), 1.0 / GQA_HEAD_DIM)), dtype=BF16)


def _rope_place_matrix():
    return jnp.asarray(np.eye(MLA_ROPE_DIM, LANES), dtype=BF16)


DFT_SPLIT = 32


def _seq_dft_tables(seq):
    s = np.arange(seq)
    n_a = seq // DFT_SPLIT
    ang_a = 2.0 * np.pi * ((np.arange(n_a)[:, None] * s[None, :]) % n_a) / n_a
    ang_b = 2.0 * np.pi * ((np.arange(DFT_SPLIT)[:, None] * s[None, :]) % seq) / seq
    return tuple(jnp.asarray(t, dtype=F32) for t in (np.cos(ang_a), np.sin(ang_a), np.cos(ang_b), np.sin(ang_b)))


def _rms(x):
    return x * lax.rsqrt(jnp.mean(x * x, axis=-1, keepdims=True) + EPS)


def _dot(a, b):
    return jnp.dot(a, b, preferred_element_type=F32)


def _group_rms(x, mean_mat):
    ms = _dot((x * x).astype(BF16), mean_mat)
    return x * lax.rsqrt(ms + EPS)


def _rotate(x, cos, sin_a, sin_b, half):
    return x * cos + pltpu.roll(x, half, 1) * sin_a + pltpu.roll(x, LANES - half, 1) * sin_b


def _lower_half(rows):
    return lax.broadcasted_iota(jnp.int32, (rows, LANES), 1) < HALF


def _layer_spec(arr, layer):
    return pl.BlockSpec((None,) + arr.shape[1:], lambda i: (layer, 0, 0))


def _ada_kernel(ct_ref, w_ref, b_ref, o_ref):
    s = ct_ref[...]
    s = s * jax.nn.sigmoid(s)
    w = w_ref[...]
    for m in range(o_ref.shape[0]):
        o_ref[m:m + 1, :] = jnp.sum(w * s[:, m:m + 1], axis=0, keepdims=True) + b_ref[...]


def _ada_specs(cond_t, w_ada, layer, tn):
    n_cond = cond_t.shape[1]
    in_specs = [pl.BlockSpec((D_MODEL, n_cond), lambda j: (0, 0)),
                pl.BlockSpec((None, D_MODEL, tn), lambda j: (layer, 0, j)),
                pl.BlockSpec((None, 1, tn), lambda j: (layer, 0, j))]
    out_shape = jax.ShapeDtypeStruct((n_cond, w_ada.shape[2]), F32)
    return in_specs, pl.BlockSpec((n_cond, tn), lambda j: (0, j)), out_shape


def _ada(cond_t, w_ada, b_ada, layer, tn=1024):
    in_specs, out_spec, out_shape = _ada_specs(cond_t, w_ada, layer, tn)
    return pl.pallas_call(
        _ada_kernel,
        grid=(w_ada.shape[2] // tn,),
        in_specs=in_specs,
        out_specs=out_spec,
        out_shape=out_shape,
        compiler_params=_params(1, "ada"),
        name="ada",
    )(cond_t, w_ada, b_ada)


_PROJ_WEIGHTS = ("g1", "w_a", "w_c", "w_kr", "g_q_a", "wq", "g_kv_a", "wk", "wv", "g_q_head", "g_k_head")


def _proj_kernel(*refs, rope, emit_cache, n_prev, multi_seq):
    (x_ref, mod_ref, g1_ref, wa_ref, wc_ref, wkr_ref, gqa_ref, wq_ref, gkva_ref, wk_ref, wv_ref,
     gqh_ref, gkh_ref, wdft_ref, mq_ref, mk_ref) = refs[:16]
    refs = refs[16:]
    if rope:
        tab_ref, refs = refs[0], refs[1:]
    if n_prev:
        prev_refs, refs = refs[:4], refs[4:]
    qm_ref, km_ref, vm_ref, uc_ref, us_ref, gq_ref, gk_ref, gv_ref = refs[:8]
    cache_refs = refs[8:]

    shift = mod_ref[:, 0:D_MODEL]
    scale = mod_ref[:, D_MODEL:2 * D_MODEL]
    lower = _lower_half(PROJ_SUB)

    for r in range(x_ref.shape[0] // PROJ_SUB):
        rows = slice(r * PROJ_SUB, (r + 1) * PROJ_SUB)
        seq_cols = (slice(None), slice(r * FNET_WIDTH, (r + 1) * FNET_WIDTH)) if multi_seq else (rows, slice(None))
        t_idx = (r,) if multi_seq else (slice(None), rows)

        hb = (_rms(x_ref[rows, :]) * g1_ref[...] * (1.0 + scale) + shift).astype(BF16)
        pa = _dot(hb, wa_ref[...])
        pc = _dot(hb, wc_ref[...])
        kr = _dot(hb, wkr_ref[...])

        if rope:
            cos_m, sa_m, sb_m = (tab_ref[rows, i * LANES:(i + 1) * LANES] for i in range(3))
            cos_g, sa_g, sb_g = (tab_ref[rows, i * LANES:(i + 1) * LANES] for i in range(3, 6))

        gqn = _group_rms(pc[:, _C_GQ:_C_GQ + GQA_WIDTH], mq_ref[...]) * gqh_ref[...]
        tiles = []
        for p in range(HEAD_PAIRS):
            gp = gqn[:, p * LANES:(p + 1) * LANES]
            if rope:
                gp = _rotate(gp, cos_g, sa_g, sb_g, GQA_HEAD_DIM // 2)
            tiles.append(gp * (LOG2_E * GQA_HEAD_DIM ** -0.5))
        gq_ref[rows, 0:LANES] = jnp.where(lower, tiles[0], tiles[1]).astype(BF16)
        gq_ref[rows, LANES:2 * LANES] = jnp.where(
            lower, pltpu.roll(tiles[0], HALF, 1), pltpu.roll(tiles[2], HALF, 1)).astype(BF16)
        gq_ref[rows, 2 * LANES:] = jnp.where(lower, tiles[1], tiles[2]).astype(BF16)
        gkn = _group_rms(pc[:, _C_GK:_C_GK + GQA_KV_WIDTH], mk_ref[...]) * gkh_ref[...]
        gkr = _rotate(gkn, cos_g, sa_g, sb_g, GQA_HEAD_DIM // 2) if rope else gkn
        gv_t = pc[:, _C_GV:_C_GV + GQA_KV_WIDTH].T
        gk_ref[rows, :] = gkr.astype(BF16)
        gv_ref[t_idx] = gv_t.astype(BF16)

        cqn = _rms(pa[:, :MLA_Q_RANK]) * gqa_ref[...]
        q = _dot(cqn.astype(BF16), wq_ref[...])
        for hd in range(MLA_HEADS):
            qh = q[:, hd * LANES:(hd + 1) * LANES]
            if rope:
                qh = _rotate(qh, cos_m, sa_m, sb_m, MLA_ROPE_DIM // 2)
            qm_ref[rows, hd * LANES:(hd + 1) * LANES] = (qh * (LOG2_E * MLA_QK_DIM ** -0.5)).astype(BF16)

        ckvn = _rms(pa[:, MLA_Q_RANK:]) * gkva_ref[...]
        cb = ckvn.astype(BF16)
        kn = _dot(cb, wk_ref[...])
        krr = _rotate(kr, cos_m, sa_m, sb_m, MLA_ROPE_DIM // 2) if rope else kr
        for hd in range(MLA_HEADS):
            km_ref[rows, hd * LANES:(hd + 1) * LANES] = (kn[:, hd * LANES:(hd + 1) * LANES] + krr).astype(BF16)
        vm_ref[t_idx] = _dot(cb, wv_ref[...]).T.astype(BF16)

        ucs = _dot(pc[:, _C_U:_C_U + FNET_WIDTH].astype(BF16), wdft_ref[...])
        uc_ref[seq_cols] = ucs[:, :FNET_WIDTH].astype(BF16)
        us_ref[seq_cols] = ucs[:, FNET_WIDTH:].astype(BF16)

        if emit_cache:
            new = (ckvn, kr.T[:MLA_ROPE_DIM, :], gkn.T, gv_t)
            for i, out_ref in enumerate(cache_refs):
                if n_prev:
                    out_ref[r, :n_prev] = prev_refs[i][r]
                out_ref[r, n_prev] = new[i]


def _proj(x, mod, wts, win_parts, layer, consts, *, seq, rope, prev_cache=None, emit_cache=False, tm=512):
    n_tok = x.shape[0]
    n_tiles = n_tok // tm
    n_seq = n_tok // seq
    tiles_per_mod = n_tiles // mod.shape[0]
    multi_seq = tm > seq
    if multi_seq:
        assert seq == PROJ_SUB and not rope
        spt = tm // seq
        seq_major = pl.BlockSpec((seq, spt * FNET_WIDTH), lambda i: (0, i))
        transposed = lambda w: pl.BlockSpec((spt, w, seq), lambda i: (i, 0, 0))
    else:
        assert not emit_cache
        tps = seq // tm
        seq_major = pl.BlockSpec((tm, FNET_WIDTH), lambda i: (i % tps, i // tps))
        transposed = lambda w: pl.BlockSpec((None, w, tm), lambda i: (i // tps, 0, i % tps))
    const = lambda arr: pl.BlockSpec(arr.shape, lambda i: (0,) * arr.ndim)
    tok = lambda w: pl.BlockSpec((tm, w), lambda i: (i, 0))
    in_specs = [tok(D_MODEL), pl.BlockSpec((None, 1, 6 * D_MODEL), lambda i: (i // tiles_per_mod, 0, 0))]
    args = [x, mod]
    for name in _PROJ_WEIGHTS:
        arr = win_parts.get(name)
        in_specs.append(const(arr) if arr is not None else _layer_spec(wts[name], layer))
        args.append(arr if arr is not None else wts[name])
    for name in ("wdft", "mean_q", "mean_k"):
        in_specs.append(const(consts[name]))
        args.append(consts[name])
    if rope:
        in_specs.append(pl.BlockSpec((tm, 6 * LANES), lambda i: (i % tps, 0)))
        args.append(consts["rope"])
    out_specs = [tok(MLA_PAD_WIDTH), tok(MLA_PAD_WIDTH), transposed(MIX_TILE), seq_major, seq_major,
                 tok(GQA_WIDTH), tok(GQA_KV_WIDTH), transposed(GQA_KV_WIDTH)]
    out_shape = [jax.ShapeDtypeStruct((n_tok, MLA_PAD_WIDTH), BF16),
                 jax.ShapeDtypeStruct((n_tok, MLA_PAD_WIDTH), BF16),
                 jax.ShapeDtypeStruct((n_seq, MIX_TILE, seq), BF16),
                 jax.ShapeDtypeStruct((seq, n_seq * FNET_WIDTH), BF16),
                 jax.ShapeDtypeStruct((seq, n_seq * FNET_WIDTH), BF16),
                 jax.ShapeDtypeStruct((n_tok, GQA_WIDTH), BF16),
                 jax.ShapeDtypeStruct((n_tok, GQA_KV_WIDTH), BF16),
                 jax.ShapeDtypeStruct((n_seq, GQA_KV_WIDTH, seq), BF16)]
    n_prev = 0
    if emit_cache:
        n_prev = prev_cache[0].shape[1] if prev_cache is not None else 0
        layers = lambda n, tail: pl.BlockSpec((spt, n) + tail, lambda i: (i, 0, 0, 0))
        tails = ((seq, MLA_KV_RANK), (MLA_ROPE_DIM, seq), (GQA_KV_WIDTH, seq), (GQA_KV_WIDTH, seq))
        if n_prev:
            in_specs += [layers(n_prev, t) for t in tails]
            args += list(prev_cache)
        out_specs += [layers(n_prev + 1, t) for t in tails]
        out_shape += [jax.ShapeDtypeStruct((n_seq, n_prev + 1) + t, F32) for t in tails]
    return pl.pallas_call(
        functools.partial(_proj_kernel, rope=rope, emit_cache=emit_cache, n_prev=n_prev, multi_seq=multi_seq),
        grid=(n_tiles,),
        in_specs=in_specs,
        out_specs=out_specs,
        out_shape=out_shape,
        compiler_params=_params(1, "proj"),
        name="proj_rope" if rope else "proj_ctx",
    )(*args)


def _ctx_kernel(ckv_ref, kr_ref, gk_ref, gv_ref, wk_ref, wv_ref, place_ref, win_ref,
                km_ref, vm_ref, gko_ref, gvo_ref, wa_ref, wc_ref, wkr_ref):
    wt = win_ref[...]
    wa_ref[...] = wt[:_IN_A, :].T.astype(BF16)
    wc_ref[...] = wt[_IN_C0:, :].T.astype(BF16)
    kr_rows = jnp.concatenate([wt[_IN_A:_IN_C0, :], jnp.zeros((LANES - MLA_ROPE_DIM, wt.shape[1]), F32)], axis=0)
    wkr_ref[...] = kr_rows.T.astype(BF16)

    cb = ckv_ref[...].astype(BF16)
    kn = _dot(cb, wk_ref[...])
    kr = _dot(kr_ref[...].astype(BF16), place_ref[...])
    for hd in range(MLA_HEADS):
        km_ref[:, hd * LANES:(hd + 1) * LANES] = (kn[:, hd * LANES:(hd + 1) * LANES] + kr).astype(BF16)
    vm_ref[...] = _dot(cb, wv_ref[...]).T.astype(BF16)
    gko_ref[...] = gk_ref[...].astype(BF16)
    gvo_ref[...] = gv_ref[...].T.astype(BF16)


def _ctx_prep(layer, cache_ckv, cache_krope, cache_gk, cache_gv, w_in_t, wts, consts):
    nb, _, past, _ = cache_ckv.shape
    rows = D_MODEL // nb
    cache = lambda w: pl.BlockSpec((None, None, past, w), lambda b: (b, layer, 0, 0))
    out = lambda w: pl.BlockSpec((None, past, w), lambda b: (b, 0, 0))
    out_t = lambda w: pl.BlockSpec((None, w, past), lambda b: (b, 0, 0))
    part = lambda w: pl.BlockSpec((rows, w), lambda b: (b, 0))
    part_widths = (_IN_A, _IN_C, LANES)
    return pl.pallas_call(
        _ctx_kernel,
        grid=(nb,),
        in_specs=[cache(MLA_KV_RANK), cache(MLA_ROPE_DIM), cache(GQA_KV_WIDTH), cache(GQA_KV_WIDTH),
                  _layer_spec(wts["wk"], layer), _layer_spec(wts["wv"], layer),
                  pl.BlockSpec((MLA_ROPE_DIM, LANES), lambda b: (0, 0)),
                  pl.BlockSpec((None, w_in_t.shape[1], rows), lambda b: (layer, 0, b))],
        out_specs=[out(MLA_PAD_WIDTH), out_t(MIX_TILE), out(GQA_KV_WIDTH), out_t(GQA_KV_WIDTH)]
                  + [part(w) for w in part_widths],
        out_shape=[jax.ShapeDtypeStruct((nb, past, MLA_PAD_WIDTH), BF16),
                   jax.ShapeDtypeStruct((nb, MIX_TILE, past), BF16),
                   jax.ShapeDtypeStruct((nb, past, GQA_KV_WIDTH), BF16),
                   jax.ShapeDtypeStruct((nb, GQA_KV_WIDTH, past), BF16)]
                  + [jax.ShapeDtypeStruct((D_MODEL, w), BF16) for w in part_widths],
        compiler_params=_params(1, "ctx_prep"),
        name="ctx_prep",
    )(cache_ckv, cache_krope, cache_gk.reshape(nb, DEPTH, past, GQA_KV_WIDTH),
      cache_gv.reshape(nb, DEPTH, past, GQA_KV_WIDTH), wts["wk"], wts["wv"], consts["place"], w_in_t)


def _attn_kernel(*refs, n_seg, ahead, n_cast):
    n_in = 1 + 2 * n_seg
    outs_at = 2 * n_in + n_cast
    families = []
    for f, mla in enumerate((True, False)):
        ins = refs[f * n_in:(f + 1) * n_in]
        families.append((mla, ins[0], ins[1:1 + n_seg], ins[1 + n_seg:], refs[outs_at + f]))
    for c in range(n_cast):
        refs[outs_at + 2 + c][...] = refs[2 * n_in + c][...].astype(BF16)
    n_seqs, tq = refs[0].shape[:2]
    lower = _lower_half(tq)
    top = lax.broadcasted_iota(jnp.int32, (LANES, tq), 0) < HALF

    def scores_t(f, b, p, hd):
        mla, q_ref, k_refs, _, _ = families[f]
        if mla:
            cols = slice((2 * p + hd) * LANES, (2 * p + hd + 1) * LANES)
            qh = q_ref[b, :, cols]
            keys = [k_ref[b, :, cols] for k_ref in k_refs]
        else:
            q = q_ref[b, :, p * LANES:(p + 1) * LANES]
            qh = jnp.where(lower if hd == 0 else ~lower, q, jnp.zeros_like(q))
            keys = [k_ref[b] for k_ref in k_refs]
        return [lax.dot_general(k, qh, (((1,), (1,)), ((), ())), preferred_element_type=F32) for k in keys]

    def softmax_t(ss):
        m = functools.reduce(jnp.maximum, [jnp.max(s, axis=0, keepdims=True) for s in ss])
        es = [jnp.exp2(s - m) for s in ss]
        denom = functools.reduce(jnp.add, [jnp.sum(e, axis=0, keepdims=True) for e in es])
        return [e.astype(BF16) for e in es], denom

    def weighted_values_t(f, b, p, es, denom):
        mla, _, _, vt_refs, _ = families[f]
        rows = slice(p * LANES, (p + 1) * LANES) if mla else slice(None)
        acc = functools.reduce(jnp.add, [_dot(vt_ref[b, rows, :], e) for e, vt_ref in zip(es, vt_refs)])
        return acc * (1.0 / denom)

    work = [(f, b, p, hd) for f in range(2) for b in range(n_seqs) for p in range(HEAD_PAIRS) for hd in range(2)]
    pending = [scores_t(*w) for w in work[:ahead]]
    outs = {}
    for i, (f, b, p, hd) in enumerate(work):
        es, denom = softmax_t(pending.pop(0))
        if i + ahead < len(work):
            pending.append(scores_t(*work[i + ahead]))
        outs[f, b, p, hd] = weighted_values_t(f, b, p, es, denom)
        if (p, hd) != (HEAD_PAIRS - 1, 1):
            continue
        mla, o_ref = families[f][0], families[f][4]
        tiles = [jnp.where(top, outs[f, b, t, 0], outs[f, b, t, 1]).T for t in range(HEAD_PAIRS)]
        if not mla:
            t1 = pltpu.roll(tiles[1], HALF, 1)
            tiles = [jnp.where(lower, tiles[0], t1), jnp.where(lower, tiles[2], tiles[0]),
                     jnp.where(lower, t1, tiles[2])]
        for t in range(HEAD_PAIRS):
            o_ref[b, :, t * LANES:(t + 1) * LANES] = tiles[t].astype(o_ref.dtype)


def _attention(mla_q, mla_kv, gqa_q, gqa_kv, *, tq, bb=1, ahead=2, cast=(), layer=0):
    n_batch, sq = mla_q.shape[:2]
    n_seg = len(mla_kv)
    n_q = sq // tq
    steps = (n_batch // bb) * n_q
    whole = lambda a: pl.BlockSpec((bb,) + a.shape[1:], lambda b, i: (b, 0, 0))
    in_specs, args = [], []
    for q, kv in ((mla_q, mla_kv), (gqa_q, gqa_kv)):
        in_specs += [pl.BlockSpec((bb, tq, q.shape[2]), lambda b, i: (b, i, 0))]
        in_specs += [whole(k) for k, _ in kv] + [whole(v) for _, v in kv]
        args += [q] + [k for k, _ in kv] + [v for _, v in kv]
    out_spec = pl.BlockSpec((bb, tq, MIX_TILE), lambda b, i: (b, i, 0))
    out_specs = [out_spec, out_spec]
    out_shape = [jax.ShapeDtypeStruct((n_batch, sq, MIX_TILE), BF16)] * 2
    for w in cast:
        rows, cols = w.shape[1] // steps, w.shape[2]
        in_specs.append(pl.BlockSpec((None, rows, cols), lambda b, i: (layer, b * n_q + i, 0)))
        out_specs.append(pl.BlockSpec((rows, cols), lambda b, i: (b * n_q + i, 0)))
        out_shape.append(jax.ShapeDtypeStruct(w.shape[1:], BF16))
        args.append(w)
    return pl.pallas_call(
        functools.partial(_attn_kernel, n_seg=n_seg, ahead=ahead, n_cast=len(cast)),
        grid=(n_batch // bb, n_q),
        in_specs=in_specs,
        out_specs=out_specs,
        out_shape=out_shape,
        compiler_params=_params(2, "attn_latent" if n_seg > 1 else "attn_prompt"),
        name=f"attn_s{n_seg}",
    )(*args)


def _fnet_kernel(ca_ref, sa_ref, cb_ref, sb_ref, uc_ref, us_ref, o_ref, tc_ref, ts_ref, *, scale):
    @pl.when(pl.program_id(1) == 0)
    def _build_twiddles():
        cb, sb = cb_ref[...], sb_ref[...]
        for a in range(ca_ref.shape[0]):
            ca, sa = ca_ref[a:a + 1, :], sa_ref[a:a + 1, :]
            rows = slice(a * DFT_SPLIT, (a + 1) * DFT_SPLIT)
            tc_ref[rows, :] = (ca * cb - sa * sb).astype(BF16)
            ts_ref[rows, :] = (sa * cb + ca * sb).astype(BF16)

    acc = _dot(tc_ref[...], uc_ref[...]) - _dot(ts_ref[...], us_ref[...])
    o_ref[...] = (acc * scale).astype(o_ref.dtype)


def _fnet(tables, uc, us, tm, tn):
    seq, width = uc.shape
    scale = float((seq * FNET_GROUP_DIM) ** -0.5)
    n_a = tm // DFT_SPLIT
    part_a = pl.BlockSpec((n_a, seq), lambda i, j: (i, 0))
    part_b = pl.BlockSpec((DFT_SPLIT, seq), lambda i, j: (0, 0))
    data = pl.BlockSpec((seq, tn), lambda i, j: (0, j))
    return pl.pallas_call(
        functools.partial(_fnet_kernel, scale=scale),
        grid=(seq // tm, width // tn),
        in_specs=[part_a, part_a, part_b, part_b, data, data],
        out_specs=pl.BlockSpec((tm, tn), lambda i, j: (i, j)),
        out_shape=jax.ShapeDtypeStruct((seq, width), BF16),
        scratch_shapes=[pltpu.VMEM((tm, seq), BF16), pltpu.VMEM((tm, seq), BF16)],
        compiler_params=_params(2, "fnet"),
        name="fnet",
    )(*tables, uc, us)


def _post_body(x_ref, mla_ref, fn_ref, gqa_ref, o_ref, mod_ref, g2_ref, wo_ref, wg_ref, wu_ref, wd_ref, gf_ref, final):
    fn = fn_ref[...]
    if fn.shape[1] > FNET_WIDTH:
        fn = jnp.concatenate([fn[:, j:j + FNET_WIDTH] for j in range(0, fn.shape[1], FNET_WIDTH)], axis=0)
    mix = jnp.concatenate([mla_ref[...], fn, gqa_ref[...]], axis=-1)
    gate1 = mod_ref[:, 2 * D_MODEL:3 * D_MODEL]
    shift2 = mod_ref[:, 3 * D_MODEL:4 * D_MODEL]
    scale2 = mod_ref[:, 4 * D_MODEL:5 * D_MODEL]
    gate2 = mod_ref[:, 5 * D_MODEL:6 * D_MODEL]
    x = x_ref[...] + gate1 * _dot(mix, wo_ref[...])
    h = (_rms(x) * g2_ref[...] * (1.0 + scale2) + shift2).astype(BF16)
    g = _dot(h, wg_ref[...])
    u = _dot(h, wu_ref[...])
    a = (g * jax.nn.sigmoid(g) * u).astype(BF16)
    x = x + gate2 * _dot(a, wd_ref[...])
    if final:
        x = _rms(x) * gf_ref[...]
    o_ref[...] = x


def _post_kernel(*refs, final, n_first, with_next_mod):
    first, second, shared = refs[0:4], refs[4:8], refs[8:15]
    n_in = 18 if with_next_mod else 15
    o_first, o_second = refs[n_in:n_in + 2]
    step = pl.program_id(0)

    def run(group, o_ref):
        if with_next_mod:
            _ada_kernel(*refs[15:18], refs[n_in + 2])
        _post_body(*group, o_ref, *shared, final)

    @pl.when(step < n_first)
    def _first_group():
        run(first, o_first)

    @pl.when(step >= n_first)
    def _second_group():
        run(second, o_second)


def _post(groups, mod, g2, ffn, layer, g_final, *, final, tm, next_mod=None):
    (x1, _, _, _, seq1), (x2, _, _, _, seq2) = groups
    n1, n2 = x1.shape[0] // tm, x2.shape[0] // tm
    tiles_per_mod = n2 // (mod.shape[0] - 1)
    tile1 = lambda i: jnp.minimum(i, n1 - 1)
    tile2 = lambda i: jnp.maximum(i - n1, 0)
    resident = lambda arr: pl.BlockSpec(arr.shape, lambda i: (0, 0), pipeline_mode=pl.Buffered(1))

    def group_specs(seq, tile):
        tok = lambda w: pl.BlockSpec((tm, w), lambda i: (tile(i), 0))
        if tm <= seq:
            tps = seq // tm
            fnet_spec = pl.BlockSpec((tm, FNET_WIDTH), lambda i: (tile(i) % tps, tile(i) // tps))
        else:
            fnet_spec = pl.BlockSpec((seq, (tm // seq) * FNET_WIDTH), lambda i: (0, tile(i)))
        return [tok(D_MODEL), tok(MIX_TILE), fnet_spec, tok(MIX_TILE)], tok(D_MODEL)

    in1, out1 = group_specs(seq1, tile1)
    in2, out2 = group_specs(seq2, tile2)
    mod_spec = pl.BlockSpec((None, 1, 6 * D_MODEL),
                            lambda i: (jnp.where(i < n1, 0, 1 + tile2(i) // tiles_per_mod), 0, 0))
    in_specs = (in1 + in2 + [mod_spec, _layer_spec(g2, layer)] + [resident(w) for w in ffn]
                + [pl.BlockSpec((1, D_MODEL), lambda i: (0, 0))])
    args = [*groups[0][:4], *groups[1][:4], mod, g2, *ffn, g_final]
    out_specs = [out1, out2]
    out_shape = [jax.ShapeDtypeStruct(x1.shape, F32), jax.ShapeDtypeStruct(x2.shape, F32)]
    if next_mod is not None:
        cond_t, w_ada, b_ada = next_mod
        ada_in, ada_out, ada_shape = _ada_specs(cond_t, w_ada, layer + 1, w_ada.shape[2] // (n1 + n2))
        in_specs += ada_in
        args += [cond_t, w_ada, b_ada]
        out_specs.append(ada_out)
        out_shape.append(ada_shape)
    return pl.pallas_call(
        functools.partial(_post_kernel, final=final, n_first=n1, with_next_mod=next_mod is not None),
        grid=(n1 + n2,),
        in_specs=in_specs,
        out_specs=out_specs,
        out_shape=out_shape,
        compiler_params=_params(1, "post"),
        name="post_final" if final else "post",
    )(*args)


def _prep_weights(g_norm1, g_norm2, g_q_a, w_q_up, g_kv_a, w_kv_up, g_q_head, g_k_head):
    row = lambda g: g.reshape(DEPTH, 1, -1)
    wq = w_q_up.reshape(DEPTH, MLA_Q_RANK, MLA_HEADS, MLA_QK_DIM)
    wq = jnp.concatenate([wq[..., MLA_NOPE_DIM:], wq[..., :MLA_NOPE_DIM]], axis=-1)
    wq = jnp.pad(wq, ((0, 0), (0, 0), (0, 0), (0, LANES - MLA_QK_DIM)))
    wkv = w_kv_up.reshape(DEPTH, MLA_KV_RANK, MLA_HEADS, MLA_NOPE_DIM + MLA_V_DIM)
    wk = jnp.pad(wkv[..., :MLA_NOPE_DIM], ((0, 0), (0, 0), (0, 0), (MLA_ROPE_DIM, LANES - MLA_QK_DIM)))
    wv = wkv[..., MLA_NOPE_DIM:]
    return {
        "g1": row(g_norm1), "g2": row(g_norm2), "g_q_a": row(g_q_a), "g_kv_a": row(g_kv_a),
        "g_q_head": row(jnp.tile(g_q_head, (1, GQA_HEADS))), "g_k_head": row(jnp.tile(g_k_head, (1, GQA_KV_HEADS))),
        "wq": wq.reshape(DEPTH, MLA_Q_RANK, MLA_PAD_WIDTH).astype(BF16),
        "wk": wk.reshape(DEPTH, MLA_KV_RANK, MLA_PAD_WIDTH).astype(BF16),
        "wv": wv.reshape(DEPTH, MLA_KV_RANK, MIX_TILE).astype(BF16),
    }


def kernel(x_prompt, x_sample, cache_mla_ckv, cache_mla_krope, cache_gqa_k, cache_gqa_v, c, c_ctx, w_ada, b_ada,
           g_norm1, g_norm2, w_in, g_q_a, w_q_up, g_kv_a, w_kv_up, g_q_head, g_k_head, w_out,
           w_ffn_gate, w_ffn_up, w_ffn_down, g_final):
    n_pb, p_seq, _ = x_prompt.shape
    n_sb, s_seq, _ = x_sample.shape
    consts = {"wdft": _channel_dft(), "mean_q": _group_mean_matrix(GQA_HEADS),
              "mean_k": _group_mean_matrix(GQA_KV_HEADS), "place": _rope_place_matrix(),
              "rope": _rope_tables(s_seq)}
    dft_p = _seq_dft_tables(p_seq)
    dft_s = _seq_dft_tables(s_seq)
    gf = g_final.reshape(1, D_MODEL)
    wts = _prep_weights(g_norm1, g_norm2, g_q_a, w_q_up, g_kv_a, w_kv_up, g_q_head, g_k_head)
    w_in_t = jnp.swapaxes(w_in, 1, 2)

    cond_t = jnp.concatenate([c_ctx[None, :], c], axis=0).T
    b_ada3 = b_ada.reshape(DEPTH, 1, 6 * D_MODEL)
    mod = _ada(cond_t, w_ada, b_ada3, 0)

    xp = x_prompt.reshape(n_pb * p_seq, D_MODEL)
    xs = x_sample.reshape(n_sb * s_seq, D_MODEL)
    b3 = lambda a, nb: a.reshape(nb, a.shape[0] // nb, a.shape[1])
    caches = None
    for l in range(DEPTH):
        final = l == DEPTH - 1
        mod_p = mod[0:1].reshape(1, 1, 6 * D_MODEL)
        mod_s = mod[1:].reshape(n_sb, 1, 6 * D_MODEL)

        km_c, vm_c, gk_c, gv_c, w_a, w_c, w_kr = _ctx_prep(
            l, cache_mla_ckv, cache_mla_krope, cache_gqa_k, cache_gqa_v, w_in_t, wts, consts)
        win_parts = {"w_a": w_a, "w_c": w_c, "w_kr": w_kr}
        qm, km, vm, uc_s, us_s, gq, gk, gv = _proj(xs, mod_s, wts, win_parts, l, consts, seq=s_seq, rope=True)
        mla_s, gqa_s, *ffn = _attention(b3(qm, n_sb), [(km_c, vm_c), (b3(km, n_sb), vm)],
                                        b3(gq, n_sb), [(gk_c, gv_c), (b3(gk, n_sb), gv)], tq=512,
                                        cast=(w_out, w_ffn_gate, w_ffn_up, w_ffn_down), layer=l)

        qm, km, vm, uc, us, gq, gk, gv, *caches = _proj(
            xp, mod_p, wts, win_parts, l, consts, seq=p_seq, rope=False, emit_cache=True, prev_cache=caches)
        mla_o, gqa_o = _attention(b3(qm, n_pb), [(b3(km, n_pb), vm)], b3(gq, n_pb), [(b3(gk, n_pb), gv)],
                                  tq=p_seq, bb=4, ahead=8)
        fn_p = _fnet(dft_p, uc, us, tm=p_seq, tn=1024)
        fn_s = _fnet(dft_s, uc_s, us_s, tm=512, tn=n_sb * FNET_WIDTH)
        flat = lambda a: a.reshape(-1, MIX_TILE)
        xp, xs, *mod_next = _post(
            [(xp, flat(mla_o), fn_p, flat(gqa_o), p_seq), (xs, flat(mla_s), fn_s, flat(gqa_s), s_seq)],
            mod.reshape(1 + n_sb, 1, 6 * D_MODEL), wts["g2"], ffn, l, gf, final=final, tm=POST_TILE,
            next_mod=None if final else (cond_t, w_ada, b_ada3))
        if mod_next:
            mod = mod_next[0]

    ckv_new, krope_t, gk_t, gv_t = caches
    heads = lambda a: jnp.swapaxes(a, 2, 3).reshape(n_pb, DEPTH, p_seq, GQA_KV_HEADS, GQA_HEAD_DIM)
    return (xp.reshape(n_pb, p_seq, D_MODEL), xs.reshape(n_sb, s_seq, D_MODEL),
            ckv_new, jnp.swapaxes(krope_t, 2, 3), heads(gk_t), heads(gv_t))
```

```python
import functools

import numpy as np
import jax
import jax.numpy as jnp
from jax import lax
from jax.experimental import pallas as pl
from jax.experimental.pallas import tpu as pltpu

D_MODEL = 1024
DEPTH = 2
GRID_W = 64
MLA_HEADS = 6
MLA_Q_RANK = 384
MLA_KV_RANK = 256
MLA_NOPE_DIM = 64
MLA_ROPE_DIM = 32
MLA_V_DIM = 64
MLA_QK_DIM = MLA_NOPE_DIM + MLA_ROPE_DIM
FNET_GROUPS = 4
FNET_GROUP_DIM = 64
FNET_WIDTH = FNET_GROUPS * FNET_GROUP_DIM
GQA_HEADS = 6
GQA_KV_HEADS = 2
GQA_HEAD_DIM = 64
GQA_GROUP = GQA_HEADS // GQA_KV_HEADS
GQA_WIDTH = GQA_HEADS * GQA_HEAD_DIM
GQA_KV_WIDTH = GQA_KV_HEADS * GQA_HEAD_DIM
D_FF = 2816
ROPE_THETA = 10000.0
EPS = 1e-6
LOG2_E = 1.4426950408889634

LANES = 128
HALF = LANES // 2
HEAD_PAIRS = MLA_HEADS // 2
MLA_PAD_WIDTH = MLA_HEADS * LANES
MIX_TILE = HEAD_PAIRS * LANES
MIB = 1024 * 1024
VMEM_MIB = {"ada": 46, "ctx_prep": 46, "proj": 46, "attn_prompt": 46, "attn_latent": 46, "fnet": 46, "post": 52}


def _params(n_grid_dims, kernel_name):
    return pltpu.CompilerParams(dimension_semantics=("arbitrary",) * n_grid_dims,
                                vmem_limit_bytes=VMEM_MIB[kernel_name] * MIB)
POST_TILE = 512
PROJ_SUB = 256

_IN_A = MLA_Q_RANK + MLA_KV_RANK
_IN_C0 = _IN_A + MLA_ROPE_DIM
_IN_C = FNET_WIDTH + GQA_WIDTH + 2 * GQA_KV_WIDTH
_C_U, _C_GQ, _C_GK, _C_GV = 0, FNET_WIDTH, FNET_WIDTH + GQA_WIDTH, FNET_WIDTH + GQA_WIDTH + GQA_KV_WIDTH

BF16 = jnp.bfloat16
F32 = jnp.float32


def _rope_tables(n_tokens):
    t = np.arange(n_tokens)
    row = (t // GRID_W).astype(np.float64)
    col = (t % GRID_W).astype(np.float64)

    def angles(rot_dim):
        n_axis = rot_dim // 4
        inv = ROPE_THETA ** (-np.arange(n_axis, dtype=np.float64) / n_axis)
        return np.concatenate([row[:, None] * inv, col[:, None] * inv], axis=-1)

    def tables(ang, lane_to_pair, is_first, is_second):
        cos = np.where((is_first | is_second)[None, :], np.cos(ang)[:, lane_to_pair], 1.0)
        sin = np.sin(ang)[:, lane_to_pair]
        sin_a = np.where(is_second[None, :], sin, 0.0)
        sin_b = np.where(is_first[None, :], -sin, 0.0)
        return [cos, sin_a, sin_b]

    lane = np.arange(LANES)
    half_m = MLA_ROPE_DIM // 2
    in_rope = lane < MLA_ROPE_DIM
    first_m = lane < half_m
    second_m = in_rope & ~first_m
    pair_m = lane % half_m
    half_g = GQA_HEAD_DIM // 2
    first_g = (lane % GQA_HEAD_DIM) < half_g
    pair_g = lane % half_g
    tabs = (tables(angles(MLA_ROPE_DIM), pair_m, first_m, second_m)
            + tables(angles(GQA_HEAD_DIM), pair_g, first_g, ~first_g))
    return jnp.asarray(np.concatenate(tabs, axis=-1), dtype=F32)


def _channel_dft():
    c = np.arange(FNET_GROUP_DIM)
    ang = 2.0 * np.pi * np.outer(c, c) / FNET_GROUP_DIM
    eye = np.eye(FNET_GROUPS)
    table = np.concatenate([np.kron(eye, np.cos(ang)), np.kron(eye, np.sin(ang))], axis=1)
    return jnp.asarray(table, dtype=F32).astype(BF16)


def _group_mean_matrix(n_heads):
    return jnp.asarray(np.kron(np.eye(n_heads), np.full((GQA_HEAD_DIM, GQA_HEAD_DIM), 1.0 / GQA_HEAD_DIM)), dtype=BF16)


def _rope_place_matrix():
    return jnp.asarray(np.eye(MLA_ROPE_DIM, LANES), dtype=BF16)


DFT_SPLIT = 32


def _seq_dft_tables(seq):
    s = np.arange(seq)
    n_a = seq // DFT_SPLIT
    ang_a = 2.0 * np.pi * ((np.arange(n_a)[:, None] * s[None, :]) % n_a) / n_a
    ang_b = 2.0 * np.pi * ((np.arange(DFT_SPLIT)[:, None] * s[None, :]) % seq) / seq
    return tuple(jnp.asarray(t, dtype=F32) for t in (np.cos(ang_a), np.sin(ang_a), np.cos(ang_b), np.sin(ang_b)))


def _rms(x):
    return x * lax.rsqrt(jnp.mean(x * x, axis=-1, keepdims=True) + EPS)


def _dot(a, b):
    return jnp.dot(a, b, preferred_element_type=F32)


def _group_rms(x, mean_mat):
    ms = _dot((x * x).astype(BF16), mean_mat)
    return x * lax.rsqrt(ms + EPS)


def _rotate(x, cos, sin_a, sin_b, half):
    return x * cos + pltpu.roll(x, half, 1) * sin_a + pltpu.roll(x, LANES - half, 1) * sin_b


def _lower_half(rows):
    return lax.broadcasted_iota(jnp.int32, (rows, LANES), 1) < HALF


def _layer_spec(arr, layer):
    return pl.BlockSpec((None,) + arr.shape[1:], lambda i: (layer, 0, 0))


def _ada_kernel(ct_ref, w_ref, b_ref, o_ref):
    s = ct_ref[...]
    s = s * jax.nn.sigmoid(s)
    w = w_ref[...]
    for m in range(o_ref.shape[0]):
        o_ref[m:m + 1, :] = jnp.sum(w * s[:, m:m + 1], axis=0, keepdims=True) + b_ref[...]


def _ada_specs(cond_t, w_ada, layer, tn):
    n_cond = cond_t.shape[1]
    in_specs = [pl.BlockSpec((D_MODEL, n_cond), lambda j: (0, 0)),
                pl.BlockSpec((None, D_MODEL, tn), lambda j: (layer, 0, j)),
                pl.BlockSpec((None, 1, tn), lambda j: (layer, 0, j))]
    out_shape = jax.ShapeDtypeStruct((n_cond, w_ada.shape[2]), F32)
    return in_specs, pl.BlockSpec((n_cond, tn), lambda j: (0, j)), out_shape


def _ada(cond_t, w_ada, b_ada, layer, tn=1024):
    in_specs, out_spec, out_shape = _ada_specs(cond_t, w_ada, layer, tn)
    return pl.pallas_call(
        _ada_kernel,
        grid=(w_ada.shape[2] // tn,),
        in_specs=in_specs,
        out_specs=out_spec,
        out_shape=out_shape,
        compiler_params=_params(1, "ada"),
        name="ada",
    )(cond_t, w_ada, b_ada)


_PROJ_WEIGHTS = ("g1", "w_a", "w_c", "w_kr", "g_q_a", "wq", "g_kv_a", "wk", "wv", "g_q_head", "g_k_head")


def _proj_kernel(*refs, rope, emit_cache, n_prev, multi_seq):
    (x_ref, mod_ref, g1_ref, wa_ref, wc_ref, wkr_ref, gqa_ref, wq_ref, gkva_ref, wk_ref, wv_ref,
     gqh_ref, gkh_ref, wdft_ref, mq_ref, mk_ref) = refs[:16]
    refs = refs[16:]
    if rope:
        tab_ref, refs = refs[0], refs[1:]
    if n_prev:
        prev_refs, refs = refs[:4], refs[4:]
    qm_ref, km_ref, vm_ref, uc_ref, us_ref, gq_ref, gk_ref, gv_ref = refs[:8]
    cache_refs = refs[8:]

    shift = mod_ref[:, 0:D_MODEL]
    scale = mod_ref[:, D_MODEL:2 * D_MODEL]
    lower = _lower_half(PROJ_SUB)

    for r in range(x_ref.shape[0] // PROJ_SUB):
        rows = slice(r * PROJ_SUB, (r + 1) * PROJ_SUB)
        seq_cols = (slice(None), slice(r * FNET_WIDTH, (r + 1) * FNET_WIDTH)) if multi_seq else (rows, slice(None))
        t_idx = (r,) if multi_seq else (slice(None), rows)

        hb = (_rms(x_ref[rows, :]) * g1_ref[...] * (1.0 + scale) + shift).astype(BF16)
        pa = _dot(hb, wa_ref[...])
        pc = _dot(hb, wc_ref[...])
        kr = _dot(hb, wkr_ref[...])

        if rope:
            cos_m, sa_m, sb_m = (tab_ref[rows, i * LANES:(i + 1) * LANES] for i in range(3))
            cos_g, sa_g, sb_g = (tab_ref[rows, i * LANES:(i + 1) * LANES] for i in range(3, 6))

        gqn = _group_rms(pc[:, _C_GQ:_C_GQ + GQA_WIDTH], mq_ref[...]) * gqh_ref[...]
        tiles = []
        for p in range(HEAD_PAIRS):
            gp = gqn[:, p * LANES:(p + 1) * LANES]
            if rope:
                gp = _rotate(gp, cos_g, sa_g, sb_g, GQA_HEAD_DIM // 2)
            tiles.append(gp * (LOG2_E * GQA_HEAD_DIM ** -0.5))
        gq_ref[rows, 0:LANES] = jnp.where(lower, tiles[0], tiles[1]).astype(BF16)
        gq_ref[rows, LANES:2 * LANES] = jnp.where(
            lower, pltpu.roll(tiles[0], HALF, 1), pltpu.roll(tiles[2], HALF, 1)).astype(BF16)
        gq_ref[rows, 2 * LANES:] = jnp.where(lower, tiles[1], tiles[2]).astype(BF16)
        gkn = _group_rms(pc[:, _C_GK:_C_GK + GQA_KV_WIDTH], mk_ref[...]) * gkh_ref[...]
        gkr = _rotate(gkn, cos_g, sa_g, sb_g, GQA_HEAD_DIM // 2) if rope else gkn
        gv_t = pc[:, _C_GV:_C_GV + GQA_KV_WIDTH].T
        gk_ref[rows, :] = gkr.astype(BF16)
        gv_ref[t_idx] = gv_t.astype(BF16)

        cqn = _rms(pa[:, :MLA_Q_RANK]) * gqa_ref[...]
        q = _dot(cqn.astype(BF16), wq_ref[...])
        for hd in range(MLA_HEADS):
            qh = q[:, hd * LANES:(hd + 1) * LANES]
            if rope:
                qh = _rotate(qh, cos_m, sa_m, sb_m, MLA_ROPE_DIM // 2)
            qm_ref[rows, hd * LANES:(hd + 1) * LANES] = (qh * (LOG2_E * MLA_QK_DIM ** -0.5)).astype(BF16)

        ckvn = _rms(pa[:, MLA_Q_RANK:]) * gkva_ref[...]
        cb = ckvn.astype(BF16)
        kn = _dot(cb, wk_ref[...])
        krr = _rotate(kr, cos_m, sa_m, sb_m, MLA_ROPE_DIM // 2) if rope else kr
        for hd in range(MLA_HEADS):
            km_ref[rows, hd * LANES:(hd + 1) * LANES] = (kn[:, hd * LANES:(hd + 1) * LANES] + krr).astype(BF16)
        vm_ref[t_idx] = _dot(cb, wv_ref[...]).T.astype(BF16)

        ucs = _dot(pc[:, _C_U:_C_U + FNET_WIDTH].astype(BF16), wdft_ref[...])
        uc_ref[seq_cols] = ucs[:, :FNET_WIDTH].astype(BF16)
        us_ref[seq_cols] = ucs[:, FNET_WIDTH:].astype(BF16)

        if emit_cache:
            new = (ckvn, kr.T[:MLA_ROPE_DIM, :], gkn.T, gv_t)
            for i, out_ref in enumerate(cache_refs):
                if n_prev:
                    out_ref[r, :n_prev] = prev_refs[i][r]
                out_ref[r, n_prev] = new[i]


def _proj(x, mod, wts, win_parts, layer, consts, *, seq, rope, prev_cache=None, emit_cache=False, tm=512):
    n_tok = x.shape[0]
    n_tiles = n_tok // tm
    n_seq = n_tok // seq
    tiles_per_mod = n_tiles // mod.shape[0]
    multi_seq = tm > seq
    if multi_seq:
        assert seq == PROJ_SUB and not rope
        spt = tm // seq
        seq_major = pl.BlockSpec((seq, spt * FNET_WIDTH), lambda i: (0, i))
        transposed = lambda w: pl.BlockSpec((spt, w, seq), lambda i: (i, 0, 0))
    else:
        assert not emit_cache
        tps = seq // tm
        seq_major = pl.BlockSpec((tm, FNET_WIDTH), lambda i: (i % tps, i // tps))
        transposed = lambda w: pl.BlockSpec((None, w, tm), lambda i: (i // tps, 0, i % tps))
    const = lambda arr: pl.BlockSpec(arr.shape, lambda i: (0,) * arr.ndim)
    tok = lambda w: pl.BlockSpec((tm, w), lambda i: (i, 0))
    in_specs = [tok(D_MODEL), pl.BlockSpec((None, 1, 6 * D_MODEL), lambda i: (i // tiles_per_mod, 0, 0))]
    args = [x, mod]
    for name in _PROJ_WEIGHTS:
        arr = win_parts.get(name)
        in_specs.append(const(arr) if arr is not None else _layer_spec(wts[name], layer))
        args.append(arr if arr is not None else wts[name])
    for name in ("wdft", "mean_q", "mean_k"):
        in_specs.append(const(consts[name]))
        args.append(consts[name])
    if rope:
        in_specs.append(pl.BlockSpec((tm, 6 * LANES), lambda i: (i % tps, 0)))
        args.append(consts["rope"])
    out_specs = [tok(MLA_PAD_WIDTH), tok(MLA_PAD_WIDTH), transposed(MIX_TILE), seq_major, seq_major,
                 tok(GQA_WIDTH), tok(GQA_KV_WIDTH), transposed(GQA_KV_WIDTH)]
    out_shape = [jax.ShapeDtypeStruct((n_tok, MLA_PAD_WIDTH), BF16),
                 jax.ShapeDtypeStruct((n_tok, MLA_PAD_WIDTH), BF16),
                 jax.ShapeDtypeStruct((n_seq, MIX_TILE, seq), BF16),
                 jax.ShapeDtypeStruct((seq, n_seq * FNET_WIDTH), BF16),
                 jax.ShapeDtypeStruct((seq, n_seq * FNET_WIDTH), BF16),
                 jax.ShapeDtypeStruct((n_tok, GQA_WIDTH), BF16),
                 jax.ShapeDtypeStruct((n_tok, GQA_KV_WIDTH), BF16),
                 jax.ShapeDtypeStruct((n_seq, GQA_KV_WIDTH, seq), BF16)]
    n_prev = 0
    if emit_cache:
        n_prev = prev_cache[0].shape[1] if prev_cache is not None else 0
        layers = lambda n, tail: pl.BlockSpec((spt, n) + tail, lambda i: (i, 0, 0, 0))
        tails = ((seq, MLA_KV_RANK), (MLA_ROPE_DIM, seq), (GQA_KV_WIDTH, seq), (GQA_KV_WIDTH, seq))
        if n_prev:
            in_specs += [layers(n_prev, t) for t in tails]
            args += list(prev_cache)
        out_specs += [layers(n_prev + 1, t) for t in tails]
        out_shape += [jax.ShapeDtypeStruct((n_seq, n_prev + 1) + t, F32) for t in tails]
    return pl.pallas_call(
        functools.partial(_proj_kernel, rope=rope, emit_cache=emit_cache, n_prev=n_prev, multi_seq=multi_seq),
        grid=(n_tiles,),
        in_specs=in_specs,
        out_specs=out_specs,
        out_shape=out_shape,
        compiler_params=_params(1, "proj"),
        name="proj_rope" if rope else "proj_ctx",
    )(*args)


def _ctx_kernel(ckv_ref, kr_ref, gk_ref, gv_ref, wk_ref, wv_ref, place_ref, win_ref,
                km_ref, vm_ref, gko_ref, gvo_ref, wa_ref, wc_ref, wkr_ref):
    wt = win_ref[...]
    wa_ref[...] = wt[:_IN_A, :].T.astype(BF16)
    wc_ref[...] = wt[_IN_C0:, :].T.astype(BF16)
    kr_rows = jnp.concatenate([wt[_IN_A:_IN_C0, :], jnp.zeros((LANES - MLA_ROPE_DIM, wt.shape[1]), F32)], axis=0)
    wkr_ref[...] = kr_rows.T.astype(BF16)

    cb = ckv_ref[...].astype(BF16)
    kn = _dot(cb, wk_ref[...])
    kr = _dot(kr_ref[...].astype(BF16), place_ref[...])
    for hd in range(MLA_HEADS):
        km_ref[:, hd * LANES:(hd + 1) * LANES] = (kn[:, hd * LANES:(hd + 1) * LANES] + kr).astype(BF16)
    vm_ref[...] = _dot(cb, wv_ref[...]).T.astype(BF16)
    gko_ref[...] = gk_ref[...].astype(BF16)
    gvo_ref[...] = gv_ref[...].T.astype(BF16)


def _ctx_prep(layer, cache_ckv, cache_krope, cache_gk, cache_gv, w_in_t, wts, consts):
    nb, _, past, _ = cache_ckv.shape
    rows = D_MODEL // nb
    cache = lambda w: pl.BlockSpec((None, None, past, w), lambda b: (b, layer, 0, 0))
    out = lambda w: pl.BlockSpec((None, past, w), lambda b: (b, 0, 0))
    out_t = lambda w: pl.BlockSpec((None, w, past), lambda b: (b, 0, 0))
    part = lambda w: pl.BlockSpec((rows, w), lambda b: (b, 0))
    part_widths = (_IN_A, _IN_C, LANES)
    return pl.pallas_call(
        _ctx_kernel,
        grid=(nb,),
        in_specs=[cache(MLA_KV_RANK), cache(MLA_ROPE_DIM), cache(GQA_KV_WIDTH), cache(GQA_KV_WIDTH),
                  _layer_spec(wts["wk"], layer), _layer_spec(wts["wv"], layer),
                  pl.BlockSpec((MLA_ROPE_DIM, LANES), lambda b: (0, 0)),
                  pl.BlockSpec((None, w_in_t.shape[1], rows), lambda b: (layer, 0, b))],
        out_specs=[out(MLA_PAD_WIDTH), out_t(MIX_TILE), out(GQA_KV_WIDTH), out_t(GQA_KV_WIDTH)]
                  + [part(w) for w in part_widths],
        out_shape=[jax.ShapeDtypeStruct((nb, past, MLA_PAD_WIDTH), BF16),
                   jax.ShapeDtypeStruct((nb, MIX_TILE, past), BF16),
                   jax.ShapeDtypeStruct((nb, past, GQA_KV_WIDTH), BF16),
                   jax.ShapeDtypeStruct((nb, GQA_KV_WIDTH, past), BF16)]
                  + [jax.ShapeDtypeStruct((D_MODEL, w), BF16) for w in part_widths],
        compiler_params=_params(1, "ctx_prep"),
        name="ctx_prep",
    )(cache_ckv, cache_krope, cache_gk.reshape(nb, DEPTH, past, GQA_KV_WIDTH),
      cache_gv.reshape(nb, DEPTH, past, GQA_KV_WIDTH), wts["wk"], wts["wv"], consts["place"], w_in_t)


def _attn_kernel(*refs, n_seg, ahead, n_cast):
    n_in = 1 + 2 * n_seg
    outs_at = 2 * n_in + n_cast
    families = []
    for f, mla in enumerate((True, False)):
        ins = refs[f * n_in:(f + 1) * n_in]
        families.append((mla, ins[0], ins[1:1 + n_seg], ins[1 + n_seg:], refs[outs_at + f]))
    for c in range(n_cast):
        refs[outs_at + 2 + c][...] = refs[2 * n_in + c][...].astype(BF16)
    n_seqs, tq = refs[0].shape[:2]
    lower = _lower_half(tq)
    top = lax.broadcasted_iota(jnp.int32, (LANES, tq), 0) < HALF

    def scores_t(f, b, p, hd):
        mla, q_ref, k_refs, _, _ = families[f]
        if mla:
            cols = slice((2 * p + hd) * LANES, (2 * p + hd + 1) * LANES)
            qh = q_ref[b, :, cols]
            keys = [k_ref[b, :, cols] for k_ref in k_refs]
        else:
            q = q_ref[b, :, p * LANES:(p + 1) * LANES]
            qh = jnp.where(lower if hd == 0 else ~lower, q, jnp.zeros_like(q))
            keys = [k_ref[b] for k_ref in k_refs]
        return [lax.dot_general(k, qh, (((1,), (1,)), ((), ())), preferred_element_type=F32) for k in keys]

    def softmax_t(ss):
        m = functools.reduce(jnp.maximum, [jnp.max(s, axis=0, keepdims=True) for s in ss])
        es = [jnp.exp2(s - m) for s in ss]
        denom = functools.reduce(jnp.add, [jnp.sum(e, axis=0, keepdims=True) for e in es])
        return [e.astype(BF16) for e in es], denom

    def weighted_values_t(f, b, p, es, denom):
        mla, _, _, vt_refs, _ = families[f]
        rows = slice(p * LANES, (p + 1) * LANES) if mla else slice(None)
        acc = functools.reduce(jnp.add, [_dot(vt_ref[b, rows, :], e) for e, vt_ref in zip(es, vt_refs)])
        return acc * (1.0 / denom)

    work = [(f, b, p, hd) for f in range(2) for b in range(n_seqs) for p in range(HEAD_PAIRS) for hd in range(2)]
    pending = [scores_t(*w) for w in work[:ahead]]
    outs = {}
    for i, (f, b, p, hd) in enumerate(work):
        es, denom = softmax_t(pending.pop(0))
        if i + ahead < len(work):
            pending.append(scores_t(*work[i + ahead]))
        outs[f, b, p, hd] = weighted_values_t(f, b, p, es, denom)
        if (p, hd) != (HEAD_PAIRS - 1, 1):
            continue
        mla, o_ref = families[f][0], families[f][4]
        tiles = [jnp.where(top, outs[f, b, t, 0], outs[f, b, t, 1]).T for t in range(HEAD_PAIRS)]
        if not mla:
            t1 = pltpu.roll(tiles[1], HALF, 1)
            tiles = [jnp.where(lower, tiles[0], t1), jnp.where(lower, tiles[2], tiles[0]),
                     jnp.where(lower, t1, tiles[2])]
        for t in range(HEAD_PAIRS):
            o_ref[b, :, t * LANES:(t + 1) * LANES] = tiles[t].astype(o_ref.dtype)


def _attention(mla_q, mla_kv, gqa_q, gqa_kv, *, tq, bb=1, ahead=2, cast=(), layer=0):
    n_batch, sq = mla_q.shape[:2]
    n_seg = len(mla_kv)
    n_q = sq // tq
    steps = (n_batch // bb) * n_q
    whole = lambda a: pl.BlockSpec((bb,) + a.shape[1:], lambda b, i: (b, 0, 0))
    in_specs, args = [], []
    for q, kv in ((mla_q, mla_kv), (gqa_q, gqa_kv)):
        in_specs += [pl.BlockSpec((bb, tq, q.shape[2]), lambda b, i: (b, i, 0))]
        in_specs += [whole(k) for k, _ in kv] + [whole(v) for _, v in kv]
        args += [q] + [k for k, _ in kv] + [v for _, v in kv]
    out_spec = pl.BlockSpec((bb, tq, MIX_TILE), lambda b, i: (b, i, 0))
    out_specs = [out_spec, out_spec]
    out_shape = [jax.ShapeDtypeStruct((n_batch, sq, MIX_TILE), BF16)] * 2
    for w in cast:
        rows, cols = w.shape[1] // steps, w.shape[2]
        in_specs.append(pl.BlockSpec((None, rows, cols), lambda b, i: (layer, b * n_q + i, 0)))
        out_specs.append(pl.BlockSpec((rows, cols), lambda b, i: (b * n_q + i, 0)))
        out_shape.append(jax.ShapeDtypeStruct(w.shape[1:], BF16))
        args.append(w)
    return pl.pallas_call(
        functools.partial(_attn_kernel, n_seg=n_seg, ahead=ahead, n_cast=len(cast)),
        grid=(n_batch // bb, n_q),
        in_specs=in_specs,
        out_specs=out_specs,
        out_shape=out_shape,
        compiler_params=_params(2, "attn_latent" if n_seg > 1 else "attn_prompt"),
        name=f"attn_s{n_seg}",
    )(*args)


def _fnet_kernel(ca_ref, sa_ref, cb_ref, sb_ref, uc_ref, us_ref, o_ref, tc_ref, ts_ref, *, scale):
    @pl.when(pl.program_id(1) == 0)
    def _build_twiddles():
        cb, sb = cb_ref[...], sb_ref[...]
        for a in range(ca_ref.shape[0]):
            ca, sa = ca_ref[a:a + 1, :], sa_ref[a:a + 1, :]
            rows = slice(a * DFT_SPLIT, (a + 1) * DFT_SPLIT)
            tc_ref[rows, :] = (ca * cb - sa * sb).astype(BF16)
            ts_ref[rows, :] = (sa * cb + ca * sb).astype(BF16)

    acc = _dot(tc_ref[...], uc_ref[...]) - _dot(ts_ref[...], us_ref[...])
    o_ref[...] = (acc * scale).astype(o_ref.dtype)


def _fnet(tables, uc, us, tm, tn):
    seq, width = uc.shape
    scale = float((seq * FNET_GROUP_DIM) ** -0.5)
    n_a = tm // DFT_SPLIT
    part_a = pl.BlockSpec((n_a, seq), lambda i, j: (i, 0))
    part_b = pl.BlockSpec((DFT_SPLIT, seq), lambda i, j: (0, 0))
    data = pl.BlockSpec((seq, tn), lambda i, j: (0, j))
    return pl.pallas_call(
        functools.partial(_fnet_kernel, scale=scale),
        grid=(seq // tm, width // tn),
        in_specs=[part_a, part_a, part_b, part_b, data, data],
        out_specs=pl.BlockSpec((tm, tn), lambda i, j: (i, j)),
        out_shape=jax.ShapeDtypeStruct((seq, width), BF16),
        scratch_shapes=[pltpu.VMEM((tm, seq), BF16), pltpu.VMEM((tm, seq), BF16)],
        compiler_params=_params(2, "fnet"),
        name="fnet",
    )(*tables, uc, us)


def _post_body(x_ref, mla_ref, fn_ref, gqa_ref, o_ref, mod_ref, g2_ref, wo_ref, wg_ref, wu_ref, wd_ref, gf_ref, final):
    fn = fn_ref[...]
    if fn.shape[1] > FNET_WIDTH:
        fn = jnp.concatenate([fn[:, j:j + FNET_WIDTH] for j in range(0, fn.shape[1], FNET_WIDTH)], axis=0)
    mix = jnp.concatenate([mla_ref[...], fn, gqa_ref[...]], axis=-1)
    gate1 = mod_ref[:, 2 * D_MODEL:3 * D_MODEL]
    shift2 = mod_ref[:, 3 * D_MODEL:4 * D_MODEL]
    scale2 = mod_ref[:, 4 * D_MODEL:5 * D_MODEL]
    gate2 = mod_ref[:, 5 * D_MODEL:6 * D_MODEL]
    x = x_ref[...] + gate1 * _dot(mix, wo_ref[...])
    h = (_rms(x) * g2_ref[...] * (1.0 + scale2) + shift2).astype(BF16)
    g = _dot(h, wg_ref[...])
    u = _dot(h, wu_ref[...])
    a = (g * jax.nn.sigmoid(g) * u).astype(BF16)
    x = x + gate2 * _dot(a, wd_ref[...])
    if final:
        x = _rms(x) * gf_ref[...]
    o_ref[...] = x


def _post_kernel(*refs, final, n_first, with_next_mod):
    first, second, shared = refs[0:4], refs[4:8], refs[8:15]
    n_in = 18 if with_next_mod else 15
    o_first, o_second = refs[n_in:n_in + 2]
    step = pl.program_id(0)

    def run(group, o_ref):
        if with_next_mod:
            _ada_kernel(*refs[15:18], refs[n_in + 2])
        _post_body(*group, o_ref, *shared, final)

    @pl.when(step < n_first)
    def _first_group():
        run(first, o_first)

    @pl.when(step >= n_first)
    def _second_group():
        run(second, o_second)


def _post(groups, mod, g2, ffn, layer, g_final, *, final, tm, next_mod=None):
    (x1, _, _, _, seq1), (x2, _, _, _, seq2) = groups
    n1, n2 = x1.shape[0] // tm, x2.shape[0] // tm
    tiles_per_mod = n2 // (mod.shape[0] - 1)
    tile1 = lambda i: jnp.minimum(i, n1 - 1)
    tile2 = lambda i: jnp.maximum(i - n1, 0)
    resident = lambda arr: pl.BlockSpec(arr.shape, lambda i: (0, 0), pipeline_mode=pl.Buffered(1))

    def group_specs(seq, tile):
        tok = lambda w: pl.BlockSpec((tm, w), lambda i: (tile(i), 0))
        if tm <= seq:
            tps = seq // tm
            fnet_spec = pl.BlockSpec((tm, FNET_WIDTH), lambda i: (tile(i) % tps, tile(i) // tps))
        else:
            fnet_spec = pl.BlockSpec((seq, (tm // seq) * FNET_WIDTH), lambda i: (0, tile(i)))
        return [tok(D_MODEL), tok(MIX_TILE), fnet_spec, tok(MIX_TILE)], tok(D_MODEL)

    in1, out1 = group_specs(seq1, tile1)
    in2, out2 = group_specs(seq2, tile2)
    mod_spec = pl.BlockSpec((None, 1, 6 * D_MODEL),
                            lambda i: (jnp.where(i < n1, 0, 1 + tile2(i) // tiles_per_mod), 0, 0))
    in_specs = (in1 + in2 + [mod_spec, _layer_spec(g2, layer)] + [resident(w) for w in ffn]
                + [pl.BlockSpec((1, D_MODEL), lambda i: (0, 0))])
    args = [*groups[0][:4], *groups[1][:4], mod, g2, *ffn, g_final]
    out_specs = [out1, out2]
    out_shape = [jax.ShapeDtypeStruct(x1.shape, F32), jax.ShapeDtypeStruct(x2.shape, F32)]
    if next_mod is not None:
        cond_t, w_ada, b_ada = next_mod
        ada_in, ada_out, ada_shape = _ada_specs(cond_t, w_ada, layer + 1, w_ada.shape[2] // (n1 + n2))
        in_specs += ada_in
        args += [cond_t, w_ada, b_ada]
        out_specs.append(ada_out)
        out_shape.append(ada_shape)
    return pl.pallas_call(
        functools.partial(_post_kernel, final=final, n_first=n1, with_next_mod=next_mod is not None),
        grid=(n1 + n2,),
        in_specs=in_specs,
        out_specs=out_specs,
        out_shape=out_shape,
        compiler_params=_params(1, "post"),
        name="post_final" if final else "post",
    )(*args)


def _prep_weights(g_norm1, g_norm2, g_q_a, w_q_up, g_kv_a, w_kv_up, g_q_head, g_k_head):
    row = lambda g: g.reshape(DEPTH, 1, -1)
    wq = w_q_up.reshape(DEPTH, MLA_Q_RANK, MLA_HEADS, MLA_QK_DIM)
    wq = jnp.concatenate([wq[..., MLA_NOPE_DIM:], wq[..., :MLA_NOPE_DIM]], axis=-1)
    wq = jnp.pad(wq, ((0, 0), (0, 0), (0, 0), (0, LANES - MLA_QK_DIM)))
    wkv = w_kv_up.reshape(DEPTH, MLA_KV_RANK, MLA_HEADS, MLA_NOPE_DIM + MLA_V_DIM)
    wk = jnp.pad(wkv[..., :MLA_NOPE_DIM], ((0, 0), (0, 0), (0, 0), (MLA_ROPE_DIM, LANES - MLA_QK_DIM)))
    wv = wkv[..., MLA_NOPE_DIM:]
    return {
        "g1": row(g_norm1), "g2": row(g_norm2), "g_q_a": row(g_q_a), "g_kv_a": row(g_kv_a),
        "g_q_head": row(jnp.tile(g_q_head, (1, GQA_HEADS))), "g_k_head": row(jnp.tile(g_k_head, (1, GQA_KV_HEADS))),
        "wq": wq.reshape(DEPTH, MLA_Q_RANK, MLA_PAD_WIDTH).astype(BF16),
        "wk": wk.reshape(DEPTH, MLA_KV_RANK, MLA_PAD_WIDTH).astype(BF16),
        "wv": wv.reshape(DEPTH, MLA_KV_RANK, MIX_TILE).astype(BF16),
    }


def kernel(x_prompt, x_sample, cache_mla_ckv, cache_mla_krope, cache_gqa_k, cache_gqa_v, c, c_ctx, w_ada, b_ada,
           g_norm1, g_norm2, w_in, g_q_a, w_q_up, g_kv_a, w_kv_up, g_q_head, g_k_head, w_out,
           w_ffn_gate, w_ffn_up, w_ffn_down, g_final):
    n_pb, p_seq, _ = x_prompt.shape
    n_sb, s_seq, _ = x_sample.shape
    consts = {"wdft": _channel_dft(), "mean_q": _group_mean_matrix(GQA_HEADS),
              "mean_k": _group_mean_matrix(GQA_KV_HEADS), "place": _rope_place_matrix(),
              "rope": _rope_tables(s_seq)}
    dft_p = _seq_dft_tables(p_seq)
    dft_s = _seq_dft_tables(s_seq)
    gf = g_final.reshape(1, D_MODEL)
    wts = _prep_weights(g_norm1, g_norm2, g_q_a, w_q_up, g_kv_a, w_kv_up, g_q_head, g_k_head)
    w_in_t = jnp.swapaxes(w_in, 1, 2)

    cond_t = jnp.concatenate([c_ctx[None, :], c], axis=0).T
    b_ada3 = b_ada.reshape(DEPTH, 1, 6 * D_MODEL)
    mod = _ada(cond_t, w_ada, b_ada3, 0)

    xp = x_prompt.reshape(n_pb * p_seq, D_MODEL)
    xs = x_sample.reshape(n_sb * s_seq, D_MODEL)
    b3 = lambda a, nb: a.reshape(nb, a.shape[0] // nb, a.shape[1])
    caches = None
    for l in range(DEPTH):
        final = l == DEPTH - 1
        mod_p = mod[0:1].reshape(1, 1, 6 * D_MODEL)
        mod_s = mod[1:].reshape(n_sb, 1, 6 * D_MODEL)

        km_c, vm_c, gk_c, gv_c, w_a, w_c, w_kr = _ctx_prep(
            l, cache_mla_ckv, cache_mla_krope, cache_gqa_k, cache_gqa_v, w_in_t, wts, consts)
        win_parts = {"w_a": w_a, "w_c": w_c, "w_kr": w_kr}
        qm, km, vm, uc_s, us_s, gq, gk, gv = _proj(xs, mod_s, wts, win_parts, l, consts, seq=s_seq, rope=True)
        mla_s, gqa_s, *ffn = _attention(b3(qm, n_sb), [(km_c, vm_c), (b3(km, n_sb), vm)],
                                        b3(gq, n_sb), [(gk_c, gv_c), (b3(gk, n_sb), gv)], tq=512,
                                        cast=(w_out, w_ffn_gate, w_ffn_up, w_ffn_down), layer=l)

        qm, km, vm, uc, us, gq, gk, gv, *caches = _proj(
            xp, mod_p, wts, win_parts, l, consts, seq=p_seq, rope=False, emit_cache=True, prev_cache=caches)
        mla_o, gqa_o = _attention(b3(qm, n_pb), [(b3(km, n_pb), vm)], b3(gq, n_pb), [(b3(gk, n_pb), gv)],
                                  tq=p_seq, bb=4, ahead=8)
        fn_p = _fnet(dft_p, uc, us, tm=p_seq, tn=1024)
        fn_s = _fnet(dft_s, uc_s, us_s, tm=512, tn=n_sb * FNET_WIDTH)
        flat = lambda a: a.reshape(-1, MIX_TILE)
        xp, xs, *mod_next = _post(
            [(xp, flat(mla_o), fn_p, flat(gqa_o), p_seq), (xs, flat(mla_s), fn_s, flat(gqa_s), s_seq)],
            mod.reshape(1 + n_sb, 1, 6 * D_MODEL), wts["g2"], ffn, l, gf, final=final, tm=POST_TILE,
            next_mod=None if final else (cond_t, w_ada, b_ada3))
        if mod_next:
            mod = mod_next[0]

    ckv_new, krope_t, gk_t, gv_t = caches
    heads = lambda a: jnp.swapaxes(a, 2, 3).reshape(n_pb, DEPTH, p_seq, GQA_KV_HEADS, GQA_HEAD_DIM)
    return (xp.reshape(n_pb, p_seq, D_MODEL), xs.reshape(n_sb, s_seq, D_MODEL),
            ckv_new, jnp.swapaxes(krope_t, 2, 3), heads(gk_t), heads(gv_t))
```

```python
import functools

import numpy as np
import jax
import jax.numpy as jnp
from jax import lax
from jax.experimental import pallas as pl
from jax.experimental.pallas import tpu as pltpu

D_MODEL = 1024
DEPTH = 2
GRID_W = 64
MLA_HEADS = 6
MLA_Q_RANK = 384
MLA_KV_RANK = 256
MLA_NOPE_DIM = 64
MLA_ROPE_DIM = 32
MLA_V_DIM = 64
MLA_QK_DIM = MLA_NOPE_DIM + MLA_ROPE_DIM
FNET_GROUPS = 4
FNET_GROUP_DIM = 64
FNET_WIDTH = FNET_GROUPS * FNET_GROUP_DIM
GQA_HEADS = 6
GQA_KV_HEADS = 2
GQA_HEAD_DIM = 64
GQA_GROUP = GQA_HEADS // GQA_KV_HEADS
GQA_WIDTH = GQA_HEADS * GQA_HEAD_DIM
GQA_KV_WIDTH = GQA_KV_HEADS * GQA_HEAD_DIM
D_FF = 2816
ROPE_THETA = 10000.0
EPS = 1e-6
LOG2_E = 1.4426950408889634

LANES = 128
HALF = LANES // 2
HEAD_PAIRS = MLA_HEADS // 2
MLA_PAD_WIDTH = MLA_HEADS * LANES
MIX_TILE = HEAD_PAIRS * LANES
MIB = 1024 * 1024
VMEM_MIB = {"ada": 52, "ctx_prep": 52, "proj": 52, "attn_prompt": 52, "attn_latent": 52, "fnet": 52, "post": 52}


def _params(n_grid_dims, kernel_name):
    return pltpu.CompilerParams(dimension_semantics=("arbitrary",) * n_grid_dims,
                                vmem_limit_bytes=VMEM_MIB[kernel_name] * MIB)
POST_TILE = 512
PROJ_SUB = 256

_IN_A = MLA_Q_RANK + MLA_KV_RANK
_IN_C0 = _IN_A + MLA_ROPE_DIM
_IN_C = FNET_WIDTH + GQA_WIDTH + 2 * GQA_KV_WIDTH
_C_U, _C_GQ, _C_GK, _C_GV = 0, FNET_WIDTH, FNET_WIDTH + GQA_WIDTH, FNET_WIDTH + GQA_WIDTH + GQA_KV_WIDTH

BF16 = jnp.bfloat16
F32 = jnp.float32


def _rope_tables(n_tokens):
    t = np.arange(n_tokens)
    row = (t // GRID_W).astype(np.float64)
    col = (t % GRID_W).astype(np.float64)

    def angles(rot_dim):
        n_axis = rot_dim // 4
        inv = ROPE_THETA ** (-np.arange(n_axis, dtype=np.float64) / n_axis)
        return np.concatenate([row[:, None] * inv, col[:, None] * inv], axis=-1)

    def tables(ang, lane_to_pair, is_first, is_second):
        cos = np.where((is_first | is_second)[None, :], np.cos(ang)[:, lane_to_pair], 1.0)
        sin = np.sin(ang)[:, lane_to_pair]
        sin_a = np.where(is_second[None, :], sin, 0.0)
        sin_b = np.where(is_first[None, :], -sin, 0.0)
        return [cos, sin_a, sin_b]

    lane = np.arange(LANES)
    half_m = MLA_ROPE_DIM // 2
    in_rope = lane < MLA_ROPE_DIM
    first_m = lane < half_m
    second_m = in_rope & ~first_m
    pair_m = lane % half_m
    half_g = GQA_HEAD_DIM // 2
    first_g = (lane % GQA_HEAD_DIM) < half_g
    pair_g = lane % half_g
    tabs = (tables(angles(MLA_ROPE_DIM), pair_m, first_m, second_m)
            + tables(angles(GQA_HEAD_DIM), pair_g, first_g, ~first_g))
    return jnp.asarray(np.concatenate(tabs, axis=-1), dtype=F32)


def _channel_dft():
    c = np.arange(FNET_GROUP_DIM)
    ang = 2.0 * np.pi * np.outer(c, c) / FNET_GROUP_DIM
    eye = np.eye(FNET_GROUPS)
    table = np.concatenate([np.kron(eye, np.cos(ang)), np.kron(eye, np.sin(ang))], axis=1)
    return jnp.asarray(table, dtype=F32).astype(BF16)


def _group_mean_matrix(n_heads):
    return jnp.asarray(np.kron(np.eye(n_heads), np.full((GQA_HEAD_DIM, GQA_HEAD_DIM), 1.0 / GQA_HEAD_DIM)), dtype=BF16)


def _rope_place_matrix():
    return jnp.asarray(np.eye(MLA_ROPE_DIM, LANES), dtype=BF16)


DFT_SPLIT = 32


def _seq_dft_matrix(seq):
    s = np.arange(seq)
    ang = 2.0 * np.pi * ((s[:, None] * s[None, :]) % seq) / seq
    return jnp.asarray(np.stack([np.cos(ang), np.sin(ang)]), dtype=F32).astype(BF16)


def _seq_dft_tables(seq):
    s = np.arange(seq)
    n_a = seq // DFT_SPLIT
    ang_a = 2.0 * np.pi * ((np.arange(n_a)[:, None] * s[None, :]) % n_a) / n_a
    ang_b = 2.0 * np.pi * ((np.arange(DFT_SPLIT)[:, None] * s[None, :]) % seq) / seq
    return tuple(jnp.asarray(t, dtype=F32) for t in (np.cos(ang_a), np.sin(ang_a), np.cos(ang_b), np.sin(ang_b)))


def _rms(x):
    return x * lax.rsqrt(jnp.mean(x * x, axis=-1, keepdims=True) + EPS)


def _dot(a, b):
    return jnp.dot(a, b, preferred_element_type=F32)


def _group_rms(x, mean_mat):
    ms = _dot((x * x).astype(BF16), mean_mat)
    return x * lax.rsqrt(ms + EPS)


def _rotate(x, cos, sin_a, sin_b, half):
    return x * cos + pltpu.roll(x, half, 1) * sin_a + pltpu.roll(x, LANES - half, 1) * sin_b


def _lower_half(rows):
    return lax.broadcasted_iota(jnp.int32, (rows, LANES), 1) < HALF


def _layer_spec(arr, layer):
    return pl.BlockSpec((None,) + arr.shape[1:], lambda i: (layer, 0, 0))


def _ada_kernel(ct_ref, w_ref, b_ref, o_ref):
    s = ct_ref[...]
    s = s * jax.nn.sigmoid(s)
    w = w_ref[...]
    for m in range(o_ref.shape[0]):
        o_ref[m:m + 1, :] = jnp.sum(w * s[:, m:m + 1], axis=0, keepdims=True) + b_ref[...]


def _ada_specs(cond_t, w_ada, layer, tn):
    n_cond = cond_t.shape[1]
    in_specs = [pl.BlockSpec((D_MODEL, n_cond), lambda j: (0, 0)),
                pl.BlockSpec((None, D_MODEL, tn), lambda j: (layer, 0, j)),
                pl.BlockSpec((None, 1, tn), lambda j: (layer, 0, j))]
    out_shape = jax.ShapeDtypeStruct((n_cond, w_ada.shape[2]), F32)
    return in_specs, pl.BlockSpec((n_cond, tn), lambda j: (0, j)), out_shape


def _ada(cond_t, w_ada, b_ada, layer, tn=1024):
    in_specs, out_spec, out_shape = _ada_specs(cond_t, w_ada, layer, tn)
    return pl.pallas_call(
        _ada_kernel,
        grid=(w_ada.shape[2] // tn,),
        in_specs=in_specs,
        out_specs=out_spec,
        out_shape=out_shape,
        compiler_params=_params(1, "ada"),
        name="ada",
    )(cond_t, w_ada, b_ada)


_PROJ_WEIGHTS = ("g1", "w_a", "w_c", "w_kr", "g_q_a", "wq", "g_kv_a", "wk", "wv", "g_q_head", "g_k_head")


def _proj_kernel(*refs, rope, emit_cache, n_prev, multi_seq):
    (x_ref, mod_ref, g1_ref, wa_ref, wc_ref, wkr_ref, gqa_ref, wq_ref, gkva_ref, wk_ref, wv_ref,
     gqh_ref, gkh_ref, wdft_ref, mq_ref, mk_ref) = refs[:16]
    refs = refs[16:]
    if multi_seq:
        sdft_ref, refs = refs[0], refs[1:]
    if rope:
        tab_ref, refs = refs[0], refs[1:]
    if n_prev:
        prev_refs, refs = refs[:4], refs[4:]
    n_fourier = 1 if multi_seq else 2
    qm_ref, km_ref, vm_ref = refs[:3]
    fourier_refs = refs[3:3 + n_fourier]
    gq_ref, gk_ref, gv_ref = refs[3 + n_fourier:6 + n_fourier]
    cache_refs = refs[6 + n_fourier:]

    shift = mod_ref[:, 0:D_MODEL]
    scale = mod_ref[:, D_MODEL:2 * D_MODEL]
    lower = _lower_half(PROJ_SUB)

    for r in range(x_ref.shape[0] // PROJ_SUB):
        rows = slice(r * PROJ_SUB, (r + 1) * PROJ_SUB)
        seq_cols = (slice(None), slice(r * FNET_WIDTH, (r + 1) * FNET_WIDTH)) if multi_seq else (rows, slice(None))
        t_idx = (r,) if multi_seq else (slice(None), rows)

        hb = (_rms(x_ref[rows, :]) * g1_ref[...] * (1.0 + scale) + shift).astype(BF16)
        pa = _dot(hb, wa_ref[...])
        pc = _dot(hb, wc_ref[...])
        kr = _dot(hb, wkr_ref[...])

        if rope:
            cos_m, sa_m, sb_m = (tab_ref[rows, i * LANES:(i + 1) * LANES] for i in range(3))
            cos_g, sa_g, sb_g = (tab_ref[rows, i * LANES:(i + 1) * LANES] for i in range(3, 6))

        gqn = _group_rms(pc[:, _C_GQ:_C_GQ + GQA_WIDTH], mq_ref[...]) * gqh_ref[...]
        tiles = []
        for p in range(HEAD_PAIRS):
            gp = gqn[:, p * LANES:(p + 1) * LANES]
            if rope:
                gp = _rotate(gp, cos_g, sa_g, sb_g, GQA_HEAD_DIM // 2)
            tiles.append(gp * (LOG2_E * GQA_HEAD_DIM ** -0.5))
        gq_ref[rows, 0:LANES] = jnp.where(lower, tiles[0], tiles[1]).astype(BF16)
        gq_ref[rows, LANES:2 * LANES] = jnp.where(
            lower, pltpu.roll(tiles[0], HALF, 1), pltpu.roll(tiles[2], HALF, 1)).astype(BF16)
        gq_ref[rows, 2 * LANES:] = jnp.where(lower, tiles[1], tiles[2]).astype(BF16)
        gkn = _group_rms(pc[:, _C_GK:_C_GK + GQA_KV_WIDTH], mk_ref[...]) * gkh_ref[...]
        gkr = _rotate(gkn, cos_g, sa_g, sb_g, GQA_HEAD_DIM // 2) if rope else gkn
        gv_t = pc[:, _C_GV:_C_GV + GQA_KV_WIDTH].T
        gk_ref[rows, :] = gkr.astype(BF16)
        gv_ref[t_idx] = gv_t.astype(BF16)

        cqn = _rms(pa[:, :MLA_Q_RANK]) * gqa_ref[...]
        q = _dot(cqn.astype(BF16), wq_ref[...])
        for hd in range(MLA_HEADS):
            qh = q[:, hd * LANES:(hd + 1) * LANES]
            if rope:
                qh = _rotate(qh, cos_m, sa_m, sb_m, MLA_ROPE_DIM // 2)
            qm_ref[rows, hd * LANES:(hd + 1) * LANES] = (qh * (LOG2_E * MLA_QK_DIM ** -0.5)).astype(BF16)

        ckvn = _rms(pa[:, MLA_Q_RANK:]) * gkva_ref[...]
        cb = ckvn.astype(BF16)
        kn = _dot(cb, wk_ref[...])
        krr = _rotate(kr, cos_m, sa_m, sb_m, MLA_ROPE_DIM // 2) if rope else kr
        for hd in range(MLA_HEADS):
            km_ref[rows, hd * LANES:(hd + 1) * LANES] = (kn[:, hd * LANES:(hd + 1) * LANES] + krr).astype(BF16)
        vm_ref[t_idx] = _dot(cb, wv_ref[...]).T.astype(BF16)

        ucs = _dot(pc[:, _C_U:_C_U + FNET_WIDTH].astype(BF16), wdft_ref[...])
        uc = ucs[:, :FNET_WIDTH].astype(BF16)
        us = ucs[:, FNET_WIDTH:].astype(BF16)
        if multi_seq:
            fn = _dot(sdft_ref[0], uc) - _dot(sdft_ref[1], us)
            fourier_refs[0][seq_cols] = (fn * (PROJ_SUB * FNET_GROUP_DIM) ** -0.5).astype(BF16)
        else:
            fourier_refs[0][seq_cols] = uc
            fourier_refs[1][seq_cols] = us

        if emit_cache:
            new = (ckvn, kr.T[:MLA_ROPE_DIM, :], gkn.T, gv_t)
            for i, out_ref in enumerate(cache_refs):
                if n_prev:
                    out_ref[r, :n_prev] = prev_refs[i][r]
                out_ref[r, n_prev] = new[i]


def _proj(x, mod, wts, win_parts, layer, consts, *, seq, rope, prev_cache=None, emit_cache=False, tm=512):
    n_tok = x.shape[0]
    n_tiles = n_tok // tm
    n_seq = n_tok // seq
    tiles_per_mod = n_tiles // mod.shape[0]
    multi_seq = tm > seq
    if multi_seq:
        assert seq == PROJ_SUB and not rope
        spt = tm // seq
        seq_major = pl.BlockSpec((seq, spt * FNET_WIDTH), lambda i: (0, i))
        transposed = lambda w: pl.BlockSpec((spt, w, seq), lambda i: (i, 0, 0))
    else:
        assert not emit_cache
        tps = seq // tm
        seq_major = pl.BlockSpec((tm, FNET_WIDTH), lambda i: (i % tps, i // tps))
        transposed = lambda w: pl.BlockSpec((None, w, tm), lambda i: (i // tps, 0, i % tps))
    const = lambda arr: pl.BlockSpec(arr.shape, lambda i: (0,) * arr.ndim)
    tok = lambda w: pl.BlockSpec((tm, w), lambda i: (i, 0))
    in_specs = [tok(D_MODEL), pl.BlockSpec((None, 1, 6 * D_MODEL), lambda i: (i // tiles_per_mod, 0, 0))]
    args = [x, mod]
    for name in _PROJ_WEIGHTS:
        arr = win_parts.get(name)
        in_specs.append(const(arr) if arr is not None else _layer_spec(wts[name], layer))
        args.append(arr if arr is not None else wts[name])
    for name in ("wdft", "mean_q", "mean_k"):
        in_specs.append(const(consts[name]))
        args.append(consts[name])
    if multi_seq:
        in_specs.append(const(consts["seq_dft"]))
        args.append(consts["seq_dft"])
    if rope:
        in_specs.append(pl.BlockSpec((tm, 6 * LANES), lambda i: (i % tps, 0)))
        args.append(consts["rope"])
    n_fourier = 1 if multi_seq else 2
    out_specs = [tok(MLA_PAD_WIDTH), tok(MLA_PAD_WIDTH), transposed(MIX_TILE)] + [seq_major] * n_fourier + [
                 tok(GQA_WIDTH), tok(GQA_KV_WIDTH), transposed(GQA_KV_WIDTH)]
    out_shape = [jax.ShapeDtypeStruct((n_tok, MLA_PAD_WIDTH), BF16),
                 jax.ShapeDtypeStruct((n_tok, MLA_PAD_WIDTH), BF16),
                 jax.ShapeDtypeStruct((n_seq, MIX_TILE, seq), BF16)] + [
                 jax.ShapeDtypeStruct((seq, n_seq * FNET_WIDTH), BF16)] * n_fourier + [
                 jax.ShapeDtypeStruct((n_tok, GQA_WIDTH), BF16),
                 jax.ShapeDtypeStruct((n_tok, GQA_KV_WIDTH), BF16),
                 jax.ShapeDtypeStruct((n_seq, GQA_KV_WIDTH, seq), BF16)]
    n_prev = 0
    if emit_cache:
        n_prev = prev_cache[0].shape[1] if prev_cache is not None else 0
        layers = lambda n, tail: pl.BlockSpec((spt, n) + tail, lambda i: (i, 0, 0, 0))
        tails = ((seq, MLA_KV_RANK), (MLA_ROPE_DIM, seq), (GQA_KV_WIDTH, seq), (GQA_KV_WIDTH, seq))
        if n_prev:
            in_specs += [layers(n_prev, t) for t in tails]
            args += list(prev_cache)
        out_specs += [layers(n_prev + 1, t) for t in tails]
        out_shape += [jax.ShapeDtypeStruct((n_seq, n_prev + 1) + t, F32) for t in tails]
    return pl.pallas_call(
        functools.partial(_proj_kernel, rope=rope, emit_cache=emit_cache, n_prev=n_prev, multi_seq=multi_seq),
        grid=(n_tiles,),
        in_specs=in_specs,
        out_specs=out_specs,
        out_shape=out_shape,
        compiler_params=_params(1, "proj"),
        name="proj_rope" if rope else "proj_ctx",
    )(*args)


def _ctx_kernel(ckv_ref, kr_ref, gk_ref, gv_ref, wk_ref, wv_ref, place_ref, win_ref,
                km_ref, vm_ref, gko_ref, gvo_ref, wa_ref, wc_ref, wkr_ref):
    wt = win_ref[...]
    wa_ref[...] = wt[:_IN_A, :].T.astype(BF16)
    wc_ref[...] = wt[_IN_C0:, :].T.astype(BF16)
    kr_rows = jnp.concatenate([wt[_IN_A:_IN_C0, :], jnp.zeros((LANES - MLA_ROPE_DIM, wt.shape[1]), F32)], axis=0)
    wkr_ref[...] = kr_rows.T.astype(BF16)

    cb = ckv_ref[...].astype(BF16)
    kn = _dot(cb, wk_ref[...])
    kr = _dot(kr_ref[...].astype(BF16), place_ref[...])
    for hd in range(MLA_HEADS):
        km_ref[:, hd * LANES:(hd + 1) * LANES] = (kn[:, hd * LANES:(hd + 1) * LANES] + kr).astype(BF16)
    vm_ref[...] = _dot(cb, wv_ref[...]).T.astype(BF16)
    gko_ref[...] = gk_ref[...].astype(BF16)
    gvo_ref[...] = gv_ref[...].T.astype(BF16)


def _ctx_prep(layer, cache_ckv, cache_krope, cache_gk, cache_gv, w_in_t, wts, consts):
    nb, _, past, _ = cache_ckv.shape
    rows = D_MODEL // nb
    cache = lambda w: pl.BlockSpec((None, None, past, w), lambda b: (b, layer, 0, 0))
    out = lambda w: pl.BlockSpec((None, past, w), lambda b: (b, 0, 0))
    out_t = lambda w: pl.BlockSpec((None, w, past), lambda b: (b, 0, 0))
    part = lambda w: pl.BlockSpec((rows, w), lambda b: (b, 0))
    part_widths = (_IN_A, _IN_C, LANES)
    return pl.pallas_call(
        _ctx_kernel,
        grid=(nb,),
        in_specs=[cache(MLA_KV_RANK), cache(MLA_ROPE_DIM), cache(GQA_KV_WIDTH), cache(GQA_KV_WIDTH),
                  _layer_spec(wts["wk"], layer), _layer_spec(wts["wv"], layer),
                  pl.BlockSpec((MLA_ROPE_DIM, LANES), lambda b: (0, 0)),
                  pl.BlockSpec((None, w_in_t.shape[1], rows), lambda b: (layer, 0, b))],
        out_specs=[out(MLA_PAD_WIDTH), out_t(MIX_TILE), out(GQA_KV_WIDTH), out_t(GQA_KV_WIDTH)]
                  + [part(w) for w in part_widths],
        out_shape=[jax.ShapeDtypeStruct((nb, past, MLA_PAD_WIDTH), BF16),
                   jax.ShapeDtypeStruct((nb, MIX_TILE, past), BF16),
                   jax.ShapeDtypeStruct((nb, past, GQA_KV_WIDTH), BF16),
                   jax.ShapeDtypeStruct((nb, GQA_KV_WIDTH, past), BF16)]
                  + [jax.ShapeDtypeStruct((D_MODEL, w), BF16) for w in part_widths],
        compiler_params=_params(1, "ctx_prep"),
        name="ctx_prep",
    )(cache_ckv, cache_krope, cache_gk.reshape(nb, DEPTH, past, GQA_KV_WIDTH),
      cache_gv.reshape(nb, DEPTH, past, GQA_KV_WIDTH), wts["wk"], wts["wv"], consts["place"], w_in_t)


def _attn_kernel(*refs, n_seg, ahead, n_cast):
    n_in = 1 + 2 * n_seg
    outs_at = 2 * n_in + n_cast
    families = []
    for f, mla in enumerate((True, False)):
        ins = refs[f * n_in:(f + 1) * n_in]
        families.append((mla, ins[0], ins[1:1 + n_seg], ins[1 + n_seg:], refs[outs_at + f]))
    for c in range(n_cast):
        refs[outs_at + 2 + c][...] = refs[2 * n_in + c][...].astype(BF16)
    n_seqs, tq = refs[0].shape[:2]
    lower = _lower_half(tq)
    top = lax.broadcasted_iota(jnp.int32, (LANES, tq), 0) < HALF

    def scores_t(f, b, p, hd):
        mla, q_ref, k_refs, _, _ = families[f]
        if mla:
            cols = slice((2 * p + hd) * LANES, (2 * p + hd + 1) * LANES)
            qh = q_ref[b, :, cols]
            keys = [k_ref[b, :, cols] for k_ref in k_refs]
        else:
            q = q_ref[b, :, p * LANES:(p + 1) * LANES]
            qh = jnp.where(lower if hd == 0 else ~lower, q, jnp.zeros_like(q))
            keys = [k_ref[b] for k_ref in k_refs]
        return [lax.dot_general(k, qh, (((1,), (1,)), ((), ())), preferred_element_type=F32) for k in keys]

    def softmax_t(ss):
        m = functools.reduce(jnp.maximum, [jnp.max(s, axis=0, keepdims=True) for s in ss])
        es = [jnp.exp2(s - m) for s in ss]
        denom = functools.reduce(jnp.add, [jnp.sum(e, axis=0, keepdims=True) for e in es])
        return [e.astype(BF16) for e in es], denom

    def weighted_values_t(f, b, p, es, denom):
        mla, _, _, vt_refs, _ = families[f]
        rows = slice(p * LANES, (p + 1) * LANES) if mla else slice(None)
        acc = functools.reduce(jnp.add, [_dot(vt_ref[b, rows, :], e) for e, vt_ref in zip(es, vt_refs)])
        return acc * (1.0 / denom)

    work = [(f, b, p, hd) for f in range(2) for b in range(n_seqs) for p in range(HEAD_PAIRS) for hd in range(2)]
    pending = [scores_t(*w) for w in work[:ahead]]
    outs = {}
    for i, (f, b, p, hd) in enumerate(work):
        es, denom = softmax_t(pending.pop(0))
        if i + ahead < len(work):
            pending.append(scores_t(*work[i + ahead]))
        outs[f, b, p, hd] = weighted_values_t(f, b, p, es, denom)
        if (p, hd) != (HEAD_PAIRS - 1, 1):
            continue
        mla, o_ref = families[f][0], families[f][4]
        tiles = [jnp.where(top, outs[f, b, t, 0], outs[f, b, t, 1]).T for t in range(HEAD_PAIRS)]
        if not mla:
            t1 = pltpu.roll(tiles[1], HALF, 1)
            tiles = [jnp.where(lower, tiles[0], t1), jnp.where(lower, tiles[2], tiles[0]),
                     jnp.where(lower, t1, tiles[2])]
        for t in range(HEAD_PAIRS):
            o_ref[b, :, t * LANES:(t + 1) * LANES] = tiles[t].astype(o_ref.dtype)


def _attention(mla_q, mla_kv, gqa_q, gqa_kv, *, tq, bb=1, ahead=2, cast=(), layer=0):
    n_batch, sq = mla_q.shape[:2]
    n_seg = len(mla_kv)
    n_q = sq // tq
    steps = (n_batch // bb) * n_q
    whole = lambda a: pl.BlockSpec((bb,) + a.shape[1:], lambda b, i: (b, 0, 0))
    in_specs, args = [], []
    for q, kv in ((mla_q, mla_kv), (gqa_q, gqa_kv)):
        in_specs += [pl.BlockSpec((bb, tq, q.shape[2]), lambda b, i: (b, i, 0))]
        in_specs += [whole(k) for k, _ in kv] + [whole(v) for _, v in kv]
        args += [q] + [k for k, _ in kv] + [v for _, v in kv]
    out_spec = pl.BlockSpec((bb, tq, MIX_TILE), lambda b, i: (b, i, 0))
    out_specs = [out_spec, out_spec]
    out_shape = [jax.ShapeDtypeStruct((n_batch, sq, MIX_TILE), BF16)] * 2
    for w in cast:
        rows, cols = w.shape[1] // steps, w.shape[2]
        in_specs.append(pl.BlockSpec((None, rows, cols), lambda b, i: (layer, b * n_q + i, 0)))
        out_specs.append(pl.BlockSpec((rows, cols), lambda b, i: (b * n_q + i, 0)))
        out_shape.append(jax.ShapeDtypeStruct(w.shape[1:], BF16))
        args.append(w)
    return pl.pallas_call(
        functools.partial(_attn_kernel, n_seg=n_seg, ahead=ahead, n_cast=len(cast)),
        grid=(n_batch // bb, n_q),
        in_specs=in_specs,
        out_specs=out_specs,
        out_shape=out_shape,
        compiler_params=_params(2, "attn_latent" if n_seg > 1 else "attn_prompt"),
        name=f"attn_s{n_seg}",
    )(*args)


def _fnet_kernel(ca_ref, sa_ref, cb_ref, sb_ref, uc_ref, us_ref, o_ref, tc_ref, ts_ref, *, scale):
    @pl.when(pl.program_id(1) == 0)
    def _build_twiddles():
        cb, sb = cb_ref[...], sb_ref[...]
        for a in range(ca_ref.shape[0]):
            ca, sa = ca_ref[a:a + 1, :], sa_ref[a:a + 1, :]
            rows = slice(a * DFT_SPLIT, (a + 1) * DFT_SPLIT)
            tc_ref[rows, :] = (ca * cb - sa * sb).astype(BF16)
            ts_ref[rows, :] = (sa * cb + ca * sb).astype(BF16)

    acc = _dot(tc_ref[...], uc_ref[...]) - _dot(ts_ref[...], us_ref[...])
    o_ref[...] = (acc * scale).astype(o_ref.dtype)


def _fnet(tables, uc, us, tm, tn):
    seq, width = uc.shape
    scale = float((seq * FNET_GROUP_DIM) ** -0.5)
    n_a = tm // DFT_SPLIT
    part_a = pl.BlockSpec((n_a, seq), lambda i, j: (i, 0))
    part_b = pl.BlockSpec((DFT_SPLIT, seq), lambda i, j: (0, 0))
    data = pl.BlockSpec((seq, tn), lambda i, j: (0, j))
    return pl.pallas_call(
        functools.partial(_fnet_kernel, scale=scale),
        grid=(seq // tm, width // tn),
        in_specs=[part_a, part_a, part_b, part_b, data, data],
        out_specs=pl.BlockSpec((tm, tn), lambda i, j: (i, j)),
        out_shape=jax.ShapeDtypeStruct((seq, width), BF16),
        scratch_shapes=[pltpu.VMEM((tm, seq), BF16), pltpu.VMEM((tm, seq), BF16)],
        compiler_params=_params(2, "fnet"),
        name="fnet",
    )(*tables, uc, us)


def _post_body(x_ref, mla_ref, fn_ref, gqa_ref, o_ref, mod_ref, g2_ref, wo_ref, wg_ref, wu_ref, wd_ref, gf_ref, final):
    fn = fn_ref[...]
    if fn.shape[1] > FNET_WIDTH:
        fn = jnp.concatenate([fn[:, j:j + FNET_WIDTH] for j in range(0, fn.shape[1], FNET_WIDTH)], axis=0)
    mix = jnp.concatenate([mla_ref[...], fn, gqa_ref[...]], axis=-1)
    gate1 = mod_ref[:, 2 * D_MODEL:3 * D_MODEL]
    shift2 = mod_ref[:, 3 * D_MODEL:4 * D_MODEL]
    scale2 = mod_ref[:, 4 * D_MODEL:5 * D_MODEL]
    gate2 = mod_ref[:, 5 * D_MODEL:6 * D_MODEL]
    x = x_ref[...] + gate1 * _dot(mix, wo_ref[...])
    h = (_rms(x) * g2_ref[...] * (1.0 + scale2) + shift2).astype(BF16)
    g = _dot(h, wg_ref[...])
    u = _dot(h, wu_ref[...])
    a = (g * jax.nn.sigmoid(g) * u).astype(BF16)
    x = x + gate2 * _dot(a, wd_ref[...])
    if final:
        x = _rms(x) * gf_ref[...]
    o_ref[...] = x


def _post_kernel(*refs, final, n_first, with_next_mod):
    first, second, shared = refs[0:4], refs[4:8], refs[8:15]
    n_in = 18 if with_next_mod else 15
    o_first, o_second = refs[n_in:n_in + 2]
    step = pl.program_id(0)

    def run(group, o_ref):
        if with_next_mod:
            _ada_kernel(*refs[15:18], refs[n_in + 2])
        _post_body(*group, o_ref, *shared, final)

    @pl.when(step < n_first)
    def _first_group():
        run(first, o_first)

    @pl.when(step >= n_first)
    def _second_group():
        run(second, o_second)


def _post(groups, mod, g2, ffn, layer, g_final, *, final, tm, next_mod=None):
    (x1, _, _, _, seq1), (x2, _, _, _, seq2) = groups
    n1, n2 = x1.shape[0] // tm, x2.shape[0] // tm
    tiles_per_mod = n2 // (mod.shape[0] - 1)
    tile1 = lambda i: jnp.minimum(i, n1 - 1)
    tile2 = lambda i: jnp.maximum(i - n1, 0)
    resident = lambda arr: pl.BlockSpec(arr.shape, lambda i: (0, 0), pipeline_mode=pl.Buffered(1))

    def group_specs(seq, tile):
        tok = lambda w: pl.BlockSpec((tm, w), lambda i: (tile(i), 0))
        if tm <= seq:
            tps = seq // tm
            fnet_spec = pl.BlockSpec((tm, FNET_WIDTH), lambda i: (tile(i) % tps, tile(i) // tps))
        else:
            fnet_spec = pl.BlockSpec((seq, (tm // seq) * FNET_WIDTH), lambda i: (0, tile(i)))
        return [tok(D_MODEL), tok(MIX_TILE), fnet_spec, tok(MIX_TILE)], tok(D_MODEL)

    in1, out1 = group_specs(seq1, tile1)
    in2, out2 = group_specs(seq2, tile2)
    mod_spec = pl.BlockSpec((None, 1, 6 * D_MODEL),
                            lambda i: (jnp.where(i < n1, 0, 1 + tile2(i) // tiles_per_mod), 0, 0))
    in_specs = (in1 + in2 + [mod_spec, _layer_spec(g2, layer)] + [resident(w) for w in ffn]
                + [pl.BlockSpec((1, D_MODEL), lambda i: (0, 0))])
    args = [*groups[0][:4], *groups[1][:4], mod, g2, *ffn, g_final]
    out_specs = [out1, out2]
    out_shape = [jax.ShapeDtypeStruct(x1.shape, F32), jax.ShapeDtypeStruct(x2.shape, F32)]
    if next_mod is not None:
        cond_t, w_ada, b_ada = next_mod
        ada_in, ada_out, ada_shape = _ada_specs(cond_t, w_ada, layer + 1, w_ada.shape[2] // (n1 + n2))
        in_specs += ada_in
        args += [cond_t, w_ada, b_ada]
        out_specs.append(ada_out)
        out_shape.append(ada_shape)
    return pl.pallas_call(
        functools.partial(_post_kernel, final=final, n_first=n1, with_next_mod=next_mod is not None),
        grid=(n1 + n2,),
        in_specs=in_specs,
        out_specs=out_specs,
        out_shape=out_shape,
        compiler_params=_params(1, "post"),
        name="post_final" if final else "post",
    )(*args)


def _prep_weights(g_norm1, g_norm2, g_q_a, w_q_up, g_kv_a, w_kv_up, g_q_head, g_k_head):
    row = lambda g: g.reshape(DEPTH, 1, -1)
    wq = w_q_up.reshape(DEPTH, MLA_Q_RANK, MLA_HEADS, MLA_QK_DIM)
    wq = jnp.concatenate([wq[..., MLA_NOPE_DIM:], wq[..., :MLA_NOPE_DIM]], axis=-1)
    wq = jnp.pad(wq, ((0, 0), (0, 0), (0, 0), (0, LANES - MLA_QK_DIM)))
    wkv = w_kv_up.reshape(DEPTH, MLA_KV_RANK, MLA_HEADS, MLA_NOPE_DIM + MLA_V_DIM)
    wk = jnp.pad(wkv[..., :MLA_NOPE_DIM], ((0, 0), (0, 0), (0, 0), (MLA_ROPE_DIM, LANES - MLA_QK_DIM)))
    wv = wkv[..., MLA_NOPE_DIM:]
    return {
        "g1": row(g_norm1), "g2": row(g_norm2), "g_q_a": row(g_q_a), "g_kv_a": row(g_kv_a),
        "g_q_head": row(jnp.tile(g_q_head, (1, GQA_HEADS))), "g_k_head": row(jnp.tile(g_k_head, (1, GQA_KV_HEADS))),
        "wq": wq.reshape(DEPTH, MLA_Q_RANK, MLA_PAD_WIDTH).astype(BF16),
        "wk": wk.reshape(DEPTH, MLA_KV_RANK, MLA_PAD_WIDTH).astype(BF16),
        "wv": wv.reshape(DEPTH, MLA_KV_RANK, MIX_TILE).astype(BF16),
    }


def kernel(x_prompt, x_sample, cache_mla_ckv, cache_mla_krope, cache_gqa_k, cache_gqa_v, c, c_ctx, w_ada, b_ada,
           g_norm1, g_norm2, w_in, g_q_a, w_q_up, g_kv_a, w_kv_up, g_q_head, g_k_head, w_out,
           w_ffn_gate, w_ffn_up, w_ffn_down, g_final):
    n_pb, p_seq, _ = x_prompt.shape
    n_sb, s_seq, _ = x_sample.shape
    consts = {"wdft": _channel_dft(), "mean_q": _group_mean_matrix(GQA_HEADS),
              "mean_k": _group_mean_matrix(GQA_KV_HEADS), "place": _rope_place_matrix(),
              "rope": _rope_tables(s_seq), "seq_dft": _seq_dft_matrix(p_seq)}
    dft_s = _seq_dft_tables(s_seq)
    gf = g_final.reshape(1, D_MODEL)
    wts = _prep_weights(g_norm1, g_norm2, g_q_a, w_q_up, g_kv_a, w_kv_up, g_q_head, g_k_head)
    w_in_t = jnp.swapaxes(w_in, 1, 2)

    cond_t = jnp.concatenate([c_ctx[None, :], c], axis=0).T
    b_ada3 = b_ada.reshape(DEPTH, 1, 6 * D_MODEL)
    mod = _ada(cond_t, w_ada, b_ada3, 0)

    xp = x_prompt.reshape(n_pb * p_seq, D_MODEL)
    xs = x_sample.reshape(n_sb * s_seq, D_MODEL)
    b3 = lambda a, nb: a.reshape(nb, a.shape[0] // nb, a.shape[1])
    caches = None
    for l in range(DEPTH):
        final = l == DEPTH - 1
        mod_p = mod[0:1].reshape(1, 1, 6 * D_MODEL)
        mod_s = mod[1:].reshape(n_sb, 1, 6 * D_MODEL)

        km_c, vm_c, gk_c, gv_c, w_a, w_c, w_kr = _ctx_prep(
            l, cache_mla_ckv, cache_mla_krope, cache_gqa_k, cache_gqa_v, w_in_t, wts, consts)
        win_parts = {"w_a": w_a, "w_c": w_c, "w_kr": w_kr}
        qm, km, vm, uc_s, us_s, gq, gk, gv = _proj(xs, mod_s, wts, win_parts, l, consts, seq=s_seq, rope=True)
        mla_s, gqa_s, *ffn = _attention(b3(qm, n_sb), [(km_c, vm_c), (b3(km, n_sb), vm)],
                                        b3(gq, n_sb), [(gk_c, gv_c), (b3(gk, n_sb), gv)], tq=512,
                                        cast=(w_out, w_ffn_gate, w_ffn_up, w_ffn_down), layer=l)

        qm, km, vm, fn_p, gq, gk, gv, *caches = _proj(
            xp, mod_p, wts, win_parts, l, consts, seq=p_seq, rope=False, emit_cache=True, prev_cache=caches)
        mla_o, gqa_o = _attention(b3(qm, n_pb), [(b3(km, n_pb), vm)], b3(gq, n_pb), [(b3(gk, n_pb), gv)],
                                  tq=p_seq, bb=4, ahead=8)
        fn_s = _fnet(dft_s, uc_s, us_s, tm=512, tn=n_sb * FNET_WIDTH)
        flat = lambda a: a.reshape(-1, MIX_TILE)
        xp, xs, *mod_next = _post(
            [(xp, flat(mla_o), fn_p, flat(gqa_o), p_seq), (xs, flat(mla_s), fn_s, flat(gqa_s), s_seq)],
            mod.reshape(1 + n_sb, 1, 6 * D_MODEL), wts["g2"], ffn, l, gf, final=final, tm=POST_TILE,
            next_mod=None if final else (cond_t, w_ada, b_ada3))
        if mod_next:
            mod = mod_next[0]

    ckv_new, krope_t, gk_t, gv_t = caches
    heads = lambda a: jnp.swapaxes(a, 2, 3).reshape(n_pb, DEPTH, p_seq, GQA_KV_HEADS, GQA_HEAD_DIM)
    return (xp.reshape(n_pb, p_seq, D_MODEL), xs.reshape(n_sb, s_seq, D_MODEL),
            ckv_new, jnp.swapaxes(krope_t, 2, 3), heads(gk_t), heads(gv_t))
```

```python
import functools

import numpy as np
import jax
import jax.numpy as jnp
from jax import lax
from jax.experimental import pallas as pl
from jax.experimental.pallas import tpu as pltpu

D_MODEL = 1024
DEPTH = 2
GRID_W = 64
MLA_HEADS = 6
MLA_Q_RANK = 384
MLA_KV_RANK = 256
MLA_NOPE_DIM = 64
MLA_ROPE_DIM = 32
MLA_V_DIM = 64
MLA_QK_DIM = MLA_NOPE_DIM + MLA_ROPE_DIM
FNET_GROUPS = 4
FNET_GROUP_DIM = 64
FNET_WIDTH = FNET_GROUPS * FNET_GROUP_DIM
GQA_HEADS = 6
GQA_KV_HEADS = 2
GQA_HEAD_DIM = 64
GQA_GROUP = GQA_HEADS // GQA_KV_HEADS
GQA_WIDTH = GQA_HEADS * GQA_HEAD_DIM
GQA_KV_WIDTH = GQA_KV_HEADS * GQA_HEAD_DIM
D_FF = 2816
ROPE_THETA = 10000.0
EPS = 1e-6
LOG2_E = 1.4426950408889634

LANES = 128
HALF = LANES // 2
HEAD_PAIRS = MLA_HEADS // 2
MLA_PAD_WIDTH = MLA_HEADS * LANES
MIX_TILE = HEAD_PAIRS * LANES
MIB = 1024 * 1024
VMEM_MIB = {"ada": 52, "ctx_prep": 52, "proj": 52, "attn_prompt": 52, "attn_latent": 52, "fnet": 52, "post": 52}


def _params(n_grid_dims, kernel_name):
    return pltpu.CompilerParams(dimension_semantics=("arbitrary",) * n_grid_dims,
                                vmem_limit_bytes=VMEM_MIB[kernel_name] * MIB)
POST_TILE = 512
PROJ_SUB = 256

_IN_A = MLA_Q_RANK + MLA_KV_RANK
_IN_C0 = _IN_A + MLA_ROPE_DIM
_IN_C = FNET_WIDTH + GQA_WIDTH + 2 * GQA_KV_WIDTH
_C_U, _C_GQ, _C_GK, _C_GV = 0, FNET_WIDTH, FNET_WIDTH + GQA_WIDTH, FNET_WIDTH + GQA_WIDTH + GQA_KV_WIDTH

BF16 = jnp.bfloat16
F32 = jnp.float32


def _rope_tables(n_tokens):
    t = np.arange(n_tokens)
    row = (t // GRID_W).astype(np.float64)
    col = (t % GRID_W).astype(np.float64)

    def angles(rot_dim):
        n_axis = rot_dim // 4
        inv = ROPE_THETA ** (-np.arange(n_axis, dtype=np.float64) / n_axis)
        return np.concatenate([row[:, None] * inv, col[:, None] * inv], axis=-1)

    def tables(ang, lane_to_pair, is_first, is_second):
        cos = np.where((is_first | is_second)[None, :], np.cos(ang)[:, lane_to_pair], 1.0)
        sin = np.sin(ang)[:, lane_to_pair]
        sin_a = np.where(is_second[None, :], sin, 0.0)
        sin_b = np.where(is_first[None, :], -sin, 0.0)
        return [cos, sin_a, sin_b]

    lane = np.arange(LANES)
    half_m = MLA_ROPE_DIM // 2
    in_rope = lane < MLA_ROPE_DIM
    first_m = lane < half_m
    second_m = in_rope & ~first_m
    pair_m = lane % half_m
    half_g = GQA_HEAD_DIM // 2
    first_g = (lane % GQA_HEAD_DIM) < half_g
    pair_g = lane % half_g
    tabs = (tables(angles(MLA_ROPE_DIM), pair_m, first_m, second_m)
            + tables(angles(GQA_HEAD_DIM), pair_g, first_g, ~first_g))
    return jnp.asarray(np.concatenate(tabs, axis=-1), dtype=F32)


def _channel_dft():
    c = np.arange(FNET_GROUP_DIM)
    ang = 2.0 * np.pi * np.outer(c, c) / FNET_GROUP_DIM
    eye = np.eye(FNET_GROUPS)
    table = np.concatenate([np.kron(eye, np.cos(ang)), np.kron(eye, np.sin(ang))], axis=1)
    return jnp.asarray(table, dtype=F32).astype(BF16)


def _group_mean_matrix(n_heads):
    return jnp.asarray(np.kron(np.eye(n_heads), np.full((GQA_HEAD_DIM, GQA_HEAD_DIM), 1.0 / GQA_HEAD_DIM)), dtype=BF16)


def _rope_place_matrix():
    return jnp.asarray(np.eye(MLA_ROPE_DIM, LANES), dtype=BF16)


DFT_SPLIT = 32


def _seq_dft_matrix(seq):
    s = np.arange(seq)
    ang = 2.0 * np.pi * ((s[:, None] * s[None, :]) % seq) / seq
    return jnp.asarray(np.stack([np.cos(ang), np.sin(ang)]), dtype=F32).astype(BF16)


def _seq_dft_tables(seq):
    s = np.arange(seq)
    n_a = seq // DFT_SPLIT
    ang_a = 2.0 * np.pi * ((np.arange(n_a)[:, None] * s[None, :]) % n_a) / n_a
    ang_b = 2.0 * np.pi * ((np.arange(DFT_SPLIT)[:, None] * s[None, :]) % seq) / seq
    return tuple(jnp.asarray(t, dtype=F32) for t in (np.cos(ang_a), np.sin(ang_a), np.cos(ang_b), np.sin(ang_b)))


def _rms(x):
    return x * lax.rsqrt(jnp.mean(x * x, axis=-1, keepdims=True) + EPS)


def _dot(a, b):
    return jnp.dot(a, b, preferred_element_type=F32)


def _group_rms(x, mean_mat):
    ms = _dot((x * x).astype(BF16), mean_mat)
    return x * lax.rsqrt(ms + EPS)


def _rotate(x, cos, sin_a, sin_b, half):
    return x * cos + pltpu.roll(x, half, 1) * sin_a + pltpu.roll(x, LANES - half, 1) * sin_b


def _lower_half(rows):
    return lax.broadcasted_iota(jnp.int32, (rows, LANES), 1) < HALF


def _layer_spec(arr, layer):
    return pl.BlockSpec((None,) + arr.shape[1:], lambda i: (layer, 0, 0))


def _ada_kernel(ct_ref, w_ref, b_ref, o_ref):
    s = ct_ref[...]
    s = s * jax.nn.sigmoid(s)
    w = w_ref[...]
    for m in range(o_ref.shape[0]):
        o_ref[m:m + 1, :] = jnp.sum(w * s[:, m:m + 1], axis=0, keepdims=True) + b_ref[...]


def _ada_specs(cond_t, w_ada, layer, tn):
    n_cond = cond_t.shape[1]
    in_specs = [pl.BlockSpec((D_MODEL, n_cond), lambda j: (0, 0)),
                pl.BlockSpec((None, D_MODEL, tn), lambda j: (layer, 0, j)),
                pl.BlockSpec((None, 1, tn), lambda j: (layer, 0, j))]
    out_shape = jax.ShapeDtypeStruct((n_cond, w_ada.shape[2]), F32)
    return in_specs, pl.BlockSpec((n_cond, tn), lambda j: (0, j)), out_shape


def _ada(cond_t, w_ada, b_ada, layer, tn=1024):
    in_specs, out_spec, out_shape = _ada_specs(cond_t, w_ada, layer, tn)
    return pl.pallas_call(
        _ada_kernel,
        grid=(w_ada.shape[2] // tn,),
        in_specs=in_specs,
        out_specs=out_spec,
        out_shape=out_shape,
        compiler_params=_params(1, "ada"),
        name="ada",
    )(cond_t, w_ada, b_ada)


_PROJ_WEIGHTS = ("g1", "w_a", "w_c", "w_kr", "g_q_a", "wq", "g_kv_a", "wk", "wv", "g_q_head", "g_k_head")


def _proj_kernel(*refs, rope, emit_cache, n_prev, multi_seq):
    (x_ref, mod_ref, g1_ref, wa_ref, wc_ref, wkr_ref, gqa_ref, wq_ref, gkva_ref, wk_ref, wv_ref,
     gqh_ref, gkh_ref, wdft_ref, mq_ref, mk_ref) = refs[:16]
    refs = refs[16:]
    if multi_seq:
        sdft_ref, refs = refs[0], refs[1:]
    if rope:
        tab_ref, refs = refs[0], refs[1:]
    if n_prev:
        prev_refs, refs = refs[:4], refs[4:]
    n_fourier = 1 if multi_seq else 2
    qm_ref, km_ref, vm_ref = refs[:3]
    fourier_refs = refs[3:3 + n_fourier]
    gq_ref, gk_ref, gv_ref = refs[3 + n_fourier:6 + n_fourier]
    cache_refs = refs[6 + n_fourier:]

    shift = mod_ref[:, 0:D_MODEL]
    scale = mod_ref[:, D_MODEL:2 * D_MODEL]
    lower = _lower_half(PROJ_SUB)

    for r in range(x_ref.shape[0] // PROJ_SUB):
        rows = slice(r * PROJ_SUB, (r + 1) * PROJ_SUB)
        seq_cols = (slice(None), slice(r * FNET_WIDTH, (r + 1) * FNET_WIDTH)) if multi_seq else (rows, slice(None))
        t_idx = (r,) if multi_seq else (slice(None), rows)

        hb = (_rms(x_ref[rows, :]) * g1_ref[...] * (1.0 + scale) + shift).astype(BF16)
        pa = _dot(hb, wa_ref[...])
        pc = _dot(hb, wc_ref[...])
        kr = _dot(hb, wkr_ref[...])

        if rope:
            cos_m, sa_m, sb_m = (tab_ref[rows, i * LANES:(i + 1) * LANES] for i in range(3))
            cos_g, sa_g, sb_g = (tab_ref[rows, i * LANES:(i + 1) * LANES] for i in range(3, 6))

        gqn = _group_rms(pc[:, _C_GQ:_C_GQ + GQA_WIDTH], mq_ref[...]) * gqh_ref[...]
        tiles = []
        for p in range(HEAD_PAIRS):
            gp = gqn[:, p * LANES:(p + 1) * LANES]
            if rope:
                gp = _rotate(gp, cos_g, sa_g, sb_g, GQA_HEAD_DIM // 2)
            tiles.append(gp * (LOG2_E * GQA_HEAD_DIM ** -0.5))
        gq_ref[rows, 0:LANES] = jnp.where(lower, tiles[0], tiles[1]).astype(BF16)
        gq_ref[rows, LANES:2 * LANES] = jnp.where(
            lower, pltpu.roll(tiles[0], HALF, 1), pltpu.roll(tiles[2], HALF, 1)).astype(BF16)
        gq_ref[rows, 2 * LANES:] = jnp.where(lower, tiles[1], tiles[2]).astype(BF16)
        gkn = _group_rms(pc[:, _C_GK:_C_GK + GQA_KV_WIDTH], mk_ref[...]) * gkh_ref[...]
        gkr = _rotate(gkn, cos_g, sa_g, sb_g, GQA_HEAD_DIM // 2) if rope else gkn
        gv_t = pc[:, _C_GV:_C_GV + GQA_KV_WIDTH].T
        gk_ref[rows, :] = gkr.astype(BF16)
        gv_ref[t_idx] = gv_t.astype(BF16)

        cqn = _rms(pa[:, :MLA_Q_RANK]) * gqa_ref[...]
        q = _dot(cqn.astype(BF16), wq_ref[...])
        for hd in range(MLA_HEADS):
            qh = q[:, hd * LANES:(hd + 1) * LANES]
            if rope:
                qh = _rotate(qh, cos_m, sa_m, sb_m, MLA_ROPE_DIM // 2)
            qm_ref[rows, hd * LANES:(hd + 1) * LANES] = (qh * (LOG2_E * MLA_QK_DIM ** -0.5)).astype(BF16)

        ckvn = _rms(pa[:, MLA_Q_RANK:]) * gkva_ref[...]
        cb = ckvn.astype(BF16)
        kn = _dot(cb, wk_ref[...])
        krr = _rotate(kr, cos_m, sa_m, sb_m, MLA_ROPE_DIM // 2) if rope else kr
        for hd in range(MLA_HEADS):
            km_ref[rows, hd * LANES:(hd + 1) * LANES] = (kn[:, hd * LANES:(hd + 1) * LANES] + krr).astype(BF16)
        vm_ref[t_idx] = _dot(cb, wv_ref[...]).T.astype(BF16)

        ucs = _dot(pc[:, _C_U:_C_U + FNET_WIDTH].astype(BF16), wdft_ref[...])
        uc = ucs[:, :FNET_WIDTH].astype(BF16)
        us = ucs[:, FNET_WIDTH:].astype(BF16)
        if multi_seq:
            fn = _dot(sdft_ref[0], uc) - _dot(sdft_ref[1], us)
            fourier_refs[0][seq_cols] = (fn * (PROJ_SUB * FNET_GROUP_DIM) ** -0.5).astype(BF16)
        else:
            fourier_refs[0][seq_cols] = uc
            fourier_refs[1][seq_cols] = us

        if emit_cache:
            new = (ckvn, kr.T[:MLA_ROPE_DIM, :], gkn.T, gv_t)
            for i, out_ref in enumerate(cache_refs):
                if n_prev:
                    out_ref[r, :n_prev] = prev_refs[i][r]
                out_ref[r, n_prev] = new[i]


def _proj_group(x, consts, *, seq, rope, prev_cache, emit_cache, tm, tile):
    n_tok = x.shape[0]
    n_seq = n_tok // seq
    multi_seq = tm > seq
    if multi_seq:
        assert seq == PROJ_SUB and not rope
        spt = tm // seq
        seq_major = pl.BlockSpec((seq, spt * FNET_WIDTH), lambda i: (0, tile(i)))
        transposed = lambda w: pl.BlockSpec((spt, w, seq), lambda i: (tile(i), 0, 0))
    else:
        assert not emit_cache
        tps = seq // tm
        seq_major = pl.BlockSpec((tm, FNET_WIDTH), lambda i: (tile(i) % tps, tile(i) // tps))
        transposed = lambda w: pl.BlockSpec((None, w, tm), lambda i: (tile(i) // tps, 0, tile(i) % tps))
    tok = lambda w: pl.BlockSpec((tm, w), lambda i: (tile(i), 0))
    extra_specs, extra_args = [], []
    if multi_seq:
        extra_specs.append(pl.BlockSpec(consts["seq_dft"].shape, lambda i: (0, 0, 0)))
        extra_args.append(consts["seq_dft"])
    if rope:
        extra_specs.append(pl.BlockSpec((tm, 6 * LANES), lambda i: (tile(i) % tps, 0)))
        extra_args.append(consts["rope"])
    n_fourier = 1 if multi_seq else 2
    out_specs = [tok(MLA_PAD_WIDTH), tok(MLA_PAD_WIDTH), transposed(MIX_TILE)] + [seq_major] * n_fourier + [
                 tok(GQA_WIDTH), tok(GQA_KV_WIDTH), transposed(GQA_KV_WIDTH)]
    out_shape = [jax.ShapeDtypeStruct((n_tok, MLA_PAD_WIDTH), BF16),
                 jax.ShapeDtypeStruct((n_tok, MLA_PAD_WIDTH), BF16),
                 jax.ShapeDtypeStruct((n_seq, MIX_TILE, seq), BF16)] + [
                 jax.ShapeDtypeStruct((seq, n_seq * FNET_WIDTH), BF16)] * n_fourier + [
                 jax.ShapeDtypeStruct((n_tok, GQA_WIDTH), BF16),
                 jax.ShapeDtypeStruct((n_tok, GQA_KV_WIDTH), BF16),
                 jax.ShapeDtypeStruct((n_seq, GQA_KV_WIDTH, seq), BF16)]
    n_prev = 0
    if emit_cache:
        n_prev = prev_cache[0].shape[1] if prev_cache is not None else 0
        layers = lambda n, tail: pl.BlockSpec((spt, n) + tail, lambda i: (tile(i), 0, 0, 0))
        tails = ((seq, MLA_KV_RANK), (MLA_ROPE_DIM, seq), (GQA_KV_WIDTH, seq), (GQA_KV_WIDTH, seq))
        if n_prev:
            extra_specs += [layers(n_prev, t) for t in tails]
            extra_args += list(prev_cache)
        out_specs += [layers(n_prev + 1, t) for t in tails]
        out_shape += [jax.ShapeDtypeStruct((n_seq, n_prev + 1) + t, F32) for t in tails]
    flags = dict(rope=rope, emit_cache=emit_cache, n_prev=n_prev, multi_seq=multi_seq)
    return tok(D_MODEL), extra_specs, extra_args, out_specs, out_shape, flags


N_PROJ_SHARED = len(_PROJ_WEIGHTS) + 3


def _proj_pair_kernel(*refs, n_first, n_extra, n_out, flags):
    x_refs, mod_ref, shared = refs[0:2], refs[2], refs[3:3 + N_PROJ_SHARED]
    pos = 3 + N_PROJ_SHARED
    extras = (refs[pos:pos + n_extra[0]], refs[pos + n_extra[0]:pos + sum(n_extra)])
    pos += sum(n_extra)
    outs = (refs[pos:pos + n_out[0]], refs[pos + n_out[0]:])
    step = pl.program_id(0)
    for g, in_this_group in enumerate((step < n_first, step >= n_first)):
        @pl.when(in_this_group)
        def _group(g=g):
            _proj_kernel(x_refs[g], mod_ref, *shared, *extras[g], *outs[g], **flags[g])


def _proj_pair(x1, x2, mod, wts, win_parts, layer, consts, *, seq1, seq2, prev_cache, tm=512):
    n1, n2 = x1.shape[0] // tm, x2.shape[0] // tm
    tiles_per_mod = n2 // (mod.shape[0] - 1)
    tile1 = lambda i: jnp.minimum(i, n1 - 1)
    tile2 = lambda i: jnp.maximum(i - n1, 0)
    g1 = _proj_group(x1, consts, seq=seq1, rope=False, prev_cache=prev_cache, emit_cache=True, tm=tm, tile=tile1)
    g2 = _proj_group(x2, consts, seq=seq2, rope=True, prev_cache=None, emit_cache=False, tm=tm, tile=tile2)
    const = lambda arr: pl.BlockSpec(arr.shape, lambda i: (0,) * arr.ndim)
    mod_spec = pl.BlockSpec((None, 1, 6 * D_MODEL),
                            lambda i: (jnp.where(i < n1, 0, 1 + tile2(i) // tiles_per_mod), 0, 0))
    in_specs, args = [g1[0], g2[0], mod_spec], [x1, x2, mod]
    for name in _PROJ_WEIGHTS:
        arr = win_parts.get(name)
        in_specs.append(const(arr) if arr is not None else _layer_spec(wts[name], layer))
        args.append(arr if arr is not None else wts[name])
    for name in ("wdft", "mean_q", "mean_k"):
        in_specs.append(const(consts[name]))
        args.append(consts[name])
    outs = pl.pallas_call(
        functools.partial(_proj_pair_kernel, n_first=n1, n_extra=(len(g1[1]), len(g2[1])),
                          n_out=(len(g1[3]), len(g2[3])), flags=(g1[5], g2[5])),
        grid=(n1 + n2,),
        in_specs=in_specs + g1[1] + g2[1],
        out_specs=g1[3] + g2[3],
        out_shape=g1[4] + g2[4],
        compiler_params=_params(1, "proj"),
        name="proj",
    )(*args, *g1[2], *g2[2])
    return outs[:len(g1[3])], outs[len(g1[3]):]


def _ctx_kernel(ckv_ref, kr_ref, gk_ref, gv_ref, wk_ref, wv_ref, place_ref, win_ref,
                km_ref, vm_ref, gko_ref, gvo_ref, wa_ref, wc_ref, wkr_ref):
    wt = win_ref[...]
    wa_ref[...] = wt[:_IN_A, :].T.astype(BF16)
    wc_ref[...] = wt[_IN_C0:, :].T.astype(BF16)
    kr_rows = jnp.concatenate([wt[_IN_A:_IN_C0, :], jnp.zeros((LANES - MLA_ROPE_DIM, wt.shape[1]), F32)], axis=0)
    wkr_ref[...] = kr_rows.T.astype(BF16)

    cb = ckv_ref[...].astype(BF16)
    kn = _dot(cb, wk_ref[...])
    kr = _dot(kr_ref[...].astype(BF16), place_ref[...])
    for hd in range(MLA_HEADS):
        km_ref[:, hd * LANES:(hd + 1) * LANES] = (kn[:, hd * LANES:(hd + 1) * LANES] + kr).astype(BF16)
    vm_ref[...] = _dot(cb, wv_ref[...]).T.astype(BF16)
    gko_ref[...] = gk_ref[...].astype(BF16)
    gvo_ref[...] = gv_ref[...].T.astype(BF16)


def _ctx_prep(layer, cache_ckv, cache_krope, cache_gk, cache_gv, w_in_t, wts, consts):
    nb, _, past, _ = cache_ckv.shape
    rows = D_MODEL // nb
    cache = lambda w: pl.BlockSpec((None, None, past, w), lambda b: (b, layer, 0, 0))
    out = lambda w: pl.BlockSpec((None, past, w), lambda b: (b, 0, 0))
    out_t = lambda w: pl.BlockSpec((None, w, past), lambda b: (b, 0, 0))
    part = lambda w: pl.BlockSpec((rows, w), lambda b: (b, 0))
    part_widths = (_IN_A, _IN_C, LANES)
    return pl.pallas_call(
        _ctx_kernel,
        grid=(nb,),
        in_specs=[cache(MLA_KV_RANK), cache(MLA_ROPE_DIM), cache(GQA_KV_WIDTH), cache(GQA_KV_WIDTH),
                  _layer_spec(wts["wk"], layer), _layer_spec(wts["wv"], layer),
                  pl.BlockSpec((MLA_ROPE_DIM, LANES), lambda b: (0, 0)),
                  pl.BlockSpec((None, w_in_t.shape[1], rows), lambda b: (layer, 0, b))],
        out_specs=[out(MLA_PAD_WIDTH), out_t(MIX_TILE), out(GQA_KV_WIDTH), out_t(GQA_KV_WIDTH)]
                  + [part(w) for w in part_widths],
        out_shape=[jax.ShapeDtypeStruct((nb, past, MLA_PAD_WIDTH), BF16),
                   jax.ShapeDtypeStruct((nb, MIX_TILE, past), BF16),
                   jax.ShapeDtypeStruct((nb, past, GQA_KV_WIDTH), BF16),
                   jax.ShapeDtypeStruct((nb, GQA_KV_WIDTH, past), BF16)]
                  + [jax.ShapeDtypeStruct((D_MODEL, w), BF16) for w in part_widths],
        compiler_params=_params(1, "ctx_prep"),
        name="ctx_prep",
    )(cache_ckv, cache_krope, cache_gk.reshape(nb, DEPTH, past, GQA_KV_WIDTH),
      cache_gv.reshape(nb, DEPTH, past, GQA_KV_WIDTH), wts["wk"], wts["wv"], consts["place"], w_in_t)


def _attn_kernel(*refs, n_seg, ahead, n_cast):
    n_in = 1 + 2 * n_seg
    outs_at = 2 * n_in + n_cast
    families = []
    for f, mla in enumerate((True, False)):
        ins = refs[f * n_in:(f + 1) * n_in]
        families.append((mla, ins[0], ins[1:1 + n_seg], ins[1 + n_seg:], refs[outs_at + f]))
    for c in range(n_cast):
        refs[outs_at + 2 + c][...] = refs[2 * n_in + c][...].astype(BF16)
    n_seqs, tq = refs[0].shape[:2]
    lower = _lower_half(tq)
    top = lax.broadcasted_iota(jnp.int32, (LANES, tq), 0) < HALF

    def scores_t(f, b, p, hd):
        mla, q_ref, k_refs, _, _ = families[f]
        if mla:
            cols = slice((2 * p + hd) * LANES, (2 * p + hd + 1) * LANES)
            qh = q_ref[b, :, cols]
            keys = [k_ref[b, :, cols] for k_ref in k_refs]
        else:
            q = q_ref[b, :, p * LANES:(p + 1) * LANES]
            qh = jnp.where(lower if hd == 0 else ~lower, q, jnp.zeros_like(q))
            keys = [k_ref[b] for k_ref in k_refs]
        return [lax.dot_general(k, qh, (((1,), (1,)), ((), ())), preferred_element_type=F32) for k in keys]

    def softmax_t(ss):
        m = functools.reduce(jnp.maximum, [jnp.max(s, axis=0, keepdims=True) for s in ss])
        es = [jnp.exp2(s - m) for s in ss]
        denom = functools.reduce(jnp.add, [jnp.sum(e, axis=0, keepdims=True) for e in es])
        return [e.astype(BF16) for e in es], denom

    def weighted_values_t(f, b, p, es, denom):
        mla, _, _, vt_refs, _ = families[f]
        rows = slice(p * LANES, (p + 1) * LANES) if mla else slice(None)
        acc = functools.reduce(jnp.add, [_dot(vt_ref[b, rows, :], e) for e, vt_ref in zip(es, vt_refs)])
        return acc * (1.0 / denom)

    work = [(f, b, p, hd) for f in range(2) for b in range(n_seqs) for p in range(HEAD_PAIRS) for hd in range(2)]
    pending = [scores_t(*w) for w in work[:ahead]]
    outs = {}
    for i, (f, b, p, hd) in enumerate(work):
        es, denom = softmax_t(pending.pop(0))
        if i + ahead < len(work):
            pending.append(scores_t(*work[i + ahead]))
        outs[f, b, p, hd] = weighted_values_t(f, b, p, es, denom)
        if (p, hd) != (HEAD_PAIRS - 1, 1):
            continue
        mla, o_ref = families[f][0], families[f][4]
        tiles = [jnp.where(top, outs[f, b, t, 0], outs[f, b, t, 1]).T for t in range(HEAD_PAIRS)]
        if not mla:
            t1 = pltpu.roll(tiles[1], HALF, 1)
            tiles = [jnp.where(lower, tiles[0], t1), jnp.where(lower, tiles[2], tiles[0]),
                     jnp.where(lower, t1, tiles[2])]
        for t in range(HEAD_PAIRS):
            o_ref[b, :, t * LANES:(t + 1) * LANES] = tiles[t].astype(o_ref.dtype)


def _attention(mla_q, mla_kv, gqa_q, gqa_kv, *, tq, bb=1, ahead=2, cast=(), layer=0):
    n_batch, sq = mla_q.shape[:2]
    n_seg = len(mla_kv)
    n_q = sq // tq
    steps = (n_batch // bb) * n_q
    whole = lambda a: pl.BlockSpec((bb,) + a.shape[1:], lambda b, i: (b, 0, 0))
    in_specs, args = [], []
    for q, kv in ((mla_q, mla_kv), (gqa_q, gqa_kv)):
        in_specs += [pl.BlockSpec((bb, tq, q.shape[2]), lambda b, i: (b, i, 0))]
        in_specs += [whole(k) for k, _ in kv] + [whole(v) for _, v in kv]
        args += [q] + [k for k, _ in kv] + [v for _, v in kv]
    out_spec = pl.BlockSpec((bb, tq, MIX_TILE), lambda b, i: (b, i, 0))
    out_specs = [out_spec, out_spec]
    out_shape = [jax.ShapeDtypeStruct((n_batch, sq, MIX_TILE), BF16)] * 2
    for w in cast:
        rows, cols = w.shape[1] // steps, w.shape[2]
        in_specs.append(pl.BlockSpec((None, rows, cols), lambda b, i: (layer, b * n_q + i, 0)))
        out_specs.append(pl.BlockSpec((rows, cols), lambda b, i: (b * n_q + i, 0)))
        out_shape.append(jax.ShapeDtypeStruct(w.shape[1:], BF16))
        args.append(w)
    return pl.pallas_call(
        functools.partial(_attn_kernel, n_seg=n_seg, ahead=ahead, n_cast=len(cast)),
        grid=(n_batch // bb, n_q),
        in_specs=in_specs,
        out_specs=out_specs,
        out_shape=out_shape,
        compiler_params=_params(2, "attn_latent" if n_seg > 1 else "attn_prompt"),
        name=f"attn_s{n_seg}",
    )(*args)


def _fnet_kernel(ca_ref, sa_ref, cb_ref, sb_ref, uc_ref, us_ref, o_ref, tc_ref, ts_ref, *, scale):
    @pl.when(pl.program_id(1) == 0)
    def _build_twiddles():
        cb, sb = cb_ref[...], sb_ref[...]
        for a in range(ca_ref.shape[0]):
            ca, sa = ca_ref[a:a + 1, :], sa_ref[a:a + 1, :]
            rows = slice(a * DFT_SPLIT, (a + 1) * DFT_SPLIT)
            tc_ref[rows, :] = (ca * cb - sa * sb).astype(BF16)
            ts_ref[rows, :] = (sa * cb + ca * sb).astype(BF16)

    acc = _dot(tc_ref[...], uc_ref[...]) - _dot(ts_ref[...], us_ref[...])
    o_ref[...] = (acc * scale).astype(o_ref.dtype)


def _fnet(tables, uc, us, tm, tn):
    seq, width = uc.shape
    scale = float((seq * FNET_GROUP_DIM) ** -0.5)
    n_a = tm // DFT_SPLIT
    part_a = pl.BlockSpec((n_a, seq), lambda i, j: (i, 0))
    part_b = pl.BlockSpec((DFT_SPLIT, seq), lambda i, j: (0, 0))
    data = pl.BlockSpec((seq, tn), lambda i, j: (0, j))
    return pl.pallas_call(
        functools.partial(_fnet_kernel, scale=scale),
        grid=(seq // tm, width // tn),
        in_specs=[part_a, part_a, part_b, part_b, data, data],
        out_specs=pl.BlockSpec((tm, tn), lambda i, j: (i, j)),
        out_shape=jax.ShapeDtypeStruct((seq, width), BF16),
        scratch_shapes=[pltpu.VMEM((tm, seq), BF16), pltpu.VMEM((tm, seq), BF16)],
        compiler_params=_params(2, "fnet"),
        name="fnet",
    )(*tables, uc, us)


def _post_body(x_ref, mla_ref, fn_ref, gqa_ref, o_ref, mod_ref, g2_ref, wo_ref, wg_ref, wu_ref, wd_ref, gf_ref, final):
    fn = fn_ref[...]
    if fn.shape[1] > FNET_WIDTH:
        fn = jnp.concatenate([fn[:, j:j + FNET_WIDTH] for j in range(0, fn.shape[1], FNET_WIDTH)], axis=0)
    mix = jnp.concatenate([mla_ref[...], fn, gqa_ref[...]], axis=-1)
    gate1 = mod_ref[:, 2 * D_MODEL:3 * D_MODEL]
    shift2 = mod_ref[:, 3 * D_MODEL:4 * D_MODEL]
    scale2 = mod_ref[:, 4 * D_MODEL:5 * D_MODEL]
    gate2 = mod_ref[:, 5 * D_MODEL:6 * D_MODEL]
    x = x_ref[...] + gate1 * _dot(mix, wo_ref[...])
    h = (_rms(x) * g2_ref[...] * (1.0 + scale2) + shift2).astype(BF16)
    g = _dot(h, wg_ref[...])
    u = _dot(h, wu_ref[...])
    a = (g * jax.nn.sigmoid(g) * u).astype(BF16)
    x = x + gate2 * _dot(a, wd_ref[...])
    if final:
        x = _rms(x) * gf_ref[...]
    o_ref[...] = x


def _post_kernel(*refs, final, n_first, with_next_mod):
    first, second, shared = refs[0:4], refs[4:8], refs[8:15]
    n_in = 18 if with_next_mod else 15
    o_first, o_second = refs[n_in:n_in + 2]
    step = pl.program_id(0)

    def run(group, o_ref):
        if with_next_mod:
            _ada_kernel(*refs[15:18], refs[n_in + 2])
        _post_body(*group, o_ref, *shared, final)

    @pl.when(step < n_first)
    def _first_group():
        run(first, o_first)

    @pl.when(step >= n_first)
    def _second_group():
        run(second, o_second)


def _post(groups, mod, g2, ffn, layer, g_final, *, final, tm, next_mod=None):
    (x1, _, _, _, seq1), (x2, _, _, _, seq2) = groups
    n1, n2 = x1.shape[0] // tm, x2.shape[0] // tm
    tiles_per_mod = n2 // (mod.shape[0] - 1)
    tile1 = lambda i: jnp.minimum(i, n1 - 1)
    tile2 = lambda i: jnp.maximum(i - n1, 0)
    resident = lambda arr: pl.BlockSpec(arr.shape, lambda i: (0, 0), pipeline_mode=pl.Buffered(1))

    def group_specs(seq, tile):
        tok = lambda w: pl.BlockSpec((tm, w), lambda i: (tile(i), 0))
        if tm <= seq:
            tps = seq // tm
            fnet_spec = pl.BlockSpec((tm, FNET_WIDTH), lambda i: (tile(i) % tps, tile(i) // tps))
        else:
            fnet_spec = pl.BlockSpec((seq, (tm // seq) * FNET_WIDTH), lambda i: (0, tile(i)))
        return [tok(D_MODEL), tok(MIX_TILE), fnet_spec, tok(MIX_TILE)], tok(D_MODEL)

    in1, out1 = group_specs(seq1, tile1)
    in2, out2 = group_specs(seq2, tile2)
    mod_spec = pl.BlockSpec((None, 1, 6 * D_MODEL),
                            lambda i: (jnp.where(i < n1, 0, 1 + tile2(i) // tiles_per_mod), 0, 0))
    in_specs = (in1 + in2 + [mod_spec, _layer_spec(g2, layer)] + [resident(w) for w in ffn]
                + [pl.BlockSpec((1, D_MODEL), lambda i: (0, 0))])
    args = [*groups[0][:4], *groups[1][:4], mod, g2, *ffn, g_final]
    out_specs = [out1, out2]
    out_shape = [jax.ShapeDtypeStruct(x1.shape, F32), jax.ShapeDtypeStruct(x2.shape, F32)]
    if next_mod is not None:
        cond_t, w_ada, b_ada = next_mod
        ada_in, ada_out, ada_shape = _ada_specs(cond_t, w_ada, layer + 1, w_ada.shape[2] // (n1 + n2))
        in_specs += ada_in
        args += [cond_t, w_ada, b_ada]
        out_specs.append(ada_out)
        out_shape.append(ada_shape)
    return pl.pallas_call(
        functools.partial(_post_kernel, final=final, n_first=n1, with_next_mod=next_mod is not None),
        grid=(n1 + n2,),
        in_specs=in_specs,
        out_specs=out_specs,
        out_shape=out_shape,
        compiler_params=_params(1, "post"),
        name="post_final" if final else "post",
    )(*args)


def _prep_weights(g_norm1, g_norm2, g_q_a, w_q_up, g_kv_a, w_kv_up, g_q_head, g_k_head):
    row = lambda g: g.reshape(DEPTH, 1, -1)
    wq = w_q_up.reshape(DEPTH, MLA_Q_RANK, MLA_HEADS, MLA_QK_DIM)
    wq = jnp.concatenate([wq[..., MLA_NOPE_DIM:], wq[..., :MLA_NOPE_DIM]], axis=-1)
    wq = jnp.pad(wq, ((0, 0), (0, 0), (0, 0), (0, LANES - MLA_QK_DIM)))
    wkv = w_kv_up.reshape(DEPTH, MLA_KV_RANK, MLA_HEADS, MLA_NOPE_DIM + MLA_V_DIM)
    wk = jnp.pad(wkv[..., :MLA_NOPE_DIM], ((0, 0), (0, 0), (0, 0), (MLA_ROPE_DIM, LANES - MLA_QK_DIM)))
    wv = wkv[..., MLA_NOPE_DIM:]
    return {
        "g1": row(g_norm1), "g2": row(g_norm2), "g_q_a": row(g_q_a), "g_kv_a": row(g_kv_a),
        "g_q_head": row(jnp.tile(g_q_head, (1, GQA_HEADS))), "g_k_head": row(jnp.tile(g_k_head, (1, GQA_KV_HEADS))),
        "wq": wq.reshape(DEPTH, MLA_Q_RANK, MLA_PAD_WIDTH).astype(BF16),
        "wk": wk.reshape(DEPTH, MLA_KV_RANK, MLA_PAD_WIDTH).astype(BF16),
        "wv": wv.reshape(DEPTH, MLA_KV_RANK, MIX_TILE).astype(BF16),
    }


def kernel(x_prompt, x_sample, cache_mla_ckv, cache_mla_krope, cache_gqa_k, cache_gqa_v, c, c_ctx, w_ada, b_ada,
           g_norm1, g_norm2, w_in, g_q_a, w_q_up, g_kv_a, w_kv_up, g_q_head, g_k_head, w_out,
           w_ffn_gate, w_ffn_up, w_ffn_down, g_final):
    n_pb, p_seq, _ = x_prompt.shape
    n_sb, s_seq, _ = x_sample.shape
    consts = {"wdft": _channel_dft(), "mean_q": _group_mean_matrix(GQA_HEADS),
              "mean_k": _group_mean_matrix(GQA_KV_HEADS), "place": _rope_place_matrix(),
              "rope": _rope_tables(s_seq), "seq_dft": _seq_dft_matrix(p_seq)}
    dft_s = _seq_dft_tables(s_seq)
    gf = g_final.reshape(1, D_MODEL)
    wts = _prep_weights(g_norm1, g_norm2, g_q_a, w_q_up, g_kv_a, w_kv_up, g_q_head, g_k_head)
    w_in_t = jnp.swapaxes(w_in, 1, 2)

    cond_t = jnp.concatenate([c_ctx[None, :], c], axis=0).T
    b_ada3 = b_ada.reshape(DEPTH, 1, 6 * D_MODEL)
    mod = _ada(cond_t, w_ada, b_ada3, 0)

    xp = x_prompt.reshape(n_pb * p_seq, D_MODEL)
    xs = x_sample.reshape(n_sb * s_seq, D_MODEL)
    b3 = lambda a, nb: a.reshape(nb, a.shape[0] // nb, a.shape[1])
    caches = None
    for l in range(DEPTH):
        final = l == DEPTH - 1
        mod3 = mod.reshape(1 + n_sb, 1, 6 * D_MODEL)
        km_c, vm_c, gk_c, gv_c, w_a, w_c, w_kr = _ctx_prep(
            l, cache_mla_ckv, cache_mla_krope, cache_gqa_k, cache_gqa_v, w_in_t, wts, consts)
        win_parts = {"w_a": w_a, "w_c": w_c, "w_kr": w_kr}
        ctx_outs, lat_outs = _proj_pair(xp, xs, mod3, wts, win_parts, l, consts, seq1=p_seq, seq2=s_seq,
                                        prev_cache=caches)

        qm, km, vm, uc_s, us_s, gq, gk, gv = lat_outs
        mla_s, gqa_s, *ffn = _attention(b3(qm, n_sb), [(km_c, vm_c), (b3(km, n_sb), vm)],
                                        b3(gq, n_sb), [(gk_c, gv_c), (b3(gk, n_sb), gv)], tq=512,
                                        cast=(w_out, w_ffn_gate, w_ffn_up, w_ffn_down), layer=l)

        qm, km, vm, fn_p, gq, gk, gv, *caches = ctx_outs
        mla_o, gqa_o = _attention(b3(qm, n_pb), [(b3(km, n_pb), vm)], b3(gq, n_pb), [(b3(gk, n_pb), gv)],
                                  tq=p_seq, bb=4, ahead=8)
        fn_s = _fnet(dft_s, uc_s, us_s, tm=512, tn=n_sb * FNET_WIDTH)
        flat = lambda a: a.reshape(-1, MIX_TILE)
        xp, xs, *mod_next = _post(
            [(xp, flat(mla_o), fn_p, flat(gqa_o), p_seq), (xs, flat(mla_s), fn_s, flat(gqa_s), s_seq)],
            mod.reshape(1 + n_sb, 1, 6 * D_MODEL), wts["g2"], ffn, l, gf, final=final, tm=POST_TILE,
            next_mod=None if final else (cond_t, w_ada, b_ada3))
        if mod_next:
            mod = mod_next[0]

    ckv_new, krope_t, gk_t, gv_t = caches
    heads = lambda a: jnp.swapaxes(a, 2, 3).reshape(n_pb, DEPTH, p_seq, GQA_KV_HEADS, GQA_HEAD_DIM)
    return (xp.reshape(n_pb, p_seq, D_MODEL), xs.reshape(n_sb, s_seq, D_MODEL),
            ckv_new, jnp.swapaxes(krope_t, 2, 3), heads(gk_t), heads(gv_t))
```

```python
import functools

import numpy as np
import jax
import jax.numpy as jnp
from jax import lax
from jax.experimental import pallas as pl
from jax.experimental.pallas import tpu as pltpu

D_MODEL = 1024
DEPTH = 2
GRID_W = 64
MLA_HEADS = 6
MLA_Q_RANK = 384
MLA_KV_RANK = 256
MLA_NOPE_DIM = 64
MLA_ROPE_DIM = 32
MLA_V_DIM = 64
MLA_QK_DIM = MLA_NOPE_DIM + MLA_ROPE_DIM
FNET_GROUPS = 4
FNET_GROUP_DIM = 64
FNET_WIDTH = FNET_GROUPS * FNET_GROUP_DIM
GQA_HEADS = 6
GQA_KV_HEADS = 2
GQA_HEAD_DIM = 64
GQA_GROUP = GQA_HEADS // GQA_KV_HEADS
GQA_WIDTH = GQA_HEADS * GQA_HEAD_DIM
GQA_KV_WIDTH = GQA_KV_HEADS * GQA_HEAD_DIM
D_FF = 2816
ROPE_THETA = 10000.0
EPS = 1e-6
LOG2_E = 1.4426950408889634

LANES = 128
HALF = LANES // 2
HEAD_PAIRS = MLA_HEADS // 2
MLA_PAD_WIDTH = MLA_HEADS * LANES
MIX_TILE = HEAD_PAIRS * LANES
MIB = 1024 * 1024
VMEM_LIMIT_MIB = 52
POST_TILE = 512
PROJ_TILE = 512
PROJ_SUB = 256
LATENT_Q_TILE = 512
CONTEXT_SEQS_PER_STEP = 4
QK_AHEAD_LATENT = 2
QK_AHEAD_CONTEXT = 8
FNET_ROW_TILE = 512
ADA_COL_TILE = 1024


def _params(n_grid_dims):
    return pltpu.CompilerParams(dimension_semantics=("arbitrary",) * n_grid_dims,
                                vmem_limit_bytes=VMEM_LIMIT_MIB * MIB)


_IN_A = MLA_Q_RANK + MLA_KV_RANK
_IN_C0 = _IN_A + MLA_ROPE_DIM
_IN_C = FNET_WIDTH + GQA_WIDTH + 2 * GQA_KV_WIDTH
_C_U, _C_GQ, _C_GK, _C_GV = 0, FNET_WIDTH, FNET_WIDTH + GQA_WIDTH, FNET_WIDTH + GQA_WIDTH + GQA_KV_WIDTH

BF16 = jnp.bfloat16
F32 = jnp.float32


def _rope_tables(n_tokens):
    t = np.arange(n_tokens)
    row = (t // GRID_W).astype(np.float64)
    col = (t % GRID_W).astype(np.float64)

    def angles(rot_dim):
        n_axis = rot_dim // 4
        inv = ROPE_THETA ** (-np.arange(n_axis, dtype=np.float64) / n_axis)
        return np.concatenate([row[:, None] * inv, col[:, None] * inv], axis=-1)

    def tables(ang, lane_to_pair, is_first, is_second):
        cos = np.where((is_first | is_second)[None, :], np.cos(ang)[:, lane_to_pair], 1.0)
        sin = np.sin(ang)[:, lane_to_pair]
        sin_a = np.where(is_second[None, :], sin, 0.0)
        sin_b = np.where(is_first[None, :], -sin, 0.0)
        return [cos, sin_a, sin_b]

    lane = np.arange(LANES)
    half_m = MLA_ROPE_DIM // 2
    in_rope = lane < MLA_ROPE_DIM
    first_m = lane < half_m
    second_m = in_rope & ~first_m
    pair_m = lane % half_m
    half_g = GQA_HEAD_DIM // 2
    first_g = (lane % GQA_HEAD_DIM) < half_g
    pair_g = lane % half_g
    tabs = (tables(angles(MLA_ROPE_DIM), pair_m, first_m, second_m)
            + tables(angles(GQA_HEAD_DIM), pair_g, first_g, ~first_g))
    return jnp.asarray(np.concatenate(tabs, axis=-1), dtype=F32)


def _channel_dft():
    c = np.arange(FNET_GROUP_DIM)
    ang = 2.0 * np.pi * np.outer(c, c) / FNET_GROUP_DIM
    eye = np.eye(FNET_GROUPS)
    table = np.concatenate([np.kron(eye, np.cos(ang)), np.kron(eye, np.sin(ang))], axis=1)
    return jnp.asarray(table, dtype=F32).astype(BF16)


def _group_mean_matrix(n_heads):
    return jnp.asarray(np.kron(np.eye(n_heads), np.full((GQA_HEAD_DIM, GQA_HEAD_DIM), 1.0 / GQA_HEAD_DIM)), dtype=BF16)


def _rope_place_matrix():
    return jnp.asarray(np.eye(MLA_ROPE_DIM, LANES), dtype=BF16)


DFT_SPLIT = 32


def _seq_dft_matrix(seq):
    s = np.arange(seq)
    ang = 2.0 * np.pi * ((s[:, None] * s[None, :]) % seq) / seq
    return jnp.asarray(np.stack([np.cos(ang), np.sin(ang)]), dtype=F32).astype(BF16)


def _seq_dft_tables(seq):
    s = np.arange(seq)
    n_a = seq // DFT_SPLIT
    ang_a = 2.0 * np.pi * ((np.arange(n_a)[:, None] * s[None, :]) % n_a) / n_a
    ang_b = 2.0 * np.pi * ((np.arange(DFT_SPLIT)[:, None] * s[None, :]) % seq) / seq
    return tuple(jnp.asarray(t, dtype=F32) for t in (np.cos(ang_a), np.sin(ang_a), np.cos(ang_b), np.sin(ang_b)))


def _rms(x):
    return x * lax.rsqrt(jnp.mean(x * x, axis=-1, keepdims=True) + EPS)


def _dot(a, b):
    return jnp.dot(a, b, preferred_element_type=F32)


def _group_rms(x, mean_mat):
    ms = _dot((x * x).astype(BF16), mean_mat)
    return x * lax.rsqrt(ms + EPS)


def _rotate(x, cos, sin_a, sin_b, half):
    return x * cos + pltpu.roll(x, half, 1) * sin_a + pltpu.roll(x, LANES - half, 1) * sin_b


def _lower_half(rows):
    return lax.broadcasted_iota(jnp.int32, (rows, LANES), 1) < HALF


def _layer_spec(arr, layer):
    return pl.BlockSpec((None,) + arr.shape[1:], lambda i: (layer, 0, 0))


def _ada_kernel(ct_ref, w_ref, b_ref, o_ref):
    s = ct_ref[...]
    s = s * jax.nn.sigmoid(s)
    w = w_ref[...]
    for m in range(o_ref.shape[0]):
        o_ref[m:m + 1, :] = jnp.sum(w * s[:, m:m + 1], axis=0, keepdims=True) + b_ref[...]


def _ada_specs(cond_t, w_ada, layer, tn):
    n_cond = cond_t.shape[1]
    in_specs = [pl.BlockSpec((D_MODEL, n_cond), lambda j: (0, 0)),
                pl.BlockSpec((None, D_MODEL, tn), lambda j: (layer, 0, j)),
                pl.BlockSpec((None, 1, tn), lambda j: (layer, 0, j))]
    out_shape = jax.ShapeDtypeStruct((n_cond, w_ada.shape[2]), F32)
    return in_specs, pl.BlockSpec((n_cond, tn), lambda j: (0, j)), out_shape


def _ada(cond_t, w_ada, b_ada, layer, tn=ADA_COL_TILE):
    in_specs, out_spec, out_shape = _ada_specs(cond_t, w_ada, layer, tn)
    return pl.pallas_call(
        _ada_kernel,
        grid=(w_ada.shape[2] // tn,),
        in_specs=in_specs,
        out_specs=out_spec,
        out_shape=out_shape,
        compiler_params=_params(1),
        name="ada",
    )(cond_t, w_ada, b_ada)


_PROJ_WEIGHTS = ("g1", "w_a", "w_c", "w_kr", "g_q_a", "wq", "g_kv_a", "wk", "wv", "g_q_head", "g_k_head")


def _proj_kernel(*refs, rope, emit_cache, n_prev, multi_seq):
    (x_ref, mod_ref, g1_ref, wa_ref, wc_ref, wkr_ref, gqa_ref, wq_ref, gkva_ref, wk_ref, wv_ref,
     gqh_ref, gkh_ref, wdft_ref, mq_ref, mk_ref) = refs[:16]
    refs = refs[16:]
    if multi_seq:
        sdft_ref, refs = refs[0], refs[1:]
    if rope:
        tab_ref, refs = refs[0], refs[1:]
    if n_prev:
        prev_refs, refs = refs[:4], refs[4:]
    n_fourier = 1 if multi_seq else 2
    qm_ref, km_ref, vm_ref = refs[:3]
    fourier_refs = refs[3:3 + n_fourier]
    gq_ref, gk_ref, gv_ref = refs[3 + n_fourier:6 + n_fourier]
    cache_refs = refs[6 + n_fourier:]

    shift = mod_ref[:, 0:D_MODEL]
    scale = mod_ref[:, D_MODEL:2 * D_MODEL]
    lower = _lower_half(PROJ_SUB)

    for r in range(x_ref.shape[0] // PROJ_SUB):
        rows = slice(r * PROJ_SUB, (r + 1) * PROJ_SUB)
        seq_cols = (slice(None), slice(r * FNET_WIDTH, (r + 1) * FNET_WIDTH)) if multi_seq else (rows, slice(None))
        t_idx = (r,) if multi_seq else (slice(None), rows)

        hb = (_rms(x_ref[rows, :]) * g1_ref[...] * (1.0 + scale) + shift).astype(BF16)
        pa = _dot(hb, wa_ref[...])
        pc = _dot(hb, wc_ref[...])
        kr = _dot(hb, wkr_ref[...])

        if rope:
            cos_m, sa_m, sb_m = (tab_ref[rows, i * LANES:(i + 1) * LANES] for i in range(3))
            cos_g, sa_g, sb_g = (tab_ref[rows, i * LANES:(i + 1) * LANES] for i in range(3, 6))

        gqn = _group_rms(pc[:, _C_GQ:_C_GQ + GQA_WIDTH], mq_ref[...]) * gqh_ref[...]
        tiles = []
        for p in range(HEAD_PAIRS):
            gp = gqn[:, p * LANES:(p + 1) * LANES]
            if rope:
                gp = _rotate(gp, cos_g, sa_g, sb_g, GQA_HEAD_DIM // 2)
            tiles.append(gp * (LOG2_E * GQA_HEAD_DIM ** -0.5))
        gq_ref[rows, 0:LANES] = jnp.where(lower, tiles[0], tiles[1]).astype(BF16)
        gq_ref[rows, LANES:2 * LANES] = jnp.where(
            lower, pltpu.roll(tiles[0], HALF, 1), pltpu.roll(tiles[2], HALF, 1)).astype(BF16)
        gq_ref[rows, 2 * LANES:] = jnp.where(lower, tiles[1], tiles[2]).astype(BF16)
        gkn = _group_rms(pc[:, _C_GK:_C_GK + GQA_KV_WIDTH], mk_ref[...]) * gkh_ref[...]
        gkr = _rotate(gkn, cos_g, sa_g, sb_g, GQA_HEAD_DIM // 2) if rope else gkn
        gv_t = pc[:, _C_GV:_C_GV + GQA_KV_WIDTH].T
        gk_ref[rows, :] = gkr.astype(BF16)
        gv_ref[t_idx] = gv_t.astype(BF16)

        cqn = _rms(pa[:, :MLA_Q_RANK]) * gqa_ref[...]
        q = _dot(cqn.astype(BF16), wq_ref[...])
        for hd in range(MLA_HEADS):
            qh = q[:, hd * LANES:(hd + 1) * LANES]
            if rope:
                qh = _rotate(qh, cos_m, sa_m, sb_m, MLA_ROPE_DIM // 2)
            qm_ref[rows, hd * LANES:(hd + 1) * LANES] = (qh * (LOG2_E * MLA_QK_DIM ** -0.5)).astype(BF16)

        ckvn = _rms(pa[:, MLA_Q_RANK:]) * gkva_ref[...]
        cb = ckvn.astype(BF16)
        kn = _dot(cb, wk_ref[...])
        krr = _rotate(kr, cos_m, sa_m, sb_m, MLA_ROPE_DIM // 2) if rope else kr
        for hd in range(MLA_HEADS):
            km_ref[rows, hd * LANES:(hd + 1) * LANES] = (kn[:, hd * LANES:(hd + 1) * LANES] + krr).astype(BF16)
        vm_ref[t_idx] = _dot(cb, wv_ref[...]).T.astype(BF16)

        ucs = _dot(pc[:, _C_U:_C_U + FNET_WIDTH].astype(BF16), wdft_ref[...])
        uc = ucs[:, :FNET_WIDTH].astype(BF16)
        us = ucs[:, FNET_WIDTH:].astype(BF16)
        if multi_seq:
            fn = _dot(sdft_ref[0], uc) - _dot(sdft_ref[1], us)
            fourier_refs[0][seq_cols] = (fn * (PROJ_SUB * FNET_GROUP_DIM) ** -0.5).astype(BF16)
        else:
            fourier_refs[0][seq_cols] = uc
            fourier_refs[1][seq_cols] = us

        if emit_cache:
            new = (ckvn, kr.T[:MLA_ROPE_DIM, :], gkn.T, gv_t)
            for i, out_ref in enumerate(cache_refs):
                if n_prev:
                    out_ref[r, :n_prev] = prev_refs[i][r]
                out_ref[r, n_prev] = new[i]


def _proj(x, mod, wts, win_parts, layer, consts, *, seq, rope, prev_cache=None, emit_cache=False, tm=PROJ_TILE):
    n_tok = x.shape[0]
    n_tiles = n_tok // tm
    n_seq = n_tok // seq
    tiles_per_mod = n_tiles // mod.shape[0]
    multi_seq = tm > seq
    if multi_seq:
        assert seq == PROJ_SUB and not rope
        spt = tm // seq
        seq_major = pl.BlockSpec((seq, spt * FNET_WIDTH), lambda i: (0, i))
        transposed = lambda w: pl.BlockSpec((spt, w, seq), lambda i: (i, 0, 0))
    else:
        assert not emit_cache
        tps = seq // tm
        seq_major = pl.BlockSpec((tm, FNET_WIDTH), lambda i: (i % tps, i // tps))
        transposed = lambda w: pl.BlockSpec((None, w, tm), lambda i: (i // tps, 0, i % tps))
    const = lambda arr: pl.BlockSpec(arr.shape, lambda i: (0,) * arr.ndim)
    tok = lambda w: pl.BlockSpec((tm, w), lambda i: (i, 0))
    in_specs = [tok(D_MODEL), pl.BlockSpec((None, 1, 6 * D_MODEL), lambda i: (i // tiles_per_mod, 0, 0))]
    args = [x, mod]
    for name in _PROJ_WEIGHTS:
        arr = win_parts.get(name)
        in_specs.append(const(arr) if arr is not None else _layer_spec(wts[name], layer))
        args.append(arr if arr is not None else wts[name])
    for name in ("wdft", "mean_q", "mean_k"):
        in_specs.append(const(consts[name]))
        args.append(consts[name])
    if multi_seq:
        in_specs.append(const(consts["seq_dft"]))
        args.append(consts["seq_dft"])
    if rope:
        in_specs.append(pl.BlockSpec((tm, 6 * LANES), lambda i: (i % tps, 0)))
        args.append(consts["rope"])
    n_fourier = 1 if multi_seq else 2
    out_specs = [tok(MLA_PAD_WIDTH), tok(MLA_PAD_WIDTH), transposed(MIX_TILE)] + [seq_major] * n_fourier + [
                 tok(GQA_WIDTH), tok(GQA_KV_WIDTH), transposed(GQA_KV_WIDTH)]
    out_shape = [jax.ShapeDtypeStruct((n_tok, MLA_PAD_WIDTH), BF16),
                 jax.ShapeDtypeStruct((n_tok, MLA_PAD_WIDTH), BF16),
                 jax.ShapeDtypeStruct((n_seq, MIX_TILE, seq), BF16)] + [
                 jax.ShapeDtypeStruct((seq, n_seq * FNET_WIDTH), BF16)] * n_fourier + [
                 jax.ShapeDtypeStruct((n_tok, GQA_WIDTH), BF16),
                 jax.ShapeDtypeStruct((n_tok, GQA_KV_WIDTH), BF16),
                 jax.ShapeDtypeStruct((n_seq, GQA_KV_WIDTH, seq), BF16)]
    n_prev = 0
    if emit_cache:
        n_prev = prev_cache[0].shape[1] if prev_cache is not None else 0
        layers = lambda n, tail: pl.BlockSpec((spt, n) + tail, lambda i: (i, 0, 0, 0))
        tails = ((seq, MLA_KV_RANK), (MLA_ROPE_DIM, seq), (GQA_KV_WIDTH, seq), (GQA_KV_WIDTH, seq))
        if n_prev:
            in_specs += [layers(n_prev, t) for t in tails]
            args += list(prev_cache)
        out_specs += [layers(n_prev + 1, t) for t in tails]
        out_shape += [jax.ShapeDtypeStruct((n_seq, n_prev + 1) + t, F32) for t in tails]
    return pl.pallas_call(
        functools.partial(_proj_kernel, rope=rope, emit_cache=emit_cache, n_prev=n_prev, multi_seq=multi_seq),
        grid=(n_tiles,),
        in_specs=in_specs,
        out_specs=out_specs,
        out_shape=out_shape,
        compiler_params=_params(1),
        name="proj_rope" if rope else "proj_ctx",
    )(*args)


def _ctx_kernel(ckv_ref, kr_ref, gk_ref, gv_ref, wk_ref, wv_ref, place_ref, win_ref,
                km_ref, vm_ref, gko_ref, gvo_ref, wa_ref, wc_ref, wkr_ref):
    wt = win_ref[...]
    wa_ref[...] = wt[:_IN_A, :].T.astype(BF16)
    wc_ref[...] = wt[_IN_C0:, :].T.astype(BF16)
    kr_rows = jnp.concatenate([wt[_IN_A:_IN_C0, :], jnp.zeros((LANES - MLA_ROPE_DIM, wt.shape[1]), F32)], axis=0)
    wkr_ref[...] = kr_rows.T.astype(BF16)

    cb = ckv_ref[...].astype(BF16)
    kn = _dot(cb, wk_ref[...])
    kr = _dot(kr_ref[...].astype(BF16), place_ref[...])
    for hd in range(MLA_HEADS):
        km_ref[:, hd * LANES:(hd + 1) * LANES] = (kn[:, hd * LANES:(hd + 1) * LANES] + kr).astype(BF16)
    vm_ref[...] = _dot(cb, wv_ref[...]).T.astype(BF16)
    gko_ref[...] = gk_ref[...].astype(BF16)
    gvo_ref[...] = gv_ref[...].T.astype(BF16)


def _ctx_prep(layer, cache_ckv, cache_krope, cache_gk, cache_gv, w_in_t, wts, consts):
    nb, _, past, _ = cache_ckv.shape
    rows = D_MODEL // nb
    cache = lambda w: pl.BlockSpec((None, None, past, w), lambda b: (b, layer, 0, 0))
    out = lambda w: pl.BlockSpec((None, past, w), lambda b: (b, 0, 0))
    out_t = lambda w: pl.BlockSpec((None, w, past), lambda b: (b, 0, 0))
    part = lambda w: pl.BlockSpec((rows, w), lambda b: (b, 0))
    part_widths = (_IN_A, _IN_C, LANES)
    return pl.pallas_call(
        _ctx_kernel,
        grid=(nb,),
        in_specs=[cache(MLA_KV_RANK), cache(MLA_ROPE_DIM), cache(GQA_KV_WIDTH), cache(GQA_KV_WIDTH),
                  _layer_spec(wts["wk"], layer), _layer_spec(wts["wv"], layer),
                  pl.BlockSpec((MLA_ROPE_DIM, LANES), lambda b: (0, 0)),
                  pl.BlockSpec((None, w_in_t.shape[1], rows), lambda b: (layer, 0, b))],
        out_specs=[out(MLA_PAD_WIDTH), out_t(MIX_TILE), out(GQA_KV_WIDTH), out_t(GQA_KV_WIDTH)]
                  + [part(w) for w in part_widths],
        out_shape=[jax.ShapeDtypeStruct((nb, past, MLA_PAD_WIDTH), BF16),
                   jax.ShapeDtypeStruct((nb, MIX_TILE, past), BF16),
                   jax.ShapeDtypeStruct((nb, past, GQA_KV_WIDTH), BF16),
                   jax.ShapeDtypeStruct((nb, GQA_KV_WIDTH, past), BF16)]
                  + [jax.ShapeDtypeStruct((D_MODEL, w), BF16) for w in part_widths],
        compiler_params=_params(1),
        name="ctx_prep",
    )(cache_ckv, cache_krope, cache_gk.reshape(nb, DEPTH, past, GQA_KV_WIDTH),
      cache_gv.reshape(nb, DEPTH, past, GQA_KV_WIDTH), wts["wk"], wts["wv"], consts["place"], w_in_t)


def _attn_kernel(*refs, n_seg, ahead, n_cast):
    n_in = 1 + 2 * n_seg
    outs_at = 2 * n_in + n_cast
    families = []
    for f, mla in enumerate((True, False)):
        ins = refs[f * n_in:(f + 1) * n_in]
        families.append((mla, ins[0], ins[1:1 + n_seg], ins[1 + n_seg:], refs[outs_at + f]))
    for c in range(n_cast):
        refs[outs_at + 2 + c][...] = refs[2 * n_in + c][...].astype(BF16)
    n_seqs, tq = refs[0].shape[:2]
    lower = _lower_half(tq)
    top = lax.broadcasted_iota(jnp.int32, (LANES, tq), 0) < HALF

    def scores_t(f, b, p, hd):
        mla, q_ref, k_refs, _, _ = families[f]
        if mla:
            cols = slice((2 * p + hd) * LANES, (2 * p + hd + 1) * LANES)
            qh = q_ref[b, :, cols]
            keys = [k_ref[b, :, cols] for k_ref in k_refs]
        else:
            q = q_ref[b, :, p * LANES:(p + 1) * LANES]
            qh = jnp.where(lower if hd == 0 else ~lower, q, jnp.zeros_like(q))
            keys = [k_ref[b] for k_ref in k_refs]
        return [lax.dot_general(k, qh, (((1,), (1,)), ((), ())), preferred_element_type=F32) for k in keys]

    def softmax_t(ss):
        m = functools.reduce(jnp.maximum, [jnp.max(s, axis=0, keepdims=True) for s in ss])
        es = [jnp.exp2(s - m) for s in ss]
        denom = functools.reduce(jnp.add, [jnp.sum(e, axis=0, keepdims=True) for e in es])
        return [e.astype(BF16) for e in es], denom

    def weighted_values_t(f, b, p, es, denom):
        mla, _, _, vt_refs, _ = families[f]
        rows = slice(p * LANES, (p + 1) * LANES) if mla else slice(None)
        acc = functools.reduce(jnp.add, [_dot(vt_ref[b, rows, :], e) for e, vt_ref in zip(es, vt_refs)])
        return acc * (1.0 / denom)

    work = [(f, b, p, hd) for f in range(2) for b in range(n_seqs) for p in range(HEAD_PAIRS) for hd in range(2)]
    pending = [scores_t(*w) for w in work[:ahead]]
    outs = {}
    for i, (f, b, p, hd) in enumerate(work):
        es, denom = softmax_t(pending.pop(0))
        if i + ahead < len(work):
            pending.append(scores_t(*work[i + ahead]))
        outs[f, b, p, hd] = weighted_values_t(f, b, p, es, denom)
        if (p, hd) != (HEAD_PAIRS - 1, 1):
            continue
        mla, o_ref = families[f][0], families[f][4]
        tiles = [jnp.where(top, outs[f, b, t, 0], outs[f, b, t, 1]).T for t in range(HEAD_PAIRS)]
        if not mla:
            t1 = pltpu.roll(tiles[1], HALF, 1)
            tiles = [jnp.where(lower, tiles[0], t1), jnp.where(lower, tiles[2], tiles[0]),
                     jnp.where(lower, t1, tiles[2])]
        for t in range(HEAD_PAIRS):
            o_ref[b, :, t * LANES:(t + 1) * LANES] = tiles[t].astype(o_ref.dtype)


def _attention(mla_q, mla_kv, gqa_q, gqa_kv, *, tq, ahead, bb=1, cast=(), layer=0):
    n_batch, sq = mla_q.shape[:2]
    n_seg = len(mla_kv)
    n_q = sq // tq
    steps = (n_batch // bb) * n_q
    whole = lambda a: pl.BlockSpec((bb,) + a.shape[1:], lambda b, i: (b, 0, 0))
    in_specs, args = [], []
    for q, kv in ((mla_q, mla_kv), (gqa_q, gqa_kv)):
        in_specs += [pl.BlockSpec((bb, tq, q.shape[2]), lambda b, i: (b, i, 0))]
        in_specs += [whole(k) for k, _ in kv] + [whole(v) for _, v in kv]
        args += [q] + [k for k, _ in kv] + [v for _, v in kv]
    out_spec = pl.BlockSpec((bb, tq, MIX_TILE), lambda b, i: (b, i, 0))
    out_specs = [out_spec, out_spec]
    out_shape = [jax.ShapeDtypeStruct((n_batch, sq, MIX_TILE), BF16)] * 2
    for w in cast:
        rows, cols = w.shape[1] // steps, w.shape[2]
        in_specs.append(pl.BlockSpec((None, rows, cols), lambda b, i: (layer, b * n_q + i, 0)))
        out_specs.append(pl.BlockSpec((rows, cols), lambda b, i: (b * n_q + i, 0)))
        out_shape.append(jax.ShapeDtypeStruct(w.shape[1:], BF16))
        args.append(w)
    return pl.pallas_call(
        functools.partial(_attn_kernel, n_seg=n_seg, ahead=ahead, n_cast=len(cast)),
        grid=(n_batch // bb, n_q),
        in_specs=in_specs,
        out_specs=out_specs,
        out_shape=out_shape,
        compiler_params=_params(2),
        name=f"attn_s{n_seg}",
    )(*args)


def _fnet_kernel(ca_ref, sa_ref, cb_ref, sb_ref, uc_ref, us_ref, o_ref, tc_ref, ts_ref, *, scale):
    @pl.when(pl.program_id(1) == 0)
    def _build_twiddles():
        cb, sb = cb_ref[...], sb_ref[...]
        for a in range(ca_ref.shape[0]):
            ca, sa = ca_ref[a:a + 1, :], sa_ref[a:a + 1, :]
            rows = slice(a * DFT_SPLIT, (a + 1) * DFT_SPLIT)
            tc_ref[rows, :] = (ca * cb - sa * sb).astype(BF16)
            ts_ref[rows, :] = (sa * cb + ca * sb).astype(BF16)

    acc = _dot(tc_ref[...], uc_ref[...]) - _dot(ts_ref[...], us_ref[...])
    o_ref[...] = (acc * scale).astype(o_ref.dtype)


def _fnet(tables, uc, us, tm, tn):
    seq, width = uc.shape
    scale = float((seq * FNET_GROUP_DIM) ** -0.5)
    n_a = tm // DFT_SPLIT
    part_a = pl.BlockSpec((n_a, seq), lambda i, j: (i, 0))
    part_b = pl.BlockSpec((DFT_SPLIT, seq), lambda i, j: (0, 0))
    data = pl.BlockSpec((seq, tn), lambda i, j: (0, j))
    return pl.pallas_call(
        functools.partial(_fnet_kernel, scale=scale),
        grid=(seq // tm, width // tn),
        in_specs=[part_a, part_a, part_b, part_b, data, data],
        out_specs=pl.BlockSpec((tm, tn), lambda i, j: (i, j)),
        out_shape=jax.ShapeDtypeStruct((seq, width), BF16),
        scratch_shapes=[pltpu.VMEM((tm, seq), BF16), pltpu.VMEM((tm, seq), BF16)],
        compiler_params=_params(2),
        name="fnet",
    )(*tables, uc, us)


def _post_body(x_ref, mla_ref, fn_ref, gqa_ref, o_ref, mod_ref, g2_ref, wo_ref, wg_ref, wu_ref, wd_ref, gf_ref, final):
    fn = fn_ref[...]
    if fn.shape[1] > FNET_WIDTH:
        fn = jnp.concatenate([fn[:, j:j + FNET_WIDTH] for j in range(0, fn.shape[1], FNET_WIDTH)], axis=0)
    mix = jnp.concatenate([mla_ref[...], fn, gqa_ref[...]], axis=-1)
    gate1 = mod_ref[:, 2 * D_MODEL:3 * D_MODEL]
    shift2 = mod_ref[:, 3 * D_MODEL:4 * D_MODEL]
    scale2 = mod_ref[:, 4 * D_MODEL:5 * D_MODEL]
    gate2 = mod_ref[:, 5 * D_MODEL:6 * D_MODEL]
    x = x_ref[...] + gate1 * _dot(mix, wo_ref[...])
    h = (_rms(x) * g2_ref[...] * (1.0 + scale2) + shift2).astype(BF16)
    g = _dot(h, wg_ref[...])
    u = _dot(h, wu_ref[...])
    a = (g * jax.nn.sigmoid(g) * u).astype(BF16)
    x = x + gate2 * _dot(a, wd_ref[...])
    if final:
        x = _rms(x) * gf_ref[...]
    o_ref[...] = x


def _post_kernel(*refs, final, n_first, with_next_mod):
    first, second, shared = refs[0:4], refs[4:8], refs[8:15]
    n_in = 18 if with_next_mod else 15
    o_first, o_second = refs[n_in:n_in + 2]
    step = pl.program_id(0)

    def run(group, o_ref):
        if with_next_mod:
            _ada_kernel(*refs[15:18], refs[n_in + 2])
        _post_body(*group, o_ref, *shared, final)

    @pl.when(step < n_first)
    def _first_group():
        run(first, o_first)

    @pl.when(step >= n_first)
    def _second_group():
        run(second, o_second)


def _post(groups, mod, g2, ffn, layer, g_final, *, final, tm, next_mod=None):
    (x1, _, _, _, seq1), (x2, _, _, _, seq2) = groups
    n1, n2 = x1.shape[0] // tm, x2.shape[0] // tm
    tiles_per_mod = n2 // (mod.shape[0] - 1)
    tile1 = lambda i: jnp.minimum(i, n1 - 1)
    tile2 = lambda i: jnp.maximum(i - n1, 0)
    resident = lambda arr: pl.BlockSpec(arr.shape, lambda i: (0, 0), pipeline_mode=pl.Buffered(1))

    def group_specs(seq, tile):
        tok = lambda w: pl.BlockSpec((tm, w), lambda i: (tile(i), 0))
        if tm <= seq:
            tps = seq // tm
            fnet_spec = pl.BlockSpec((tm, FNET_WIDTH), lambda i: (tile(i) % tps, tile(i) // tps))
        else:
            fnet_spec = pl.BlockSpec((seq, (tm // seq) * FNET_WIDTH), lambda i: (0, tile(i)))
        return [tok(D_MODEL), tok(MIX_TILE), fnet_spec, tok(MIX_TILE)], tok(D_MODEL)

    in1, out1 = group_specs(seq1, tile1)
    in2, out2 = group_specs(seq2, tile2)
    mod_spec = pl.BlockSpec((None, 1, 6 * D_MODEL),
                            lambda i: (jnp.where(i < n1, 0, 1 + tile2(i) // tiles_per_mod), 0, 0))
    in_specs = (in1 + in2 + [mod_spec, _layer_spec(g2, layer)] + [resident(w) for w in ffn]
                + [pl.BlockSpec((1, D_MODEL), lambda i: (0, 0))])
    args = [*groups[0][:4], *groups[1][:4], mod, g2, *ffn, g_final]
    out_specs = [out1, out2]
    out_shape = [jax.ShapeDtypeStruct(x1.shape, F32), jax.ShapeDtypeStruct(x2.shape, F32)]
    if next_mod is not None:
        cond_t, w_ada, b_ada = next_mod
        ada_in, ada_out, ada_shape = _ada_specs(cond_t, w_ada, layer + 1, w_ada.shape[2] // (n1 + n2))
        in_specs += ada_in
        args += [cond_t, w_ada, b_ada]
        out_specs.append(ada_out)
        out_shape.append(ada_shape)
    return pl.pallas_call(
        functools.partial(_post_kernel, final=final, n_first=n1, with_next_mod=next_mod is not None),
        grid=(n1 + n2,),
        in_specs=in_specs,
        out_specs=out_specs,
        out_shape=out_shape,
        compiler_params=_params(1),
        name="post_final" if final else "post",
    )(*args)


def _prep_weights(g_norm1, g_norm2, g_q_a, w_q_up, g_kv_a, w_kv_up, g_q_head, g_k_head):
    row = lambda g: g.reshape(DEPTH, 1, -1)
    wq = w_q_up.reshape(DEPTH, MLA_Q_RANK, MLA_HEADS, MLA_QK_DIM)
    wq = jnp.concatenate([wq[..., MLA_NOPE_DIM:], wq[..., :MLA_NOPE_DIM]], axis=-1)
    wq = jnp.pad(wq, ((0, 0), (0, 0), (0, 0), (0, LANES - MLA_QK_DIM)))
    wkv = w_kv_up.reshape(DEPTH, MLA_KV_RANK, MLA_HEADS, MLA_NOPE_DIM + MLA_V_DIM)
    wk = jnp.pad(wkv[..., :MLA_NOPE_DIM], ((0, 0), (0, 0), (0, 0), (MLA_ROPE_DIM, LANES - MLA_QK_DIM)))
    wv = wkv[..., MLA_NOPE_DIM:]
    return {
        "g1": row(g_norm1), "g2": row(g_norm2), "g_q_a": row(g_q_a), "g_kv_a": row(g_kv_a),
        "g_q_head": row(jnp.tile(g_q_head, (1, GQA_HEADS))), "g_k_head": row(jnp.tile(g_k_head, (1, GQA_KV_HEADS))),
        "wq": wq.reshape(DEPTH, MLA_Q_RANK, MLA_PAD_WIDTH).astype(BF16),
        "wk": wk.reshape(DEPTH, MLA_KV_RANK, MLA_PAD_WIDTH).astype(BF16),
        "wv": wv.reshape(DEPTH, MLA_KV_RANK, MIX_TILE).astype(BF16),
    }


def kernel(x_prompt, x_sample, cache_mla_ckv, cache_mla_krope, cache_gqa_k, cache_gqa_v, c, c_ctx, w_ada, b_ada,
           g_norm1, g_norm2, w_in, g_q_a, w_q_up, g_kv_a, w_kv_up, g_q_head, g_k_head, w_out,
           w_ffn_gate, w_ffn_up, w_ffn_down, g_final):
    n_pb, p_seq, _ = x_prompt.shape
    n_sb, s_seq, _ = x_sample.shape
    consts = {"wdft": _channel_dft(), "mean_q": _group_mean_matrix(GQA_HEADS),
              "mean_k": _group_mean_matrix(GQA_KV_HEADS), "place": _rope_place_matrix(),
              "rope": _rope_tables(s_seq), "seq_dft": _seq_dft_matrix(p_seq)}
    dft_s = _seq_dft_tables(s_seq)
    gf = g_final.reshape(1, D_MODEL)
    wts = _prep_weights(g_norm1, g_norm2, g_q_a, w_q_up, g_kv_a, w_kv_up, g_q_head, g_k_head)
    w_in_t = jnp.swapaxes(w_in, 1, 2)

    cond_t = jnp.concatenate([c_ctx[None, :], c], axis=0).T
    b_ada3 = b_ada.reshape(DEPTH, 1, 6 * D_MODEL)
    mod = _ada(cond_t, w_ada, b_ada3, 0)

    xp = x_prompt.reshape(n_pb * p_seq, D_MODEL)
    xs = x_sample.reshape(n_sb * s_seq, D_MODEL)
    b3 = lambda a, nb: a.reshape(nb, a.shape[0] // nb, a.shape[1])
    caches = None
    for l in range(DEPTH):
        final = l == DEPTH - 1
        mod_p = mod[0:1].reshape(1, 1, 6 * D_MODEL)
        mod_s = mod[1:].reshape(n_sb, 1, 6 * D_MODEL)

        km_c, vm_c, gk_c, gv_c, w_a, w_c, w_kr = _ctx_prep(
            l, cache_mla_ckv, cache_mla_krope, cache_gqa_k, cache_gqa_v, w_in_t, wts, consts)
        win_parts = {"w_a": w_a, "w_c": w_c, "w_kr": w_kr}
        qm, km, vm, uc_s, us_s, gq, gk, gv = _proj(xs, mod_s, wts, win_parts, l, consts, seq=s_seq, rope=True)
        mla_s, gqa_s, *ffn = _attention(b3(qm, n_sb), [(km_c, vm_c), (b3(km, n_sb), vm)],
                                        b3(gq, n_sb), [(gk_c, gv_c), (b3(gk, n_sb), gv)], tq=LATENT_Q_TILE,
                                        ahead=QK_AHEAD_LATENT,
                                        cast=(w_out, w_ffn_gate, w_ffn_up, w_ffn_down), layer=l)

        qm, km, vm, fn_p, gq, gk, gv, *caches = _proj(
            xp, mod_p, wts, win_parts, l, consts, seq=p_seq, rope=False, emit_cache=True, prev_cache=caches)
        mla_o, gqa_o = _attention(b3(qm, n_pb), [(b3(km, n_pb), vm)], b3(gq, n_pb), [(b3(gk, n_pb), gv)],
                                  tq=p_seq, bb=CONTEXT_SEQS_PER_STEP, ahead=QK_AHEAD_CONTEXT)
        fn_s = _fnet(dft_s, uc_s, us_s, tm=FNET_ROW_TILE, tn=n_sb * FNET_WIDTH)
        flat = lambda a: a.reshape(-1, MIX_TILE)
        xp, xs, *mod_next = _post(
            [(xp, flat(mla_o), fn_p, flat(gqa_o), p_seq), (xs, flat(mla_s), fn_s, flat(gqa_s), s_seq)],
            mod.reshape(1 + n_sb, 1, 6 * D_MODEL), wts["g2"], ffn, l, gf, final=final, tm=POST_TILE,
            next_mod=None if final else (cond_t, w_ada, b_ada3))
        if mod_next:
            mod = mod_next[0]

    ckv_new, krope_t, gk_t, gv_t = caches
    heads = lambda a: jnp.swapaxes(a, 2, 3).reshape(n_pb, DEPTH, p_seq, GQA_KV_HEADS, GQA_HEAD_DIM)
    return (xp.reshape(n_pb, p_seq, D_MODEL), xs.reshape(n_sb, s_seq, D_MODEL),
            ckv_new, jnp.swapaxes(krope_t, 2, 3), heads(gk_t), heads(gv_t))
```

```python
import functools

import numpy as np
import jax
import jax.numpy as jnp
from jax import lax
from jax.experimental import pallas as pl
from jax.experimental.pallas import tpu as pltpu

D_MODEL = 1024
DEPTH = 2
GRID_W = 64
MLA_HEADS = 6
MLA_Q_RANK = 384
MLA_KV_RANK = 256
MLA_NOPE_DIM = 64
MLA_ROPE_DIM = 32
MLA_V_DIM = 64
MLA_QK_DIM = MLA_NOPE_DIM + MLA_ROPE_DIM
FNET_GROUPS = 4
FNET_GROUP_DIM = 64
FNET_WIDTH = FNET_GROUPS * FNET_GROUP_DIM
GQA_HEADS = 6
GQA_KV_HEADS = 2
GQA_HEAD_DIM = 64
GQA_GROUP = GQA_HEADS // GQA_KV_HEADS
GQA_WIDTH = GQA_HEADS * GQA_HEAD_DIM
GQA_KV_WIDTH = GQA_KV_HEADS * GQA_HEAD_DIM
D_FF = 2816
ROPE_THETA = 10000.0
EPS = 1e-6
LOG2_E = 1.4426950408889634

LANES = 128
HALF = LANES // 2
HEAD_PAIRS = MLA_HEADS // 2
MLA_PAD_WIDTH = MLA_HEADS * LANES
MIX_TILE = HEAD_PAIRS * LANES
MIB = 1024 * 1024
VMEM_LIMIT_MIB = 52
POST_TILE = 512
PROJ_TILE = 512
PROJ_SUB = 256
LATENT_Q_TILE = 512
CONTEXT_SEQS_PER_STEP = 4
QK_AHEAD_LATENT = 2
QK_AHEAD_CONTEXT = 8
FNET_ROW_TILE = 512


def _params(n_grid_dims):
    return pltpu.CompilerParams(dimension_semantics=("arbitrary",) * n_grid_dims,
                                vmem_limit_bytes=VMEM_LIMIT_MIB * MIB)


_IN_A = MLA_Q_RANK + MLA_KV_RANK
_IN_C0 = _IN_A + MLA_ROPE_DIM
_IN_C = FNET_WIDTH + GQA_WIDTH + 2 * GQA_KV_WIDTH
_C_U, _C_GQ, _C_GK, _C_GV = 0, FNET_WIDTH, FNET_WIDTH + GQA_WIDTH, FNET_WIDTH + GQA_WIDTH + GQA_KV_WIDTH

BF16 = jnp.bfloat16
F32 = jnp.float32


def _rope_tables(n_tokens):
    t = np.arange(n_tokens)
    row = (t // GRID_W).astype(np.float64)
    col = (t % GRID_W).astype(np.float64)

    def angles(rot_dim):
        n_axis = rot_dim // 4
        inv = ROPE_THETA ** (-np.arange(n_axis, dtype=np.float64) / n_axis)
        return np.concatenate([row[:, None] * inv, col[:, None] * inv], axis=-1)

    def tables(ang, lane_to_pair, is_first, is_second):
        cos = np.where((is_first | is_second)[None, :], np.cos(ang)[:, lane_to_pair], 1.0)
        sin = np.sin(ang)[:, lane_to_pair]
        sin_a = np.where(is_second[None, :], sin, 0.0)
        sin_b = np.where(is_first[None, :], -sin, 0.0)
        return [cos, sin_a, sin_b]

    lane = np.arange(LANES)
    half_m = MLA_ROPE_DIM // 2
    in_rope = lane < MLA_ROPE_DIM
    first_m = lane < half_m
    second_m = in_rope & ~first_m
    pair_m = lane % half_m
    half_g = GQA_HEAD_DIM // 2
    first_g = (lane % GQA_HEAD_DIM) < half_g
    pair_g = lane % half_g
    tabs = (tables(angles(MLA_ROPE_DIM), pair_m, first_m, second_m)
            + tables(angles(GQA_HEAD_DIM), pair_g, first_g, ~first_g))
    return jnp.asarray(np.concatenate(tabs, axis=-1), dtype=F32)


def _channel_dft():
    c = np.arange(FNET_GROUP_DIM)
    ang = 2.0 * np.pi * np.outer(c, c) / FNET_GROUP_DIM
    eye = np.eye(FNET_GROUPS)
    table = np.concatenate([np.kron(eye, np.cos(ang)), np.kron(eye, np.sin(ang))], axis=1)
    return jnp.asarray(table, dtype=F32).astype(BF16)


def _group_mean_matrix(n_heads):
    return jnp.asarray(np.kron(np.eye(n_heads), np.full((GQA_HEAD_DIM, GQA_HEAD_DIM), 1.0 / GQA_HEAD_DIM)), dtype=BF16)


def _rope_place_matrix():
    return jnp.asarray(np.eye(MLA_ROPE_DIM, LANES), dtype=BF16)


DFT_SPLIT = 32


def _seq_dft_matrix(seq):
    s = np.arange(seq)
    ang = 2.0 * np.pi * ((s[:, None] * s[None, :]) % seq) / seq
    return jnp.asarray(np.stack([np.cos(ang), np.sin(ang)]), dtype=F32).astype(BF16)


def _seq_dft_tables(seq):
    s = np.arange(seq)
    n_a = seq // DFT_SPLIT
    ang_a = 2.0 * np.pi * ((np.arange(n_a)[:, None] * s[None, :]) % n_a) / n_a
    ang_b = 2.0 * np.pi * ((np.arange(DFT_SPLIT)[:, None] * s[None, :]) % seq) / seq
    return tuple(jnp.asarray(t, dtype=F32) for t in (np.cos(ang_a), np.sin(ang_a), np.cos(ang_b), np.sin(ang_b)))


def _rms(x):
    return x * lax.rsqrt(jnp.mean(x * x, axis=-1, keepdims=True) + EPS)


def _dot(a, b):
    return jnp.dot(a, b, preferred_element_type=F32)


def _group_rms(x, mean_mat):
    ms = _dot((x * x).astype(BF16), mean_mat)
    return x * lax.rsqrt(ms + EPS)


def _rotate(x, cos, sin_a, sin_b, half):
    return x * cos + pltpu.roll(x, half, 1) * sin_a + pltpu.roll(x, LANES - half, 1) * sin_b


def _lower_half(rows):
    return lax.broadcasted_iota(jnp.int32, (rows, LANES), 1) < HALF


def _layer_spec(arr, layer):
    return pl.BlockSpec((None,) + arr.shape[1:], lambda i: (layer, 0, 0))


def _ada_kernel(ct_ref, w_ref, b_ref, o_ref):
    s = ct_ref[...]
    s = s * jax.nn.sigmoid(s)
    w = w_ref[...]
    for m in range(o_ref.shape[0]):
        o_ref[m:m + 1, :] = jnp.sum(w * s[:, m:m + 1], axis=0, keepdims=True) + b_ref[...]


def _ada_specs(cond_t, w_ada, layer, tn):
    n_cond = cond_t.shape[1]
    in_specs = [pl.BlockSpec((D_MODEL, n_cond), lambda j: (0, 0)),
                pl.BlockSpec((None, D_MODEL, tn), lambda j: (layer, 0, j)),
                pl.BlockSpec((None, 1, tn), lambda j: (layer, 0, j))]
    out_shape = jax.ShapeDtypeStruct((n_cond, w_ada.shape[2]), F32)
    return in_specs, pl.BlockSpec((n_cond, tn), lambda j: (0, j)), out_shape


_PROJ_WEIGHTS = ("g1", "w_a", "w_c", "w_kr", "g_q_a", "wq", "g_kv_a", "wk", "wv", "g_q_head", "g_k_head")


def _proj_kernel(*refs, rope, emit_cache, n_prev, multi_seq):
    (x_ref, mod_ref, g1_ref, wa_ref, wc_ref, wkr_ref, gqa_ref, wq_ref, gkva_ref, wk_ref, wv_ref,
     gqh_ref, gkh_ref, wdft_ref, mq_ref, mk_ref) = refs[:16]
    refs = refs[16:]
    if multi_seq:
        sdft_ref, refs = refs[0], refs[1:]
    if rope:
        tab_ref, refs = refs[0], refs[1:]
    if n_prev:
        prev_refs, refs = refs[:4], refs[4:]
    n_fourier = 1 if multi_seq else 2
    qm_ref, km_ref, vm_ref = refs[:3]
    fourier_refs = refs[3:3 + n_fourier]
    gq_ref, gk_ref, gv_ref = refs[3 + n_fourier:6 + n_fourier]
    cache_refs = refs[6 + n_fourier:]

    shift = mod_ref[:, 0:D_MODEL]
    scale = mod_ref[:, D_MODEL:2 * D_MODEL]
    lower = _lower_half(PROJ_SUB)

    for r in range(x_ref.shape[0] // PROJ_SUB):
        rows = slice(r * PROJ_SUB, (r + 1) * PROJ_SUB)
        seq_cols = (slice(None), slice(r * FNET_WIDTH, (r + 1) * FNET_WIDTH)) if multi_seq else (rows, slice(None))
        t_idx = (r,) if multi_seq else (slice(None), rows)

        hb = (_rms(x_ref[rows, :]) * g1_ref[...] * (1.0 + scale) + shift).astype(BF16)
        pa = _dot(hb, wa_ref[...])
        pc = _dot(hb, wc_ref[...])
        kr = _dot(hb, wkr_ref[...])

        if rope:
            cos_m, sa_m, sb_m = (tab_ref[rows, i * LANES:(i + 1) * LANES] for i in range(3))
            cos_g, sa_g, sb_g = (tab_ref[rows, i * LANES:(i + 1) * LANES] for i in range(3, 6))

        gqn = _group_rms(pc[:, _C_GQ:_C_GQ + GQA_WIDTH], mq_ref[...]) * gqh_ref[...]
        tiles = []
        for p in range(HEAD_PAIRS):
            gp = gqn[:, p * LANES:(p + 1) * LANES]
            if rope:
                gp = _rotate(gp, cos_g, sa_g, sb_g, GQA_HEAD_DIM // 2)
            tiles.append(gp * (LOG2_E * GQA_HEAD_DIM ** -0.5))
        gq_ref[rows, 0:LANES] = jnp.where(lower, tiles[0], tiles[1]).astype(BF16)
        gq_ref[rows, LANES:2 * LANES] = jnp.where(
            lower, pltpu.roll(tiles[0], HALF, 1), pltpu.roll(tiles[2], HALF, 1)).astype(BF16)
        gq_ref[rows, 2 * LANES:] = jnp.where(lower, tiles[1], tiles[2]).astype(BF16)
        gkn = _group_rms(pc[:, _C_GK:_C_GK + GQA_KV_WIDTH], mk_ref[...]) * gkh_ref[...]
        gkr = _rotate(gkn, cos_g, sa_g, sb_g, GQA_HEAD_DIM // 2) if rope else gkn
        gv_t = pc[:, _C_GV:_C_GV + GQA_KV_WIDTH].T
        gk_ref[rows, :] = gkr.astype(BF16)
        gv_ref[t_idx] = gv_t.astype(BF16)

        cqn = _rms(pa[:, :MLA_Q_RANK]) * gqa_ref[...]
        q = _dot(cqn.astype(BF16), wq_ref[...])
        for hd in range(MLA_HEADS):
            qh = q[:, hd * LANES:(hd + 1) * LANES]
            if rope:
                qh = _rotate(qh, cos_m, sa_m, sb_m, MLA_ROPE_DIM // 2)
            qm_ref[rows, hd * LANES:(hd + 1) * LANES] = (qh * (LOG2_E * MLA_QK_DIM ** -0.5)).astype(BF16)

        ckvn = _rms(pa[:, MLA_Q_RANK:]) * gkva_ref[...]
        cb = ckvn.astype(BF16)
        kn = _dot(cb, wk_ref[...])
        krr = _rotate(kr, cos_m, sa_m, sb_m, MLA_ROPE_DIM // 2) if rope else kr
        for hd in range(MLA_HEADS):
            km_ref[rows, hd * LANES:(hd + 1) * LANES] = (kn[:, hd * LANES:(hd + 1) * LANES] + krr).astype(BF16)
        vm_ref[t_idx] = _dot(cb, wv_ref[...]).T.astype(BF16)

        ucs = _dot(pc[:, _C_U:_C_U + FNET_WIDTH].astype(BF16), wdft_ref[...])
        uc = ucs[:, :FNET_WIDTH].astype(BF16)
        us = ucs[:, FNET_WIDTH:].astype(BF16)
        if multi_seq:
            fn = _dot(sdft_ref[0], uc) - _dot(sdft_ref[1], us)
            fourier_refs[0][seq_cols] = (fn * (PROJ_SUB * FNET_GROUP_DIM) ** -0.5).astype(BF16)
        else:
            fourier_refs[0][seq_cols] = uc
            fourier_refs[1][seq_cols] = us

        if emit_cache:
            new = (ckvn, kr.T[:MLA_ROPE_DIM, :], gkn.T, gv_t)
            for i, out_ref in enumerate(cache_refs):
                if n_prev:
                    out_ref[r, :n_prev] = prev_refs[i][r]
                out_ref[r, n_prev] = new[i]


def _proj(x, mod, wts, win_parts, layer, consts, *, seq, rope, prev_cache=None, emit_cache=False, tm=PROJ_TILE):
    n_tok = x.shape[0]
    n_tiles = n_tok // tm
    n_seq = n_tok // seq
    tiles_per_mod = n_tiles // mod.shape[0]
    multi_seq = tm > seq
    if multi_seq:
        assert seq == PROJ_SUB and not rope
        spt = tm // seq
        seq_major = pl.BlockSpec((seq, spt * FNET_WIDTH), lambda i: (0, i))
        transposed = lambda w: pl.BlockSpec((spt, w, seq), lambda i: (i, 0, 0))
    else:
        assert not emit_cache
        tps = seq // tm
        seq_major = pl.BlockSpec((tm, FNET_WIDTH), lambda i: (i % tps, i // tps))
        transposed = lambda w: pl.BlockSpec((None, w, tm), lambda i: (i // tps, 0, i % tps))
    const = lambda arr: pl.BlockSpec(arr.shape, lambda i: (0,) * arr.ndim)
    tok = lambda w: pl.BlockSpec((tm, w), lambda i: (i, 0))
    in_specs = [tok(D_MODEL), pl.BlockSpec((None, 1, 6 * D_MODEL), lambda i: (i // tiles_per_mod, 0, 0))]
    args = [x, mod]
    for name in _PROJ_WEIGHTS:
        arr = win_parts.get(name)
        in_specs.append(const(arr) if arr is not None else _layer_spec(wts[name], layer))
        args.append(arr if arr is not None else wts[name])
    for name in ("wdft", "mean_q", "mean_k"):
        in_specs.append(const(consts[name]))
        args.append(consts[name])
    if multi_seq:
        in_specs.append(const(consts["seq_dft"]))
        args.append(consts["seq_dft"])
    if rope:
        in_specs.append(pl.BlockSpec((tm, 6 * LANES), lambda i: (i % tps, 0)))
        args.append(consts["rope"])
    n_fourier = 1 if multi_seq else 2
    out_specs = [tok(MLA_PAD_WIDTH), tok(MLA_PAD_WIDTH), transposed(MIX_TILE)] + [seq_major] * n_fourier + [
                 tok(GQA_WIDTH), tok(GQA_KV_WIDTH), transposed(GQA_KV_WIDTH)]
    out_shape = [jax.ShapeDtypeStruct((n_tok, MLA_PAD_WIDTH), BF16),
                 jax.ShapeDtypeStruct((n_tok, MLA_PAD_WIDTH), BF16),
                 jax.ShapeDtypeStruct((n_seq, MIX_TILE, seq), BF16)] + [
                 jax.ShapeDtypeStruct((seq, n_seq * FNET_WIDTH), BF16)] * n_fourier + [
                 jax.ShapeDtypeStruct((n_tok, GQA_WIDTH), BF16),
                 jax.ShapeDtypeStruct((n_tok, GQA_KV_WIDTH), BF16),
                 jax.ShapeDtypeStruct((n_seq, GQA_KV_WIDTH, seq), BF16)]
    n_prev = 0
    if emit_cache:
        n_prev = prev_cache[0].shape[1] if prev_cache is not None else 0
        layers = lambda n, tail: pl.BlockSpec((spt, n) + tail, lambda i: (i, 0, 0, 0))
        tails = ((seq, MLA_KV_RANK), (MLA_ROPE_DIM, seq), (GQA_KV_WIDTH, seq), (GQA_KV_WIDTH, seq))
        if n_prev:
            in_specs += [layers(n_prev, t) for t in tails]
            args += list(prev_cache)
        out_specs += [layers(n_prev + 1, t) for t in tails]
        out_shape += [jax.ShapeDtypeStruct((n_seq, n_prev + 1) + t, F32) for t in tails]
    return pl.pallas_call(
        functools.partial(_proj_kernel, rope=rope, emit_cache=emit_cache, n_prev=n_prev, multi_seq=multi_seq),
        grid=(n_tiles,),
        in_specs=in_specs,
        out_specs=out_specs,
        out_shape=out_shape,
        compiler_params=_params(1),
        name="proj_rope" if rope else "proj_ctx",
    )(*args)


def _ctx_kernel(ckv_ref, kr_ref, gk_ref, gv_ref, wk_ref, wv_ref, place_ref, win_ref, *refs):
    if len(refs) > 7:
        _ada_kernel(*refs[:3], refs[-1])
        refs = refs[3:-1]
    km_ref, vm_ref, gko_ref, gvo_ref, wa_ref, wc_ref, wkr_ref = refs
    wt = win_ref[...]
    wa_ref[...] = wt[:_IN_A, :].T.astype(BF16)
    wc_ref[...] = wt[_IN_C0:, :].T.astype(BF16)
    kr_rows = jnp.concatenate([wt[_IN_A:_IN_C0, :], jnp.zeros((LANES - MLA_ROPE_DIM, wt.shape[1]), F32)], axis=0)
    wkr_ref[...] = kr_rows.T.astype(BF16)

    cb = ckv_ref[...].astype(BF16)
    kn = _dot(cb, wk_ref[...])
    kr = _dot(kr_ref[...].astype(BF16), place_ref[...])
    for hd in range(MLA_HEADS):
        km_ref[:, hd * LANES:(hd + 1) * LANES] = (kn[:, hd * LANES:(hd + 1) * LANES] + kr).astype(BF16)
    vm_ref[...] = _dot(cb, wv_ref[...]).T.astype(BF16)
    gko_ref[...] = gk_ref[...].astype(BF16)
    gvo_ref[...] = gv_ref[...].T.astype(BF16)


def _ctx_prep(layer, cache_ckv, cache_krope, cache_gk, cache_gv, w_in_t, wts, consts, this_mod=None):
    nb, _, past, _ = cache_ckv.shape
    rows = D_MODEL // nb
    cache = lambda w: pl.BlockSpec((None, None, past, w), lambda b: (b, layer, 0, 0))
    out = lambda w: pl.BlockSpec((None, past, w), lambda b: (b, 0, 0))
    out_t = lambda w: pl.BlockSpec((None, w, past), lambda b: (b, 0, 0))
    part = lambda w: pl.BlockSpec((rows, w), lambda b: (b, 0))
    part_widths = (_IN_A, _IN_C, LANES)
    in_specs = [cache(MLA_KV_RANK), cache(MLA_ROPE_DIM), cache(GQA_KV_WIDTH), cache(GQA_KV_WIDTH),
                _layer_spec(wts["wk"], layer), _layer_spec(wts["wv"], layer),
                pl.BlockSpec((MLA_ROPE_DIM, LANES), lambda b: (0, 0)),
                pl.BlockSpec((None, w_in_t.shape[1], rows), lambda b: (layer, 0, b))]
    args = [cache_ckv, cache_krope, cache_gk.reshape(nb, DEPTH, past, GQA_KV_WIDTH),
            cache_gv.reshape(nb, DEPTH, past, GQA_KV_WIDTH), wts["wk"], wts["wv"], consts["place"], w_in_t]
    out_specs = ([out(MLA_PAD_WIDTH), out_t(MIX_TILE), out(GQA_KV_WIDTH), out_t(GQA_KV_WIDTH)]
                 + [part(w) for w in part_widths])
    out_shape = ([jax.ShapeDtypeStruct((nb, past, MLA_PAD_WIDTH), BF16),
                  jax.ShapeDtypeStruct((nb, MIX_TILE, past), BF16),
                  jax.ShapeDtypeStruct((nb, past, GQA_KV_WIDTH), BF16),
                  jax.ShapeDtypeStruct((nb, GQA_KV_WIDTH, past), BF16)]
                 + [jax.ShapeDtypeStruct((D_MODEL, w), BF16) for w in part_widths])
    if this_mod is not None:
        cond_t, w_ada, b_ada = this_mod
        ada_in, ada_out, ada_shape = _ada_specs(cond_t, w_ada, layer, w_ada.shape[2] // nb)
        in_specs += ada_in
        args += [cond_t, w_ada, b_ada]
        out_specs.append(ada_out)
        out_shape.append(ada_shape)
    return pl.pallas_call(
        _ctx_kernel,
        grid=(nb,),
        in_specs=in_specs,
        out_specs=out_specs,
        out_shape=out_shape,
        compiler_params=_params(1),
        name="ctx_prep",
    )(*args)


def _attn_kernel(*refs, n_seg, ahead, n_cast):
    n_in = 1 + 2 * n_seg
    outs_at = 2 * n_in + n_cast
    families = []
    for f, mla in enumerate((True, False)):
        ins = refs[f * n_in:(f + 1) * n_in]
        families.append((mla, ins[0], ins[1:1 + n_seg], ins[1 + n_seg:], refs[outs_at + f]))
    for c in range(n_cast):
        refs[outs_at + 2 + c][...] = refs[2 * n_in + c][...].astype(BF16)
    n_seqs, tq = refs[0].shape[:2]
    lower = _lower_half(tq)
    top = lax.broadcasted_iota(jnp.int32, (LANES, tq), 0) < HALF

    def scores_t(f, b, p, hd):
        mla, q_ref, k_refs, _, _ = families[f]
        if mla:
            cols = slice((2 * p + hd) * LANES, (2 * p + hd + 1) * LANES)
            qh = q_ref[b, :, cols]
            keys = [k_ref[b, :, cols] for k_ref in k_refs]
        else:
            q = q_ref[b, :, p * LANES:(p + 1) * LANES]
            qh = jnp.where(lower if hd == 0 else ~lower, q, jnp.zeros_like(q))
            keys = [k_ref[b] for k_ref in k_refs]
        return [lax.dot_general(k, qh, (((1,), (1,)), ((), ())), preferred_element_type=F32) for k in keys]

    def softmax_t(ss):
        m = functools.reduce(jnp.maximum, [jnp.max(s, axis=0, keepdims=True) for s in ss])
        es = [jnp.exp2(s - m) for s in ss]
        denom = functools.reduce(jnp.add, [jnp.sum(e, axis=0, keepdims=True) for e in es])
        return [e.astype(BF16) for e in es], denom

    def weighted_values_t(f, b, p, es, denom):
        mla, _, _, vt_refs, _ = families[f]
        rows = slice(p * LANES, (p + 1) * LANES) if mla else slice(None)
        acc = functools.reduce(jnp.add, [_dot(vt_ref[b, rows, :], e) for e, vt_ref in zip(es, vt_refs)])
        return acc * (1.0 / denom)

    work = [(f, b, p, hd) for f in range(2) for b in range(n_seqs) for p in range(HEAD_PAIRS) for hd in range(2)]
    pending = [scores_t(*w) for w in work[:ahead]]
    outs = {}
    for i, (f, b, p, hd) in enumerate(work):
        es, denom = softmax_t(pending.pop(0))
        if i + ahead < len(work):
            pending.append(scores_t(*work[i + ahead]))
        outs[f, b, p, hd] = weighted_values_t(f, b, p, es, denom)
        if (p, hd) != (HEAD_PAIRS - 1, 1):
            continue
        mla, o_ref = families[f][0], families[f][4]
        tiles = [jnp.where(top, outs[f, b, t, 0], outs[f, b, t, 1]).T for t in range(HEAD_PAIRS)]
        if not mla:
            t1 = pltpu.roll(tiles[1], HALF, 1)
            tiles = [jnp.where(lower, tiles[0], t1), jnp.where(lower, tiles[2], tiles[0]),
                     jnp.where(lower, t1, tiles[2])]
        for t in range(HEAD_PAIRS):
            o_ref[b, :, t * LANES:(t + 1) * LANES] = tiles[t].astype(o_ref.dtype)


def _attention(mla_q, mla_kv, gqa_q, gqa_kv, *, tq, ahead, bb=1, cast=(), layer=0):
    n_batch, sq = mla_q.shape[:2]
    n_seg = len(mla_kv)
    n_q = sq // tq
    steps = (n_batch // bb) * n_q
    whole = lambda a: pl.BlockSpec((bb,) + a.shape[1:], lambda b, i: (b, 0, 0))
    in_specs, args = [], []
    for q, kv in ((mla_q, mla_kv), (gqa_q, gqa_kv)):
        in_specs += [pl.BlockSpec((bb, tq, q.shape[2]), lambda b, i: (b, i, 0))]
        in_specs += [whole(k) for k, _ in kv] + [whole(v) for _, v in kv]
        args += [q] + [k for k, _ in kv] + [v for _, v in kv]
    out_spec = pl.BlockSpec((bb, tq, MIX_TILE), lambda b, i: (b, i, 0))
    out_specs = [out_spec, out_spec]
    out_shape = [jax.ShapeDtypeStruct((n_batch, sq, MIX_TILE), BF16)] * 2
    for w in cast:
        rows, cols = w.shape[1] // steps, w.shape[2]
        in_specs.append(pl.BlockSpec((None, rows, cols), lambda b, i: (layer, b * n_q + i, 0)))
        out_specs.append(pl.BlockSpec((rows, cols), lambda b, i: (b * n_q + i, 0)))
        out_shape.append(jax.ShapeDtypeStruct(w.shape[1:], BF16))
        args.append(w)
    return pl.pallas_call(
        functools.partial(_attn_kernel, n_seg=n_seg, ahead=ahead, n_cast=len(cast)),
        grid=(n_batch // bb, n_q),
        in_specs=in_specs,
        out_specs=out_specs,
        out_shape=out_shape,
        compiler_params=_params(2),
        name=f"attn_s{n_seg}",
    )(*args)


def _fnet_kernel(ca_ref, sa_ref, cb_ref, sb_ref, uc_ref, us_ref, o_ref, tc_ref, ts_ref, *, scale):
    @pl.when(pl.program_id(1) == 0)
    def _build_twiddles():
        cb, sb = cb_ref[...], sb_ref[...]
        for a in range(ca_ref.shape[0]):
            ca, sa = ca_ref[a:a + 1, :], sa_ref[a:a + 1, :]
            rows = slice(a * DFT_SPLIT, (a + 1) * DFT_SPLIT)
            tc_ref[rows, :] = (ca * cb - sa * sb).astype(BF16)
            ts_ref[rows, :] = (sa * cb + ca * sb).astype(BF16)

    acc = _dot(tc_ref[...], uc_ref[...]) - _dot(ts_ref[...], us_ref[...])
    o_ref[...] = (acc * scale).astype(o_ref.dtype)


def _fnet(tables, uc, us, tm, tn):
    seq, width = uc.shape
    scale = float((seq * FNET_GROUP_DIM) ** -0.5)
    n_a = tm // DFT_SPLIT
    part_a = pl.BlockSpec((n_a, seq), lambda i, j: (i, 0))
    part_b = pl.BlockSpec((DFT_SPLIT, seq), lambda i, j: (0, 0))
    data = pl.BlockSpec((seq, tn), lambda i, j: (0, j))
    return pl.pallas_call(
        functools.partial(_fnet_kernel, scale=scale),
        grid=(seq // tm, width // tn),
        in_specs=[part_a, part_a, part_b, part_b, data, data],
        out_specs=pl.BlockSpec((tm, tn), lambda i, j: (i, j)),
        out_shape=jax.ShapeDtypeStruct((seq, width), BF16),
        scratch_shapes=[pltpu.VMEM((tm, seq), BF16), pltpu.VMEM((tm, seq), BF16)],
        compiler_params=_params(2),
        name="fnet",
    )(*tables, uc, us)


def _post_body(x_ref, mla_ref, fn_ref, gqa_ref, o_ref, mod_ref, g2_ref, wo_ref, wg_ref, wu_ref, wd_ref, gf_ref, final):
    fn = fn_ref[...]
    if fn.shape[1] > FNET_WIDTH:
        fn = jnp.concatenate([fn[:, j:j + FNET_WIDTH] for j in range(0, fn.shape[1], FNET_WIDTH)], axis=0)
    mix = jnp.concatenate([mla_ref[...], fn, gqa_ref[...]], axis=-1)
    gate1 = mod_ref[:, 2 * D_MODEL:3 * D_MODEL]
    shift2 = mod_ref[:, 3 * D_MODEL:4 * D_MODEL]
    scale2 = mod_ref[:, 4 * D_MODEL:5 * D_MODEL]
    gate2 = mod_ref[:, 5 * D_MODEL:6 * D_MODEL]
    x = x_ref[...] + gate1 * _dot(mix, wo_ref[...])
    h = (_rms(x) * g2_ref[...] * (1.0 + scale2) + shift2).astype(BF16)
    g = _dot(h, wg_ref[...])
    u = _dot(h, wu_ref[...])
    a = (g * jax.nn.sigmoid(g) * u).astype(BF16)
    x = x + gate2 * _dot(a, wd_ref[...])
    if final:
        x = _rms(x) * gf_ref[...]
    o_ref[...] = x


def _post_kernel(*refs, final, n_first, with_next_mod):
    first, second, shared = refs[0:4], refs[4:8], refs[8:15]
    n_in = 18 if with_next_mod else 15
    o_first, o_second = refs[n_in:n_in + 2]
    step = pl.program_id(0)

    def run(group, o_ref):
        if with_next_mod:
            _ada_kernel(*refs[15:18], refs[n_in + 2])
        _post_body(*group, o_ref, *shared, final)

    @pl.when(step < n_first)
    def _first_group():
        run(first, o_first)

    @pl.when(step >= n_first)
    def _second_group():
        run(second, o_second)


def _post(groups, mod, g2, ffn, layer, g_final, *, final, tm, next_mod=None):
    (x1, _, _, _, seq1), (x2, _, _, _, seq2) = groups
    n1, n2 = x1.shape[0] // tm, x2.shape[0] // tm
    tiles_per_mod = n2 // (mod.shape[0] - 1)
    tile1 = lambda i: jnp.minimum(i, n1 - 1)
    tile2 = lambda i: jnp.maximum(i - n1, 0)
    resident = lambda arr: pl.BlockSpec(arr.shape, lambda i: (0, 0), pipeline_mode=pl.Buffered(1))

    def group_specs(seq, tile):
        tok = lambda w: pl.BlockSpec((tm, w), lambda i: (tile(i), 0))
        if tm <= seq:
            tps = seq // tm
            fnet_spec = pl.BlockSpec((tm, FNET_WIDTH), lambda i: (tile(i) % tps, tile(i) // tps))
        else:
            fnet_spec = pl.BlockSpec((seq, (tm // seq) * FNET_WIDTH), lambda i: (0, tile(i)))
        return [tok(D_MODEL), tok(MIX_TILE), fnet_spec, tok(MIX_TILE)], tok(D_MODEL)

    in1, out1 = group_specs(seq1, tile1)
    in2, out2 = group_specs(seq2, tile2)
    mod_spec = pl.BlockSpec((None, 1, 6 * D_MODEL),
                            lambda i: (jnp.where(i < n1, 0, 1 + tile2(i) // tiles_per_mod), 0, 0))
    in_specs = (in1 + in2 + [mod_spec, _layer_spec(g2, layer)] + [resident(w) for w in ffn]
                + [pl.BlockSpec((1, D_MODEL), lambda i: (0, 0))])
    args = [*groups[0][:4], *groups[1][:4], mod, g2, *ffn, g_final]
    out_specs = [out1, out2]
    out_shape = [jax.ShapeDtypeStruct(x1.shape, F32), jax.ShapeDtypeStruct(x2.shape, F32)]
    if next_mod is not None:
        cond_t, w_ada, b_ada = next_mod
        ada_in, ada_out, ada_shape = _ada_specs(cond_t, w_ada, layer + 1, w_ada.shape[2] // (n1 + n2))
        in_specs += ada_in
        args += [cond_t, w_ada, b_ada]
        out_specs.append(ada_out)
        out_shape.append(ada_shape)
    return pl.pallas_call(
        functools.partial(_post_kernel, final=final, n_first=n1, with_next_mod=next_mod is not None),
        grid=(n1 + n2,),
        in_specs=in_specs,
        out_specs=out_specs,
        out_shape=out_shape,
        compiler_params=_params(1),
        name="post_final" if final else "post",
    )(*args)


def _prep_weights(g_norm1, g_norm2, g_q_a, w_q_up, g_kv_a, w_kv_up, g_q_head, g_k_head):
    row = lambda g: g.reshape(DEPTH, 1, -1)
    wq = w_q_up.reshape(DEPTH, MLA_Q_RANK, MLA_HEADS, MLA_QK_DIM)
    wq = jnp.concatenate([wq[..., MLA_NOPE_DIM:], wq[..., :MLA_NOPE_DIM]], axis=-1)
    wq = jnp.pad(wq, ((0, 0), (0, 0), (0, 0), (0, LANES - MLA_QK_DIM)))
    wkv = w_kv_up.reshape(DEPTH, MLA_KV_RANK, MLA_HEADS, MLA_NOPE_DIM + MLA_V_DIM)
    wk = jnp.pad(wkv[..., :MLA_NOPE_DIM], ((0, 0), (0, 0), (0, 0), (MLA_ROPE_DIM, LANES - MLA_QK_DIM)))
    wv = wkv[..., MLA_NOPE_DIM:]
    return {
        "g1": row(g_norm1), "g2": row(g_norm2), "g_q_a": row(g_q_a), "g_kv_a": row(g_kv_a),
        "g_q_head": row(jnp.tile(g_q_head, (1, GQA_HEADS))), "g_k_head": row(jnp.tile(g_k_head, (1, GQA_KV_HEADS))),
        "wq": wq.reshape(DEPTH, MLA_Q_RANK, MLA_PAD_WIDTH).astype(BF16),
        "wk": wk.reshape(DEPTH, MLA_KV_RANK, MLA_PAD_WIDTH).astype(BF16),
        "wv": wv.reshape(DEPTH, MLA_KV_RANK, MIX_TILE).astype(BF16),
    }


def kernel(x_prompt, x_sample, cache_mla_ckv, cache_mla_krope, cache_gqa_k, cache_gqa_v, c, c_ctx, w_ada, b_ada,
           g_norm1, g_norm2, w_in, g_q_a, w_q_up, g_kv_a, w_kv_up, g_q_head, g_k_head, w_out,
           w_ffn_gate, w_ffn_up, w_ffn_down, g_final):
    n_pb, p_seq, _ = x_prompt.shape
    n_sb, s_seq, _ = x_sample.shape
    consts = {"wdft": _channel_dft(), "mean_q": _group_mean_matrix(GQA_HEADS),
              "mean_k": _group_mean_matrix(GQA_KV_HEADS), "place": _rope_place_matrix(),
              "rope": _rope_tables(s_seq), "seq_dft": _seq_dft_matrix(p_seq)}
    dft_s = _seq_dft_tables(s_seq)
    gf = g_final.reshape(1, D_MODEL)
    wts = _prep_weights(g_norm1, g_norm2, g_q_a, w_q_up, g_kv_a, w_kv_up, g_q_head, g_k_head)
    w_in_t = jnp.swapaxes(w_in, 1, 2)

    cond_t = jnp.concatenate([c_ctx[None, :], c], axis=0).T
    b_ada3 = b_ada.reshape(DEPTH, 1, 6 * D_MODEL)
    mod = None

    xp = x_prompt.reshape(n_pb * p_seq, D_MODEL)
    xs = x_sample.reshape(n_sb * s_seq, D_MODEL)
    b3 = lambda a, nb: a.reshape(nb, a.shape[0] // nb, a.shape[1])
    caches = None
    for l in range(DEPTH):
        final = l == DEPTH - 1
        km_c, vm_c, gk_c, gv_c, w_a, w_c, w_kr, *first_mod = _ctx_prep(
            l, cache_mla_ckv, cache_mla_krope, cache_gqa_k, cache_gqa_v, w_in_t, wts, consts,
            this_mod=(cond_t, w_ada, b_ada3) if mod is None else None)
        win_parts = {"w_a": w_a, "w_c": w_c, "w_kr": w_kr}
        if first_mod:
            mod = first_mod[0]
        mod_p = mod[0:1].reshape(1, 1, 6 * D_MODEL)
        mod_s = mod[1:].reshape(n_sb, 1, 6 * D_MODEL)

        qm, km, vm, uc_s, us_s, gq, gk, gv = _proj(xs, mod_s, wts, win_parts, l, consts, seq=s_seq, rope=True)
        mla_s, gqa_s, *ffn = _attention(b3(qm, n_sb), [(km_c, vm_c), (b3(km, n_sb), vm)],
                                        b3(gq, n_sb), [(gk_c, gv_c), (b3(gk, n_sb), gv)], tq=LATENT_Q_TILE,
                                        ahead=QK_AHEAD_LATENT,
                                        cast=(w_out, w_ffn_gate, w_ffn_up, w_ffn_down), layer=l)

        qm, km, vm, fn_p, gq, gk, gv, *caches = _proj(
            xp, mod_p, wts, win_parts, l, consts, seq=p_seq, rope=False, emit_cache=True, prev_cache=caches)
        mla_o, gqa_o = _attention(b3(qm, n_pb), [(b3(km, n_pb), vm)], b3(gq, n_pb), [(b3(gk, n_pb), gv)],
                                  tq=p_seq, bb=CONTEXT_SEQS_PER_STEP, ahead=QK_AHEAD_CONTEXT)
        fn_s = _fnet(dft_s, uc_s, us_s, tm=FNET_ROW_TILE, tn=n_sb * FNET_WIDTH)
        flat = lambda a: a.reshape(-1, MIX_TILE)
        xp, xs, *mod_next = _post(
            [(xp, flat(mla_o), fn_p, flat(gqa_o), p_seq), (xs, flat(mla_s), fn_s, flat(gqa_s), s_seq)],
            mod.reshape(1 + n_sb, 1, 6 * D_MODEL), wts["g2"], ffn, l, gf, final=final, tm=POST_TILE,
            next_mod=None if final else (cond_t, w_ada, b_ada3))
        if mod_next:
            mod = mod_next[0]

    ckv_new, krope_t, gk_t, gv_t = caches
    heads = lambda a: jnp.swapaxes(a, 2, 3).reshape(n_pb, DEPTH, p_seq, GQA_KV_HEADS, GQA_HEAD_DIM)
    return (xp.reshape(n_pb, p_seq, D_MODEL), xs.reshape(n_sb, s_seq, D_MODEL),
            ckv_new, jnp.swapaxes(krope_t, 2, 3), heads(gk_t), heads(gv_t))
```

```python
import functools

import numpy as np
import jax
import jax.numpy as jnp
from jax import lax
from jax.experimental import pallas as pl
from jax.experimental.pallas import tpu as pltpu

D_MODEL = 1024
DEPTH = 2
GRID_W = 64
MLA_HEADS = 6
MLA_Q_RANK = 384
MLA_KV_RANK = 256
MLA_NOPE_DIM = 64
MLA_ROPE_DIM = 32
MLA_V_DIM = 64
MLA_QK_DIM = MLA_NOPE_DIM + MLA_ROPE_DIM
FNET_GROUPS = 4
FNET_GROUP_DIM = 64
FNET_WIDTH = FNET_GROUPS * FNET_GROUP_DIM
GQA_HEADS = 6
GQA_KV_HEADS = 2
GQA_HEAD_DIM = 64
GQA_GROUP = GQA_HEADS // GQA_KV_HEADS
GQA_WIDTH = GQA_HEADS * GQA_HEAD_DIM
GQA_KV_WIDTH = GQA_KV_HEADS * GQA_HEAD_DIM
D_FF = 2816
ROPE_THETA = 10000.0
EPS = 1e-6
LOG2_E = 1.4426950408889634

LANES = 128
HALF = LANES // 2
HEAD_PAIRS = MLA_HEADS // 2
MLA_PAD_WIDTH = MLA_HEADS * LANES
MIX_TILE = HEAD_PAIRS * LANES
MIB = 1024 * 1024
VMEM_LIMIT_MIB = 52
POST_TILE = 512
PROJ_TILE = 512
PROJ_SUB = 256
LATENT_Q_TILE = 512
CONTEXT_SEQS_PER_STEP = 4
QK_AHEAD_LATENT = 2
QK_AHEAD_CONTEXT = 8
FNET_ROW_TILE = 512
ADA_COL_TILE = 1024


def _params(n_grid_dims):
    return pltpu.CompilerParams(dimension_semantics=("arbitrary",) * n_grid_dims,
                                vmem_limit_bytes=VMEM_LIMIT_MIB * MIB)


_IN_A = MLA_Q_RANK + MLA_KV_RANK
_IN_C0 = _IN_A + MLA_ROPE_DIM
_IN_C = FNET_WIDTH + GQA_WIDTH + 2 * GQA_KV_WIDTH
_C_U, _C_GQ, _C_GK, _C_GV = 0, FNET_WIDTH, FNET_WIDTH + GQA_WIDTH, FNET_WIDTH + GQA_WIDTH + GQA_KV_WIDTH

BF16 = jnp.bfloat16
F32 = jnp.float32


def _rope_tables(n_tokens):
    t = np.arange(n_tokens)
    row = (t // GRID_W).astype(np.float64)
    col = (t % GRID_W).astype(np.float64)

    def angles(rot_dim):
        n_axis = rot_dim // 4
        inv = ROPE_THETA ** (-np.arange(n_axis, dtype=np.float64) / n_axis)
        return np.concatenate([row[:, None] * inv, col[:, None] * inv], axis=-1)

    def tables(ang, lane_to_pair, is_first, is_second):
        cos = np.where((is_first | is_second)[None, :], np.cos(ang)[:, lane_to_pair], 1.0)
        sin = np.sin(ang)[:, lane_to_pair]
        sin_a = np.where(is_second[None, :], sin, 0.0)
        sin_b = np.where(is_first[None, :], -sin, 0.0)
        return [cos, sin_a, sin_b]

    lane = np.arange(LANES)
    half_m = MLA_ROPE_DIM // 2
    in_rope = lane < MLA_ROPE_DIM
    first_m = lane < half_m
    second_m = in_rope & ~first_m
    pair_m = lane % half_m
    half_g = GQA_HEAD_DIM // 2
    first_g = (lane % GQA_HEAD_DIM) < half_g
    pair_g = lane % half_g
    tabs = (tables(angles(MLA_ROPE_DIM), pair_m, first_m, second_m)
            + tables(angles(GQA_HEAD_DIM), pair_g, first_g, ~first_g))
    return jnp.asarray(np.concatenate(tabs, axis=-1), dtype=F32)


def _channel_dft():
    c = np.arange(FNET_GROUP_DIM)
    ang = 2.0 * np.pi * np.outer(c, c) / FNET_GROUP_DIM
    eye = np.eye(FNET_GROUPS)
    table = np.concatenate([np.kron(eye, np.cos(ang)), np.kron(eye, np.sin(ang))], axis=1)
    return jnp.asarray(table, dtype=F32).astype(BF16)


def _group_mean_matrix(n_heads):
    return jnp.asarray(np.kron(np.eye(n_heads), np.full((GQA_HEAD_DIM, GQA_HEAD_DIM), 1.0 / GQA_HEAD_DIM)), dtype=BF16)


def _rope_place_matrix():
    return jnp.asarray(np.eye(MLA_ROPE_DIM, LANES), dtype=BF16)


DFT_SPLIT = 32


def _seq_dft_matrix(seq):
    s = np.arange(seq)
    ang = 2.0 * np.pi * ((s[:, None] * s[None, :]) % seq) / seq
    return jnp.asarray(np.stack([np.cos(ang), np.sin(ang)]), dtype=F32).astype(BF16)


def _seq_dft_tables(seq):
    s = np.arange(seq)
    n_a = seq // DFT_SPLIT
    ang_a = 2.0 * np.pi * ((np.arange(n_a)[:, None] * s[None, :]) % n_a) / n_a
    ang_b = 2.0 * np.pi * ((np.arange(DFT_SPLIT)[:, None] * s[None, :]) % seq) / seq
    return tuple(jnp.asarray(t, dtype=F32) for t in (np.cos(ang_a), np.sin(ang_a), np.cos(ang_b), np.sin(ang_b)))


def _rms(x):
    return x * lax.rsqrt(jnp.mean(x * x, axis=-1, keepdims=True) + EPS)


def _dot(a, b):
    return jnp.dot(a, b, preferred_element_type=F32)


def _group_rms(x, mean_mat):
    ms = _dot((x * x).astype(BF16), mean_mat)
    return x * lax.rsqrt(ms + EPS)


def _rotate(x, cos, sin_a, sin_b, half):
    return x * cos + pltpu.roll(x, half, 1) * sin_a + pltpu.roll(x, LANES - half, 1) * sin_b


def _lower_half(rows):
    return lax.broadcasted_iota(jnp.int32, (rows, LANES), 1) < HALF


def _layer_spec(arr, layer):
    return pl.BlockSpec((None,) + arr.shape[1:], lambda i: (layer, 0, 0))


def _ada_kernel(ct_ref, w_ref, b_ref, o_ref):
    s = ct_ref[...]
    s = s * jax.nn.sigmoid(s)
    w = w_ref[...]
    for m in range(o_ref.shape[0]):
        o_ref[m:m + 1, :] = jnp.sum(w * s[:, m:m + 1], axis=0, keepdims=True) + b_ref[...]


def _ada_specs(cond_t, w_ada, layer, tn):
    n_cond = cond_t.shape[1]
    in_specs = [pl.BlockSpec((D_MODEL, n_cond), lambda j: (0, 0)),
                pl.BlockSpec((None, D_MODEL, tn), lambda j: (layer, 0, j)),
                pl.BlockSpec((None, 1, tn), lambda j: (layer, 0, j))]
    out_shape = jax.ShapeDtypeStruct((n_cond, w_ada.shape[2]), F32)
    return in_specs, pl.BlockSpec((n_cond, tn), lambda j: (0, j)), out_shape


def _ada(cond_t, w_ada, b_ada, layer, tn=ADA_COL_TILE):
    in_specs, out_spec, out_shape = _ada_specs(cond_t, w_ada, layer, tn)
    return pl.pallas_call(
        _ada_kernel,
        grid=(w_ada.shape[2] // tn,),
        in_specs=in_specs,
        out_specs=out_spec,
        out_shape=out_shape,
        compiler_params=_params(1),
        name="ada",
    )(cond_t, w_ada, b_ada)


_PROJ_WEIGHTS = ("g1", "w_a", "w_c", "g_q_a", "wq", "g_kv_a", "wk", "wv", "g_q_head", "g_k_head")


def _proj_kernel(*refs, rope, emit_cache, n_prev, multi_seq):
    (x_ref, mod_ref, g1_ref, wa_ref, wc_ref, gqa_ref, wq_ref, gkva_ref, wk_ref, wv_ref,
     gqh_ref, gkh_ref, wdft_ref, mq_ref, mk_ref) = refs[:15]
    refs = refs[15:]
    if multi_seq:
        sdft_ref, refs = refs[0], refs[1:]
    if rope:
        tab_ref, refs = refs[0], refs[1:]
    if n_prev:
        prev_refs, refs = refs[:4], refs[4:]
    n_fourier = 1 if multi_seq else 2
    qm_ref, km_ref, vm_ref = refs[:3]
    fourier_refs = refs[3:3 + n_fourier]
    gq_ref, gk_ref, gv_ref = refs[3 + n_fourier:6 + n_fourier]
    cache_refs = refs[6 + n_fourier:]

    shift = mod_ref[:, 0:D_MODEL]
    scale = mod_ref[:, D_MODEL:2 * D_MODEL]
    lower = _lower_half(PROJ_SUB)

    for r in range(x_ref.shape[0] // PROJ_SUB):
        rows = slice(r * PROJ_SUB, (r + 1) * PROJ_SUB)
        seq_cols = (slice(None), slice(r * FNET_WIDTH, (r + 1) * FNET_WIDTH)) if multi_seq else (rows, slice(None))
        t_idx = (r,) if multi_seq else (slice(None), rows)

        hb = (_rms(x_ref[rows, :]) * g1_ref[...] * (1.0 + scale) + shift).astype(BF16)
        pa = _dot(hb, wa_ref[...])
        pc = _dot(hb, wc_ref[...])
        kr = pa[:, _IN_A:]

        if rope:
            cos_m, sa_m, sb_m = (tab_ref[rows, i * LANES:(i + 1) * LANES] for i in range(3))
            cos_g, sa_g, sb_g = (tab_ref[rows, i * LANES:(i + 1) * LANES] for i in range(3, 6))

        gqn = _group_rms(pc[:, _C_GQ:_C_GQ + GQA_WIDTH], mq_ref[...]) * gqh_ref[...]
        tiles = []
        for p in range(HEAD_PAIRS):
            gp = gqn[:, p * LANES:(p + 1) * LANES]
            if rope:
                gp = _rotate(gp, cos_g, sa_g, sb_g, GQA_HEAD_DIM // 2)
            tiles.append(gp * (LOG2_E * GQA_HEAD_DIM ** -0.5))
        gq_ref[rows, 0:LANES] = jnp.where(lower, tiles[0], tiles[1]).astype(BF16)
        gq_ref[rows, LANES:2 * LANES] = jnp.where(
            lower, pltpu.roll(tiles[0], HALF, 1), pltpu.roll(tiles[2], HALF, 1)).astype(BF16)
        gq_ref[rows, 2 * LANES:] = jnp.where(lower, tiles[1], tiles[2]).astype(BF16)
        gkn = _group_rms(pc[:, _C_GK:_C_GK + GQA_KV_WIDTH], mk_ref[...]) * gkh_ref[...]
        gkr = _rotate(gkn, cos_g, sa_g, sb_g, GQA_HEAD_DIM // 2) if rope else gkn
        gv_t = pc[:, _C_GV:_C_GV + GQA_KV_WIDTH].T
        gk_ref[rows, :] = gkr.astype(BF16)
        gv_ref[t_idx] = gv_t.astype(BF16)

        cqn = _rms(pa[:, :MLA_Q_RANK]) * gqa_ref[...]
        q = _dot(cqn.astype(BF16), wq_ref[...])
        for hd in range(MLA_HEADS):
            qh = q[:, hd * LANES:(hd + 1) * LANES]
            if rope:
                qh = _rotate(qh, cos_m, sa_m, sb_m, MLA_ROPE_DIM // 2)
            qm_ref[rows, hd * LANES:(hd + 1) * LANES] = (qh * (LOG2_E * MLA_QK_DIM ** -0.5)).astype(BF16)

        ckvn = _rms(pa[:, MLA_Q_RANK:_IN_A]) * gkva_ref[...]
        cb = ckvn.astype(BF16)
        kn = _dot(cb, wk_ref[...])
        krr = _rotate(kr, cos_m, sa_m, sb_m, MLA_ROPE_DIM // 2) if rope else kr
        for hd in range(MLA_HEADS):
            km_ref[rows, hd * LANES:(hd + 1) * LANES] = (kn[:, hd * LANES:(hd + 1) * LANES] + krr).astype(BF16)
        vm_ref[t_idx] = _dot(cb, wv_ref[...]).T.astype(BF16)

        ucs = _dot(pc[:, _C_U:_C_U + FNET_WIDTH].astype(BF16), wdft_ref[...])
        uc = ucs[:, :FNET_WIDTH].astype(BF16)
        us = ucs[:, FNET_WIDTH:].astype(BF16)
        if multi_seq:
            fn = _dot(sdft_ref[0], uc) - _dot(sdft_ref[1], us)
            fourier_refs[0][seq_cols] = (fn * (PROJ_SUB * FNET_GROUP_DIM) ** -0.5).astype(BF16)
        else:
            fourier_refs[0][seq_cols] = uc
            fourier_refs[1][seq_cols] = us

        if emit_cache:
            new = (ckvn, kr.T[:MLA_ROPE_DIM, :], gkn.T, gv_t)
            for i, out_ref in enumerate(cache_refs):
                if n_prev:
                    out_ref[r, :n_prev] = prev_refs[i][r]
                out_ref[r, n_prev] = new[i]


def _proj(x, mod, wts, win_parts, layer, consts, *, seq, rope, prev_cache=None, emit_cache=False, tm=PROJ_TILE):
    n_tok = x.shape[0]
    n_tiles = n_tok // tm
    n_seq = n_tok // seq
    tiles_per_mod = n_tiles // mod.shape[0]
    multi_seq = tm > seq
    if multi_seq:
        assert seq == PROJ_SUB and not rope
        spt = tm // seq
        seq_major = pl.BlockSpec((seq, spt * FNET_WIDTH), lambda i: (0, i))
        transposed = lambda w: pl.BlockSpec((spt, w, seq), lambda i: (i, 0, 0))
    else:
        assert not emit_cache
        tps = seq // tm
        seq_major = pl.BlockSpec((tm, FNET_WIDTH), lambda i: (i % tps, i // tps))
        transposed = lambda w: pl.BlockSpec((None, w, tm), lambda i: (i // tps, 0, i % tps))
    const = lambda arr: pl.BlockSpec(arr.shape, lambda i: (0,) * arr.ndim)
    tok = lambda w: pl.BlockSpec((tm, w), lambda i: (i, 0))
    in_specs = [tok(D_MODEL), pl.BlockSpec((None, 1, 6 * D_MODEL), lambda i: (i // tiles_per_mod, 0, 0))]
    args = [x, mod]
    for name in _PROJ_WEIGHTS:
        arr = win_parts.get(name)
        in_specs.append(const(arr) if arr is not None else _layer_spec(wts[name], layer))
        args.append(arr if arr is not None else wts[name])
    for name in ("wdft", "mean_q", "mean_k"):
        in_specs.append(const(consts[name]))
        args.append(consts[name])
    if multi_seq:
        in_specs.append(const(consts["seq_dft"]))
        args.append(consts["seq_dft"])
    if rope:
        in_specs.append(pl.BlockSpec((tm, 6 * LANES), lambda i: (i % tps, 0)))
        args.append(consts["rope"])
    n_fourier = 1 if multi_seq else 2
    out_specs = [tok(MLA_PAD_WIDTH), tok(MLA_PAD_WIDTH), transposed(MIX_TILE)] + [seq_major] * n_fourier + [
                 tok(GQA_WIDTH), tok(GQA_KV_WIDTH), transposed(GQA_KV_WIDTH)]
    out_shape = [jax.ShapeDtypeStruct((n_tok, MLA_PAD_WIDTH), BF16),
                 jax.ShapeDtypeStruct((n_tok, MLA_PAD_WIDTH), BF16),
                 jax.ShapeDtypeStruct((n_seq, MIX_TILE, seq), BF16)] + [
                 jax.ShapeDtypeStruct((seq, n_seq * FNET_WIDTH), BF16)] * n_fourier + [
                 jax.ShapeDtypeStruct((n_tok, GQA_WIDTH), BF16),
                 jax.ShapeDtypeStruct((n_tok, GQA_KV_WIDTH), BF16),
                 jax.ShapeDtypeStruct((n_seq, GQA_KV_WIDTH, seq), BF16)]
    n_prev = 0
    if emit_cache:
        n_prev = prev_cache[0].shape[1] if prev_cache is not None else 0
        layers = lambda n, tail: pl.BlockSpec((spt, n) + tail, lambda i: (i, 0, 0, 0))
        tails = ((seq, MLA_KV_RANK), (MLA_ROPE_DIM, seq), (GQA_KV_WIDTH, seq), (GQA_KV_WIDTH, seq))
        if n_prev:
            in_specs += [layers(n_prev, t) for t in tails]
            args += list(prev_cache)
        out_specs += [layers(n_prev + 1, t) for t in tails]
        out_shape += [jax.ShapeDtypeStruct((n_seq, n_prev + 1) + t, F32) for t in tails]
    return pl.pallas_call(
        functools.partial(_proj_kernel, rope=rope, emit_cache=emit_cache, n_prev=n_prev, multi_seq=multi_seq),
        grid=(n_tiles,),
        in_specs=in_specs,
        out_specs=out_specs,
        out_shape=out_shape,
        compiler_params=_params(1),
        name="proj_rope" if rope else "proj_ctx",
    )(*args)


def _ctx_kernel(ckv_ref, kr_ref, gk_ref, gv_ref, wk_ref, wv_ref, place_ref, win_ref,
                km_ref, vm_ref, gko_ref, gvo_ref, wa_ref, wc_ref):
    wt = win_ref[...]
    kr_rows = jnp.concatenate([wt[_IN_A:_IN_C0, :], jnp.zeros((LANES - MLA_ROPE_DIM, wt.shape[1]), F32)], axis=0)
    wa_ref[:, :_IN_A] = wt[:_IN_A, :].T.astype(BF16)
    wa_ref[:, _IN_A:] = kr_rows.T.astype(BF16)
    wc_ref[...] = wt[_IN_C0:, :].T.astype(BF16)

    cb = ckv_ref[...].astype(BF16)
    kn = _dot(cb, wk_ref[...])
    kr = _dot(kr_ref[...].astype(BF16), place_ref[...])
    for hd in range(MLA_HEADS):
        km_ref[:, hd * LANES:(hd + 1) * LANES] = (kn[:, hd * LANES:(hd + 1) * LANES] + kr).astype(BF16)
    vm_ref[...] = _dot(cb, wv_ref[...]).T.astype(BF16)
    gko_ref[...] = gk_ref[...].astype(BF16)
    gvo_ref[...] = gv_ref[...].T.astype(BF16)


def _ctx_prep(layer, cache_ckv, cache_krope, cache_gk, cache_gv, w_in_t, wts, consts):
    nb, _, past, _ = cache_ckv.shape
    rows = D_MODEL // nb
    cache = lambda w: pl.BlockSpec((None, None, past, w), lambda b: (b, layer, 0, 0))
    out = lambda w: pl.BlockSpec((None, past, w), lambda b: (b, 0, 0))
    out_t = lambda w: pl.BlockSpec((None, w, past), lambda b: (b, 0, 0))
    part = lambda w: pl.BlockSpec((rows, w), lambda b: (b, 0))
    part_widths = (_IN_A + LANES, _IN_C)
    return pl.pallas_call(
        _ctx_kernel,
        grid=(nb,),
        in_specs=[cache(MLA_KV_RANK), cache(MLA_ROPE_DIM), cache(GQA_KV_WIDTH), cache(GQA_KV_WIDTH),
                  _layer_spec(wts["wk"], layer), _layer_spec(wts["wv"], layer),
                  pl.BlockSpec((MLA_ROPE_DIM, LANES), lambda b: (0, 0)),
                  pl.BlockSpec((None, w_in_t.shape[1], rows), lambda b: (layer, 0, b))],
        out_specs=[out(MLA_PAD_WIDTH), out_t(MIX_TILE), out(GQA_KV_WIDTH), out_t(GQA_KV_WIDTH)]
                  + [part(w) for w in part_widths],
        out_shape=[jax.ShapeDtypeStruct((nb, past, MLA_PAD_WIDTH), BF16),
                   jax.ShapeDtypeStruct((nb, MIX_TILE, past), BF16),
                   jax.ShapeDtypeStruct((nb, past, GQA_KV_WIDTH), BF16),
                   jax.ShapeDtypeStruct((nb, GQA_KV_WIDTH, past), BF16)]
                  + [jax.ShapeDtypeStruct((D_MODEL, w), BF16) for w in part_widths],
        compiler_params=_params(1),
        name="ctx_prep",
    )(cache_ckv, cache_krope, cache_gk.reshape(nb, DEPTH, past, GQA_KV_WIDTH),
      cache_gv.reshape(nb, DEPTH, past, GQA_KV_WIDTH), wts["wk"], wts["wv"], consts["place"], w_in_t)


def _attn_kernel(*refs, n_seg, ahead, n_cast):
    n_in = 1 + 2 * n_seg
    outs_at = 2 * n_in + n_cast
    families = []
    for f, mla in enumerate((True, False)):
        ins = refs[f * n_in:(f + 1) * n_in]
        families.append((mla, ins[0], ins[1:1 + n_seg], ins[1 + n_seg:], refs[outs_at + f]))
    for c in range(n_cast):
        refs[outs_at + 2 + c][...] = refs[2 * n_in + c][...].astype(BF16)
    n_seqs, tq = refs[0].shape[:2]
    lower = _lower_half(tq)
    top = lax.broadcasted_iota(jnp.int32, (LANES, tq), 0) < HALF

    def scores_t(f, b, p, hd):
        mla, q_ref, k_refs, _, _ = families[f]
        if mla:
            cols = slice((2 * p + hd) * LANES, (2 * p + hd + 1) * LANES)
            qh = q_ref[b, :, cols]
            keys = [k_ref[b, :, cols] for k_ref in k_refs]
        else:
            q = q_ref[b, :, p * LANES:(p + 1) * LANES]
            qh = jnp.where(lower if hd == 0 else ~lower, q, jnp.zeros_like(q))
            keys = [k_ref[b] for k_ref in k_refs]
        return [lax.dot_general(k, qh, (((1,), (1,)), ((), ())), preferred_element_type=F32) for k in keys]

    def softmax_t(ss):
        m = functools.reduce(jnp.maximum, [jnp.max(s, axis=0, keepdims=True) for s in ss])
        es = [jnp.exp2(s - m) for s in ss]
        denom = functools.reduce(jnp.add, [jnp.sum(e, axis=0, keepdims=True) for e in es])
        return [e.astype(BF16) for e in es], denom

    def weighted_values_t(f, b, p, es, denom):
        mla, _, _, vt_refs, _ = families[f]
        rows = slice(p * LANES, (p + 1) * LANES) if mla else slice(None)
        acc = functools.reduce(jnp.add, [_dot(vt_ref[b, rows, :], e) for e, vt_ref in zip(es, vt_refs)])
        return acc * (1.0 / denom)

    work = [(f, b, p, hd) for f in range(2) for b in range(n_seqs) for p in range(HEAD_PAIRS) for hd in range(2)]
    pending = [scores_t(*w) for w in work[:ahead]]
    outs = {}
    for i, (f, b, p, hd) in enumerate(work):
        es, denom = softmax_t(pending.pop(0))
        if i + ahead < len(work):
            pending.append(scores_t(*work[i + ahead]))
        outs[f, b, p, hd] = weighted_values_t(f, b, p, es, denom)
        if (p, hd) != (HEAD_PAIRS - 1, 1):
            continue
        mla, o_ref = families[f][0], families[f][4]
        tiles = [jnp.where(top, outs[f, b, t, 0], outs[f, b, t, 1]).T for t in range(HEAD_PAIRS)]
        if not mla:
            t1 = pltpu.roll(tiles[1], HALF, 1)
            tiles = [jnp.where(lower, tiles[0], t1), jnp.where(lower, tiles[2], tiles[0]),
                     jnp.where(lower, t1, tiles[2])]
        for t in range(HEAD_PAIRS):
            o_ref[b, :, t * LANES:(t + 1) * LANES] = tiles[t].astype(o_ref.dtype)


def _attention(mla_q, mla_kv, gqa_q, gqa_kv, *, tq, ahead, bb=1, cast=(), layer=0):
    n_batch, sq = mla_q.shape[:2]
    n_seg = len(mla_kv)
    n_q = sq // tq
    steps = (n_batch // bb) * n_q
    whole = lambda a: pl.BlockSpec((bb,) + a.shape[1:], lambda b, i: (b, 0, 0))
    in_specs, args = [], []
    for q, kv in ((mla_q, mla_kv), (gqa_q, gqa_kv)):
        in_specs += [pl.BlockSpec((bb, tq, q.shape[2]), lambda b, i: (b, i, 0))]
        in_specs += [whole(k) for k, _ in kv] + [whole(v) for _, v in kv]
        args += [q] + [k for k, _ in kv] + [v for _, v in kv]
    out_spec = pl.BlockSpec((bb, tq, MIX_TILE), lambda b, i: (b, i, 0))
    out_specs = [out_spec, out_spec]
    out_shape = [jax.ShapeDtypeStruct((n_batch, sq, MIX_TILE), BF16)] * 2
    for w in cast:
        rows, cols = w.shape[1] // steps, w.shape[2]
        in_specs.append(pl.BlockSpec((None, rows, cols), lambda b, i: (layer, b * n_q + i, 0)))
        out_specs.append(pl.BlockSpec((rows, cols), lambda b, i: (b * n_q + i, 0)))
        out_shape.append(jax.ShapeDtypeStruct(w.shape[1:], BF16))
        args.append(w)
    return pl.pallas_call(
        functools.partial(_attn_kernel, n_seg=n_seg, ahead=ahead, n_cast=len(cast)),
        grid=(n_batch // bb, n_q),
        in_specs=in_specs,
        out_specs=out_specs,
        out_shape=out_shape,
        compiler_params=_params(2),
        name=f"attn_s{n_seg}",
    )(*args)


def _fnet_kernel(ca_ref, sa_ref, cb_ref, sb_ref, uc_ref, us_ref, o_ref, tc_ref, ts_ref, *, scale):
    @pl.when(pl.program_id(1) == 0)
    def _build_twiddles():
        cb, sb = cb_ref[...], sb_ref[...]
        for a in range(ca_ref.shape[0]):
            ca, sa = ca_ref[a:a + 1, :], sa_ref[a:a + 1, :]
            rows = slice(a * DFT_SPLIT, (a + 1) * DFT_SPLIT)
            tc_ref[rows, :] = (ca * cb - sa * sb).astype(BF16)
            ts_ref[rows, :] = (sa * cb + ca * sb).astype(BF16)

    acc = _dot(tc_ref[...], uc_ref[...]) - _dot(ts_ref[...], us_ref[...])
    o_ref[...] = (acc * scale).astype(o_ref.dtype)


def _fnet(tables, uc, us, tm, tn):
    seq, width = uc.shape
    scale = float((seq * FNET_GROUP_DIM) ** -0.5)
    n_a = tm // DFT_SPLIT
    part_a = pl.BlockSpec((n_a, seq), lambda i, j: (i, 0))
    part_b = pl.BlockSpec((DFT_SPLIT, seq), lambda i, j: (0, 0))
    data = pl.BlockSpec((seq, tn), lambda i, j: (0, j))
    return pl.pallas_call(
        functools.partial(_fnet_kernel, scale=scale),
        grid=(seq // tm, width // tn),
        in_specs=[part_a, part_a, part_b, part_b, data, data],
        out_specs=pl.BlockSpec((tm, tn), lambda i, j: (i, j)),
        out_shape=jax.ShapeDtypeStruct((seq, width), BF16),
        scratch_shapes=[pltpu.VMEM((tm, seq), BF16), pltpu.VMEM((tm, seq), BF16)],
        compiler_params=_params(2),
        name="fnet",
    )(*tables, uc, us)


def _post_body(x_ref, mla_ref, fn_ref, gqa_ref, o_ref, mod_ref, g2_ref, wo_ref, wg_ref, wu_ref, wd_ref, gf_ref, final):
    fn = fn_ref[...]
    if fn.shape[1] > FNET_WIDTH:
        fn = jnp.concatenate([fn[:, j:j + FNET_WIDTH] for j in range(0, fn.shape[1], FNET_WIDTH)], axis=0)
    mix = jnp.concatenate([mla_ref[...], fn, gqa_ref[...]], axis=-1)
    gate1 = mod_ref[:, 2 * D_MODEL:3 * D_MODEL]
    shift2 = mod_ref[:, 3 * D_MODEL:4 * D_MODEL]
    scale2 = mod_ref[:, 4 * D_MODEL:5 * D_MODEL]
    gate2 = mod_ref[:, 5 * D_MODEL:6 * D_MODEL]
    x = x_ref[...] + gate1 * _dot(mix, wo_ref[...])
    h = (_rms(x) * g2_ref[...] * (1.0 + scale2) + shift2).astype(BF16)
    g = _dot(h, wg_ref[...])
    u = _dot(h, wu_ref[...])
    a = (g * jax.nn.sigmoid(g) * u).astype(BF16)
    x = x + gate2 * _dot(a, wd_ref[...])
    if final:
        x = _rms(x) * gf_ref[...]
    o_ref[...] = x


def _post_kernel(*refs, final, n_first, with_next_mod):
    first, second, shared = refs[0:4], refs[4:8], refs[8:15]
    n_in = 18 if with_next_mod else 15
    o_first, o_second = refs[n_in:n_in + 2]
    step = pl.program_id(0)

    def run(group, o_ref):
        if with_next_mod:
            _ada_kernel(*refs[15:18], refs[n_in + 2])
        _post_body(*group, o_ref, *shared, final)

    @pl.when(step < n_first)
    def _first_group():
        run(first, o_first)

    @pl.when(step >= n_first)
    def _second_group():
        run(second, o_second)


def _post(groups, mod, g2, ffn, layer, g_final, *, final, tm, next_mod=None):
    (x1, _, _, _, seq1), (x2, _, _, _, seq2) = groups
    n1, n2 = x1.shape[0] // tm, x2.shape[0] // tm
    tiles_per_mod = n2 // (mod.shape[0] - 1)
    tile1 = lambda i: jnp.minimum(i, n1 - 1)
    tile2 = lambda i: jnp.maximum(i - n1, 0)
    resident = lambda arr: pl.BlockSpec(arr.shape, lambda i: (0, 0), pipeline_mode=pl.Buffered(1))

    def group_specs(seq, tile):
        tok = lambda w: pl.BlockSpec((tm, w), lambda i: (tile(i), 0))
        if tm <= seq:
            tps = seq // tm
            fnet_spec = pl.BlockSpec((tm, FNET_WIDTH), lambda i: (tile(i) % tps, tile(i) // tps))
        else:
            fnet_spec = pl.BlockSpec((seq, (tm // seq) * FNET_WIDTH), lambda i: (0, tile(i)))
        return [tok(D_MODEL), tok(MIX_TILE), fnet_spec, tok(MIX_TILE)], tok(D_MODEL)

    in1, out1 = group_specs(seq1, tile1)
    in2, out2 = group_specs(seq2, tile2)
    mod_spec = pl.BlockSpec((None, 1, 6 * D_MODEL),
                            lambda i: (jnp.where(i < n1, 0, 1 + tile2(i) // tiles_per_mod), 0, 0))
    in_specs = (in1 + in2 + [mod_spec, _layer_spec(g2, layer)] + [resident(w) for w in ffn]
                + [pl.BlockSpec((1, D_MODEL), lambda i: (0, 0))])
    args = [*groups[0][:4], *groups[1][:4], mod, g2, *ffn, g_final]
    out_specs = [out1, out2]
    out_shape = [jax.ShapeDtypeStruct(x1.shape, F32), jax.ShapeDtypeStruct(x2.shape, F32)]
    if next_mod is not None:
        cond_t, w_ada, b_ada = next_mod
        ada_in, ada_out, ada_shape = _ada_specs(cond_t, w_ada, layer + 1, w_ada.shape[2] // (n1 + n2))
        in_specs += ada_in
        args += [cond_t, w_ada, b_ada]
        out_specs.append(ada_out)
        out_shape.append(ada_shape)
    return pl.pallas_call(
        functools.partial(_post_kernel, final=final, n_first=n1, with_next_mod=next_mod is not None),
        grid=(n1 + n2,),
        in_specs=in_specs,
        out_specs=out_specs,
        out_shape=out_shape,
        compiler_params=_params(1),
        name="post_final" if final else "post",
    )(*args)


def _prep_weights(g_norm1, g_norm2, g_q_a, w_q_up, g_kv_a, w_kv_up, g_q_head, g_k_head):
    row = lambda g: g.reshape(DEPTH, 1, -1)
    wq = w_q_up.reshape(DEPTH, MLA_Q_RANK, MLA_HEADS, MLA_QK_DIM)
    wq = jnp.concatenate([wq[..., MLA_NOPE_DIM:], wq[..., :MLA_NOPE_DIM]], axis=-1)
    wq = jnp.pad(wq, ((0, 0), (0, 0), (0, 0), (0, LANES - MLA_QK_DIM)))
    wkv = w_kv_up.reshape(DEPTH, MLA_KV_RANK, MLA_HEADS, MLA_NOPE_DIM + MLA_V_DIM)
    wk = jnp.pad(wkv[..., :MLA_NOPE_DIM], ((0, 0), (0, 0), (0, 0), (MLA_ROPE_DIM, LANES - MLA_QK_DIM)))
    wv = wkv[..., MLA_NOPE_DIM:]
    return {
        "g1": row(g_norm1), "g2": row(g_norm2), "g_q_a": row(g_q_a), "g_kv_a": row(g_kv_a),
        "g_q_head": row(jnp.tile(g_q_head, (1, GQA_HEADS))), "g_k_head": row(jnp.tile(g_k_head, (1, GQA_KV_HEADS))),
        "wq": wq.reshape(DEPTH, MLA_Q_RANK, MLA_PAD_WIDTH).astype(BF16),
        "wk": wk.reshape(DEPTH, MLA_KV_RANK, MLA_PAD_WIDTH).astype(BF16),
        "wv": wv.reshape(DEPTH, MLA_KV_RANK, MIX_TILE).astype(BF16),
    }


def kernel(x_prompt, x_sample, cache_mla_ckv, cache_mla_krope, cache_gqa_k, cache_gqa_v, c, c_ctx, w_ada, b_ada,
           g_norm1, g_norm2, w_in, g_q_a, w_q_up, g_kv_a, w_kv_up, g_q_head, g_k_head, w_out,
           w_ffn_gate, w_ffn_up, w_ffn_down, g_final):
    n_pb, p_seq, _ = x_prompt.shape
    n_sb, s_seq, _ = x_sample.shape
    consts = {"wdft": _channel_dft(), "mean_q": _group_mean_matrix(GQA_HEADS),
              "mean_k": _group_mean_matrix(GQA_KV_HEADS), "place": _rope_place_matrix(),
              "rope": _rope_tables(s_seq), "seq_dft": _seq_dft_matrix(p_seq)}
    dft_s = _seq_dft_tables(s_seq)
    gf = g_final.reshape(1, D_MODEL)
    wts = _prep_weights(g_norm1, g_norm2, g_q_a, w_q_up, g_kv_a, w_kv_up, g_q_head, g_k_head)
    w_in_t = jnp.swapaxes(w_in, 1, 2)

    cond_t = jnp.concatenate([c_ctx[None, :], c], axis=0).T
    b_ada3 = b_ada.reshape(DEPTH, 1, 6 * D_MODEL)
    mod = _ada(cond_t, w_ada, b_ada3, 0)

    xp = x_prompt.reshape(n_pb * p_seq, D_MODEL)
    xs = x_sample.reshape(n_sb * s_seq, D_MODEL)
    b3 = lambda a, nb: a.reshape(nb, a.shape[0] // nb, a.shape[1])
    caches = None
    for l in range(DEPTH):
        final = l == DEPTH - 1
        mod_p = mod[0:1].reshape(1, 1, 6 * D_MODEL)
        mod_s = mod[1:].reshape(n_sb, 1, 6 * D_MODEL)

        km_c, vm_c, gk_c, gv_c, w_a, w_c = _ctx_prep(
            l, cache_mla_ckv, cache_mla_krope, cache_gqa_k, cache_gqa_v, w_in_t, wts, consts)
        win_parts = {"w_a": w_a, "w_c": w_c}
        qm, km, vm, uc_s, us_s, gq, gk, gv = _proj(xs, mod_s, wts, win_parts, l, consts, seq=s_seq, rope=True)
        mla_s, gqa_s, *ffn = _attention(b3(qm, n_sb), [(km_c, vm_c), (b3(km, n_sb), vm)],
                                        b3(gq, n_sb), [(gk_c, gv_c), (b3(gk, n_sb), gv)], tq=LATENT_Q_TILE,
                                        ahead=QK_AHEAD_LATENT,
                                        cast=(w_out, w_ffn_gate, w_ffn_up, w_ffn_down), layer=l)

        qm, km, vm, fn_p, gq, gk, gv, *caches = _proj(
            xp, mod_p, wts, win_parts, l, consts, seq=p_seq, rope=False, emit_cache=True, prev_cache=caches)
        mla_o, gqa_o = _attention(b3(qm, n_pb), [(b3(km, n_pb), vm)], b3(gq, n_pb), [(b3(gk, n_pb), gv)],
                                  tq=p_seq, bb=CONTEXT_SEQS_PER_STEP, ahead=QK_AHEAD_CONTEXT)
        fn_s = _fnet(dft_s, uc_s, us_s, tm=FNET_ROW_TILE, tn=n_sb * FNET_WIDTH)
        flat = lambda a: a.reshape(-1, MIX_TILE)
        xp, xs, *mod_next = _post(
            [(xp, flat(mla_o), fn_p, flat(gqa_o), p_seq), (xs, flat(mla_s), fn_s, flat(gqa_s), s_seq)],
            mod.reshape(1 + n_sb, 1, 6 * D_MODEL), wts["g2"], ffn, l, gf, final=final, tm=POST_TILE,
            next_mod=None if final else (cond_t, w_ada, b_ada3))
        if mod_next:
            mod = mod_next[0]

    ckv_new, krope_t, gk_t, gv_t = caches
    heads = lambda a: jnp.swapaxes(a, 2, 3).reshape(n_pb, DEPTH, p_seq, GQA_KV_HEADS, GQA_HEAD_DIM)
    return (xp.reshape(n_pb, p_seq, D_MODEL), xs.reshape(n_sb, s_seq, D_MODEL),
            ckv_new, jnp.swapaxes(krope_t, 2, 3), heads(gk_t), heads(gv_t))
```

```python
import functools

import numpy as np
import jax
import jax.numpy as jnp
from jax import lax
from jax.experimental import pallas as pl
from jax.experimental.pallas import tpu as pltpu

D_MODEL = 1024
DEPTH = 2
GRID_W = 64
MLA_HEADS = 6
MLA_Q_RANK = 384
MLA_KV_RANK = 256
MLA_NOPE_DIM = 64
MLA_ROPE_DIM = 32
MLA_V_DIM = 64
MLA_QK_DIM = MLA_NOPE_DIM + MLA_ROPE_DIM
FNET_GROUPS = 4
FNET_GROUP_DIM = 64
FNET_WIDTH = FNET_GROUPS * FNET_GROUP_DIM
GQA_HEADS = 6
GQA_KV_HEADS = 2
GQA_HEAD_DIM = 64
GQA_GROUP = GQA_HEADS // GQA_KV_HEADS
GQA_WIDTH = GQA_HEADS * GQA_HEAD_DIM
GQA_KV_WIDTH = GQA_KV_HEADS * GQA_HEAD_DIM
D_FF = 2816
ROPE_THETA = 10000.0
EPS = 1e-6
LOG2_E = 1.4426950408889634

LANES = 128
HALF = LANES // 2
HEAD_PAIRS = MLA_HEADS // 2
MLA_PAD_WIDTH = MLA_HEADS * LANES
MIX_TILE = HEAD_PAIRS * LANES
MIB = 1024 * 1024
VMEM_LIMIT_MIB = 52
POST_TILE = 512
PROJ_TILE = 512
PROJ_SUB = 256
LATENT_Q_TILE = 512
CONTEXT_SEQS_PER_STEP = 4
QK_AHEAD_LATENT = 2
QK_AHEAD_CONTEXT = 8
FNET_ROW_TILE = 512
ADA_COL_TILE = 1024


def _params(n_grid_dims):
    return pltpu.CompilerParams(dimension_semantics=("arbitrary",) * n_grid_dims,
                                vmem_limit_bytes=VMEM_LIMIT_MIB * MIB)


_IN_A = MLA_Q_RANK + MLA_KV_RANK
_IN_C0 = _IN_A + MLA_ROPE_DIM
_IN_C = FNET_WIDTH + GQA_WIDTH + 2 * GQA_KV_WIDTH
_C_U, _C_GQ, _C_GK, _C_GV = 0, FNET_WIDTH, FNET_WIDTH + GQA_WIDTH, FNET_WIDTH + GQA_WIDTH + GQA_KV_WIDTH

BF16 = jnp.bfloat16
F32 = jnp.float32


def _rope_tables(n_tokens):
    t = np.arange(n_tokens)
    row = (t // GRID_W).astype(np.float64)
    col = (t % GRID_W).astype(np.float64)

    def angles(rot_dim):
        n_axis = rot_dim // 4
        inv = ROPE_THETA ** (-np.arange(n_axis, dtype=np.float64) / n_axis)
        return np.concatenate([row[:, None] * inv, col[:, None] * inv], axis=-1)

    def tables(ang, lane_to_pair, is_first, is_second):
        cos = np.where((is_first | is_second)[None, :], np.cos(ang)[:, lane_to_pair], 1.0)
        sin = np.sin(ang)[:, lane_to_pair]
        sin_a = np.where(is_second[None, :], sin, 0.0)
        sin_b = np.where(is_first[None, :], -sin, 0.0)
        return [cos, sin_a, sin_b]

    lane = np.arange(LANES)
    half_m = MLA_ROPE_DIM // 2
    in_rope = lane < MLA_ROPE_DIM
    first_m = lane < half_m
    second_m = in_rope & ~first_m
    pair_m = lane % half_m
    half_g = GQA_HEAD_DIM // 2
    first_g = (lane % GQA_HEAD_DIM) < half_g
    pair_g = lane % half_g
    tabs = (tables(angles(MLA_ROPE_DIM), pair_m, first_m, second_m)
            + tables(angles(GQA_HEAD_DIM), pair_g, first_g, ~first_g))
    return jnp.asarray(np.concatenate(tabs, axis=-1), dtype=F32)


def _channel_dft():
    c = np.arange(FNET_GROUP_DIM)
    ang = 2.0 * np.pi * np.outer(c, c) / FNET_GROUP_DIM
    eye = np.eye(FNET_GROUPS)
    table = np.concatenate([np.kron(eye, np.cos(ang)), np.kron(eye, np.sin(ang))], axis=1)
    return jnp.asarray(table, dtype=F32).astype(BF16)


def _group_mean_matrix(n_heads):
    return jnp.asarray(np.kron(np.eye(n_heads), np.full((GQA_HEAD_DIM, GQA_HEAD_DIM), 1.0 / GQA_HEAD_DIM)), dtype=BF16)


def _rope_place_matrix():
    return jnp.asarray(np.eye(MLA_ROPE_DIM, LANES), dtype=BF16)


DFT_SPLIT = 32


def _seq_dft_matrix(seq):
    s = np.arange(seq)
    ang = 2.0 * np.pi * ((s[:, None] * s[None, :]) % seq) / seq
    return jnp.asarray(np.stack([np.cos(ang), np.sin(ang)]), dtype=F32).astype(BF16)


def _seq_dft_tables(seq):
    s = np.arange(seq)
    n_a = seq // DFT_SPLIT
    ang_a = 2.0 * np.pi * ((np.arange(n_a)[:, None] * s[None, :]) % n_a) / n_a
    ang_b = 2.0 * np.pi * ((np.arange(DFT_SPLIT)[:, None] * s[None, :]) % seq) / seq
    return tuple(jnp.asarray(t, dtype=F32) for t in (np.cos(ang_a), np.sin(ang_a), np.cos(ang_b), np.sin(ang_b)))


def _rms(x):
    return x * lax.rsqrt(jnp.mean(x * x, axis=-1, keepdims=True) + EPS)


def _dot(a, b):
    return jnp.dot(a, b, preferred_element_type=F32)


def _group_rms(x, mean_mat):
    ms = _dot((x * x).astype(BF16), mean_mat)
    return x * lax.rsqrt(ms + EPS)


def _rotate(x, cos, sin_a, sin_b, half):
    return x * cos + pltpu.roll(x, half, 1) * sin_a + pltpu.roll(x, LANES - half, 1) * sin_b


def _lower_half(rows):
    return lax.broadcasted_iota(jnp.int32, (rows, LANES), 1) < HALF


def _layer_spec(arr, layer):
    return pl.BlockSpec((None,) + arr.shape[1:], lambda i: (layer, 0, 0))


def _ada_kernel(ct_ref, w_ref, b_ref, o_ref):
    s = ct_ref[...]
    s = s * jax.nn.sigmoid(s)
    w = w_ref[...]
    for m in range(o_ref.shape[0]):
        o_ref[m:m + 1, :] = jnp.sum(w * s[:, m:m + 1], axis=0, keepdims=True) + b_ref[...]


def _ada_specs(cond_t, w_ada, layer, tn):
    n_cond = cond_t.shape[1]
    in_specs = [pl.BlockSpec((D_MODEL, n_cond), lambda j: (0, 0)),
                pl.BlockSpec((None, D_MODEL, tn), lambda j: (layer, 0, j)),
                pl.BlockSpec((None, 1, tn), lambda j: (layer, 0, j))]
    out_shape = jax.ShapeDtypeStruct((n_cond, w_ada.shape[2]), F32)
    return in_specs, pl.BlockSpec((n_cond, tn), lambda j: (0, j)), out_shape


def _ada(cond_t, w_ada, b_ada, layer, tn=ADA_COL_TILE):
    in_specs, out_spec, out_shape = _ada_specs(cond_t, w_ada, layer, tn)
    return pl.pallas_call(
        _ada_kernel,
        grid=(w_ada.shape[2] // tn,),
        in_specs=in_specs,
        out_specs=out_spec,
        out_shape=out_shape,
        compiler_params=_params(1),
        name="ada",
    )(cond_t, w_ada, b_ada)


_PROJ_WEIGHTS = ("g1", "w_a", "w_c", "g_q_a", "wq", "g_kv_a", "wk", "wv", "g_q_head", "g_k_head")


def _proj_kernel(*refs, rope, emit_cache, n_prev, multi_seq):
    (x_ref, mod_ref, g1_ref, wa_ref, wc_ref, gqa_ref, wq_ref, gkva_ref, wk_ref, wv_ref,
     gqh_ref, gkh_ref, wdft_ref, mqk_ref) = refs[:14]
    refs = refs[14:]
    if multi_seq:
        sdft_ref, refs = refs[0], refs[1:]
    if rope:
        tab_ref, refs = refs[0], refs[1:]
    if n_prev:
        prev_refs, refs = refs[:4], refs[4:]
    n_fourier = 1 if multi_seq else 2
    qm_ref, km_ref, vm_ref = refs[:3]
    fourier_refs = refs[3:3 + n_fourier]
    gq_ref, gk_ref, gv_ref = refs[3 + n_fourier:6 + n_fourier]
    cache_refs = refs[6 + n_fourier:]

    shift = mod_ref[:, 0:D_MODEL]
    scale = mod_ref[:, D_MODEL:2 * D_MODEL]
    lower = _lower_half(PROJ_SUB)

    for r in range(x_ref.shape[0] // PROJ_SUB):
        rows = slice(r * PROJ_SUB, (r + 1) * PROJ_SUB)
        seq_cols = (slice(None), slice(r * FNET_WIDTH, (r + 1) * FNET_WIDTH)) if multi_seq else (rows, slice(None))
        t_idx = (r,) if multi_seq else (slice(None), rows)

        hb = (_rms(x_ref[rows, :]) * g1_ref[...] * (1.0 + scale) + shift).astype(BF16)
        pa = _dot(hb, wa_ref[...])
        pc = _dot(hb, wc_ref[...])
        kr = pa[:, _IN_A:]

        if rope:
            cos_m, sa_m, sb_m = (tab_ref[rows, i * LANES:(i + 1) * LANES] for i in range(3))
            cos_g, sa_g, sb_g = (tab_ref[rows, i * LANES:(i + 1) * LANES] for i in range(3, 6))

        qk_normed = _group_rms(pc[:, _C_GQ:_C_GV], mqk_ref[...])
        gqn = qk_normed[:, :GQA_WIDTH] * gqh_ref[...]
        tiles = []
        for p in range(HEAD_PAIRS):
            gp = gqn[:, p * LANES:(p + 1) * LANES]
            if rope:
                gp = _rotate(gp, cos_g, sa_g, sb_g, GQA_HEAD_DIM // 2)
            tiles.append(gp * (LOG2_E * GQA_HEAD_DIM ** -0.5))
        gq_ref[rows, 0:LANES] = jnp.where(lower, tiles[0], tiles[1]).astype(BF16)
        gq_ref[rows, LANES:2 * LANES] = jnp.where(
            lower, pltpu.roll(tiles[0], HALF, 1), pltpu.roll(tiles[2], HALF, 1)).astype(BF16)
        gq_ref[rows, 2 * LANES:] = jnp.where(lower, tiles[1], tiles[2]).astype(BF16)
        gkn = qk_normed[:, GQA_WIDTH:] * gkh_ref[...]
        gkr = _rotate(gkn, cos_g, sa_g, sb_g, GQA_HEAD_DIM // 2) if rope else gkn
        gv_t = pc[:, _C_GV:_C_GV + GQA_KV_WIDTH].T
        gk_ref[rows, :] = gkr.astype(BF16)
        gv_ref[t_idx] = gv_t.astype(BF16)

        cqn = _rms(pa[:, :MLA_Q_RANK]) * gqa_ref[...]
        q = _dot(cqn.astype(BF16), wq_ref[...])
        for hd in range(MLA_HEADS):
            qh = q[:, hd * LANES:(hd + 1) * LANES]
            if rope:
                qh = _rotate(qh, cos_m, sa_m, sb_m, MLA_ROPE_DIM // 2)
            qm_ref[rows, hd * LANES:(hd + 1) * LANES] = (qh * (LOG2_E * MLA_QK_DIM ** -0.5)).astype(BF16)

        ckvn = _rms(pa[:, MLA_Q_RANK:_IN_A]) * gkva_ref[...]
        cb = ckvn.astype(BF16)
        kn = _dot(cb, wk_ref[...])
        krr = _rotate(kr, cos_m, sa_m, sb_m, MLA_ROPE_DIM // 2) if rope else kr
        for hd in range(MLA_HEADS):
            km_ref[rows, hd * LANES:(hd + 1) * LANES] = (kn[:, hd * LANES:(hd + 1) * LANES] + krr).astype(BF16)
        vm_ref[t_idx] = _dot(cb, wv_ref[...]).T.astype(BF16)

        ucs = _dot(pc[:, _C_U:_C_U + FNET_WIDTH].astype(BF16), wdft_ref[...])
        uc = ucs[:, :FNET_WIDTH].astype(BF16)
        us = ucs[:, FNET_WIDTH:].astype(BF16)
        if multi_seq:
            fn = _dot(sdft_ref[0], uc) - _dot(sdft_ref[1], us)
            fourier_refs[0][seq_cols] = (fn * (PROJ_SUB * FNET_GROUP_DIM) ** -0.5).astype(BF16)
        else:
            fourier_refs[0][seq_cols] = uc
            fourier_refs[1][seq_cols] = us

        if emit_cache:
            new = (ckvn, kr.T[:MLA_ROPE_DIM, :], gkn.T, gv_t)
            for i, out_ref in enumerate(cache_refs):
                if n_prev:
                    out_ref[r, :n_prev] = prev_refs[i][r]
                out_ref[r, n_prev] = new[i]


def _proj(x, mod, wts, win_parts, layer, consts, *, seq, rope, prev_cache=None, emit_cache=False, tm=PROJ_TILE):
    n_tok = x.shape[0]
    n_tiles = n_tok // tm
    n_seq = n_tok // seq
    tiles_per_mod = n_tiles // mod.shape[0]
    multi_seq = tm > seq
    if multi_seq:
        assert seq == PROJ_SUB and not rope
        spt = tm // seq
        seq_major = pl.BlockSpec((seq, spt * FNET_WIDTH), lambda i: (0, i))
        transposed = lambda w: pl.BlockSpec((spt, w, seq), lambda i: (i, 0, 0))
    else:
        assert not emit_cache
        tps = seq // tm
        seq_major = pl.BlockSpec((tm, FNET_WIDTH), lambda i: (i % tps, i // tps))
        transposed = lambda w: pl.BlockSpec((None, w, tm), lambda i: (i // tps, 0, i % tps))
    const = lambda arr: pl.BlockSpec(arr.shape, lambda i: (0,) * arr.ndim)
    tok = lambda w: pl.BlockSpec((tm, w), lambda i: (i, 0))
    in_specs = [tok(D_MODEL), pl.BlockSpec((None, 1, 6 * D_MODEL), lambda i: (i // tiles_per_mod, 0, 0))]
    args = [x, mod]
    for name in _PROJ_WEIGHTS:
        arr = win_parts.get(name)
        in_specs.append(const(arr) if arr is not None else _layer_spec(wts[name], layer))
        args.append(arr if arr is not None else wts[name])
    for name in ("wdft", "mean_qk"):
        in_specs.append(const(consts[name]))
        args.append(consts[name])
    if multi_seq:
        in_specs.append(const(consts["seq_dft"]))
        args.append(consts["seq_dft"])
    if rope:
        in_specs.append(pl.BlockSpec((tm, 6 * LANES), lambda i: (i % tps, 0)))
        args.append(consts["rope"])
    n_fourier = 1 if multi_seq else 2
    out_specs = [tok(MLA_PAD_WIDTH), tok(MLA_PAD_WIDTH), transposed(MIX_TILE)] + [seq_major] * n_fourier + [
                 tok(GQA_WIDTH), tok(GQA_KV_WIDTH), transposed(GQA_KV_WIDTH)]
    out_shape = [jax.ShapeDtypeStruct((n_tok, MLA_PAD_WIDTH), BF16),
                 jax.ShapeDtypeStruct((n_tok, MLA_PAD_WIDTH), BF16),
                 jax.ShapeDtypeStruct((n_seq, MIX_TILE, seq), BF16)] + [
                 jax.ShapeDtypeStruct((seq, n_seq * FNET_WIDTH), BF16)] * n_fourier + [
                 jax.ShapeDtypeStruct((n_tok, GQA_WIDTH), BF16),
                 jax.ShapeDtypeStruct((n_tok, GQA_KV_WIDTH), BF16),
                 jax.ShapeDtypeStruct((n_seq, GQA_KV_WIDTH, seq), BF16)]
    n_prev = 0
    if emit_cache:
        n_prev = prev_cache[0].shape[1] if prev_cache is not None else 0
        layers = lambda n, tail: pl.BlockSpec((spt, n) + tail, lambda i: (i, 0, 0, 0))
        tails = ((seq, MLA_KV_RANK), (MLA_ROPE_DIM, seq), (GQA_KV_WIDTH, seq), (GQA_KV_WIDTH, seq))
        if n_prev:
            in_specs += [layers(n_prev, t) for t in tails]
            args += list(prev_cache)
        out_specs += [layers(n_prev + 1, t) for t in tails]
        out_shape += [jax.ShapeDtypeStruct((n_seq, n_prev + 1) + t, F32) for t in tails]
    return pl.pallas_call(
        functools.partial(_proj_kernel, rope=rope, emit_cache=emit_cache, n_prev=n_prev, multi_seq=multi_seq),
        grid=(n_tiles,),
        in_specs=in_specs,
        out_specs=out_specs,
        out_shape=out_shape,
        compiler_params=_params(1),
        name="proj_rope" if rope else "proj_ctx",
    )(*args)


def _ctx_kernel(ckv_ref, kr_ref, gk_ref, gv_ref, wk_ref, wv_ref, place_ref, win_ref,
                km_ref, vm_ref, gko_ref, gvo_ref, wa_ref, wc_ref):
    wt = win_ref[...]
    kr_rows = jnp.concatenate([wt[_IN_A:_IN_C0, :], jnp.zeros((LANES - MLA_ROPE_DIM, wt.shape[1]), F32)], axis=0)
    wa_ref[:, :_IN_A] = wt[:_IN_A, :].T.astype(BF16)
    wa_ref[:, _IN_A:] = kr_rows.T.astype(BF16)
    wc_ref[...] = wt[_IN_C0:, :].T.astype(BF16)

    cb = ckv_ref[...].astype(BF16)
    kn = _dot(cb, wk_ref[...])
    kr = _dot(kr_ref[...].astype(BF16), place_ref[...])
    for hd in range(MLA_HEADS):
        km_ref[:, hd * LANES:(hd + 1) * LANES] = (kn[:, hd * LANES:(hd + 1) * LANES] + kr).astype(BF16)
    vm_ref[...] = _dot(cb, wv_ref[...]).T.astype(BF16)
    gko_ref[...] = gk_ref[...].astype(BF16)
    gvo_ref[...] = gv_ref[...].T.astype(BF16)


def _ctx_prep(layer, cache_ckv, cache_krope, cache_gk, cache_gv, w_in_t, wts, consts):
    nb, _, past, _ = cache_ckv.shape
    rows = D_MODEL // nb
    cache = lambda w: pl.BlockSpec((None, None, past, w), lambda b: (b, layer, 0, 0))
    out = lambda w: pl.BlockSpec((None, past, w), lambda b: (b, 0, 0))
    out_t = lambda w: pl.BlockSpec((None, w, past), lambda b: (b, 0, 0))
    part = lambda w: pl.BlockSpec((rows, w), lambda b: (b, 0))
    part_widths = (_IN_A + LANES, _IN_C)
    return pl.pallas_call(
        _ctx_kernel,
        grid=(nb,),
        in_specs=[cache(MLA_KV_RANK), cache(MLA_ROPE_DIM), cache(GQA_KV_WIDTH), cache(GQA_KV_WIDTH),
                  _layer_spec(wts["wk"], layer), _layer_spec(wts["wv"], layer),
                  pl.BlockSpec((MLA_ROPE_DIM, LANES), lambda b: (0, 0)),
                  pl.BlockSpec((None, w_in_t.shape[1], rows), lambda b: (layer, 0, b))],
        out_specs=[out(MLA_PAD_WIDTH), out_t(MIX_TILE), out(GQA_KV_WIDTH), out_t(GQA_KV_WIDTH)]
                  + [part(w) for w in part_widths],
        out_shape=[jax.ShapeDtypeStruct((nb, past, MLA_PAD_WIDTH), BF16),
                   jax.ShapeDtypeStruct((nb, MIX_TILE, past), BF16),
                   jax.ShapeDtypeStruct((nb, past, GQA_KV_WIDTH), BF16),
                   jax.ShapeDtypeStruct((nb, GQA_KV_WIDTH, past), BF16)]
                  + [jax.ShapeDtypeStruct((D_MODEL, w), BF16) for w in part_widths],
        compiler_params=_params(1),
        name="ctx_prep",
    )(cache_ckv, cache_krope, cache_gk.reshape(nb, DEPTH, past, GQA_KV_WIDTH),
      cache_gv.reshape(nb, DEPTH, past, GQA_KV_WIDTH), wts["wk"], wts["wv"], consts["place"], w_in_t)


def _attn_kernel(*refs, n_seg, ahead, n_cast):
    n_in = 1 + 2 * n_seg
    outs_at = 2 * n_in + n_cast
    families = []
    for f, mla in enumerate((True, False)):
        ins = refs[f * n_in:(f + 1) * n_in]
        families.append((mla, ins[0], ins[1:1 + n_seg], ins[1 + n_seg:], refs[outs_at + f]))
    for c in range(n_cast):
        refs[outs_at + 2 + c][...] = refs[2 * n_in + c][...].astype(BF16)
    n_seqs, tq = refs[0].shape[:2]
    lower = _lower_half(tq)
    top = lax.broadcasted_iota(jnp.int32, (LANES, tq), 0) < HALF

    def scores_t(f, b, p, hd):
        mla, q_ref, k_refs, _, _ = families[f]
        if mla:
            cols = slice((2 * p + hd) * LANES, (2 * p + hd + 1) * LANES)
            qh = q_ref[b, :, cols]
            keys = [k_ref[b, :, cols] for k_ref in k_refs]
        else:
            q = q_ref[b, :, p * LANES:(p + 1) * LANES]
            qh = jnp.where(lower if hd == 0 else ~lower, q, jnp.zeros_like(q))
            keys = [k_ref[b] for k_ref in k_refs]
        return [lax.dot_general(k, qh, (((1,), (1,)), ((), ())), preferred_element_type=F32) for k in keys]

    def softmax_t(ss):
        m = functools.reduce(jnp.maximum, [jnp.max(s, axis=0, keepdims=True) for s in ss])
        es = [jnp.exp2(s - m) for s in ss]
        denom = functools.reduce(jnp.add, [jnp.sum(e, axis=0, keepdims=True) for e in es])
        return [e.astype(BF16) for e in es], denom

    def weighted_values_t(f, b, p, es, denom):
        mla, _, _, vt_refs, _ = families[f]
        rows = slice(p * LANES, (p + 1) * LANES) if mla else slice(None)
        acc = functools.reduce(jnp.add, [_dot(vt_ref[b, rows, :], e) for e, vt_ref in zip(es, vt_refs)])
        return acc * (1.0 / denom)

    work = [(f, b, p, hd) for f in range(2) for b in range(n_seqs) for p in range(HEAD_PAIRS) for hd in range(2)]
    pending = [scores_t(*w) for w in work[:ahead]]
    outs = {}
    for i, (f, b, p, hd) in enumerate(work):
        es, denom = softmax_t(pending.pop(0))
        if i + ahead < len(work):
            pending.append(scores_t(*work[i + ahead]))
        outs[f, b, p, hd] = weighted_values_t(f, b, p, es, denom)
        if (p, hd) != (HEAD_PAIRS - 1, 1):
            continue
        mla, o_ref = families[f][0], families[f][4]
        tiles = [jnp.where(top, outs[f, b, t, 0], outs[f, b, t, 1]).T for t in range(HEAD_PAIRS)]
        if not mla:
            t1 = pltpu.roll(tiles[1], HALF, 1)
            tiles = [jnp.where(lower, tiles[0], t1), jnp.where(lower, tiles[2], tiles[0]),
                     jnp.where(lower, t1, tiles[2])]
        for t in range(HEAD_PAIRS):
            o_ref[b, :, t * LANES:(t + 1) * LANES] = tiles[t].astype(o_ref.dtype)


def _attention(mla_q, mla_kv, gqa_q, gqa_kv, *, tq, ahead, bb=1, cast=(), layer=0):
    n_batch, sq = mla_q.shape[:2]
    n_seg = len(mla_kv)
    n_q = sq // tq
    steps = (n_batch // bb) * n_q
    whole = lambda a: pl.BlockSpec((bb,) + a.shape[1:], lambda b, i: (b, 0, 0))
    in_specs, args = [], []
    for q, kv in ((mla_q, mla_kv), (gqa_q, gqa_kv)):
        in_specs += [pl.BlockSpec((bb, tq, q.shape[2]), lambda b, i: (b, i, 0))]
        in_specs += [whole(k) for k, _ in kv] + [whole(v) for _, v in kv]
        args += [q] + [k for k, _ in kv] + [v for _, v in kv]
    out_spec = pl.BlockSpec((bb, tq, MIX_TILE), lambda b, i: (b, i, 0))
    out_specs = [out_spec, out_spec]
    out_shape = [jax.ShapeDtypeStruct((n_batch, sq, MIX_TILE), BF16)] * 2
    for w in cast:
        rows, cols = w.shape[1] // steps, w.shape[2]
        in_specs.append(pl.BlockSpec((None, rows, cols), lambda b, i: (layer, b * n_q + i, 0)))
        out_specs.append(pl.BlockSpec((rows, cols), lambda b, i: (b * n_q + i, 0)))
        out_shape.append(jax.ShapeDtypeStruct(w.shape[1:], BF16))
        args.append(w)
    return pl.pallas_call(
        functools.partial(_attn_kernel, n_seg=n_seg, ahead=ahead, n_cast=len(cast)),
        grid=(n_batch // bb, n_q),
        in_specs=in_specs,
        out_specs=out_specs,
        out_shape=out_shape,
        compiler_params=_params(2),
        name=f"attn_s{n_seg}",
    )(*args)


def _fnet_kernel(ca_ref, sa_ref, cb_ref, sb_ref, uc_ref, us_ref, o_ref, tc_ref, ts_ref, *, scale):
    @pl.when(pl.program_id(1) == 0)
    def _build_twiddles():
        cb, sb = cb_ref[...], sb_ref[...]
        for a in range(ca_ref.shape[0]):
            ca, sa = ca_ref[a:a + 1, :], sa_ref[a:a + 1, :]
            rows = slice(a * DFT_SPLIT, (a + 1) * DFT_SPLIT)
            tc_ref[rows, :] = (ca * cb - sa * sb).astype(BF16)
            ts_ref[rows, :] = (sa * cb + ca * sb).astype(BF16)

    acc = _dot(tc_ref[...], uc_ref[...]) - _dot(ts_ref[...], us_ref[...])
    o_ref[...] = (acc * scale).astype(o_ref.dtype)


def _fnet(tables, uc, us, tm, tn):
    seq, width = uc.shape
    scale = float((seq * FNET_GROUP_DIM) ** -0.5)
    n_a = tm // DFT_SPLIT
    part_a = pl.BlockSpec((n_a, seq), lambda i, j: (i, 0))
    part_b = pl.BlockSpec((DFT_SPLIT, seq), lambda i, j: (0, 0))
    data = pl.BlockSpec((seq, tn), lambda i, j: (0, j))
    return pl.pallas_call(
        functools.partial(_fnet_kernel, scale=scale),
        grid=(seq // tm, width // tn),
        in_specs=[part_a, part_a, part_b, part_b, data, data],
        out_specs=pl.BlockSpec((tm, tn), lambda i, j: (i, j)),
        out_shape=jax.ShapeDtypeStruct((seq, width), BF16),
        scratch_shapes=[pltpu.VMEM((tm, seq), BF16), pltpu.VMEM((tm, seq), BF16)],
        compiler_params=_params(2),
        name="fnet",
    )(*tables, uc, us)


def _post_body(x_ref, mla_ref, fn_ref, gqa_ref, o_ref, mod_ref, g2_ref, wo_ref, wg_ref, wu_ref, wd_ref, gf_ref, final):
    fn = fn_ref[...]
    if fn.shape[1] > FNET_WIDTH:
        fn = jnp.concatenate([fn[:, j:j + FNET_WIDTH] for j in range(0, fn.shape[1], FNET_WIDTH)], axis=0)
    mix = jnp.concatenate([mla_ref[...], fn, gqa_ref[...]], axis=-1)
    gate1 = mod_ref[:, 2 * D_MODEL:3 * D_MODEL]
    shift2 = mod_ref[:, 3 * D_MODEL:4 * D_MODEL]
    scale2 = mod_ref[:, 4 * D_MODEL:5 * D_MODEL]
    gate2 = mod_ref[:, 5 * D_MODEL:6 * D_MODEL]
    x = x_ref[...] + gate1 * _dot(mix, wo_ref[...])
    h = (_rms(x) * g2_ref[...] * (1.0 + scale2) + shift2).astype(BF16)
    g = _dot(h, wg_ref[...])
    u = _dot(h, wu_ref[...])
    a = (g * jax.nn.sigmoid(g) * u).astype(BF16)
    x = x + gate2 * _dot(a, wd_ref[...])
    if final:
        x = _rms(x) * gf_ref[...]
    o_ref[...] = x


def _post_kernel(*refs, final, n_first, with_next_mod):
    first, second, shared = refs[0:4], refs[4:8], refs[8:15]
    n_in = 18 if with_next_mod else 15
    o_first, o_second = refs[n_in:n_in + 2]
    step = pl.program_id(0)

    def run(group, o_ref):
        if with_next_mod:
            _ada_kernel(*refs[15:18], refs[n_in + 2])
        _post_body(*group, o_ref, *shared, final)

    @pl.when(step < n_first)
    def _first_group():
        run(first, o_first)

    @pl.when(step >= n_first)
    def _second_group():
        run(second, o_second)


def _post(groups, mod, g2, ffn, layer, g_final, *, final, tm, next_mod=None):
    (x1, _, _, _, seq1), (x2, _, _, _, seq2) = groups
    n1, n2 = x1.shape[0] // tm, x2.shape[0] // tm
    tiles_per_mod = n2 // (mod.shape[0] - 1)
    tile1 = lambda i: jnp.minimum(i, n1 - 1)
    tile2 = lambda i: jnp.maximum(i - n1, 0)
    resident = lambda arr: pl.BlockSpec(arr.shape, lambda i: (0, 0), pipeline_mode=pl.Buffered(1))

    def group_specs(seq, tile):
        tok = lambda w: pl.BlockSpec((tm, w), lambda i: (tile(i), 0))
        if tm <= seq:
            tps = seq // tm
            fnet_spec = pl.BlockSpec((tm, FNET_WIDTH), lambda i: (tile(i) % tps, tile(i) // tps))
        else:
            fnet_spec = pl.BlockSpec((seq, (tm // seq) * FNET_WIDTH), lambda i: (0, tile(i)))
        return [tok(D_MODEL), tok(MIX_TILE), fnet_spec, tok(MIX_TILE)], tok(D_MODEL)

    in1, out1 = group_specs(seq1, tile1)
    in2, out2 = group_specs(seq2, tile2)
    mod_spec = pl.BlockSpec((None, 1, 6 * D_MODEL),
                            lambda i: (jnp.where(i < n1, 0, 1 + tile2(i) // tiles_per_mod), 0, 0))
    in_specs = (in1 + in2 + [mod_spec, _layer_spec(g2, layer)] + [resident(w) for w in ffn]
                + [pl.BlockSpec((1, D_MODEL), lambda i: (0, 0))])
    args = [*groups[0][:4], *groups[1][:4], mod, g2, *ffn, g_final]
    out_specs = [out1, out2]
    out_shape = [jax.ShapeDtypeStruct(x1.shape, F32), jax.ShapeDtypeStruct(x2.shape, F32)]
    if next_mod is not None:
        cond_t, w_ada, b_ada = next_mod
        ada_in, ada_out, ada_shape = _ada_specs(cond_t, w_ada, layer + 1, w_ada.shape[2] // (n1 + n2))
        in_specs += ada_in
        args += [cond_t, w_ada, b_ada]
        out_specs.append(ada_out)
        out_shape.append(ada_shape)
    return pl.pallas_call(
        functools.partial(_post_kernel, final=final, n_first=n1, with_next_mod=next_mod is not None),
        grid=(n1 + n2,),
        in_specs=in_specs,
        out_specs=out_specs,
        out_shape=out_shape,
        compiler_params=_params(1),
        name="post_final" if final else "post",
    )(*args)


def _prep_weights(g_norm1, g_norm2, g_q_a, w_q_up, g_kv_a, w_kv_up, g_q_head, g_k_head):
    row = lambda g: g.reshape(DEPTH, 1, -1)
    wq = w_q_up.reshape(DEPTH, MLA_Q_RANK, MLA_HEADS, MLA_QK_DIM)
    wq = jnp.concatenate([wq[..., MLA_NOPE_DIM:], wq[..., :MLA_NOPE_DIM]], axis=-1)
    wq = jnp.pad(wq, ((0, 0), (0, 0), (0, 0), (0, LANES - MLA_QK_DIM)))
    wkv = w_kv_up.reshape(DEPTH, MLA_KV_RANK, MLA_HEADS, MLA_NOPE_DIM + MLA_V_DIM)
    wk = jnp.pad(wkv[..., :MLA_NOPE_DIM], ((0, 0), (0, 0), (0, 0), (MLA_ROPE_DIM, LANES - MLA_QK_DIM)))
    wv = wkv[..., MLA_NOPE_DIM:]
    return {
        "g1": row(g_norm1), "g2": row(g_norm2), "g_q_a": row(g_q_a), "g_kv_a": row(g_kv_a),
        "g_q_head": row(jnp.tile(g_q_head, (1, GQA_HEADS))), "g_k_head": row(jnp.tile(g_k_head, (1, GQA_KV_HEADS))),
        "wq": wq.reshape(DEPTH, MLA_Q_RANK, MLA_PAD_WIDTH).astype(BF16),
        "wk": wk.reshape(DEPTH, MLA_KV_RANK, MLA_PAD_WIDTH).astype(BF16),
        "wv": wv.reshape(DEPTH, MLA_KV_RANK, MIX_TILE).astype(BF16),
    }


def kernel(x_prompt, x_sample, cache_mla_ckv, cache_mla_krope, cache_gqa_k, cache_gqa_v, c, c_ctx, w_ada, b_ada,
           g_norm1, g_norm2, w_in, g_q_a, w_q_up, g_kv_a, w_kv_up, g_q_head, g_k_head, w_out,
           w_ffn_gate, w_ffn_up, w_ffn_down, g_final):
    n_pb, p_seq, _ = x_prompt.shape
    n_sb, s_seq, _ = x_sample.shape
    consts = {"wdft": _channel_dft(), "mean_qk": _group_mean_matrix(GQA_HEADS + GQA_KV_HEADS),
              "place": _rope_place_matrix(),
              "rope": _rope_tables(s_seq), "seq_dft": _seq_dft_matrix(p_seq)}
    dft_s = _seq_dft_tables(s_seq)
    gf = g_final.reshape(1, D_MODEL)
    wts = _prep_weights(g_norm1, g_norm2, g_q_a, w_q_up, g_kv_a, w_kv_up, g_q_head, g_k_head)
    w_in_t = jnp.swapaxes(w_in, 1, 2)

    cond_t = jnp.concatenate([c_ctx[None, :], c], axis=0).T
    b_ada3 = b_ada.reshape(DEPTH, 1, 6 * D_MODEL)
    mod = _ada(cond_t, w_ada, b_ada3, 0)

    xp = x_prompt.reshape(n_pb * p_seq, D_MODEL)
    xs = x_sample.reshape(n_sb * s_seq, D_MODEL)
    b3 = lambda a, nb: a.reshape(nb, a.shape[0] // nb, a.shape[1])
    caches = None
    for l in range(DEPTH):
        final = l == DEPTH - 1
        mod_p = mod[0:1].reshape(1, 1, 6 * D_MODEL)
        mod_s = mod[1:].reshape(n_sb, 1, 6 * D_MODEL)

        km_c, vm_c, gk_c, gv_c, w_a, w_c = _ctx_prep(
            l, cache_mla_ckv, cache_mla_krope, cache_gqa_k, cache_gqa_v, w_in_t, wts, consts)
        win_parts = {"w_a": w_a, "w_c": w_c}
        qm, km, vm, uc_s, us_s, gq, gk, gv = _proj(xs, mod_s, wts, win_parts, l, consts, seq=s_seq, rope=True)
        mla_s, gqa_s, *ffn = _attention(b3(qm, n_sb), [(km_c, vm_c), (b3(km, n_sb), vm)],
                                        b3(gq, n_sb), [(gk_c, gv_c), (b3(gk, n_sb), gv)], tq=LATENT_Q_TILE,
                                        ahead=QK_AHEAD_LATENT,
                                        cast=(w_out, w_ffn_gate, w_ffn_up, w_ffn_down), layer=l)

        qm, km, vm, fn_p, gq, gk, gv, *caches = _proj(
            xp, mod_p, wts, win_parts, l, consts, seq=p_seq, rope=False, emit_cache=True, prev_cache=caches)
        mla_o, gqa_o = _attention(b3(qm, n_pb), [(b3(km, n_pb), vm)], b3(gq, n_pb), [(b3(gk, n_pb), gv)],
                                  tq=p_seq, bb=CONTEXT_SEQS_PER_STEP, ahead=QK_AHEAD_CONTEXT)
        fn_s = _fnet(dft_s, uc_s, us_s, tm=FNET_ROW_TILE, tn=n_sb * FNET_WIDTH)
        flat = lambda a: a.reshape(-1, MIX_TILE)
        xp, xs, *mod_next = _post(
            [(xp, flat(mla_o), fn_p, flat(gqa_o), p_seq), (xs, flat(mla_s), fn_s, flat(gqa_s), s_seq)],
            mod.reshape(1 + n_sb, 1, 6 * D_MODEL), wts["g2"], ffn, l, gf, final=final, tm=POST_TILE,
            next_mod=None if final else (cond_t, w_ada, b_ada3))
        if mod_next:
            mod = mod_next[0]

    ckv_new, krope_t, gk_t, gv_t = caches
    heads = lambda a: jnp.swapaxes(a, 2, 3).reshape(n_pb, DEPTH, p_seq, GQA_KV_HEADS, GQA_HEAD_DIM)
    return (xp.reshape(n_pb, p_seq, D_MODEL), xs.reshape(n_sb, s_seq, D_MODEL),
            ckv_new, jnp.swapaxes(krope_t, 2, 3), heads(gk_t), heads(gv_t))
```

```python
import functools

import numpy as np
import jax
import jax.numpy as jnp
from jax import lax
from jax.experimental import pallas as pl
from jax.experimental.pallas import tpu as pltpu

D_MODEL = 1024
DEPTH = 2
GRID_W = 64
MLA_HEADS = 6
MLA_Q_RANK = 384
MLA_KV_RANK = 256
MLA_NOPE_DIM = 64
MLA_ROPE_DIM = 32
MLA_V_DIM = 64
MLA_QK_DIM = MLA_NOPE_DIM + MLA_ROPE_DIM
FNET_GROUPS = 4
FNET_GROUP_DIM = 64
FNET_WIDTH = FNET_GROUPS * FNET_GROUP_DIM
GQA_HEADS = 6
GQA_KV_HEADS = 2
GQA_HEAD_DIM = 64
GQA_GROUP = GQA_HEADS // GQA_KV_HEADS
GQA_WIDTH = GQA_HEADS * GQA_HEAD_DIM
GQA_KV_WIDTH = GQA_KV_HEADS * GQA_HEAD_DIM
D_FF = 2816
ROPE_THETA = 10000.0
EPS = 1e-6
LOG2_E = 1.4426950408889634

LANES = 128
HALF = LANES // 2
HEAD_PAIRS = MLA_HEADS // 2
MLA_PAD_WIDTH = MLA_HEADS * LANES
MIX_TILE = HEAD_PAIRS * LANES
MIB = 1024 * 1024
VMEM_LIMIT_MIB = 52
POST_TILE = 512
PROJ_TILE = 512
PROJ_SUB = 256
LATENT_Q_TILE = 512
CONTEXT_SEQS_PER_STEP = 4
QK_AHEAD_LATENT = 2
QK_AHEAD_CONTEXT = 8
FNET_ROW_TILE = 512
ADA_COL_TILE = 1024


def _params(n_grid_dims):
    return pltpu.CompilerParams(dimension_semantics=("arbitrary",) * n_grid_dims,
                                vmem_limit_bytes=VMEM_LIMIT_MIB * MIB)


_IN_A = MLA_Q_RANK + MLA_KV_RANK
_IN_C0 = _IN_A + MLA_ROPE_DIM
_IN_C = FNET_WIDTH + GQA_WIDTH + 2 * GQA_KV_WIDTH
_C_U, _C_GQ, _C_GK, _C_GV = 0, FNET_WIDTH, FNET_WIDTH + GQA_WIDTH, FNET_WIDTH + GQA_WIDTH + GQA_KV_WIDTH

BF16 = jnp.bfloat16
F32 = jnp.float32


def _rope_tables(n_tokens):
    t = np.arange(n_tokens)
    row = (t // GRID_W).astype(np.float64)
    col = (t % GRID_W).astype(np.float64)

    def angles(rot_dim):
        n_axis = rot_dim // 4
        inv = ROPE_THETA ** (-np.arange(n_axis, dtype=np.float64) / n_axis)
        return np.concatenate([row[:, None] * inv, col[:, None] * inv], axis=-1)

    def tables(ang, lane_to_pair, is_first, is_second):
        cos = np.where((is_first | is_second)[None, :], np.cos(ang)[:, lane_to_pair], 1.0)
        sin = np.sin(ang)[:, lane_to_pair]
        sin_a = np.where(is_second[None, :], sin, 0.0)
        sin_b = np.where(is_first[None, :], -sin, 0.0)
        return [cos, sin_a, sin_b]

    lane = np.arange(LANES)
    half_m = MLA_ROPE_DIM // 2
    in_rope = lane < MLA_ROPE_DIM
    first_m = lane < half_m
    second_m = in_rope & ~first_m
    pair_m = lane % half_m
    half_g = GQA_HEAD_DIM // 2
    first_g = (lane % GQA_HEAD_DIM) < half_g
    pair_g = lane % half_g
    tabs = (tables(angles(MLA_ROPE_DIM), pair_m, first_m, second_m)
            + tables(angles(GQA_HEAD_DIM), pair_g, first_g, ~first_g))
    return jnp.asarray(np.concatenate(tabs, axis=-1), dtype=F32)


def _channel_dft():
    c = np.arange(FNET_GROUP_DIM)
    ang = 2.0 * np.pi * np.outer(c, c) / FNET_GROUP_DIM
    eye = np.eye(FNET_GROUPS)
    table = np.concatenate([np.kron(eye, np.cos(ang)), np.kron(eye, np.sin(ang))], axis=1)
    return jnp.asarray(table, dtype=F32).astype(BF16)


def _group_mean_matrix(n_heads):
    return jnp.asarray(np.kron(np.eye(n_heads), np.full((GQA_HEAD_DIM, GQA_HEAD_DIM), 1.0 / GQA_HEAD_DIM)), dtype=BF16)


def _rope_place_matrix():
    return jnp.asarray(np.eye(MLA_ROPE_DIM, LANES), dtype=BF16)


DFT_SPLIT = 32


def _seq_dft_matrix(seq):
    s = np.arange(seq)
    ang = 2.0 * np.pi * ((s[:, None] * s[None, :]) % seq) / seq
    return jnp.asarray(np.stack([np.cos(ang), np.sin(ang)]), dtype=F32).astype(BF16)


def _seq_dft_tables(seq):
    s = np.arange(seq)
    n_a = seq // DFT_SPLIT
    ang_a = 2.0 * np.pi * ((np.arange(n_a)[:, None] * s[None, :]) % n_a) / n_a
    ang_b = 2.0 * np.pi * ((np.arange(DFT_SPLIT)[:, None] * s[None, :]) % seq) / seq
    return tuple(jnp.asarray(t, dtype=F32) for t in (np.cos(ang_a), np.sin(ang_a), np.cos(ang_b), np.sin(ang_b)))


def _rms(x):
    return x * lax.rsqrt(jnp.mean(x * x, axis=-1, keepdims=True) + EPS)


def _dot(a, b):
    return jnp.dot(a, b, preferred_element_type=F32)


def _group_rms(x, mean_mat):
    ms = _dot((x * x).astype(BF16), mean_mat)
    return x * lax.rsqrt(ms + EPS)


def _rotate(x, cos, sin_a, sin_b, half):
    return x * cos + pltpu.roll(x, half, 1) * sin_a + pltpu.roll(x, LANES - half, 1) * sin_b


def _lower_half(rows):
    return lax.broadcasted_iota(jnp.int32, (rows, LANES), 1) < HALF


def _layer_spec(arr, layer):
    return pl.BlockSpec((None,) + arr.shape[1:], lambda i: (layer, 0, 0))


def _ada_kernel(ct_ref, w_ref, b_ref, o_ref):
    s = ct_ref[...]
    s = s * jax.nn.sigmoid(s)
    w = w_ref[...]
    for m in range(o_ref.shape[0]):
        o_ref[m:m + 1, :] = jnp.sum(w * s[:, m:m + 1], axis=0, keepdims=True) + b_ref[...]


def _ada_specs(cond_t, w_ada, layer, tn):
    n_cond = cond_t.shape[1]
    in_specs = [pl.BlockSpec((D_MODEL, n_cond), lambda j: (0, 0)),
                pl.BlockSpec((None, D_MODEL, tn), lambda j: (layer, 0, j)),
                pl.BlockSpec((None, 1, tn), lambda j: (layer, 0, j))]
    out_shape = jax.ShapeDtypeStruct((n_cond, w_ada.shape[2]), F32)
    return in_specs, pl.BlockSpec((n_cond, tn), lambda j: (0, j)), out_shape


def _ada(cond_t, w_ada, b_ada, layer, tn=ADA_COL_TILE):
    in_specs, out_spec, out_shape = _ada_specs(cond_t, w_ada, layer, tn)
    return pl.pallas_call(
        _ada_kernel,
        grid=(w_ada.shape[2] // tn,),
        in_specs=in_specs,
        out_specs=out_spec,
        out_shape=out_shape,
        compiler_params=_params(1),
        name="ada",
    )(cond_t, w_ada, b_ada)


_PROJ_WEIGHTS = ("g1", "w_a", "w_c", "g_q_a", "wq", "g_kv_a", "wk", "wv", "g_q_head", "g_k_head")


def _proj_kernel(*refs, rope, emit_cache, n_prev, multi_seq, n_cast=0):
    (x_ref, mod_ref, g1_ref, wa_ref, wc_ref, gqa_ref, wq_ref, gkva_ref, wk_ref, wv_ref,
     gqh_ref, gkh_ref, wdft_ref, mq_ref, mk_ref) = refs[:15]
    refs = refs[15:]
    if multi_seq:
        sdft_ref, refs = refs[0], refs[1:]
    if rope:
        tab_ref, refs = refs[0], refs[1:]
    if n_prev:
        prev_refs, refs = refs[:4], refs[4:]
    if n_cast:
        for c in range(n_cast):
            refs[len(refs) - n_cast + c][...] = refs[c][...].astype(BF16)
        refs = refs[n_cast:len(refs) - n_cast]
    n_fourier = 1 if multi_seq else 2
    qm_ref, km_ref, vm_ref = refs[:3]
    fourier_refs = refs[3:3 + n_fourier]
    gq_ref, gk_ref, gv_ref = refs[3 + n_fourier:6 + n_fourier]
    cache_refs = refs[6 + n_fourier:]

    shift = mod_ref[:, 0:D_MODEL]
    scale = mod_ref[:, D_MODEL:2 * D_MODEL]
    lower = _lower_half(PROJ_SUB)

    for r in range(x_ref.shape[0] // PROJ_SUB):
        rows = slice(r * PROJ_SUB, (r + 1) * PROJ_SUB)
        seq_cols = (slice(None), slice(r * FNET_WIDTH, (r + 1) * FNET_WIDTH)) if multi_seq else (rows, slice(None))
        t_idx = (r,) if multi_seq else (slice(None), rows)

        hb = (_rms(x_ref[rows, :]) * g1_ref[...] * (1.0 + scale) + shift).astype(BF16)
        pa = _dot(hb, wa_ref[...])
        pc = _dot(hb, wc_ref[...])
        kr = pa[:, _IN_A:]

        if rope:
            cos_m, sa_m, sb_m = (tab_ref[rows, i * LANES:(i + 1) * LANES] for i in range(3))
            cos_g, sa_g, sb_g = (tab_ref[rows, i * LANES:(i + 1) * LANES] for i in range(3, 6))

        gqn = _group_rms(pc[:, _C_GQ:_C_GQ + GQA_WIDTH], mq_ref[...]) * gqh_ref[...]
        tiles = []
        for p in range(HEAD_PAIRS):
            gp = gqn[:, p * LANES:(p + 1) * LANES]
            if rope:
                gp = _rotate(gp, cos_g, sa_g, sb_g, GQA_HEAD_DIM // 2)
            tiles.append(gp * (LOG2_E * GQA_HEAD_DIM ** -0.5))
        gq_ref[rows, 0:LANES] = jnp.where(lower, tiles[0], tiles[1]).astype(BF16)
        gq_ref[rows, LANES:2 * LANES] = jnp.where(
            lower, pltpu.roll(tiles[0], HALF, 1), pltpu.roll(tiles[2], HALF, 1)).astype(BF16)
        gq_ref[rows, 2 * LANES:] = jnp.where(lower, tiles[1], tiles[2]).astype(BF16)
        gkn = _group_rms(pc[:, _C_GK:_C_GK + GQA_KV_WIDTH], mk_ref[...]) * gkh_ref[...]
        gkr = _rotate(gkn, cos_g, sa_g, sb_g, GQA_HEAD_DIM // 2) if rope else gkn
        gv_t = pc[:, _C_GV:_C_GV + GQA_KV_WIDTH].T
        gk_ref[rows, :] = gkr.astype(BF16)
        gv_ref[t_idx] = gv_t.astype(BF16)

        cqn = _rms(pa[:, :MLA_Q_RANK]) * gqa_ref[...]
        q = _dot(cqn.astype(BF16), wq_ref[...])
        for hd in range(MLA_HEADS):
            qh = q[:, hd * LANES:(hd + 1) * LANES]
            if rope:
                qh = _rotate(qh, cos_m, sa_m, sb_m, MLA_ROPE_DIM // 2)
            qm_ref[rows, hd * LANES:(hd + 1) * LANES] = (qh * (LOG2_E * MLA_QK_DIM ** -0.5)).astype(BF16)

        ckvn = _rms(pa[:, MLA_Q_RANK:_IN_A]) * gkva_ref[...]
        cb = ckvn.astype(BF16)
        kn = _dot(cb, wk_ref[...])
        krr = _rotate(kr, cos_m, sa_m, sb_m, MLA_ROPE_DIM // 2) if rope else kr
        for hd in range(MLA_HEADS):
            km_ref[rows, hd * LANES:(hd + 1) * LANES] = (kn[:, hd * LANES:(hd + 1) * LANES] + krr).astype(BF16)
        vm_ref[t_idx] = _dot(cb, wv_ref[...]).T.astype(BF16)

        ucs = _dot(pc[:, _C_U:_C_U + FNET_WIDTH].astype(BF16), wdft_ref[...])
        uc = ucs[:, :FNET_WIDTH].astype(BF16)
        us = ucs[:, FNET_WIDTH:].astype(BF16)
        if multi_seq:
            fn = _dot(sdft_ref[0], uc) - _dot(sdft_ref[1], us)
            fourier_refs[0][seq_cols] = (fn * (PROJ_SUB * FNET_GROUP_DIM) ** -0.5).astype(BF16)
        else:
            fourier_refs[0][seq_cols] = uc
            fourier_refs[1][seq_cols] = us

        if emit_cache:
            new = (ckvn, kr.T[:MLA_ROPE_DIM, :], gkn.T, gv_t)
            for i, out_ref in enumerate(cache_refs):
                if n_prev:
                    out_ref[r, :n_prev] = prev_refs[i][r]
                out_ref[r, n_prev] = new[i]


def _proj(x, mod, wts, win_parts, layer, consts, *, seq, rope, prev_cache=None, emit_cache=False, tm=PROJ_TILE,
          cast=()):
    n_tok = x.shape[0]
    n_tiles = n_tok // tm
    n_seq = n_tok // seq
    tiles_per_mod = n_tiles // mod.shape[0]
    multi_seq = tm > seq
    if multi_seq:
        assert seq == PROJ_SUB and not rope
        spt = tm // seq
        seq_major = pl.BlockSpec((seq, spt * FNET_WIDTH), lambda i: (0, i))
        transposed = lambda w: pl.BlockSpec((spt, w, seq), lambda i: (i, 0, 0))
    else:
        assert not emit_cache
        tps = seq // tm
        seq_major = pl.BlockSpec((tm, FNET_WIDTH), lambda i: (i % tps, i // tps))
        transposed = lambda w: pl.BlockSpec((None, w, tm), lambda i: (i // tps, 0, i % tps))
    const = lambda arr: pl.BlockSpec(arr.shape, lambda i: (0,) * arr.ndim)
    tok = lambda w: pl.BlockSpec((tm, w), lambda i: (i, 0))
    in_specs = [tok(D_MODEL), pl.BlockSpec((None, 1, 6 * D_MODEL), lambda i: (i // tiles_per_mod, 0, 0))]
    args = [x, mod]
    for name in _PROJ_WEIGHTS:
        arr = win_parts.get(name)
        in_specs.append(const(arr) if arr is not None else _layer_spec(wts[name], layer))
        args.append(arr if arr is not None else wts[name])
    for name in ("wdft", "mean_q", "mean_k"):
        in_specs.append(const(consts[name]))
        args.append(consts[name])
    if multi_seq:
        in_specs.append(const(consts["seq_dft"]))
        args.append(consts["seq_dft"])
    if rope:
        in_specs.append(pl.BlockSpec((tm, 6 * LANES), lambda i: (i % tps, 0)))
        args.append(consts["rope"])
    n_fourier = 1 if multi_seq else 2
    out_specs = [tok(MLA_PAD_WIDTH), tok(MLA_PAD_WIDTH), transposed(MIX_TILE)] + [seq_major] * n_fourier + [
                 tok(GQA_WIDTH), tok(GQA_KV_WIDTH), transposed(GQA_KV_WIDTH)]
    out_shape = [jax.ShapeDtypeStruct((n_tok, MLA_PAD_WIDTH), BF16),
                 jax.ShapeDtypeStruct((n_tok, MLA_PAD_WIDTH), BF16),
                 jax.ShapeDtypeStruct((n_seq, MIX_TILE, seq), BF16)] + [
                 jax.ShapeDtypeStruct((seq, n_seq * FNET_WIDTH), BF16)] * n_fourier + [
                 jax.ShapeDtypeStruct((n_tok, GQA_WIDTH), BF16),
                 jax.ShapeDtypeStruct((n_tok, GQA_KV_WIDTH), BF16),
                 jax.ShapeDtypeStruct((n_seq, GQA_KV_WIDTH, seq), BF16)]
    n_prev = 0
    if emit_cache:
        n_prev = prev_cache[0].shape[1] if prev_cache is not None else 0
        layers = lambda n, tail: pl.BlockSpec((spt, n) + tail, lambda i: (i, 0, 0, 0))
        tails = ((seq, MLA_KV_RANK), (MLA_ROPE_DIM, seq), (GQA_KV_WIDTH, seq), (GQA_KV_WIDTH, seq))
        if n_prev:
            in_specs += [layers(n_prev, t) for t in tails]
            args += list(prev_cache)
        out_specs += [layers(n_prev + 1, t) for t in tails]
        out_shape += [jax.ShapeDtypeStruct((n_seq, n_prev + 1) + t, F32) for t in tails]
    for w in cast:
        rows, cols = w.shape[1] // n_tiles, w.shape[2]
        in_specs.append(pl.BlockSpec((None, rows, cols), lambda i: (layer, i, 0)))
        out_specs.append(pl.BlockSpec((rows, cols), lambda i: (i, 0)))
        out_shape.append(jax.ShapeDtypeStruct(w.shape[1:], BF16))
        args.append(w)
    return pl.pallas_call(
        functools.partial(_proj_kernel, rope=rope, emit_cache=emit_cache, n_prev=n_prev, multi_seq=multi_seq,
                          n_cast=len(cast)),
        grid=(n_tiles,),
        in_specs=in_specs,
        out_specs=out_specs,
        out_shape=out_shape,
        compiler_params=_params(1),
        name="proj_rope" if rope else "proj_ctx",
    )(*args)


def _ctx_kernel(ckv_ref, kr_ref, gk_ref, gv_ref, wk_ref, wv_ref, place_ref, win_ref,
                km_ref, vm_ref, gko_ref, gvo_ref, wa_ref, wc_ref):
    wt = win_ref[...]
    kr_rows = jnp.concatenate([wt[_IN_A:_IN_C0, :], jnp.zeros((LANES - MLA_ROPE_DIM, wt.shape[1]), F32)], axis=0)
    wa_ref[:, :_IN_A] = wt[:_IN_A, :].T.astype(BF16)
    wa_ref[:, _IN_A:] = kr_rows.T.astype(BF16)
    wc_ref[...] = wt[_IN_C0:, :].T.astype(BF16)

    cb = ckv_ref[...].astype(BF16)
    kn = _dot(cb, wk_ref[...])
    kr = _dot(kr_ref[...].astype(BF16), place_ref[...])
    for hd in range(MLA_HEADS):
        km_ref[:, hd * LANES:(hd + 1) * LANES] = (kn[:, hd * LANES:(hd + 1) * LANES] + kr).astype(BF16)
    vm_ref[...] = _dot(cb, wv_ref[...]).T.astype(BF16)
    gko_ref[...] = gk_ref[...].astype(BF16)
    gvo_ref[...] = gv_ref[...].T.astype(BF16)


def _ctx_prep(layer, cache_ckv, cache_krope, cache_gk, cache_gv, w_in_t, wts, consts):
    nb, _, past, _ = cache_ckv.shape
    rows = D_MODEL // nb
    cache = lambda w: pl.BlockSpec((None, None, past, w), lambda b: (b, layer, 0, 0))
    out = lambda w: pl.BlockSpec((None, past, w), lambda b: (b, 0, 0))
    out_t = lambda w: pl.BlockSpec((None, w, past), lambda b: (b, 0, 0))
    part = lambda w: pl.BlockSpec((rows, w), lambda b: (b, 0))
    part_widths = (_IN_A + LANES, _IN_C)
    return pl.pallas_call(
        _ctx_kernel,
        grid=(nb,),
        in_specs=[cache(MLA_KV_RANK), cache(MLA_ROPE_DIM), cache(GQA_KV_WIDTH), cache(GQA_KV_WIDTH),
                  _layer_spec(wts["wk"], layer), _layer_spec(wts["wv"], layer),
                  pl.BlockSpec((MLA_ROPE_DIM, LANES), lambda b: (0, 0)),
                  pl.BlockSpec((None, w_in_t.shape[1], rows), lambda b: (layer, 0, b))],
        out_specs=[out(MLA_PAD_WIDTH), out_t(MIX_TILE), out(GQA_KV_WIDTH), out_t(GQA_KV_WIDTH)]
                  + [part(w) for w in part_widths],
        out_shape=[jax.ShapeDtypeStruct((nb, past, MLA_PAD_WIDTH), BF16),
                   jax.ShapeDtypeStruct((nb, MIX_TILE, past), BF16),
                   jax.ShapeDtypeStruct((nb, past, GQA_KV_WIDTH), BF16),
                   jax.ShapeDtypeStruct((nb, GQA_KV_WIDTH, past), BF16)]
                  + [jax.ShapeDtypeStruct((D_MODEL, w), BF16) for w in part_widths],
        compiler_params=_params(1),
        name="ctx_prep",
    )(cache_ckv, cache_krope, cache_gk.reshape(nb, DEPTH, past, GQA_KV_WIDTH),
      cache_gv.reshape(nb, DEPTH, past, GQA_KV_WIDTH), wts["wk"], wts["wv"], consts["place"], w_in_t)


def _attn_kernel(*refs, n_seg, ahead, n_cast):
    n_in = 1 + 2 * n_seg
    outs_at = 2 * n_in + n_cast
    families = []
    for f, mla in enumerate((True, False)):
        ins = refs[f * n_in:(f + 1) * n_in]
        families.append((mla, ins[0], ins[1:1 + n_seg], ins[1 + n_seg:], refs[outs_at + f]))
    for c in range(n_cast):
        refs[outs_at + 2 + c][...] = refs[2 * n_in + c][...].astype(BF16)
    n_seqs, tq = refs[0].shape[:2]
    lower = _lower_half(tq)
    top = lax.broadcasted_iota(jnp.int32, (LANES, tq), 0) < HALF

    def scores_t(f, b, p, hd):
        mla, q_ref, k_refs, _, _ = families[f]
        if mla:
            cols = slice((2 * p + hd) * LANES, (2 * p + hd + 1) * LANES)
            qh = q_ref[b, :, cols]
            keys = [k_ref[b, :, cols] for k_ref in k_refs]
        else:
            q = q_ref[b, :, p * LANES:(p + 1) * LANES]
            qh = jnp.where(lower if hd == 0 else ~lower, q, jnp.zeros_like(q))
            keys = [k_ref[b] for k_ref in k_refs]
        return [lax.dot_general(k, qh, (((1,), (1,)), ((), ())), preferred_element_type=F32) for k in keys]

    def softmax_t(ss):
        m = functools.reduce(jnp.maximum, [jnp.max(s, axis=0, keepdims=True) for s in ss])
        es = [jnp.exp2(s - m) for s in ss]
        denom = functools.reduce(jnp.add, [jnp.sum(e, axis=0, keepdims=True) for e in es])
        return [e.astype(BF16) for e in es], denom

    def weighted_values_t(f, b, p, es, denom):
        mla, _, _, vt_refs, _ = families[f]
        rows = slice(p * LANES, (p + 1) * LANES) if mla else slice(None)
        acc = functools.reduce(jnp.add, [_dot(vt_ref[b, rows, :], e) for e, vt_ref in zip(es, vt_refs)])
        return acc * (1.0 / denom)

    work = [(f, b, p, hd) for f in range(2) for b in range(n_seqs) for p in range(HEAD_PAIRS) for hd in range(2)]
    pending = [scores_t(*w) for w in work[:ahead]]
    outs = {}
    for i, (f, b, p, hd) in enumerate(work):
        es, denom = softmax_t(pending.pop(0))
        if i + ahead < len(work):
            pending.append(scores_t(*work[i + ahead]))
        outs[f, b, p, hd] = weighted_values_t(f, b, p, es, denom)
        if (p, hd) != (HEAD_PAIRS - 1, 1):
            continue
        mla, o_ref = families[f][0], families[f][4]
        tiles = [jnp.where(top, outs[f, b, t, 0], outs[f, b, t, 1]).T for t in range(HEAD_PAIRS)]
        if not mla:
            t1 = pltpu.roll(tiles[1], HALF, 1)
            tiles = [jnp.where(lower, tiles[0], t1), jnp.where(lower, tiles[2], tiles[0]),
                     jnp.where(lower, t1, tiles[2])]
        for t in range(HEAD_PAIRS):
            o_ref[b, :, t * LANES:(t + 1) * LANES] = tiles[t].astype(o_ref.dtype)


def _attention(mla_q, mla_kv, gqa_q, gqa_kv, *, tq, ahead, bb=1, cast=(), layer=0):
    n_batch, sq = mla_q.shape[:2]
    n_seg = len(mla_kv)
    n_q = sq // tq
    steps = (n_batch // bb) * n_q
    whole = lambda a: pl.BlockSpec((bb,) + a.shape[1:], lambda b, i: (b, 0, 0))
    in_specs, args = [], []
    for q, kv in ((mla_q, mla_kv), (gqa_q, gqa_kv)):
        in_specs += [pl.BlockSpec((bb, tq, q.shape[2]), lambda b, i: (b, i, 0))]
        in_specs += [whole(k) for k, _ in kv] + [whole(v) for _, v in kv]
        args += [q] + [k for k, _ in kv] + [v for _, v in kv]
    out_spec = pl.BlockSpec((bb, tq, MIX_TILE), lambda b, i: (b, i, 0))
    out_specs = [out_spec, out_spec]
    out_shape = [jax.ShapeDtypeStruct((n_batch, sq, MIX_TILE), BF16)] * 2
    for w in cast:
        rows, cols = w.shape[1] // steps, w.shape[2]
        in_specs.append(pl.BlockSpec((None, rows, cols), lambda b, i: (layer, b * n_q + i, 0)))
        out_specs.append(pl.BlockSpec((rows, cols), lambda b, i: (b * n_q + i, 0)))
        out_shape.append(jax.ShapeDtypeStruct(w.shape[1:], BF16))
        args.append(w)
    return pl.pallas_call(
        functools.partial(_attn_kernel, n_seg=n_seg, ahead=ahead, n_cast=len(cast)),
        grid=(n_batch // bb, n_q),
        in_specs=in_specs,
        out_specs=out_specs,
        out_shape=out_shape,
        compiler_params=_params(2),
        name=f"attn_s{n_seg}",
    )(*args)


def _fnet_kernel(ca_ref, sa_ref, cb_ref, sb_ref, uc_ref, us_ref, o_ref, tc_ref, ts_ref, *, scale):
    @pl.when(pl.program_id(1) == 0)
    def _build_twiddles():
        cb, sb = cb_ref[...], sb_ref[...]
        for a in range(ca_ref.shape[0]):
            ca, sa = ca_ref[a:a + 1, :], sa_ref[a:a + 1, :]
            rows = slice(a * DFT_SPLIT, (a + 1) * DFT_SPLIT)
            tc_ref[rows, :] = (ca * cb - sa * sb).astype(BF16)
            ts_ref[rows, :] = (sa * cb + ca * sb).astype(BF16)

    acc = _dot(tc_ref[...], uc_ref[...]) - _dot(ts_ref[...], us_ref[...])
    o_ref[...] = (acc * scale).astype(o_ref.dtype)


def _fnet(tables, uc, us, tm, tn):
    seq, width = uc.shape
    scale = float((seq * FNET_GROUP_DIM) ** -0.5)
    n_a = tm // DFT_SPLIT
    part_a = pl.BlockSpec((n_a, seq), lambda i, j: (i, 0))
    part_b = pl.BlockSpec((DFT_SPLIT, seq), lambda i, j: (0, 0))
    data = pl.BlockSpec((seq, tn), lambda i, j: (0, j))
    return pl.pallas_call(
        functools.partial(_fnet_kernel, scale=scale),
        grid=(seq // tm, width // tn),
        in_specs=[part_a, part_a, part_b, part_b, data, data],
        out_specs=pl.BlockSpec((tm, tn), lambda i, j: (i, j)),
        out_shape=jax.ShapeDtypeStruct((seq, width), BF16),
        scratch_shapes=[pltpu.VMEM((tm, seq), BF16), pltpu.VMEM((tm, seq), BF16)],
        compiler_params=_params(2),
        name="fnet",
    )(*tables, uc, us)


def _post_body(x_ref, mla_ref, fn_ref, gqa_ref, o_ref, mod_ref, g2_ref, wo_ref, wg_ref, wu_ref, wd_ref, gf_ref, final):
    fn = fn_ref[...]
    if fn.shape[1] > FNET_WIDTH:
        fn = jnp.concatenate([fn[:, j:j + FNET_WIDTH] for j in range(0, fn.shape[1], FNET_WIDTH)], axis=0)
    mix = jnp.concatenate([mla_ref[...], fn, gqa_ref[...]], axis=-1)
    gate1 = mod_ref[:, 2 * D_MODEL:3 * D_MODEL]
    shift2 = mod_ref[:, 3 * D_MODEL:4 * D_MODEL]
    scale2 = mod_ref[:, 4 * D_MODEL:5 * D_MODEL]
    gate2 = mod_ref[:, 5 * D_MODEL:6 * D_MODEL]
    x = x_ref[...] + gate1 * _dot(mix, wo_ref[...])
    h = (_rms(x) * g2_ref[...] * (1.0 + scale2) + shift2).astype(BF16)
    g = _dot(h, wg_ref[...])
    u = _dot(h, wu_ref[...])
    a = (g * jax.nn.sigmoid(g) * u).astype(BF16)
    x = x + gate2 * _dot(a, wd_ref[...])
    if final:
        x = _rms(x) * gf_ref[...]
    o_ref[...] = x


def _post_kernel(*refs, final, n_first, with_next_mod):
    first, second, shared = refs[0:4], refs[4:8], refs[8:15]
    n_in = 18 if with_next_mod else 15
    o_first, o_second = refs[n_in:n_in + 2]
    step = pl.program_id(0)

    def run(group, o_ref):
        if with_next_mod:
            _ada_kernel(*refs[15:18], refs[n_in + 2])
        _post_body(*group, o_ref, *shared, final)

    @pl.when(step < n_first)
    def _first_group():
        run(first, o_first)

    @pl.when(step >= n_first)
    def _second_group():
        run(second, o_second)


def _post(groups, mod, g2, ffn, layer, g_final, *, final, tm, next_mod=None):
    (x1, _, _, _, seq1), (x2, _, _, _, seq2) = groups
    n1, n2 = x1.shape[0] // tm, x2.shape[0] // tm
    tiles_per_mod = n2 // (mod.shape[0] - 1)
    tile1 = lambda i: jnp.minimum(i, n1 - 1)
    tile2 = lambda i: jnp.maximum(i - n1, 0)
    resident = lambda arr: pl.BlockSpec(arr.shape, lambda i: (0, 0), pipeline_mode=pl.Buffered(1))

    def group_specs(seq, tile):
        tok = lambda w: pl.BlockSpec((tm, w), lambda i: (tile(i), 0))
        if tm <= seq:
            tps = seq // tm
            fnet_spec = pl.BlockSpec((tm, FNET_WIDTH), lambda i: (tile(i) % tps, tile(i) // tps))
        else:
            fnet_spec = pl.BlockSpec((seq, (tm // seq) * FNET_WIDTH), lambda i: (0, tile(i)))
        return [tok(D_MODEL), tok(MIX_TILE), fnet_spec, tok(MIX_TILE)], tok(D_MODEL)

    in1, out1 = group_specs(seq1, tile1)
    in2, out2 = group_specs(seq2, tile2)
    mod_spec = pl.BlockSpec((None, 1, 6 * D_MODEL),
                            lambda i: (jnp.where(i < n1, 0, 1 + tile2(i) // tiles_per_mod), 0, 0))
    in_specs = (in1 + in2 + [mod_spec, _layer_spec(g2, layer)] + [resident(w) for w in ffn]
                + [pl.BlockSpec((1, D_MODEL), lambda i: (0, 0))])
    args = [*groups[0][:4], *groups[1][:4], mod, g2, *ffn, g_final]
    out_specs = [out1, out2]
    out_shape = [jax.ShapeDtypeStruct(x1.shape, F32), jax.ShapeDtypeStruct(x2.shape, F32)]
    if next_mod is not None:
        cond_t, w_ada, b_ada = next_mod
        ada_in, ada_out, ada_shape = _ada_specs(cond_t, w_ada, layer + 1, w_ada.shape[2] // (n1 + n2))
        in_specs += ada_in
        args += [cond_t, w_ada, b_ada]
        out_specs.append(ada_out)
        out_shape.append(ada_shape)
    return pl.pallas_call(
        functools.partial(_post_kernel, final=final, n_first=n1, with_next_mod=next_mod is not None),
        grid=(n1 + n2,),
        in_specs=in_specs,
        out_specs=out_specs,
        out_shape=out_shape,
        compiler_params=_params(1),
        name="post_final" if final else "post",
    )(*args)


def _prep_weights(g_norm1, g_norm2, g_q_a, w_q_up, g_kv_a, w_kv_up, g_q_head, g_k_head):
    row = lambda g: g.reshape(DEPTH, 1, -1)
    wq = w_q_up.reshape(DEPTH, MLA_Q_RANK, MLA_HEADS, MLA_QK_DIM)
    wq = jnp.concatenate([wq[..., MLA_NOPE_DIM:], wq[..., :MLA_NOPE_DIM]], axis=-1)
    wq = jnp.pad(wq, ((0, 0), (0, 0), (0, 0), (0, LANES - MLA_QK_DIM)))
    wkv = w_kv_up.reshape(DEPTH, MLA_KV_RANK, MLA_HEADS, MLA_NOPE_DIM + MLA_V_DIM)
    wk = jnp.pad(wkv[..., :MLA_NOPE_DIM], ((0, 0), (0, 0), (0, 0), (MLA_ROPE_DIM, LANES - MLA_QK_DIM)))
    wv = wkv[..., MLA_NOPE_DIM:]
    return {
        "g1": row(g_norm1), "g2": row(g_norm2), "g_q_a": row(g_q_a), "g_kv_a": row(g_kv_a),
        "g_q_head": row(jnp.tile(g_q_head, (1, GQA_HEADS))), "g_k_head": row(jnp.tile(g_k_head, (1, GQA_KV_HEADS))),
        "wq": wq.reshape(DEPTH, MLA_Q_RANK, MLA_PAD_WIDTH).astype(BF16),
        "wk": wk.reshape(DEPTH, MLA_KV_RANK, MLA_PAD_WIDTH).astype(BF16),
        "wv": wv.reshape(DEPTH, MLA_KV_RANK, MIX_TILE).astype(BF16),
    }


def kernel(x_prompt, x_sample, cache_mla_ckv, cache_mla_krope, cache_gqa_k, cache_gqa_v, c, c_ctx, w_ada, b_ada,
           g_norm1, g_norm2, w_in, g_q_a, w_q_up, g_kv_a, w_kv_up, g_q_head, g_k_head, w_out,
           w_ffn_gate, w_ffn_up, w_ffn_down, g_final):
    n_pb, p_seq, _ = x_prompt.shape
    n_sb, s_seq, _ = x_sample.shape
    consts = {"wdft": _channel_dft(), "mean_q": _group_mean_matrix(GQA_HEADS),
              "mean_k": _group_mean_matrix(GQA_KV_HEADS), "place": _rope_place_matrix(),
              "rope": _rope_tables(s_seq), "seq_dft": _seq_dft_matrix(p_seq)}
    dft_s = _seq_dft_tables(s_seq)
    gf = g_final.reshape(1, D_MODEL)
    wts = _prep_weights(g_norm1, g_norm2, g_q_a, w_q_up, g_kv_a, w_kv_up, g_q_head, g_k_head)
    w_in_t = jnp.swapaxes(w_in, 1, 2)

    cond_t = jnp.concatenate([c_ctx[None, :], c], axis=0).T
    b_ada3 = b_ada.reshape(DEPTH, 1, 6 * D_MODEL)
    mod = _ada(cond_t, w_ada, b_ada3, 0)

    xp = x_prompt.reshape(n_pb * p_seq, D_MODEL)
    xs = x_sample.reshape(n_sb * s_seq, D_MODEL)
    b3 = lambda a, nb: a.reshape(nb, a.shape[0] // nb, a.shape[1])
    caches = None
    for l in range(DEPTH):
        final = l == DEPTH - 1
        mod_p = mod[0:1].reshape(1, 1, 6 * D_MODEL)
        mod_s = mod[1:].reshape(n_sb, 1, 6 * D_MODEL)

        km_c, vm_c, gk_c, gv_c, w_a, w_c = _ctx_prep(
            l, cache_mla_ckv, cache_mla_krope, cache_gqa_k, cache_gqa_v, w_in_t, wts, consts)
        win_parts = {"w_a": w_a, "w_c": w_c}
        qm, km, vm, uc_s, us_s, gq, gk, gv, *ffn = _proj(
            xs, mod_s, wts, win_parts, l, consts, seq=s_seq, rope=True,
            cast=(w_out, w_ffn_gate, w_ffn_up, w_ffn_down))
        mla_s, gqa_s = _attention(b3(qm, n_sb), [(km_c, vm_c), (b3(km, n_sb), vm)],
                                  b3(gq, n_sb), [(gk_c, gv_c), (b3(gk, n_sb), gv)], tq=LATENT_Q_TILE,
                                  ahead=QK_AHEAD_LATENT)

        qm, km, vm, fn_p, gq, gk, gv, *caches = _proj(
            xp, mod_p, wts, win_parts, l, consts, seq=p_seq, rope=False, emit_cache=True, prev_cache=caches)
        mla_o, gqa_o = _attention(b3(qm, n_pb), [(b3(km, n_pb), vm)], b3(gq, n_pb), [(b3(gk, n_pb), gv)],
                                  tq=p_seq, bb=CONTEXT_SEQS_PER_STEP, ahead=QK_AHEAD_CONTEXT)
        fn_s = _fnet(dft_s, uc_s, us_s, tm=FNET_ROW_TILE, tn=n_sb * FNET_WIDTH)
        flat = lambda a: a.reshape(-1, MIX_TILE)
        xp, xs, *mod_next = _post(
            [(xp, flat(mla_o), fn_p, flat(gqa_o), p_seq), (xs, flat(mla_s), fn_s, flat(gqa_s), s_seq)],
            mod.reshape(1 + n_sb, 1, 6 * D_MODEL), wts["g2"], ffn, l, gf, final=final, tm=POST_TILE,
            next_mod=None if final else (cond_t, w_ada, b_ada3))
        if mod_next:
            mod = mod_next[0]

    ckv_new, krope_t, gk_t, gv_t = caches
    heads = lambda a: jnp.swapaxes(a, 2, 3).reshape(n_pb, DEPTH, p_seq, GQA_KV_HEADS, GQA_HEAD_DIM)
    return (xp.reshape(n_pb, p_seq, D_MODEL), xs.reshape(n_sb, s_seq, D_MODEL),
            ckv_new, jnp.swapaxes(krope_t, 2, 3), heads(gk_t), heads(gv_t))
```

```python
import functools

import numpy as np
import jax
import jax.numpy as jnp
from jax import lax
from jax.experimental import pallas as pl
from jax.experimental.pallas import tpu as pltpu

D_MODEL = 1024
DEPTH = 2
GRID_W = 64
MLA_HEADS = 6
MLA_Q_RANK = 384
MLA_KV_RANK = 256
MLA_NOPE_DIM = 64
MLA_ROPE_DIM = 32
MLA_V_DIM = 64
MLA_QK_DIM = MLA_NOPE_DIM + MLA_ROPE_DIM
FNET_GROUPS = 4
FNET_GROUP_DIM = 64
FNET_WIDTH = FNET_GROUPS * FNET_GROUP_DIM
GQA_HEADS = 6
GQA_KV_HEADS = 2
GQA_HEAD_DIM = 64
GQA_GROUP = GQA_HEADS // GQA_KV_HEADS
GQA_WIDTH = GQA_HEADS * GQA_HEAD_DIM
GQA_KV_WIDTH = GQA_KV_HEADS * GQA_HEAD_DIM
D_FF = 2816
ROPE_THETA = 10000.0
EPS = 1e-6
LOG2_E = 1.4426950408889634

LANES = 128
HALF = LANES // 2
HEAD_PAIRS = MLA_HEADS // 2
MLA_PAD_WIDTH = MLA_HEADS * LANES
MIX_TILE = HEAD_PAIRS * LANES
MIB = 1024 * 1024
VMEM_LIMIT_MIB = 52
POST_TILE = 512
PROJ_TILE = 512
PROJ_SUB = 256
LATENT_Q_TILE = 512
CONTEXT_SEQS_PER_STEP = 8
QK_AHEAD_LATENT = 2
QK_AHEAD_CONTEXT = 8
FNET_ROW_TILE = 1024
ADA_COL_TILE = 1024


def _params(n_grid_dims):
    return pltpu.CompilerParams(dimension_semantics=("arbitrary",) * n_grid_dims,
                                vmem_limit_bytes=VMEM_LIMIT_MIB * MIB)


_IN_A = MLA_Q_RANK + MLA_KV_RANK
_IN_C0 = _IN_A + MLA_ROPE_DIM
_IN_C = FNET_WIDTH + GQA_WIDTH + 2 * GQA_KV_WIDTH
_C_U, _C_GQ, _C_GK, _C_GV = 0, FNET_WIDTH, FNET_WIDTH + GQA_WIDTH, FNET_WIDTH + GQA_WIDTH + GQA_KV_WIDTH

BF16 = jnp.bfloat16
F32 = jnp.float32


def _rope_tables(n_tokens):
    t = np.arange(n_tokens)
    row = (t // GRID_W).astype(np.float64)
    col = (t % GRID_W).astype(np.float64)

    def angles(rot_dim):
        n_axis = rot_dim // 4
        inv = ROPE_THETA ** (-np.arange(n_axis, dtype=np.float64) / n_axis)
        return np.concatenate([row[:, None] * inv, col[:, None] * inv], axis=-1)

    def tables(ang, lane_to_pair, is_first, is_second):
        cos = np.where((is_first | is_second)[None, :], np.cos(ang)[:, lane_to_pair], 1.0)
        sin = np.sin(ang)[:, lane_to_pair]
        sin_a = np.where(is_second[None, :], sin, 0.0)
        sin_b = np.where(is_first[None, :], -sin, 0.0)
        return [cos, sin_a, sin_b]

    lane = np.arange(LANES)
    half_m = MLA_ROPE_DIM // 2
    in_rope = lane < MLA_ROPE_DIM
    first_m = lane < half_m
    second_m = in_rope & ~first_m
    pair_m = lane % half_m
    half_g = GQA_HEAD_DIM // 2
    first_g = (lane % GQA_HEAD_DIM) < half_g
    pair_g = lane % half_g
    tabs = (tables(angles(MLA_ROPE_DIM), pair_m, first_m, second_m)
            + tables(angles(GQA_HEAD_DIM), pair_g, first_g, ~first_g))
    return jnp.asarray(np.concatenate(tabs, axis=-1), dtype=F32)


def _channel_dft():
    c = np.arange(FNET_GROUP_DIM)
    ang = 2.0 * np.pi * np.outer(c, c) / FNET_GROUP_DIM
    eye = np.eye(FNET_GROUPS)
    table = np.concatenate([np.kron(eye, np.cos(ang)), np.kron(eye, np.sin(ang))], axis=1)
    return jnp.asarray(table, dtype=F32).astype(BF16)


def _group_mean_matrix(n_heads):
    return jnp.asarray(np.kron(np.eye(n_heads), np.full((GQA_HEAD_DIM, GQA_HEAD_DIM), 1.0 / GQA_HEAD_DIM)), dtype=BF16)


def _rope_place_matrix():
    return jnp.asarray(np.eye(MLA_ROPE_DIM, LANES), dtype=BF16)


DFT_SPLIT = 32


def _seq_dft_matrix(seq):
    s = np.arange(seq)
    ang = 2.0 * np.pi * ((s[:, None] * s[None, :]) % seq) / seq
    return jnp.asarray(np.stack([np.cos(ang), np.sin(ang)]), dtype=F32).astype(BF16)


def _seq_dft_tables(seq):
    s = np.arange(seq)
    n_a = seq // DFT_SPLIT
    ang_a = 2.0 * np.pi * ((np.arange(n_a)[:, None] * s[None, :]) % n_a) / n_a
    ang_b = 2.0 * np.pi * ((np.arange(DFT_SPLIT)[:, None] * s[None, :]) % seq) / seq
    return tuple(jnp.asarray(t, dtype=F32) for t in (np.cos(ang_a), np.sin(ang_a), np.cos(ang_b), np.sin(ang_b)))


def _rms(x):
    return x * lax.rsqrt(jnp.mean(x * x, axis=-1, keepdims=True) + EPS)


def _dot(a, b):
    return jnp.dot(a, b, preferred_element_type=F32)


def _group_rms(x, mean_mat):
    ms = _dot((x * x).astype(BF16), mean_mat)
    return x * lax.rsqrt(ms + EPS)


def _rotate(x, cos, sin_a, sin_b, half):
    return x * cos + pltpu.roll(x, half, 1) * sin_a + pltpu.roll(x, LANES - half, 1) * sin_b


def _lower_half(rows):
    return lax.broadcasted_iota(jnp.int32, (rows, LANES), 1) < HALF


def _layer_spec(arr, layer):
    return pl.BlockSpec((None,) + arr.shape[1:], lambda i: (layer, 0, 0))


def _ada_kernel(ct_ref, w_ref, b_ref, o_ref):
    s = ct_ref[...]
    s = s * jax.nn.sigmoid(s)
    w = w_ref[...]
    for m in range(o_ref.shape[0]):
        o_ref[m:m + 1, :] = jnp.sum(w * s[:, m:m + 1], axis=0, keepdims=True) + b_ref[...]


def _ada_specs(cond_t, w_ada, layer, tn):
    n_cond = cond_t.shape[1]
    in_specs = [pl.BlockSpec((D_MODEL, n_cond), lambda j: (0, 0)),
                pl.BlockSpec((None, D_MODEL, tn), lambda j: (layer, 0, j)),
                pl.BlockSpec((None, 1, tn), lambda j: (layer, 0, j))]
    out_shape = jax.ShapeDtypeStruct((n_cond, w_ada.shape[2]), F32)
    return in_specs, pl.BlockSpec((n_cond, tn), lambda j: (0, j)), out_shape


def _ada(cond_t, w_ada, b_ada, layer, tn=ADA_COL_TILE):
    in_specs, out_spec, out_shape = _ada_specs(cond_t, w_ada, layer, tn)
    return pl.pallas_call(
        _ada_kernel,
        grid=(w_ada.shape[2] // tn,),
        in_specs=in_specs,
        out_specs=out_spec,
        out_shape=out_shape,
        compiler_params=_params(1),
        name="ada",
    )(cond_t, w_ada, b_ada)


_PROJ_WEIGHTS = ("g1", "w_a", "w_c", "g_q_a", "wq", "g_kv_a", "wk", "wv", "g_q_head", "g_k_head")


def _proj_kernel(*refs, rope, emit_cache, n_prev, multi_seq):
    (x_ref, mod_ref, g1_ref, wa_ref, wc_ref, gqa_ref, wq_ref, gkva_ref, wk_ref, wv_ref,
     gqh_ref, gkh_ref, wdft_ref, mqk_ref) = refs[:14]
    refs = refs[14:]
    if multi_seq:
        sdft_ref, refs = refs[0], refs[1:]
    if rope:
        tab_ref, refs = refs[0], refs[1:]
    if n_prev:
        prev_refs, refs = refs[:4], refs[4:]
    n_fourier = 1 if multi_seq else 2
    qm_ref, km_ref, vm_ref = refs[:3]
    fourier_refs = refs[3:3 + n_fourier]
    gq_ref, gk_ref, gv_ref = refs[3 + n_fourier:6 + n_fourier]
    cache_refs = refs[6 + n_fourier:]

    shift = mod_ref[:, 0:D_MODEL]
    scale = mod_ref[:, D_MODEL:2 * D_MODEL]
    lower = _lower_half(PROJ_SUB)

    for r in range(x_ref.shape[0] // PROJ_SUB):
        rows = slice(r * PROJ_SUB, (r + 1) * PROJ_SUB)
        seq_cols = (slice(None), slice(r * FNET_WIDTH, (r + 1) * FNET_WIDTH)) if multi_seq else (rows, slice(None))
        t_idx = (r,) if multi_seq else (slice(None), rows)

        hb = (_rms(x_ref[rows, :]) * g1_ref[...] * (1.0 + scale) + shift).astype(BF16)
        pa = _dot(hb, wa_ref[...])
        pc = _dot(hb, wc_ref[...])
        kr = pa[:, _IN_A:]

        if rope:
            cos_m, sa_m, sb_m = (tab_ref[rows, i * LANES:(i + 1) * LANES] for i in range(3))
            cos_g, sa_g, sb_g = (tab_ref[rows, i * LANES:(i + 1) * LANES] for i in range(3, 6))

        qk_normed = _group_rms(pc[:, _C_GQ:_C_GV], mqk_ref[...])
        gqn = qk_normed[:, :GQA_WIDTH] * gqh_ref[...]
        tiles = []
        for p in range(HEAD_PAIRS):
            gp = gqn[:, p * LANES:(p + 1) * LANES]
            if rope:
                gp = _rotate(gp, cos_g, sa_g, sb_g, GQA_HEAD_DIM // 2)
            tiles.append(gp * (LOG2_E * GQA_HEAD_DIM ** -0.5))
        gq_ref[rows, 0:LANES] = jnp.where(lower, tiles[0], tiles[1]).astype(BF16)
        gq_ref[rows, LANES:2 * LANES] = jnp.where(
            lower, pltpu.roll(tiles[0], HALF, 1), pltpu.roll(tiles[2], HALF, 1)).astype(BF16)
        gq_ref[rows, 2 * LANES:] = jnp.where(lower, tiles[1], tiles[2]).astype(BF16)
        gkn = qk_normed[:, GQA_WIDTH:] * gkh_ref[...]
        gkr = _rotate(gkn, cos_g, sa_g, sb_g, GQA_HEAD_DIM // 2) if rope else gkn
        gv_t = pc[:, _C_GV:_C_GV + GQA_KV_WIDTH].T
        gk_ref[rows, :] = gkr.astype(BF16)
        gv_ref[t_idx] = gv_t.astype(BF16)

        cqn = _rms(pa[:, :MLA_Q_RANK]) * gqa_ref[...]
        q = _dot(cqn.astype(BF16), wq_ref[...])
        for hd in range(MLA_HEADS):
            qh = q[:, hd * LANES:(hd + 1) * LANES]
            if rope:
                qh = _rotate(qh, cos_m, sa_m, sb_m, MLA_ROPE_DIM // 2)
            qm_ref[rows, hd * LANES:(hd + 1) * LANES] = (qh * (LOG2_E * MLA_QK_DIM ** -0.5)).astype(BF16)

        ckvn = _rms(pa[:, MLA_Q_RANK:_IN_A]) * gkva_ref[...]
        cb = ckvn.astype(BF16)
        kn = _dot(cb, wk_ref[...])
        krr = _rotate(kr, cos_m, sa_m, sb_m, MLA_ROPE_DIM // 2) if rope else kr
        for hd in range(MLA_HEADS):
            km_ref[rows, hd * LANES:(hd + 1) * LANES] = (kn[:, hd * LANES:(hd + 1) * LANES] + krr).astype(BF16)
        vm_ref[t_idx] = _dot(cb, wv_ref[...]).T.astype(BF16)

        ucs = _dot(pc[:, _C_U:_C_U + FNET_WIDTH].astype(BF16), wdft_ref[...])
        uc = ucs[:, :FNET_WIDTH].astype(BF16)
        us = ucs[:, FNET_WIDTH:].astype(BF16)
        if multi_seq:
            fn = _dot(sdft_ref[0], uc) - _dot(sdft_ref[1], us)
            fourier_refs[0][seq_cols] = (fn * (PROJ_SUB * FNET_GROUP_DIM) ** -0.5).astype(BF16)
        else:
            fourier_refs[0][seq_cols] = uc
            fourier_refs[1][seq_cols] = us

        if emit_cache:
            new = (ckvn, kr.T[:MLA_ROPE_DIM, :], gkn.T, gv_t)
            for i, out_ref in enumerate(cache_refs):
                if n_prev:
                    out_ref[r, :n_prev] = prev_refs[i][r]
                out_ref[r, n_prev] = new[i]


def _proj(x, mod, wts, win_parts, layer, consts, *, seq, rope, prev_cache=None, emit_cache=False, tm=PROJ_TILE):
    n_tok = x.shape[0]
    n_tiles = n_tok // tm
    n_seq = n_tok // seq
    tiles_per_mod = n_tiles // mod.shape[0]
    multi_seq = tm > seq
    if multi_seq:
        assert seq == PROJ_SUB and not rope
        spt = tm // seq
        seq_major = pl.BlockSpec((seq, spt * FNET_WIDTH), lambda i: (0, i))
        transposed = lambda w: pl.BlockSpec((spt, w, seq), lambda i: (i, 0, 0))
    else:
        assert not emit_cache
        tps = seq // tm
        seq_major = pl.BlockSpec((tm, FNET_WIDTH), lambda i: (i % tps, i // tps))
        transposed = lambda w: pl.BlockSpec((None, w, tm), lambda i: (i // tps, 0, i % tps))
    const = lambda arr: pl.BlockSpec(arr.shape, lambda i: (0,) * arr.ndim)
    tok = lambda w: pl.BlockSpec((tm, w), lambda i: (i, 0))
    in_specs = [tok(D_MODEL), pl.BlockSpec((None, 1, 6 * D_MODEL), lambda i: (i // tiles_per_mod, 0, 0))]
    args = [x, mod]
    for name in _PROJ_WEIGHTS:
        arr = win_parts.get(name)
        in_specs.append(const(arr) if arr is not None else _layer_spec(wts[name], layer))
        args.append(arr if arr is not None else wts[name])
    for name in ("wdft", "mean_qk"):
        in_specs.append(const(consts[name]))
        args.append(consts[name])
    if multi_seq:
        in_specs.append(const(consts["seq_dft"]))
        args.append(consts["seq_dft"])
    if rope:
        in_specs.append(pl.BlockSpec((tm, 6 * LANES), lambda i: (i % tps, 0)))
        args.append(consts["rope"])
    n_fourier = 1 if multi_seq else 2
    out_specs = [tok(MLA_PAD_WIDTH), tok(MLA_PAD_WIDTH), transposed(MIX_TILE)] + [seq_major] * n_fourier + [
                 tok(GQA_WIDTH), tok(GQA_KV_WIDTH), transposed(GQA_KV_WIDTH)]
    out_shape = [jax.ShapeDtypeStruct((n_tok, MLA_PAD_WIDTH), BF16),
                 jax.ShapeDtypeStruct((n_tok, MLA_PAD_WIDTH), BF16),
                 jax.ShapeDtypeStruct((n_seq, MIX_TILE, seq), BF16)] + [
                 jax.ShapeDtypeStruct((seq, n_seq * FNET_WIDTH), BF16)] * n_fourier + [
                 jax.ShapeDtypeStruct((n_tok, GQA_WIDTH), BF16),
                 jax.ShapeDtypeStruct((n_tok, GQA_KV_WIDTH), BF16),
                 jax.ShapeDtypeStruct((n_seq, GQA_KV_WIDTH, seq), BF16)]
    n_prev = 0
    if emit_cache:
        n_prev = prev_cache[0].shape[1] if prev_cache is not None else 0
        layers = lambda n, tail: pl.BlockSpec((spt, n) + tail, lambda i: (i, 0, 0, 0))
        tails = ((seq, MLA_KV_RANK), (MLA_ROPE_DIM, seq), (GQA_KV_WIDTH, seq), (GQA_KV_WIDTH, seq))
        if n_prev:
            in_specs += [layers(n_prev, t) for t in tails]
            args += list(prev_cache)
        out_specs += [layers(n_prev + 1, t) for t in tails]
        out_shape += [jax.ShapeDtypeStruct((n_seq, n_prev + 1) + t, F32) for t in tails]
    return pl.pallas_call(
        functools.partial(_proj_kernel, rope=rope, emit_cache=emit_cache, n_prev=n_prev, multi_seq=multi_seq),
        grid=(n_tiles,),
        in_specs=in_specs,
        out_specs=out_specs,
        out_shape=out_shape,
        compiler_params=_params(1),
        name="proj_rope" if rope else "proj_ctx",
    )(*args)


def _ctx_kernel(ckv_ref, kr_ref, gk_ref, gv_ref, wk_ref, wv_ref, place_ref, win_ref,
                km_ref, vm_ref, gko_ref, gvo_ref, wa_ref, wc_ref):
    wt = win_ref[...]
    kr_rows = jnp.concatenate([wt[_IN_A:_IN_C0, :], jnp.zeros((LANES - MLA_ROPE_DIM, wt.shape[1]), F32)], axis=0)
    wa_ref[:, :_IN_A] = wt[:_IN_A, :].T.astype(BF16)
    wa_ref[:, _IN_A:] = kr_rows.T.astype(BF16)
    wc_ref[...] = wt[_IN_C0:, :].T.astype(BF16)

    cb = ckv_ref[...].astype(BF16)
    kn = _dot(cb, wk_ref[...])
    kr = _dot(kr_ref[...].astype(BF16), place_ref[...])
    for hd in range(MLA_HEADS):
        km_ref[:, hd * LANES:(hd + 1) * LANES] = (kn[:, hd * LANES:(hd + 1) * LANES] + kr).astype(BF16)
    vm_ref[...] = _dot(cb, wv_ref[...]).T.astype(BF16)
    gko_ref[...] = gk_ref[...].astype(BF16)
    gvo_ref[...] = gv_ref[...].T.astype(BF16)


def _ctx_prep(layer, cache_ckv, cache_krope, cache_gk, cache_gv, w_in_t, wts, consts):
    nb, _, past, _ = cache_ckv.shape
    rows = D_MODEL // nb
    cache = lambda w: pl.BlockSpec((None, None, past, w), lambda b: (b, layer, 0, 0))
    out = lambda w: pl.BlockSpec((None, past, w), lambda b: (b, 0, 0))
    out_t = lambda w: pl.BlockSpec((None, w, past), lambda b: (b, 0, 0))
    part = lambda w: pl.BlockSpec((rows, w), lambda b: (b, 0))
    part_widths = (_IN_A + LANES, _IN_C)
    return pl.pallas_call(
        _ctx_kernel,
        grid=(nb,),
        in_specs=[cache(MLA_KV_RANK), cache(MLA_ROPE_DIM), cache(GQA_KV_WIDTH), cache(GQA_KV_WIDTH),
                  _layer_spec(wts["wk"], layer), _layer_spec(wts["wv"], layer),
                  pl.BlockSpec((MLA_ROPE_DIM, LANES), lambda b: (0, 0)),
                  pl.BlockSpec((None, w_in_t.shape[1], rows), lambda b: (layer, 0, b))],
        out_specs=[out(MLA_PAD_WIDTH), out_t(MIX_TILE), out(GQA_KV_WIDTH), out_t(GQA_KV_WIDTH)]
                  + [part(w) for w in part_widths],
        out_shape=[jax.ShapeDtypeStruct((nb, past, MLA_PAD_WIDTH), BF16),
                   jax.ShapeDtypeStruct((nb, MIX_TILE, past), BF16),
                   jax.ShapeDtypeStruct((nb, past, GQA_KV_WIDTH), BF16),
                   jax.ShapeDtypeStruct((nb, GQA_KV_WIDTH, past), BF16)]
                  + [jax.ShapeDtypeStruct((D_MODEL, w), BF16) for w in part_widths],
        compiler_params=_params(1),
        name="ctx_prep",
    )(cache_ckv, cache_krope, cache_gk.reshape(nb, DEPTH, past, GQA_KV_WIDTH),
      cache_gv.reshape(nb, DEPTH, past, GQA_KV_WIDTH), wts["wk"], wts["wv"], consts["place"], w_in_t)


def _attn_kernel(*refs, n_seg, ahead, n_cast):
    n_in = 1 + 2 * n_seg
    outs_at = 2 * n_in + n_cast
    families = []
    for f, mla in enumerate((True, False)):
        ins = refs[f * n_in:(f + 1) * n_in]
        families.append((mla, ins[0], ins[1:1 + n_seg], ins[1 + n_seg:], refs[outs_at + f]))
    for c in range(n_cast):
        refs[outs_at + 2 + c][...] = refs[2 * n_in + c][...].astype(BF16)
    n_seqs, tq = refs[0].shape[:2]
    lower = _lower_half(tq)
    top = lax.broadcasted_iota(jnp.int32, (LANES, tq), 0) < HALF

    def scores_t(f, b, p, hd):
        mla, q_ref, k_refs, _, _ = families[f]
        if mla:
            cols = slice((2 * p + hd) * LANES, (2 * p + hd + 1) * LANES)
            qh = q_ref[b, :, cols]
            keys = [k_ref[b, :, cols] for k_ref in k_refs]
        else:
            q = q_ref[b, :, p * LANES:(p + 1) * LANES]
            qh = jnp.where(lower if hd == 0 else ~lower, q, jnp.zeros_like(q))
            keys = [k_ref[b] for k_ref in k_refs]
        return [lax.dot_general(k, qh, (((1,), (1,)), ((), ())), preferred_element_type=F32) for k in keys]

    def softmax_t(ss):
        m = functools.reduce(jnp.maximum, [jnp.max(s, axis=0, keepdims=True) for s in ss])
        es = [jnp.exp2(s - m) for s in ss]
        denom = functools.reduce(jnp.add, [jnp.sum(e, axis=0, keepdims=True) for e in es])
        return [e.astype(BF16) for e in es], denom

    def weighted_values_t(f, b, p, es, denom):
        mla, _, _, vt_refs, _ = families[f]
        rows = slice(p * LANES, (p + 1) * LANES) if mla else slice(None)
        acc = functools.reduce(jnp.add, [_dot(vt_ref[b, rows, :], e) for e, vt_ref in zip(es, vt_refs)])
        return acc * (1.0 / denom)

    work = [(f, b, p, hd) for f in range(2) for b in range(n_seqs) for p in range(HEAD_PAIRS) for hd in range(2)]
    pending = [scores_t(*w) for w in work[:ahead]]
    outs = {}
    for i, (f, b, p, hd) in enumerate(work):
        es, denom = softmax_t(pending.pop(0))
        if i + ahead < len(work):
            pending.append(scores_t(*work[i + ahead]))
        outs[f, b, p, hd] = weighted_values_t(f, b, p, es, denom)
        if (p, hd) != (HEAD_PAIRS - 1, 1):
            continue
        mla, o_ref = families[f][0], families[f][4]
        tiles = [jnp.where(top, outs[f, b, t, 0], outs[f, b, t, 1]).T for t in range(HEAD_PAIRS)]
        if not mla:
            t1 = pltpu.roll(tiles[1], HALF, 1)
            tiles = [jnp.where(lower, tiles[0], t1), jnp.where(lower, tiles[2], tiles[0]),
                     jnp.where(lower, t1, tiles[2])]
        for t in range(HEAD_PAIRS):
            o_ref[b, :, t * LANES:(t + 1) * LANES] = tiles[t].astype(o_ref.dtype)


def _attention(mla_q, mla_kv, gqa_q, gqa_kv, *, tq, ahead, bb=1, cast=(), layer=0):
    n_batch, sq = mla_q.shape[:2]
    n_seg = len(mla_kv)
    n_q = sq // tq
    steps = (n_batch // bb) * n_q
    whole = lambda a: pl.BlockSpec((bb,) + a.shape[1:], lambda b, i: (b, 0, 0))
    in_specs, args = [], []
    for q, kv in ((mla_q, mla_kv), (gqa_q, gqa_kv)):
        in_specs += [pl.BlockSpec((bb, tq, q.shape[2]), lambda b, i: (b, i, 0))]
        in_specs += [whole(k) for k, _ in kv] + [whole(v) for _, v in kv]
        args += [q] + [k for k, _ in kv] + [v for _, v in kv]
    out_spec = pl.BlockSpec((bb, tq, MIX_TILE), lambda b, i: (b, i, 0))
    out_specs = [out_spec, out_spec]
    out_shape = [jax.ShapeDtypeStruct((n_batch, sq, MIX_TILE), BF16)] * 2
    for w in cast:
        rows, cols = w.shape[1] // steps, w.shape[2]
        in_specs.append(pl.BlockSpec((None, rows, cols), lambda b, i: (layer, b * n_q + i, 0)))
        out_specs.append(pl.BlockSpec((rows, cols), lambda b, i: (b * n_q + i, 0)))
        out_shape.append(jax.ShapeDtypeStruct(w.shape[1:], BF16))
        args.append(w)
    return pl.pallas_call(
        functools.partial(_attn_kernel, n_seg=n_seg, ahead=ahead, n_cast=len(cast)),
        grid=(n_batch // bb, n_q),
        in_specs=in_specs,
        out_specs=out_specs,
        out_shape=out_shape,
        compiler_params=_params(2),
        name=f"attn_s{n_seg}",
    )(*args)


def _fnet_kernel(ca_ref, sa_ref, cb_ref, sb_ref, uc_ref, us_ref, o_ref, tc_ref, ts_ref, *, scale):
    @pl.when(pl.program_id(1) == 0)
    def _build_twiddles():
        cb, sb = cb_ref[...], sb_ref[...]
        for a in range(ca_ref.shape[0]):
            ca, sa = ca_ref[a:a + 1, :], sa_ref[a:a + 1, :]
            rows = slice(a * DFT_SPLIT, (a + 1) * DFT_SPLIT)
            tc_ref[rows, :] = (ca * cb - sa * sb).astype(BF16)
            ts_ref[rows, :] = (sa * cb + ca * sb).astype(BF16)

    acc = _dot(tc_ref[...], uc_ref[...]) - _dot(ts_ref[...], us_ref[...])
    o_ref[...] = (acc * scale).astype(o_ref.dtype)


def _fnet(tables, uc, us, tm, tn):
    seq, width = uc.shape
    scale = float((seq * FNET_GROUP_DIM) ** -0.5)
    n_a = tm // DFT_SPLIT
    part_a = pl.BlockSpec((n_a, seq), lambda i, j: (i, 0))
    part_b = pl.BlockSpec((DFT_SPLIT, seq), lambda i, j: (0, 0))
    data = pl.BlockSpec((seq, tn), lambda i, j: (0, j))
    return pl.pallas_call(
        functools.partial(_fnet_kernel, scale=scale),
        grid=(seq // tm, width // tn),
        in_specs=[part_a, part_a, part_b, part_b, data, data],
        out_specs=pl.BlockSpec((tm, tn), lambda i, j: (i, j)),
        out_shape=jax.ShapeDtypeStruct((seq, width), BF16),
        scratch_shapes=[pltpu.VMEM((tm, seq), BF16), pltpu.VMEM((tm, seq), BF16)],
        compiler_params=_params(2),
        name="fnet",
    )(*tables, uc, us)


def _post_body(x_ref, mla_ref, fn_ref, gqa_ref, o_ref, mod_ref, g2_ref, wo_ref, wg_ref, wu_ref, wd_ref, gf_ref, final):
    fn = fn_ref[...]
    if fn.shape[1] > FNET_WIDTH:
        fn = jnp.concatenate([fn[:, j:j + FNET_WIDTH] for j in range(0, fn.shape[1], FNET_WIDTH)], axis=0)
    mix = jnp.concatenate([mla_ref[...], fn, gqa_ref[...]], axis=-1)
    gate1 = mod_ref[:, 2 * D_MODEL:3 * D_MODEL]
    shift2 = mod_ref[:, 3 * D_MODEL:4 * D_MODEL]
    scale2 = mod_ref[:, 4 * D_MODEL:5 * D_MODEL]
    gate2 = mod_ref[:, 5 * D_MODEL:6 * D_MODEL]
    x = x_ref[...] + gate1 * _dot(mix, wo_ref[...])
    h = (_rms(x) * g2_ref[...] * (1.0 + scale2) + shift2).astype(BF16)
    g = _dot(h, wg_ref[...])
    u = _dot(h, wu_ref[...])
    a = (g * jax.nn.sigmoid(g) * u).astype(BF16)
    x = x + gate2 * _dot(a, wd_ref[...])
    if final:
        x = _rms(x) * gf_ref[...]
    o_ref[...] = x


def _post_kernel(*refs, final, n_first, with_next_mod):
    first, second, shared = refs[0:4], refs[4:8], refs[8:15]
    n_in = 18 if with_next_mod else 15
    o_first, o_second = refs[n_in:n_in + 2]
    step = pl.program_id(0)

    def run(group, o_ref):
        if with_next_mod:
            _ada_kernel(*refs[15:18], refs[n_in + 2])
        _post_body(*group, o_ref, *shared, final)

    @pl.when(step < n_first)
    def _first_group():
        run(first, o_first)

    @pl.when(step >= n_first)
    def _second_group():
        run(second, o_second)


def _post(groups, mod, g2, ffn, layer, g_final, *, final, tm, next_mod=None):
    (x1, _, _, _, seq1), (x2, _, _, _, seq2) = groups
    n1, n2 = x1.shape[0] // tm, x2.shape[0] // tm
    tiles_per_mod = n2 // (mod.shape[0] - 1)
    tile1 = lambda i: jnp.minimum(i, n1 - 1)
    tile2 = lambda i: jnp.maximum(i - n1, 0)
    resident = lambda arr: pl.BlockSpec(arr.shape, lambda i: (0, 0), pipeline_mode=pl.Buffered(1))

    def group_specs(seq, tile):
        tok = lambda w: pl.BlockSpec((tm, w), lambda i: (tile(i), 0))
        if tm <= seq:
            tps = seq // tm
            fnet_spec = pl.BlockSpec((tm, FNET_WIDTH), lambda i: (tile(i) % tps, tile(i) // tps))
        else:
            fnet_spec = pl.BlockSpec((seq, (tm // seq) * FNET_WIDTH), lambda i: (0, tile(i)))
        return [tok(D_MODEL), tok(MIX_TILE), fnet_spec, tok(MIX_TILE)], tok(D_MODEL)

    in1, out1 = group_specs(seq1, tile1)
    in2, out2 = group_specs(seq2, tile2)
    mod_spec = pl.BlockSpec((None, 1, 6 * D_MODEL),
                            lambda i: (jnp.where(i < n1, 0, 1 + tile2(i) // tiles_per_mod), 0, 0))
    in_specs = (in1 + in2 + [mod_spec, _layer_spec(g2, layer)] + [resident(w) for w in ffn]
                + [pl.BlockSpec((1, D_MODEL), lambda i: (0, 0))])
    args = [*groups[0][:4], *groups[1][:4], mod, g2, *ffn, g_final]
    out_specs = [out1, out2]
    out_shape = [jax.ShapeDtypeStruct(x1.shape, F32), jax.ShapeDtypeStruct(x2.shape, F32)]
    if next_mod is not None:
        cond_t, w_ada, b_ada = next_mod
        ada_in, ada_out, ada_shape = _ada_specs(cond_t, w_ada, layer + 1, w_ada.shape[2] // (n1 + n2))
        in_specs += ada_in
        args += [cond_t, w_ada, b_ada]
        out_specs.append(ada_out)
        out_shape.append(ada_shape)
    return pl.pallas_call(
        functools.partial(_post_kernel, final=final, n_first=n1, with_next_mod=next_mod is not None),
        grid=(n1 + n2,),
        in_specs=in_specs,
        out_specs=out_specs,
        out_shape=out_shape,
        compiler_params=_params(1),
        name="post_final" if final else "post",
    )(*args)


def _prep_weights(g_norm1, g_norm2, g_q_a, w_q_up, g_kv_a, w_kv_up, g_q_head, g_k_head):
    row = lambda g: g.reshape(DEPTH, 1, -1)
    wq = w_q_up.reshape(DEPTH, MLA_Q_RANK, MLA_HEADS, MLA_QK_DIM)
    wq = jnp.concatenate([wq[..., MLA_NOPE_DIM:], wq[..., :MLA_NOPE_DIM]], axis=-1)
    wq = jnp.pad(wq, ((0, 0), (0, 0), (0, 0), (0, LANES - MLA_QK_DIM)))
    wkv = w_kv_up.reshape(DEPTH, MLA_KV_RANK, MLA_HEADS, MLA_NOPE_DIM + MLA_V_DIM)
    wk = jnp.pad(wkv[..., :MLA_NOPE_DIM], ((0, 0), (0, 0), (0, 0), (MLA_ROPE_DIM, LANES - MLA_QK_DIM)))
    wv = wkv[..., MLA_NOPE_DIM:]
    return {
        "g1": row(g_norm1), "g2": row(g_norm2), "g_q_a": row(g_q_a), "g_kv_a": row(g_kv_a),
        "g_q_head": row(jnp.tile(g_q_head, (1, GQA_HEADS))), "g_k_head": row(jnp.tile(g_k_head, (1, GQA_KV_HEADS))),
        "wq": wq.reshape(DEPTH, MLA_Q_RANK, MLA_PAD_WIDTH).astype(BF16),
        "wk": wk.reshape(DEPTH, MLA_KV_RANK, MLA_PAD_WIDTH).astype(BF16),
        "wv": wv.reshape(DEPTH, MLA_KV_RANK, MIX_TILE).astype(BF16),
    }


def kernel(x_prompt, x_sample, cache_mla_ckv, cache_mla_krope, cache_gqa_k, cache_gqa_v, c, c_ctx, w_ada, b_ada,
           g_norm1, g_norm2, w_in, g_q_a, w_q_up, g_kv_a, w_kv_up, g_q_head, g_k_head, w_out,
           w_ffn_gate, w_ffn_up, w_ffn_down, g_final):
    n_pb, p_seq, _ = x_prompt.shape
    n_sb, s_seq, _ = x_sample.shape
    consts = {"wdft": _channel_dft(), "mean_qk": _group_mean_matrix(GQA_HEADS + GQA_KV_HEADS),
              "place": _rope_place_matrix(),
              "rope": _rope_tables(s_seq), "seq_dft": _seq_dft_matrix(p_seq)}
    dft_s = _seq_dft_tables(s_seq)
    gf = g_final.reshape(1, D_MODEL)
    wts = _prep_weights(g_norm1, g_norm2, g_q_a, w_q_up, g_kv_a, w_kv_up, g_q_head, g_k_head)
    w_in_t = jnp.swapaxes(w_in, 1, 2)

    cond_t = jnp.concatenate([c_ctx[None, :], c], axis=0).T
    b_ada3 = b_ada.reshape(DEPTH, 1, 6 * D_MODEL)
    mod = _ada(cond_t, w_ada, b_ada3, 0)

    xp = x_prompt.reshape(n_pb * p_seq, D_MODEL)
    xs = x_sample.reshape(n_sb * s_seq, D_MODEL)
    b3 = lambda a, nb: a.reshape(nb, a.shape[0] // nb, a.shape[1])
    caches = None
    for l in range(DEPTH):
        final = l == DEPTH - 1
        mod_p = mod[0:1].reshape(1, 1, 6 * D_MODEL)
        mod_s = mod[1:].reshape(n_sb, 1, 6 * D_MODEL)

        km_c, vm_c, gk_c, gv_c, w_a, w_c = _ctx_prep(
            l, cache_mla_ckv, cache_mla_krope, cache_gqa_k, cache_gqa_v, w_in_t, wts, consts)
        win_parts = {"w_a": w_a, "w_c": w_c}
        qm, km, vm, uc_s, us_s, gq, gk, gv = _proj(xs, mod_s, wts, win_parts, l, consts, seq=s_seq, rope=True)
        mla_s, gqa_s, *ffn = _attention(b3(qm, n_sb), [(km_c, vm_c), (b3(km, n_sb), vm)],
                                        b3(gq, n_sb), [(gk_c, gv_c), (b3(gk, n_sb), gv)], tq=LATENT_Q_TILE,
                                        ahead=QK_AHEAD_LATENT,
                                        cast=(w_out, w_ffn_gate, w_ffn_up, w_ffn_down), layer=l)

        qm, km, vm, fn_p, gq, gk, gv, *caches = _proj(
            xp, mod_p, wts, win_parts, l, consts, seq=p_seq, rope=False, emit_cache=True, prev_cache=caches)
        mla_o, gqa_o = _attention(b3(qm, n_pb), [(b3(km, n_pb), vm)], b3(gq, n_pb), [(b3(gk, n_pb), gv)],
                                  tq=p_seq, bb=CONTEXT_SEQS_PER_STEP, ahead=QK_AHEAD_CONTEXT)
        fn_s = _fnet(dft_s, uc_s, us_s, tm=FNET_ROW_TILE, tn=n_sb * FNET_WIDTH)
        flat = lambda a: a.reshape(-1, MIX_TILE)
        xp, xs, *mod_next = _post(
            [(xp, flat(mla_o), fn_p, flat(gqa_o), p_seq), (xs, flat(mla_s), fn_s, flat(gqa_s), s_seq)],
            mod.reshape(1 + n_sb, 1, 6 * D_MODEL), wts["g2"], ffn, l, gf, final=final, tm=POST_TILE,
            next_mod=None if final else (cond_t, w_ada, b_ada3))
        if mod_next:
            mod = mod_next[0]

    ckv_new, krope_t, gk_t, gv_t = caches
    heads = lambda a: jnp.swapaxes(a, 2, 3).reshape(n_pb, DEPTH, p_seq, GQA_KV_HEADS, GQA_HEAD_DIM)
    return (xp.reshape(n_pb, p_seq, D_MODEL), xs.reshape(n_sb, s_seq, D_MODEL),
            ckv_new, jnp.swapaxes(krope_t, 2, 3), heads(gk_t), heads(gv_t))
```
